```python
import math
import jax, jax.numpy as jnp
from jax import lax
import numpy as np

D_MODEL = 1024
BATCH = 4
SEQ = 4096
DEPTH = 4

N_MIXERS = 3
N_HEADS = 16
HEAD_DIM = 64
ATTN_SCALE = HEAD_DIM ** -0.5
BLOCK = 128
NUM_BUCKETS = 32
MAX_DISTANCE = 2048
RMS_EPS = 1e-6
NEG_INF = -1e30
FORCE_SCORE = 1e9
DIL_CONFIGS = ((128, 1), (512, 4), (2048, 16))
N_DIL = len(DIL_CONFIGS)
A_IN = N_DIL * 3 * N_HEADS * HEAD_DIM
B_KV_HEADS = 2
B_WINDOW = 128
B_IN = (N_HEADS + 2 * B_KV_HEADS) * HEAD_DIM
C_KV_HEADS = 2
CMP_BLOCK = 32
CMP_STRIDE = 16
SEL_BLOCK = 64
N_SELECT = 16
C_WINDOW = 512
CMP_HIDDEN = 256
C_IN = N_HEADS * HEAD_DIM + 6 * C_KV_HEADS * HEAD_DIM + 3 * N_HEADS
D_FF = 2816
CONV_WIDTH = 3
N_A = (DEPTH + 2) // 3
N_B = (DEPTH + 1) // 3
N_C = DEPTH // 3

kernel_name = 'hybrid_dilated_sink_nsa_trunk'


def rms_norm(x, g):
    xf = x.astype(jnp.float32)
    y = xf * lax.rsqrt(jnp.mean(xf * xf, axis=-1, keepdims=True) + RMS_EPS)
    return (y * g.astype(jnp.float32)).astype(x.dtype)


def t5_bucket(dist):
    max_exact = NUM_BUCKETS // 2
    d = jnp.maximum(dist, 0)
    df = jnp.maximum(d, 1).astype(jnp.float32)
    large = max_exact + (jnp.log(df / max_exact) / math.log(MAX_DISTANCE / max_exact)
                         * (NUM_BUCKETS - max_exact)).astype(jnp.int32)
    large = jnp.minimum(large, NUM_BUCKETS - 1)
    return jnp.where(d < max_exact, d, large)


def banded_attention(q, k, v, max_dist, stride, table, sinks=None, with_lse=False):
    N, L, H, D = q.shape
    G = k.shape[2]
    R = H // G
    nb = -(-L // BLOCK)
    Lp = nb * BLOCK
    pad = Lp - L
    n_prev = -(-max_dist // BLOCK)
    KB = (n_prev + 1) * BLOCK
    qb = jnp.pad(q, ((0, 0), (0, pad), (0, 0), (0, 0))).reshape(N, nb, BLOCK, G, R, D)

    def band(t):
        tp = jnp.pad(t, ((0, 0), (n_prev * BLOCK, pad), (0, 0), (0, 0)))
        return jnp.concatenate([tp[:, j * BLOCK:j * BLOCK + Lp].reshape(N, nb, BLOCK, G, D)
                                for j in range(n_prev + 1)], axis=2)

    kb, vb = band(k), band(v)
    dist = jnp.arange(BLOCK)[:, None] + n_prev * BLOCK - jnp.arange(KB)[None, :]
    key_pos = (jnp.arange(nb)[:, None, None] - n_prev) * BLOCK + jnp.arange(KB)[None, None, :]
    valid = (dist >= 0) & (dist <= max_dist) & (key_pos >= 0)
    bias = table[t5_bucket(dist * stride)].astype(jnp.float32).transpose(2, 0, 1).reshape(G, R, BLOCK, KB)
    logits = jnp.einsum('nbqgrd,nbkgd->nbgrqk', qb, kb).astype(jnp.float32) * ATTN_SCALE + bias
    logits = jnp.where(valid[None, :, None, None], logits, NEG_INF)
    m = jnp.max(logits, axis=-1, keepdims=True)
    if sinks is not None:
        s = sinks.astype(jnp.float32).reshape(1, 1, G, R, 1, 1)
        m = jnp.maximum(m, s)
    p = jnp.exp(logits - m)
    den = jnp.sum(p, axis=-1, keepdims=True)
    norm = den if sinks is None else den + jnp.exp(s - m)
    out = jnp.einsum('nbgrqk,nbkgd->nbqgrd', (p / norm).astype(v.dtype), vb).reshape(N, Lp, H, D)[:, :L]
    if not with_lse:
        return out
    lse = (m + jnp.log(den))[..., 0].transpose(0, 1, 4, 2, 3).reshape(N, Lp, H)[:, :L]
    return out, lse


def dilated_attention(h, w_in, w_o, table):
    B, S, _ = h.shape
    qkv = (h @ w_in).reshape(B, S, N_DIL, 3, N_HEADS, HEAD_DIM)
    outs, lses = [], []
    for g, (window, dil) in enumerate(DIL_CONFIGS):
        L = S // dil

        def to_strided(t):
            return t.reshape(B, L, dil, N_HEADS, HEAD_DIM).transpose(0, 2, 1, 3, 4).reshape(B * dil, L, N_HEADS, HEAD_DIM)

        q, k, v = (to_strided(qkv[:, :, g, i]) for i in range(3))
        o, lse = banded_attention(q, k, v, window // dil, dil, table, with_lse=True)
        outs.append(o.reshape(B, dil, L, N_HEADS, HEAD_DIM).transpose(0, 2, 1, 3, 4).reshape(B, S, N_HEADS, HEAD_DIM))
        lses.append(lse.reshape(B, dil, L, N_HEADS).transpose(0, 2, 1, 3).reshape(B, S, N_HEADS))
    wts = jax.nn.softmax(jnp.stack(lses, axis=0), axis=0)
    o = outs[0] * wts[0, ..., None].astype(outs[0].dtype)
    for g in range(1, N_DIL):
        o = o + outs[g] * wts[g, ..., None].astype(outs[g].dtype)
    return o.reshape(B, S, N_HEADS * HEAD_DIM) @ w_o


def sink_window_attention(h, w_in, sinks, w_o, table):
    B, S, _ = h.shape
    HQ, HK = N_HEADS * HEAD_DIM, B_KV_HEADS * HEAD_DIM
    qkv = h @ w_in
    q = qkv[..., :HQ].reshape(B, S, N_HEADS, HEAD_DIM)
    k = qkv[..., HQ:HQ + HK].reshape(B, S, B_KV_HEADS, HEAD_DIM)
    v = qkv[..., HQ + HK:].reshape(B, S, B_KV_HEADS, HEAD_DIM)
    o = banded_attention(q, k, v, B_WINDOW - 1, 1, table, sinks=sinks)
    return o.reshape(B, S, HQ) @ w_o


def nsa_attention(h, w_in, cmp_pos, cmp_w1, cmp_w2, w_o, table):
    B, S, _ = h.shape
    G, R = C_KV_HEADS, N_HEADS // C_KV_HEADS
    HQ, HK = N_HEADS * HEAD_DIM, C_KV_HEADS * HEAD_DIM
    proj = h @ w_in
    q = proj[..., :HQ].reshape(B, S, G, R, HEAD_DIM)
    kv = proj[..., HQ:HQ + 6 * HK].reshape(B, S, 6, G, HEAD_DIM)
    gates = jax.nn.sigmoid(proj[..., HQ + 6 * HK:].reshape(B, S, 3, N_HEADS, 1))
    k_c, v_c, k_s, v_s, k_w, v_w = (kv[:, :, i] for i in range(6))
    pos = jnp.arange(S)

    nc = (S - CMP_BLOCK) // CMP_STRIDE + 1
    starts = jnp.arange(nc) * CMP_STRIDE
    idx = starts[:, None] + jnp.arange(CMP_BLOCK)[None, :]

    def compress(t, i):
        blocks = t[:, idx] + cmp_pos[i][None, None, :, None, :]
        flat = blocks.transpose(0, 1, 3, 2, 4).reshape(B, nc, G, CMP_BLOCK * HEAD_DIM)
        return jax.nn.gelu(flat @ cmp_w1[i], approximate=True) @ cmp_w2[i]

    kc, vc = compress(k_c, 0), compress(v_c, 1)
    logits_c = jnp.einsum('bsgrd,bcgd->bgrsc', q, kc).astype(jnp.float32) * ATTN_SCALE
    valid_c = (starts + CMP_BLOCK - 1)[None, :] <= pos[:, None]
    p_c = jax.nn.softmax(jnp.where(valid_c, logits_c, NEG_INF), axis=-1) \
        * jnp.any(valid_c, axis=-1)[:, None].astype(jnp.float32)
    o_c = jnp.einsum('bgrsc,bcgd->bsgrd', p_c.astype(vc.dtype), vc).reshape(B, S, N_HEADS, HEAD_DIM)

    ns = S // SEL_BLOCK
    k_sel = min(N_SELECT, ns)
    blk = jnp.arange(ns)
    overlap = ((starts[:, None] < (blk[None, :] + 1) * SEL_BLOCK)
               & (starts[:, None] + CMP_BLOCK > blk[None, :] * SEL_BLOCK)).astype(jnp.float32)
    imp = jnp.einsum('bgrsc,cj->bsgj', p_c, overlap)
    cur = (pos // SEL_BLOCK)[:, None]
    forced = ((blk[None, :] == 0) | (blk[None, :] == cur) | (blk[None, :] == cur - 1))[:, None, :]
    allowed = (blk[None, :] <= cur)[:, None, :]
    score = jnp.where(forced, FORCE_SCORE, jnp.where(allowed, imp, NEG_INF))
    sel = lax.top_k(score, k_sel)[1]

    nqb = S // BLOCK
    q_blocks = q.reshape(B, nqb, BLOCK, G, R, HEAD_DIM).transpose(1, 0, 2, 3, 4, 5)
    sel_blocks = sel.reshape(B, nqb, BLOCK, G, k_sel).transpose(1, 0, 2, 3, 4)
    pos_blocks = pos.reshape(nqb, BLOCK)
    b_ix = jnp.arange(B)[:, None, None, None]
    g_ix = jnp.arange(G)[None, None, :, None]
    table_g = table.reshape(NUM_BUCKETS, G, R)

    def selected_block(args):
        qb, sb, pb = args
        kpos = (sb[..., None] * SEL_BLOCK + jnp.arange(SEL_BLOCK)).reshape(B, BLOCK, G, k_sel * SEL_BLOCK)
        kg = k_s[b_ix, kpos, g_ix]
        vg = v_s[b_ix, kpos, g_ix]
        dist = pb[None, :, None, None] - kpos
        bias = table_g[t5_bucket(dist), g_ix].astype(jnp.float32).transpose(0, 2, 4, 1, 3)
        lg = jnp.einsum('bqgrd,bqgkd->bgrqk', qb, kg).astype(jnp.float32) * ATTN_SCALE + bias
        lg = jnp.where((dist >= 0).transpose(0, 2, 1, 3)[:, :, None], lg, NEG_INF)
        p = jax.nn.softmax(lg, axis=-1)
        return jnp.einsum('bgrqk,bqgkd->bqgrd', p.astype(vg.dtype), vg)

    o_s = lax.map(selected_block, (q_blocks, sel_blocks, pos_blocks))
    o_s = o_s.transpose(1, 0, 2, 3, 4, 5).reshape(B, S, N_HEADS, HEAD_DIM)

    o_w = banded_attention(q.reshape(B, S, N_HEADS, HEAD_DIM), k_w, v_w, C_WINDOW - 1, 1, table)

    o = gates[:, :, 0] * o_c + gates[:, :, 1] * o_s + gates[:, :, 2] * o_w
    return o.reshape(B, S, HQ) @ w_o


def conv_ffn(h, w_up, conv_w, conv_b, w_down):
    S = h.shape[1]
    u = h @ w_up
    up = jnp.pad(u, ((0, 0), (CONV_WIDTH - 1, 0), (0, 0)))
    c = conv_b + up[:, CONV_WIDTH - 1:] * conv_w[CONV_WIDTH - 1]
    for j in range(CONV_WIDTH - 1):
        c = c + up[:, j:j + S] * conv_w[j]
    gate, val = jnp.split(c, 2, axis=-1)
    return (jax.nn.gelu(gate, approximate=True) * val) @ w_down


def setup_inputs(seed: int = 0) -> dict:
    key = jax.random.key(seed)
    ks = jax.random.split(key, 17)
    HQ = N_HEADS * HEAD_DIM

    def nrm(k, shape, scale):
        return jax.random.normal(k, shape, jnp.float32) * scale

    return {
        'x': nrm(ks[0], (BATCH, SEQ, D_MODEL), 1.0),
        'rel_table': nrm(ks[1], (NUM_BUCKETS, N_HEADS), 0.3),
        'norm_gains': 1.0 + nrm(ks[2], (DEPTH, 4, D_MODEL), 0.05),
        'a_w_in': nrm(ks[3], (N_A, D_MODEL, A_IN), D_MODEL ** -0.5),
        'a_w_o': nrm(ks[4], (N_A, HQ, D_MODEL), HQ ** -0.5),
        'b_w_in': nrm(ks[5], (N_B, D_MODEL, B_IN), D_MODEL ** -0.5),
        'b_sinks': nrm(ks[6], (N_B, N_HEADS), 0.5),
        'b_w_o': nrm(ks[7], (N_B, HQ, D_MODEL), HQ ** -0.5),
        'c_w_in': nrm(ks[8], (N_C, D_MODEL, C_IN), D_MODEL ** -0.5),
        'c_cmp_pos': nrm(ks[9], (N_C, 2, CMP_BLOCK, HEAD_DIM), 0.1),
        'c_cmp_w1': nrm(ks[10], (N_C, 2, CMP_BLOCK * HEAD_DIM, CMP_HIDDEN), (CMP_BLOCK * HEAD_DIM) ** -0.5),
        'c_cmp_w2': nrm(ks[11], (N_C, 2, CMP_HIDDEN, HEAD_DIM), CMP_HIDDEN ** -0.5),
        'c_w_o': nrm(ks[12], (N_C, HQ, D_MODEL), HQ ** -0.5),
        'ffn_w_up': nrm(ks[13], (DEPTH, D_MODEL, 2 * D_FF), D_MODEL ** -0.5),
        'ffn_conv_w': nrm(ks[14], (DEPTH, CONV_WIDTH, 2 * D_FF), 0.5),
        'ffn_conv_b': nrm(ks[15], (DEPTH, 2 * D_FF), 0.01),
        'ffn_w_down': nrm(ks[16], (DEPTH, D_FF, D_MODEL), D_FF ** -0.5),
    }


def reference(x, rel_table, norm_gains, a_w_in, a_w_o, b_w_in, b_sinks, b_w_o,
              c_w_in, c_cmp_pos, c_cmp_w1, c_cmp_w2, c_w_o,
              ffn_w_up, ffn_conv_w, ffn_conv_b, ffn_w_down):
    h = x
    for i in range(DEPTH):
        kind, j = i % N_MIXERS, i // N_MIXERS
        g = norm_gains[i]
        y = rms_norm(h, g[0])
        if kind == 0:
            y = dilated_attention(y, a_w_in[j], a_w_o[j], rel_table)
        elif kind == 1:
            y = sink_window_attention(y, b_w_in[j], b_sinks[j], b_w_o[j], rel_table)
        else:
            y = nsa_attention(y, c_w_in[j], c_cmp_pos[j], c_cmp_w1[j], c_cmp_w2[j], c_w_o[j], rel_table)
        h = h + rms_norm(y, g[1])
        y = conv_ffn(rms_norm(h, g[2]), ffn_w_up[i], ffn_conv_w[i], ffn_conv_b[i], ffn_w_down[i])
        h = h + rms_norm(y, g[3])
    return h
```

```python
import functools
import math

import numpy as np
import jax
import jax.numpy as jnp
from jax import lax
from jax.experimental import pallas as pl
from jax.experimental.pallas import tpu as pltpu

F32 = jnp.float32
BF16 = jnp.bfloat16

N_HEADS = 16
HEAD_DIM = 64
HQ = N_HEADS * HEAD_DIM
LANES = 128
ATTN_SCALE = HEAD_DIM ** -0.5
NUM_BUCKETS = 32
MAX_DISTANCE = 2048
RMS_EPS = 1e-6
NEG_INF = -1e30
FORCE_SCORE = 1e9
DIL_CONFIGS = ((128, 1), (512, 4), (2048, 16))
B_KV_HEADS = 2
B_WINDOW = 128
C_KV_HEADS = 2
CMP_BLOCK = 32
CMP_STRIDE = 16
SEL_BLOCK = 64
N_SELECT = 16
C_WINDOW = 512
CONV_WIDTH = 3
QBLK = 128
SEL_TK = 256
VMEM_LIMIT = 56 * 1024 * 1024


def _cparams(*sem):
    return pltpu.CompilerParams(dimension_semantics=sem, vmem_limit_bytes=VMEM_LIMIT)


def _t5_bucket_np(dist):
    max_exact = NUM_BUCKETS // 2
    d = np.maximum(dist, 0)
    df = np.maximum(d, 1).astype(np.float64)
    large = max_exact + np.floor(np.log(df / max_exact) / math.log(MAX_DISTANCE / max_exact)
                                 * (NUM_BUCKETS - max_exact) + 1e-9).astype(np.int64)
    large = np.minimum(large, NUM_BUCKETS - 1)
    return np.where(d < max_exact, d, large).astype(np.int32)


def _rms(x, g):
    ms = jnp.mean(x * x, axis=-1, keepdims=True)
    return (x * lax.rsqrt(ms + RMS_EPS)) * g


def _split3(w):
    hi = w.astype(BF16)
    r1 = w - hi.astype(F32)
    mid = r1.astype(BF16)
    lo = (r1 - mid.astype(F32)).astype(BF16)
    return hi, mid, lo


def _dot3(w, e):
    hi, mid, lo = _split3(w)
    d = functools.partial(jnp.dot, preferred_element_type=F32)
    return (d(hi, e) + d(mid, e)) + d(lo, e)


def _dot_nt(a, b):
    return lax.dot_general(a, b, (((1,), (1,)), ((), ())), preferred_element_type=F32)


def _stack_heads(q_tile, heads):
    lane = lax.broadcasted_iota(jnp.int32, (QBLK, LANES), 1)
    lo = lane < HEAD_DIM
    pieces = []
    for h in heads:
        q2 = q_tile(h // 2)
        keep = lo if h % 2 == 0 else jnp.logical_not(lo)
        pieces.append(jnp.where(keep, q2, jnp.zeros_like(q2)))
    return jnp.concatenate(pieces, axis=0)


def _merge_pairs(o, n_heads):
    lane = lax.broadcasted_iota(jnp.int32, (QBLK, LANES), 1)
    lo = lane < HEAD_DIM
    out = []
    for t in range(0, n_heads, 2):
        out.append(jnp.where(lo, o[t * QBLK:(t + 1) * QBLK], o[(t + 1) * QBLK:(t + 2) * QBLK]))
    return out


def _bias_kernel(tab_ref, idx_ref, o_ref):
    h = pl.program_id(1)
    idx = idx_ref[0]
    acc = jnp.full(idx.shape, NEG_INF, F32)
    for b in range(NUM_BUCKETS):
        acc = jnp.where(idx == b, tab_ref[b, h], acc)
    o_ref[0, 0] = acc


def bias_expand(table, idx):
    n, r, c = idx.shape
    return pl.pallas_call(
        _bias_kernel,
        grid=(n, N_HEADS),
        in_specs=[pl.BlockSpec(memory_space=pltpu.SMEM),
                  pl.BlockSpec((1, r, c), lambda i, h: (i, 0, 0))],
        out_specs=pl.BlockSpec((1, 1, r, c), lambda i, h: (i, h, 0, 0)),
        out_shape=jax.ShapeDtypeStruct((n, N_HEADS, r, c), F32),
        compiler_params=_cparams("parallel", "parallel"),
        name="bias_expand",
    )(table, jnp.asarray(idx))


def _banded_idx(n_prev, max_dist, stride):
    kb = (n_prev + 1) * QBLK
    dist = np.arange(QBLK)[:, None] + n_prev * QBLK - np.arange(kb)[None, :]
    valid = (dist >= 0) & (dist <= max_dist)
    return np.where(valid, _t5_bucket_np(dist * stride), -1).astype(np.int32)


def _norm_mm_kernel(x_ref, g_ref, w_ref, o_ref, xn_ref):
    @pl.when(pl.program_id(1) == 0)
    def _():
        xn_ref[...] = _rms(x_ref[...], g_ref[...]).astype(xn_ref.dtype)

    o_ref[...] = jnp.dot(xn_ref[...], w_ref[...], preferred_element_type=F32).astype(o_ref.dtype)


def norm_matmul(x, gain, w, out_dtype, tm, tn):
    t, d = x.shape
    n = w.shape[1]
    return pl.pallas_call(
        _norm_mm_kernel,
        grid=(t // tm, n // tn),
        in_specs=[pl.BlockSpec((tm, d), lambda i, j: (i, 0)),
                  pl.BlockSpec((1, d), lambda i, j: (0, 0)),
                  pl.BlockSpec((d, tn), lambda i, j: (0, j))],
        out_specs=pl.BlockSpec((tm, tn), lambda i, j: (i, j)),
        out_shape=jax.ShapeDtypeStruct((t, n), out_dtype),
        scratch_shapes=[pltpu.VMEM((tm, d), BF16)],
        compiler_params=_cparams("parallel", "arbitrary"),
        name="norm_matmul",
    )(x, gain.reshape(1, d), w)


def _finish_proj(y, w_ref, g_ref, h_ref, o_ref):
    z = jnp.dot(y.astype(BF16), w_ref[...], preferred_element_type=F32)
    o_ref[...] = h_ref[...] + _rms(z, g_ref[...])


def _proj_kernel(y_ref, w_ref, g_ref, h_ref, o_ref):
    _finish_proj(y_ref[...], w_ref, g_ref, h_ref, o_ref)


def _proj_a_kernel(o0_ref, o1_ref, o2_ref, l0_ref, l1_ref, l2_ref, e_ref, w_ref, g_ref, h_ref, o_ref):
    l0, l1, l2 = l0_ref[...], l1_ref[...], l2_ref[...]
    mx = jnp.maximum(jnp.maximum(l0, l1), l2)
    e0, e1, e2 = jnp.exp(l0 - mx), jnp.exp(l1 - mx), jnp.exp(l2 - mx)
    inv = 1.0 / ((e0 + e1) + e2)
    e = e_ref[...]
    y = o0_ref[...] * _dot3(e0 * inv, e)
    y = y + o1_ref[...] * _dot3(e1 * inv, e)
    y = y + o2_ref[...] * _dot3(e2 * inv, e)
    _finish_proj(y, w_ref, g_ref, h_ref, o_ref)


def _proj_c_kernel(oc_ref, os_ref, ow_ref, gr_ref, e_ref, w_ref, g_ref, h_ref, o_ref):
    sig = jax.nn.sigmoid(gr_ref[...])
    y = _dot3(sig, e_ref[0]) * oc_ref[...]
    y = y + _dot3(sig, e_ref[1]) * os_ref[...]
    y = y + _dot3(sig, e_ref[2]) * ow_ref[...]
    _finish_proj(y, w_ref, g_ref, h_ref, o_ref)


def _proj_call(kernel, name, tm, row_inputs, const_inputs, w, gain, h):
    t, d = h.shape
    k = w.shape[0]
    row_specs = [pl.BlockSpec((tm, a.shape[1]), lambda i: (i, 0)) for a in row_inputs]
    const_specs = [pl.BlockSpec(a.shape, (lambda nd: (lambda i: (0,) * nd))(a.ndim)) for a in const_inputs]
    return pl.pallas_call(
        kernel,
        grid=(t // tm,),
        in_specs=row_specs + const_specs + [
            pl.BlockSpec((k, d), lambda i: (0, 0)),
            pl.BlockSpec((1, d), lambda i: (0, 0)),
            pl.BlockSpec((tm, d), lambda i: (i, 0))],
        out_specs=pl.BlockSpec((tm, d), lambda i: (i, 0)),
        out_shape=jax.ShapeDtypeStruct((t, d), F32),
        compiler_params=_cparams("parallel"),
        name=name,
    )(*row_inputs, *const_inputs, w, gain.reshape(1, d), h)


def _head_expand_matrix(row_offset):
    e = np.zeros((LANES, HQ), np.float32)
    for h in range(N_HEADS):
        e[row_offset + h, h * HEAD_DIM:(h + 1) * HEAD_DIM] = 1.0
    return e


def _banded_kernel(*refs, tq, n_prev, n_chunks, with_sinks, with_lse):
    q_ref, kp_ref, kc_ref, vp_ref, vc_ref, bias_ref = refs[:6]
    pos = 6
    sink_ref = None
    if with_sinks:
        sink_ref = refs[pos]
        pos += 1
    o_ref = refs[pos]
    pos += 1
    lse_ref = None
    if with_lse:
        lse_ref = refs[pos]
        pos += 1
    kbuf, vbuf = refs[pos], refs[pos + 1]

    i = pl.program_id(1)
    hpc = N_HEADS // n_chunks
    kb = (n_prev + 1) * QBLK
    kbuf[0:tq] = kp_ref[0]
    kbuf[tq:2 * tq] = kc_ref[0]
    vbuf[0:tq] = vp_ref[0]
    vbuf[tq:2 * tq] = vc_ref[0]
    lane = lax.broadcasted_iota(jnp.int32, (QBLK, LANES), 1)

    for s in range(tq // QBLK):
        r0 = s * QBLK
        k0 = tq + r0 - n_prev * QBLK
        col = lax.broadcasted_iota(jnp.int32, (1, kb), 1) + k0
        kmask = jnp.where(jnp.logical_and(i == 0, col < tq), NEG_INF, 0.0).astype(F32)
        lse_acc = jnp.zeros((QBLK, LANES), F32)
        for c in range(n_chunks):
            heads = list(range(c * hpc, (c + 1) * hpc))
            qst = _stack_heads(lambda p: q_ref[0, r0:r0 + QBLK, p * LANES:(p + 1) * LANES], heads)
            kx = kbuf[k0:k0 + kb, c * LANES:(c + 1) * LANES]
            vx = vbuf[k0:k0 + kb, c * LANES:(c + 1) * LANES]
            sc = _dot_nt(qst, kx)
            sc = sc + bias_ref[c * hpc:(c + 1) * hpc].reshape(hpc * QBLK, kb) + kmask
            m = jnp.max(sc, axis=-1, keepdims=True)
            if with_sinks:
                sk = sink_ref[c * hpc * QBLK:(c + 1) * hpc * QBLK, :]
                m = jnp.maximum(m, sk)
            p = jnp.exp(sc - m)
            den = jnp.sum(p, axis=-1, keepdims=True)
            norm = den + jnp.exp(sk - m) if with_sinks else den
            pn = (p * (1.0 / norm)).astype(BF16)
            o = jnp.dot(pn, vx, preferred_element_type=F32)
            for t2, blk in enumerate(_merge_pairs(o, hpc)):
                pidx = (c * hpc) // 2 + t2
                o_ref[0, r0:r0 + QBLK, pidx * LANES:(pidx + 1) * LANES] = blk.astype(o_ref.dtype)
            if with_lse:
                lse = m + jnp.log(den)
                for t, h in enumerate(heads):
                    lse_acc = jnp.where(lane == h, lse[t * QBLK:(t + 1) * QBLK], lse_acc)
        if with_lse:
            lse_ref[0, r0:r0 + QBLK, :] = lse_acc


def banded_attention(q_arr, k_arr, v_arr, bias, *, n_rows, n_tiles, tq, n_prev, kw,
                     q_map, k_map, v_map, o_map, out_shape, lse_shape=None, sinks=None, name):
    n_chunks = kw // LANES
    kb = (n_prev + 1) * QBLK
    with_sinks = sinks is not None
    with_lse = lse_shape is not None

    def prev(fn):
        return lambda n, i: fn(n, jnp.maximum(i - 1, 0))

    in_specs = [pl.BlockSpec((1, tq, HQ), q_map),
                pl.BlockSpec((1, tq, kw), prev(k_map)),
                pl.BlockSpec((1, tq, kw), k_map),
                pl.BlockSpec((1, tq, kw), prev(v_map)),
                pl.BlockSpec((1, tq, kw), v_map),
                pl.BlockSpec((N_HEADS, QBLK, kb), lambda n, i: (0, 0, 0))]
    args = [q_arr, k_arr, k_arr, v_arr, v_arr, bias]
    if with_sinks:
        in_specs.append(pl.BlockSpec((N_HEADS * QBLK, 1), lambda n, i: (0, 0)))
        args.append(sinks)
    out_specs = [pl.BlockSpec((1, tq, HQ), o_map)]
    out_shapes = [jax.ShapeDtypeStruct(out_shape, F32)]
    if with_lse:
        out_specs.append(pl.BlockSpec((1, tq, LANES), o_map))
        out_shapes.append(jax.ShapeDtypeStruct(lse_shape, F32))
    res = pl.pallas_call(
        functools.partial(_banded_kernel, tq=tq, n_prev=n_prev, n_chunks=n_chunks,
                          with_sinks=with_sinks, with_lse=with_lse),
        grid=(n_rows, n_tiles),
        in_specs=in_specs,
        out_specs=out_specs,
        out_shape=out_shapes,
        scratch_shapes=[pltpu.VMEM((2 * tq, kw), BF16), pltpu.VMEM((2 * tq, kw), BF16)],
        compiler_params=_cparams("parallel", "arbitrary"),
        name=name,
    )(*args)
    return res


def _dup_groups(x):
    g0, g1 = x[..., :HEAD_DIM], x[..., HEAD_DIM:]
    return jnp.concatenate([g0, g0, g1, g1], axis=-1)


def _compress_kernel(ch_ref, pos_ref, w1_ref, w2_ref, o_ref):
    ch = ch_ref[0, 0, 0]
    rows = ch.shape[0]
    posv = pos_ref[0]
    a = jnp.dot((ch + posv[0:1]).astype(BF16), w1_ref[0, 0], preferred_element_type=F32)
    b = jnp.dot((ch + posv[1:2]).astype(BF16), w1_ref[0, 1], preferred_element_type=F32)
    hid = a + pltpu.roll(b, rows - 1, 0)
    act = jax.nn.gelu(hid, approximate=True)
    o_ref[0, 0, 0] = jnp.dot(act.astype(BF16), w2_ref[0], preferred_element_type=F32)


def compress(chunks, pos, w1, w2):
    _, b, g, rows, width = chunks.shape
    hid = w1.shape[-1]
    return pl.pallas_call(
        _compress_kernel,
        grid=(2, b, g),
        in_specs=[pl.BlockSpec((1, 1, 1, rows, width), lambda i, bb, gg: (i, bb, gg, 0, 0)),
                  pl.BlockSpec((1, 2, width), lambda i, bb, gg: (i, 0, 0)),
                  pl.BlockSpec((1, 2, width, hid), lambda i, bb, gg: (i, 0, 0, 0)),
                  pl.BlockSpec((1, hid, HEAD_DIM), lambda i, bb, gg: (i, 0, 0))],
        out_specs=pl.BlockSpec((1, 1, 1, rows, HEAD_DIM), lambda i, bb, gg: (i, bb, gg, 0, 0)),
        out_shape=jax.ShapeDtypeStruct((2, b, g, rows, HEAD_DIM), F32),
        compiler_params=_cparams("parallel", "parallel", "parallel"),
        name="nsa_compress",
    )(chunks, pos, w1, w2)


def _cmp_attn_kernel(q_ref, kc_ref, vc_ref, ov_ref, o_ref, sel_ref, *, n_sel_blocks, k_sel):
    qi = pl.program_id(1)
    ncr = kc_ref.shape[1]
    hpc = N_HEADS // C_KV_HEADS
    qpos = qi * QBLK + lax.broadcasted_iota(jnp.int32, (QBLK, 1), 0)
    cidx = lax.broadcasted_iota(jnp.int32, (1, ncr), 1)
    valid = (cidx * CMP_STRIDE + (CMP_BLOCK - 1)) <= qpos
    maskc = jnp.where(valid, 0.0, NEG_INF).astype(F32)
    anyv = (qpos >= CMP_BLOCK - 1).astype(F32)
    lane = lax.broadcasted_iota(jnp.int32, (QBLK, LANES), 1)
    cur = qpos // SEL_BLOCK
    forced = jnp.logical_or(jnp.logical_or(lane == 0, lane == cur), lane == cur - 1)
    allowed = lane <= cur
    for g in range(C_KV_HEADS):
        heads = list(range(g * hpc, (g + 1) * hpc))
        qst = _stack_heads(lambda p: q_ref[0, :, p * LANES:(p + 1) * LANES], heads)
        sc = _dot_nt(qst, kc_ref[0, :, g * LANES:(g + 1) * LANES])
        sc3 = sc.reshape(hpc, QBLK, ncr) + maskc[None]
        m = jnp.max(sc3, axis=-1, keepdims=True)
        e = jnp.exp(sc3 - m)
        ssum = jnp.sum(e, axis=-1, keepdims=True)
        p = (e * (1.0 / ssum)) * anyv[None]
        o = jnp.dot(p.reshape(hpc * QBLK, ncr).astype(BF16), vc_ref[0, :, g * LANES:(g + 1) * LANES],
                    preferred_element_type=F32)
        for t2, blk in enumerate(_merge_pairs(o, hpc)):
            pidx = (g * hpc) // 2 + t2
            o_ref[0, :, pidx * LANES:(pidx + 1) * LANES] = blk
        imp = _dot3(jnp.sum(p, axis=0), ov_ref[...])
        score = jnp.where(forced, FORCE_SCORE, jnp.where(allowed, imp, NEG_INF))
        rank = jnp.zeros((QBLK, LANES), F32)
        for i in range(n_sel_blocks):
            colv = score[:, i:i + 1]
            ge = jnp.where(colv >= score, 1.0, 0.0)
            gt = jnp.where(colv > score, 1.0, 0.0)
            rank = rank + jnp.where(lane > i, ge, gt)
        keep = jnp.logical_and(rank < k_sel, lane < n_sel_blocks)
        sel_ref[0, :, g * LANES:(g + 1) * LANES] = jnp.where(keep, 1.0, 0.0).astype(sel_ref.dtype)


def cmp_attention(q, kcd, vcd, ov, n_sel_blocks, k_sel):
    b, s, _ = q.shape
    ncr = kcd.shape[1]
    return pl.pallas_call(
        functools.partial(_cmp_attn_kernel, n_sel_blocks=n_sel_blocks, k_sel=k_sel),
        grid=(b, s // QBLK),
        in_specs=[pl.BlockSpec((1, QBLK, HQ), lambda bb, i: (bb, i, 0)),
                  pl.BlockSpec((1, ncr, 2 * LANES), lambda bb, i: (bb, 0, 0)),
                  pl.BlockSpec((1, ncr, 2 * LANES), lambda bb, i: (bb, 0, 0)),
                  pl.BlockSpec((ncr, LANES), lambda bb, i: (0, 0))],
        out_specs=[pl.BlockSpec((1, QBLK, HQ), lambda bb, i: (bb, i, 0)),
                   pl.BlockSpec((1, QBLK, 2 * LANES), lambda bb, i: (bb, i, 0))],
        out_shape=[jax.ShapeDtypeStruct((b, s, HQ), F32),
                   jax.ShapeDtypeStruct((b, s, 2 * LANES), BF16)],
        compiler_params=_cparams("parallel", "parallel"),
        name="nsa_cmp_attention",
    )(q, kcd, vcd, ov)


def _sel_attn_kernel(q_ref, k_ref, v_ref, sel_ref, bias_ref, o_ref, qst_ref, m_ref, l_ref, acc_ref, *, nb):
    qi = pl.program_id(0)
    kj = pl.program_id(1)
    last = qi // (SEL_TK // QBLK)
    hpc = N_HEADS // C_KV_HEADS

    @pl.when(kj == 0)
    def _():
        for b in range(nb):
            for g in range(C_KV_HEADS):
                heads = list(range(g * hpc, (g + 1) * hpc))
                qst_ref[b, g] = _stack_heads(lambda p: q_ref[b, :, p * LANES:(p + 1) * LANES], heads)
        m_ref[...] = jnp.full(m_ref.shape, NEG_INF, F32)
        l_ref[...] = jnp.zeros(l_ref.shape, F32)
        acc_ref[...] = jnp.zeros(acc_ref.shape, F32)

    @pl.when(kj <= last)
    def _():
        row = lax.broadcasted_iota(jnp.int32, (LANES, SEL_TK), 0)
        colb = lax.broadcasted_iota(jnp.int32, (LANES, SEL_TK), 1) // SEL_BLOCK + kj * (SEL_TK // SEL_BLOCK)
        expand = jnp.where(row == colb, 1.0, 0.0).astype(BF16)
        for b in range(nb):
            for g in range(C_KV_HEADS):
                selg = sel_ref[b, :, g * LANES:(g + 1) * LANES]
                madd = (jnp.dot(selg, expand, preferred_element_type=F32) - 1.0) * (-NEG_INF)
                sc = _dot_nt(qst_ref[b, g], k_ref[b, :, g * LANES:(g + 1) * LANES])
                sc3 = sc.reshape(hpc, QBLK, SEL_TK) + bias_ref[0, g * hpc:(g + 1) * hpc] + madd[None]
                sc = sc3.reshape(hpc * QBLK, SEL_TK)
                m_old = m_ref[b, g]
                m_new = jnp.maximum(m_old, jnp.max(sc, axis=-1, keepdims=True))
                alpha = jnp.exp(m_old - m_new)
                p = jnp.exp(sc - m_new)
                l_ref[b, g] = alpha * l_ref[b, g] + jnp.sum(p, axis=-1, keepdims=True)
                acc_ref[b, g] = alpha * acc_ref[b, g] + jnp.dot(
                    p.astype(BF16), v_ref[b, :, g * LANES:(g + 1) * LANES], preferred_element_type=F32)
                m_ref[b, g] = m_new

    @pl.when(kj == last)
    def _():
        for b in range(nb):
            for g in range(C_KV_HEADS):
                o = acc_ref[b, g] * (1.0 / l_ref[b, g])
                for t2, blk in enumerate(_merge_pairs(o, hpc)):
                    pidx = (g * hpc) // 2 + t2
                    o_ref[b, :, pidx * LANES:(pidx + 1) * LANES] = blk


def sel_attention(q, kd, vd, sel, bias_tiles):
    b, s, _ = q.shape
    nqt, nkt = s // QBLK, s // SEL_TK
    ratio = SEL_TK // QBLK
    nbt = bias_tiles.shape[0]
    hpc = N_HEADS // C_KV_HEADS

    def kmap(qi, kj):
        return (0, jnp.minimum(kj, qi // ratio), 0)

    def bmap(qi, kj):
        return (jnp.minimum(qi - ratio * jnp.minimum(kj, qi // ratio), nbt - 1), 0, 0, 0)

    return pl.pallas_call(
        functools.partial(_sel_attn_kernel, nb=b),
        grid=(nqt, nkt),
        in_specs=[pl.BlockSpec((b, QBLK, HQ), lambda qi, kj: (0, qi, 0)),
                  pl.BlockSpec((b, SEL_TK, 2 * LANES), kmap),
                  pl.BlockSpec((b, SEL_TK, 2 * LANES), kmap),
                  pl.BlockSpec((b, QBLK, 2 * LANES), lambda qi, kj: (0, qi, 0)),
                  pl.BlockSpec((1, N_HEADS, QBLK, SEL_TK), bmap)],
        out_specs=pl.BlockSpec((b, QBLK, HQ), lambda qi, kj: (0, qi, 0)),
        out_shape=jax.ShapeDtypeStruct((b, s, HQ), F32),
        scratch_shapes=[pltpu.VMEM((b, C_KV_HEADS, hpc * QBLK, LANES), BF16),
                        pltpu.VMEM((b, C_KV_HEADS, hpc * QBLK, 1), F32),
                        pltpu.VMEM((b, C_KV_HEADS, hpc * QBLK, 1), F32),
                        pltpu.VMEM((b, C_KV_HEADS, hpc * QBLK, LANES), F32)],
        compiler_params=_cparams("parallel", "arbitrary"),
        name="nsa_sel_attention",
    )(q, kd, vd, sel, bias_tiles)


def _sel_bias_idx(s):
    nqt = s // QBLK
    far = -(-(int(np.argmax(_t5_bucket_np(np.arange(4 * MAX_DISTANCE)) == NUM_BUCKETS - 1)) + SEL_TK) // QBLK)
    nbt = min(nqt, far + 1)
    d0 = np.arange(nbt)[:, None, None] * QBLK
    dist = d0 + np.arange(QBLK)[None, :, None] - np.arange(SEL_TK)[None, None, :]
    return np.where(dist >= 0, _t5_bucket_np(dist), -1).astype(np.int32)


def _ffn_kernel(h_ref, g2_ref, wg_ref, wv_ref, cwg_ref, cwv_ref, cbg_ref, cbv_ref, wd_ref, g3_ref,
                o_ref, xn_ref, acc_ref, cg_ref, cv_ref, *, tiles_per_seq):
    i = pl.program_id(0)
    j = pl.program_id(1)
    tm, tn = xn_ref.shape[0], wg_ref.shape[1]

    @pl.when(j == 0)
    def _():
        xn_ref[...] = _rms(h_ref[...], g2_ref[...]).astype(xn_ref.dtype)
        acc_ref[...] = jnp.zeros(acc_ref.shape, F32)

    xn = xn_ref[...]
    seq_start = (i % tiles_per_seq) == 0
    row = lax.broadcasted_iota(jnp.int32, (tm, tn), 0)

    @pl.when(seq_start)
    def _():
        cg_ref[j] = jnp.zeros(cg_ref.shape[1:], F32)
        cv_ref[j] = jnp.zeros(cv_ref.shape[1:], F32)

    def conv(u, carry_ref, cw_ref, cb_ref):
        prev = carry_ref[j]
        p1, p2 = prev[7:8], prev[6:7]
        s1 = jnp.where(row == 0, p1, pltpu.roll(u, 1, 0))
        s2 = jnp.where(row == 0, p2, jnp.where(row == 1, p1, pltpu.roll(u, 2, 0)))
        carry_ref[j] = u[tm - 8:tm]
        cw = cw_ref[...]
        return ((cb_ref[...] + u * cw[2:3]) + s2 * cw[0:1]) + s1 * cw[1:2]

    cg = conv(jnp.dot(xn, wg_ref[...], preferred_element_type=F32), cg_ref, cwg_ref, cbg_ref)
    cv = conv(jnp.dot(xn, wv_ref[...], preferred_element_type=F32), cv_ref, cwv_ref, cbv_ref)
    act = (jax.nn.gelu(cg, approximate=True) * cv).astype(BF16)
    acc_ref[...] += jnp.dot(act, wd_ref[...], preferred_element_type=F32)

    @pl.when(j == pl.num_programs(1) - 1)
    def _():
        o_ref[...] = h_ref[...] + _rms(acc_ref[...], g3_ref[...])


def conv_ffn(h, g2, w_up, conv_w, conv_b, w_down, g3, seq, tm, tn):
    t, d = h.shape
    dff = w_down.shape[0]
    nj = dff // tn
    return pl.pallas_call(
        functools.partial(_ffn_kernel, tiles_per_seq=seq // tm),
        grid=(t // tm, nj),
        in_specs=[pl.BlockSpec((tm, d), lambda i, j: (i, 0)),
                  pl.BlockSpec((1, d), lambda i, j: (0, 0)),
                  pl.BlockSpec((d, tn), lambda i, j: (0, j)),
                  pl.BlockSpec((d, tn), lambda i, j: (0, j + nj)),
                  pl.BlockSpec((CONV_WIDTH, tn), lambda i, j: (0, j)),
                  pl.BlockSpec((CONV_WIDTH, tn), lambda i, j: (0, j + nj)),
                  pl.BlockSpec((1, tn), lambda i, j: (0, j)),
                  pl.BlockSpec((1, tn), lambda i, j: (0, j + nj)),
                  pl.BlockSpec((tn, d), lambda i, j: (j, 0)),
                  pl.BlockSpec((1, d), lambda i, j: (0, 0))],
        out_specs=pl.BlockSpec((tm, d), lambda i, j: (i, 0)),
        out_shape=jax.ShapeDtypeStruct((t, d), F32),
        scratch_shapes=[pltpu.VMEM((tm, d), BF16), pltpu.VMEM((tm, d), F32),
                        pltpu.VMEM((nj, 8, tn), F32), pltpu.VMEM((nj, 8, tn), F32)],
        compiler_params=_cparams("arbitrary", "arbitrary"),
        name="conv_ffn",
    )(h, g2.reshape(1, d), w_up, w_up, conv_w, conv_w, conv_b.reshape(1, -1), conv_b.reshape(1, -1),
      w_down, g3.reshape(1, d))


def _row_tile(t):
    for tm in (1024, 512, 256, 128):
        if t % tm == 0:
            return tm
    raise ValueError(f"token count {t} is not a multiple of 128")


def mixer_a(h, gains, w_in, w_o, rel_table, bsz, seq):
    t, d = h.shape
    tm = _row_tile(t)
    n_dil = len(DIL_CONFIGS)
    a_in = w_in.shape[1]
    col_scale = np.ones((a_in,), np.float32).reshape(n_dil, 3, HQ)
    col_scale[:, 0] = ATTN_SCALE
    w = (w_in * col_scale.reshape(1, a_in)).astype(BF16)
    qkv = norm_matmul(h, gains[0], w, BF16, tm, HQ)
    idx = np.stack([_banded_idx(1, window // dil, dil) for window, dil in DIL_CONFIGS])
    bias = bias_expand(rel_table, idx)
    outs, lses = [], []
    blocks_per_pos = a_in // HQ
    for gi, (window, dil) in enumerate(DIL_CONFIGS):
        assert window // dil <= QBLK
        ln = seq // dil
        qv = qkv.reshape(bsz, ln, dil * a_in)

        def cmap(part, gi=gi, dil=dil):
            return lambda n, i: (n // dil, i, (n % dil) * blocks_per_pos + gi * 3 + part)

        def omap(n, i, dil=dil):
            return (n // dil, i, n % dil)

        o, lse = banded_attention(
            qv, qv, qv, bias[gi], n_rows=bsz * dil, n_tiles=ln // QBLK, tq=QBLK, n_prev=1, kw=HQ,
            q_map=cmap(0), k_map=cmap(1), v_map=cmap(2), o_map=omap,
            out_shape=(bsz, ln, dil * HQ), lse_shape=(bsz, ln, dil * LANES), name=f"dilated_attention_{dil}")
        outs.append(o.reshape(t, HQ))
        lses.append(lse.reshape(t, LANES))
    e = jnp.asarray(_head_expand_matrix(0), BF16)
    return _proj_call(_proj_a_kernel, "dilated_combine_proj", min(tm, 512), outs + lses, [e],
                      w_o.astype(BF16), gains[1], h)


def mixer_b(h, gains, w_in, sinks, w_o, rel_table, bsz, seq):
    t, d = h.shape
    tm = _row_tile(t)
    hk = B_KV_HEADS * HEAD_DIM
    n_in = w_in.shape[1]
    col_scale = np.ones((n_in,), np.float32)
    col_scale[:HQ] = ATTN_SCALE
    w = (w_in * col_scale[None]).astype(BF16)
    qkv = norm_matmul(h, gains[0], w, BF16, tm, n_in // 2 if (n_in // 2) % LANES == 0 else n_in)
    q = qkv[:, :HQ].reshape(bsz, seq, HQ)
    kd = _dup_groups(qkv[:, HQ:HQ + hk].reshape(bsz, seq, hk))
    vd = _dup_groups(qkv[:, HQ + hk:].reshape(bsz, seq, hk))
    bias = bias_expand(rel_table, _banded_idx(1, B_WINDOW - 1, 1)[None])[0]
    sink_rows = jnp.repeat(sinks.astype(F32), QBLK).reshape(N_HEADS * QBLK, 1)
    ident = lambda n, i: (n, i, 0)
    (o,) = banded_attention(q, kd, vd, bias, n_rows=bsz, n_tiles=seq // QBLK, tq=QBLK, n_prev=1,
                            kw=2 * LANES, q_map=ident, k_map=ident, v_map=ident, o_map=ident,
                            out_shape=(bsz, seq, HQ), sinks=sink_rows, name="sink_window_attention")
    return _proj_call(_proj_kernel, "sink_proj", tm, [o.reshape(t, HQ)], [], w_o.astype(BF16), gains[1], h)


def mixer_c(h, gains, w_in, cmp_pos, cmp_w1, cmp_w2, w_o, rel_table, bsz, seq):
    t, d = h.shape
    tm = _row_tile(t)
    g = C_KV_HEADS
    hk = g * HEAD_DIM
    rest = w_in.shape[1] - HQ
    rest_pad = -(-rest // LANES) * LANES
    wq = (w_in[:, :HQ] * ATTN_SCALE).astype(BF16)
    wr = jnp.pad(w_in[:, HQ:], ((0, 0), (0, rest_pad - rest))).astype(BF16)
    q = norm_matmul(h, gains[0], wq, BF16, tm, HQ).reshape(bsz, seq, HQ)
    r = norm_matmul(h, gains[0], wr, F32, tm, rest_pad)
    kv = [r[:, i * hk:(i + 1) * hk].reshape(bsz, seq, hk) for i in range(6)]
    gates_raw = r[:, 6 * hk:6 * hk + LANES]

    ncr = seq // CMP_STRIDE
    half = CMP_BLOCK // 2
    assert half == CMP_STRIDE
    chunks = jnp.stack([kv[0], kv[1]]).reshape(2, bsz, ncr, half, g, HEAD_DIM)
    chunks = chunks.transpose(0, 1, 4, 2, 3, 5).reshape(2, bsz, g, ncr, half * HEAD_DIM)
    pos = cmp_pos.reshape(2, 2, half * HEAD_DIM)
    w1 = cmp_w1.reshape(2, 2, half * HEAD_DIM, -1).astype(BF16)
    cmp = compress(chunks, pos, w1, cmp_w2.astype(BF16))
    cmp = cmp.transpose(0, 1, 3, 2, 4).reshape(2, bsz, ncr, hk).astype(BF16)
    kcd, vcd = _dup_groups(cmp[0]), _dup_groups(cmp[1])

    ns = seq // SEL_BLOCK
    assert ns <= LANES
    k_sel = min(N_SELECT, ns)
    starts = np.arange(ncr) * CMP_STRIDE
    blk = np.arange(LANES)
    ov = ((starts[:, None] < (blk[None, :] + 1) * SEL_BLOCK)
          & (starts[:, None] + CMP_BLOCK > blk[None, :] * SEL_BLOCK)
          & (blk[None, :] < ns) & (starts[:, None] + CMP_BLOCK <= seq))
    o_c, sel = cmp_attention(q, kcd, vcd, jnp.asarray(ov.astype(np.float32), BF16), ns, k_sel)

    sel_bias = bias_expand(rel_table, _sel_bias_idx(seq))
    o_s = sel_attention(q, _dup_groups(kv[2].astype(BF16)), _dup_groups(kv[3].astype(BF16)), sel, sel_bias)

    n_prev = -(-(C_WINDOW - 1) // QBLK)
    tqw = n_prev * QBLK
    wbias = bias_expand(rel_table, _banded_idx(n_prev, C_WINDOW - 1, 1)[None])[0]
    ident = lambda n, i: (n, i, 0)
    (o_w,) = banded_attention(q, _dup_groups(kv[4].astype(BF16)), _dup_groups(kv[5].astype(BF16)), wbias,
                              n_rows=bsz, n_tiles=seq // tqw, tq=tqw, n_prev=n_prev, kw=2 * LANES,
                              q_map=ident, k_map=ident, v_map=ident, o_map=ident,
                              out_shape=(bsz, seq, HQ), name="nsa_window_attention")

    e = jnp.asarray(np.stack([_head_expand_matrix(i * N_HEADS) for i in range(3)]), BF16)
    return _proj_call(_proj_c_kernel, "nsa_gate_proj", min(tm, 512),
                      [o_c.reshape(t, HQ), o_s.reshape(t, HQ), o_w.reshape(t, HQ), gates_raw], [e],
                      w_o.astype(BF16), gains[1], h)


def kernel(x, rel_table, norm_gains, a_w_in, a_w_o, b_w_in, b_sinks, b_w_o, c_w_in, c_cmp_pos, c_cmp_w1,
           c_cmp_w2, c_w_o, ffn_w_up, ffn_conv_w, ffn_conv_b, ffn_w_down):
    bsz, seq, d = x.shape
    depth = norm_gains.shape[0]
    h = x.reshape(bsz * seq, d)
    tm = _row_tile(seq)
    for i in range(depth):
        kind, j = i % 3, i // 3
        g = norm_gains[i]
        if kind == 0:
            h = mixer_a(h, g, a_w_in[j], a_w_o[j], rel_table, bsz, seq)
        elif kind == 1:
            h = mixer_b(h, g, b_w_in[j], b_sinks[j], b_w_o[j], rel_table, bsz, seq)
        else:
            h = mixer_c(h, g, c_w_in[j], c_cmp_pos[j], c_cmp_w1[j], c_cmp_w2[j], c_w_o[j], rel_table, bsz, seq)
        h = conv_ffn(h, g[2], ffn_w_up[i].astype(BF16), ffn_conv_w[i], ffn_conv_b[i],
                     ffn_w_down[i].astype(BF16), g[3], seq, tm, 256)
    return h.reshape(bsz, seq, d)
```

```python
import functools
import math

import numpy as np
import jax
import jax.numpy as jnp
from jax import lax
from jax.experimental import pallas as pl
from jax.experimental.pallas import tpu as pltpu

F32 = jnp.float32
BF16 = jnp.bfloat16

N_HEADS = 16
HEAD_DIM = 64
HQ = N_HEADS * HEAD_DIM
LANES = 128
ATTN_SCALE = HEAD_DIM ** -0.5
NUM_BUCKETS = 32
MAX_DISTANCE = 2048
RMS_EPS = 1e-6
NEG_INF = -1e30
FORCE_SCORE = 1e9
DIL_CONFIGS = ((128, 1), (512, 4), (2048, 16))
B_KV_HEADS = 2
B_WINDOW = 128
C_KV_HEADS = 2
CMP_BLOCK = 32
CMP_STRIDE = 16
SEL_BLOCK = 64
N_SELECT = 16
C_WINDOW = 512
CONV_WIDTH = 3
QBLK = 128
SEL_TK = 256
VMEM_LIMIT = 56 * 1024 * 1024


def _cparams(*sem):
    return pltpu.CompilerParams(dimension_semantics=sem, vmem_limit_bytes=VMEM_LIMIT)


def _t5_bucket_np(dist):
    max_exact = NUM_BUCKETS // 2
    d = np.maximum(dist, 0)
    df = np.maximum(d, 1).astype(np.float64)
    large = max_exact + np.floor(np.log(df / max_exact) / math.log(MAX_DISTANCE / max_exact)
                                 * (NUM_BUCKETS - max_exact) + 1e-9).astype(np.int64)
    large = np.minimum(large, NUM_BUCKETS - 1)
    return np.where(d < max_exact, d, large).astype(np.int32)


def _rms(x, g):
    ms = jnp.mean(x * x, axis=-1, keepdims=True)
    return (x * lax.rsqrt(ms + RMS_EPS)) * g


def _split3(w):
    hi = w.astype(BF16)
    r1 = w - hi.astype(F32)
    mid = r1.astype(BF16)
    lo = (r1 - mid.astype(F32)).astype(BF16)
    return hi, mid, lo


def _dot3(w, e):
    hi, mid, lo = _split3(w)
    d = functools.partial(jnp.dot, preferred_element_type=F32)
    return (d(hi, e) + d(mid, e)) + d(lo, e)


def _dot_nt(a, b):
    return lax.dot_general(a, b, (((1,), (1,)), ((), ())), preferred_element_type=F32)


def _stack_heads(q_tile, heads):
    lane = lax.broadcasted_iota(jnp.int32, (QBLK, LANES), 1)
    lo = lane < HEAD_DIM
    pieces = []
    for h in heads:
        q2 = q_tile(h // 2)
        keep = lo if h % 2 == 0 else jnp.logical_not(lo)
        pieces.append(jnp.where(keep, q2, jnp.zeros_like(q2)))
    return jnp.concatenate(pieces, axis=0)


def _merge_pairs(o, n_heads):
    lane = lax.broadcasted_iota(jnp.int32, (QBLK, LANES), 1)
    lo = lane < HEAD_DIM
    out = []
    for t in range(0, n_heads, 2):
        out.append(jnp.where(lo, o[t * QBLK:(t + 1) * QBLK], o[(t + 1) * QBLK:(t + 2) * QBLK]))
    return out


def _bias_kernel(tab_ref, idx_ref, o_ref):
    h = pl.program_id(1)
    idx = idx_ref[0]
    acc = jnp.full(idx.shape, NEG_INF, F32)
    for b in range(NUM_BUCKETS):
        acc = jnp.where(idx == b, tab_ref[b, h], acc)
    o_ref[0, 0] = acc


def bias_expand(table, idx):
    n, r, c = idx.shape
    return pl.pallas_call(
        _bias_kernel,
        grid=(n, N_HEADS),
        in_specs=[pl.BlockSpec(memory_space=pltpu.SMEM),
                  pl.BlockSpec((1, r, c), lambda i, h: (i, 0, 0))],
        out_specs=pl.BlockSpec((1, 1, r, c), lambda i, h: (i, h, 0, 0)),
        out_shape=jax.ShapeDtypeStruct((n, N_HEADS, r, c), F32),
        compiler_params=_cparams("parallel", "parallel"),
        name="bias_expand",
    )(table, jnp.asarray(idx))


def _banded_idx(n_prev, max_dist, stride):
    kb = (n_prev + 1) * QBLK
    dist = np.arange(QBLK)[:, None] + n_prev * QBLK - np.arange(kb)[None, :]
    valid = (dist >= 0) & (dist <= max_dist)
    return np.where(valid, _t5_bucket_np(dist * stride), -1).astype(np.int32)


def _norm_mm_kernel(x_ref, g_ref, w_ref, o_ref, xn_ref, *chunk_ref, dil):
    @pl.when(pl.program_id(1) == 0)
    def _():
        if dil == 1:
            xn_ref[...] = _rms(x_ref[...], g_ref[...]).astype(xn_ref.dtype)
        else:
            (xs_ref,) = chunk_ref
            nc = xs_ref.shape[0]
            for c in range(nc):
                xs_ref[c] = x_ref[:, c * LANES:(c + 1) * LANES]
            span = QBLK * dil
            for u in range(x_ref.shape[0] // span):
                for r in range(dil):
                    rows = jnp.concatenate(
                        [xs_ref[c, pl.ds(u * span + r, QBLK, stride=dil), :] for c in range(nc)], axis=1)
                    dst = u * span + r * QBLK
                    xn_ref[dst:dst + QBLK, :] = _rms(rows, g_ref[...]).astype(xn_ref.dtype)

    o_ref[...] = jnp.dot(xn_ref[...], w_ref[...], preferred_element_type=F32).astype(o_ref.dtype)


def norm_matmul(x, gain, w, out_dtype, tm, tn, dil=1):
    t, d = x.shape
    n = w.shape[1]
    assert tm % (QBLK * dil) == 0
    scratch = [pltpu.VMEM((tm, d), BF16)]
    if dil > 1:
        scratch.append(pltpu.VMEM((d // LANES, tm, LANES), F32))
    return pl.pallas_call(
        functools.partial(_norm_mm_kernel, dil=dil),
        grid=(t // tm, n // tn),
        in_specs=[pl.BlockSpec((tm, d), lambda i, j: (i, 0)),
                  pl.BlockSpec((1, d), lambda i, j: (0, 0)),
                  pl.BlockSpec((d, tn), lambda i, j: (0, j))],
        out_specs=pl.BlockSpec((tm, tn), lambda i, j: (i, j)),
        out_shape=jax.ShapeDtypeStruct((t, n), out_dtype),
        scratch_shapes=scratch,
        compiler_params=_cparams("parallel", "arbitrary"),
        name="norm_matmul",
    )(x, gain.reshape(1, d), w)


def _finish_proj(y, w_ref, g_ref, h_ref, o_ref):
    z = jnp.dot(y.astype(BF16), w_ref[...], preferred_element_type=F32)
    o_ref[...] = h_ref[...] + _rms(z, g_ref[...])


def _proj_kernel(y_ref, w_ref, g_ref, h_ref, o_ref):
    _finish_proj(y_ref[...], w_ref, g_ref, h_ref, o_ref)


def _unpermute(src_ref, dst_ref, dil, tm, i):
    span = QBLK * dil
    nc = dst_ref.shape[0]
    if span <= tm:
        for u in range(tm // span):
            for r in range(dil):
                lo = u * span + r * QBLK
                for c in range(nc):
                    dst_ref[c, pl.ds(u * span + r, QBLK, stride=dil), :] = src_ref[lo:lo + QBLK,
                                                                                   c * LANES:(c + 1) * LANES]
    else:
        per = tm // dil
        off = (i % (span // tm)) * per
        for r in range(dil):
            lo = pl.multiple_of(r * QBLK + off, 8)
            for c in range(nc):
                dst_ref[c, pl.ds(r, per, stride=dil), :] = src_ref[pl.ds(lo, per), c * LANES:(c + 1) * LANES]
    return jnp.concatenate([dst_ref[c] for c in range(nc)], axis=1) if nc > 1 else dst_ref[0]


def _proj_a_kernel(*refs, dils, tm):
    n = len(dils)
    o_refs, l_refs = refs[:n], refs[n:2 * n]
    e_ref, w_ref, g_ref, h_ref, out_ref = refs[2 * n:2 * n + 5]
    scratch = refs[2 * n + 5:]
    i = pl.program_id(0)
    outs, lses = [], []
    si = 0
    for gi, dil in enumerate(dils):
        if dil == 1:
            outs.append(o_refs[gi][...])
            lses.append(l_refs[gi][...])
        else:
            outs.append(_unpermute(o_refs[gi], scratch[si], dil, tm, i))
            lses.append(_unpermute(l_refs[gi], scratch[si + 1], dil, tm, i))
            si += 2
    mx = functools.reduce(jnp.maximum, lses)
    es = [jnp.exp(l - mx) for l in lses]
    inv = 1.0 / functools.reduce(lambda a, b: a + b, es)
    e = e_ref[...]
    y = outs[0] * _dot3(es[0] * inv, e)
    for gi in range(1, n):
        y = y + outs[gi] * _dot3(es[gi] * inv, e)
    _finish_proj(y, w_ref, g_ref, h_ref, out_ref)


def combine_proj_a(outs, lses, dils, e, w, gain, h, tm):
    t, d = h.shape

    def row_spec(width, dil):
        rows = max(tm, QBLK * dil)
        return pl.BlockSpec((rows, width), lambda i, q=rows // tm: (i // q, 0))

    in_specs = [row_spec(HQ, dil) for dil in dils] + [row_spec(LANES, dil) for dil in dils] + [
        pl.BlockSpec(e.shape, lambda i: (0, 0)),
        pl.BlockSpec(w.shape, lambda i: (0, 0)),
        pl.BlockSpec((1, d), lambda i: (0, 0)),
        pl.BlockSpec((tm, d), lambda i: (i, 0))]
    scratch = []
    for dil in dils:
        if dil > 1:
            scratch += [pltpu.VMEM((HQ // LANES, tm, LANES), F32), pltpu.VMEM((1, tm, LANES), F32)]
    return pl.pallas_call(
        functools.partial(_proj_a_kernel, dils=tuple(dils), tm=tm),
        grid=(t // tm,),
        in_specs=in_specs,
        out_specs=pl.BlockSpec((tm, d), lambda i: (i, 0)),
        out_shape=jax.ShapeDtypeStruct((t, d), F32),
        scratch_shapes=scratch,
        compiler_params=_cparams("arbitrary"),
        name="dilated_combine_proj",
    )(*outs, *lses, e, w, gain.reshape(1, d), h)


def _proj_c_kernel(oc_ref, os_ref, ow_ref, gr_ref, e_ref, w_ref, g_ref, h_ref, o_ref):
    sig = jax.nn.sigmoid(gr_ref[...])
    y = _dot3(sig, e_ref[0]) * oc_ref[...]
    y = y + _dot3(sig, e_ref[1]) * os_ref[...]
    y = y + _dot3(sig, e_ref[2]) * ow_ref[...]
    _finish_proj(y, w_ref, g_ref, h_ref, o_ref)


def _proj_call(kernel, name, tm, row_inputs, const_inputs, w, gain, h):
    t, d = h.shape
    k = w.shape[0]
    row_specs = [pl.BlockSpec((tm, a.shape[1]), lambda i: (i, 0)) for a in row_inputs]
    const_specs = [pl.BlockSpec(a.shape, (lambda nd: (lambda i: (0,) * nd))(a.ndim)) for a in const_inputs]
    return pl.pallas_call(
        kernel,
        grid=(t // tm,),
        in_specs=row_specs + const_specs + [
            pl.BlockSpec((k, d), lambda i: (0, 0)),
            pl.BlockSpec((1, d), lambda i: (0, 0)),
            pl.BlockSpec((tm, d), lambda i: (i, 0))],
        out_specs=pl.BlockSpec((tm, d), lambda i: (i, 0)),
        out_shape=jax.ShapeDtypeStruct((t, d), F32),
        compiler_params=_cparams("parallel"),
        name=name,
    )(*row_inputs, *const_inputs, w, gain.reshape(1, d), h)


def _head_expand_matrix(row_offset):
    e = np.zeros((LANES, HQ), np.float32)
    for h in range(N_HEADS):
        e[row_offset + h, h * HEAD_DIM:(h + 1) * HEAD_DIM] = 1.0
    return e


def _banded_kernel(*refs, tq, n_prev, n_chunks, with_sinks, with_lse):
    q_ref, kp_ref, kc_ref, vp_ref, vc_ref, bias_ref = refs[:6]
    pos = 6
    sink_ref = None
    if with_sinks:
        sink_ref = refs[pos]
        pos += 1
    o_ref = refs[pos]
    pos += 1
    lse_ref = None
    if with_lse:
        lse_ref = refs[pos]
        pos += 1
    kbuf, vbuf = refs[pos], refs[pos + 1]

    i = pl.program_id(1)
    hpc = N_HEADS // n_chunks
    kb = (n_prev + 1) * QBLK
    kbuf[0:tq] = kp_ref[0]
    kbuf[tq:2 * tq] = kc_ref[0]
    vbuf[0:tq] = vp_ref[0]
    vbuf[tq:2 * tq] = vc_ref[0]
    lane = lax.broadcasted_iota(jnp.int32, (QBLK, LANES), 1)

    for s in range(tq // QBLK):
        r0 = s * QBLK
        k0 = tq + r0 - n_prev * QBLK
        col = lax.broadcasted_iota(jnp.int32, (1, kb), 1) + k0
        kmask = jnp.where(jnp.logical_and(i == 0, col < tq), NEG_INF, 0.0).astype(F32)
        lse_acc = jnp.zeros((QBLK, LANES), F32)
        for c in range(n_chunks):
            heads = list(range(c * hpc, (c + 1) * hpc))
            qst = _stack_heads(lambda p: q_ref[0, r0:r0 + QBLK, p * LANES:(p + 1) * LANES], heads)
            kx = kbuf[k0:k0 + kb, c * LANES:(c + 1) * LANES]
            vx = vbuf[k0:k0 + kb, c * LANES:(c + 1) * LANES]
            sc = _dot_nt(qst, kx)
            sc = sc + bias_ref[c * hpc:(c + 1) * hpc].reshape(hpc * QBLK, kb) + kmask
            m = jnp.max(sc, axis=-1, keepdims=True)
            if with_sinks:
                sk = sink_ref[c * hpc * QBLK:(c + 1) * hpc * QBLK, :]
                m = jnp.maximum(m, sk)
            p = jnp.exp(sc - m)
            den = jnp.sum(p, axis=-1, keepdims=True)
            norm = den + jnp.exp(sk - m) if with_sinks else den
            pn = (p * (1.0 / norm)).astype(BF16)
            o = jnp.dot(pn, vx, preferred_element_type=F32)
            for t2, blk in enumerate(_merge_pairs(o, hpc)):
                pidx = (c * hpc) // 2 + t2
                o_ref[0, r0:r0 + QBLK, pidx * LANES:(pidx + 1) * LANES] = blk.astype(o_ref.dtype)
            if with_lse:
                lse = m + jnp.log(den)
                for t, h in enumerate(heads):
                    lse_acc = jnp.where(lane == h, lse[t * QBLK:(t + 1) * QBLK], lse_acc)
        if with_lse:
            lse_ref[0, r0:r0 + QBLK, :] = lse_acc


def banded_attention(q_arr, k_arr, v_arr, bias, *, n_rows, n_tiles, tq, n_prev, kw,
                     q_map, k_map, v_map, o_map, out_shape, lse_shape=None, sinks=None, name):
    n_chunks = kw // LANES
    kb = (n_prev + 1) * QBLK
    with_sinks = sinks is not None
    with_lse = lse_shape is not None

    def prev(fn):
        return lambda n, i: fn(n, jnp.maximum(i - 1, 0))

    in_specs = [pl.BlockSpec((1, tq, HQ), q_map),
                pl.BlockSpec((1, tq, kw), prev(k_map)),
                pl.BlockSpec((1, tq, kw), k_map),
                pl.BlockSpec((1, tq, kw), prev(v_map)),
                pl.BlockSpec((1, tq, kw), v_map),
                pl.BlockSpec((N_HEADS, QBLK, kb), lambda n, i: (0, 0, 0))]
    args = [q_arr, k_arr, k_arr, v_arr, v_arr, bias]
    if with_sinks:
        in_specs.append(pl.BlockSpec((N_HEADS * QBLK, 1), lambda n, i: (0, 0)))
        args.append(sinks)
    out_specs = [pl.BlockSpec((1, tq, HQ), o_map)]
    out_shapes = [jax.ShapeDtypeStruct(out_shape, F32)]
    if with_lse:
        out_specs.append(pl.BlockSpec((1, tq, LANES), o_map))
        out_shapes.append(jax.ShapeDtypeStruct(lse_shape, F32))
    res = pl.pallas_call(
        functools.partial(_banded_kernel, tq=tq, n_prev=n_prev, n_chunks=n_chunks,
                          with_sinks=with_sinks, with_lse=with_lse),
        grid=(n_rows, n_tiles),
        in_specs=in_specs,
        out_specs=out_specs,
        out_shape=out_shapes,
        scratch_shapes=[pltpu.VMEM((2 * tq, kw), BF16), pltpu.VMEM((2 * tq, kw), BF16)],
        compiler_params=_cparams("parallel", "arbitrary"),
        name=name,
    )(*args)
    return res


def _dup_groups(x):
    g0, g1 = x[..., :HEAD_DIM], x[..., HEAD_DIM:]
    return jnp.concatenate([g0, g0, g1, g1], axis=-1)


def _compress_kernel(ch_ref, pos_ref, w1_ref, w2_ref, o_ref):
    ch = ch_ref[0, 0, 0]
    rows = ch.shape[0]
    posv = pos_ref[0]
    a = jnp.dot((ch + posv[0:1]).astype(BF16), w1_ref[0, 0], preferred_element_type=F32)
    b = jnp.dot((ch + posv[1:2]).astype(BF16), w1_ref[0, 1], preferred_element_type=F32)
    hid = a + pltpu.roll(b, rows - 1, 0)
    act = jax.nn.gelu(hid, approximate=True)
    o_ref[0, 0, 0] = jnp.dot(act.astype(BF16), w2_ref[0], preferred_element_type=F32)


def compress(chunks, pos, w1, w2):
    _, b, g, rows, width = chunks.shape
    hid = w1.shape[-1]
    return pl.pallas_call(
        _compress_kernel,
        grid=(2, b, g),
        in_specs=[pl.BlockSpec((1, 1, 1, rows, width), lambda i, bb, gg: (i, bb, gg, 0, 0)),
                  pl.BlockSpec((1, 2, width), lambda i, bb, gg: (i, 0, 0)),
                  pl.BlockSpec((1, 2, width, hid), lambda i, bb, gg: (i, 0, 0, 0)),
                  pl.BlockSpec((1, hid, HEAD_DIM), lambda i, bb, gg: (i, 0, 0))],
        out_specs=pl.BlockSpec((1, 1, 1, rows, HEAD_DIM), lambda i, bb, gg: (i, bb, gg, 0, 0)),
        out_shape=jax.ShapeDtypeStruct((2, b, g, rows, HEAD_DIM), F32),
        compiler_params=_cparams("parallel", "parallel", "parallel"),
        name="nsa_compress",
    )(chunks, pos, w1, w2)


def _cmp_attn_kernel(q_ref, kc_ref, vc_ref, ov_ref, o_ref, sel_ref, *, n_sel_blocks, k_sel):
    qi = pl.program_id(1)
    ncr = kc_ref.shape[1]
    hpc = N_HEADS // C_KV_HEADS
    qpos = qi * QBLK + lax.broadcasted_iota(jnp.int32, (QBLK, 1), 0)
    cidx = lax.broadcasted_iota(jnp.int32, (1, ncr), 1)
    valid = (cidx * CMP_STRIDE + (CMP_BLOCK - 1)) <= qpos
    maskc = jnp.where(valid, 0.0, NEG_INF).astype(F32)
    anyv = (qpos >= CMP_BLOCK - 1).astype(F32)
    lane = lax.broadcasted_iota(jnp.int32, (QBLK, LANES), 1)
    cur = qpos // SEL_BLOCK
    forced = jnp.logical_or(jnp.logical_or(lane == 0, lane == cur), lane == cur - 1)
    allowed = lane <= cur
    for g in range(C_KV_HEADS):
        heads = list(range(g * hpc, (g + 1) * hpc))
        qst = _stack_heads(lambda p: q_ref[0, :, p * LANES:(p + 1) * LANES], heads)
        sc = _dot_nt(qst, kc_ref[0, :, g * LANES:(g + 1) * LANES])
        sc3 = sc.reshape(hpc, QBLK, ncr) + maskc[None]
        m = jnp.max(sc3, axis=-1, keepdims=True)
        e = jnp.exp(sc3 - m)
        ssum = jnp.sum(e, axis=-1, keepdims=True)
        p = (e * (1.0 / ssum)) * anyv[None]
        o = jnp.dot(p.reshape(hpc * QBLK, ncr).astype(BF16), vc_ref[0, :, g * LANES:(g + 1) * LANES],
                    preferred_element_type=F32)
        for t2, blk in enumerate(_merge_pairs(o, hpc)):
            pidx = (g * hpc) // 2 + t2
            o_ref[0, :, pidx * LANES:(pidx + 1) * LANES] = blk
        imp = _dot3(jnp.sum(p, axis=0), ov_ref[...])
        score = jnp.where(forced, FORCE_SCORE, jnp.where(allowed, imp, NEG_INF))
        rank = jnp.zeros((QBLK, LANES), F32)
        for i in range(n_sel_blocks):
            colv = score[:, i:i + 1]
            ge = jnp.where(colv >= score, 1.0, 0.0)
            gt = jnp.where(colv > score, 1.0, 0.0)
            rank = rank + jnp.where(lane > i, ge, gt)
        keep = jnp.logical_and(rank < k_sel, lane < n_sel_blocks)
        sel_ref[0, :, g * LANES:(g + 1) * LANES] = jnp.where(keep, 1.0, 0.0).astype(sel_ref.dtype)


def cmp_attention(q, kcd, vcd, ov, n_sel_blocks, k_sel):
    b, s, _ = q.shape
    ncr = kcd.shape[1]
    return pl.pallas_call(
        functools.partial(_cmp_attn_kernel, n_sel_blocks=n_sel_blocks, k_sel=k_sel),
        grid=(b, s // QBLK),
        in_specs=[pl.BlockSpec((1, QBLK, HQ), lambda bb, i: (bb, i, 0)),
                  pl.BlockSpec((1, ncr, 2 * LANES), lambda bb, i: (bb, 0, 0)),
                  pl.BlockSpec((1, ncr, 2 * LANES), lambda bb, i: (bb, 0, 0)),
                  pl.BlockSpec((ncr, LANES), lambda bb, i: (0, 0))],
        out_specs=[pl.BlockSpec((1, QBLK, HQ), lambda bb, i: (bb, i, 0)),
                   pl.BlockSpec((1, QBLK, 2 * LANES), lambda bb, i: (bb, i, 0))],
        out_shape=[jax.ShapeDtypeStruct((b, s, HQ), F32),
                   jax.ShapeDtypeStruct((b, s, 2 * LANES), BF16)],
        compiler_params=_cparams("parallel", "parallel"),
        name="nsa_cmp_attention",
    )(q, kcd, vcd, ov)


def _sel_attn_kernel(q_ref, k_ref, v_ref, sel_ref, bias_ref, o_ref, qst_ref, m_ref, l_ref, acc_ref, *, nb):
    qi = pl.program_id(0)
    kj = pl.program_id(1)
    last = qi // (SEL_TK // QBLK)
    hpc = N_HEADS // C_KV_HEADS

    @pl.when(kj == 0)
    def _():
        for b in range(nb):
            for g in range(C_KV_HEADS):
                heads = list(range(g * hpc, (g + 1) * hpc))
                qst_ref[b, g] = _stack_heads(lambda p: q_ref[b, :, p * LANES:(p + 1) * LANES], heads)
        m_ref[...] = jnp.full(m_ref.shape, NEG_INF, F32)
        l_ref[...] = jnp.zeros(l_ref.shape, F32)
        acc_ref[...] = jnp.zeros(acc_ref.shape, F32)

    @pl.when(kj <= last)
    def _():
        row = lax.broadcasted_iota(jnp.int32, (LANES, SEL_TK), 0)
        colb = lax.broadcasted_iota(jnp.int32, (LANES, SEL_TK), 1) // SEL_BLOCK + kj * (SEL_TK // SEL_BLOCK)
        expand = jnp.where(row == colb, 1.0, 0.0).astype(BF16)
        for b in range(nb):
            for g in range(C_KV_HEADS):
                selg = sel_ref[b, :, g * LANES:(g + 1) * LANES]
                madd = (jnp.dot(selg, expand, preferred_element_type=F32) - 1.0) * (-NEG_INF)
                sc = _dot_nt(qst_ref[b, g], k_ref[b, :, g * LANES:(g + 1) * LANES])
                sc3 = sc.reshape(hpc, QBLK, SEL_TK) + bias_ref[0, g * hpc:(g + 1) * hpc] + madd[None]
                sc = sc3.reshape(hpc * QBLK, SEL_TK)
                m_old = m_ref[b, g]
                m_new = jnp.maximum(m_old, jnp.max(sc, axis=-1, keepdims=True))
                alpha = jnp.exp(m_old - m_new)
                p = jnp.exp(sc - m_new)
                l_ref[b, g] = alpha * l_ref[b, g] + jnp.sum(p, axis=-1, keepdims=True)
                acc_ref[b, g] = alpha * acc_ref[b, g] + jnp.dot(
                    p.astype(BF16), v_ref[b, :, g * LANES:(g + 1) * LANES], preferred_element_type=F32)
                m_ref[b, g] = m_new

    @pl.when(kj == last)
    def _():
        for b in range(nb):
            for g in range(C_KV_HEADS):
                o = acc_ref[b, g] * (1.0 / l_ref[b, g])
                for t2, blk in enumerate(_merge_pairs(o, hpc)):
                    pidx = (g * hpc) // 2 + t2
                    o_ref[b, :, pidx * LANES:(pidx + 1) * LANES] = blk


def sel_attention(q, kd, vd, sel, bias_tiles):
    b, s, _ = q.shape
    nqt, nkt = s // QBLK, s // SEL_TK
    ratio = SEL_TK // QBLK
    nbt = bias_tiles.shape[0]
    hpc = N_HEADS // C_KV_HEADS

    def kmap(qi, kj):
        return (0, jnp.minimum(kj, qi // ratio), 0)

    def bmap(qi, kj):
        return (jnp.minimum(qi - ratio * jnp.minimum(kj, qi // ratio), nbt - 1), 0, 0, 0)

    return pl.pallas_call(
        functools.partial(_sel_attn_kernel, nb=b),
        grid=(nqt, nkt),
        in_specs=[pl.BlockSpec((b, QBLK, HQ), lambda qi, kj: (0, qi, 0)),
                  pl.BlockSpec((b, SEL_TK, 2 * LANES), kmap),
                  pl.BlockSpec((b, SEL_TK, 2 * LANES), kmap),
                  pl.BlockSpec((b, QBLK, 2 * LANES), lambda qi, kj: (0, qi, 0)),
                  pl.BlockSpec((1, N_HEADS, QBLK, SEL_TK), bmap)],
        out_specs=pl.BlockSpec((b, QBLK, HQ), lambda qi, kj: (0, qi, 0)),
        out_shape=jax.ShapeDtypeStruct((b, s, HQ), F32),
        scratch_shapes=[pltpu.VMEM((b, C_KV_HEADS, hpc * QBLK, LANES), BF16),
                        pltpu.VMEM((b, C_KV_HEADS, hpc * QBLK, 1), F32),
                        pltpu.VMEM((b, C_KV_HEADS, hpc * QBLK, 1), F32),
                        pltpu.VMEM((b, C_KV_HEADS, hpc * QBLK, LANES), F32)],
        compiler_params=_cparams("parallel", "arbitrary"),
        name="nsa_sel_attention",
    )(q, kd, vd, sel, bias_tiles)


def _sel_bias_idx(s):
    nqt = s // QBLK
    far = -(-(int(np.argmax(_t5_bucket_np(np.arange(4 * MAX_DISTANCE)) == NUM_BUCKETS - 1)) + SEL_TK) // QBLK)
    nbt = min(nqt, far + 1)
    d0 = np.arange(nbt)[:, None, None] * QBLK
    dist = d0 + np.arange(QBLK)[None, :, None] - np.arange(SEL_TK)[None, None, :]
    return np.where(dist >= 0, _t5_bucket_np(dist), -1).astype(np.int32)


def _ffn_kernel(h_ref, g2_ref, wg_ref, wv_ref, cwg_ref, cwv_ref, cbg_ref, cbv_ref, wd_ref, g3_ref,
                o_ref, xn_ref, acc_ref, cg_ref, cv_ref, *, tiles_per_seq):
    i = pl.program_id(0)
    j = pl.program_id(1)
    tm, tn = xn_ref.shape[0], wg_ref.shape[1]

    @pl.when(j == 0)
    def _():
        xn_ref[...] = _rms(h_ref[...], g2_ref[...]).astype(xn_ref.dtype)
        acc_ref[...] = jnp.zeros(acc_ref.shape, F32)

    xn = xn_ref[...]
    seq_start = (i % tiles_per_seq) == 0
    row = lax.broadcasted_iota(jnp.int32, (tm, tn), 0)

    @pl.when(seq_start)
    def _():
        cg_ref[j] = jnp.zeros(cg_ref.shape[1:], F32)
        cv_ref[j] = jnp.zeros(cv_ref.shape[1:], F32)

    def conv(u, carry_ref, cw_ref, cb_ref):
        prev = carry_ref[j]
        p1, p2 = prev[7:8], prev[6:7]
        s1 = jnp.where(row == 0, p1, pltpu.roll(u, 1, 0))
        s2 = jnp.where(row == 0, p2, jnp.where(row == 1, p1, pltpu.roll(u, 2, 0)))
        carry_ref[j] = u[tm - 8:tm]
        cw = cw_ref[...]
        return ((cb_ref[...] + u * cw[2:3]) + s2 * cw[0:1]) + s1 * cw[1:2]

    cg = conv(jnp.dot(xn, wg_ref[...], preferred_element_type=F32), cg_ref, cwg_ref, cbg_ref)
    cv = conv(jnp.dot(xn, wv_ref[...], preferred_element_type=F32), cv_ref, cwv_ref, cbv_ref)
    act = (jax.nn.gelu(cg, approximate=True) * cv).astype(BF16)
    acc_ref[...] += jnp.dot(act, wd_ref[...], preferred_element_type=F32)

    @pl.when(j == pl.num_programs(1) - 1)
    def _():
        o_ref[...] = h_ref[...] + _rms(acc_ref[...], g3_ref[...])


def conv_ffn(h, g2, w_up, conv_w, conv_b, w_down, g3, seq, tm, tn):
    t, d = h.shape
    dff = w_down.shape[0]
    nj = dff // tn
    return pl.pallas_call(
        functools.partial(_ffn_kernel, tiles_per_seq=seq // tm),
        grid=(t // tm, nj),
        in_specs=[pl.BlockSpec((tm, d), lambda i, j: (i, 0)),
                  pl.BlockSpec((1, d), lambda i, j: (0, 0)),
                  pl.BlockSpec((d, tn), lambda i, j: (0, j)),
                  pl.BlockSpec((d, tn), lambda i, j: (0, j + nj)),
                  pl.BlockSpec((CONV_WIDTH, tn), lambda i, j: (0, j)),
                  pl.BlockSpec((CONV_WIDTH, tn), lambda i, j: (0, j + nj)),
                  pl.BlockSpec((1, tn), lambda i, j: (0, j)),
                  pl.BlockSpec((1, tn), lambda i, j: (0, j + nj)),
                  pl.BlockSpec((tn, d), lambda i, j: (j, 0)),
                  pl.BlockSpec((1, d), lambda i, j: (0, 0))],
        out_specs=pl.BlockSpec((tm, d), lambda i, j: (i, 0)),
        out_shape=jax.ShapeDtypeStruct((t, d), F32),
        scratch_shapes=[pltpu.VMEM((tm, d), BF16), pltpu.VMEM((tm, d), F32),
                        pltpu.VMEM((nj, 8, tn), F32), pltpu.VMEM((nj, 8, tn), F32)],
        compiler_params=_cparams("arbitrary", "arbitrary"),
        name="conv_ffn",
    )(h, g2.reshape(1, d), w_up, w_up, conv_w, conv_w, conv_b.reshape(1, -1), conv_b.reshape(1, -1),
      w_down, g3.reshape(1, d))


def _row_tile(t):
    for tm in (1024, 512, 256, 128):
        if t % tm == 0:
            return tm
    raise ValueError(f"token count {t} is not a multiple of 128")


def mixer_a(h, gains, w_in, w_o, rel_table, bsz, seq):
    t, d = h.shape
    tm = _row_tile(t)
    n_dil = len(DIL_CONFIGS)
    a_in = w_in.shape[1]
    col_scale = np.ones((a_in,), np.float32).reshape(n_dil, 3, HQ)
    col_scale[:, 0] = ATTN_SCALE
    w = (w_in * col_scale.reshape(1, a_in)).astype(BF16)
    idx = np.stack([_banded_idx(1, window // dil, dil) for window, dil in DIL_CONFIGS])
    bias = bias_expand(rel_table, idx)
    outs, lses, dils = [], [], []
    blocks_per_seq = seq // QBLK
    for gi, (window, dil) in enumerate(DIL_CONFIGS):
        assert window // dil <= QBLK and seq % (QBLK * dil) == 0
        tmg = max(tm, QBLK * dil)
        qkv = norm_matmul(h, gains[0], w[:, gi * 3 * HQ:(gi + 1) * 3 * HQ], BF16, tmg, HQ, dil=dil)
        qkv = qkv.reshape(1, t, 3 * HQ)

        def rmap(part, dil=dil):
            return lambda n, i: (0, (n // dil) * blocks_per_seq + i * dil + n % dil, part)

        o, lse = banded_attention(
            qkv, qkv, qkv, bias[gi], n_rows=bsz * dil, n_tiles=seq // (QBLK * dil), tq=QBLK, n_prev=1, kw=HQ,
            q_map=rmap(0), k_map=rmap(1), v_map=rmap(2), o_map=rmap(0),
            out_shape=(1, t, HQ), lse_shape=(1, t, LANES), name=f"dilated_attention_{dil}")
        outs.append(o.reshape(t, HQ))
        lses.append(lse.reshape(t, LANES))
        dils.append(dil)
    e = jnp.asarray(_head_expand_matrix(0), BF16)
    return combine_proj_a(outs, lses, dils, e, w_o.astype(BF16), gains[1], h, min(tm, 512))


def mixer_b(h, gains, w_in, sinks, w_o, rel_table, bsz, seq):
    t, d = h.shape
    tm = _row_tile(t)
    hk = B_KV_HEADS * HEAD_DIM
    n_in = w_in.shape[1]
    col_scale = np.ones((n_in,), np.float32)
    col_scale[:HQ] = ATTN_SCALE
    w = (w_in * col_scale[None]).astype(BF16)
    qkv = norm_matmul(h, gains[0], w, BF16, tm, n_in // 2 if (n_in // 2) % LANES == 0 else n_in)
    q = qkv[:, :HQ].reshape(bsz, seq, HQ)
    kd = _dup_groups(qkv[:, HQ:HQ + hk].reshape(bsz, seq, hk))
    vd = _dup_groups(qkv[:, HQ + hk:].reshape(bsz, seq, hk))
    bias = bias_expand(rel_table, _banded_idx(1, B_WINDOW - 1, 1)[None])[0]
    sink_rows = jnp.repeat(sinks.astype(F32), QBLK).reshape(N_HEADS * QBLK, 1)
    ident = lambda n, i: (n, i, 0)
    (o,) = banded_attention(q, kd, vd, bias, n_rows=bsz, n_tiles=seq // QBLK, tq=QBLK, n_prev=1,
                            kw=2 * LANES, q_map=ident, k_map=ident, v_map=ident, o_map=ident,
                            out_shape=(bsz, seq, HQ), sinks=sink_rows, name="sink_window_attention")
    return _proj_call(_proj_kernel, "sink_proj", tm, [o.reshape(t, HQ)], [], w_o.astype(BF16), gains[1], h)


def mixer_c(h, gains, w_in, cmp_pos, cmp_w1, cmp_w2, w_o, rel_table, bsz, seq):
    t, d = h.shape
    tm = _row_tile(t)
    g = C_KV_HEADS
    hk = g * HEAD_DIM
    rest = w_in.shape[1] - HQ
    rest_pad = -(-rest // LANES) * LANES
    wq = (w_in[:, :HQ] * ATTN_SCALE).astype(BF16)
    wr = jnp.pad(w_in[:, HQ:], ((0, 0), (0, rest_pad - rest))).astype(BF16)
    q = norm_matmul(h, gains[0], wq, BF16, tm, HQ).reshape(bsz, seq, HQ)
    r = norm_matmul(h, gains[0], wr, F32, tm, rest_pad)
    kv = [r[:, i * hk:(i + 1) * hk].reshape(bsz, seq, hk) for i in range(6)]
    gates_raw = r[:, 6 * hk:6 * hk + LANES]

    ncr = seq // CMP_STRIDE
    half = CMP_BLOCK // 2
    assert half == CMP_STRIDE
    chunks = jnp.stack([kv[0], kv[1]]).reshape(2, bsz, ncr, half, g, HEAD_DIM)
    chunks = chunks.transpose(0, 1, 4, 2, 3, 5).reshape(2, bsz, g, ncr, half * HEAD_DIM)
    pos = cmp_pos.reshape(2, 2, half * HEAD_DIM)
    w1 = cmp_w1.reshape(2, 2, half * HEAD_DIM, -1).astype(BF16)
    cmp = compress(chunks, pos, w1, cmp_w2.astype(BF16))
    cmp = cmp.transpose(0, 1, 3, 2, 4).reshape(2, bsz, ncr, hk).astype(BF16)
    kcd, vcd = _dup_groups(cmp[0]), _dup_groups(cmp[1])

    ns = seq // SEL_BLOCK
    assert ns <= LANES
    k_sel = min(N_SELECT, ns)
    starts = np.arange(ncr) * CMP_STRIDE
    blk = np.arange(LANES)
    ov = ((starts[:, None] < (blk[None, :] + 1) * SEL_BLOCK)
          & (starts[:, None] + CMP_BLOCK > blk[None, :] * SEL_BLOCK)
          & (blk[None, :] < ns) & (starts[:, None] + CMP_BLOCK <= seq))
    o_c, sel = cmp_attention(q, kcd, vcd, jnp.asarray(ov.astype(np.float32), BF16), ns, k_sel)

    sel_bias = bias_expand(rel_table, _sel_bias_idx(seq))
    o_s = sel_attention(q, _dup_groups(kv[2].astype(BF16)), _dup_groups(kv[3].astype(BF16)), sel, sel_bias)

    n_prev = -(-(C_WINDOW - 1) // QBLK)
    tqw = n_prev * QBLK
    wbias = bias_expand(rel_table, _banded_idx(n_prev, C_WINDOW - 1, 1)[None])[0]
    ident = lambda n, i: (n, i, 0)
    (o_w,) = banded_attention(q, _dup_groups(kv[4].astype(BF16)), _dup_groups(kv[5].astype(BF16)), wbias,
                              n_rows=bsz, n_tiles=seq // tqw, tq=tqw, n_prev=n_prev, kw=2 * LANES,
                              q_map=ident, k_map=ident, v_map=ident, o_map=ident,
                              out_shape=(bsz, seq, HQ), name="nsa_window_attention")

    e = jnp.asarray(np.stack([_head_expand_matrix(i * N_HEADS) for i in range(3)]), BF16)
    return _proj_call(_proj_c_kernel, "nsa_gate_proj", min(tm, 512),
                      [o_c.reshape(t, HQ), o_s.reshape(t, HQ), o_w.reshape(t, HQ), gates_raw], [e],
                      w_o.astype(BF16), gains[1], h)


def kernel(x, rel_table, norm_gains, a_w_in, a_w_o, b_w_in, b_sinks, b_w_o, c_w_in, c_cmp_pos, c_cmp_w1,
           c_cmp_w2, c_w_o, ffn_w_up, ffn_conv_w, ffn_conv_b, ffn_w_down):
    bsz, seq, d = x.shape
    depth = norm_gains.shape[0]
    h = x.reshape(bsz * seq, d)
    tm = _row_tile(seq)
    for i in range(depth):
        kind, j = i % 3, i // 3
        g = norm_gains[i]
        if kind == 0:
            h = mixer_a(h, g, a_w_in[j], a_w_o[j], rel_table, bsz, seq)
        elif kind == 1:
            h = mixer_b(h, g, b_w_in[j], b_sinks[j], b_w_o[j], rel_table, bsz, seq)
        else:
            h = mixer_c(h, g, c_w_in[j], c_cmp_pos[j], c_cmp_w1[j], c_cmp_w2[j], c_w_o[j], rel_table, bsz, seq)
        h = conv_ffn(h, g[2], ffn_w_up[i].astype(BF16), ffn_conv_w[i], ffn_conv_b[i],
                     ffn_w_down[i].astype(BF16), g[3], seq, tm, 256)
    return h.reshape(bsz, seq, d)
```

```python
import functools
import math

import numpy as np
import jax
import jax.numpy as jnp
from jax import lax
from jax.experimental import pallas as pl
from jax.experimental.pallas import tpu as pltpu

F32 = jnp.float32
BF16 = jnp.bfloat16

N_HEADS = 16
HEAD_DIM = 64
HQ = N_HEADS * HEAD_DIM
LANES = 128
ATTN_SCALE = HEAD_DIM ** -0.5
NUM_BUCKETS = 32
MAX_DISTANCE = 2048
RMS_EPS = 1e-6
NEG_INF = -1e30
FORCE_SCORE = 1e9
DIL_CONFIGS = ((128, 1), (512, 4), (2048, 16))
B_KV_HEADS = 2
B_WINDOW = 128
C_KV_HEADS = 2
CMP_BLOCK = 32
CMP_STRIDE = 16
SEL_BLOCK = 64
N_SELECT = 16
C_WINDOW = 512
CONV_WIDTH = 3
QBLK = 128
SEL_CHUNK = 512
VMEM_LIMIT = 56 * 1024 * 1024


def _cparams(*sem):
    return pltpu.CompilerParams(dimension_semantics=sem, vmem_limit_bytes=VMEM_LIMIT)


def _t5_bucket_np(dist):
    max_exact = NUM_BUCKETS // 2
    d = np.maximum(dist, 0)
    df = np.maximum(d, 1).astype(np.float64)
    large = max_exact + np.floor(np.log(df / max_exact) / math.log(MAX_DISTANCE / max_exact)
                                 * (NUM_BUCKETS - max_exact) + 1e-9).astype(np.int64)
    large = np.minimum(large, NUM_BUCKETS - 1)
    return np.where(d < max_exact, d, large).astype(np.int32)


def _rms(x, g):
    ms = jnp.mean(x * x, axis=-1, keepdims=True)
    return (x * lax.rsqrt(ms + RMS_EPS)) * g


def _split3(w):
    hi = w.astype(BF16)
    r1 = w - hi.astype(F32)
    mid = r1.astype(BF16)
    lo = (r1 - mid.astype(F32)).astype(BF16)
    return hi, mid, lo


def _dot3(w, e):
    hi, mid, lo = _split3(w)
    d = functools.partial(jnp.dot, preferred_element_type=F32)
    return (d(hi, e) + d(mid, e)) + d(lo, e)


def _dot_nt(a, b):
    return lax.dot_general(a, b, (((1,), (1,)), ((), ())), preferred_element_type=F32)


def _stack_heads(q_tile, heads):
    lane = lax.broadcasted_iota(jnp.int32, (QBLK, LANES), 1)
    lo = lane < HEAD_DIM
    pieces = []
    for h in heads:
        q2 = q_tile(h // 2)
        keep = lo if h % 2 == 0 else jnp.logical_not(lo)
        pieces.append(jnp.where(keep, q2, jnp.zeros_like(q2)))
    return jnp.concatenate(pieces, axis=0)


def _merge_pairs(o, n_heads):
    lane = lax.broadcasted_iota(jnp.int32, (QBLK, LANES), 1)
    lo = lane < HEAD_DIM
    out = []
    for t in range(0, n_heads, 2):
        out.append(jnp.where(lo, o[t * QBLK:(t + 1) * QBLK], o[(t + 1) * QBLK:(t + 2) * QBLK]))
    return out


def _bias_kernel(tab_ref, idx_ref, o_ref):
    h = pl.program_id(1)
    idx = idx_ref[0]
    acc = jnp.full(idx.shape, NEG_INF, F32)
    for b in range(NUM_BUCKETS):
        acc = jnp.where(idx == b, tab_ref[b, h], acc)
    o_ref[0, 0] = acc


def bias_expand(table, idx):
    n, r, c = idx.shape
    return pl.pallas_call(
        _bias_kernel,
        grid=(n, N_HEADS),
        in_specs=[pl.BlockSpec(memory_space=pltpu.SMEM),
                  pl.BlockSpec((1, r, c), lambda i, h: (i, 0, 0))],
        out_specs=pl.BlockSpec((1, 1, r, c), lambda i, h: (i, h, 0, 0)),
        out_shape=jax.ShapeDtypeStruct((n, N_HEADS, r, c), F32),
        compiler_params=_cparams("parallel", "parallel"),
        name="bias_expand",
    )(table, jnp.asarray(idx))


def _banded_idx(n_prev, max_dist, stride):
    kb = (n_prev + 1) * QBLK
    dist = np.arange(QBLK)[:, None] + n_prev * QBLK - np.arange(kb)[None, :]
    valid = (dist >= 0) & (dist <= max_dist)
    return np.where(valid, _t5_bucket_np(dist * stride), -1).astype(np.int32)


def _norm_mm_kernel(x_ref, g_ref, w_ref, o_ref, xn_ref, *chunk_ref, dil):
    @pl.when(pl.program_id(1) == 0)
    def _():
        if dil == 1:
            xn_ref[...] = _rms(x_ref[...], g_ref[...]).astype(xn_ref.dtype)
        else:
            (xs_ref,) = chunk_ref
            nc = xs_ref.shape[0]
            for c in range(nc):
                xs_ref[c] = x_ref[:, c * LANES:(c + 1) * LANES]
            span = QBLK * dil
            for u in range(x_ref.shape[0] // span):
                for r in range(dil):
                    rows = jnp.concatenate(
                        [xs_ref[c, pl.ds(u * span + r, QBLK, stride=dil), :] for c in range(nc)], axis=1)
                    dst = u * span + r * QBLK
                    xn_ref[dst:dst + QBLK, :] = _rms(rows, g_ref[...]).astype(xn_ref.dtype)

    o_ref[...] = jnp.dot(xn_ref[...], w_ref[...], preferred_element_type=F32).astype(o_ref.dtype)


def norm_matmul(x, gain, w, out_dtype, tm, tn, dil=1):
    t, d = x.shape
    n = w.shape[1]
    assert tm % (QBLK * dil) == 0
    scratch = [pltpu.VMEM((tm, d), BF16)]
    if dil > 1:
        scratch.append(pltpu.VMEM((d // LANES, tm, LANES), F32))
    return pl.pallas_call(
        functools.partial(_norm_mm_kernel, dil=dil),
        grid=(t // tm, n // tn),
        in_specs=[pl.BlockSpec((tm, d), lambda i, j: (i, 0)),
                  pl.BlockSpec((1, d), lambda i, j: (0, 0)),
                  pl.BlockSpec((d, tn), lambda i, j: (0, j))],
        out_specs=pl.BlockSpec((tm, tn), lambda i, j: (i, j)),
        out_shape=jax.ShapeDtypeStruct((t, n), out_dtype),
        scratch_shapes=scratch,
        compiler_params=_cparams("parallel", "arbitrary"),
        name="norm_matmul",
    )(x, gain.reshape(1, d), w)


def _finish_proj(y, w_ref, g_ref, h_ref, o_ref):
    z = jnp.dot(y.astype(BF16), w_ref[...], preferred_element_type=F32)
    o_ref[...] = h_ref[...] + _rms(z, g_ref[...])


def _proj_kernel(y_ref, w_ref, g_ref, h_ref, o_ref):
    _finish_proj(y_ref[...], w_ref, g_ref, h_ref, o_ref)


def _unpermute(src_ref, dst_ref, dil, tm, i):
    span = QBLK * dil
    nc = dst_ref.shape[0]
    if span <= tm:
        for u in range(tm // span):
            for r in range(dil):
                lo = u * span + r * QBLK
                for c in range(nc):
                    dst_ref[c, pl.ds(u * span + r, QBLK, stride=dil), :] = src_ref[lo:lo + QBLK,
                                                                                   c * LANES:(c + 1) * LANES]
    else:
        per = tm // dil
        off = (i % (span // tm)) * per
        for r in range(dil):
            lo = pl.multiple_of(r * QBLK + off, 8)
            for c in range(nc):
                dst_ref[c, pl.ds(r, per, stride=dil), :] = src_ref[pl.ds(lo, per), c * LANES:(c + 1) * LANES]
    return jnp.concatenate([dst_ref[c] for c in range(nc)], axis=1) if nc > 1 else dst_ref[0]


def _proj_a_kernel(*refs, dils, tm):
    n = len(dils)
    o_refs, l_refs = refs[:n], refs[n:2 * n]
    e_ref, w_ref, g_ref, h_ref, out_ref = refs[2 * n:2 * n + 5]
    scratch = refs[2 * n + 5:]
    i = pl.program_id(0)
    outs, lses = [], []
    si = 0
    for gi, dil in enumerate(dils):
        if dil == 1:
            outs.append(o_refs[gi][...])
            lses.append(l_refs[gi][...])
        else:
            outs.append(_unpermute(o_refs[gi], scratch[si], dil, tm, i))
            lses.append(_unpermute(l_refs[gi], scratch[si + 1], dil, tm, i))
            si += 2
    mx = functools.reduce(jnp.maximum, lses)
    es = [jnp.exp(l - mx) for l in lses]
    inv = 1.0 / functools.reduce(lambda a, b: a + b, es)
    e = e_ref[...]
    y = outs[0] * _dot3(es[0] * inv, e)
    for gi in range(1, n):
        y = y + outs[gi] * _dot3(es[gi] * inv, e)
    _finish_proj(y, w_ref, g_ref, h_ref, out_ref)


def combine_proj_a(outs, lses, dils, e, w, gain, h, tm):
    t, d = h.shape

    def row_spec(width, dil):
        rows = max(tm, QBLK * dil)
        return pl.BlockSpec((rows, width), lambda i, q=rows // tm: (i // q, 0))

    in_specs = [row_spec(HQ, dil) for dil in dils] + [row_spec(LANES, dil) for dil in dils] + [
        pl.BlockSpec(e.shape, lambda i: (0, 0)),
        pl.BlockSpec(w.shape, lambda i: (0, 0)),
        pl.BlockSpec((1, d), lambda i: (0, 0)),
        pl.BlockSpec((tm, d), lambda i: (i, 0))]
    scratch = []
    for dil in dils:
        if dil > 1:
            scratch += [pltpu.VMEM((HQ // LANES, tm, LANES), F32), pltpu.VMEM((1, tm, LANES), F32)]
    return pl.pallas_call(
        functools.partial(_proj_a_kernel, dils=tuple(dils), tm=tm),
        grid=(t // tm,),
        in_specs=in_specs,
        out_specs=pl.BlockSpec((tm, d), lambda i: (i, 0)),
        out_shape=jax.ShapeDtypeStruct((t, d), F32),
        scratch_shapes=scratch,
        compiler_params=_cparams("arbitrary"),
        name="dilated_combine_proj",
    )(*outs, *lses, e, w, gain.reshape(1, d), h)


def _proj_c_kernel(oc_ref, os_ref, ow_ref, gr_ref, e_ref, w_ref, g_ref, h_ref, o_ref):
    sig = jax.nn.sigmoid(gr_ref[...])
    y = _dot3(sig, e_ref[0]) * oc_ref[...]
    y = y + _dot3(sig, e_ref[1]) * os_ref[...]
    y = y + _dot3(sig, e_ref[2]) * ow_ref[...]
    _finish_proj(y, w_ref, g_ref, h_ref, o_ref)


def _proj_call(kernel, name, tm, row_inputs, const_inputs, w, gain, h):
    t, d = h.shape
    k = w.shape[0]
    row_specs = [pl.BlockSpec((tm, a.shape[1]), lambda i: (i, 0)) for a in row_inputs]
    const_specs = [pl.BlockSpec(a.shape, (lambda nd: (lambda i: (0,) * nd))(a.ndim)) for a in const_inputs]
    return pl.pallas_call(
        kernel,
        grid=(t // tm,),
        in_specs=row_specs + const_specs + [
            pl.BlockSpec((k, d), lambda i: (0, 0)),
            pl.BlockSpec((1, d), lambda i: (0, 0)),
            pl.BlockSpec((tm, d), lambda i: (i, 0))],
        out_specs=pl.BlockSpec((tm, d), lambda i: (i, 0)),
        out_shape=jax.ShapeDtypeStruct((t, d), F32),
        compiler_params=_cparams("parallel"),
        name=name,
    )(*row_inputs, *const_inputs, w, gain.reshape(1, d), h)


def _head_expand_matrix(row_offset):
    e = np.zeros((LANES, HQ), np.float32)
    for h in range(N_HEADS):
        e[row_offset + h, h * HEAD_DIM:(h + 1) * HEAD_DIM] = 1.0
    return e


def _banded_kernel(*refs, tq, n_prev, n_chunks, with_sinks, with_lse):
    q_ref, kp_ref, kc_ref, vp_ref, vc_ref, bias_ref = refs[:6]
    pos = 6
    sink_ref = None
    if with_sinks:
        sink_ref = refs[pos]
        pos += 1
    o_ref = refs[pos]
    pos += 1
    lse_ref = None
    if with_lse:
        lse_ref = refs[pos]
        pos += 1
    kbuf, vbuf = refs[pos], refs[pos + 1]

    i = pl.program_id(1)
    hpc = N_HEADS // n_chunks
    kb = (n_prev + 1) * QBLK
    kbuf[0:tq] = kp_ref[0]
    kbuf[tq:2 * tq] = kc_ref[0]
    vbuf[0:tq] = vp_ref[0]
    vbuf[tq:2 * tq] = vc_ref[0]
    lane = lax.broadcasted_iota(jnp.int32, (QBLK, LANES), 1)

    for s in range(tq // QBLK):
        r0 = s * QBLK
        k0 = tq + r0 - n_prev * QBLK
        col = lax.broadcasted_iota(jnp.int32, (1, kb), 1) + k0
        kmask = jnp.where(jnp.logical_and(i == 0, col < tq), NEG_INF, 0.0).astype(F32)
        lse_acc = jnp.zeros((QBLK, LANES), F32)
        for c in range(n_chunks):
            heads = list(range(c * hpc, (c + 1) * hpc))
            qst = _stack_heads(lambda p: q_ref[0, r0:r0 + QBLK, p * LANES:(p + 1) * LANES], heads)
            kx = kbuf[k0:k0 + kb, c * LANES:(c + 1) * LANES]
            vx = vbuf[k0:k0 + kb, c * LANES:(c + 1) * LANES]
            sc = _dot_nt(qst, kx)
            sc = sc + bias_ref[c * hpc:(c + 1) * hpc].reshape(hpc * QBLK, kb) + kmask
            m = jnp.max(sc, axis=-1, keepdims=True)
            if with_sinks:
                sk = sink_ref[c * hpc * QBLK:(c + 1) * hpc * QBLK, :]
                m = jnp.maximum(m, sk)
            p = jnp.exp(sc - m)
            den = jnp.sum(p, axis=-1, keepdims=True)
            norm = den + jnp.exp(sk - m) if with_sinks else den
            pn = (p * (1.0 / norm)).astype(BF16)
            o = jnp.dot(pn, vx, preferred_element_type=F32)
            for t2, blk in enumerate(_merge_pairs(o, hpc)):
                pidx = (c * hpc) // 2 + t2
                o_ref[0, r0:r0 + QBLK, pidx * LANES:(pidx + 1) * LANES] = blk.astype(o_ref.dtype)
            if with_lse:
                lse = m + jnp.log(den)
                for t, h in enumerate(heads):
                    lse_acc = jnp.where(lane == h, lse[t * QBLK:(t + 1) * QBLK], lse_acc)
        if with_lse:
            lse_ref[0, r0:r0 + QBLK, :] = lse_acc


def banded_attention(q_arr, k_arr, v_arr, bias, *, n_rows, n_tiles, tq, n_prev, kw,
                     q_map, k_map, v_map, o_map, out_shape, lse_shape=None, sinks=None, name):
    n_chunks = kw // LANES
    kb = (n_prev + 1) * QBLK
    with_sinks = sinks is not None
    with_lse = lse_shape is not None

    def prev(fn):
        return lambda n, i: fn(n, jnp.maximum(i - 1, 0))

    in_specs = [pl.BlockSpec((1, tq, HQ), q_map),
                pl.BlockSpec((1, tq, kw), prev(k_map)),
                pl.BlockSpec((1, tq, kw), k_map),
                pl.BlockSpec((1, tq, kw), prev(v_map)),
                pl.BlockSpec((1, tq, kw), v_map),
                pl.BlockSpec((N_HEADS, QBLK, kb), lambda n, i: (0, 0, 0))]
    args = [q_arr, k_arr, k_arr, v_arr, v_arr, bias]
    if with_sinks:
        in_specs.append(pl.BlockSpec((N_HEADS * QBLK, 1), lambda n, i: (0, 0)))
        args.append(sinks)
    out_specs = [pl.BlockSpec((1, tq, HQ), o_map)]
    out_shapes = [jax.ShapeDtypeStruct(out_shape, F32)]
    if with_lse:
        out_specs.append(pl.BlockSpec((1, tq, LANES), o_map))
        out_shapes.append(jax.ShapeDtypeStruct(lse_shape, F32))
    res = pl.pallas_call(
        functools.partial(_banded_kernel, tq=tq, n_prev=n_prev, n_chunks=n_chunks,
                          with_sinks=with_sinks, with_lse=with_lse),
        grid=(n_rows, n_tiles),
        in_specs=in_specs,
        out_specs=out_specs,
        out_shape=out_shapes,
        scratch_shapes=[pltpu.VMEM((2 * tq, kw), BF16), pltpu.VMEM((2 * tq, kw), BF16)],
        compiler_params=_cparams("parallel", "arbitrary"),
        name=name,
    )(*args)
    return res


def _dup_groups(x):
    g0, g1 = x[..., :HEAD_DIM], x[..., HEAD_DIM:]
    return jnp.concatenate([g0, g0, g1, g1], axis=-1)


def _compress_kernel(ch_ref, pos_ref, w1_ref, w2_ref, o_ref):
    ch = ch_ref[0, 0, 0]
    rows = ch.shape[0]
    posv = pos_ref[0]
    a = jnp.dot((ch + posv[0:1]).astype(BF16), w1_ref[0, 0], preferred_element_type=F32)
    b = jnp.dot((ch + posv[1:2]).astype(BF16), w1_ref[0, 1], preferred_element_type=F32)
    hid = a + pltpu.roll(b, rows - 1, 0)
    act = jax.nn.gelu(hid, approximate=True)
    o_ref[0, 0, 0] = jnp.dot(act.astype(BF16), w2_ref[0], preferred_element_type=F32)


def compress(chunks, pos, w1, w2):
    _, b, g, rows, width = chunks.shape
    hid = w1.shape[-1]
    return pl.pallas_call(
        _compress_kernel,
        grid=(2, b, g),
        in_specs=[pl.BlockSpec((1, 1, 1, rows, width), lambda i, bb, gg: (i, bb, gg, 0, 0)),
                  pl.BlockSpec((1, 2, width), lambda i, bb, gg: (i, 0, 0)),
                  pl.BlockSpec((1, 2, width, hid), lambda i, bb, gg: (i, 0, 0, 0)),
                  pl.BlockSpec((1, hid, HEAD_DIM), lambda i, bb, gg: (i, 0, 0))],
        out_specs=pl.BlockSpec((1, 1, 1, rows, HEAD_DIM), lambda i, bb, gg: (i, bb, gg, 0, 0)),
        out_shape=jax.ShapeDtypeStruct((2, b, g, rows, HEAD_DIM), F32),
        compiler_params=_cparams("parallel", "parallel", "parallel"),
        name="nsa_compress",
    )(chunks, pos, w1, w2)


def _cmp_attn_kernel(q_ref, kc_ref, vc_ref, ov_ref, o_ref, sel_ref, *, n_sel_blocks, k_sel):
    qi = pl.program_id(1)
    ncr = kc_ref.shape[1]
    hpc = N_HEADS // C_KV_HEADS
    qpos = qi * QBLK + lax.broadcasted_iota(jnp.int32, (QBLK, 1), 0)
    cidx = lax.broadcasted_iota(jnp.int32, (1, ncr), 1)
    valid = (cidx * CMP_STRIDE + (CMP_BLOCK - 1)) <= qpos
    maskc = jnp.where(valid, 0.0, NEG_INF).astype(F32)
    anyv = (qpos >= CMP_BLOCK - 1).astype(F32)
    lane = lax.broadcasted_iota(jnp.int32, (QBLK, LANES), 1)
    blk_id = lane - HEAD_DIM
    cur = qpos // SEL_BLOCK
    forced = jnp.logical_or(jnp.logical_or(blk_id == 0, blk_id == cur), blk_id == cur - 1)
    allowed = blk_id <= cur
    for g in range(C_KV_HEADS):
        heads = list(range(g * hpc, (g + 1) * hpc))
        qst = _stack_heads(lambda p: q_ref[0, :, p * LANES:(p + 1) * LANES], heads)
        sc = _dot_nt(qst, kc_ref[0, :, g * LANES:(g + 1) * LANES])
        sc3 = sc.reshape(hpc, QBLK, ncr) + maskc[None]
        m = jnp.max(sc3, axis=-1, keepdims=True)
        e = jnp.exp(sc3 - m)
        ssum = jnp.sum(e, axis=-1, keepdims=True)
        p = (e * (1.0 / ssum)) * anyv[None]
        o = jnp.dot(p.reshape(hpc * QBLK, ncr).astype(BF16), vc_ref[0, :, g * LANES:(g + 1) * LANES],
                    preferred_element_type=F32)
        for t2, blk in enumerate(_merge_pairs(o, hpc)):
            pidx = (g * hpc) // 2 + t2
            o_ref[0, :, pidx * LANES:(pidx + 1) * LANES] = blk
        imp = _dot3(jnp.sum(p, axis=0), ov_ref[...])
        score = jnp.where(forced, FORCE_SCORE, jnp.where(allowed, imp, NEG_INF))
        rank = jnp.zeros((QBLK, LANES), F32)
        for i in range(n_sel_blocks):
            colv = score[:, HEAD_DIM + i:HEAD_DIM + i + 1]
            ge = jnp.where(colv >= score, 1.0, 0.0)
            gt = jnp.where(colv > score, 1.0, 0.0)
            rank = rank + jnp.where(blk_id > i, ge, gt)
        keep = jnp.logical_and(rank < k_sel, jnp.logical_and(blk_id >= 0, blk_id < n_sel_blocks))
        sel_ref[0, :, g * LANES:(g + 1) * LANES] = jnp.where(keep, 0.0, -1.0).astype(sel_ref.dtype)


def cmp_attention(q, kcd, vcd, ov, n_sel_blocks, k_sel):
    b, s, _ = q.shape
    ncr = kcd.shape[1]
    return pl.pallas_call(
        functools.partial(_cmp_attn_kernel, n_sel_blocks=n_sel_blocks, k_sel=k_sel),
        grid=(b, s // QBLK),
        in_specs=[pl.BlockSpec((1, QBLK, HQ), lambda bb, i: (bb, i, 0)),
                  pl.BlockSpec((1, ncr, 2 * LANES), lambda bb, i: (bb, 0, 0)),
                  pl.BlockSpec((1, ncr, 2 * LANES), lambda bb, i: (bb, 0, 0)),
                  pl.BlockSpec((ncr, LANES), lambda bb, i: (0, 0))],
        out_specs=[pl.BlockSpec((1, QBLK, HQ), lambda bb, i: (bb, i, 0)),
                   pl.BlockSpec((1, QBLK, 2 * LANES), lambda bb, i: (bb, i, 0))],
        out_shape=[jax.ShapeDtypeStruct((b, s, HQ), F32),
                   jax.ShapeDtypeStruct((b, s, 2 * LANES), BF16)],
        compiler_params=_cparams("parallel", "parallel"),
        name="nsa_cmp_attention",
    )(q, kcd, vcd, ov)


def _sel_attn_kernel(q_ref, pen_ref, k_ref, v_ref, bias_ref, o_ref, qst_ref, m_ref, acc_ref, *, nbt):
    qi = pl.program_id(2)
    hpc = N_HEADS // C_KV_HEADS
    nsub = SEL_CHUNK // QBLK
    lane = lax.broadcasted_iota(jnp.int32, (QBLK, LANES), 1)
    lo = lane < HEAD_DIM
    pen = pen_ref[0].astype(F32)
    for t in range(hpc):
        q2 = q_ref[0, :, (t // 2) * LANES:(t // 2 + 1) * LANES].astype(F32)
        if t % 2:
            q2 = pltpu.roll(q2, HEAD_DIM, 1)
        qst_ref[t * QBLK:(t + 1) * QBLK, :] = jnp.where(lo, q2, pen).astype(BF16)
    m_ref[...] = jnp.full(m_ref.shape, NEG_INF, F32)
    acc_ref[...] = jnp.zeros(acc_ref.shape, F32)

    def body(c, carry):
        k0 = pl.multiple_of(c * SEL_CHUNK, SEL_CHUNK)
        s = _dot_nt(qst_ref[...], k_ref[0, pl.ds(k0, SEL_CHUNK), :])
        tiles = []
        for u in range(nsub):
            off = qi - nsub * c - u
            idx = jnp.where(off < 0, nbt, jnp.minimum(off, nbt - 1))
            tiles.append(s[:, u * QBLK:(u + 1) * QBLK] + bias_ref[idx].reshape(hpc * QBLK, QBLK))
        m_old = m_ref[...]
        m_new = jnp.maximum(m_old, jnp.max(functools.reduce(jnp.maximum, tiles), axis=-1, keepdims=True))
        p = jnp.concatenate([jnp.exp(tl - m_new).astype(BF16) for tl in tiles], axis=1)
        pv = jnp.dot(p, v_ref[0, pl.ds(k0, SEL_CHUNK), :], preferred_element_type=F32)
        acc_ref[...] = jnp.exp(m_old - m_new) * acc_ref[...] + pv
        m_ref[...] = m_new
        return carry

    lax.fori_loop(0, qi // nsub + 1, body, 0)
    acc = acc_ref[...]
    rolled = pltpu.roll(acc, HEAD_DIM, 1)
    for t2 in range(hpc // 2):
        ev = slice(2 * t2 * QBLK, (2 * t2 + 1) * QBLK)
        od = slice((2 * t2 + 1) * QBLK, (2 * t2 + 2) * QBLK)
        even = acc[ev] * (1.0 / rolled[ev])
        odd = rolled[od] * (1.0 / acc[od])
        o_ref[0, :, t2 * LANES:(t2 + 1) * LANES] = jnp.where(lo, even, odd)


def sel_attention(q, kx, vx, pen, bias_tiles):
    b, s, _ = q.shape
    hpc = N_HEADS // C_KV_HEADS
    gw = hpc * HEAD_DIM
    nbt = bias_tiles.shape[0] - 1
    assert s % SEL_CHUNK == 0
    return pl.pallas_call(
        functools.partial(_sel_attn_kernel, nbt=nbt),
        grid=(b, C_KV_HEADS, s // QBLK),
        in_specs=[pl.BlockSpec((1, QBLK, gw), lambda bb, g, i: (bb, i, g)),
                  pl.BlockSpec((1, QBLK, LANES), lambda bb, g, i: (bb, i, g)),
                  pl.BlockSpec((1, s, LANES), lambda bb, g, i: (bb, 0, g)),
                  pl.BlockSpec((1, s, LANES), lambda bb, g, i: (bb, 0, g)),
                  pl.BlockSpec((nbt + 1, hpc, QBLK, QBLK), lambda bb, g, i: (0, g, 0, 0))],
        out_specs=pl.BlockSpec((1, QBLK, gw), lambda bb, g, i: (bb, i, g)),
        out_shape=jax.ShapeDtypeStruct((b, s, HQ), F32),
        scratch_shapes=[pltpu.VMEM((hpc * QBLK, LANES), BF16),
                        pltpu.VMEM((hpc * QBLK, LANES), F32),
                        pltpu.VMEM((hpc * QBLK, LANES), F32)],
        compiler_params=_cparams("parallel", "parallel", "arbitrary"),
        name="nsa_sel_attention",
    )(q, pen, kx, vx, bias_tiles)


def _sel_bias_idx(s):
    nqt = s // QBLK
    far = -(-(int(np.argmax(_t5_bucket_np(np.arange(4 * MAX_DISTANCE)) == NUM_BUCKETS - 1)) + QBLK) // QBLK)
    nbt = min(nqt, far + 1)
    d0 = np.arange(nbt)[:, None, None] * QBLK
    dist = d0 + np.arange(QBLK)[None, :, None] - np.arange(QBLK)[None, None, :]
    idx = np.where(dist >= 0, _t5_bucket_np(dist), -1).astype(np.int32)
    return np.concatenate([idx, np.full((1, QBLK, QBLK), -1, np.int32)])


def _ffn_kernel(h_ref, g2_ref, wg_ref, wv_ref, cwg_ref, cwv_ref, cbg_ref, cbv_ref, wd_ref, g3_ref,
                o_ref, xn_ref, acc_ref, cg_ref, cv_ref, *, tiles_per_seq):
    i = pl.program_id(0)
    j = pl.program_id(1)
    tm, tn = xn_ref.shape[0], wg_ref.shape[1]

    @pl.when(j == 0)
    def _():
        xn_ref[...] = _rms(h_ref[...], g2_ref[...]).astype(xn_ref.dtype)
        acc_ref[...] = jnp.zeros(acc_ref.shape, F32)

    xn = xn_ref[...]
    seq_start = (i % tiles_per_seq) == 0
    row = lax.broadcasted_iota(jnp.int32, (tm, tn), 0)

    @pl.when(seq_start)
    def _():
        cg_ref[j] = jnp.zeros(cg_ref.shape[1:], F32)
        cv_ref[j] = jnp.zeros(cv_ref.shape[1:], F32)

    def conv(u, carry_ref, cw_ref, cb_ref):
        prev = carry_ref[j]
        p1, p2 = prev[7:8], prev[6:7]
        s1 = jnp.where(row == 0, p1, pltpu.roll(u, 1, 0))
        s2 = jnp.where(row == 0, p2, jnp.where(row == 1, p1, pltpu.roll(u, 2, 0)))
        carry_ref[j] = u[tm - 8:tm]
        cw = cw_ref[...]
        return ((cb_ref[...] + u * cw[2:3]) + s2 * cw[0:1]) + s1 * cw[1:2]

    cg = conv(jnp.dot(xn, wg_ref[...], preferred_element_type=F32), cg_ref, cwg_ref, cbg_ref)
    cv = conv(jnp.dot(xn, wv_ref[...], preferred_element_type=F32), cv_ref, cwv_ref, cbv_ref)
    act = (jax.nn.gelu(cg, approximate=True) * cv).astype(BF16)
    acc_ref[...] += jnp.dot(act, wd_ref[...], preferred_element_type=F32)

    @pl.when(j == pl.num_programs(1) - 1)
    def _():
        o_ref[...] = h_ref[...] + _rms(acc_ref[...], g3_ref[...])


def conv_ffn(h, g2, w_up, conv_w, conv_b, w_down, g3, seq, tm, tn):
    t, d = h.shape
    dff = w_down.shape[0]
    nj = dff // tn
    return pl.pallas_call(
        functools.partial(_ffn_kernel, tiles_per_seq=seq // tm),
        grid=(t // tm, nj),
        in_specs=[pl.BlockSpec((tm, d), lambda i, j: (i, 0)),
                  pl.BlockSpec((1, d), lambda i, j: (0, 0)),
                  pl.BlockSpec((d, tn), lambda i, j: (0, j)),
                  pl.BlockSpec((d, tn), lambda i, j: (0, j + nj)),
                  pl.BlockSpec((CONV_WIDTH, tn), lambda i, j: (0, j)),
                  pl.BlockSpec((CONV_WIDTH, tn), lambda i, j: (0, j + nj)),
                  pl.BlockSpec((1, tn), lambda i, j: (0, j)),
                  pl.BlockSpec((1, tn), lambda i, j: (0, j + nj)),
                  pl.BlockSpec((tn, d), lambda i, j: (j, 0)),
                  pl.BlockSpec((1, d), lambda i, j: (0, 0))],
        out_specs=pl.BlockSpec((tm, d), lambda i, j: (i, 0)),
        out_shape=jax.ShapeDtypeStruct((t, d), F32),
        scratch_shapes=[pltpu.VMEM((tm, d), BF16), pltpu.VMEM((tm, d), F32),
                        pltpu.VMEM((nj, 8, tn), F32), pltpu.VMEM((nj, 8, tn), F32)],
        compiler_params=_cparams("arbitrary", "arbitrary"),
        name="conv_ffn",
    )(h, g2.reshape(1, d), w_up, w_up, conv_w, conv_w, conv_b.reshape(1, -1), conv_b.reshape(1, -1),
      w_down, g3.reshape(1, d))


def _row_tile(t):
    for tm in (1024, 512, 256, 128):
        if t % tm == 0:
            return tm
    raise ValueError(f"token count {t} is not a multiple of 128")


def mixer_a(h, gains, w_in, w_o, rel_table, bsz, seq):
    t, d = h.shape
    tm = _row_tile(t)
    n_dil = len(DIL_CONFIGS)
    a_in = w_in.shape[1]
    col_scale = np.ones((a_in,), np.float32).reshape(n_dil, 3, HQ)
    col_scale[:, 0] = ATTN_SCALE
    w = (w_in * col_scale.reshape(1, a_in)).astype(BF16)
    idx = np.stack([_banded_idx(1, window // dil, dil) for window, dil in DIL_CONFIGS])
    bias = bias_expand(rel_table, idx)
    outs, lses, dils = [], [], []
    blocks_per_seq = seq // QBLK
    for gi, (window, dil) in enumerate(DIL_CONFIGS):
        assert window // dil <= QBLK and seq % (QBLK * dil) == 0
        tmg = max(tm, QBLK * dil)
        qkv = norm_matmul(h, gains[0], w[:, gi * 3 * HQ:(gi + 1) * 3 * HQ], BF16, tmg, HQ, dil=dil)
        qkv = qkv.reshape(1, t, 3 * HQ)

        def rmap(part, dil=dil):
            return lambda n, i: (0, (n // dil) * blocks_per_seq + i * dil + n % dil, part)

        o, lse = banded_attention(
            qkv, qkv, qkv, bias[gi], n_rows=bsz * dil, n_tiles=seq // (QBLK * dil), tq=QBLK, n_prev=1, kw=HQ,
            q_map=rmap(0), k_map=rmap(1), v_map=rmap(2), o_map=rmap(0),
            out_shape=(1, t, HQ), lse_shape=(1, t, LANES), name=f"dilated_attention_{dil}")
        outs.append(o.reshape(t, HQ))
        lses.append(lse.reshape(t, LANES))
        dils.append(dil)
    e = jnp.asarray(_head_expand_matrix(0), BF16)
    return combine_proj_a(outs, lses, dils, e, w_o.astype(BF16), gains[1], h, min(tm, 512))


def mixer_b(h, gains, w_in, sinks, w_o, rel_table, bsz, seq):
    t, d = h.shape
    tm = _row_tile(t)
    hk = B_KV_HEADS * HEAD_DIM
    n_in = w_in.shape[1]
    col_scale = np.ones((n_in,), np.float32)
    col_scale[:HQ] = ATTN_SCALE
    w = (w_in * col_scale[None]).astype(BF16)
    qkv = norm_matmul(h, gains[0], w, BF16, tm, n_in // 2 if (n_in // 2) % LANES == 0 else n_in)
    q = qkv[:, :HQ].reshape(bsz, seq, HQ)
    kd = _dup_groups(qkv[:, HQ:HQ + hk].reshape(bsz, seq, hk))
    vd = _dup_groups(qkv[:, HQ + hk:].reshape(bsz, seq, hk))
    bias = bias_expand(rel_table, _banded_idx(1, B_WINDOW - 1, 1)[None])[0]
    sink_rows = jnp.repeat(sinks.astype(F32), QBLK).reshape(N_HEADS * QBLK, 1)
    ident = lambda n, i: (n, i, 0)
    (o,) = banded_attention(q, kd, vd, bias, n_rows=bsz, n_tiles=seq // QBLK, tq=QBLK, n_prev=1,
                            kw=2 * LANES, q_map=ident, k_map=ident, v_map=ident, o_map=ident,
                            out_shape=(bsz, seq, HQ), sinks=sink_rows, name="sink_window_attention")
    return _proj_call(_proj_kernel, "sink_proj", tm, [o.reshape(t, HQ)], [], w_o.astype(BF16), gains[1], h)


def mixer_c(h, gains, w_in, cmp_pos, cmp_w1, cmp_w2, w_o, rel_table, bsz, seq):
    t, d = h.shape
    tm = _row_tile(t)
    g = C_KV_HEADS
    hk = g * HEAD_DIM
    rest = w_in.shape[1] - HQ
    rest_pad = -(-rest // LANES) * LANES
    wq = (w_in[:, :HQ] * ATTN_SCALE).astype(BF16)
    wr = jnp.pad(w_in[:, HQ:], ((0, 0), (0, rest_pad - rest))).astype(BF16)
    q = norm_matmul(h, gains[0], wq, BF16, tm, HQ).reshape(bsz, seq, HQ)
    r = norm_matmul(h, gains[0], wr, F32, tm, rest_pad)
    kv = [r[:, i * hk:(i + 1) * hk].reshape(bsz, seq, hk) for i in range(6)]
    gates_raw = r[:, 6 * hk:6 * hk + LANES]

    ncr = seq // CMP_STRIDE
    half = CMP_BLOCK // 2
    assert half == CMP_STRIDE
    chunks = jnp.stack([kv[0], kv[1]]).reshape(2, bsz, ncr, half, g, HEAD_DIM)
    chunks = chunks.transpose(0, 1, 4, 2, 3, 5).reshape(2, bsz, g, ncr, half * HEAD_DIM)
    pos = cmp_pos.reshape(2, 2, half * HEAD_DIM)
    w1 = cmp_w1.reshape(2, 2, half * HEAD_DIM, -1).astype(BF16)
    cmp = compress(chunks, pos, w1, cmp_w2.astype(BF16))
    cmp = cmp.transpose(0, 1, 3, 2, 4).reshape(2, bsz, ncr, hk).astype(BF16)
    kcd, vcd = _dup_groups(cmp[0]), _dup_groups(cmp[1])

    ns = seq // SEL_BLOCK
    assert ns <= HEAD_DIM
    k_sel = min(N_SELECT, ns)
    starts = np.arange(ncr) * CMP_STRIDE
    blk = np.arange(LANES) - HEAD_DIM
    ov = ((starts[:, None] < (blk[None, :] + 1) * SEL_BLOCK)
          & (starts[:, None] + CMP_BLOCK > blk[None, :] * SEL_BLOCK)
          & (blk[None, :] >= 0) & (blk[None, :] < ns) & (starts[:, None] + CMP_BLOCK <= seq))
    o_c, pen = cmp_attention(q, kcd, vcd, jnp.asarray(ov.astype(np.float32), BF16), ns, k_sel)

    sel_bias = bias_expand(rel_table, _sel_bias_idx(seq))
    key_blk = np.arange(seq)[:, None] // SEL_BLOCK == np.arange(HEAD_DIM)[None, :]
    onehot = jnp.broadcast_to(jnp.asarray(np.where(key_blk, -NEG_INF, 0.0), BF16), (bsz, seq, HEAD_DIM))
    ones = jnp.ones((bsz, seq, HEAD_DIM), BF16)
    ks, vs = kv[2].astype(BF16), kv[3].astype(BF16)
    kx = jnp.concatenate([ks[..., :HEAD_DIM], onehot, ks[..., HEAD_DIM:], onehot], axis=-1)
    vx = jnp.concatenate([vs[..., :HEAD_DIM], ones, vs[..., HEAD_DIM:], ones], axis=-1)
    o_s = sel_attention(q, kx, vx, pen, sel_bias)

    n_prev = -(-(C_WINDOW - 1) // QBLK)
    tqw = n_prev * QBLK
    wbias = bias_expand(rel_table, _banded_idx(n_prev, C_WINDOW - 1, 1)[None])[0]
    ident = lambda n, i: (n, i, 0)
    (o_w,) = banded_attention(q, _dup_groups(kv[4].astype(BF16)), _dup_groups(kv[5].astype(BF16)), wbias,
                              n_rows=bsz, n_tiles=seq // tqw, tq=tqw, n_prev=n_prev, kw=2 * LANES,
                              q_map=ident, k_map=ident, v_map=ident, o_map=ident,
                              out_shape=(bsz, seq, HQ), name="nsa_window_attention")

    e = jnp.asarray(np.stack([_head_expand_matrix(i * N_HEADS) for i in range(3)]), BF16)
    return _proj_call(_proj_c_kernel, "nsa_gate_proj", min(tm, 512),
                      [o_c.reshape(t, HQ), o_s.reshape(t, HQ), o_w.reshape(t, HQ), gates_raw], [e],
                      w_o.astype(BF16), gains[1], h)


def kernel(x, rel_table, norm_gains, a_w_in, a_w_o, b_w_in, b_sinks, b_w_o, c_w_in, c_cmp_pos, c_cmp_w1,
           c_cmp_w2, c_w_o, ffn_w_up, ffn_conv_w, ffn_conv_b, ffn_w_down):
    bsz, seq, d = x.shape
    depth = norm_gains.shape[0]
    h = x.reshape(bsz * seq, d)
    tm = _row_tile(seq)
    for i in range(depth):
        kind, j = i % 3, i // 3
        g = norm_gains[i]
        if kind == 0:
            h = mixer_a(h, g, a_w_in[j], a_w_o[j], rel_table, bsz, seq)
        elif kind == 1:
            h = mixer_b(h, g, b_w_in[j], b_sinks[j], b_w_o[j], rel_table, bsz, seq)
        else:
            h = mixer_c(h, g, c_w_in[j], c_cmp_pos[j], c_cmp_w1[j], c_cmp_w2[j], c_w_o[j], rel_table, bsz, seq)
        h = conv_ffn(h, g[2], ffn_w_up[i].astype(BF16), ffn_conv_w[i], ffn_conv_b[i],
                     ffn_w_down[i].astype(BF16), g[3], seq, tm, 256)
    return h.reshape(bsz, seq, d)
```

```python
import functools
import math

import numpy as np
import jax
import jax.numpy as jnp
from jax import lax
from jax.experimental import pallas as pl
from jax.experimental.pallas import tpu as pltpu

F32 = jnp.float32
BF16 = jnp.bfloat16

N_HEADS = 16
HEAD_DIM = 64
HQ = N_HEADS * HEAD_DIM
LANES = 128
ATTN_SCALE = HEAD_DIM ** -0.5
NUM_BUCKETS = 32
MAX_DISTANCE = 2048
RMS_EPS = 1e-6
NEG_INF = -1e30
FORCE_SCORE = 1e9
DIL_CONFIGS = ((128, 1), (512, 4), (2048, 16))
B_KV_HEADS = 2
B_WINDOW = 128
C_KV_HEADS = 2
CMP_BLOCK = 32
CMP_STRIDE = 16
SEL_BLOCK = 64
N_SELECT = 16
C_WINDOW = 512
CONV_WIDTH = 3
QBLK = 128
SEL_CHUNK = 512
VMEM_LIMIT = 56 * 1024 * 1024


def _cparams(*sem):
    return pltpu.CompilerParams(dimension_semantics=sem, vmem_limit_bytes=VMEM_LIMIT)


def _t5_bucket_np(dist):
    max_exact = NUM_BUCKETS // 2
    d = np.maximum(dist, 0)
    df = np.maximum(d, 1).astype(np.float64)
    large = max_exact + np.floor(np.log(df / max_exact) / math.log(MAX_DISTANCE / max_exact)
                                 * (NUM_BUCKETS - max_exact) + 1e-9).astype(np.int64)
    large = np.minimum(large, NUM_BUCKETS - 1)
    return np.where(d < max_exact, d, large).astype(np.int32)


def _rms(x, g):
    ms = jnp.mean(x * x, axis=-1, keepdims=True)
    return (x * lax.rsqrt(ms + RMS_EPS)) * g


def _split3(w):
    hi = w.astype(BF16)
    r1 = w - hi.astype(F32)
    mid = r1.astype(BF16)
    lo = (r1 - mid.astype(F32)).astype(BF16)
    return hi, mid, lo


def _dot3(w, e):
    hi, mid, lo = _split3(w)
    d = functools.partial(jnp.dot, preferred_element_type=F32)
    return (d(hi, e) + d(mid, e)) + d(lo, e)


def _dot_nt(a, b):
    return lax.dot_general(a, b, (((1,), (1,)), ((), ())), preferred_element_type=F32)


def _stack_heads(q_tile, heads):
    lane = lax.broadcasted_iota(jnp.int32, (QBLK, LANES), 1)
    lo = lane < HEAD_DIM
    pieces = []
    for h in heads:
        q2 = q_tile(h // 2)
        keep = lo if h % 2 == 0 else jnp.logical_not(lo)
        pieces.append(jnp.where(keep, q2, jnp.zeros_like(q2)))
    return jnp.concatenate(pieces, axis=0)


def _merge_pairs(o, n_heads):
    lane = lax.broadcasted_iota(jnp.int32, (QBLK, LANES), 1)
    lo = lane < HEAD_DIM
    out = []
    for t in range(0, n_heads, 2):
        out.append(jnp.where(lo, o[t * QBLK:(t + 1) * QBLK], o[(t + 1) * QBLK:(t + 2) * QBLK]))
    return out


def _bias_kernel(tab_ref, idx_ref, o_ref):
    h = pl.program_id(1)
    idx = idx_ref[0]
    acc = jnp.full(idx.shape, NEG_INF, F32)
    for b in range(NUM_BUCKETS):
        acc = jnp.where(idx == b, tab_ref[b, h], acc)
    o_ref[0, 0] = acc


def bias_expand(table, idx):
    n, r, c = idx.shape
    return pl.pallas_call(
        _bias_kernel,
        grid=(n, N_HEADS),
        in_specs=[pl.BlockSpec(memory_space=pltpu.SMEM),
                  pl.BlockSpec((1, r, c), lambda i, h: (i, 0, 0))],
        out_specs=pl.BlockSpec((1, 1, r, c), lambda i, h: (i, h, 0, 0)),
        out_shape=jax.ShapeDtypeStruct((n, N_HEADS, r, c), F32),
        compiler_params=_cparams("parallel", "parallel"),
        name="bias_expand",
    )(table, jnp.asarray(idx))


def _banded_idx(n_prev, max_dist, stride):
    kb = (n_prev + 1) * QBLK
    dist = np.arange(QBLK)[:, None] + n_prev * QBLK - np.arange(kb)[None, :]
    valid = (dist >= 0) & (dist <= max_dist)
    return np.where(valid, _t5_bucket_np(dist * stride), -1).astype(np.int32)


def _norm_mm_kernel(x_ref, g_ref, w_ref, o_ref, xn_ref, *chunk_ref, dil):
    @pl.when(pl.program_id(1) == 0)
    def _():
        if dil == 1:
            xn_ref[...] = _rms(x_ref[...], g_ref[...]).astype(xn_ref.dtype)
        else:
            (xs_ref,) = chunk_ref
            nc = xs_ref.shape[0]
            for c in range(nc):
                xs_ref[c] = x_ref[:, c * LANES:(c + 1) * LANES]
            span = QBLK * dil
            for u in range(x_ref.shape[0] // span):
                for r in range(dil):
                    rows = jnp.concatenate(
                        [xs_ref[c, pl.ds(u * span + r, QBLK, stride=dil), :] for c in range(nc)], axis=1)
                    dst = u * span + r * QBLK
                    xn_ref[dst:dst + QBLK, :] = _rms(rows, g_ref[...]).astype(xn_ref.dtype)

    o_ref[...] = jnp.dot(xn_ref[...], w_ref[...], preferred_element_type=F32).astype(o_ref.dtype)


def norm_matmul(x, gain, w, out_dtype, tm, tn, dil=1):
    t, d = x.shape
    n = w.shape[1]
    assert tm % (QBLK * dil) == 0
    scratch = [pltpu.VMEM((tm, d), BF16)]
    if dil > 1:
        scratch.append(pltpu.VMEM((d // LANES, tm, LANES), F32))
    return pl.pallas_call(
        functools.partial(_norm_mm_kernel, dil=dil),
        grid=(t // tm, n // tn),
        in_specs=[pl.BlockSpec((tm, d), lambda i, j: (i, 0)),
                  pl.BlockSpec((1, d), lambda i, j: (0, 0)),
                  pl.BlockSpec((d, tn), lambda i, j: (0, j))],
        out_specs=pl.BlockSpec((tm, tn), lambda i, j: (i, j)),
        out_shape=jax.ShapeDtypeStruct((t, n), out_dtype),
        scratch_shapes=scratch,
        compiler_params=_cparams("parallel", "arbitrary"),
        name="norm_matmul",
    )(x, gain.reshape(1, d), w)


def _finish_proj(y, w_ref, g_ref, h_ref, o_ref):
    z = jnp.dot(y.astype(BF16), w_ref[...], preferred_element_type=F32)
    o_ref[...] = h_ref[...] + _rms(z, g_ref[...])


def _proj_kernel(y_ref, w_ref, g_ref, h_ref, o_ref):
    _finish_proj(y_ref[...], w_ref, g_ref, h_ref, o_ref)


def _unpermute(src_ref, dst_ref, dil, tm, i):
    span = QBLK * dil
    nc = dst_ref.shape[0]
    if span <= tm:
        for u in range(tm // span):
            for r in range(dil):
                lo = u * span + r * QBLK
                for c in range(nc):
                    dst_ref[c, pl.ds(u * span + r, QBLK, stride=dil), :] = src_ref[lo:lo + QBLK,
                                                                                   c * LANES:(c + 1) * LANES]
    else:
        per = tm // dil
        off = (i % (span // tm)) * per
        for r in range(dil):
            lo = pl.multiple_of(r * QBLK + off, 8)
            for c in range(nc):
                dst_ref[c, pl.ds(r, per, stride=dil), :] = src_ref[pl.ds(lo, per), c * LANES:(c + 1) * LANES]
    return jnp.concatenate([dst_ref[c] for c in range(nc)], axis=1) if nc > 1 else dst_ref[0]


def _proj_a_kernel(*refs, dils, tm):
    n = len(dils)
    o_refs, l_refs = refs[:n], refs[n:2 * n]
    e_ref, w_ref, g_ref, h_ref, out_ref = refs[2 * n:2 * n + 5]
    scratch = refs[2 * n + 5:]
    i = pl.program_id(0)
    outs, lses = [], []
    si = 0
    for gi, dil in enumerate(dils):
        if dil == 1:
            outs.append(o_refs[gi][...])
            lses.append(l_refs[gi][...])
        else:
            outs.append(_unpermute(o_refs[gi], scratch[si], dil, tm, i))
            lses.append(_unpermute(l_refs[gi], scratch[si + 1], dil, tm, i))
            si += 2
    mx = functools.reduce(jnp.maximum, lses)
    es = [jnp.exp(l - mx) for l in lses]
    inv = 1.0 / functools.reduce(lambda a, b: a + b, es)
    e = e_ref[...]
    y = outs[0] * _dot3(es[0] * inv, e)
    for gi in range(1, n):
        y = y + outs[gi] * _dot3(es[gi] * inv, e)
    _finish_proj(y, w_ref, g_ref, h_ref, out_ref)


def combine_proj_a(outs, lses, dils, e, w, gain, h, tm):
    t, d = h.shape

    def row_spec(width, dil):
        rows = max(tm, QBLK * dil)
        return pl.BlockSpec((rows, width), lambda i, q=rows // tm: (i // q, 0))

    in_specs = [row_spec(HQ, dil) for dil in dils] + [row_spec(LANES, dil) for dil in dils] + [
        pl.BlockSpec(e.shape, lambda i: (0, 0)),
        pl.BlockSpec(w.shape, lambda i: (0, 0)),
        pl.BlockSpec((1, d), lambda i: (0, 0)),
        pl.BlockSpec((tm, d), lambda i: (i, 0))]
    scratch = []
    for dil in dils:
        if dil > 1:
            scratch += [pltpu.VMEM((HQ // LANES, tm, LANES), F32), pltpu.VMEM((1, tm, LANES), F32)]
    return pl.pallas_call(
        functools.partial(_proj_a_kernel, dils=tuple(dils), tm=tm),
        grid=(t // tm,),
        in_specs=in_specs,
        out_specs=pl.BlockSpec((tm, d), lambda i: (i, 0)),
        out_shape=jax.ShapeDtypeStruct((t, d), F32),
        scratch_shapes=scratch,
        compiler_params=_cparams("arbitrary"),
        name="dilated_combine_proj",
    )(*outs, *lses, e, w, gain.reshape(1, d), h)


def _proj_c_kernel(oc_ref, os_ref, ow_ref, gr_ref, e_ref, w_ref, g_ref, h_ref, o_ref):
    sig = jax.nn.sigmoid(gr_ref[...])
    y = _dot3(sig, e_ref[0]) * oc_ref[...]
    y = y + _dot3(sig, e_ref[1]) * os_ref[...]
    y = y + _dot3(sig, e_ref[2]) * ow_ref[...]
    _finish_proj(y, w_ref, g_ref, h_ref, o_ref)


def _proj_call(kernel, name, tm, row_inputs, const_inputs, w, gain, h):
    t, d = h.shape
    k = w.shape[0]
    row_specs = [pl.BlockSpec((tm, a.shape[1]), lambda i: (i, 0)) for a in row_inputs]
    const_specs = [pl.BlockSpec(a.shape, (lambda nd: (lambda i: (0,) * nd))(a.ndim)) for a in const_inputs]
    return pl.pallas_call(
        kernel,
        grid=(t // tm,),
        in_specs=row_specs + const_specs + [
            pl.BlockSpec((k, d), lambda i: (0, 0)),
            pl.BlockSpec((1, d), lambda i: (0, 0)),
            pl.BlockSpec((tm, d), lambda i: (i, 0))],
        out_specs=pl.BlockSpec((tm, d), lambda i: (i, 0)),
        out_shape=jax.ShapeDtypeStruct((t, d), F32),
        compiler_params=_cparams("parallel"),
        name=name,
    )(*row_inputs, *const_inputs, w, gain.reshape(1, d), h)


def _head_expand_matrix(row_offset):
    e = np.zeros((LANES, HQ), np.float32)
    for h in range(N_HEADS):
        e[row_offset + h, h * HEAD_DIM:(h + 1) * HEAD_DIM] = 1.0
    return e


def _banded_kernel(*refs, tq, n_prev, n_chunks, with_sinks, with_lse):
    q_ref, kp_ref, kc_ref, vp_ref, vc_ref, bias_ref = refs[:6]
    pos = 6
    sink_ref = None
    if with_sinks:
        sink_ref = refs[pos]
        pos += 1
    o_ref = refs[pos]
    pos += 1
    lse_ref = None
    if with_lse:
        lse_ref = refs[pos]
        pos += 1
    kbuf, vbuf = refs[pos], refs[pos + 1]

    i = pl.program_id(1)
    hpc = N_HEADS // n_chunks
    kb = (n_prev + 1) * QBLK
    kbuf[0:tq] = kp_ref[0]
    kbuf[tq:2 * tq] = kc_ref[0]
    vbuf[0:tq] = vp_ref[0]
    vbuf[tq:2 * tq] = vc_ref[0]
    lane = lax.broadcasted_iota(jnp.int32, (QBLK, LANES), 1)

    for s in range(tq // QBLK):
        r0 = s * QBLK
        k0 = tq + r0 - n_prev * QBLK
        col = lax.broadcasted_iota(jnp.int32, (1, kb), 1) + k0
        kmask = jnp.where(jnp.logical_and(i == 0, col < tq), NEG_INF, 0.0).astype(F32)
        lse_acc = jnp.zeros((QBLK, LANES), F32)
        for c in range(n_chunks):
            heads = list(range(c * hpc, (c + 1) * hpc))
            qst = _stack_heads(lambda p: q_ref[0, r0:r0 + QBLK, p * LANES:(p + 1) * LANES], heads)
            kx = kbuf[k0:k0 + kb, c * LANES:(c + 1) * LANES]
            vx = vbuf[k0:k0 + kb, c * LANES:(c + 1) * LANES]
            sc = _dot_nt(qst, kx)
            sc = sc + bias_ref[c * hpc:(c + 1) * hpc].reshape(hpc * QBLK, kb) + kmask
            m = jnp.max(sc, axis=-1, keepdims=True)
            if with_sinks:
                sk = sink_ref[c * hpc * QBLK:(c + 1) * hpc * QBLK, :]
                m = jnp.maximum(m, sk)
            p = jnp.exp(sc - m)
            den = jnp.sum(p, axis=-1, keepdims=True)
            norm = den + jnp.exp(sk - m) if with_sinks else den
            pn = (p * (1.0 / norm)).astype(BF16)
            o = jnp.dot(pn, vx, preferred_element_type=F32)
            for t2, blk in enumerate(_merge_pairs(o, hpc)):
                pidx = (c * hpc) // 2 + t2
                o_ref[0, r0:r0 + QBLK, pidx * LANES:(pidx + 1) * LANES] = blk.astype(o_ref.dtype)
            if with_lse:
                lse = m + jnp.log(den)
                for t, h in enumerate(heads):
                    lse_acc = jnp.where(lane == h, lse[t * QBLK:(t + 1) * QBLK], lse_acc)
        if with_lse:
            lse_ref[0, r0:r0 + QBLK, :] = lse_acc


def banded_attention(q_arr, k_arr, v_arr, bias, *, n_rows, n_tiles, tq, n_prev, kw,
                     q_map, k_map, v_map, o_map, out_shape, lse_shape=None, sinks=None, name):
    n_chunks = kw // LANES
    kb = (n_prev + 1) * QBLK
    with_sinks = sinks is not None
    with_lse = lse_shape is not None

    def prev(fn):
        return lambda n, i: fn(n, jnp.maximum(i - 1, 0))

    in_specs = [pl.BlockSpec((1, tq, HQ), q_map),
                pl.BlockSpec((1, tq, kw), prev(k_map)),
                pl.BlockSpec((1, tq, kw), k_map),
                pl.BlockSpec((1, tq, kw), prev(v_map)),
                pl.BlockSpec((1, tq, kw), v_map),
                pl.BlockSpec((N_HEADS, QBLK, kb), lambda n, i: (0, 0, 0))]
    args = [q_arr, k_arr, k_arr, v_arr, v_arr, bias]
    if with_sinks:
        in_specs.append(pl.BlockSpec((N_HEADS * QBLK, 1), lambda n, i: (0, 0)))
        args.append(sinks)
    out_specs = [pl.BlockSpec((1, tq, HQ), o_map)]
    out_shapes = [jax.ShapeDtypeStruct(out_shape, F32)]
    if with_lse:
        out_specs.append(pl.BlockSpec((1, tq, LANES), o_map))
        out_shapes.append(jax.ShapeDtypeStruct(lse_shape, F32))
    res = pl.pallas_call(
        functools.partial(_banded_kernel, tq=tq, n_prev=n_prev, n_chunks=n_chunks,
                          with_sinks=with_sinks, with_lse=with_lse),
        grid=(n_rows, n_tiles),
        in_specs=in_specs,
        out_specs=out_specs,
        out_shape=out_shapes,
        scratch_shapes=[pltpu.VMEM((2 * tq, kw), BF16), pltpu.VMEM((2 * tq, kw), BF16)],
        compiler_params=_cparams("parallel", "arbitrary"),
        name=name,
    )(*args)
    return res


def _dup_groups(x):
    g0, g1 = x[..., :HEAD_DIM], x[..., HEAD_DIM:]
    return jnp.concatenate([g0, g0, g1, g1], axis=-1)


def _compress_kernel(ch_ref, pos_ref, w1_ref, w2_ref, o_ref):
    ch = ch_ref[0, 0, 0]
    rows = ch.shape[0]
    posv = pos_ref[0]
    a = jnp.dot((ch + posv[0:1]).astype(BF16), w1_ref[0, 0], preferred_element_type=F32)
    b = jnp.dot((ch + posv[1:2]).astype(BF16), w1_ref[0, 1], preferred_element_type=F32)
    hid = a + pltpu.roll(b, rows - 1, 0)
    act = jax.nn.gelu(hid, approximate=True)
    o_ref[0, 0, 0] = jnp.dot(act.astype(BF16), w2_ref[0], preferred_element_type=F32)


def compress(chunks, pos, w1, w2):
    _, b, g, rows, width = chunks.shape
    hid = w1.shape[-1]
    return pl.pallas_call(
        _compress_kernel,
        grid=(2, b, g),
        in_specs=[pl.BlockSpec((1, 1, 1, rows, width), lambda i, bb, gg: (i, bb, gg, 0, 0)),
                  pl.BlockSpec((1, 2, width), lambda i, bb, gg: (i, 0, 0)),
                  pl.BlockSpec((1, 2, width, hid), lambda i, bb, gg: (i, 0, 0, 0)),
                  pl.BlockSpec((1, hid, HEAD_DIM), lambda i, bb, gg: (i, 0, 0))],
        out_specs=pl.BlockSpec((1, 1, 1, rows, HEAD_DIM), lambda i, bb, gg: (i, bb, gg, 0, 0)),
        out_shape=jax.ShapeDtypeStruct((2, b, g, rows, HEAD_DIM), F32),
        compiler_params=_cparams("parallel", "parallel", "parallel"),
        name="nsa_compress",
    )(chunks, pos, w1, w2)


def _cmp_attn_kernel(q_ref, kc_ref, vc_ref, ov_ref, o_ref, sel_ref, *, n_sel_blocks, k_sel):
    qi = pl.program_id(1)
    ncr = kc_ref.shape[1]
    hpc = N_HEADS // C_KV_HEADS
    qpos = qi * QBLK + lax.broadcasted_iota(jnp.int32, (QBLK, 1), 0)
    cidx = lax.broadcasted_iota(jnp.int32, (1, ncr), 1)
    valid = (cidx * CMP_STRIDE + (CMP_BLOCK - 1)) <= qpos
    maskc = jnp.where(valid, 0.0, NEG_INF).astype(F32)
    anyv = (qpos >= CMP_BLOCK - 1).astype(F32)
    lane = lax.broadcasted_iota(jnp.int32, (QBLK, LANES), 1)
    blk_id = lane - HEAD_DIM
    cur = qpos // SEL_BLOCK
    forced = jnp.logical_or(jnp.logical_or(blk_id == 0, blk_id == cur), blk_id == cur - 1)
    allowed = blk_id <= cur
    for g in range(C_KV_HEADS):
        heads = list(range(g * hpc, (g + 1) * hpc))
        qst = _stack_heads(lambda p: q_ref[0, :, p * LANES:(p + 1) * LANES], heads)
        sc = _dot_nt(qst, kc_ref[0, :, g * LANES:(g + 1) * LANES])
        sc3 = sc.reshape(hpc, QBLK, ncr) + maskc[None]
        m = jnp.max(sc3, axis=-1, keepdims=True)
        e = jnp.exp(sc3 - m)
        ssum = jnp.sum(e, axis=-1, keepdims=True)
        p = (e * (1.0 / ssum)) * anyv[None]
        o = jnp.dot(p.reshape(hpc * QBLK, ncr).astype(BF16), vc_ref[0, :, g * LANES:(g + 1) * LANES],
                    preferred_element_type=F32)
        for t2, blk in enumerate(_merge_pairs(o, hpc)):
            pidx = (g * hpc) // 2 + t2
            o_ref[0, :, pidx * LANES:(pidx + 1) * LANES] = blk
        imp = _dot3(jnp.sum(p, axis=0), ov_ref[...])
        score = jnp.where(forced, FORCE_SCORE, jnp.where(allowed, imp, NEG_INF))
        rank = jnp.zeros((QBLK, LANES), F32)
        for i in range(n_sel_blocks):
            colv = score[:, HEAD_DIM + i:HEAD_DIM + i + 1]
            ge = jnp.where(colv >= score, 1.0, 0.0)
            gt = jnp.where(colv > score, 1.0, 0.0)
            rank = rank + jnp.where(blk_id > i, ge, gt)
        keep = jnp.logical_and(rank < k_sel, jnp.logical_and(blk_id >= 0, blk_id < n_sel_blocks))
        sel_ref[0, :, g * LANES:(g + 1) * LANES] = jnp.where(keep, 0.0, -1.0).astype(sel_ref.dtype)


def cmp_attention(q, kcd, vcd, ov, n_sel_blocks, k_sel):
    b, s, _ = q.shape
    ncr = kcd.shape[1]
    return pl.pallas_call(
        functools.partial(_cmp_attn_kernel, n_sel_blocks=n_sel_blocks, k_sel=k_sel),
        grid=(b, s // QBLK),
        in_specs=[pl.BlockSpec((1, QBLK, HQ), lambda bb, i: (bb, i, 0)),
                  pl.BlockSpec((1, ncr, 2 * LANES), lambda bb, i: (bb, 0, 0)),
                  pl.BlockSpec((1, ncr, 2 * LANES), lambda bb, i: (bb, 0, 0)),
                  pl.BlockSpec((ncr, LANES), lambda bb, i: (0, 0))],
        out_specs=[pl.BlockSpec((1, QBLK, HQ), lambda bb, i: (bb, i, 0)),
                   pl.BlockSpec((1, QBLK, 2 * LANES), lambda bb, i: (bb, i, 0))],
        out_shape=[jax.ShapeDtypeStruct((b, s, HQ), F32),
                   jax.ShapeDtypeStruct((b, s, 2 * LANES), BF16)],
        compiler_params=_cparams("parallel", "parallel"),
        name="nsa_cmp_attention",
    )(q, kcd, vcd, ov)


def _sel_attn_kernel(q_ref, pen_ref, k_ref, v_ref, bias_ref, o_ref, qst_ref, m_ref, acc_ref, *, nbt):
    qi = pl.program_id(2)
    hpc = N_HEADS // C_KV_HEADS
    nsub = SEL_CHUNK // QBLK
    lane = lax.broadcasted_iota(jnp.int32, (QBLK, LANES), 1)
    lo = lane < HEAD_DIM
    pen = pen_ref[0].astype(F32)
    for t in range(hpc):
        q2 = q_ref[0, :, (t // 2) * LANES:(t // 2 + 1) * LANES].astype(F32)
        if t % 2:
            q2 = pltpu.roll(q2, HEAD_DIM, 1)
        qst_ref[t * QBLK:(t + 1) * QBLK, :] = jnp.where(lo, q2, pen).astype(BF16)
    m_ref[...] = jnp.full(m_ref.shape, NEG_INF, F32)
    acc_ref[...] = jnp.zeros(acc_ref.shape, F32)

    def body(c, carry):
        k0 = pl.multiple_of(c * SEL_CHUNK, SEL_CHUNK)
        s = _dot_nt(qst_ref[...], k_ref[0, pl.ds(k0, SEL_CHUNK), :])
        tiles = []
        for u in range(nsub):
            off = qi - nsub * c - u
            idx = jnp.where(off < 0, nbt, jnp.minimum(off, nbt - 1))
            tiles.append(s[:, u * QBLK:(u + 1) * QBLK] + bias_ref[idx].reshape(hpc * QBLK, QBLK))
        m_old = m_ref[...]
        m_new = jnp.maximum(m_old, jnp.max(functools.reduce(jnp.maximum, tiles), axis=-1, keepdims=True))
        p = jnp.concatenate([jnp.exp(tl - m_new).astype(BF16) for tl in tiles], axis=1)
        pv = jnp.dot(p, v_ref[0, pl.ds(k0, SEL_CHUNK), :], preferred_element_type=F32)
        acc_ref[...] = jnp.exp(m_old - m_new) * acc_ref[...] + pv
        m_ref[...] = m_new
        return carry

    lax.fori_loop(0, qi // nsub + 1, body, 0)
    acc = acc_ref[...]
    rolled = pltpu.roll(acc, HEAD_DIM, 1)
    for t2 in range(hpc // 2):
        ev = slice(2 * t2 * QBLK, (2 * t2 + 1) * QBLK)
        od = slice((2 * t2 + 1) * QBLK, (2 * t2 + 2) * QBLK)
        even = acc[ev] * (1.0 / rolled[ev])
        odd = rolled[od] * (1.0 / acc[od])
        o_ref[0, :, t2 * LANES:(t2 + 1) * LANES] = jnp.where(lo, even, odd)


def sel_attention(q, kx, vx, pen, bias_tiles):
    b, s, _ = q.shape
    hpc = N_HEADS // C_KV_HEADS
    gw = hpc * HEAD_DIM
    nbt = bias_tiles.shape[0] - 1
    assert s % SEL_CHUNK == 0
    return pl.pallas_call(
        functools.partial(_sel_attn_kernel, nbt=nbt),
        grid=(b, C_KV_HEADS, s // QBLK),
        in_specs=[pl.BlockSpec((1, QBLK, gw), lambda bb, g, i: (bb, i, g)),
                  pl.BlockSpec((1, QBLK, LANES), lambda bb, g, i: (bb, i, g)),
                  pl.BlockSpec((1, s, LANES), lambda bb, g, i: (bb, 0, g)),
                  pl.BlockSpec((1, s, LANES), lambda bb, g, i: (bb, 0, g)),
                  pl.BlockSpec((nbt + 1, hpc, QBLK, QBLK), lambda bb, g, i: (0, g, 0, 0))],
        out_specs=pl.BlockSpec((1, QBLK, gw), lambda bb, g, i: (bb, i, g)),
        out_shape=jax.ShapeDtypeStruct((b, s, HQ), F32),
        scratch_shapes=[pltpu.VMEM((hpc * QBLK, LANES), BF16),
                        pltpu.VMEM((hpc * QBLK, LANES), F32),
                        pltpu.VMEM((hpc * QBLK, LANES), F32)],
        compiler_params=_cparams("parallel", "parallel", "arbitrary"),
        name="nsa_sel_attention",
    )(q, pen, kx, vx, bias_tiles)


def _sel_bias_idx(s):
    nqt = s // QBLK
    far = -(-(int(np.argmax(_t5_bucket_np(np.arange(4 * MAX_DISTANCE)) == NUM_BUCKETS - 1)) + QBLK) // QBLK)
    nbt = min(nqt, far + 1)
    d0 = np.arange(nbt)[:, None, None] * QBLK
    dist = d0 + np.arange(QBLK)[None, :, None] - np.arange(QBLK)[None, None, :]
    idx = np.where(dist >= 0, _t5_bucket_np(dist), -1).astype(np.int32)
    return np.concatenate([idx, np.full((1, QBLK, QBLK), -1, np.int32)])


def _ffn_kernel(h_ref, g2_ref, wu_ref, cw_ref, cb_ref, wd_ref, g3_ref,
                o_ref, xn_ref, acc_ref, ua_ref, ub_ref, aa_ref, ab_ref, carry_ref, *, tiles_per_seq, nj):
    i = pl.program_id(0)
    tm, tn = xn_ref.shape[0], wu_ref.shape[2]
    xn_ref[...] = _rms(h_ref[...], g2_ref[...]).astype(xn_ref.dtype)
    acc_ref[...] = jnp.zeros(acc_ref.shape, F32)

    @pl.when(i % tiles_per_seq == 0)
    def _():
        carry_ref[...] = jnp.zeros(carry_ref.shape, F32)

    top = 16
    rowt = lax.broadcasted_iota(jnp.int32, (top, tn), 0)

    def up(jj, u_ref):
        xn = xn_ref[...]
        u_ref[0] = jnp.dot(xn, wu_ref[jj], preferred_element_type=F32)
        u_ref[1] = jnp.dot(xn, wu_ref[nj + jj], preferred_element_type=F32)

    def conv(u, s1, s2, c):
        cw = cw_ref[c]
        return ((cb_ref[c] + u * cw[2:3]) + s2 * cw[0:1]) + s1 * cw[1:2]

    def conv_body(u, c):
        return conv(u, pltpu.roll(u, 1, 0), pltpu.roll(u, 2, 0), c)

    def conv_top(u_ref, c):
        prev = carry_ref[c]
        p1, p2 = prev[7:8], prev[6:7]
        u = u_ref[0:top]
        s1 = jnp.where(rowt == 0, p1, pltpu.roll(u, 1, 0))
        s2 = jnp.where(rowt == 0, p2, jnp.where(rowt == 1, p1, pltpu.roll(u, 2, 0)))
        carry_ref[c] = u_ref[tm - 8:tm]
        return conv(u, s1, s2, c)

    def gated(cg, cv):
        return (jax.nn.gelu(cg, approximate=True) * cv).astype(BF16)

    def act(j, u_ref, a_ref):
        a_ref[...] = gated(conv_body(u_ref[0], j), conv_body(u_ref[1], nj + j))
        a_ref[0:top] = gated(conv_top(u_ref.at[0], j), conv_top(u_ref.at[1], nj + j))

    def down(j, a_ref):
        acc_ref[...] += jnp.dot(a_ref[...], wd_ref[j], preferred_element_type=F32)

    up(0, ua_ref)
    up(1, ub_ref)
    act(0, ua_ref, aa_ref)

    def body(jj, carry):
        j = 2 * jj + 1
        up(j + 1, ua_ref)
        act(j, ub_ref, ab_ref)
        down(j - 1, aa_ref)
        up(j + 2, ub_ref)
        act(j + 1, ua_ref, aa_ref)
        down(j, ab_ref)
        return carry

    lax.fori_loop(0, (nj - 3) // 2, body, 0)
    up(nj - 1, ua_ref)
    act(nj - 2, ub_ref, ab_ref)
    down(nj - 3, aa_ref)
    act(nj - 1, ua_ref, aa_ref)
    down(nj - 2, ab_ref)
    down(nj - 1, aa_ref)
    o_ref[...] = h_ref[...] + _rms(acc_ref[...], g3_ref[...])


def conv_ffn(h, g2, w_up, conv_w, conv_b, w_down, g3, seq, tm, tn):
    t, d = h.shape
    dff = w_down.shape[0]
    nj = dff // tn
    assert nj * tn == dff and nj % 2 == 1 and nj >= 3
    wu = w_up.reshape(d, 2 * nj, tn).transpose(1, 0, 2)
    cw = conv_w.reshape(CONV_WIDTH, 2 * nj, tn).transpose(1, 0, 2)
    cb = conv_b.reshape(2 * nj, 1, tn)
    wd = w_down.reshape(nj, tn, d)
    resident = lambda a: pl.BlockSpec(a.shape, (lambda nd: (lambda i: (0,) * nd))(a.ndim))
    return pl.pallas_call(
        functools.partial(_ffn_kernel, tiles_per_seq=seq // tm, nj=nj),
        grid=(t // tm,),
        in_specs=[pl.BlockSpec((tm, d), lambda i: (i, 0)),
                  pl.BlockSpec((1, d), lambda i: (0, 0)),
                  resident(wu), resident(cw), resident(cb), resident(wd),
                  pl.BlockSpec((1, d), lambda i: (0, 0))],
        out_specs=pl.BlockSpec((tm, d), lambda i: (i, 0)),
        out_shape=jax.ShapeDtypeStruct((t, d), F32),
        scratch_shapes=[pltpu.VMEM((tm, d), BF16), pltpu.VMEM((tm, d), F32),
                        pltpu.VMEM((2, tm, tn), F32), pltpu.VMEM((2, tm, tn), F32),
                        pltpu.VMEM((tm, tn), BF16), pltpu.VMEM((tm, tn), BF16),
                        pltpu.VMEM((2 * nj, 8, tn), F32)],
        compiler_params=_cparams("arbitrary"),
        name="conv_ffn",
    )(h, g2.reshape(1, d), wu, cw, cb, wd, g3.reshape(1, d))


def _row_tile(t):
    for tm in (1024, 512, 256, 128):
        if t % tm == 0:
            return tm
    raise ValueError(f"token count {t} is not a multiple of 128")


def mixer_a(h, gains, w_in, w_o, rel_table, bsz, seq):
    t, d = h.shape
    tm = _row_tile(t)
    n_dil = len(DIL_CONFIGS)
    a_in = w_in.shape[1]
    col_scale = np.ones((a_in,), np.float32).reshape(n_dil, 3, HQ)
    col_scale[:, 0] = ATTN_SCALE
    w = (w_in * col_scale.reshape(1, a_in)).astype(BF16)
    idx = np.stack([_banded_idx(1, window // dil, dil) for window, dil in DIL_CONFIGS])
    bias = bias_expand(rel_table, idx)
    outs, lses, dils = [], [], []
    blocks_per_seq = seq // QBLK
    for gi, (window, dil) in enumerate(DIL_CONFIGS):
        assert window // dil <= QBLK and seq % (QBLK * dil) == 0
        tmg = max(tm, QBLK * dil)
        qkv = norm_matmul(h, gains[0], w[:, gi * 3 * HQ:(gi + 1) * 3 * HQ], BF16, tmg, HQ, dil=dil)
        qkv = qkv.reshape(1, t, 3 * HQ)

        def rmap(part, dil=dil):
            return lambda n, i: (0, (n // dil) * blocks_per_seq + i * dil + n % dil, part)

        o, lse = banded_attention(
            qkv, qkv, qkv, bias[gi], n_rows=bsz * dil, n_tiles=seq // (QBLK * dil), tq=QBLK, n_prev=1, kw=HQ,
            q_map=rmap(0), k_map=rmap(1), v_map=rmap(2), o_map=rmap(0),
            out_shape=(1, t, HQ), lse_shape=(1, t, LANES), name=f"dilated_attention_{dil}")
        outs.append(o.reshape(t, HQ))
        lses.append(lse.reshape(t, LANES))
        dils.append(dil)
    e = jnp.asarray(_head_expand_matrix(0), BF16)
    return combine_proj_a(outs, lses, dils, e, w_o.astype(BF16), gains[1], h, min(tm, 512))


def mixer_b(h, gains, w_in, sinks, w_o, rel_table, bsz, seq):
    t, d = h.shape
    tm = _row_tile(t)
    hk = B_KV_HEADS * HEAD_DIM
    n_in = w_in.shape[1]
    col_scale = np.ones((n_in,), np.float32)
    col_scale[:HQ] = ATTN_SCALE
    w = (w_in * col_scale[None]).astype(BF16)
    qkv = norm_matmul(h, gains[0], w, BF16, tm, n_in // 2 if (n_in // 2) % LANES == 0 else n_in)
    q = qkv[:, :HQ].reshape(bsz, seq, HQ)
    kd = _dup_groups(qkv[:, HQ:HQ + hk].reshape(bsz, seq, hk))
    vd = _dup_groups(qkv[:, HQ + hk:].reshape(bsz, seq, hk))
    bias = bias_expand(rel_table, _banded_idx(1, B_WINDOW - 1, 1)[None])[0]
    sink_rows = jnp.repeat(sinks.astype(F32), QBLK).reshape(N_HEADS * QBLK, 1)
    ident = lambda n, i: (n, i, 0)
    (o,) = banded_attention(q, kd, vd, bias, n_rows=bsz, n_tiles=seq // QBLK, tq=QBLK, n_prev=1,
                            kw=2 * LANES, q_map=ident, k_map=ident, v_map=ident, o_map=ident,
                            out_shape=(bsz, seq, HQ), sinks=sink_rows, name="sink_window_attention")
    return _proj_call(_proj_kernel, "sink_proj", tm, [o.reshape(t, HQ)], [], w_o.astype(BF16), gains[1], h)


def mixer_c(h, gains, w_in, cmp_pos, cmp_w1, cmp_w2, w_o, rel_table, bsz, seq):
    t, d = h.shape
    tm = _row_tile(t)
    g = C_KV_HEADS
    hk = g * HEAD_DIM
    rest = w_in.shape[1] - HQ
    rest_pad = -(-rest // LANES) * LANES
    wq = (w_in[:, :HQ] * ATTN_SCALE).astype(BF16)
    wr = jnp.pad(w_in[:, HQ:], ((0, 0), (0, rest_pad - rest))).astype(BF16)
    q = norm_matmul(h, gains[0], wq, BF16, tm, HQ).reshape(bsz, seq, HQ)
    r = norm_matmul(h, gains[0], wr, F32, tm, rest_pad)
    kv = [r[:, i * hk:(i + 1) * hk].reshape(bsz, seq, hk) for i in range(6)]
    gates_raw = r[:, 6 * hk:6 * hk + LANES]

    ncr = seq // CMP_STRIDE
    half = CMP_BLOCK // 2
    assert half == CMP_STRIDE
    chunks = jnp.stack([kv[0], kv[1]]).reshape(2, bsz, ncr, half, g, HEAD_DIM)
    chunks = chunks.transpose(0, 1, 4, 2, 3, 5).reshape(2, bsz, g, ncr, half * HEAD_DIM)
    pos = cmp_pos.reshape(2, 2, half * HEAD_DIM)
    w1 = cmp_w1.reshape(2, 2, half * HEAD_DIM, -1).astype(BF16)
    cmp = compress(chunks, pos, w1, cmp_w2.astype(BF16))
    cmp = cmp.transpose(0, 1, 3, 2, 4).reshape(2, bsz, ncr, hk).astype(BF16)
    kcd, vcd = _dup_groups(cmp[0]), _dup_groups(cmp[1])

    ns = seq // SEL_BLOCK
    assert ns <= HEAD_DIM
    k_sel = min(N_SELECT, ns)
    starts = np.arange(ncr) * CMP_STRIDE
    blk = np.arange(LANES) - HEAD_DIM
    ov = ((starts[:, None] < (blk[None, :] + 1) * SEL_BLOCK)
          & (starts[:, None] + CMP_BLOCK > blk[None, :] * SEL_BLOCK)
          & (blk[None, :] >= 0) & (blk[None, :] < ns) & (starts[:, None] + CMP_BLOCK <= seq))
    o_c, pen = cmp_attention(q, kcd, vcd, jnp.asarray(ov.astype(np.float32), BF16), ns, k_sel)

    sel_bias = bias_expand(rel_table, _sel_bias_idx(seq))
    key_blk = np.arange(seq)[:, None] // SEL_BLOCK == np.arange(HEAD_DIM)[None, :]
    onehot = jnp.broadcast_to(jnp.asarray(np.where(key_blk, -NEG_INF, 0.0), BF16), (bsz, seq, HEAD_DIM))
    ones = jnp.ones((bsz, seq, HEAD_DIM), BF16)
    ks, vs = kv[2].astype(BF16), kv[3].astype(BF16)
    kx = jnp.concatenate([ks[..., :HEAD_DIM], onehot, ks[..., HEAD_DIM:], onehot], axis=-1)
    vx = jnp.concatenate([vs[..., :HEAD_DIM], ones, vs[..., HEAD_DIM:], ones], axis=-1)
    o_s = sel_attention(q, kx, vx, pen, sel_bias)

    n_prev = -(-(C_WINDOW - 1) // QBLK)
    tqw = n_prev * QBLK
    wbias = bias_expand(rel_table, _banded_idx(n_prev, C_WINDOW - 1, 1)[None])[0]
    ident = lambda n, i: (n, i, 0)
    (o_w,) = banded_attention(q, _dup_groups(kv[4].astype(BF16)), _dup_groups(kv[5].astype(BF16)), wbias,
                              n_rows=bsz, n_tiles=seq // tqw, tq=tqw, n_prev=n_prev, kw=2 * LANES,
                              q_map=ident, k_map=ident, v_map=ident, o_map=ident,
                              out_shape=(bsz, seq, HQ), name="nsa_window_attention")

    e = jnp.asarray(np.stack([_head_expand_matrix(i * N_HEADS) for i in range(3)]), BF16)
    return _proj_call(_proj_c_kernel, "nsa_gate_proj", min(tm, 512),
                      [o_c.reshape(t, HQ), o_s.reshape(t, HQ), o_w.reshape(t, HQ), gates_raw], [e],
                      w_o.astype(BF16), gains[1], h)


def kernel(x, rel_table, norm_gains, a_w_in, a_w_o, b_w_in, b_sinks, b_w_o, c_w_in, c_cmp_pos, c_cmp_w1,
           c_cmp_w2, c_w_o, ffn_w_up, ffn_conv_w, ffn_conv_b, ffn_w_down):
    bsz, seq, d = x.shape
    depth = norm_gains.shape[0]
    h = x.reshape(bsz * seq, d)
    tm = _row_tile(seq)
    for i in range(depth):
        kind, j = i % 3, i // 3
        g = norm_gains[i]
        if kind == 0:
            h = mixer_a(h, g, a_w_in[j], a_w_o[j], rel_table, bsz, seq)
        elif kind == 1:
            h = mixer_b(h, g, b_w_in[j], b_sinks[j], b_w_o[j], rel_table, bsz, seq)
        else:
            h = mixer_c(h, g, c_w_in[j], c_cmp_pos[j], c_cmp_w1[j], c_cmp_w2[j], c_w_o[j], rel_table, bsz, seq)
        h = conv_ffn(h, g[2], ffn_w_up[i].astype(BF16), ffn_conv_w[i], ffn_conv_b[i],
                     ffn_w_down[i].astype(BF16), g[3], seq, min(tm, 512), 256)
    return h.reshape(bsz, seq, d)
```

```python
import functools
import math

import numpy as np
import jax
import jax.numpy as jnp
from jax import lax
from jax.experimental import pallas as pl
from jax.experimental.pallas import tpu as pltpu

F32 = jnp.float32
BF16 = jnp.bfloat16

N_HEADS = 16
HEAD_DIM = 64
HQ = N_HEADS * HEAD_DIM
LANES = 128
ATTN_SCALE = HEAD_DIM ** -0.5
NUM_BUCKETS = 32
MAX_DISTANCE = 2048
RMS_EPS = 1e-6
NEG_INF = -1e30
FORCE_SCORE = 1e9
DIL_CONFIGS = ((128, 1), (512, 4), (2048, 16))
B_KV_HEADS = 2
B_WINDOW = 128
C_KV_HEADS = 2
CMP_BLOCK = 32
CMP_STRIDE = 16
SEL_BLOCK = 64
N_SELECT = 16
C_WINDOW = 512
CONV_WIDTH = 3
QBLK = 128
SEL_CHUNK = 512
VMEM_LIMIT = 56 * 1024 * 1024


def _cparams(*sem):
    return pltpu.CompilerParams(dimension_semantics=sem, vmem_limit_bytes=VMEM_LIMIT)


def _t5_bucket_np(dist):
    max_exact = NUM_BUCKETS // 2
    d = np.maximum(dist, 0)
    df = np.maximum(d, 1).astype(np.float64)
    large = max_exact + np.floor(np.log(df / max_exact) / math.log(MAX_DISTANCE / max_exact)
                                 * (NUM_BUCKETS - max_exact) + 1e-9).astype(np.int64)
    large = np.minimum(large, NUM_BUCKETS - 1)
    return np.where(d < max_exact, d, large).astype(np.int32)


def _rms(x, g):
    ms = jnp.mean(x * x, axis=-1, keepdims=True)
    return (x * lax.rsqrt(ms + RMS_EPS)) * g


def _split3(w):
    hi = w.astype(BF16)
    r1 = w - hi.astype(F32)
    mid = r1.astype(BF16)
    lo = (r1 - mid.astype(F32)).astype(BF16)
    return hi, mid, lo


def _dot3(w, e):
    hi, mid, lo = _split3(w)
    d = functools.partial(jnp.dot, preferred_element_type=F32)
    return (d(hi, e) + d(mid, e)) + d(lo, e)


def _dot_nt(a, b):
    return lax.dot_general(a, b, (((1,), (1,)), ((), ())), preferred_element_type=F32)


def _stack_heads(q_tile, heads):
    lane = lax.broadcasted_iota(jnp.int32, (QBLK, LANES), 1)
    lo = lane < HEAD_DIM
    pieces = []
    for h in heads:
        q2 = q_tile(h // 2)
        keep = lo if h % 2 == 0 else jnp.logical_not(lo)
        pieces.append(jnp.where(keep, q2, jnp.zeros_like(q2)))
    return jnp.concatenate(pieces, axis=0)


def _merge_pairs(o, n_heads):
    lane = lax.broadcasted_iota(jnp.int32, (QBLK, LANES), 1)
    lo = lane < HEAD_DIM
    out = []
    for t in range(0, n_heads, 2):
        out.append(jnp.where(lo, o[t * QBLK:(t + 1) * QBLK], o[(t + 1) * QBLK:(t + 2) * QBLK]))
    return out


def _bias_kernel(tab_ref, idx_ref, o_ref):
    h = pl.program_id(1)
    idx = idx_ref[0]
    acc = jnp.full(idx.shape, NEG_INF, F32)
    for b in range(NUM_BUCKETS):
        acc = jnp.where(idx == b, tab_ref[b, h], acc)
    o_ref[0, 0] = acc


def bias_expand(table, idx):
    n, r, c = idx.shape
    return pl.pallas_call(
        _bias_kernel,
        grid=(n, N_HEADS),
        in_specs=[pl.BlockSpec(memory_space=pltpu.SMEM),
                  pl.BlockSpec((1, r, c), lambda i, h: (i, 0, 0))],
        out_specs=pl.BlockSpec((1, 1, r, c), lambda i, h: (i, h, 0, 0)),
        out_shape=jax.ShapeDtypeStruct((n, N_HEADS, r, c), F32),
        compiler_params=_cparams("parallel", "parallel"),
        name="bias_expand",
    )(table, jnp.asarray(idx))


def _banded_idx(n_prev, max_dist, stride):
    kb = (n_prev + 1) * QBLK
    dist = np.arange(QBLK)[:, None] + n_prev * QBLK - np.arange(kb)[None, :]
    valid = (dist >= 0) & (dist <= max_dist)
    return np.where(valid, _t5_bucket_np(dist * stride), -1).astype(np.int32)


def _norm_mm_kernel(x_ref, g_ref, w_ref, o_ref, xn_ref, *chunk_ref, dil):
    @pl.when(pl.program_id(1) == 0)
    def _():
        if dil == 1:
            xn_ref[...] = _rms(x_ref[...], g_ref[...]).astype(xn_ref.dtype)
        else:
            (xs_ref,) = chunk_ref
            nc = xs_ref.shape[0]
            for c in range(nc):
                xs_ref[c] = x_ref[:, c * LANES:(c + 1) * LANES]
            span = QBLK * dil
            for u in range(x_ref.shape[0] // span):
                for r in range(dil):
                    rows = jnp.concatenate(
                        [xs_ref[c, pl.ds(u * span + r, QBLK, stride=dil), :] for c in range(nc)], axis=1)
                    dst = u * span + r * QBLK
                    xn_ref[dst:dst + QBLK, :] = _rms(rows, g_ref[...]).astype(xn_ref.dtype)

    o_ref[...] = jnp.dot(xn_ref[...], w_ref[...], preferred_element_type=F32).astype(o_ref.dtype)


def norm_matmul(x, gain, w, out_dtype, tm, tn, dil=1):
    t, d = x.shape
    n = w.shape[1]
    assert tm % (QBLK * dil) == 0
    scratch = [pltpu.VMEM((tm, d), BF16)]
    if dil > 1:
        scratch.append(pltpu.VMEM((d // LANES, tm, LANES), F32))
    return pl.pallas_call(
        functools.partial(_norm_mm_kernel, dil=dil),
        grid=(t // tm, n // tn),
        in_specs=[pl.BlockSpec((tm, d), lambda i, j: (i, 0)),
                  pl.BlockSpec((1, d), lambda i, j: (0, 0)),
                  pl.BlockSpec((d, tn), lambda i, j: (0, j))],
        out_specs=pl.BlockSpec((tm, tn), lambda i, j: (i, j)),
        out_shape=jax.ShapeDtypeStruct((t, n), out_dtype),
        scratch_shapes=scratch,
        compiler_params=_cparams("parallel", "arbitrary"),
        name="norm_matmul",
    )(x, gain.reshape(1, d), w)


def _finish_proj(y, w_ref, g_ref, h_ref, o_ref):
    z = jnp.dot(y.astype(BF16), w_ref[...], preferred_element_type=F32)
    o_ref[...] = h_ref[...] + _rms(z, g_ref[...])


def _proj_kernel(y_ref, w_ref, g_ref, h_ref, o_ref):
    _finish_proj(y_ref[...], w_ref, g_ref, h_ref, o_ref)


def _unpermute(src_ref, dst_ref, dil, tm, i):
    span = QBLK * dil
    nc = dst_ref.shape[0]
    if span <= tm:
        for u in range(tm // span):
            for r in range(dil):
                lo = u * span + r * QBLK
                for c in range(nc):
                    dst_ref[c, pl.ds(u * span + r, QBLK, stride=dil), :] = src_ref[lo:lo + QBLK,
                                                                                   c * LANES:(c + 1) * LANES]
    else:
        per = tm // dil
        off = (i % (span // tm)) * per
        for r in range(dil):
            lo = pl.multiple_of(r * QBLK + off, 8)
            for c in range(nc):
                dst_ref[c, pl.ds(r, per, stride=dil), :] = src_ref[pl.ds(lo, per), c * LANES:(c + 1) * LANES]
    return jnp.concatenate([dst_ref[c] for c in range(nc)], axis=1) if nc > 1 else dst_ref[0]


def _proj_a_kernel(*refs, dils, tm):
    n = len(dils)
    o_refs, l_refs = refs[:n], refs[n:2 * n]
    e_ref, w_ref, g_ref, h_ref, out_ref = refs[2 * n:2 * n + 5]
    scratch = refs[2 * n + 5:]
    i = pl.program_id(0)
    outs, lses = [], []
    si = 0
    for gi, dil in enumerate(dils):
        if dil == 1:
            outs.append(o_refs[gi][...])
            lses.append(l_refs[gi][...])
        else:
            outs.append(_unpermute(o_refs[gi], scratch[si], dil, tm, i))
            lses.append(_unpermute(l_refs[gi], scratch[si + 1], dil, tm, i))
            si += 2
    mx = functools.reduce(jnp.maximum, lses)
    es = [jnp.exp(l - mx) for l in lses]
    inv = 1.0 / functools.reduce(lambda a, b: a + b, es)
    e = e_ref[...]
    y = outs[0] * _dot3(es[0] * inv, e)
    for gi in range(1, n):
        y = y + outs[gi] * _dot3(es[gi] * inv, e)
    _finish_proj(y, w_ref, g_ref, h_ref, out_ref)


def combine_proj_a(outs, lses, dils, e, w, gain, h, tm):
    t, d = h.shape

    def row_spec(width, dil):
        rows = max(tm, QBLK * dil)
        return pl.BlockSpec((rows, width), lambda i, q=rows // tm: (i // q, 0))

    in_specs = [row_spec(HQ, dil) for dil in dils] + [row_spec(LANES, dil) for dil in dils] + [
        pl.BlockSpec(e.shape, lambda i: (0, 0)),
        pl.BlockSpec(w.shape, lambda i: (0, 0)),
        pl.BlockSpec((1, d), lambda i: (0, 0)),
        pl.BlockSpec((tm, d), lambda i: (i, 0))]
    scratch = []
    for dil in dils:
        if dil > 1:
            scratch += [pltpu.VMEM((HQ // LANES, tm, LANES), F32), pltpu.VMEM((1, tm, LANES), F32)]
    return pl.pallas_call(
        functools.partial(_proj_a_kernel, dils=tuple(dils), tm=tm),
        grid=(t // tm,),
        in_specs=in_specs,
        out_specs=pl.BlockSpec((tm, d), lambda i: (i, 0)),
        out_shape=jax.ShapeDtypeStruct((t, d), F32),
        scratch_shapes=scratch,
        compiler_params=_cparams("arbitrary"),
        name="dilated_combine_proj",
    )(*outs, *lses, e, w, gain.reshape(1, d), h)


def _proj_c_kernel(oc_ref, os_ref, ow_ref, gr_ref, e_ref, w_ref, g_ref, h_ref, o_ref):
    sig = jax.nn.sigmoid(gr_ref[...])
    y = _dot3(sig, e_ref[0]) * oc_ref[...]
    y = y + _dot3(sig, e_ref[1]) * os_ref[...]
    y = y + _dot3(sig, e_ref[2]) * ow_ref[...]
    _finish_proj(y, w_ref, g_ref, h_ref, o_ref)


def _proj_call(kernel, name, tm, row_inputs, const_inputs, w, gain, h):
    t, d = h.shape
    k = w.shape[0]
    row_specs = [pl.BlockSpec((tm, a.shape[1]), lambda i: (i, 0)) for a in row_inputs]
    const_specs = [pl.BlockSpec(a.shape, (lambda nd: (lambda i: (0,) * nd))(a.ndim)) for a in const_inputs]
    return pl.pallas_call(
        kernel,
        grid=(t // tm,),
        in_specs=row_specs + const_specs + [
            pl.BlockSpec((k, d), lambda i: (0, 0)),
            pl.BlockSpec((1, d), lambda i: (0, 0)),
            pl.BlockSpec((tm, d), lambda i: (i, 0))],
        out_specs=pl.BlockSpec((tm, d), lambda i: (i, 0)),
        out_shape=jax.ShapeDtypeStruct((t, d), F32),
        compiler_params=_cparams("parallel"),
        name=name,
    )(*row_inputs, *const_inputs, w, gain.reshape(1, d), h)


def _head_expand_matrix(row_offset):
    e = np.zeros((LANES, HQ), np.float32)
    for h in range(N_HEADS):
        e[row_offset + h, h * HEAD_DIM:(h + 1) * HEAD_DIM] = 1.0
    return e


def _banded_kernel(*refs, tq, n_prev, n_chunks, with_sinks, with_lse):
    q_ref, kp_ref, kc_ref, vp_ref, vc_ref, bias_ref = refs[:6]
    pos = 6
    sink_ref = None
    if with_sinks:
        sink_ref = refs[pos]
        pos += 1
    o_ref = refs[pos]
    pos += 1
    lse_ref = None
    if with_lse:
        lse_ref = refs[pos]
        pos += 1
    kbuf, vbuf = refs[pos], refs[pos + 1]

    i = pl.program_id(1)
    hpc = N_HEADS // n_chunks
    kb = (n_prev + 1) * QBLK
    kbuf[0:tq] = kp_ref[0]
    kbuf[tq:2 * tq] = kc_ref[0]
    vbuf[0:tq] = vp_ref[0]
    vbuf[tq:2 * tq] = vc_ref[0]
    lane = lax.broadcasted_iota(jnp.int32, (QBLK, LANES), 1)
    first_head = lax.broadcasted_iota(jnp.int32, (2 * QBLK, 1), 0) < QBLK

    for s in range(tq // QBLK):
        r0 = s * QBLK
        k0 = tq + r0 - n_prev * QBLK
        col = lax.broadcasted_iota(jnp.int32, (1, kb), 1) + k0
        kmask = jnp.where(jnp.logical_and(i == 0, col < tq), NEG_INF, 0.0).astype(F32)
        lse_acc = jnp.zeros((QBLK, LANES), F32)
        for pair in range(N_HEADS // 2):
            heads = [2 * pair, 2 * pair + 1]
            c = (2 * pair) // hpc
            qst = _stack_heads(lambda p: q_ref[0, r0:r0 + QBLK, p * LANES:(p + 1) * LANES], heads)
            kx = kbuf[k0:k0 + kb, c * LANES:(c + 1) * LANES]
            vx = vbuf[k0:k0 + kb, c * LANES:(c + 1) * LANES]
            sc = _dot_nt(qst, kx)
            sc = sc + bias_ref[2 * pair:2 * pair + 2].reshape(2 * QBLK, kb) + kmask
            m = jnp.max(sc, axis=-1, keepdims=True)
            if with_sinks:
                sk = jnp.where(first_head, sink_ref[2 * pair], sink_ref[2 * pair + 1])
                m = jnp.maximum(m, sk)
            p = jnp.exp(sc - m)
            den = jnp.sum(p, axis=-1, keepdims=True)
            norm = den + jnp.exp(sk - m) if with_sinks else den
            o = jnp.dot(p.astype(BF16), vx, preferred_element_type=F32) * (1.0 / norm)
            (blk,) = _merge_pairs(o, 2)
            o_ref[0, r0:r0 + QBLK, pair * LANES:(pair + 1) * LANES] = blk.astype(o_ref.dtype)
            if with_lse:
                lse = m + jnp.log(den)
                for t, h in enumerate(heads):
                    lse_acc = jnp.where(lane == h, lse[t * QBLK:(t + 1) * QBLK], lse_acc)
        if with_lse:
            lse_ref[0, r0:r0 + QBLK, :] = lse_acc


def banded_attention(q_arr, k_arr, v_arr, bias, *, n_rows, n_tiles, tq, n_prev, kw,
                     q_map, k_map, v_map, o_map, out_shape, lse_shape=None, sinks=None, name):
    n_chunks = kw // LANES
    kb = (n_prev + 1) * QBLK
    with_sinks = sinks is not None
    with_lse = lse_shape is not None

    def prev(fn):
        return lambda n, i: fn(n, jnp.maximum(i - 1, 0))

    in_specs = [pl.BlockSpec((1, tq, HQ), q_map),
                pl.BlockSpec((1, tq, kw), prev(k_map)),
                pl.BlockSpec((1, tq, kw), k_map),
                pl.BlockSpec((1, tq, kw), prev(v_map)),
                pl.BlockSpec((1, tq, kw), v_map),
                pl.BlockSpec((N_HEADS, QBLK, kb), lambda n, i: (0, 0, 0))]
    args = [q_arr, k_arr, k_arr, v_arr, v_arr, bias]
    if with_sinks:
        in_specs.append(pl.BlockSpec(memory_space=pltpu.SMEM))
        args.append(sinks)
    out_specs = [pl.BlockSpec((1, tq, HQ), o_map)]
    out_shapes = [jax.ShapeDtypeStruct(out_shape, F32)]
    if with_lse:
        out_specs.append(pl.BlockSpec((1, tq, LANES), o_map))
        out_shapes.append(jax.ShapeDtypeStruct(lse_shape, F32))
    res = pl.pallas_call(
        functools.partial(_banded_kernel, tq=tq, n_prev=n_prev, n_chunks=n_chunks,
                          with_sinks=with_sinks, with_lse=with_lse),
        grid=(n_rows, n_tiles),
        in_specs=in_specs,
        out_specs=out_specs,
        out_shape=out_shapes,
        scratch_shapes=[pltpu.VMEM((2 * tq, kw), BF16), pltpu.VMEM((2 * tq, kw), BF16)],
        compiler_params=_cparams("parallel", "arbitrary"),
        name=name,
    )(*args)
    return res


def _dup_groups(x):
    g0, g1 = x[..., :HEAD_DIM], x[..., HEAD_DIM:]
    return jnp.concatenate([g0, g0, g1, g1], axis=-1)


def _compress_kernel(ch_ref, pos_ref, w1_ref, w2_ref, o_ref):
    ch = ch_ref[0, 0, 0]
    rows = ch.shape[0]
    posv = pos_ref[0]
    a = jnp.dot((ch + posv[0:1]).astype(BF16), w1_ref[0, 0], preferred_element_type=F32)
    b = jnp.dot((ch + posv[1:2]).astype(BF16), w1_ref[0, 1], preferred_element_type=F32)
    hid = a + pltpu.roll(b, rows - 1, 0)
    act = jax.nn.gelu(hid, approximate=True)
    o_ref[0, 0, 0] = jnp.dot(act.astype(BF16), w2_ref[0], preferred_element_type=F32)


def compress(chunks, pos, w1, w2):
    _, b, g, rows, width = chunks.shape
    hid = w1.shape[-1]
    return pl.pallas_call(
        _compress_kernel,
        grid=(2, b, g),
        in_specs=[pl.BlockSpec((1, 1, 1, rows, width), lambda i, bb, gg: (i, bb, gg, 0, 0)),
                  pl.BlockSpec((1, 2, width), lambda i, bb, gg: (i, 0, 0)),
                  pl.BlockSpec((1, 2, width, hid), lambda i, bb, gg: (i, 0, 0, 0)),
                  pl.BlockSpec((1, hid, HEAD_DIM), lambda i, bb, gg: (i, 0, 0))],
        out_specs=pl.BlockSpec((1, 1, 1, rows, HEAD_DIM), lambda i, bb, gg: (i, bb, gg, 0, 0)),
        out_shape=jax.ShapeDtypeStruct((2, b, g, rows, HEAD_DIM), F32),
        compiler_params=_cparams("parallel", "parallel", "parallel"),
        name="nsa_compress",
    )(chunks, pos, w1, w2)


def _cmp_attn_kernel(q_ref, kc_ref, vc_ref, ov_ref, o_ref, sel_ref, *, n_sel_blocks, k_sel):
    qi = pl.program_id(1)
    ncr = kc_ref.shape[1]
    hpc = N_HEADS // C_KV_HEADS
    qpos = qi * QBLK + lax.broadcasted_iota(jnp.int32, (QBLK, 1), 0)
    cidx = lax.broadcasted_iota(jnp.int32, (1, ncr), 1)
    valid = (cidx * CMP_STRIDE + (CMP_BLOCK - 1)) <= qpos
    maskc = jnp.where(valid, 0.0, NEG_INF).astype(F32)
    anyv = (qpos >= CMP_BLOCK - 1).astype(F32)
    lane = lax.broadcasted_iota(jnp.int32, (QBLK, LANES), 1)
    blk_id = lane - HEAD_DIM
    cur = qpos // SEL_BLOCK
    forced = jnp.logical_or(jnp.logical_or(blk_id == 0, blk_id == cur), blk_id == cur - 1)
    allowed = blk_id <= cur
    for g in range(C_KV_HEADS):
        heads = list(range(g * hpc, (g + 1) * hpc))
        qst = _stack_heads(lambda p: q_ref[0, :, p * LANES:(p + 1) * LANES], heads)
        sc = _dot_nt(qst, kc_ref[0, :, g * LANES:(g + 1) * LANES])
        sc3 = sc.reshape(hpc, QBLK, ncr) + maskc[None]
        m = jnp.max(sc3, axis=-1, keepdims=True)
        e = jnp.exp(sc3 - m)
        ssum = jnp.sum(e, axis=-1, keepdims=True)
        p = (e * (1.0 / ssum)) * anyv[None]
        o = jnp.dot(p.reshape(hpc * QBLK, ncr).astype(BF16), vc_ref[0, :, g * LANES:(g + 1) * LANES],
                    preferred_element_type=F32)
        for t2, blk in enumerate(_merge_pairs(o, hpc)):
            pidx = (g * hpc) // 2 + t2
            o_ref[0, :, pidx * LANES:(pidx + 1) * LANES] = blk
        imp = _dot3(jnp.sum(p, axis=0), ov_ref[...])
        score = jnp.where(forced, FORCE_SCORE, jnp.where(allowed, imp, NEG_INF))
        rank = jnp.zeros((QBLK, LANES), F32)
        for i in range(n_sel_blocks):
            colv = score[:, HEAD_DIM + i:HEAD_DIM + i + 1]
            ge = jnp.where(colv >= score, 1.0, 0.0)
            gt = jnp.where(colv > score, 1.0, 0.0)
            rank = rank + jnp.where(blk_id > i, ge, gt)
        keep = jnp.logical_and(rank < k_sel, jnp.logical_and(blk_id >= 0, blk_id < n_sel_blocks))
        sel_ref[0, :, g * LANES:(g + 1) * LANES] = jnp.where(keep, 0.0, -1.0).astype(sel_ref.dtype)


def cmp_attention(q, kcd, vcd, ov, n_sel_blocks, k_sel):
    b, s, _ = q.shape
    ncr = kcd.shape[1]
    return pl.pallas_call(
        functools.partial(_cmp_attn_kernel, n_sel_blocks=n_sel_blocks, k_sel=k_sel),
        grid=(b, s // QBLK),
        in_specs=[pl.BlockSpec((1, QBLK, HQ), lambda bb, i: (bb, i, 0)),
                  pl.BlockSpec((1, ncr, 2 * LANES), lambda bb, i: (bb, 0, 0)),
                  pl.BlockSpec((1, ncr, 2 * LANES), lambda bb, i: (bb, 0, 0)),
                  pl.BlockSpec((ncr, LANES), lambda bb, i: (0, 0))],
        out_specs=[pl.BlockSpec((1, QBLK, HQ), lambda bb, i: (bb, i, 0)),
                   pl.BlockSpec((1, QBLK, 2 * LANES), lambda bb, i: (bb, i, 0))],
        out_shape=[jax.ShapeDtypeStruct((b, s, HQ), F32),
                   jax.ShapeDtypeStruct((b, s, 2 * LANES), BF16)],
        compiler_params=_cparams("parallel", "parallel"),
        name="nsa_cmp_attention",
    )(q, kcd, vcd, ov)


def _sel_attn_kernel(q_ref, pen_ref, k_ref, v_ref, bias_ref, o_ref, qst_ref, m_ref, acc_ref, *, nbt):
    qi = pl.program_id(2)
    hpc = N_HEADS // C_KV_HEADS
    nsub = SEL_CHUNK // QBLK
    lane = lax.broadcasted_iota(jnp.int32, (QBLK, LANES), 1)
    lo = lane < HEAD_DIM
    pen = pen_ref[0].astype(F32)
    for t in range(hpc):
        q2 = q_ref[0, :, (t // 2) * LANES:(t // 2 + 1) * LANES].astype(F32)
        if t % 2:
            q2 = pltpu.roll(q2, HEAD_DIM, 1)
        qst_ref[t * QBLK:(t + 1) * QBLK, :] = jnp.where(lo, q2, pen).astype(BF16)
    m_ref[...] = jnp.full(m_ref.shape, NEG_INF, F32)
    acc_ref[...] = jnp.zeros(acc_ref.shape, F32)

    def body(c, carry):
        k0 = pl.multiple_of(c * SEL_CHUNK, SEL_CHUNK)
        s = _dot_nt(qst_ref[...], k_ref[0, pl.ds(k0, SEL_CHUNK), :])
        tiles = []
        for u in range(nsub):
            off = qi - nsub * c - u
            idx = jnp.where(off < 0, nbt, jnp.minimum(off, nbt - 1))
            tiles.append(s[:, u * QBLK:(u + 1) * QBLK] + bias_ref[idx].reshape(hpc * QBLK, QBLK))
        m_old = m_ref[...]
        m_new = jnp.maximum(m_old, jnp.max(functools.reduce(jnp.maximum, tiles), axis=-1, keepdims=True))
        p = jnp.concatenate([jnp.exp(tl - m_new).astype(BF16) for tl in tiles], axis=1)
        pv = jnp.dot(p, v_ref[0, pl.ds(k0, SEL_CHUNK), :], preferred_element_type=F32)
        acc_ref[...] = jnp.exp(m_old - m_new) * acc_ref[...] + pv
        m_ref[...] = m_new
        return carry

    lax.fori_loop(0, qi // nsub + 1, body, 0)
    acc = acc_ref[...]
    rolled = pltpu.roll(acc, HEAD_DIM, 1)
    for t2 in range(hpc // 2):
        ev = slice(2 * t2 * QBLK, (2 * t2 + 1) * QBLK)
        od = slice((2 * t2 + 1) * QBLK, (2 * t2 + 2) * QBLK)
        even = acc[ev] * (1.0 / rolled[ev])
        odd = rolled[od] * (1.0 / acc[od])
        o_ref[0, :, t2 * LANES:(t2 + 1) * LANES] = jnp.where(lo, even, odd)


def sel_attention(q, kx, vx, pen, bias_tiles):
    b, s, _ = q.shape
    hpc = N_HEADS // C_KV_HEADS
    gw = hpc * HEAD_DIM
    nbt = bias_tiles.shape[0] - 1
    assert s % SEL_CHUNK == 0
    return pl.pallas_call(
        functools.partial(_sel_attn_kernel, nbt=nbt),
        grid=(b, C_KV_HEADS, s // QBLK),
        in_specs=[pl.BlockSpec((1, QBLK, gw), lambda bb, g, i: (bb, i, g)),
                  pl.BlockSpec((1, QBLK, LANES), lambda bb, g, i: (bb, i, g)),
                  pl.BlockSpec((1, s, LANES), lambda bb, g, i: (bb, 0, g)),
                  pl.BlockSpec((1, s, LANES), lambda bb, g, i: (bb, 0, g)),
                  pl.BlockSpec((nbt + 1, hpc, QBLK, QBLK), lambda bb, g, i: (0, g, 0, 0))],
        out_specs=pl.BlockSpec((1, QBLK, gw), lambda bb, g, i: (bb, i, g)),
        out_shape=jax.ShapeDtypeStruct((b, s, HQ), F32),
        scratch_shapes=[pltpu.VMEM((hpc * QBLK, LANES), BF16),
                        pltpu.VMEM((hpc * QBLK, LANES), F32),
                        pltpu.VMEM((hpc * QBLK, LANES), F32)],
        compiler_params=_cparams("parallel", "parallel", "arbitrary"),
        name="nsa_sel_attention",
    )(q, pen, kx, vx, bias_tiles)


def _sel_bias_idx(s):
    nqt = s // QBLK
    far = -(-(int(np.argmax(_t5_bucket_np(np.arange(4 * MAX_DISTANCE)) == NUM_BUCKETS - 1)) + QBLK) // QBLK)
    nbt = min(nqt, far + 1)
    d0 = np.arange(nbt)[:, None, None] * QBLK
    dist = d0 + np.arange(QBLK)[None, :, None] - np.arange(QBLK)[None, None, :]
    idx = np.where(dist >= 0, _t5_bucket_np(dist), -1).astype(np.int32)
    return np.concatenate([idx, np.full((1, QBLK, QBLK), -1, np.int32)])


def _ffn_kernel(h_ref, g2_ref, wu_ref, cw_ref, cb_ref, wd_ref, g3_ref,
                o_ref, xn_ref, acc_ref, ua_ref, ub_ref, aa_ref, ab_ref, carry_ref, *, tiles_per_seq, nj):
    i = pl.program_id(0)
    tm, tn = xn_ref.shape[0], wu_ref.shape[2]
    xn_ref[...] = _rms(h_ref[...], g2_ref[...]).astype(xn_ref.dtype)
    acc_ref[...] = jnp.zeros(acc_ref.shape, F32)

    @pl.when(i % tiles_per_seq == 0)
    def _():
        carry_ref[...] = jnp.zeros(carry_ref.shape, F32)

    top = 16
    rowt = lax.broadcasted_iota(jnp.int32, (top, tn), 0)

    def up(jj, u_ref):
        xn = xn_ref[...]
        u_ref[0] = jnp.dot(xn, wu_ref[jj], preferred_element_type=F32)
        u_ref[1] = jnp.dot(xn, wu_ref[nj + jj], preferred_element_type=F32)

    def conv(u, s1, s2, c):
        cw = cw_ref[c]
        return ((cb_ref[c] + u * cw[2:3]) + s2 * cw[0:1]) + s1 * cw[1:2]

    def conv_body(u, c):
        return conv(u, pltpu.roll(u, 1, 0), pltpu.roll(u, 2, 0), c)

    def conv_top(u_ref, c):
        prev = carry_ref[c]
        p1, p2 = prev[7:8], prev[6:7]
        u = u_ref[0:top]
        s1 = jnp.where(rowt == 0, p1, pltpu.roll(u, 1, 0))
        s2 = jnp.where(rowt == 0, p2, jnp.where(rowt == 1, p1, pltpu.roll(u, 2, 0)))
        carry_ref[c] = u_ref[tm - 8:tm]
        return conv(u, s1, s2, c)

    def gated(cg, cv):
        return (jax.nn.gelu(cg, approximate=True) * cv).astype(BF16)

    def act(j, u_ref, a_ref):
        a_ref[...] = gated(conv_body(u_ref[0], j), conv_body(u_ref[1], nj + j))
        a_ref[0:top] = gated(conv_top(u_ref.at[0], j), conv_top(u_ref.at[1], nj + j))

    def down(j, a_ref):
        acc_ref[...] += jnp.dot(a_ref[...], wd_ref[j], preferred_element_type=F32)

    up(0, ua_ref)
    up(1, ub_ref)
    act(0, ua_ref, aa_ref)

    def body(jj, carry):
        j = 2 * jj + 1
        up(j + 1, ua_ref)
        act(j, ub_ref, ab_ref)
        down(j - 1, aa_ref)
        up(j + 2, ub_ref)
        act(j + 1, ua_ref, aa_ref)
        down(j, ab_ref)
        return carry

    lax.fori_loop(0, (nj - 3) // 2, body, 0)
    up(nj - 1, ua_ref)
    act(nj - 2, ub_ref, ab_ref)
    down(nj - 3, aa_ref)
    act(nj - 1, ua_ref, aa_ref)
    down(nj - 2, ab_ref)
    down(nj - 1, aa_ref)
    o_ref[...] = h_ref[...] + _rms(acc_ref[...], g3_ref[...])


def conv_ffn(h, g2, w_up, conv_w, conv_b, w_down, g3, seq, tm, tn):
    t, d = h.shape
    dff = w_down.shape[0]
    nj = dff // tn
    assert nj * tn == dff and nj % 2 == 1 and nj >= 3
    wu = w_up.reshape(d, 2 * nj, tn).transpose(1, 0, 2)
    cw = conv_w.reshape(CONV_WIDTH, 2 * nj, tn).transpose(1, 0, 2)
    cb = conv_b.reshape(2 * nj, 1, tn)
    wd = w_down.reshape(nj, tn, d)
    resident = lambda a: pl.BlockSpec(a.shape, (lambda nd: (lambda i: (0,) * nd))(a.ndim))
    return pl.pallas_call(
        functools.partial(_ffn_kernel, tiles_per_seq=seq // tm, nj=nj),
        grid=(t // tm,),
        in_specs=[pl.BlockSpec((tm, d), lambda i: (i, 0)),
                  pl.BlockSpec((1, d), lambda i: (0, 0)),
                  resident(wu), resident(cw), resident(cb), resident(wd),
                  pl.BlockSpec((1, d), lambda i: (0, 0))],
        out_specs=pl.BlockSpec((tm, d), lambda i: (i, 0)),
        out_shape=jax.ShapeDtypeStruct((t, d), F32),
        scratch_shapes=[pltpu.VMEM((tm, d), BF16), pltpu.VMEM((tm, d), F32),
                        pltpu.VMEM((2, tm, tn), F32), pltpu.VMEM((2, tm, tn), F32),
                        pltpu.VMEM((tm, tn), BF16), pltpu.VMEM((tm, tn), BF16),
                        pltpu.VMEM((2 * nj, 8, tn), F32)],
        compiler_params=_cparams("arbitrary"),
        name="conv_ffn",
    )(h, g2.reshape(1, d), wu, cw, cb, wd, g3.reshape(1, d))


def _row_tile(t):
    for tm in (1024, 512, 256, 128):
        if t % tm == 0:
            return tm
    raise ValueError(f"token count {t} is not a multiple of 128")


def mixer_a(h, gains, w_in, w_o, rel_table, bsz, seq):
    t, d = h.shape
    tm = _row_tile(t)
    n_dil = len(DIL_CONFIGS)
    a_in = w_in.shape[1]
    col_scale = np.ones((a_in,), np.float32).reshape(n_dil, 3, HQ)
    col_scale[:, 0] = ATTN_SCALE
    w = (w_in * col_scale.reshape(1, a_in)).astype(BF16)
    idx = np.stack([_banded_idx(1, window // dil, dil) for window, dil in DIL_CONFIGS])
    bias = bias_expand(rel_table, idx)
    outs, lses, dils = [], [], []
    blocks_per_seq = seq // QBLK
    for gi, (window, dil) in enumerate(DIL_CONFIGS):
        assert window // dil <= QBLK and seq % (QBLK * dil) == 0
        tmg = max(tm, QBLK * dil)
        qkv = norm_matmul(h, gains[0], w[:, gi * 3 * HQ:(gi + 1) * 3 * HQ], BF16, tmg, HQ, dil=dil)
        qkv = qkv.reshape(1, t, 3 * HQ)

        def rmap(part, dil=dil):
            return lambda n, i: (0, (n // dil) * blocks_per_seq + i * dil + n % dil, part)

        o, lse = banded_attention(
            qkv, qkv, qkv, bias[gi], n_rows=bsz * dil, n_tiles=seq // (QBLK * dil), tq=QBLK, n_prev=1, kw=HQ,
            q_map=rmap(0), k_map=rmap(1), v_map=rmap(2), o_map=rmap(0),
            out_shape=(1, t, HQ), lse_shape=(1, t, LANES), name=f"dilated_attention_{dil}")
        outs.append(o.reshape(t, HQ))
        lses.append(lse.reshape(t, LANES))
        dils.append(dil)
    e = jnp.asarray(_head_expand_matrix(0), BF16)
    return combine_proj_a(outs, lses, dils, e, w_o.astype(BF16), gains[1], h, min(tm, 512))


def mixer_b(h, gains, w_in, sinks, w_o, rel_table, bsz, seq):
    t, d = h.shape
    tm = _row_tile(t)
    hk = B_KV_HEADS * HEAD_DIM
    n_in = w_in.shape[1]
    col_scale = np.ones((n_in,), np.float32)
    col_scale[:HQ] = ATTN_SCALE
    w = (w_in * col_scale[None]).astype(BF16)
    qkv = norm_matmul(h, gains[0], w, BF16, tm, n_in // 2 if (n_in // 2) % LANES == 0 else n_in)
    q = qkv[:, :HQ].reshape(bsz, seq, HQ)
    kd = _dup_groups(qkv[:, HQ:HQ + hk].reshape(bsz, seq, hk))
    vd = _dup_groups(qkv[:, HQ + hk:].reshape(bsz, seq, hk))
    bias = bias_expand(rel_table, _banded_idx(1, B_WINDOW - 1, 1)[None])[0]
    sink_rows = sinks.astype(F32)
    ident = lambda n, i: (n, i, 0)
    (o,) = banded_attention(q, kd, vd, bias, n_rows=bsz, n_tiles=seq // QBLK, tq=QBLK, n_prev=1,
                            kw=2 * LANES, q_map=ident, k_map=ident, v_map=ident, o_map=ident,
                            out_shape=(bsz, seq, HQ), sinks=sink_rows, name="sink_window_attention")
    return _proj_call(_proj_kernel, "sink_proj", tm, [o.reshape(t, HQ)], [], w_o.astype(BF16), gains[1], h)


def mixer_c(h, gains, w_in, cmp_pos, cmp_w1, cmp_w2, w_o, rel_table, bsz, seq):
    t, d = h.shape
    tm = _row_tile(t)
    g = C_KV_HEADS
    hk = g * HEAD_DIM
    rest = w_in.shape[1] - HQ
    rest_pad = -(-rest // LANES) * LANES
    wq = (w_in[:, :HQ] * ATTN_SCALE).astype(BF16)
    wr = jnp.pad(w_in[:, HQ:], ((0, 0), (0, rest_pad - rest))).astype(BF16)
    q = norm_matmul(h, gains[0], wq, BF16, tm, HQ).reshape(bsz, seq, HQ)
    r = norm_matmul(h, gains[0], wr, F32, tm, rest_pad)
    kv = [r[:, i * hk:(i + 1) * hk].reshape(bsz, seq, hk) for i in range(6)]
    gates_raw = r[:, 6 * hk:6 * hk + LANES]

    ncr = seq // CMP_STRIDE
    half = CMP_BLOCK // 2
    assert half == CMP_STRIDE
    chunks = jnp.stack([kv[0], kv[1]]).reshape(2, bsz, ncr, half, g, HEAD_DIM)
    chunks = chunks.transpose(0, 1, 4, 2, 3, 5).reshape(2, bsz, g, ncr, half * HEAD_DIM)
    pos = cmp_pos.reshape(2, 2, half * HEAD_DIM)
    w1 = cmp_w1.reshape(2, 2, half * HEAD_DIM, -1).astype(BF16)
    cmp = compress(chunks, pos, w1, cmp_w2.astype(BF16))
    cmp = cmp.transpose(0, 1, 3, 2, 4).reshape(2, bsz, ncr, hk).astype(BF16)
    kcd, vcd = _dup_groups(cmp[0]), _dup_groups(cmp[1])

    ns = seq // SEL_BLOCK
    assert ns <= HEAD_DIM
    k_sel = min(N_SELECT, ns)
    starts = np.arange(ncr) * CMP_STRIDE
    blk = np.arange(LANES) - HEAD_DIM
    ov = ((starts[:, None] < (blk[None, :] + 1) * SEL_BLOCK)
          & (starts[:, None] + CMP_BLOCK > blk[None, :] * SEL_BLOCK)
          & (blk[None, :] >= 0) & (blk[None, :] < ns) & (starts[:, None] + CMP_BLOCK <= seq))
    o_c, pen = cmp_attention(q, kcd, vcd, jnp.asarray(ov.astype(np.float32), BF16), ns, k_sel)

    sel_bias = bias_expand(rel_table, _sel_bias_idx(seq))
    key_blk = np.arange(seq)[:, None] // SEL_BLOCK == np.arange(HEAD_DIM)[None, :]
    onehot = jnp.broadcast_to(jnp.asarray(np.where(key_blk, -NEG_INF, 0.0), BF16), (bsz, seq, HEAD_DIM))
    ones = jnp.ones((bsz, seq, HEAD_DIM), BF16)
    ks, vs = kv[2].astype(BF16), kv[3].astype(BF16)
    kx = jnp.concatenate([ks[..., :HEAD_DIM], onehot, ks[..., HEAD_DIM:], onehot], axis=-1)
    vx = jnp.concatenate([vs[..., :HEAD_DIM], ones, vs[..., HEAD_DIM:], ones], axis=-1)
    o_s = sel_attention(q, kx, vx, pen, sel_bias)

    n_prev = -(-(C_WINDOW - 1) // QBLK)
    tqw = n_prev * QBLK
    wbias = bias_expand(rel_table, _banded_idx(n_prev, C_WINDOW - 1, 1)[None])[0]
    ident = lambda n, i: (n, i, 0)
    (o_w,) = banded_attention(q, _dup_groups(kv[4].astype(BF16)), _dup_groups(kv[5].astype(BF16)), wbias,
                              n_rows=bsz, n_tiles=seq // tqw, tq=tqw, n_prev=n_prev, kw=2 * LANES,
                              q_map=ident, k_map=ident, v_map=ident, o_map=ident,
                              out_shape=(bsz, seq, HQ), name="nsa_window_attention")

    e = jnp.asarray(np.stack([_head_expand_matrix(i * N_HEADS) for i in range(3)]), BF16)
    return _proj_call(_proj_c_kernel, "nsa_gate_proj", min(tm, 512),
                      [o_c.reshape(t, HQ), o_s.reshape(t, HQ), o_w.reshape(t, HQ), gates_raw], [e],
                      w_o.astype(BF16), gains[1], h)


def kernel(x, rel_table, norm_gains, a_w_in, a_w_o, b_w_in, b_sinks, b_w_o, c_w_in, c_cmp_pos, c_cmp_w1,
           c_cmp_w2, c_w_o, ffn_w_up, ffn_conv_w, ffn_conv_b, ffn_w_down):
    bsz, seq, d = x.shape
    depth = norm_gains.shape[0]
    h = x.reshape(bsz * seq, d)
    tm = _row_tile(seq)
    for i in range(depth):
        kind, j = i % 3, i // 3
        g = norm_gains[i]
        if kind == 0:
            h = mixer_a(h, g, a_w_in[j], a_w_o[j], rel_table, bsz, seq)
        elif kind == 1:
            h = mixer_b(h, g, b_w_in[j], b_sinks[j], b_w_o[j], rel_table, bsz, seq)
        else:
            h = mixer_c(h, g, c_w_in[j], c_cmp_pos[j], c_cmp_w1[j], c_cmp_w2[j], c_w_o[j], rel_table, bsz, seq)
        h = conv_ffn(h, g[2], ffn_w_up[i].astype(BF16), ffn_conv_w[i], ffn_conv_b[i],
                     ffn_w_down[i].astype(BF16), g[3], seq, min(tm, 512), 256)
    return h.reshape(bsz, seq, d)
```

```python
import functools
import math

import numpy as np
import jax
import jax.numpy as jnp
from jax import lax
from jax.experimental import pallas as pl
from jax.experimental.pallas import tpu as pltpu

F32 = jnp.float32
BF16 = jnp.bfloat16

N_HEADS = 16
HEAD_DIM = 64
HQ = N_HEADS * HEAD_DIM
LANES = 128
ATTN_SCALE = HEAD_DIM ** -0.5
NUM_BUCKETS = 32
MAX_DISTANCE = 2048
RMS_EPS = 1e-6
NEG_INF = -1e30
FORCE_SCORE = 1e9
DIL_CONFIGS = ((128, 1), (512, 4), (2048, 16))
B_KV_HEADS = 2
B_WINDOW = 128
C_KV_HEADS = 2
CMP_BLOCK = 32
CMP_STRIDE = 16
SEL_BLOCK = 64
N_SELECT = 16
C_WINDOW = 512
CONV_WIDTH = 3
QBLK = 128
SEL_CHUNK = 512
VMEM_LIMIT = 56 * 1024 * 1024


def _cparams(*sem):
    return pltpu.CompilerParams(dimension_semantics=sem, vmem_limit_bytes=VMEM_LIMIT)


def _t5_bucket_np(dist):
    max_exact = NUM_BUCKETS // 2
    d = np.maximum(dist, 0)
    df = np.maximum(d, 1).astype(np.float64)
    large = max_exact + np.floor(np.log(df / max_exact) / math.log(MAX_DISTANCE / max_exact)
                                 * (NUM_BUCKETS - max_exact) + 1e-9).astype(np.int64)
    large = np.minimum(large, NUM_BUCKETS - 1)
    return np.where(d < max_exact, d, large).astype(np.int32)


def _rms(x, g):
    ms = jnp.mean(x * x, axis=-1, keepdims=True)
    return (x * lax.rsqrt(ms + RMS_EPS)) * g


def _split3(w):
    hi = w.astype(BF16)
    r1 = w - hi.astype(F32)
    mid = r1.astype(BF16)
    lo = (r1 - mid.astype(F32)).astype(BF16)
    return hi, mid, lo


def _dot3(w, e):
    hi, mid, lo = _split3(w)
    d = functools.partial(jnp.dot, preferred_element_type=F32)
    return (d(hi, e) + d(mid, e)) + d(lo, e)


def _dot_nt(a, b):
    return lax.dot_general(a, b, (((1,), (1,)), ((), ())), preferred_element_type=F32)


def _stack_heads(q_tile, heads):
    lane = lax.broadcasted_iota(jnp.int32, (QBLK, LANES), 1)
    lo = lane < HEAD_DIM
    pieces = []
    for h in heads:
        q2 = q_tile(h // 2)
        keep = lo if h % 2 == 0 else jnp.logical_not(lo)
        pieces.append(jnp.where(keep, q2, jnp.zeros_like(q2)))
    return jnp.concatenate(pieces, axis=0)


def _merge_pairs(o, n_heads):
    lane = lax.broadcasted_iota(jnp.int32, (QBLK, LANES), 1)
    lo = lane < HEAD_DIM
    out = []
    for t in range(0, n_heads, 2):
        out.append(jnp.where(lo, o[t * QBLK:(t + 1) * QBLK], o[(t + 1) * QBLK:(t + 2) * QBLK]))
    return out


def _bias_kernel(tab_ref, idx_ref, o_ref):
    h = pl.program_id(1)
    idx = idx_ref[0]
    acc = jnp.full(idx.shape, NEG_INF, F32)
    for b in range(NUM_BUCKETS):
        acc = jnp.where(idx == b, tab_ref[b, h], acc)
    o_ref[0, 0] = acc


def bias_expand(table, idx):
    n, r, c = idx.shape
    return pl.pallas_call(
        _bias_kernel,
        grid=(n, N_HEADS),
        in_specs=[pl.BlockSpec(memory_space=pltpu.SMEM),
                  pl.BlockSpec((1, r, c), lambda i, h: (i, 0, 0))],
        out_specs=pl.BlockSpec((1, 1, r, c), lambda i, h: (i, h, 0, 0)),
        out_shape=jax.ShapeDtypeStruct((n, N_HEADS, r, c), F32),
        compiler_params=_cparams("parallel", "parallel"),
        name="bias_expand",
    )(table, jnp.asarray(idx))


def _banded_idx(n_prev, max_dist, stride):
    kb = (n_prev + 1) * QBLK
    dist = np.arange(QBLK)[:, None] + n_prev * QBLK - np.arange(kb)[None, :]
    valid = (dist >= 0) & (dist <= max_dist)
    return np.where(valid, _t5_bucket_np(dist * stride), -1).astype(np.int32)


def _norm_mm_kernel(x_ref, g_ref, w_ref, o_ref, xn_ref, *chunk_ref, dil):
    @pl.when(pl.program_id(1) == 0)
    def _():
        if dil == 1:
            xn_ref[...] = _rms(x_ref[...], g_ref[...]).astype(xn_ref.dtype)
        else:
            (xs_ref,) = chunk_ref
            nc = xs_ref.shape[0]
            for c in range(nc):
                xs_ref[c] = x_ref[:, c * LANES:(c + 1) * LANES]
            span = QBLK * dil
            for u in range(x_ref.shape[0] // span):
                for r in range(dil):
                    rows = jnp.concatenate(
                        [xs_ref[c, pl.ds(u * span + r, QBLK, stride=dil), :] for c in range(nc)], axis=1)
                    dst = u * span + r * QBLK
                    xn_ref[dst:dst + QBLK, :] = _rms(rows, g_ref[...]).astype(xn_ref.dtype)

    o_ref[...] = jnp.dot(xn_ref[...], w_ref[...], preferred_element_type=F32).astype(o_ref.dtype)


def norm_matmul(x, gain, w, out_dtype, tm, tn, dil=1):
    t, d = x.shape
    n = w.shape[1]
    assert tm % (QBLK * dil) == 0
    scratch = [pltpu.VMEM((tm, d), BF16)]
    if dil > 1:
        scratch.append(pltpu.VMEM((d // LANES, tm, LANES), F32))
    return pl.pallas_call(
        functools.partial(_norm_mm_kernel, dil=dil),
        grid=(t // tm, n // tn),
        in_specs=[pl.BlockSpec((tm, d), lambda i, j: (i, 0)),
                  pl.BlockSpec((1, d), lambda i, j: (0, 0)),
                  pl.BlockSpec((d, tn), lambda i, j: (0, j))],
        out_specs=pl.BlockSpec((tm, tn), lambda i, j: (i, j)),
        out_shape=jax.ShapeDtypeStruct((t, n), out_dtype),
        scratch_shapes=scratch,
        compiler_params=_cparams("parallel", "arbitrary"),
        name="norm_matmul",
    )(x, gain.reshape(1, d), w)


def _finish_proj(y, w_ref, g_ref, h_ref, o_ref):
    z = jnp.dot(y.astype(BF16), w_ref[...], preferred_element_type=F32)
    o_ref[...] = h_ref[...] + _rms(z, g_ref[...])


def _proj_kernel(y_ref, w_ref, g_ref, h_ref, o_ref):
    _finish_proj(y_ref[...], w_ref, g_ref, h_ref, o_ref)


def _unpermute(src_ref, dst_ref, dil, tm, i):
    span = QBLK * dil
    nc = dst_ref.shape[0]
    if span <= tm:
        for u in range(tm // span):
            for r in range(dil):
                lo = u * span + r * QBLK
                for c in range(nc):
                    dst_ref[c, pl.ds(u * span + r, QBLK, stride=dil), :] = src_ref[lo:lo + QBLK,
                                                                                   c * LANES:(c + 1) * LANES]
    else:
        per = tm // dil
        off = (i % (span // tm)) * per
        for r in range(dil):
            lo = pl.multiple_of(r * QBLK + off, 8)
            for c in range(nc):
                dst_ref[c, pl.ds(r, per, stride=dil), :] = src_ref[pl.ds(lo, per), c * LANES:(c + 1) * LANES]
    return jnp.concatenate([dst_ref[c] for c in range(nc)], axis=1) if nc > 1 else dst_ref[0]


def _proj_a_kernel(*refs, dils, tm):
    n = len(dils)
    o_refs, l_refs = refs[:n], refs[n:2 * n]
    e_ref, w_ref, g_ref, h_ref, out_ref = refs[2 * n:2 * n + 5]
    scratch = refs[2 * n + 5:]
    i = pl.program_id(0)
    outs, lses = [], []
    si = 0
    for gi, dil in enumerate(dils):
        if dil == 1:
            outs.append(o_refs[gi][...])
            lses.append(l_refs[gi][...])
        else:
            outs.append(_unpermute(o_refs[gi], scratch[si], dil, tm, i))
            lses.append(_unpermute(l_refs[gi], scratch[si + 1], dil, tm, i))
            si += 2
    mx = functools.reduce(jnp.maximum, lses)
    es = [jnp.exp(l - mx) for l in lses]
    inv = 1.0 / functools.reduce(lambda a, b: a + b, es)
    e = e_ref[...]
    y = outs[0] * _dot3(es[0] * inv, e)
    for gi in range(1, n):
        y = y + outs[gi] * _dot3(es[gi] * inv, e)
    _finish_proj(y, w_ref, g_ref, h_ref, out_ref)


def combine_proj_a(outs, lses, dils, e, w, gain, h, tm):
    t, d = h.shape

    def row_spec(width, dil):
        rows = max(tm, QBLK * dil)
        return pl.BlockSpec((rows, width), lambda i, q=rows // tm: (i // q, 0))

    in_specs = [row_spec(HQ, dil) for dil in dils] + [row_spec(LANES, dil) for dil in dils] + [
        pl.BlockSpec(e.shape, lambda i: (0, 0)),
        pl.BlockSpec(w.shape, lambda i: (0, 0)),
        pl.BlockSpec((1, d), lambda i: (0, 0)),
        pl.BlockSpec((tm, d), lambda i: (i, 0))]
    scratch = []
    for dil in dils:
        if dil > 1:
            scratch += [pltpu.VMEM((HQ // LANES, tm, LANES), F32), pltpu.VMEM((1, tm, LANES), F32)]
    return pl.pallas_call(
        functools.partial(_proj_a_kernel, dils=tuple(dils), tm=tm),
        grid=(t // tm,),
        in_specs=in_specs,
        out_specs=pl.BlockSpec((tm, d), lambda i: (i, 0)),
        out_shape=jax.ShapeDtypeStruct((t, d), F32),
        scratch_shapes=scratch,
        compiler_params=_cparams("arbitrary"),
        name="dilated_combine_proj",
    )(*outs, *lses, e, w, gain.reshape(1, d), h)


def _proj_c_kernel(oc_ref, os_ref, ow_ref, gr_ref, e_ref, w_ref, g_ref, h_ref, o_ref):
    sig = jax.nn.sigmoid(gr_ref[...])
    y = _dot3(sig, e_ref[0]) * oc_ref[...]
    y = y + _dot3(sig, e_ref[1]) * os_ref[...]
    y = y + _dot3(sig, e_ref[2]) * ow_ref[...]
    _finish_proj(y, w_ref, g_ref, h_ref, o_ref)


def _proj_call(kernel, name, tm, row_inputs, const_inputs, w, gain, h):
    t, d = h.shape
    k = w.shape[0]
    row_specs = [pl.BlockSpec((tm, a.shape[1]), lambda i: (i, 0)) for a in row_inputs]
    const_specs = [pl.BlockSpec(a.shape, (lambda nd: (lambda i: (0,) * nd))(a.ndim)) for a in const_inputs]
    return pl.pallas_call(
        kernel,
        grid=(t // tm,),
        in_specs=row_specs + const_specs + [
            pl.BlockSpec((k, d), lambda i: (0, 0)),
            pl.BlockSpec((1, d), lambda i: (0, 0)),
            pl.BlockSpec((tm, d), lambda i: (i, 0))],
        out_specs=pl.BlockSpec((tm, d), lambda i: (i, 0)),
        out_shape=jax.ShapeDtypeStruct((t, d), F32),
        compiler_params=_cparams("parallel"),
        name=name,
    )(*row_inputs, *const_inputs, w, gain.reshape(1, d), h)


def _head_expand_matrix(row_offset):
    e = np.zeros((LANES, HQ), np.float32)
    for h in range(N_HEADS):
        e[row_offset + h, h * HEAD_DIM:(h + 1) * HEAD_DIM] = 1.0
    return e


def _banded_kernel(*refs, tq, n_prev, n_chunks, stack, with_sinks, with_lse):
    q_ref, kp_ref, kc_ref, vp_ref, vc_ref, bias_ref = refs[:6]
    pos = 6
    sink_ref = None
    if with_sinks:
        sink_ref = refs[pos]
        pos += 1
    o_ref = refs[pos]
    pos += 1
    lse_ref = None
    if with_lse:
        lse_ref = refs[pos]
        pos += 1
    kbuf, vbuf = refs[pos], refs[pos + 1]

    i = pl.program_id(1)
    hpc = N_HEADS // n_chunks
    kb = (n_prev + 1) * QBLK
    kbuf[0:tq] = kp_ref[0]
    kbuf[tq:2 * tq] = kc_ref[0]
    vbuf[0:tq] = vp_ref[0]
    vbuf[tq:2 * tq] = vc_ref[0]
    lane = lax.broadcasted_iota(jnp.int32, (QBLK, LANES), 1)
    head_row = lax.broadcasted_iota(jnp.int32, (stack * QBLK, 1), 0)

    for s in range(tq // QBLK):
        r0 = s * QBLK
        k0 = tq + r0 - n_prev * QBLK
        col = lax.broadcasted_iota(jnp.int32, (1, kb), 1) + k0
        kmask = jnp.where(jnp.logical_and(i == 0, col < tq), NEG_INF, 0.0).astype(F32)
        lse_acc = jnp.zeros((QBLK, LANES), F32)
        for h0 in range(0, N_HEADS, stack):
            heads = list(range(h0, h0 + stack))
            c = h0 // hpc
            qst = _stack_heads(lambda p: q_ref[0, r0:r0 + QBLK, p * LANES:(p + 1) * LANES], heads)
            kx = kbuf[k0:k0 + kb, c * LANES:(c + 1) * LANES]
            vx = vbuf[k0:k0 + kb, c * LANES:(c + 1) * LANES]
            sc = _dot_nt(qst, kx)
            sc = sc + bias_ref[h0:h0 + stack].reshape(stack * QBLK, kb) + kmask
            m = jnp.max(sc, axis=-1, keepdims=True)
            if with_sinks:
                sk = sink_ref[h0 + stack - 1]
                for t in range(stack - 2, -1, -1):
                    sk = jnp.where(head_row < (t + 1) * QBLK, sink_ref[h0 + t], sk)
                m = jnp.maximum(m, sk)
            p = jnp.exp(sc - m)
            den = jnp.sum(p, axis=-1, keepdims=True)
            norm = den + jnp.exp(sk - m) if with_sinks else den
            o = jnp.dot(p.astype(BF16), vx, preferred_element_type=F32) * (1.0 / norm)
            for t2, blk in enumerate(_merge_pairs(o, stack)):
                pidx = h0 // 2 + t2
                o_ref[0, r0:r0 + QBLK, pidx * LANES:(pidx + 1) * LANES] = blk.astype(o_ref.dtype)
            if with_lse:
                lse = m + jnp.log(den)
                for t, h in enumerate(heads):
                    lse_acc = jnp.where(lane == h, lse[t * QBLK:(t + 1) * QBLK], lse_acc)
        if with_lse:
            lse_ref[0, r0:r0 + QBLK, :] = lse_acc


def banded_attention(q_arr, k_arr, v_arr, bias, *, n_rows, n_tiles, tq, n_prev, kw,
                     q_map, k_map, v_map, o_map, out_shape, lse_shape=None, sinks=None, stack=2, name):
    n_chunks = kw // LANES
    assert stack % 2 == 0 and (N_HEADS // n_chunks) % stack == 0
    kb = (n_prev + 1) * QBLK
    with_sinks = sinks is not None
    with_lse = lse_shape is not None

    def prev(fn):
        return lambda n, i: fn(n, jnp.maximum(i - 1, 0))

    in_specs = [pl.BlockSpec((1, tq, HQ), q_map),
                pl.BlockSpec((1, tq, kw), prev(k_map)),
                pl.BlockSpec((1, tq, kw), k_map),
                pl.BlockSpec((1, tq, kw), prev(v_map)),
                pl.BlockSpec((1, tq, kw), v_map),
                pl.BlockSpec((N_HEADS, QBLK, kb), lambda n, i: (0, 0, 0))]
    args = [q_arr, k_arr, k_arr, v_arr, v_arr, bias]
    if with_sinks:
        in_specs.append(pl.BlockSpec(memory_space=pltpu.SMEM))
        args.append(sinks)
    out_specs = [pl.BlockSpec((1, tq, HQ), o_map)]
    out_shapes = [jax.ShapeDtypeStruct(out_shape, F32)]
    if with_lse:
        out_specs.append(pl.BlockSpec((1, tq, LANES), o_map))
        out_shapes.append(jax.ShapeDtypeStruct(lse_shape, F32))
    res = pl.pallas_call(
        functools.partial(_banded_kernel, tq=tq, n_prev=n_prev, n_chunks=n_chunks, stack=stack,
                          with_sinks=with_sinks, with_lse=with_lse),
        grid=(n_rows, n_tiles),
        in_specs=in_specs,
        out_specs=out_specs,
        out_shape=out_shapes,
        scratch_shapes=[pltpu.VMEM((2 * tq, kw), BF16), pltpu.VMEM((2 * tq, kw), BF16)],
        compiler_params=_cparams("parallel", "arbitrary"),
        name=name,
    )(*args)
    return res


def _dup_groups(x):
    g0, g1 = x[..., :HEAD_DIM], x[..., HEAD_DIM:]
    return jnp.concatenate([g0, g0, g1, g1], axis=-1)


def _compress_kernel(ch_ref, pos_ref, w1_ref, w2_ref, o_ref):
    ch = ch_ref[0, 0, 0]
    rows = ch.shape[0]
    posv = pos_ref[0]
    a = jnp.dot((ch + posv[0:1]).astype(BF16), w1_ref[0, 0], preferred_element_type=F32)
    b = jnp.dot((ch + posv[1:2]).astype(BF16), w1_ref[0, 1], preferred_element_type=F32)
    hid = a + pltpu.roll(b, rows - 1, 0)
    act = jax.nn.gelu(hid, approximate=True)
    o_ref[0, 0, 0] = jnp.dot(act.astype(BF16), w2_ref[0], preferred_element_type=F32)


def compress(chunks, pos, w1, w2):
    _, b, g, rows, width = chunks.shape
    hid = w1.shape[-1]
    return pl.pallas_call(
        _compress_kernel,
        grid=(2, b, g),
        in_specs=[pl.BlockSpec((1, 1, 1, rows, width), lambda i, bb, gg: (i, bb, gg, 0, 0)),
                  pl.BlockSpec((1, 2, width), lambda i, bb, gg: (i, 0, 0)),
                  pl.BlockSpec((1, 2, width, hid), lambda i, bb, gg: (i, 0, 0, 0)),
                  pl.BlockSpec((1, hid, HEAD_DIM), lambda i, bb, gg: (i, 0, 0))],
        out_specs=pl.BlockSpec((1, 1, 1, rows, HEAD_DIM), lambda i, bb, gg: (i, bb, gg, 0, 0)),
        out_shape=jax.ShapeDtypeStruct((2, b, g, rows, HEAD_DIM), F32),
        compiler_params=_cparams("parallel", "parallel", "parallel"),
        name="nsa_compress",
    )(chunks, pos, w1, w2)


def _cmp_attn_kernel(q_ref, kc_ref, vc_ref, ov_ref, place_ref, o_ref, sel_ref, *, n_sel_blocks, k_sel):
    qi = pl.program_id(1)
    ncr = kc_ref.shape[1]
    hpc = N_HEADS // C_KV_HEADS
    qpos = qi * QBLK + lax.broadcasted_iota(jnp.int32, (QBLK, 1), 0)
    cidx = lax.broadcasted_iota(jnp.int32, (1, ncr), 1)
    valid = (cidx * CMP_STRIDE + (CMP_BLOCK - 1)) <= qpos
    maskc = jnp.where(valid, 0.0, NEG_INF).astype(F32)
    anyv = (qpos >= CMP_BLOCK - 1).astype(F32)
    blk_id = lax.broadcasted_iota(jnp.int32, (HEAD_DIM, QBLK), 0)
    cur = (qi * QBLK + lax.broadcasted_iota(jnp.int32, (HEAD_DIM, QBLK), 1)) // SEL_BLOCK
    forced = jnp.logical_or(jnp.logical_or(blk_id == 0, blk_id == cur), blk_id == cur - 1)
    allowed = blk_id <= cur
    for g in range(C_KV_HEADS):
        heads = list(range(g * hpc, (g + 1) * hpc))
        qst = _stack_heads(lambda p: q_ref[0, :, p * LANES:(p + 1) * LANES], heads)
        sc = _dot_nt(qst, kc_ref[0, :, g * LANES:(g + 1) * LANES])
        sc3 = sc.reshape(hpc, QBLK, ncr) + maskc[None]
        m = jnp.max(sc3, axis=-1, keepdims=True)
        e = jnp.exp(sc3 - m)
        ssum = jnp.sum(e, axis=-1, keepdims=True)
        p = (e * (1.0 / ssum)) * anyv[None]
        o = jnp.dot(p.reshape(hpc * QBLK, ncr).astype(BF16), vc_ref[0, :, g * LANES:(g + 1) * LANES],
                    preferred_element_type=F32)
        for t2, blk in enumerate(_merge_pairs(o, hpc)):
            pidx = (g * hpc) // 2 + t2
            o_ref[0, :, pidx * LANES:(pidx + 1) * LANES] = blk
        hi, mid, lo = _split3(jnp.sum(p, axis=0))
        ovt = ov_ref[...]
        imp = (_dot_nt(ovt, hi) + _dot_nt(ovt, mid)) + _dot_nt(ovt, lo)
        score = jnp.where(forced, FORCE_SCORE, jnp.where(allowed, imp, NEG_INF))
        bits = pltpu.bitcast(score, jnp.int32)
        key = jnp.where(bits < 0, bits ^ jnp.int32(0x7FFFFFFF), bits)
        key_m1 = key - 1
        rank = jnp.zeros((HEAD_DIM, QBLK), jnp.int32)
        for i in range(n_sel_blocks):
            thr = jnp.where(blk_id > i, key_m1, key)
            rank = rank + jnp.where(key[i:i + 1, :] > thr, 1, 0)
        keep = jnp.logical_and(rank < k_sel, blk_id < n_sel_blocks)
        pen_t = jnp.where(keep, 0.0, -1.0).astype(BF16)
        pen = lax.dot_general(pen_t, place_ref[...], (((0,), (0,)), ((), ())), preferred_element_type=F32)
        sel_ref[0, :, g * LANES:(g + 1) * LANES] = pen.astype(sel_ref.dtype)


def cmp_attention(q, kcd, vcd, ov, n_sel_blocks, k_sel):
    b, s, _ = q.shape
    ncr = kcd.shape[1]
    place = np.zeros((HEAD_DIM, LANES), np.float32)
    place[np.arange(HEAD_DIM), HEAD_DIM + np.arange(HEAD_DIM)] = 1.0
    return pl.pallas_call(
        functools.partial(_cmp_attn_kernel, n_sel_blocks=n_sel_blocks, k_sel=k_sel),
        grid=(b, s // QBLK),
        in_specs=[pl.BlockSpec((1, QBLK, HQ), lambda bb, i: (bb, i, 0)),
                  pl.BlockSpec((1, ncr, 2 * LANES), lambda bb, i: (bb, 0, 0)),
                  pl.BlockSpec((1, ncr, 2 * LANES), lambda bb, i: (bb, 0, 0)),
                  pl.BlockSpec((HEAD_DIM, ncr), lambda bb, i: (0, 0)),
                  pl.BlockSpec((HEAD_DIM, LANES), lambda bb, i: (0, 0))],
        out_specs=[pl.BlockSpec((1, QBLK, HQ), lambda bb, i: (bb, i, 0)),
                   pl.BlockSpec((1, QBLK, 2 * LANES), lambda bb, i: (bb, i, 0))],
        out_shape=[jax.ShapeDtypeStruct((b, s, HQ), F32),
                   jax.ShapeDtypeStruct((b, s, 2 * LANES), BF16)],
        compiler_params=_cparams("parallel", "parallel"),
        name="nsa_cmp_attention",
    )(q, kcd, vcd, ov, jnp.asarray(place, BF16))


def _sel_attn_kernel(q_ref, pen_ref, k_ref, v_ref, bias_ref, o_ref, qst_ref, m_ref, acc_ref, *, nbt):
    qi = pl.program_id(2)
    hpc = N_HEADS // C_KV_HEADS
    nsub = SEL_CHUNK // QBLK
    lane = lax.broadcasted_iota(jnp.int32, (QBLK, LANES), 1)
    lo = lane < HEAD_DIM
    pen = pen_ref[0].astype(F32)
    for t in range(hpc):
        q2 = q_ref[0, :, (t // 2) * LANES:(t // 2 + 1) * LANES].astype(F32)
        if t % 2:
            q2 = pltpu.roll(q2, HEAD_DIM, 1)
        qst_ref[t * QBLK:(t + 1) * QBLK, :] = jnp.where(lo, q2, pen).astype(BF16)
    m_ref[...] = jnp.full(m_ref.shape, NEG_INF, F32)
    acc_ref[...] = jnp.zeros(acc_ref.shape, F32)

    def body(c, carry):
        k0 = pl.multiple_of(c * SEL_CHUNK, SEL_CHUNK)
        s = _dot_nt(qst_ref[...], k_ref[0, pl.ds(k0, SEL_CHUNK), :])
        tiles = []
        for u in range(nsub):
            off = qi - nsub * c - u
            idx = jnp.where(off < 0, nbt, jnp.minimum(off, nbt - 1))
            tiles.append(s[:, u * QBLK:(u + 1) * QBLK] + bias_ref[idx].reshape(hpc * QBLK, QBLK))
        m_old = m_ref[...]
        m_new = jnp.maximum(m_old, jnp.max(functools.reduce(jnp.maximum, tiles), axis=-1, keepdims=True))
        p = jnp.concatenate([jnp.exp(tl - m_new).astype(BF16) for tl in tiles], axis=1)
        pv = jnp.dot(p, v_ref[0, pl.ds(k0, SEL_CHUNK), :], preferred_element_type=F32)
        acc_ref[...] = jnp.exp(m_old - m_new) * acc_ref[...] + pv
        m_ref[...] = m_new
        return carry

    lax.fori_loop(0, qi // nsub + 1, body, 0)
    acc = acc_ref[...]
    rolled = pltpu.roll(acc, HEAD_DIM, 1)
    for t2 in range(hpc // 2):
        ev = slice(2 * t2 * QBLK, (2 * t2 + 1) * QBLK)
        od = slice((2 * t2 + 1) * QBLK, (2 * t2 + 2) * QBLK)
        even = acc[ev] * (1.0 / rolled[ev])
        odd = rolled[od] * (1.0 / acc[od])
        o_ref[0, :, t2 * LANES:(t2 + 1) * LANES] = jnp.where(lo, even, odd)


def sel_attention(q, kx, vx, pen, bias_tiles):
    b, s, _ = q.shape
    hpc = N_HEADS // C_KV_HEADS
    gw = hpc * HEAD_DIM
    nbt = bias_tiles.shape[0] - 1
    assert s % SEL_CHUNK == 0
    return pl.pallas_call(
        functools.partial(_sel_attn_kernel, nbt=nbt),
        grid=(b, C_KV_HEADS, s // QBLK),
        in_specs=[pl.BlockSpec((1, QBLK, gw), lambda bb, g, i: (bb, i, g)),
                  pl.BlockSpec((1, QBLK, LANES), lambda bb, g, i: (bb, i, g)),
                  pl.BlockSpec((1, s, LANES), lambda bb, g, i: (bb, 0, g)),
                  pl.BlockSpec((1, s, LANES), lambda bb, g, i: (bb, 0, g)),
                  pl.BlockSpec((nbt + 1, hpc, QBLK, QBLK), lambda bb, g, i: (0, g, 0, 0))],
        out_specs=pl.BlockSpec((1, QBLK, gw), lambda bb, g, i: (bb, i, g)),
        out_shape=jax.ShapeDtypeStruct((b, s, HQ), F32),
        scratch_shapes=[pltpu.VMEM((hpc * QBLK, LANES), BF16),
                        pltpu.VMEM((hpc * QBLK, LANES), F32),
                        pltpu.VMEM((hpc * QBLK, LANES), F32)],
        compiler_params=_cparams("parallel", "parallel", "arbitrary"),
        name="nsa_sel_attention",
    )(q, pen, kx, vx, bias_tiles)


def _sel_bias_idx(s):
    nqt = s // QBLK
    far = -(-(int(np.argmax(_t5_bucket_np(np.arange(4 * MAX_DISTANCE)) == NUM_BUCKETS - 1)) + QBLK) // QBLK)
    nbt = min(nqt, far + 1)
    d0 = np.arange(nbt)[:, None, None] * QBLK
    dist = d0 + np.arange(QBLK)[None, :, None] - np.arange(QBLK)[None, None, :]
    idx = np.where(dist >= 0, _t5_bucket_np(dist), -1).astype(np.int32)
    return np.concatenate([idx, np.full((1, QBLK, QBLK), -1, np.int32)])


def _ffn_kernel(h_ref, g2_ref, wu_ref, cw_ref, cb_ref, wd_ref, g3_ref,
                o_ref, xn_ref, acc_ref, ua_ref, ub_ref, aa_ref, ab_ref, carry_ref, *, tiles_per_seq, nj):
    i = pl.program_id(0)
    tm, tn = xn_ref.shape[0], wu_ref.shape[2]
    xn_ref[...] = _rms(h_ref[...], g2_ref[...]).astype(xn_ref.dtype)
    acc_ref[...] = jnp.zeros(acc_ref.shape, F32)

    @pl.when(i % tiles_per_seq == 0)
    def _():
        carry_ref[...] = jnp.zeros(carry_ref.shape, F32)

    top = 16
    rowt = lax.broadcasted_iota(jnp.int32, (top, tn), 0)

    def up(jj, u_ref):
        xn = xn_ref[...]
        u_ref[0] = jnp.dot(xn, wu_ref[jj], preferred_element_type=F32)
        u_ref[1] = jnp.dot(xn, wu_ref[nj + jj], preferred_element_type=F32)

    def conv(u, s1, s2, c):
        cw = cw_ref[c]
        return ((cb_ref[c] + u * cw[2:3]) + s2 * cw[0:1]) + s1 * cw[1:2]

    def conv_body(u, c):
        return conv(u, pltpu.roll(u, 1, 0), pltpu.roll(u, 2, 0), c)

    def conv_top(u_ref, c):
        prev = carry_ref[c]
        p1, p2 = prev[7:8], prev[6:7]
        u = u_ref[0:top]
        s1 = jnp.where(rowt == 0, p1, pltpu.roll(u, 1, 0))
        s2 = jnp.where(rowt == 0, p2, jnp.where(rowt == 1, p1, pltpu.roll(u, 2, 0)))
        carry_ref[c] = u_ref[tm - 8:tm]
        return conv(u, s1, s2, c)

    def gated(cg, cv):
        return (jax.nn.gelu(cg, approximate=True) * cv).astype(BF16)

    def act(j, u_ref, a_ref):
        a_ref[...] = gated(conv_body(u_ref[0], j), conv_body(u_ref[1], nj + j))
        a_ref[0:top] = gated(conv_top(u_ref.at[0], j), conv_top(u_ref.at[1], nj + j))

    def down(j, a_ref):
        acc_ref[...] += jnp.dot(a_ref[...], wd_ref[j], preferred_element_type=F32)

    up(0, ua_ref)
    up(1, ub_ref)
    act(0, ua_ref, aa_ref)

    def body(jj, carry):
        j = 2 * jj + 1
        up(j + 1, ua_ref)
        act(j, ub_ref, ab_ref)
        down(j - 1, aa_ref)
        up(j + 2, ub_ref)
        act(j + 1, ua_ref, aa_ref)
        down(j, ab_ref)
        return carry

    lax.fori_loop(0, (nj - 3) // 2, body, 0)
    up(nj - 1, ua_ref)
    act(nj - 2, ub_ref, ab_ref)
    down(nj - 3, aa_ref)
    act(nj - 1, ua_ref, aa_ref)
    down(nj - 2, ab_ref)
    down(nj - 1, aa_ref)
    o_ref[...] = h_ref[...] + _rms(acc_ref[...], g3_ref[...])


def conv_ffn(h, g2, w_up, conv_w, conv_b, w_down, g3, seq, tm, tn):
    t, d = h.shape
    dff = w_down.shape[0]
    nj = dff // tn
    assert nj * tn == dff and nj % 2 == 1 and nj >= 3
    wu = w_up.reshape(d, 2 * nj, tn).transpose(1, 0, 2)
    cw = conv_w.reshape(CONV_WIDTH, 2 * nj, tn).transpose(1, 0, 2)
    cb = conv_b.reshape(2 * nj, 1, tn)
    wd = w_down.reshape(nj, tn, d)
    resident = lambda a: pl.BlockSpec(a.shape, (lambda nd: (lambda i: (0,) * nd))(a.ndim))
    return pl.pallas_call(
        functools.partial(_ffn_kernel, tiles_per_seq=seq // tm, nj=nj),
        grid=(t // tm,),
        in_specs=[pl.BlockSpec((tm, d), lambda i: (i, 0)),
                  pl.BlockSpec((1, d), lambda i: (0, 0)),
                  resident(wu), resident(cw), resident(cb), resident(wd),
                  pl.BlockSpec((1, d), lambda i: (0, 0))],
        out_specs=pl.BlockSpec((tm, d), lambda i: (i, 0)),
        out_shape=jax.ShapeDtypeStruct((t, d), F32),
        scratch_shapes=[pltpu.VMEM((tm, d), BF16), pltpu.VMEM((tm, d), F32),
                        pltpu.VMEM((2, tm, tn), F32), pltpu.VMEM((2, tm, tn), F32),
                        pltpu.VMEM((tm, tn), BF16), pltpu.VMEM((tm, tn), BF16),
                        pltpu.VMEM((2 * nj, 8, tn), F32)],
        compiler_params=_cparams("arbitrary"),
        name="conv_ffn",
    )(h, g2.reshape(1, d), wu, cw, cb, wd, g3.reshape(1, d))


def _row_tile(t):
    for tm in (1024, 512, 256, 128):
        if t % tm == 0:
            return tm
    raise ValueError(f"token count {t} is not a multiple of 128")


def mixer_a(h, gains, w_in, w_o, rel_table, bsz, seq):
    t, d = h.shape
    tm = _row_tile(t)
    n_dil = len(DIL_CONFIGS)
    a_in = w_in.shape[1]
    col_scale = np.ones((a_in,), np.float32).reshape(n_dil, 3, HQ)
    col_scale[:, 0] = ATTN_SCALE
    w = (w_in * col_scale.reshape(1, a_in)).astype(BF16)
    idx = np.stack([_banded_idx(1, window // dil, dil) for window, dil in DIL_CONFIGS])
    bias = bias_expand(rel_table, idx)
    outs, lses, dils = [], [], []
    blocks_per_seq = seq // QBLK
    for gi, (window, dil) in enumerate(DIL_CONFIGS):
        assert window // dil <= QBLK and seq % (QBLK * dil) == 0
        tmg = max(tm, QBLK * dil)
        qkv = norm_matmul(h, gains[0], w[:, gi * 3 * HQ:(gi + 1) * 3 * HQ], BF16, tmg, HQ, dil=dil)
        qkv = qkv.reshape(1, t, 3 * HQ)

        def rmap(part, dil=dil):
            return lambda n, i: (0, (n // dil) * blocks_per_seq + i * dil + n % dil, part)

        o, lse = banded_attention(
            qkv, qkv, qkv, bias[gi], n_rows=bsz * dil, n_tiles=seq // (QBLK * dil), tq=QBLK, n_prev=1, kw=HQ,
            q_map=rmap(0), k_map=rmap(1), v_map=rmap(2), o_map=rmap(0),
            out_shape=(1, t, HQ), lse_shape=(1, t, LANES), name=f"dilated_attention_{dil}")
        outs.append(o.reshape(t, HQ))
        lses.append(lse.reshape(t, LANES))
        dils.append(dil)
    e = jnp.asarray(_head_expand_matrix(0), BF16)
    return combine_proj_a(outs, lses, dils, e, w_o.astype(BF16), gains[1], h, min(tm, 512))


def mixer_b(h, gains, w_in, sinks, w_o, rel_table, bsz, seq):
    t, d = h.shape
    tm = _row_tile(t)
    hk = B_KV_HEADS * HEAD_DIM
    n_in = w_in.shape[1]
    col_scale = np.ones((n_in,), np.float32)
    col_scale[:HQ] = ATTN_SCALE
    w = (w_in * col_scale[None]).astype(BF16)
    qkv = norm_matmul(h, gains[0], w, BF16, tm, n_in // 2 if (n_in // 2) % LANES == 0 else n_in)
    q = qkv[:, :HQ].reshape(bsz, seq, HQ)
    kd = _dup_groups(qkv[:, HQ:HQ + hk].reshape(bsz, seq, hk))
    vd = _dup_groups(qkv[:, HQ + hk:].reshape(bsz, seq, hk))
    bias = bias_expand(rel_table, _banded_idx(1, B_WINDOW - 1, 1)[None])[0]
    sink_rows = sinks.astype(F32)
    ident = lambda n, i: (n, i, 0)
    (o,) = banded_attention(q, kd, vd, bias, n_rows=bsz, n_tiles=seq // QBLK, tq=QBLK, n_prev=1,
                            kw=2 * LANES, q_map=ident, k_map=ident, v_map=ident, o_map=ident,
                            out_shape=(bsz, seq, HQ), sinks=sink_rows, stack=8, name="sink_window_attention")
    return _proj_call(_proj_kernel, "sink_proj", tm, [o.reshape(t, HQ)], [], w_o.astype(BF16), gains[1], h)


def mixer_c(h, gains, w_in, cmp_pos, cmp_w1, cmp_w2, w_o, rel_table, bsz, seq):
    t, d = h.shape
    tm = _row_tile(t)
    g = C_KV_HEADS
    hk = g * HEAD_DIM
    rest = w_in.shape[1] - HQ
    rest_pad = -(-rest // LANES) * LANES
    wq = (w_in[:, :HQ] * ATTN_SCALE).astype(BF16)
    wr = jnp.pad(w_in[:, HQ:], ((0, 0), (0, rest_pad - rest))).astype(BF16)
    q = norm_matmul(h, gains[0], wq, BF16, tm, HQ).reshape(bsz, seq, HQ)
    r = norm_matmul(h, gains[0], wr, F32, tm, rest_pad)
    kv = [r[:, i * hk:(i + 1) * hk].reshape(bsz, seq, hk) for i in range(6)]
    gates_raw = r[:, 6 * hk:6 * hk + LANES]

    ncr = seq // CMP_STRIDE
    half = CMP_BLOCK // 2
    assert half == CMP_STRIDE
    chunks = jnp.stack([kv[0], kv[1]]).reshape(2, bsz, ncr, half, g, HEAD_DIM)
    chunks = chunks.transpose(0, 1, 4, 2, 3, 5).reshape(2, bsz, g, ncr, half * HEAD_DIM)
    pos = cmp_pos.reshape(2, 2, half * HEAD_DIM)
    w1 = cmp_w1.reshape(2, 2, half * HEAD_DIM, -1).astype(BF16)
    cmp = compress(chunks, pos, w1, cmp_w2.astype(BF16))
    cmp = cmp.transpose(0, 1, 3, 2, 4).reshape(2, bsz, ncr, hk).astype(BF16)
    kcd, vcd = _dup_groups(cmp[0]), _dup_groups(cmp[1])

    ns = seq // SEL_BLOCK
    assert ns <= HEAD_DIM
    k_sel = min(N_SELECT, ns)
    starts = np.arange(ncr) * CMP_STRIDE
    blk = np.arange(HEAD_DIM)
    ov = ((starts[None, :] < (blk[:, None] + 1) * SEL_BLOCK)
          & (starts[None, :] + CMP_BLOCK > blk[:, None] * SEL_BLOCK)
          & (blk[:, None] < ns) & (starts[None, :] + CMP_BLOCK <= seq))
    o_c, pen = cmp_attention(q, kcd, vcd, jnp.asarray(ov.astype(np.float32), BF16), ns, k_sel)

    sel_bias = bias_expand(rel_table, _sel_bias_idx(seq))
    key_blk = np.arange(seq)[:, None] // SEL_BLOCK == np.arange(HEAD_DIM)[None, :]
    onehot = jnp.broadcast_to(jnp.asarray(np.where(key_blk, -NEG_INF, 0.0), BF16), (bsz, seq, HEAD_DIM))
    ones = jnp.ones((bsz, seq, HEAD_DIM), BF16)
    ks, vs = kv[2].astype(BF16), kv[3].astype(BF16)
    kx = jnp.concatenate([ks[..., :HEAD_DIM], onehot, ks[..., HEAD_DIM:], onehot], axis=-1)
    vx = jnp.concatenate([vs[..., :HEAD_DIM], ones, vs[..., HEAD_DIM:], ones], axis=-1)
    o_s = sel_attention(q, kx, vx, pen, sel_bias)

    n_prev = -(-(C_WINDOW - 1) // QBLK)
    tqw = n_prev * QBLK
    wbias = bias_expand(rel_table, _banded_idx(n_prev, C_WINDOW - 1, 1)[None])[0]
    ident = lambda n, i: (n, i, 0)
    (o_w,) = banded_attention(q, _dup_groups(kv[4].astype(BF16)), _dup_groups(kv[5].astype(BF16)), wbias,
                              n_rows=bsz, n_tiles=seq // tqw, tq=tqw, n_prev=n_prev, kw=2 * LANES,
                              q_map=ident, k_map=ident, v_map=ident, o_map=ident,
                              out_shape=(bsz, seq, HQ), stack=8, name="nsa_window_attention")

    e = jnp.asarray(np.stack([_head_expand_matrix(i * N_HEADS) for i in range(3)]), BF16)
    return _proj_call(_proj_c_kernel, "nsa_gate_proj", min(tm, 512),
                      [o_c.reshape(t, HQ), o_s.reshape(t, HQ), o_w.reshape(t, HQ), gates_raw], [e],
                      w_o.astype(BF16), gains[1], h)


def kernel(x, rel_table, norm_gains, a_w_in, a_w_o, b_w_in, b_sinks, b_w_o, c_w_in, c_cmp_pos, c_cmp_w1,
           c_cmp_w2, c_w_o, ffn_w_up, ffn_conv_w, ffn_conv_b, ffn_w_down):
    bsz, seq, d = x.shape
    depth = norm_gains.shape[0]
    h = x.reshape(bsz * seq, d)
    tm = _row_tile(seq)
    for i in range(depth):
        kind, j = i % 3, i // 3
        g = norm_gains[i]
        if kind == 0:
            h = mixer_a(h, g, a_w_in[j], a_w_o[j], rel_table, bsz, seq)
        elif kind == 1:
            h = mixer_b(h, g, b_w_in[j], b_sinks[j], b_w_o[j], rel_table, bsz, seq)
        else:
            h = mixer_c(h, g, c_w_in[j], c_cmp_pos[j], c_cmp_w1[j], c_cmp_w2[j], c_w_o[j], rel_table, bsz, seq)
        h = conv_ffn(h, g[2], ffn_w_up[i].astype(BF16), ffn_conv_w[i], ffn_conv_b[i],
                     ffn_w_down[i].astype(BF16), g[3], seq, min(tm, 512), 256)
    return h.reshape(bsz, seq, d)
```

```python
import functools
import math

import numpy as np
import jax
import jax.numpy as jnp
from jax import lax
from jax.experimental import pallas as pl
from jax.experimental.pallas import tpu as pltpu

F32 = jnp.float32
BF16 = jnp.bfloat16

N_HEADS = 16
HEAD_DIM = 64
HQ = N_HEADS * HEAD_DIM
LANES = 128
ATTN_SCALE = HEAD_DIM ** -0.5
NUM_BUCKETS = 32
MAX_DISTANCE = 2048
RMS_EPS = 1e-6
NEG_INF = -1e30
FORCE_SCORE = 1e9
DIL_CONFIGS = ((128, 1), (512, 4), (2048, 16))
B_KV_HEADS = 2
B_WINDOW = 128
C_KV_HEADS = 2
CMP_BLOCK = 32
CMP_STRIDE = 16
SEL_BLOCK = 64
N_SELECT = 16
C_WINDOW = 512
CONV_WIDTH = 3
QBLK = 128
SEL_CHUNK = 512
VMEM_LIMIT = 56 * 1024 * 1024


def _cparams(*sem):
    return pltpu.CompilerParams(dimension_semantics=sem, vmem_limit_bytes=VMEM_LIMIT)


def _t5_bucket_np(dist):
    max_exact = NUM_BUCKETS // 2
    d = np.maximum(dist, 0)
    df = np.maximum(d, 1).astype(np.float64)
    large = max_exact + np.floor(np.log(df / max_exact) / math.log(MAX_DISTANCE / max_exact)
                                 * (NUM_BUCKETS - max_exact) + 1e-9).astype(np.int64)
    large = np.minimum(large, NUM_BUCKETS - 1)
    return np.where(d < max_exact, d, large).astype(np.int32)


def _rms(x, g):
    ms = jnp.mean(x * x, axis=-1, keepdims=True)
    return (x * lax.rsqrt(ms + RMS_EPS)) * g


def _split3(w):
    hi = w.astype(BF16)
    r1 = w - hi.astype(F32)
    mid = r1.astype(BF16)
    lo = (r1 - mid.astype(F32)).astype(BF16)
    return hi, mid, lo


def _dot3(w, e):
    hi, mid, lo = _split3(w)
    d = functools.partial(jnp.dot, preferred_element_type=F32)
    return (d(hi, e) + d(mid, e)) + d(lo, e)


def _dot_nt(a, b):
    return lax.dot_general(a, b, (((1,), (1,)), ((), ())), preferred_element_type=F32)


def _stack_heads(q_tile, heads):
    lane = lax.broadcasted_iota(jnp.int32, (QBLK, LANES), 1)
    lo = lane < HEAD_DIM
    pieces = []
    for h in heads:
        q2 = q_tile(h // 2)
        keep = lo if h % 2 == 0 else jnp.logical_not(lo)
        pieces.append(jnp.where(keep, q2, jnp.zeros_like(q2)))
    return jnp.concatenate(pieces, axis=0)


def _merge_pairs(o, n_heads):
    lane = lax.broadcasted_iota(jnp.int32, (QBLK, LANES), 1)
    lo = lane < HEAD_DIM
    out = []
    for t in range(0, n_heads, 2):
        out.append(jnp.where(lo, o[t * QBLK:(t + 1) * QBLK], o[(t + 1) * QBLK:(t + 2) * QBLK]))
    return out


def _bias_kernel(tab_ref, idx_ref, o_ref):
    h = pl.program_id(1)
    idx = idx_ref[0]
    acc = jnp.full(idx.shape, NEG_INF, F32)
    for b in range(NUM_BUCKETS):
        acc = jnp.where(idx == b, tab_ref[b, h], acc)
    o_ref[0, 0] = acc


def bias_expand(table, idx):
    n, r, c = idx.shape
    return pl.pallas_call(
        _bias_kernel,
        grid=(n, N_HEADS),
        in_specs=[pl.BlockSpec(memory_space=pltpu.SMEM),
                  pl.BlockSpec((1, r, c), lambda i, h: (i, 0, 0))],
        out_specs=pl.BlockSpec((1, 1, r, c), lambda i, h: (i, h, 0, 0)),
        out_shape=jax.ShapeDtypeStruct((n, N_HEADS, r, c), F32),
        compiler_params=_cparams("parallel", "parallel"),
        name="bias_expand",
    )(table, jnp.asarray(idx))


def _banded_idx(n_prev, max_dist, stride):
    kb = (n_prev + 1) * QBLK
    dist = np.arange(QBLK)[:, None] + n_prev * QBLK - np.arange(kb)[None, :]
    valid = (dist >= 0) & (dist <= max_dist)
    idx = np.where(valid, _t5_bucket_np(dist * stride), -1).astype(np.int32)
    if n_prev > 1:
        return idx[None]
    first = np.where(np.arange(kb)[None, :] < n_prev * QBLK, -1, idx).astype(np.int32)
    return np.stack([idx, first])


def _norm_mm_kernel(x_ref, g_ref, w_ref, o_ref, xn_ref, *chunk_ref, dil):
    @pl.when(pl.program_id(1) == 0)
    def _():
        if dil == 1:
            xn_ref[...] = _rms(x_ref[...], g_ref[...]).astype(xn_ref.dtype)
        else:
            (xs_ref,) = chunk_ref
            nc = xs_ref.shape[0]
            for c in range(nc):
                xs_ref[c] = x_ref[:, c * LANES:(c + 1) * LANES]
            span = QBLK * dil
            for u in range(x_ref.shape[0] // span):
                for r in range(dil):
                    rows = jnp.concatenate(
                        [xs_ref[c, pl.ds(u * span + r, QBLK, stride=dil), :] for c in range(nc)], axis=1)
                    dst = u * span + r * QBLK
                    xn_ref[dst:dst + QBLK, :] = _rms(rows, g_ref[...]).astype(xn_ref.dtype)

    o_ref[...] = jnp.dot(xn_ref[...], w_ref[...], preferred_element_type=F32).astype(o_ref.dtype)


def norm_matmul(x, gain, w, out_dtype, tm, tn, dil=1):
    t, d = x.shape
    n = w.shape[1]
    assert tm % (QBLK * dil) == 0
    scratch = [pltpu.VMEM((tm, d), BF16)]
    if dil > 1:
        scratch.append(pltpu.VMEM((d // LANES, tm, LANES), F32))
    return pl.pallas_call(
        functools.partial(_norm_mm_kernel, dil=dil),
        grid=(t // tm, n // tn),
        in_specs=[pl.BlockSpec((tm, d), lambda i, j: (i, 0)),
                  pl.BlockSpec((1, d), lambda i, j: (0, 0)),
                  pl.BlockSpec((d, tn), lambda i, j: (0, j))],
        out_specs=pl.BlockSpec((tm, tn), lambda i, j: (i, j)),
        out_shape=jax.ShapeDtypeStruct((t, n), out_dtype),
        scratch_shapes=scratch,
        compiler_params=_cparams("parallel", "arbitrary"),
        name="norm_matmul",
    )(x, gain.reshape(1, d), w)


def _finish_proj(y, w_ref, g_ref, h_ref, o_ref):
    z = jnp.dot(y.astype(BF16), w_ref[...], preferred_element_type=F32)
    o_ref[...] = h_ref[...] + _rms(z, g_ref[...])


def _proj_kernel(y_ref, w_ref, g_ref, h_ref, o_ref):
    _finish_proj(y_ref[...], w_ref, g_ref, h_ref, o_ref)


def _unpermute(src_ref, dst_ref, dil, tm, i):
    span = QBLK * dil
    nc = dst_ref.shape[0]
    if span <= tm:
        for u in range(tm // span):
            for r in range(dil):
                lo = u * span + r * QBLK
                for c in range(nc):
                    dst_ref[c, pl.ds(u * span + r, QBLK, stride=dil), :] = src_ref[lo:lo + QBLK,
                                                                                   c * LANES:(c + 1) * LANES]
    else:
        per = tm // dil
        off = (i % (span // tm)) * per
        for r in range(dil):
            lo = pl.multiple_of(r * QBLK + off, 8)
            for c in range(nc):
                dst_ref[c, pl.ds(r, per, stride=dil), :] = src_ref[pl.ds(lo, per), c * LANES:(c + 1) * LANES]
    return jnp.concatenate([dst_ref[c] for c in range(nc)], axis=1) if nc > 1 else dst_ref[0]


def _proj_a_kernel(*refs, dils, tm):
    n = len(dils)
    o_refs, l_refs = refs[:n], refs[n:2 * n]
    e_ref, w_ref, g_ref, h_ref, out_ref = refs[2 * n:2 * n + 5]
    scratch = refs[2 * n + 5:]
    i = pl.program_id(0)
    outs, lses = [], []
    si = 0
    for gi, dil in enumerate(dils):
        if dil == 1:
            outs.append(o_refs[gi][...])
            lses.append(l_refs[gi][...])
        else:
            outs.append(_unpermute(o_refs[gi], scratch[si], dil, tm, i))
            lses.append(_unpermute(l_refs[gi], scratch[si + 1], dil, tm, i))
            si += 2
    mx = functools.reduce(jnp.maximum, lses)
    es = [jnp.exp(l - mx) for l in lses]
    inv = 1.0 / functools.reduce(lambda a, b: a + b, es)
    e = e_ref[...]
    y = outs[0] * _dot3(es[0] * inv, e)
    for gi in range(1, n):
        y = y + outs[gi] * _dot3(es[gi] * inv, e)
    _finish_proj(y, w_ref, g_ref, h_ref, out_ref)


def combine_proj_a(outs, lses, dils, e, w, gain, h, tm):
    t, d = h.shape

    def row_spec(width, dil):
        rows = max(tm, QBLK * dil)
        return pl.BlockSpec((rows, width), lambda i, q=rows // tm: (i // q, 0))

    in_specs = [row_spec(HQ, dil) for dil in dils] + [row_spec(LANES, dil) for dil in dils] + [
        pl.BlockSpec(e.shape, lambda i: (0, 0)),
        pl.BlockSpec(w.shape, lambda i: (0, 0)),
        pl.BlockSpec((1, d), lambda i: (0, 0)),
        pl.BlockSpec((tm, d), lambda i: (i, 0))]
    scratch = []
    for dil in dils:
        if dil > 1:
            scratch += [pltpu.VMEM((HQ // LANES, tm, LANES), F32), pltpu.VMEM((1, tm, LANES), F32)]
    return pl.pallas_call(
        functools.partial(_proj_a_kernel, dils=tuple(dils), tm=tm),
        grid=(t // tm,),
        in_specs=in_specs,
        out_specs=pl.BlockSpec((tm, d), lambda i: (i, 0)),
        out_shape=jax.ShapeDtypeStruct((t, d), F32),
        scratch_shapes=scratch,
        compiler_params=_cparams("arbitrary"),
        name="dilated_combine_proj",
    )(*outs, *lses, e, w, gain.reshape(1, d), h)


def _proj_c_kernel(oc_ref, os_ref, ow_ref, gr_ref, e_ref, w_ref, g_ref, h_ref, o_ref):
    sig = jax.nn.sigmoid(gr_ref[...])
    y = _dot3(sig, e_ref[0]) * oc_ref[...]
    y = y + _dot3(sig, e_ref[1]) * os_ref[...]
    y = y + _dot3(sig, e_ref[2]) * ow_ref[...]
    _finish_proj(y, w_ref, g_ref, h_ref, o_ref)


def _proj_call(kernel, name, tm, row_inputs, const_inputs, w, gain, h):
    t, d = h.shape
    k = w.shape[0]
    row_specs = [pl.BlockSpec((tm, a.shape[1]), lambda i: (i, 0)) for a in row_inputs]
    const_specs = [pl.BlockSpec(a.shape, (lambda nd: (lambda i: (0,) * nd))(a.ndim)) for a in const_inputs]
    return pl.pallas_call(
        kernel,
        grid=(t // tm,),
        in_specs=row_specs + const_specs + [
            pl.BlockSpec((k, d), lambda i: (0, 0)),
            pl.BlockSpec((1, d), lambda i: (0, 0)),
            pl.BlockSpec((tm, d), lambda i: (i, 0))],
        out_specs=pl.BlockSpec((tm, d), lambda i: (i, 0)),
        out_shape=jax.ShapeDtypeStruct((t, d), F32),
        compiler_params=_cparams("parallel"),
        name=name,
    )(*row_inputs, *const_inputs, w, gain.reshape(1, d), h)


def _head_expand_matrix(row_offset):
    e = np.zeros((LANES, HQ), np.float32)
    for h in range(N_HEADS):
        e[row_offset + h, h * HEAD_DIM:(h + 1) * HEAD_DIM] = 1.0
    return e


def _banded_kernel(*refs, tq, n_prev, n_chunks, stack, with_sinks, with_lse):
    q_ref, kp_ref, kc_ref, vp_ref, vc_ref, bias_ref = refs[:6]
    pos = 6
    sink_ref = None
    if with_sinks:
        sink_ref = refs[pos]
        pos += 1
    o_ref = refs[pos]
    pos += 1
    lse_ref = None
    if with_lse:
        lse_ref = refs[pos]
        pos += 1
    kbuf, vbuf = refs[pos], refs[pos + 1]

    i = pl.program_id(1)
    hpc = N_HEADS // n_chunks
    kb = (n_prev + 1) * QBLK
    kbuf[0:tq] = kp_ref[0]
    kbuf[tq:2 * tq] = kc_ref[0]
    vbuf[0:tq] = vp_ref[0]
    vbuf[tq:2 * tq] = vc_ref[0]
    lane = lax.broadcasted_iota(jnp.int32, (QBLK, LANES), 1)
    head_row = lax.broadcasted_iota(jnp.int32, (stack * QBLK, 1), 0)

    for s in range(tq // QBLK):
        r0 = s * QBLK
        k0 = tq + r0 - n_prev * QBLK
        variants = bias_ref.shape[0] > 1
        if variants:
            first = jnp.where(i == 0, 1, 0)
        else:
            col = lax.broadcasted_iota(jnp.int32, (1, kb), 1) + k0
            kmask = jnp.where(jnp.logical_and(i == 0, col < tq), NEG_INF, 0.0).astype(F32)
        lse_acc = jnp.zeros((QBLK, LANES), F32)
        for h0 in range(0, N_HEADS, stack):
            heads = list(range(h0, h0 + stack))
            c = h0 // hpc
            qst = _stack_heads(lambda p: q_ref[0, r0:r0 + QBLK, p * LANES:(p + 1) * LANES], heads)
            kx = kbuf[k0:k0 + kb, c * LANES:(c + 1) * LANES]
            vx = vbuf[k0:k0 + kb, c * LANES:(c + 1) * LANES]
            sc = _dot_nt(qst, kx)
            if variants:
                sc = sc + bias_ref[first, h0:h0 + stack].reshape(stack * QBLK, kb)
            else:
                sc = sc + bias_ref[0, h0:h0 + stack].reshape(stack * QBLK, kb) + kmask
            m = jnp.max(sc, axis=-1, keepdims=True)
            if with_sinks:
                sk = sink_ref[h0 + stack - 1]
                for t in range(stack - 2, -1, -1):
                    sk = jnp.where(head_row < (t + 1) * QBLK, sink_ref[h0 + t], sk)
                m = jnp.maximum(m, sk)
            p = jnp.exp(sc - m)
            den = jnp.sum(p, axis=-1, keepdims=True)
            norm = den + jnp.exp(sk - m) if with_sinks else den
            o = jnp.dot(p.astype(BF16), vx, preferred_element_type=F32) * (1.0 / norm)
            for t2, blk in enumerate(_merge_pairs(o, stack)):
                pidx = h0 // 2 + t2
                o_ref[0, r0:r0 + QBLK, pidx * LANES:(pidx + 1) * LANES] = blk.astype(o_ref.dtype)
            if with_lse:
                lse = m + jnp.log(den)
                for t, h in enumerate(heads):
                    lse_acc = jnp.where(lane == h, lse[t * QBLK:(t + 1) * QBLK], lse_acc)
        if with_lse:
            lse_ref[0, r0:r0 + QBLK, :] = lse_acc


def banded_attention(q_arr, k_arr, v_arr, bias, *, n_rows, n_tiles, tq, n_prev, kw,
                     q_map, k_map, v_map, o_map, out_shape, lse_shape=None, sinks=None, stack=2, name):
    n_chunks = kw // LANES
    assert stack % 2 == 0 and (N_HEADS // n_chunks) % stack == 0
    kb = (n_prev + 1) * QBLK
    with_sinks = sinks is not None
    with_lse = lse_shape is not None

    def prev(fn):
        return lambda n, i: fn(n, jnp.maximum(i - 1, 0))

    in_specs = [pl.BlockSpec((1, tq, HQ), q_map),
                pl.BlockSpec((1, tq, kw), prev(k_map)),
                pl.BlockSpec((1, tq, kw), k_map),
                pl.BlockSpec((1, tq, kw), prev(v_map)),
                pl.BlockSpec((1, tq, kw), v_map),
                pl.BlockSpec(bias.shape, lambda n, i: (0, 0, 0, 0))]
    assert bias.shape[1:] == (N_HEADS, QBLK, kb) and (bias.shape[0] == 1 or tq == QBLK)
    args = [q_arr, k_arr, k_arr, v_arr, v_arr, bias]
    if with_sinks:
        in_specs.append(pl.BlockSpec(memory_space=pltpu.SMEM))
        args.append(sinks)
    out_specs = [pl.BlockSpec((1, tq, HQ), o_map)]
    out_shapes = [jax.ShapeDtypeStruct(out_shape, F32)]
    if with_lse:
        out_specs.append(pl.BlockSpec((1, tq, LANES), o_map))
        out_shapes.append(jax.ShapeDtypeStruct(lse_shape, F32))
    res = pl.pallas_call(
        functools.partial(_banded_kernel, tq=tq, n_prev=n_prev, n_chunks=n_chunks, stack=stack,
                          with_sinks=with_sinks, with_lse=with_lse),
        grid=(n_rows, n_tiles),
        in_specs=in_specs,
        out_specs=out_specs,
        out_shape=out_shapes,
        scratch_shapes=[pltpu.VMEM((2 * tq, kw), BF16), pltpu.VMEM((2 * tq, kw), BF16)],
        compiler_params=_cparams("parallel", "arbitrary"),
        name=name,
    )(*args)
    return res


def _dup_groups(x):
    g0, g1 = x[..., :HEAD_DIM], x[..., HEAD_DIM:]
    return jnp.concatenate([g0, g0, g1, g1], axis=-1)


def _compress_kernel(ch_ref, pos_ref, w1_ref, w2_ref, o_ref):
    ch = ch_ref[0, 0, 0]
    rows = ch.shape[0]
    posv = pos_ref[0]
    a = jnp.dot((ch + posv[0:1]).astype(BF16), w1_ref[0, 0], preferred_element_type=F32)
    b = jnp.dot((ch + posv[1:2]).astype(BF16), w1_ref[0, 1], preferred_element_type=F32)
    hid = a + pltpu.roll(b, rows - 1, 0)
    act = jax.nn.gelu(hid, approximate=True)
    o_ref[0, 0, 0] = jnp.dot(act.astype(BF16), w2_ref[0], preferred_element_type=F32)


def compress(chunks, pos, w1, w2):
    _, b, g, rows, width = chunks.shape
    hid = w1.shape[-1]
    return pl.pallas_call(
        _compress_kernel,
        grid=(2, b, g),
        in_specs=[pl.BlockSpec((1, 1, 1, rows, width), lambda i, bb, gg: (i, bb, gg, 0, 0)),
                  pl.BlockSpec((1, 2, width), lambda i, bb, gg: (i, 0, 0)),
                  pl.BlockSpec((1, 2, width, hid), lambda i, bb, gg: (i, 0, 0, 0)),
                  pl.BlockSpec((1, hid, HEAD_DIM), lambda i, bb, gg: (i, 0, 0))],
        out_specs=pl.BlockSpec((1, 1, 1, rows, HEAD_DIM), lambda i, bb, gg: (i, bb, gg, 0, 0)),
        out_shape=jax.ShapeDtypeStruct((2, b, g, rows, HEAD_DIM), F32),
        compiler_params=_cparams("parallel", "parallel", "parallel"),
        name="nsa_compress",
    )(chunks, pos, w1, w2)


def _cmp_attn_kernel(q_ref, kc_ref, vc_ref, ov_ref, place_ref, o_ref, sel_ref, *, n_sel_blocks, k_sel):
    qi = pl.program_id(1)
    ncr = kc_ref.shape[1]
    hpc = N_HEADS // C_KV_HEADS
    qpos = qi * QBLK + lax.broadcasted_iota(jnp.int32, (QBLK, 1), 0)
    cidx = lax.broadcasted_iota(jnp.int32, (1, ncr), 1)
    valid = (cidx * CMP_STRIDE + (CMP_BLOCK - 1)) <= qpos
    maskc = jnp.where(valid, 0.0, NEG_INF).astype(F32)
    anyv = (qpos >= CMP_BLOCK - 1).astype(F32)
    blk_id = lax.broadcasted_iota(jnp.int32, (HEAD_DIM, QBLK), 0)
    cur = (qi * QBLK + lax.broadcasted_iota(jnp.int32, (HEAD_DIM, QBLK), 1)) // SEL_BLOCK
    forced = jnp.logical_or(jnp.logical_or(blk_id == 0, blk_id == cur), blk_id == cur - 1)
    allowed = blk_id <= cur
    for g in range(C_KV_HEADS):
        heads = list(range(g * hpc, (g + 1) * hpc))
        qst = _stack_heads(lambda p: q_ref[0, :, p * LANES:(p + 1) * LANES], heads)
        sc = _dot_nt(qst, kc_ref[0, :, g * LANES:(g + 1) * LANES])
        sc3 = sc.reshape(hpc, QBLK, ncr) + maskc[None]
        m = jnp.max(sc3, axis=-1, keepdims=True)
        e = jnp.exp(sc3 - m)
        ssum = jnp.sum(e, axis=-1, keepdims=True)
        p = (e * (1.0 / ssum)) * anyv[None]
        o = jnp.dot(p.reshape(hpc * QBLK, ncr).astype(BF16), vc_ref[0, :, g * LANES:(g + 1) * LANES],
                    preferred_element_type=F32)
        for t2, blk in enumerate(_merge_pairs(o, hpc)):
            pidx = (g * hpc) // 2 + t2
            o_ref[0, :, pidx * LANES:(pidx + 1) * LANES] = blk
        hi, mid, lo = _split3(jnp.sum(p, axis=0))
        ovt = ov_ref[...]
        imp = (_dot_nt(ovt, hi) + _dot_nt(ovt, mid)) + _dot_nt(ovt, lo)
        score = jnp.where(forced, FORCE_SCORE, jnp.where(allowed, imp, NEG_INF))
        bits = pltpu.bitcast(score, jnp.int32)
        key = jnp.where(bits < 0, bits ^ jnp.int32(0x7FFFFFFF), bits)
        key_m1 = key - 1
        rank = jnp.zeros((HEAD_DIM, QBLK), jnp.int32)
        for i in range(n_sel_blocks):
            thr = jnp.where(blk_id > i, key_m1, key)
            rank = rank + jnp.where(key[i:i + 1, :] > thr, 1, 0)
        keep = jnp.logical_and(rank < k_sel, blk_id < n_sel_blocks)
        pen_t = jnp.where(keep, 0.0, -1.0).astype(BF16)
        pen = lax.dot_general(pen_t, place_ref[...], (((0,), (0,)), ((), ())), preferred_element_type=F32)
        sel_ref[0, :, g * LANES:(g + 1) * LANES] = pen.astype(sel_ref.dtype)


def cmp_attention(q, kcd, vcd, ov, n_sel_blocks, k_sel):
    b, s, _ = q.shape
    ncr = kcd.shape[1]
    place = np.zeros((HEAD_DIM, LANES), np.float32)
    place[np.arange(HEAD_DIM), HEAD_DIM + np.arange(HEAD_DIM)] = 1.0
    return pl.pallas_call(
        functools.partial(_cmp_attn_kernel, n_sel_blocks=n_sel_blocks, k_sel=k_sel),
        grid=(b, s // QBLK),
        in_specs=[pl.BlockSpec((1, QBLK, HQ), lambda bb, i: (bb, i, 0)),
                  pl.BlockSpec((1, ncr, 2 * LANES), lambda bb, i: (bb, 0, 0)),
                  pl.BlockSpec((1, ncr, 2 * LANES), lambda bb, i: (bb, 0, 0)),
                  pl.BlockSpec((HEAD_DIM, ncr), lambda bb, i: (0, 0)),
                  pl.BlockSpec((HEAD_DIM, LANES), lambda bb, i: (0, 0))],
        out_specs=[pl.BlockSpec((1, QBLK, HQ), lambda bb, i: (bb, i, 0)),
                   pl.BlockSpec((1, QBLK, 2 * LANES), lambda bb, i: (bb, i, 0))],
        out_shape=[jax.ShapeDtypeStruct((b, s, HQ), F32),
                   jax.ShapeDtypeStruct((b, s, 2 * LANES), BF16)],
        compiler_params=_cparams("parallel", "parallel"),
        name="nsa_cmp_attention",
    )(q, kcd, vcd, ov, jnp.asarray(place, BF16))


def _sel_attn_kernel(q_ref, pen_ref, k_ref, v_ref, bias_ref, o_ref, qst_ref, m_ref, acc_ref, *, nbt):
    qi = pl.program_id(2)
    hpc = N_HEADS // C_KV_HEADS
    nsub = SEL_CHUNK // QBLK
    lane = lax.broadcasted_iota(jnp.int32, (QBLK, LANES), 1)
    lo = lane < HEAD_DIM
    pen = pen_ref[0].astype(F32)
    for t in range(hpc):
        q2 = q_ref[0, :, (t // 2) * LANES:(t // 2 + 1) * LANES].astype(F32)
        if t % 2:
            q2 = pltpu.roll(q2, HEAD_DIM, 1)
        qst_ref[t * QBLK:(t + 1) * QBLK, :] = jnp.where(lo, q2, pen).astype(BF16)
    m_ref[...] = jnp.full(m_ref.shape, NEG_INF, F32)
    acc_ref[...] = jnp.zeros(acc_ref.shape, F32)

    def body(c, carry):
        k0 = pl.multiple_of(c * SEL_CHUNK, SEL_CHUNK)
        s = _dot_nt(qst_ref[...], k_ref[0, pl.ds(k0, SEL_CHUNK), :])
        tiles = []
        for u in range(nsub):
            off = qi - nsub * c - u
            idx = jnp.where(off < 0, nbt, jnp.minimum(off, nbt - 1))
            tiles.append(s[:, u * QBLK:(u + 1) * QBLK] + bias_ref[idx].reshape(hpc * QBLK, QBLK))
        m_old = m_ref[...]
        m_new = jnp.maximum(m_old, jnp.max(functools.reduce(jnp.maximum, tiles), axis=-1, keepdims=True))
        p = jnp.concatenate([jnp.exp(tl - m_new).astype(BF16) for tl in tiles], axis=1)
        pv = jnp.dot(p, v_ref[0, pl.ds(k0, SEL_CHUNK), :], preferred_element_type=F32)
        acc_ref[...] = jnp.exp(m_old - m_new) * acc_ref[...] + pv
        m_ref[...] = m_new
        return carry

    lax.fori_loop(0, qi // nsub + 1, body, 0)
    acc = acc_ref[...]
    rolled = pltpu.roll(acc, HEAD_DIM, 1)
    for t2 in range(hpc // 2):
        ev = slice(2 * t2 * QBLK, (2 * t2 + 1) * QBLK)
        od = slice((2 * t2 + 1) * QBLK, (2 * t2 + 2) * QBLK)
        even = acc[ev] * (1.0 / rolled[ev])
        odd = rolled[od] * (1.0 / acc[od])
        o_ref[0, :, t2 * LANES:(t2 + 1) * LANES] = jnp.where(lo, even, odd)


def sel_attention(q, kx, vx, pen, bias_tiles):
    b, s, _ = q.shape
    hpc = N_HEADS // C_KV_HEADS
    gw = hpc * HEAD_DIM
    nbt = bias_tiles.shape[0] - 1
    assert s % SEL_CHUNK == 0
    return pl.pallas_call(
        functools.partial(_sel_attn_kernel, nbt=nbt),
        grid=(b, C_KV_HEADS, s // QBLK),
        in_specs=[pl.BlockSpec((1, QBLK, gw), lambda bb, g, i: (bb, i, g)),
                  pl.BlockSpec((1, QBLK, LANES), lambda bb, g, i: (bb, i, g)),
                  pl.BlockSpec((1, s, LANES), lambda bb, g, i: (bb, 0, g)),
                  pl.BlockSpec((1, s, LANES), lambda bb, g, i: (bb, 0, g)),
                  pl.BlockSpec((nbt + 1, hpc, QBLK, QBLK), lambda bb, g, i: (0, g, 0, 0))],
        out_specs=pl.BlockSpec((1, QBLK, gw), lambda bb, g, i: (bb, i, g)),
        out_shape=jax.ShapeDtypeStruct((b, s, HQ), F32),
        scratch_shapes=[pltpu.VMEM((hpc * QBLK, LANES), BF16),
                        pltpu.VMEM((hpc * QBLK, LANES), F32),
                        pltpu.VMEM((hpc * QBLK, LANES), F32)],
        compiler_params=_cparams("parallel", "parallel", "arbitrary"),
        name="nsa_sel_attention",
    )(q, pen, kx, vx, bias_tiles)


def _sel_bias_idx(s):
    nqt = s // QBLK
    far = -(-(int(np.argmax(_t5_bucket_np(np.arange(4 * MAX_DISTANCE)) == NUM_BUCKETS - 1)) + QBLK) // QBLK)
    nbt = min(nqt, far + 1)
    d0 = np.arange(nbt)[:, None, None] * QBLK
    dist = d0 + np.arange(QBLK)[None, :, None] - np.arange(QBLK)[None, None, :]
    idx = np.where(dist >= 0, _t5_bucket_np(dist), -1).astype(np.int32)
    return np.concatenate([idx, np.full((1, QBLK, QBLK), -1, np.int32)])


def _ffn_kernel(h_ref, g2_ref, wu_ref, cw_ref, cb_ref, wd_ref, g3_ref,
                o_ref, xn_ref, acc_ref, ua_ref, ub_ref, aa_ref, ab_ref, carry_ref, *, tiles_per_seq, nj):
    i = pl.program_id(0)
    tm, tn = xn_ref.shape[0], wu_ref.shape[2]
    xn_ref[...] = _rms(h_ref[...], g2_ref[...]).astype(xn_ref.dtype)
    acc_ref[...] = jnp.zeros(acc_ref.shape, F32)

    @pl.when(i % tiles_per_seq == 0)
    def _():
        carry_ref[...] = jnp.zeros(carry_ref.shape, F32)

    top = 16
    rowt = lax.broadcasted_iota(jnp.int32, (top, tn), 0)

    def up(jj, u_ref):
        xn = xn_ref[...]
        u_ref[0] = jnp.dot(xn, wu_ref[jj], preferred_element_type=F32)
        u_ref[1] = jnp.dot(xn, wu_ref[nj + jj], preferred_element_type=F32)

    def conv(u, s1, s2, c):
        cw = cw_ref[c]
        return ((cb_ref[c] + u * cw[2:3]) + s2 * cw[0:1]) + s1 * cw[1:2]

    def conv_body(u, c):
        return conv(u, pltpu.roll(u, 1, 0), pltpu.roll(u, 2, 0), c)

    def conv_top(u_ref, c):
        prev = carry_ref[c]
        p1, p2 = prev[7:8], prev[6:7]
        u = u_ref[0:top]
        s1 = jnp.where(rowt == 0, p1, pltpu.roll(u, 1, 0))
        s2 = jnp.where(rowt == 0, p2, jnp.where(rowt == 1, p1, pltpu.roll(u, 2, 0)))
        carry_ref[c] = u_ref[tm - 8:tm]
        return conv(u, s1, s2, c)

    def gated(cg, cv):
        return (jax.nn.gelu(cg, approximate=True) * cv).astype(BF16)

    def act(j, u_ref, a_ref):
        a_ref[...] = gated(conv_body(u_ref[0], j), conv_body(u_ref[1], nj + j))
        a_ref[0:top] = gated(conv_top(u_ref.at[0], j), conv_top(u_ref.at[1], nj + j))

    def down(j, a_ref):
        acc_ref[...] += jnp.dot(a_ref[...], wd_ref[j], preferred_element_type=F32)

    up(0, ua_ref)
    up(1, ub_ref)
    act(0, ua_ref, aa_ref)

    for jj in range((nj - 3) // 2):
        j = 2 * jj + 1
        up(j + 1, ua_ref)
        act(j, ub_ref, ab_ref)
        down(j - 1, aa_ref)
        up(j + 2, ub_ref)
        act(j + 1, ua_ref, aa_ref)
        down(j, ab_ref)
    up(nj - 1, ua_ref)
    act(nj - 2, ub_ref, ab_ref)
    down(nj - 3, aa_ref)
    act(nj - 1, ua_ref, aa_ref)
    down(nj - 2, ab_ref)
    down(nj - 1, aa_ref)
    o_ref[...] = h_ref[...] + _rms(acc_ref[...], g3_ref[...])


def conv_ffn(h, g2, w_up, conv_w, conv_b, w_down, g3, seq, tm, tn):
    t, d = h.shape
    dff = w_down.shape[0]
    nj = dff // tn
    assert nj * tn == dff and nj % 2 == 1 and nj >= 3
    wu = w_up.reshape(d, 2 * nj, tn).transpose(1, 0, 2)
    cw = conv_w.reshape(CONV_WIDTH, 2 * nj, tn).transpose(1, 0, 2)
    cb = conv_b.reshape(2 * nj, 1, tn)
    wd = w_down.reshape(nj, tn, d)
    resident = lambda a: pl.BlockSpec(a.shape, (lambda nd: (lambda i: (0,) * nd))(a.ndim))
    return pl.pallas_call(
        functools.partial(_ffn_kernel, tiles_per_seq=seq // tm, nj=nj),
        grid=(t // tm,),
        in_specs=[pl.BlockSpec((tm, d), lambda i: (i, 0)),
                  pl.BlockSpec((1, d), lambda i: (0, 0)),
                  resident(wu), resident(cw), resident(cb), resident(wd),
                  pl.BlockSpec((1, d), lambda i: (0, 0))],
        out_specs=pl.BlockSpec((tm, d), lambda i: (i, 0)),
        out_shape=jax.ShapeDtypeStruct((t, d), F32),
        scratch_shapes=[pltpu.VMEM((tm, d), BF16), pltpu.VMEM((tm, d), F32),
                        pltpu.VMEM((2, tm, tn), F32), pltpu.VMEM((2, tm, tn), F32),
                        pltpu.VMEM((tm, tn), BF16), pltpu.VMEM((tm, tn), BF16),
                        pltpu.VMEM((2 * nj, 8, tn), F32)],
        compiler_params=_cparams("arbitrary"),
        name="conv_ffn",
    )(h, g2.reshape(1, d), wu, cw, cb, wd, g3.reshape(1, d))


def _row_tile(t):
    for tm in (1024, 512, 256, 128):
        if t % tm == 0:
            return tm
    raise ValueError(f"token count {t} is not a multiple of 128")


def mixer_a(h, gains, w_in, w_o, rel_table, bsz, seq):
    t, d = h.shape
    tm = _row_tile(t)
    n_dil = len(DIL_CONFIGS)
    a_in = w_in.shape[1]
    col_scale = np.ones((a_in,), np.float32).reshape(n_dil, 3, HQ)
    col_scale[:, 0] = ATTN_SCALE
    w = (w_in * col_scale.reshape(1, a_in)).astype(BF16)
    idx = np.concatenate([_banded_idx(1, window // dil, dil) for window, dil in DIL_CONFIGS])
    bias = bias_expand(rel_table, idx)
    outs, lses, dils = [], [], []
    blocks_per_seq = seq // QBLK
    for gi, (window, dil) in enumerate(DIL_CONFIGS):
        assert window // dil <= QBLK and seq % (QBLK * dil) == 0
        tmg = max(tm, QBLK * dil)
        qkv = norm_matmul(h, gains[0], w[:, gi * 3 * HQ:(gi + 1) * 3 * HQ], BF16, tmg, HQ, dil=dil)
        qkv = qkv.reshape(1, t, 3 * HQ)

        def rmap(part, dil=dil):
            return lambda n, i: (0, (n // dil) * blocks_per_seq + i * dil + n % dil, part)

        o, lse = banded_attention(
            qkv, qkv, qkv, bias[2 * gi:2 * gi + 2], n_rows=bsz * dil, n_tiles=seq // (QBLK * dil), tq=QBLK, n_prev=1, kw=HQ,
            q_map=rmap(0), k_map=rmap(1), v_map=rmap(2), o_map=rmap(0),
            out_shape=(1, t, HQ), lse_shape=(1, t, LANES), name=f"dilated_attention_{dil}")
        outs.append(o.reshape(t, HQ))
        lses.append(lse.reshape(t, LANES))
        dils.append(dil)
    e = jnp.asarray(_head_expand_matrix(0), BF16)
    return combine_proj_a(outs, lses, dils, e, w_o.astype(BF16), gains[1], h, min(tm, 512))


def mixer_b(h, gains, w_in, sinks, w_o, rel_table, bsz, seq):
    t, d = h.shape
    tm = _row_tile(t)
    hk = B_KV_HEADS * HEAD_DIM
    n_in = w_in.shape[1]
    col_scale = np.ones((n_in,), np.float32)
    col_scale[:HQ] = ATTN_SCALE
    w = (w_in * col_scale[None]).astype(BF16)
    qkv = norm_matmul(h, gains[0], w, BF16, tm, n_in // 2 if (n_in // 2) % LANES == 0 else n_in)
    q = qkv[:, :HQ].reshape(bsz, seq, HQ)
    kd = _dup_groups(qkv[:, HQ:HQ + hk].reshape(bsz, seq, hk))
    vd = _dup_groups(qkv[:, HQ + hk:].reshape(bsz, seq, hk))
    bias = bias_expand(rel_table, _banded_idx(1, B_WINDOW - 1, 1))
    sink_rows = sinks.astype(F32)
    ident = lambda n, i: (n, i, 0)
    (o,) = banded_attention(q, kd, vd, bias, n_rows=bsz, n_tiles=seq // QBLK, tq=QBLK, n_prev=1,
                            kw=2 * LANES, q_map=ident, k_map=ident, v_map=ident, o_map=ident,
                            out_shape=(bsz, seq, HQ), sinks=sink_rows, stack=8, name="sink_window_attention")
    return _proj_call(_proj_kernel, "sink_proj", tm, [o.reshape(t, HQ)], [], w_o.astype(BF16), gains[1], h)


def mixer_c(h, gains, w_in, cmp_pos, cmp_w1, cmp_w2, w_o, rel_table, bsz, seq):
    t, d = h.shape
    tm = _row_tile(t)
    g = C_KV_HEADS
    hk = g * HEAD_DIM
    rest = w_in.shape[1] - HQ
    rest_pad = -(-rest // LANES) * LANES
    wq = (w_in[:, :HQ] * ATTN_SCALE).astype(BF16)
    wr = jnp.pad(w_in[:, HQ:], ((0, 0), (0, rest_pad - rest))).astype(BF16)
    q = norm_matmul(h, gains[0], wq, BF16, tm, HQ).reshape(bsz, seq, HQ)
    r = norm_matmul(h, gains[0], wr, F32, tm, rest_pad)
    kv = [r[:, i * hk:(i + 1) * hk].reshape(bsz, seq, hk) for i in range(6)]
    gates_raw = r[:, 6 * hk:6 * hk + LANES]

    ncr = seq // CMP_STRIDE
    half = CMP_BLOCK // 2
    assert half == CMP_STRIDE
    chunks = jnp.stack([kv[0], kv[1]]).reshape(2, bsz, ncr, half, g, HEAD_DIM)
    chunks = chunks.transpose(0, 1, 4, 2, 3, 5).reshape(2, bsz, g, ncr, half * HEAD_DIM)
    pos = cmp_pos.reshape(2, 2, half * HEAD_DIM)
    w1 = cmp_w1.reshape(2, 2, half * HEAD_DIM, -1).astype(BF16)
    cmp = compress(chunks, pos, w1, cmp_w2.astype(BF16))
    cmp = cmp.transpose(0, 1, 3, 2, 4).reshape(2, bsz, ncr, hk).astype(BF16)
    kcd, vcd = _dup_groups(cmp[0]), _dup_groups(cmp[1])

    ns = seq // SEL_BLOCK
    assert ns <= HEAD_DIM
    k_sel = min(N_SELECT, ns)
    starts = np.arange(ncr) * CMP_STRIDE
    blk = np.arange(HEAD_DIM)
    ov = ((starts[None, :] < (blk[:, None] + 1) * SEL_BLOCK)
          & (starts[None, :] + CMP_BLOCK > blk[:, None] * SEL_BLOCK)
          & (blk[:, None] < ns) & (starts[None, :] + CMP_BLOCK <= seq))
    o_c, pen = cmp_attention(q, kcd, vcd, jnp.asarray(ov.astype(np.float32), BF16), ns, k_sel)

    sel_bias = bias_expand(rel_table, _sel_bias_idx(seq))
    key_blk = np.arange(seq)[:, None] // SEL_BLOCK == np.arange(HEAD_DIM)[None, :]
    onehot = jnp.broadcast_to(jnp.asarray(np.where(key_blk, -NEG_INF, 0.0), BF16), (bsz, seq, HEAD_DIM))
    ones = jnp.ones((bsz, seq, HEAD_DIM), BF16)
    ks, vs = kv[2].astype(BF16), kv[3].astype(BF16)
    kx = jnp.concatenate([ks[..., :HEAD_DIM], onehot, ks[..., HEAD_DIM:], onehot], axis=-1)
    vx = jnp.concatenate([vs[..., :HEAD_DIM], ones, vs[..., HEAD_DIM:], ones], axis=-1)
    o_s = sel_attention(q, kx, vx, pen, sel_bias)

    n_prev = -(-(C_WINDOW - 1) // QBLK)
    tqw = n_prev * QBLK
    wbias = bias_expand(rel_table, _banded_idx(n_prev, C_WINDOW - 1, 1))
    ident = lambda n, i: (n, i, 0)
    (o_w,) = banded_attention(q, _dup_groups(kv[4].astype(BF16)), _dup_groups(kv[5].astype(BF16)), wbias,
                              n_rows=bsz, n_tiles=seq // tqw, tq=tqw, n_prev=n_prev, kw=2 * LANES,
                              q_map=ident, k_map=ident, v_map=ident, o_map=ident,
                              out_shape=(bsz, seq, HQ), stack=8, name="nsa_window_attention")

    e = jnp.asarray(np.stack([_head_expand_matrix(i * N_HEADS) for i in range(3)]), BF16)
    return _proj_call(_proj_c_kernel, "nsa_gate_proj", min(tm, 512),
                      [o_c.reshape(t, HQ), o_s.reshape(t, HQ), o_w.reshape(t, HQ), gates_raw], [e],
                      w_o.astype(BF16), gains[1], h)


def kernel(x, rel_table, norm_gains, a_w_in, a_w_o, b_w_in, b_sinks, b_w_o, c_w_in, c_cmp_pos, c_cmp_w1,
           c_cmp_w2, c_w_o, ffn_w_up, ffn_conv_w, ffn_conv_b, ffn_w_down):
    bsz, seq, d = x.shape
    depth = norm_gains.shape[0]
    h = x.reshape(bsz * seq, d)
    tm = _row_tile(seq)
    for i in range(depth):
        kind, j = i % 3, i // 3
        g = norm_gains[i]
        if kind == 0:
            h = mixer_a(h, g, a_w_in[j], a_w_o[j], rel_table, bsz, seq)
        elif kind == 1:
            h = mixer_b(h, g, b_w_in[j], b_sinks[j], b_w_o[j], rel_table, bsz, seq)
        else:
            h = mixer_c(h, g, c_w_in[j], c_cmp_pos[j], c_cmp_w1[j], c_cmp_w2[j], c_w_o[j], rel_table, bsz, seq)
        h = conv_ffn(h, g[2], ffn_w_up[i].astype(BF16), ffn_conv_w[i], ffn_conv_b[i],
                     ffn_w_down[i].astype(BF16), g[3], seq, min(tm, 512), 256)
    return h.reshape(bsz, seq, d)
```

```python
import functools
import math

import numpy as np
import jax
import jax.numpy as jnp
from jax import lax
from jax.experimental import pallas as pl
from jax.experimental.pallas import tpu as pltpu

F32 = jnp.float32
BF16 = jnp.bfloat16

N_HEADS = 16
HEAD_DIM = 64
HQ = N_HEADS * HEAD_DIM
LANES = 128
ATTN_SCALE = HEAD_DIM ** -0.5
NUM_BUCKETS = 32
MAX_DISTANCE = 2048
RMS_EPS = 1e-6
NEG_INF = -1e30
FORCE_SCORE = 1e9
DIL_CONFIGS = ((128, 1), (512, 4), (2048, 16))
B_KV_HEADS = 2
B_WINDOW = 128
C_KV_HEADS = 2
CMP_BLOCK = 32
CMP_STRIDE = 16
SEL_BLOCK = 64
N_SELECT = 16
C_WINDOW = 512
CONV_WIDTH = 3
QBLK = 128
SEL_CHUNK = 512
VMEM_LIMIT = 56 * 1024 * 1024


def _cparams(*sem):
    return pltpu.CompilerParams(dimension_semantics=sem, vmem_limit_bytes=VMEM_LIMIT)


def _t5_bucket_np(dist):
    max_exact = NUM_BUCKETS // 2
    d = np.maximum(dist, 0)
    df = np.maximum(d, 1).astype(np.float64)
    large = max_exact + np.floor(np.log(df / max_exact) / math.log(MAX_DISTANCE / max_exact)
                                 * (NUM_BUCKETS - max_exact) + 1e-9).astype(np.int64)
    large = np.minimum(large, NUM_BUCKETS - 1)
    return np.where(d < max_exact, d, large).astype(np.int32)


def _rms(x, g):
    ms = jnp.mean(x * x, axis=-1, keepdims=True)
    return (x * lax.rsqrt(ms + RMS_EPS)) * g


def _split3(w):
    hi = w.astype(BF16)
    r1 = w - hi.astype(F32)
    mid = r1.astype(BF16)
    lo = (r1 - mid.astype(F32)).astype(BF16)
    return hi, mid, lo


def _expand_heads(w, e3):
    lane = lax.broadcasted_iota(jnp.int32, w.shape, 1)
    r1 = w - w.astype(BF16).astype(F32)
    r2 = r1 - r1.astype(BF16).astype(F32)
    packed = jnp.where(lane < N_HEADS, w,
                       jnp.where(lane < 2 * N_HEADS, pltpu.roll(r1, N_HEADS, 1), pltpu.roll(r2, 2 * N_HEADS, 1)))
    return jnp.dot(packed.astype(BF16), e3, preferred_element_type=F32)


def _dot_nt(a, b):
    return lax.dot_general(a, b, (((1,), (1,)), ((), ())), preferred_element_type=F32)


def _stack_heads(q_tile, heads):
    lane = lax.broadcasted_iota(jnp.int32, (QBLK, LANES), 1)
    lo = lane < HEAD_DIM
    pieces = []
    for h in heads:
        q2 = q_tile(h // 2)
        keep = lo if h % 2 == 0 else jnp.logical_not(lo)
        pieces.append(jnp.where(keep, q2, jnp.zeros_like(q2)))
    return jnp.concatenate(pieces, axis=0)


def _merge_pairs(o, n_heads):
    lane = lax.broadcasted_iota(jnp.int32, (QBLK, LANES), 1)
    lo = lane < HEAD_DIM
    out = []
    for t in range(0, n_heads, 2):
        out.append(jnp.where(lo, o[t * QBLK:(t + 1) * QBLK], o[(t + 1) * QBLK:(t + 2) * QBLK]))
    return out


def _bias_kernel(tab_ref, rng_ref, idx_ref, o_ref):
    n = pl.program_id(0)
    h = pl.program_id(1)
    idx = idx_ref[0]

    def body(b, acc):
        return jnp.where(idx == b, tab_ref[b, h], acc)

    o_ref[0, 0] = lax.fori_loop(rng_ref[n, 0], rng_ref[n, 1] + 1, body, jnp.full(idx.shape, NEG_INF, F32))


def bias_expand(table, idx):
    n, r, c = idx.shape
    flat = idx.reshape(n, -1)
    lo = np.where(flat >= 0, flat, NUM_BUCKETS).min(axis=1)
    rng = np.stack([lo, flat.max(axis=1)], axis=1).astype(np.int32)
    return pl.pallas_call(
        _bias_kernel,
        grid=(n, N_HEADS),
        in_specs=[pl.BlockSpec(memory_space=pltpu.SMEM),
                  pl.BlockSpec(memory_space=pltpu.SMEM),
                  pl.BlockSpec((1, r, c), lambda i, h: (i, 0, 0))],
        out_specs=pl.BlockSpec((1, 1, r, c), lambda i, h: (i, h, 0, 0)),
        out_shape=jax.ShapeDtypeStruct((n, N_HEADS, r, c), F32),
        compiler_params=_cparams("parallel", "parallel"),
        name="bias_expand",
    )(table, jnp.asarray(rng), jnp.asarray(idx))


def _banded_idx(n_prev, max_dist, stride):
    kb = (n_prev + 1) * QBLK
    dist = np.arange(QBLK)[:, None] + n_prev * QBLK - np.arange(kb)[None, :]
    valid = (dist >= 0) & (dist <= max_dist)
    idx = np.where(valid, _t5_bucket_np(dist * stride), -1).astype(np.int32)
    if n_prev > 1:
        return idx[None]
    first = np.where(np.arange(kb)[None, :] < n_prev * QBLK, -1, idx).astype(np.int32)
    return np.stack([idx, first])


def _norm_mm_kernel(x_ref, g_ref, w_ref, o_ref, xn_ref):
    @pl.when(pl.program_id(1) == 0)
    def _():
        xn_ref[...] = _rms(x_ref[...], g_ref[...]).astype(xn_ref.dtype)

    o_ref[...] = jnp.dot(xn_ref[...], w_ref[...], preferred_element_type=F32).astype(o_ref.dtype)


def norm_matmul(x, gain, w, out_dtype, tm, tn):
    t, d = x.shape
    n = w.shape[1]
    scratch = [pltpu.VMEM((tm, d), BF16)]
    return pl.pallas_call(
        _norm_mm_kernel,
        grid=(t // tm, n // tn),
        in_specs=[pl.BlockSpec((tm, d), lambda i, j: (i, 0)),
                  pl.BlockSpec((1, d), lambda i, j: (0, 0)),
                  pl.BlockSpec((d, tn), lambda i, j: (0, j))],
        out_specs=pl.BlockSpec((tm, tn), lambda i, j: (i, j)),
        out_shape=jax.ShapeDtypeStruct((t, n), out_dtype),
        scratch_shapes=scratch,
        compiler_params=_cparams("parallel", "arbitrary"),
        name="norm_matmul",
    )(x, gain.reshape(1, d), w)


def _norm_perm_kernel(x_ref, g_ref, *refs, dils):
    o_refs, xs_ref = refs[:len(dils)], refs[len(dils)]
    xn = _rms(x_ref[...], g_ref[...])
    nc = xs_ref.shape[0]
    for c in range(nc):
        xs_ref[c] = xn[:, c * LANES:(c + 1) * LANES]
    for o_ref, dil in zip(o_refs, dils):
        if dil == 1:
            o_ref[...] = xn.astype(o_ref.dtype)
            continue
        span = QBLK * dil
        for u in range(x_ref.shape[0] // span):
            for r in range(dil):
                rows = jnp.concatenate(
                    [xs_ref[c, pl.ds(u * span + r, QBLK, stride=dil), :] for c in range(nc)], axis=1)
                dst = u * span + r * QBLK
                o_ref[dst:dst + QBLK, :] = rows.astype(o_ref.dtype)


def norm_permute(x, gain, dils, tm):
    t, d = x.shape
    assert all(tm % (QBLK * dil) == 0 for dil in dils)
    return pl.pallas_call(
        functools.partial(_norm_perm_kernel, dils=tuple(dils)),
        grid=(t // tm,),
        in_specs=[pl.BlockSpec((tm, d), lambda i: (i, 0)),
                  pl.BlockSpec((1, d), lambda i: (0, 0))],
        out_specs=[pl.BlockSpec((tm, d), lambda i: (i, 0)) for _ in dils],
        out_shape=[jax.ShapeDtypeStruct((t, d), BF16) for _ in dils],
        scratch_shapes=[pltpu.VMEM((d // LANES, tm, LANES), F32)],
        compiler_params=_cparams("parallel"),
        name="norm_permute",
    )(x, gain.reshape(1, d))


def _mm_kernel(x_ref, w_ref, o_ref):
    o_ref[...] = jnp.dot(x_ref[...], w_ref[...], preferred_element_type=F32).astype(o_ref.dtype)


def matmul_resident(x, w, out_dtype, tm):
    t, k = x.shape
    n = w.shape[1]
    return pl.pallas_call(
        _mm_kernel,
        grid=(t // tm,),
        in_specs=[pl.BlockSpec((tm, k), lambda i: (i, 0)),
                  pl.BlockSpec((k, n), lambda i: (0, 0))],
        out_specs=pl.BlockSpec((tm, n), lambda i: (i, 0)),
        out_shape=jax.ShapeDtypeStruct((t, n), out_dtype),
        compiler_params=_cparams("parallel"),
        name="matmul_resident",
    )(x, w)


def _finish_proj(y, w_ref, g_ref, h_ref, o_ref):
    z = jnp.dot(y.astype(BF16), w_ref[...], preferred_element_type=F32)
    o_ref[...] = h_ref[...] + _rms(z, g_ref[...])


def _proj_kernel(y_ref, w_ref, g_ref, h_ref, o_ref):
    _finish_proj(y_ref[...], w_ref, g_ref, h_ref, o_ref)


def _unpermute(src_ref, dst_ref, dil, tm, i):
    span = QBLK * dil
    nc = dst_ref.shape[0]
    if span <= tm:
        for u in range(tm // span):
            for r in range(dil):
                lo = u * span + r * QBLK
                for c in range(nc):
                    dst_ref[c, pl.ds(u * span + r, QBLK, stride=dil), :] = src_ref[lo:lo + QBLK,
                                                                                   c * LANES:(c + 1) * LANES]
    else:
        per = tm // dil
        off = (i % (span // tm)) * per
        for r in range(dil):
            lo = pl.multiple_of(r * QBLK + off, 8)
            for c in range(nc):
                dst_ref[c, pl.ds(r, per, stride=dil), :] = src_ref[pl.ds(lo, per), c * LANES:(c + 1) * LANES]
    return jnp.concatenate([dst_ref[c] for c in range(nc)], axis=1) if nc > 1 else dst_ref[0]


def _proj_a_kernel(*refs, dils, tm):
    n = len(dils)
    o_refs, l_refs = refs[:n], refs[n:2 * n]
    e_ref, w_ref, g_ref, h_ref, out_ref = refs[2 * n:2 * n + 5]
    scratch = refs[2 * n + 5:]
    i = pl.program_id(0)
    outs, lses = [], []
    si = 0
    for gi, dil in enumerate(dils):
        if dil == 1:
            outs.append(o_refs[gi][...])
            lses.append(l_refs[gi][...])
        else:
            outs.append(_unpermute(o_refs[gi], scratch[si], dil, tm, i))
            lses.append(_unpermute(l_refs[gi], scratch[si + 1], dil, tm, i))
            si += 2
    mx = functools.reduce(jnp.maximum, lses)
    es = [jnp.exp(l - mx) for l in lses]
    inv = 1.0 / functools.reduce(lambda a, b: a + b, es)
    e = e_ref[...]
    y = outs[0] * _expand_heads(es[0] * inv, e)
    for gi in range(1, n):
        y = y + outs[gi] * _expand_heads(es[gi] * inv, e)
    _finish_proj(y, w_ref, g_ref, h_ref, out_ref)


def combine_proj_a(outs, lses, dils, e, w, gain, h, tm):
    t, d = h.shape

    def row_spec(width, dil):
        rows = max(tm, QBLK * dil)
        return pl.BlockSpec((rows, width), lambda i, q=rows // tm: (i // q, 0))

    in_specs = [row_spec(HQ, dil) for dil in dils] + [row_spec(LANES, dil) for dil in dils] + [
        pl.BlockSpec(e.shape, lambda i: (0, 0)),
        pl.BlockSpec(w.shape, lambda i: (0, 0)),
        pl.BlockSpec((1, d), lambda i: (0, 0)),
        pl.BlockSpec((tm, d), lambda i: (i, 0))]
    scratch = []
    for dil in dils:
        if dil > 1:
            scratch += [pltpu.VMEM((HQ // LANES, tm, LANES), F32), pltpu.VMEM((1, tm, LANES), F32)]
    return pl.pallas_call(
        functools.partial(_proj_a_kernel, dils=tuple(dils), tm=tm),
        grid=(t // tm,),
        in_specs=in_specs,
        out_specs=pl.BlockSpec((tm, d), lambda i: (i, 0)),
        out_shape=jax.ShapeDtypeStruct((t, d), F32),
        scratch_shapes=scratch,
        compiler_params=_cparams("arbitrary"),
        name="dilated_combine_proj",
    )(*outs, *lses, e, w, gain.reshape(1, d), h)


def _proj_c_kernel(oc_ref, os_ref, ow_ref, gr_ref, e_ref, w_ref, g_ref, h_ref, o_ref):
    sig = jax.nn.sigmoid(gr_ref[...])
    e = e_ref[...]
    gate = lambda i: _expand_heads(sig if i == 0 else pltpu.roll(sig, LANES - i * N_HEADS, 1), e)
    y = gate(0) * oc_ref[...]
    y = y + gate(1) * os_ref[...]
    y = y + gate(2) * ow_ref[...]
    _finish_proj(y, w_ref, g_ref, h_ref, o_ref)


def _proj_call(kernel, name, tm, row_inputs, const_inputs, w, gain, h):
    t, d = h.shape
    k = w.shape[0]
    row_specs = [pl.BlockSpec((tm, a.shape[1]), lambda i: (i, 0)) for a in row_inputs]
    const_specs = [pl.BlockSpec(a.shape, (lambda nd: (lambda i: (0,) * nd))(a.ndim)) for a in const_inputs]
    return pl.pallas_call(
        kernel,
        grid=(t // tm,),
        in_specs=row_specs + const_specs + [
            pl.BlockSpec((k, d), lambda i: (0, 0)),
            pl.BlockSpec((1, d), lambda i: (0, 0)),
            pl.BlockSpec((tm, d), lambda i: (i, 0))],
        out_specs=pl.BlockSpec((tm, d), lambda i: (i, 0)),
        out_shape=jax.ShapeDtypeStruct((t, d), F32),
        compiler_params=_cparams("parallel"),
        name=name,
    )(*row_inputs, *const_inputs, w, gain.reshape(1, d), h)


def _head_expand_matrix():
    e = np.zeros((LANES, HQ), np.float32)
    for term in range(3):
        for h in range(N_HEADS):
            e[term * N_HEADS + h, h * HEAD_DIM:(h + 1) * HEAD_DIM] = 1.0
    return e


def _banded_kernel(*refs, tq, n_prev, n_chunks, stack, with_sinks, with_lse):
    q_ref, kp_ref, kc_ref, vp_ref, vc_ref, bias_ref = refs[:6]
    pos = 6
    sink_ref = None
    if with_sinks:
        sink_ref = refs[pos]
        pos += 1
    o_ref = refs[pos]
    pos += 1
    lse_ref = None
    if with_lse:
        lse_ref = refs[pos]
        pos += 1
    kbuf, vbuf = refs[pos], refs[pos + 1]

    i = pl.program_id(1)
    hpc = N_HEADS // n_chunks
    kb = (n_prev + 1) * QBLK
    kbuf[0:tq] = kp_ref[0]
    kbuf[tq:2 * tq] = kc_ref[0]
    vbuf[0:tq] = vp_ref[0]
    vbuf[tq:2 * tq] = vc_ref[0]
    lane = lax.broadcasted_iota(jnp.int32, (QBLK, LANES), 1)
    head_row = lax.broadcasted_iota(jnp.int32, (stack * QBLK, 1), 0)

    for s in range(tq // QBLK):
        r0 = s * QBLK
        k0 = tq + r0 - n_prev * QBLK
        variants = bias_ref.shape[0] > 1
        if variants:
            first = jnp.where(i == 0, 1, 0)
        else:
            col = lax.broadcasted_iota(jnp.int32, (1, kb), 1) + k0
            kmask = jnp.where(jnp.logical_and(i == 0, col < tq), NEG_INF, 0.0).astype(F32)
        lse_acc = jnp.zeros((QBLK, LANES), F32)
        for h0 in range(0, N_HEADS, stack):
            heads = list(range(h0, h0 + stack))
            c = h0 // hpc
            qst = _stack_heads(lambda p: q_ref[0, r0:r0 + QBLK, p * LANES:(p + 1) * LANES], heads)
            kx = kbuf[k0:k0 + kb, c * LANES:(c + 1) * LANES]
            vx = vbuf[k0:k0 + kb, c * LANES:(c + 1) * LANES]
            sc = _dot_nt(qst, kx)
            if variants:
                sc = sc + bias_ref[first, h0:h0 + stack].reshape(stack * QBLK, kb)
            else:
                sc = sc + bias_ref[0, h0:h0 + stack].reshape(stack * QBLK, kb) + kmask
            m = jnp.max(sc, axis=-1, keepdims=True)
            if with_sinks:
                sk = sink_ref[h0 + stack - 1]
                for t in range(stack - 2, -1, -1):
                    sk = jnp.where(head_row < (t + 1) * QBLK, sink_ref[h0 + t], sk)
                m = jnp.maximum(m, sk)
            p = jnp.exp(sc - m)
            den = jnp.sum(p, axis=-1, keepdims=True)
            norm = den + jnp.exp(sk - m) if with_sinks else den
            o = jnp.dot(p.astype(BF16), vx, preferred_element_type=F32) * (1.0 / norm)
            for t2, blk in enumerate(_merge_pairs(o, stack)):
                pidx = h0 // 2 + t2
                o_ref[0, r0:r0 + QBLK, pidx * LANES:(pidx + 1) * LANES] = blk.astype(o_ref.dtype)
            if with_lse:
                lse = m + jnp.log(den)
                for t, h in enumerate(heads):
                    lse_acc = jnp.where(lane == h, lse[t * QBLK:(t + 1) * QBLK], lse_acc)
        if with_lse:
            lse_ref[0, r0:r0 + QBLK, :] = lse_acc


def banded_attention(q_arr, k_arr, v_arr, bias, *, n_rows, n_tiles, tq, n_prev, kw,
                     q_map, k_map, v_map, o_map, out_shape, lse_shape=None, sinks=None, stack=2, name):
    n_chunks = kw // LANES
    assert stack % 2 == 0 and (N_HEADS // n_chunks) % stack == 0
    kb = (n_prev + 1) * QBLK
    with_sinks = sinks is not None
    with_lse = lse_shape is not None

    def prev(fn):
        return lambda n, i: fn(n, jnp.maximum(i - 1, 0))

    in_specs = [pl.BlockSpec((1, tq, HQ), q_map),
                pl.BlockSpec((1, tq, kw), prev(k_map)),
                pl.BlockSpec((1, tq, kw), k_map),
                pl.BlockSpec((1, tq, kw), prev(v_map)),
                pl.BlockSpec((1, tq, kw), v_map),
                pl.BlockSpec(bias.shape, lambda n, i: (0, 0, 0, 0))]
    assert bias.shape[1:] == (N_HEADS, QBLK, kb) and (bias.shape[0] == 1 or tq == QBLK)
    args = [q_arr, k_arr, k_arr, v_arr, v_arr, bias]
    if with_sinks:
        in_specs.append(pl.BlockSpec(memory_space=pltpu.SMEM))
        args.append(sinks)
    out_specs = [pl.BlockSpec((1, tq, HQ), o_map)]
    out_shapes = [jax.ShapeDtypeStruct(out_shape, F32)]
    if with_lse:
        out_specs.append(pl.BlockSpec((1, tq, LANES), o_map))
        out_shapes.append(jax.ShapeDtypeStruct(lse_shape, F32))
    res = pl.pallas_call(
        functools.partial(_banded_kernel, tq=tq, n_prev=n_prev, n_chunks=n_chunks, stack=stack,
                          with_sinks=with_sinks, with_lse=with_lse),
        grid=(n_rows, n_tiles),
        in_specs=in_specs,
        out_specs=out_specs,
        out_shape=out_shapes,
        scratch_shapes=[pltpu.VMEM((2 * tq, kw), BF16), pltpu.VMEM((2 * tq, kw), BF16)],
        compiler_params=_cparams("parallel", "arbitrary"),
        name=name,
    )(*args)
    return res


def _dup_groups(x):
    g0, g1 = x[..., :HEAD_DIM], x[..., HEAD_DIM:]
    return jnp.concatenate([g0, g0, g1, g1], axis=-1)


def _compress_kernel(ch_ref, pos_ref, w1_ref, w2_ref, o_ref):
    ch = ch_ref[0, 0, 0]
    rows = ch.shape[0]
    posv = pos_ref[0]
    a = jnp.dot((ch + posv[0:1]).astype(BF16), w1_ref[0, 0], preferred_element_type=F32)
    b = jnp.dot((ch + posv[1:2]).astype(BF16), w1_ref[0, 1], preferred_element_type=F32)
    hid = a + pltpu.roll(b, rows - 1, 0)
    act = jax.nn.gelu(hid, approximate=True)
    o_ref[0, 0, 0] = jnp.dot(act.astype(BF16), w2_ref[0], preferred_element_type=F32)


def compress(chunks, pos, w1, w2):
    _, b, g, rows, width = chunks.shape
    hid = w1.shape[-1]
    return pl.pallas_call(
        _compress_kernel,
        grid=(2, b, g),
        in_specs=[pl.BlockSpec((1, 1, 1, rows, width), lambda i, bb, gg: (i, bb, gg, 0, 0)),
                  pl.BlockSpec((1, 2, width), lambda i, bb, gg: (i, 0, 0)),
                  pl.BlockSpec((1, 2, width, hid), lambda i, bb, gg: (i, 0, 0, 0)),
                  pl.BlockSpec((1, hid, HEAD_DIM), lambda i, bb, gg: (i, 0, 0))],
        out_specs=pl.BlockSpec((1, 1, 1, rows, HEAD_DIM), lambda i, bb, gg: (i, bb, gg, 0, 0)),
        out_shape=jax.ShapeDtypeStruct((2, b, g, rows, HEAD_DIM), F32),
        compiler_params=_cparams("parallel", "parallel", "parallel"),
        name="nsa_compress",
    )(chunks, pos, w1, w2)


def _cmp_attn_kernel(q_ref, kc_ref, vc_ref, ov_ref, place_ref, o_ref, sel_ref, *, n_sel_blocks, k_sel):
    qi = pl.program_id(1)
    ncr = kc_ref.shape[1]
    hpc = N_HEADS // C_KV_HEADS
    qpos = qi * QBLK + lax.broadcasted_iota(jnp.int32, (QBLK, 1), 0)
    cidx = lax.broadcasted_iota(jnp.int32, (1, ncr), 1)
    valid = (cidx * CMP_STRIDE + (CMP_BLOCK - 1)) <= qpos
    maskc = jnp.where(valid, 0.0, NEG_INF).astype(F32)
    anyv = (qpos >= CMP_BLOCK - 1).astype(F32)
    blk_id = lax.broadcasted_iota(jnp.int32, (HEAD_DIM, QBLK), 0)
    cur = (qi * QBLK + lax.broadcasted_iota(jnp.int32, (HEAD_DIM, QBLK), 1)) // SEL_BLOCK
    forced = jnp.logical_or(jnp.logical_or(blk_id == 0, blk_id == cur), blk_id == cur - 1)
    allowed = blk_id <= cur
    for g in range(C_KV_HEADS):
        heads = list(range(g * hpc, (g + 1) * hpc))
        qst = _stack_heads(lambda p: q_ref[0, :, p * LANES:(p + 1) * LANES], heads)
        sc = _dot_nt(qst, kc_ref[0, :, g * LANES:(g + 1) * LANES])
        sc3 = sc.reshape(hpc, QBLK, ncr) + maskc[None]
        m = jnp.max(sc3, axis=-1, keepdims=True)
        e = jnp.exp(sc3 - m)
        ssum = jnp.sum(e, axis=-1, keepdims=True)
        p = (e * (1.0 / ssum)) * anyv[None]
        o = jnp.dot(p.reshape(hpc * QBLK, ncr).astype(BF16), vc_ref[0, :, g * LANES:(g + 1) * LANES],
                    preferred_element_type=F32)
        for t2, blk in enumerate(_merge_pairs(o, hpc)):
            pidx = (g * hpc) // 2 + t2
            o_ref[0, :, pidx * LANES:(pidx + 1) * LANES] = blk
        hi, mid, lo = _split3(jnp.sum(p, axis=0))
        ovt = ov_ref[...]
        imp = (_dot_nt(ovt, hi) + _dot_nt(ovt, mid)) + _dot_nt(ovt, lo)
        score = jnp.where(forced, FORCE_SCORE, jnp.where(allowed, imp, NEG_INF))
        bits = pltpu.bitcast(score, jnp.int32)
        key = jnp.where(bits < 0, bits ^ jnp.int32(0x7FFFFFFF), bits)
        key_m1 = key - 1
        rank = jnp.zeros((HEAD_DIM, QBLK), jnp.int32)
        for i in range(n_sel_blocks):
            thr = jnp.where(blk_id > i, key_m1, key)
            rank = rank + jnp.where(key[i:i + 1, :] > thr, 1, 0)
        keep = jnp.logical_and(rank < k_sel, blk_id < n_sel_blocks)
        pen_t = jnp.where(keep, 0.0, -1.0).astype(BF16)
        pen = lax.dot_general(pen_t, place_ref[...], (((0,), (0,)), ((), ())), preferred_element_type=F32)
        sel_ref[0, :, g * LANES:(g + 1) * LANES] = pen.astype(sel_ref.dtype)


def cmp_attention(q, kcd, vcd, ov, n_sel_blocks, k_sel):
    b, s, _ = q.shape
    ncr = kcd.shape[1]
    place = np.zeros((HEAD_DIM, LANES), np.float32)
    place[np.arange(HEAD_DIM), HEAD_DIM + np.arange(HEAD_DIM)] = 1.0
    return pl.pallas_call(
        functools.partial(_cmp_attn_kernel, n_sel_blocks=n_sel_blocks, k_sel=k_sel),
        grid=(b, s // QBLK),
        in_specs=[pl.BlockSpec((1, QBLK, HQ), lambda bb, i: (bb, i, 0)),
                  pl.BlockSpec((1, ncr, 2 * LANES), lambda bb, i: (bb, 0, 0)),
                  pl.BlockSpec((1, ncr, 2 * LANES), lambda bb, i: (bb, 0, 0)),
                  pl.BlockSpec((HEAD_DIM, ncr), lambda bb, i: (0, 0)),
                  pl.BlockSpec((HEAD_DIM, LANES), lambda bb, i: (0, 0))],
        out_specs=[pl.BlockSpec((1, QBLK, HQ), lambda bb, i: (bb, i, 0)),
                   pl.BlockSpec((1, QBLK, 2 * LANES), lambda bb, i: (bb, i, 0))],
        out_shape=[jax.ShapeDtypeStruct((b, s, HQ), F32),
                   jax.ShapeDtypeStruct((b, s, 2 * LANES), BF16)],
        compiler_params=_cparams("parallel", "parallel"),
        name="nsa_cmp_attention",
    )(q, kcd, vcd, ov, jnp.asarray(place, BF16))


def _sel_attn_kernel(q_ref, pen_ref, k_ref, v_ref, bias_ref, o_ref, qst_ref, m_ref, acc_ref, *, nbt):
    qi = pl.program_id(2)
    hpc = N_HEADS // C_KV_HEADS
    nsub = SEL_CHUNK // QBLK
    lane = lax.broadcasted_iota(jnp.int32, (QBLK, LANES), 1)
    lo = lane < HEAD_DIM
    pen = pen_ref[0].astype(F32)
    for t in range(hpc):
        q2 = q_ref[0, :, (t // 2) * LANES:(t // 2 + 1) * LANES].astype(F32)
        if t % 2:
            q2 = pltpu.roll(q2, HEAD_DIM, 1)
        qst_ref[t * QBLK:(t + 1) * QBLK, :] = jnp.where(lo, q2, pen).astype(BF16)
    m_ref[...] = jnp.full(m_ref.shape, NEG_INF, F32)
    acc_ref[...] = jnp.zeros(acc_ref.shape, F32)

    def body(c, carry):
        k0 = pl.multiple_of(c * SEL_CHUNK, SEL_CHUNK)
        s = _dot_nt(qst_ref[...], k_ref[0, pl.ds(k0, SEL_CHUNK), :])
        tiles = []
        for u in range(nsub):
            off = qi - nsub * c - u
            idx = jnp.where(off < 0, nbt, jnp.minimum(off, nbt - 1))
            tiles.append(s[:, u * QBLK:(u + 1) * QBLK] + bias_ref[idx].reshape(hpc * QBLK, QBLK))
        m_old = m_ref[...]
        m_new = jnp.maximum(m_old, jnp.max(functools.reduce(jnp.maximum, tiles), axis=-1, keepdims=True))
        p = jnp.concatenate([jnp.exp(tl - m_new).astype(BF16) for tl in tiles], axis=1)
        pv = jnp.dot(p, v_ref[0, pl.ds(k0, SEL_CHUNK), :], preferred_element_type=F32)
        acc_ref[...] = jnp.exp(m_old - m_new) * acc_ref[...] + pv
        m_ref[...] = m_new
        return carry

    lax.fori_loop(0, qi // nsub + 1, body, 0)
    acc = acc_ref[...]
    rolled = pltpu.roll(acc, HEAD_DIM, 1)
    for t2 in range(hpc // 2):
        ev = slice(2 * t2 * QBLK, (2 * t2 + 1) * QBLK)
        od = slice((2 * t2 + 1) * QBLK, (2 * t2 + 2) * QBLK)
        even = acc[ev] * (1.0 / rolled[ev])
        odd = rolled[od] * (1.0 / acc[od])
        o_ref[0, :, t2 * LANES:(t2 + 1) * LANES] = jnp.where(lo, even, odd)


def sel_attention(q, kx, vx, pen, bias_tiles):
    b, s, _ = q.shape
    hpc = N_HEADS // C_KV_HEADS
    gw = hpc * HEAD_DIM
    nbt = bias_tiles.shape[0] - 1
    assert s % SEL_CHUNK == 0
    return pl.pallas_call(
        functools.partial(_sel_attn_kernel, nbt=nbt),
        grid=(b, C_KV_HEADS, s // QBLK),
        in_specs=[pl.BlockSpec((1, QBLK, gw), lambda bb, g, i: (bb, i, g)),
                  pl.BlockSpec((1, QBLK, LANES), lambda bb, g, i: (bb, i, g)),
                  pl.BlockSpec((1, s, LANES), lambda bb, g, i: (bb, 0, g)),
                  pl.BlockSpec((1, s, LANES), lambda bb, g, i: (bb, 0, g)),
                  pl.BlockSpec((nbt + 1, hpc, QBLK, QBLK), lambda bb, g, i: (0, g, 0, 0))],
        out_specs=pl.BlockSpec((1, QBLK, gw), lambda bb, g, i: (bb, i, g)),
        out_shape=jax.ShapeDtypeStruct((b, s, HQ), F32),
        scratch_shapes=[pltpu.VMEM((hpc * QBLK, LANES), BF16),
                        pltpu.VMEM((hpc * QBLK, LANES), F32),
                        pltpu.VMEM((hpc * QBLK, LANES), F32)],
        compiler_params=_cparams("parallel", "parallel", "arbitrary"),
        name="nsa_sel_attention",
    )(q, pen, kx, vx, bias_tiles)


def _sel_bias_idx(s):
    nqt = s // QBLK
    far = -(-(int(np.argmax(_t5_bucket_np(np.arange(4 * MAX_DISTANCE)) == NUM_BUCKETS - 1)) + QBLK) // QBLK)
    nbt = min(nqt, far + 1)
    d0 = np.arange(nbt)[:, None, None] * QBLK
    dist = d0 + np.arange(QBLK)[None, :, None] - np.arange(QBLK)[None, None, :]
    idx = np.where(dist >= 0, _t5_bucket_np(dist), -1).astype(np.int32)
    return np.concatenate([idx, np.full((1, QBLK, QBLK), -1, np.int32)])


def _ffn_kernel(h_ref, g2_ref, wu_ref, cw_ref, cb_ref, wd_ref, g3_ref,
                o_ref, xn_ref, acc_ref, ua_ref, ub_ref, aa_ref, ab_ref, carry_ref, *, tiles_per_seq, nj):
    i = pl.program_id(0)
    tm, tn = xn_ref.shape[0], wu_ref.shape[2]
    xn_ref[...] = _rms(h_ref[...], g2_ref[...]).astype(xn_ref.dtype)
    acc_ref[...] = jnp.zeros(acc_ref.shape, F32)

    @pl.when(i % tiles_per_seq == 0)
    def _():
        carry_ref[...] = jnp.zeros(carry_ref.shape, F32)

    top = 16
    rowt = lax.broadcasted_iota(jnp.int32, (top, tn), 0)

    def up(jj, u_ref):
        xn = xn_ref[...]
        u_ref[0] = jnp.dot(xn, wu_ref[jj], preferred_element_type=F32)
        u_ref[1] = jnp.dot(xn, wu_ref[nj + jj], preferred_element_type=F32)

    def conv(u, s1, s2, c):
        cw = cw_ref[c]
        return ((cb_ref[c] + u * cw[2:3]) + s2 * cw[0:1]) + s1 * cw[1:2]

    def conv_body(u, c):
        return conv(u, pltpu.roll(u, 1, 0), pltpu.roll(u, 2, 0), c)

    def conv_top(u_ref, c):
        prev = carry_ref[c]
        p1, p2 = prev[7:8], prev[6:7]
        u = u_ref[0:top]
        s1 = jnp.where(rowt == 0, p1, pltpu.roll(u, 1, 0))
        s2 = jnp.where(rowt == 0, p2, jnp.where(rowt == 1, p1, pltpu.roll(u, 2, 0)))
        carry_ref[c] = u_ref[tm - 8:tm]
        return conv(u, s1, s2, c)

    def gated(cg, cv):
        return (jax.nn.gelu(cg, approximate=True) * cv).astype(BF16)

    def act(j, u_ref, a_ref):
        a_ref[...] = gated(conv_body(u_ref[0], j), conv_body(u_ref[1], nj + j))
        a_ref[0:top] = gated(conv_top(u_ref.at[0], j), conv_top(u_ref.at[1], nj + j))

    def down(j, a_ref):
        acc_ref[...] += jnp.dot(a_ref[...], wd_ref[j], preferred_element_type=F32)

    up(0, ua_ref)
    up(1, ub_ref)
    act(0, ua_ref, aa_ref)

    for jj in range((nj - 3) // 2):
        j = 2 * jj + 1
        up(j + 1, ua_ref)
        act(j, ub_ref, ab_ref)
        down(j - 1, aa_ref)
        up(j + 2, ub_ref)
        act(j + 1, ua_ref, aa_ref)
        down(j, ab_ref)
    up(nj - 1, ua_ref)
    act(nj - 2, ub_ref, ab_ref)
    down(nj - 3, aa_ref)
    act(nj - 1, ua_ref, aa_ref)
    down(nj - 2, ab_ref)
    down(nj - 1, aa_ref)
    o_ref[...] = h_ref[...] + _rms(acc_ref[...], g3_ref[...])


def conv_ffn(h, g2, w_up, conv_w, conv_b, w_down, g3, seq, tm, tn):
    t, d = h.shape
    dff = w_down.shape[0]
    nj = dff // tn
    assert nj * tn == dff and nj % 2 == 1 and nj >= 3
    wu = w_up.reshape(d, 2 * nj, tn).transpose(1, 0, 2)
    cw = conv_w.reshape(CONV_WIDTH, 2 * nj, tn).transpose(1, 0, 2)
    cb = conv_b.reshape(2 * nj, 1, tn)
    wd = w_down.reshape(nj, tn, d)
    resident = lambda a: pl.BlockSpec(a.shape, (lambda nd: (lambda i: (0,) * nd))(a.ndim))
    return pl.pallas_call(
        functools.partial(_ffn_kernel, tiles_per_seq=seq // tm, nj=nj),
        grid=(t // tm,),
        in_specs=[pl.BlockSpec((tm, d), lambda i: (i, 0)),
                  pl.BlockSpec((1, d), lambda i: (0, 0)),
                  resident(wu), resident(cw), resident(cb), resident(wd),
                  pl.BlockSpec((1, d), lambda i: (0, 0))],
        out_specs=pl.BlockSpec((tm, d), lambda i: (i, 0)),
        out_shape=jax.ShapeDtypeStruct((t, d), F32),
        scratch_shapes=[pltpu.VMEM((tm, d), BF16), pltpu.VMEM((tm, d), F32),
                        pltpu.VMEM((2, tm, tn), F32), pltpu.VMEM((2, tm, tn), F32),
                        pltpu.VMEM((tm, tn), BF16), pltpu.VMEM((tm, tn), BF16),
                        pltpu.VMEM((2 * nj, 8, tn), F32)],
        compiler_params=_cparams("arbitrary"),
        name="conv_ffn",
    )(h, g2.reshape(1, d), wu, cw, cb, wd, g3.reshape(1, d))


def _row_tile(t):
    for tm in (1024, 512, 256, 128):
        if t % tm == 0:
            return tm
    raise ValueError(f"token count {t} is not a multiple of 128")


def mixer_a(h, gains, w_in, w_o, rel_table, bsz, seq):
    t, d = h.shape
    tm = _row_tile(t)
    n_dil = len(DIL_CONFIGS)
    a_in = w_in.shape[1]
    col_scale = np.ones((a_in,), np.float32).reshape(n_dil, 3, HQ)
    col_scale[:, 0] = ATTN_SCALE
    w = (w_in * col_scale.reshape(1, a_in)).astype(BF16)
    idx = np.concatenate([_banded_idx(1, window // dil, dil) for window, dil in DIL_CONFIGS])
    bias = bias_expand(rel_table, idx)
    outs, lses = [], []
    dils = [dil for _, dil in DIL_CONFIGS]
    blocks_per_seq = seq // QBLK
    xns = norm_permute(h, gains[0], dils, max([tm] + [QBLK * dil for dil in dils]))
    for gi, (window, dil) in enumerate(DIL_CONFIGS):
        assert window // dil <= QBLK and seq % (QBLK * dil) == 0
        qkv = matmul_resident(xns[gi], w[:, gi * 3 * HQ:(gi + 1) * 3 * HQ], BF16, tm)
        qkv = qkv.reshape(1, t, 3 * HQ)

        def rmap(part, dil=dil):
            return lambda n, i: (0, (n // dil) * blocks_per_seq + i * dil + n % dil, part)

        o, lse = banded_attention(
            qkv, qkv, qkv, bias[2 * gi:2 * gi + 2], n_rows=bsz * dil, n_tiles=seq // (QBLK * dil), tq=QBLK, n_prev=1, kw=HQ,
            q_map=rmap(0), k_map=rmap(1), v_map=rmap(2), o_map=rmap(0),
            out_shape=(1, t, HQ), lse_shape=(1, t, LANES), name=f"dilated_attention_{dil}")
        outs.append(o.reshape(t, HQ))
        lses.append(lse.reshape(t, LANES))
    e = jnp.asarray(_head_expand_matrix(), BF16)
    return combine_proj_a(outs, lses, dils, e, w_o.astype(BF16), gains[1], h, min(tm, 512))


def mixer_b(h, gains, w_in, sinks, w_o, rel_table, bsz, seq):
    t, d = h.shape
    tm = _row_tile(t)
    hk = B_KV_HEADS * HEAD_DIM
    n_in = w_in.shape[1]
    col_scale = np.ones((n_in,), np.float32)
    col_scale[:HQ] = ATTN_SCALE
    w = (w_in * col_scale[None]).astype(BF16)
    qkv = norm_matmul(h, gains[0], w, BF16, tm, n_in // 2 if (n_in // 2) % LANES == 0 else n_in)
    q = qkv[:, :HQ].reshape(bsz, seq, HQ)
    kd = _dup_groups(qkv[:, HQ:HQ + hk].reshape(bsz, seq, hk))
    vd = _dup_groups(qkv[:, HQ + hk:].reshape(bsz, seq, hk))
    bias = bias_expand(rel_table, _banded_idx(1, B_WINDOW - 1, 1))
    sink_rows = sinks.astype(F32)
    ident = lambda n, i: (n, i, 0)
    (o,) = banded_attention(q, kd, vd, bias, n_rows=bsz, n_tiles=seq // QBLK, tq=QBLK, n_prev=1,
                            kw=2 * LANES, q_map=ident, k_map=ident, v_map=ident, o_map=ident,
                            out_shape=(bsz, seq, HQ), sinks=sink_rows, stack=8, name="sink_window_attention")
    return _proj_call(_proj_kernel, "sink_proj", tm, [o.reshape(t, HQ)], [], w_o.astype(BF16), gains[1], h)


def mixer_c(h, gains, w_in, cmp_pos, cmp_w1, cmp_w2, w_o, rel_table, bsz, seq):
    t, d = h.shape
    tm = _row_tile(t)
    g = C_KV_HEADS
    hk = g * HEAD_DIM
    rest = w_in.shape[1] - HQ
    rest_pad = -(-rest // LANES) * LANES
    wq = (w_in[:, :HQ] * ATTN_SCALE).astype(BF16)
    wr = jnp.pad(w_in[:, HQ:], ((0, 0), (0, rest_pad - rest))).astype(BF16)
    q = norm_matmul(h, gains[0], wq, BF16, tm, HQ).reshape(bsz, seq, HQ)
    r = norm_matmul(h, gains[0], wr, F32, tm, rest_pad)
    kv = [r[:, i * hk:(i + 1) * hk].reshape(bsz, seq, hk) for i in range(6)]
    gates_raw = r[:, 6 * hk:6 * hk + LANES]

    ncr = seq // CMP_STRIDE
    half = CMP_BLOCK // 2
    assert half == CMP_STRIDE
    chunks = jnp.stack([kv[0], kv[1]]).reshape(2, bsz, ncr, half, g, HEAD_DIM)
    chunks = chunks.transpose(0, 1, 4, 2, 3, 5).reshape(2, bsz, g, ncr, half * HEAD_DIM)
    pos = cmp_pos.reshape(2, 2, half * HEAD_DIM)
    w1 = cmp_w1.reshape(2, 2, half * HEAD_DIM, -1).astype(BF16)
    cmp = compress(chunks, pos, w1, cmp_w2.astype(BF16))
    cmp = cmp.transpose(0, 1, 3, 2, 4).reshape(2, bsz, ncr, hk).astype(BF16)
    kcd, vcd = _dup_groups(cmp[0]), _dup_groups(cmp[1])

    ns = seq // SEL_BLOCK
    assert ns <= HEAD_DIM
    k_sel = min(N_SELECT, ns)
    starts = np.arange(ncr) * CMP_STRIDE
    blk = np.arange(HEAD_DIM)
    ov = ((starts[None, :] < (blk[:, None] + 1) * SEL_BLOCK)
          & (starts[None, :] + CMP_BLOCK > blk[:, None] * SEL_BLOCK)
          & (blk[:, None] < ns) & (starts[None, :] + CMP_BLOCK <= seq))
    o_c, pen = cmp_attention(q, kcd, vcd, jnp.asarray(ov.astype(np.float32), BF16), ns, k_sel)

    sel_bias = bias_expand(rel_table, _sel_bias_idx(seq))
    key_blk = np.arange(seq)[:, None] // SEL_BLOCK == np.arange(HEAD_DIM)[None, :]
    onehot = jnp.broadcast_to(jnp.asarray(np.where(key_blk, -NEG_INF, 0.0), BF16), (bsz, seq, HEAD_DIM))
    ones = jnp.ones((bsz, seq, HEAD_DIM), BF16)
    ks, vs = kv[2].astype(BF16), kv[3].astype(BF16)
    kx = jnp.concatenate([ks[..., :HEAD_DIM], onehot, ks[..., HEAD_DIM:], onehot], axis=-1)
    vx = jnp.concatenate([vs[..., :HEAD_DIM], ones, vs[..., HEAD_DIM:], ones], axis=-1)
    o_s = sel_attention(q, kx, vx, pen, sel_bias)

    n_prev = -(-(C_WINDOW - 1) // QBLK)
    tqw = n_prev * QBLK
    wbias = bias_expand(rel_table, _banded_idx(n_prev, C_WINDOW - 1, 1))
    ident = lambda n, i: (n, i, 0)
    (o_w,) = banded_attention(q, _dup_groups(kv[4].astype(BF16)), _dup_groups(kv[5].astype(BF16)), wbias,
                              n_rows=bsz, n_tiles=seq // tqw, tq=tqw, n_prev=n_prev, kw=2 * LANES,
                              q_map=ident, k_map=ident, v_map=ident, o_map=ident,
                              out_shape=(bsz, seq, HQ), stack=8, name="nsa_window_attention")

    e = jnp.asarray(_head_expand_matrix(), BF16)
    return _proj_call(_proj_c_kernel, "nsa_gate_proj", min(tm, 512),
                      [o_c.reshape(t, HQ), o_s.reshape(t, HQ), o_w.reshape(t, HQ), gates_raw], [e],
                      w_o.astype(BF16), gains[1], h)


def kernel(x, rel_table, norm_gains, a_w_in, a_w_o, b_w_in, b_sinks, b_w_o, c_w_in, c_cmp_pos, c_cmp_w1,
           c_cmp_w2, c_w_o, ffn_w_up, ffn_conv_w, ffn_conv_b, ffn_w_down):
    bsz, seq, d = x.shape
    depth = norm_gains.shape[0]
    h = x.reshape(bsz * seq, d)
    tm = _row_tile(seq)
    for i in range(depth):
        kind, j = i % 3, i // 3
        g = norm_gains[i]
        if kind == 0:
            h = mixer_a(h, g, a_w_in[j], a_w_o[j], rel_table, bsz, seq)
        elif kind == 1:
            h = mixer_b(h, g, b_w_in[j], b_sinks[j], b_w_o[j], rel_table, bsz, seq)
        else:
            h = mixer_c(h, g, c_w_in[j], c_cmp_pos[j], c_cmp_w1[j], c_cmp_w2[j], c_w_o[j], rel_table, bsz, seq)
        h = conv_ffn(h, g[2], ffn_w_up[i].astype(BF16), ffn_conv_w[i], ffn_conv_b[i],
                     ffn_w_down[i].astype(BF16), g[3], seq, min(tm, 512), 256)
    return h.reshape(bsz, seq, d)
```

```python
import functools
import math

import numpy as np
import jax
import jax.numpy as jnp
from jax import lax
from jax.experimental import pallas as pl
from jax.experimental.pallas import tpu as pltpu

F32 = jnp.float32
BF16 = jnp.bfloat16

N_HEADS = 16
HEAD_DIM = 64
HQ = N_HEADS * HEAD_DIM
LANES = 128
ATTN_SCALE = HEAD_DIM ** -0.5
NUM_BUCKETS = 32
MAX_DISTANCE = 2048
RMS_EPS = 1e-6
NEG_INF = -1e30
FORCE_SCORE = 1e9
DIL_CONFIGS = ((128, 1), (512, 4), (2048, 16))
B_KV_HEADS = 2
B_WINDOW = 128
C_KV_HEADS = 2
CMP_BLOCK = 32
CMP_STRIDE = 16
SEL_BLOCK = 64
N_SELECT = 16
C_WINDOW = 512
CONV_WIDTH = 3
QBLK = 128
SEL_CHUNK = 512
VMEM_LIMIT = 56 * 1024 * 1024


def _cparams(*sem):
    return pltpu.CompilerParams(dimension_semantics=sem, vmem_limit_bytes=VMEM_LIMIT)


def _t5_bucket_np(dist):
    max_exact = NUM_BUCKETS // 2
    d = np.maximum(dist, 0)
    df = np.maximum(d, 1).astype(np.float64)
    large = max_exact + np.floor(np.log(df / max_exact) / math.log(MAX_DISTANCE / max_exact)
                                 * (NUM_BUCKETS - max_exact) + 1e-9).astype(np.int64)
    large = np.minimum(large, NUM_BUCKETS - 1)
    return np.where(d < max_exact, d, large).astype(np.int32)


def _rms(x, g):
    ms = jnp.mean(x * x, axis=-1, keepdims=True)
    return (x * lax.rsqrt(ms + RMS_EPS)) * g


def _split3(w):
    hi = w.astype(BF16)
    r1 = w - hi.astype(F32)
    mid = r1.astype(BF16)
    lo = (r1 - mid.astype(F32)).astype(BF16)
    return hi, mid, lo


def _expand_heads(w, e3):
    lane = lax.broadcasted_iota(jnp.int32, w.shape, 1)
    r1 = w - w.astype(BF16).astype(F32)
    r2 = r1 - r1.astype(BF16).astype(F32)
    packed = jnp.where(lane < N_HEADS, w,
                       jnp.where(lane < 2 * N_HEADS, pltpu.roll(r1, N_HEADS, 1), pltpu.roll(r2, 2 * N_HEADS, 1)))
    return jnp.dot(packed.astype(BF16), e3, preferred_element_type=F32)


def _dot_nt(a, b):
    return lax.dot_general(a, b, (((1,), (1,)), ((), ())), preferred_element_type=F32)


def _stack_heads(q_tile, heads):
    lane = lax.broadcasted_iota(jnp.int32, (QBLK, LANES), 1)
    lo = lane < HEAD_DIM
    pieces = []
    for h in heads:
        q2 = q_tile(h // 2)
        keep = lo if h % 2 == 0 else jnp.logical_not(lo)
        pieces.append(jnp.where(keep, q2, jnp.zeros_like(q2)))
    return jnp.concatenate(pieces, axis=0)


def _merge_pairs(o, n_heads):
    lane = lax.broadcasted_iota(jnp.int32, (QBLK, LANES), 1)
    lo = lane < HEAD_DIM
    out = []
    for t in range(0, n_heads, 2):
        out.append(jnp.where(lo, o[t * QBLK:(t + 1) * QBLK], o[(t + 1) * QBLK:(t + 2) * QBLK]))
    return out


def _bias_kernel(tab_ref, rng_ref, idx_ref, o_ref):
    n = pl.program_id(0)
    idx = idx_ref[0]
    o_ref[...] = jnp.full(o_ref.shape, NEG_INF, F32)

    def body(b, carry):
        hit = idx == b
        for h in range(N_HEADS):
            o_ref[0, h] = jnp.where(hit, tab_ref[b, h], o_ref[0, h])
        return carry

    lax.fori_loop(rng_ref[n, 0], rng_ref[n, 1] + 1, body, 0)


def bias_expand(table, idx):
    n, r, c = idx.shape
    flat = idx.reshape(n, -1)
    lo = np.where(flat >= 0, flat, NUM_BUCKETS).min(axis=1)
    rng = np.stack([lo, flat.max(axis=1)], axis=1).astype(np.int32)
    return pl.pallas_call(
        _bias_kernel,
        grid=(n,),
        in_specs=[pl.BlockSpec(memory_space=pltpu.SMEM),
                  pl.BlockSpec(memory_space=pltpu.SMEM),
                  pl.BlockSpec((1, r, c), lambda i: (i, 0, 0))],
        out_specs=pl.BlockSpec((1, N_HEADS, r, c), lambda i: (i, 0, 0, 0)),
        out_shape=jax.ShapeDtypeStruct((n, N_HEADS, r, c), F32),
        compiler_params=_cparams("parallel"),
        name="bias_expand",
    )(table, jnp.asarray(rng), jnp.asarray(idx))


def _banded_idx(n_prev, max_dist, stride):
    kb = (n_prev + 1) * QBLK
    dist = np.arange(QBLK)[:, None] + n_prev * QBLK - np.arange(kb)[None, :]
    valid = (dist >= 0) & (dist <= max_dist)
    idx = np.where(valid, _t5_bucket_np(dist * stride), -1).astype(np.int32)
    if n_prev > 1:
        return idx[None]
    first = np.where(np.arange(kb)[None, :] < n_prev * QBLK, -1, idx).astype(np.int32)
    return np.stack([idx, first])


def _norm_mm_kernel(x_ref, g_ref, w_ref, o_ref, xn_ref):
    @pl.when(pl.program_id(1) == 0)
    def _():
        xn_ref[...] = _rms(x_ref[...], g_ref[...]).astype(xn_ref.dtype)

    o_ref[...] = jnp.dot(xn_ref[...], w_ref[...], preferred_element_type=F32).astype(o_ref.dtype)


def norm_matmul(x, gain, w, out_dtype, tm, tn):
    t, d = x.shape
    n = w.shape[1]
    scratch = [pltpu.VMEM((tm, d), BF16)]
    return pl.pallas_call(
        _norm_mm_kernel,
        grid=(t // tm, n // tn),
        in_specs=[pl.BlockSpec((tm, d), lambda i, j: (i, 0)),
                  pl.BlockSpec((1, d), lambda i, j: (0, 0)),
                  pl.BlockSpec((d, tn), lambda i, j: (0, j))],
        out_specs=pl.BlockSpec((tm, tn), lambda i, j: (i, j)),
        out_shape=jax.ShapeDtypeStruct((t, n), out_dtype),
        scratch_shapes=scratch,
        compiler_params=_cparams("parallel", "arbitrary"),
        name="norm_matmul",
    )(x, gain.reshape(1, d), w)


def _norm_perm_kernel(x_ref, g_ref, *refs, dils):
    o_refs, xs_ref = refs[:len(dils)], refs[len(dils)]
    xn = _rms(x_ref[...], g_ref[...])
    nc = xs_ref.shape[0]
    for c in range(nc):
        xs_ref[c] = xn[:, c * LANES:(c + 1) * LANES]
    for o_ref, dil in zip(o_refs, dils):
        if dil == 1:
            o_ref[...] = xn.astype(o_ref.dtype)
            continue
        span = QBLK * dil
        for u in range(x_ref.shape[0] // span):
            for r in range(dil):
                rows = jnp.concatenate(
                    [xs_ref[c, pl.ds(u * span + r, QBLK, stride=dil), :] for c in range(nc)], axis=1)
                dst = u * span + r * QBLK
                o_ref[dst:dst + QBLK, :] = rows.astype(o_ref.dtype)


def norm_permute(x, gain, dils, tm):
    t, d = x.shape
    assert all(tm % (QBLK * dil) == 0 for dil in dils)
    return pl.pallas_call(
        functools.partial(_norm_perm_kernel, dils=tuple(dils)),
        grid=(t // tm,),
        in_specs=[pl.BlockSpec((tm, d), lambda i: (i, 0)),
                  pl.BlockSpec((1, d), lambda i: (0, 0))],
        out_specs=[pl.BlockSpec((tm, d), lambda i: (i, 0)) for _ in dils],
        out_shape=[jax.ShapeDtypeStruct((t, d), BF16) for _ in dils],
        scratch_shapes=[pltpu.VMEM((d // LANES, tm, LANES), F32)],
        compiler_params=_cparams("parallel"),
        name="norm_permute",
    )(x, gain.reshape(1, d))


def _mm_kernel(x_ref, w_ref, o_ref):
    o_ref[...] = jnp.dot(x_ref[...], w_ref[...], preferred_element_type=F32).astype(o_ref.dtype)


def matmul_resident(x, w, out_dtype, tm):
    t, k = x.shape
    n = w.shape[1]
    return pl.pallas_call(
        _mm_kernel,
        grid=(t // tm,),
        in_specs=[pl.BlockSpec((tm, k), lambda i: (i, 0)),
                  pl.BlockSpec((k, n), lambda i: (0, 0))],
        out_specs=pl.BlockSpec((tm, n), lambda i: (i, 0)),
        out_shape=jax.ShapeDtypeStruct((t, n), out_dtype),
        compiler_params=_cparams("parallel"),
        name="matmul_resident",
    )(x, w)


def _finish_proj(y, w_ref, g_ref, h_ref, o_ref):
    z = jnp.dot(y.astype(BF16), w_ref[...], preferred_element_type=F32)
    o_ref[...] = h_ref[...] + _rms(z, g_ref[...])


def _proj_kernel(y_ref, w_ref, g_ref, h_ref, o_ref):
    _finish_proj(y_ref[...], w_ref, g_ref, h_ref, o_ref)


def _unpermute(src_ref, dst_ref, dil, tm, i):
    span = QBLK * dil
    nc = dst_ref.shape[0]
    if span <= tm:
        for u in range(tm // span):
            for r in range(dil):
                lo = u * span + r * QBLK
                for c in range(nc):
                    dst_ref[c, pl.ds(u * span + r, QBLK, stride=dil), :] = src_ref[lo:lo + QBLK,
                                                                                   c * LANES:(c + 1) * LANES]
    else:
        per = tm // dil
        off = (i % (span // tm)) * per
        for r in range(dil):
            lo = pl.multiple_of(r * QBLK + off, 8)
            for c in range(nc):
                dst_ref[c, pl.ds(r, per, stride=dil), :] = src_ref[pl.ds(lo, per), c * LANES:(c + 1) * LANES]
    return jnp.concatenate([dst_ref[c] for c in range(nc)], axis=1) if nc > 1 else dst_ref[0]


def _proj_a_kernel(*refs, dils, tm):
    n = len(dils)
    o_refs, l_refs = refs[:n], refs[n:2 * n]
    e_ref, w_ref, g_ref, h_ref, out_ref = refs[2 * n:2 * n + 5]
    scratch = refs[2 * n + 5:]
    i = pl.program_id(0)
    outs, lses = [], []
    si = 0
    for gi, dil in enumerate(dils):
        if dil == 1:
            outs.append(o_refs[gi][...])
            lses.append(l_refs[gi][...])
        else:
            outs.append(_unpermute(o_refs[gi], scratch[si], dil, tm, i))
            lses.append(_unpermute(l_refs[gi], scratch[si + 1], dil, tm, i))
            si += 2
    mx = functools.reduce(jnp.maximum, lses)
    es = [jnp.exp(l - mx) for l in lses]
    inv = 1.0 / functools.reduce(lambda a, b: a + b, es)
    e = e_ref[...]
    y = outs[0] * _expand_heads(es[0] * inv, e)
    for gi in range(1, n):
        y = y + outs[gi] * _expand_heads(es[gi] * inv, e)
    _finish_proj(y, w_ref, g_ref, h_ref, out_ref)


def combine_proj_a(outs, lses, dils, e, w, gain, h, tm):
    t, d = h.shape

    def row_spec(width, dil):
        rows = max(tm, QBLK * dil)
        return pl.BlockSpec((rows, width), lambda i, q=rows // tm: (i // q, 0))

    in_specs = [row_spec(HQ, dil) for dil in dils] + [row_spec(LANES, dil) for dil in dils] + [
        pl.BlockSpec(e.shape, lambda i: (0, 0)),
        pl.BlockSpec(w.shape, lambda i: (0, 0)),
        pl.BlockSpec((1, d), lambda i: (0, 0)),
        pl.BlockSpec((tm, d), lambda i: (i, 0))]
    scratch = []
    for dil in dils:
        if dil > 1:
            scratch += [pltpu.VMEM((HQ // LANES, tm, LANES), F32), pltpu.VMEM((1, tm, LANES), F32)]
    return pl.pallas_call(
        functools.partial(_proj_a_kernel, dils=tuple(dils), tm=tm),
        grid=(t // tm,),
        in_specs=in_specs,
        out_specs=pl.BlockSpec((tm, d), lambda i: (i, 0)),
        out_shape=jax.ShapeDtypeStruct((t, d), F32),
        scratch_shapes=scratch,
        compiler_params=_cparams("arbitrary"),
        name="dilated_combine_proj",
    )(*outs, *lses, e, w, gain.reshape(1, d), h)


def _proj_c_kernel(oc_ref, os_ref, ow_ref, gr_ref, e_ref, w_ref, g_ref, h_ref, o_ref):
    sig = jax.nn.sigmoid(gr_ref[...])
    e = e_ref[...]
    gate = lambda i: _expand_heads(sig if i == 0 else pltpu.roll(sig, LANES - i * N_HEADS, 1), e)
    y = gate(0) * oc_ref[...]
    y = y + gate(1) * os_ref[...]
    y = y + gate(2) * ow_ref[...]
    _finish_proj(y, w_ref, g_ref, h_ref, o_ref)


def _proj_call(kernel, name, tm, row_inputs, const_inputs, w, gain, h):
    t, d = h.shape
    k = w.shape[0]
    row_specs = [pl.BlockSpec((tm, a.shape[1]), lambda i: (i, 0)) for a in row_inputs]
    const_specs = [pl.BlockSpec(a.shape, (lambda nd: (lambda i: (0,) * nd))(a.ndim)) for a in const_inputs]
    return pl.pallas_call(
        kernel,
        grid=(t // tm,),
        in_specs=row_specs + const_specs + [
            pl.BlockSpec((k, d), lambda i: (0, 0)),
            pl.BlockSpec((1, d), lambda i: (0, 0)),
            pl.BlockSpec((tm, d), lambda i: (i, 0))],
        out_specs=pl.BlockSpec((tm, d), lambda i: (i, 0)),
        out_shape=jax.ShapeDtypeStruct((t, d), F32),
        compiler_params=_cparams("parallel"),
        name=name,
    )(*row_inputs, *const_inputs, w, gain.reshape(1, d), h)


def _head_expand_matrix():
    e = np.zeros((LANES, HQ), np.float32)
    for term in range(3):
        for h in range(N_HEADS):
            e[term * N_HEADS + h, h * HEAD_DIM:(h + 1) * HEAD_DIM] = 1.0
    return e


def _banded_kernel(*refs, tq, n_prev, n_chunks, stack, with_sinks, with_lse):
    q_ref, kp_ref, kc_ref, vp_ref, vc_ref, bias_ref = refs[:6]
    pos = 6
    sink_ref = None
    if with_sinks:
        sink_ref = refs[pos]
        pos += 1
    o_ref = refs[pos]
    pos += 1
    lse_ref = None
    if with_lse:
        lse_ref = refs[pos]
        pos += 1
    kbuf, vbuf = refs[pos], refs[pos + 1]

    i = pl.program_id(1)
    hpc = N_HEADS // n_chunks
    kb = (n_prev + 1) * QBLK
    kbuf[0:tq] = kp_ref[0]
    kbuf[tq:2 * tq] = kc_ref[0]
    vbuf[0:tq] = vp_ref[0]
    vbuf[tq:2 * tq] = vc_ref[0]
    lane = lax.broadcasted_iota(jnp.int32, (QBLK, LANES), 1)
    head_row = lax.broadcasted_iota(jnp.int32, (stack * QBLK, 1), 0)

    for s in range(tq // QBLK):
        r0 = s * QBLK
        k0 = tq + r0 - n_prev * QBLK
        variants = bias_ref.shape[0] > 1
        if variants:
            first = jnp.where(i == 0, 1, 0)
        else:
            col = lax.broadcasted_iota(jnp.int32, (1, kb), 1) + k0
            kmask = jnp.where(jnp.logical_and(i == 0, col < tq), NEG_INF, 0.0).astype(F32)
        lse_acc = jnp.zeros((QBLK, LANES), F32)
        for h0 in range(0, N_HEADS, stack):
            heads = list(range(h0, h0 + stack))
            c = h0 // hpc
            qst = _stack_heads(lambda p: q_ref[0, r0:r0 + QBLK, p * LANES:(p + 1) * LANES], heads)
            kx = kbuf[k0:k0 + kb, c * LANES:(c + 1) * LANES]
            vx = vbuf[k0:k0 + kb, c * LANES:(c + 1) * LANES]
            sc = _dot_nt(qst, kx)
            if variants:
                sc = sc + bias_ref[first, h0:h0 + stack].reshape(stack * QBLK, kb)
            else:
                sc = sc + bias_ref[0, h0:h0 + stack].reshape(stack * QBLK, kb) + kmask
            m = jnp.max(sc, axis=-1, keepdims=True)
            if with_sinks:
                sk = sink_ref[h0 + stack - 1]
                for t in range(stack - 2, -1, -1):
                    sk = jnp.where(head_row < (t + 1) * QBLK, sink_ref[h0 + t], sk)
                m = jnp.maximum(m, sk)
            p = jnp.exp(sc - m)
            den = jnp.sum(p, axis=-1, keepdims=True)
            norm = den + jnp.exp(sk - m) if with_sinks else den
            o = jnp.dot(p.astype(BF16), vx, preferred_element_type=F32) * (1.0 / norm)
            for t2, blk in enumerate(_merge_pairs(o, stack)):
                pidx = h0 // 2 + t2
                o_ref[0, r0:r0 + QBLK, pidx * LANES:(pidx + 1) * LANES] = blk.astype(o_ref.dtype)
            if with_lse:
                lse = m + jnp.log(den)
                for t, h in enumerate(heads):
                    lse_acc = jnp.where(lane == h, lse[t * QBLK:(t + 1) * QBLK], lse_acc)
        if with_lse:
            lse_ref[0, r0:r0 + QBLK, :] = lse_acc


def banded_attention(q_arr, k_arr, v_arr, bias, *, n_rows, n_tiles, tq, n_prev, kw,
                     q_map, k_map, v_map, o_map, out_shape, lse_shape=None, sinks=None, stack=2, name):
    n_chunks = kw // LANES
    assert stack % 2 == 0 and (N_HEADS // n_chunks) % stack == 0
    kb = (n_prev + 1) * QBLK
    with_sinks = sinks is not None
    with_lse = lse_shape is not None

    def prev(fn):
        return lambda n, i: fn(n, jnp.maximum(i - 1, 0))

    in_specs = [pl.BlockSpec((1, tq, HQ), q_map),
                pl.BlockSpec((1, tq, kw), prev(k_map)),
                pl.BlockSpec((1, tq, kw), k_map),
                pl.BlockSpec((1, tq, kw), prev(v_map)),
                pl.BlockSpec((1, tq, kw), v_map),
                pl.BlockSpec(bias.shape, lambda n, i: (0, 0, 0, 0))]
    assert bias.shape[1:] == (N_HEADS, QBLK, kb) and (bias.shape[0] == 1 or tq == QBLK)
    args = [q_arr, k_arr, k_arr, v_arr, v_arr, bias]
    if with_sinks:
        in_specs.append(pl.BlockSpec(memory_space=pltpu.SMEM))
        args.append(sinks)
    out_specs = [pl.BlockSpec((1, tq, HQ), o_map)]
    out_shapes = [jax.ShapeDtypeStruct(out_shape, F32)]
    if with_lse:
        out_specs.append(pl.BlockSpec((1, tq, LANES), o_map))
        out_shapes.append(jax.ShapeDtypeStruct(lse_shape, F32))
    res = pl.pallas_call(
        functools.partial(_banded_kernel, tq=tq, n_prev=n_prev, n_chunks=n_chunks, stack=stack,
                          with_sinks=with_sinks, with_lse=with_lse),
        grid=(n_rows, n_tiles),
        in_specs=in_specs,
        out_specs=out_specs,
        out_shape=out_shapes,
        scratch_shapes=[pltpu.VMEM((2 * tq, kw), BF16), pltpu.VMEM((2 * tq, kw), BF16)],
        compiler_params=_cparams("parallel", "arbitrary"),
        name=name,
    )(*args)
    return res


def _dup_groups(x):
    g0, g1 = x[..., :HEAD_DIM], x[..., HEAD_DIM:]
    return jnp.concatenate([g0, g0, g1, g1], axis=-1)


def _compress_kernel(ch_ref, pos_ref, w1_ref, w2_ref, o_ref):
    ch = ch_ref[0, 0, 0]
    rows = ch.shape[0]
    posv = pos_ref[0]
    a = jnp.dot((ch + posv[0:1]).astype(BF16), w1_ref[0, 0], preferred_element_type=F32)
    b = jnp.dot((ch + posv[1:2]).astype(BF16), w1_ref[0, 1], preferred_element_type=F32)
    hid = a + pltpu.roll(b, rows - 1, 0)
    act = jax.nn.gelu(hid, approximate=True)
    o_ref[0, 0, 0] = jnp.dot(act.astype(BF16), w2_ref[0], preferred_element_type=F32)


def compress(chunks, pos, w1, w2):
    _, b, g, rows, width = chunks.shape
    hid = w1.shape[-1]
    return pl.pallas_call(
        _compress_kernel,
        grid=(2, b, g),
        in_specs=[pl.BlockSpec((1, 1, 1, rows, width), lambda i, bb, gg: (i, bb, gg, 0, 0)),
                  pl.BlockSpec((1, 2, width), lambda i, bb, gg: (i, 0, 0)),
                  pl.BlockSpec((1, 2, width, hid), lambda i, bb, gg: (i, 0, 0, 0)),
                  pl.BlockSpec((1, hid, HEAD_DIM), lambda i, bb, gg: (i, 0, 0))],
        out_specs=pl.BlockSpec((1, 1, 1, rows, HEAD_DIM), lambda i, bb, gg: (i, bb, gg, 0, 0)),
        out_shape=jax.ShapeDtypeStruct((2, b, g, rows, HEAD_DIM), F32),
        compiler_params=_cparams("parallel", "parallel", "parallel"),
        name="nsa_compress",
    )(chunks, pos, w1, w2)


def _cmp_attn_kernel(q_ref, kc_ref, vc_ref, ov_ref, place_ref, o_ref, sel_ref, *, n_sel_blocks, k_sel):
    qi = pl.program_id(1)
    ncr = kc_ref.shape[1]
    hpc = N_HEADS // C_KV_HEADS
    qpos = qi * QBLK + lax.broadcasted_iota(jnp.int32, (QBLK, 1), 0)
    cidx = lax.broadcasted_iota(jnp.int32, (1, ncr), 1)
    valid = (cidx * CMP_STRIDE + (CMP_BLOCK - 1)) <= qpos
    maskc = jnp.where(valid, 0.0, NEG_INF).astype(F32)
    anyv = (qpos >= CMP_BLOCK - 1).astype(F32)
    blk_id = lax.broadcasted_iota(jnp.int32, (HEAD_DIM, QBLK), 0)
    cur = (qi * QBLK + lax.broadcasted_iota(jnp.int32, (HEAD_DIM, QBLK), 1)) // SEL_BLOCK
    forced = jnp.logical_or(jnp.logical_or(blk_id == 0, blk_id == cur), blk_id == cur - 1)
    allowed = blk_id <= cur
    for g in range(C_KV_HEADS):
        heads = list(range(g * hpc, (g + 1) * hpc))
        qst = _stack_heads(lambda p: q_ref[0, :, p * LANES:(p + 1) * LANES], heads)
        sc = _dot_nt(qst, kc_ref[0, :, g * LANES:(g + 1) * LANES])
        sc3 = sc.reshape(hpc, QBLK, ncr) + maskc[None]
        m = jnp.max(sc3, axis=-1, keepdims=True)
        e = jnp.exp(sc3 - m)
        ssum = jnp.sum(e, axis=-1, keepdims=True)
        p = (e * (1.0 / ssum)) * anyv[None]
        o = jnp.dot(p.reshape(hpc * QBLK, ncr).astype(BF16), vc_ref[0, :, g * LANES:(g + 1) * LANES],
                    preferred_element_type=F32)
        for t2, blk in enumerate(_merge_pairs(o, hpc)):
            pidx = (g * hpc) // 2 + t2
            o_ref[0, :, pidx * LANES:(pidx + 1) * LANES] = blk
        hi, mid, lo = _split3(jnp.sum(p, axis=0))
        ovt = ov_ref[...]
        imp = (_dot_nt(ovt, hi) + _dot_nt(ovt, mid)) + _dot_nt(ovt, lo)
        score = jnp.where(forced, FORCE_SCORE, jnp.where(allowed, imp, NEG_INF))
        bits = pltpu.bitcast(score, jnp.int32)
        key = jnp.where(bits < 0, bits ^ jnp.int32(0x7FFFFFFF), bits)
        key_m1 = key - 1
        rank = jnp.zeros((HEAD_DIM, QBLK), jnp.int32)
        for i in range(n_sel_blocks):
            thr = jnp.where(blk_id > i, key_m1, key)
            rank = rank + jnp.where(key[i:i + 1, :] > thr, 1, 0)
        keep = jnp.logical_and(rank < k_sel, blk_id < n_sel_blocks)
        pen_t = jnp.where(keep, 0.0, -1.0).astype(BF16)
        pen = lax.dot_general(pen_t, place_ref[...], (((0,), (0,)), ((), ())), preferred_element_type=F32)
        sel_ref[0, :, g * LANES:(g + 1) * LANES] = pen.astype(sel_ref.dtype)


def cmp_attention(q, kcd, vcd, ov, n_sel_blocks, k_sel):
    b, s, _ = q.shape
    ncr = kcd.shape[1]
    place = np.zeros((HEAD_DIM, LANES), np.float32)
    place[np.arange(HEAD_DIM), HEAD_DIM + np.arange(HEAD_DIM)] = 1.0
    return pl.pallas_call(
        functools.partial(_cmp_attn_kernel, n_sel_blocks=n_sel_blocks, k_sel=k_sel),
        grid=(b, s // QBLK),
        in_specs=[pl.BlockSpec((1, QBLK, HQ), lambda bb, i: (bb, i, 0)),
                  pl.BlockSpec((1, ncr, 2 * LANES), lambda bb, i: (bb, 0, 0)),
                  pl.BlockSpec((1, ncr, 2 * LANES), lambda bb, i: (bb, 0, 0)),
                  pl.BlockSpec((HEAD_DIM, ncr), lambda bb, i: (0, 0)),
                  pl.BlockSpec((HEAD_DIM, LANES), lambda bb, i: (0, 0))],
        out_specs=[pl.BlockSpec((1, QBLK, HQ), lambda bb, i: (bb, i, 0)),
                   pl.BlockSpec((1, QBLK, 2 * LANES), lambda bb, i: (bb, i, 0))],
        out_shape=[jax.ShapeDtypeStruct((b, s, HQ), F32),
                   jax.ShapeDtypeStruct((b, s, 2 * LANES), BF16)],
        compiler_params=_cparams("parallel", "parallel"),
        name="nsa_cmp_attention",
    )(q, kcd, vcd, ov, jnp.asarray(place, BF16))


def _sel_attn_kernel(q_ref, pen_ref, k_ref, v_ref, bias_ref, o_ref, qst_ref, m_ref, acc_ref, *, nbt):
    qi = pl.program_id(2)
    hpc = N_HEADS // C_KV_HEADS
    nsub = SEL_CHUNK // QBLK
    lane = lax.broadcasted_iota(jnp.int32, (QBLK, LANES), 1)
    lo = lane < HEAD_DIM
    pen = pen_ref[0].astype(F32)
    for t in range(hpc):
        q2 = q_ref[0, :, (t // 2) * LANES:(t // 2 + 1) * LANES].astype(F32)
        if t % 2:
            q2 = pltpu.roll(q2, HEAD_DIM, 1)
        qst_ref[t * QBLK:(t + 1) * QBLK, :] = jnp.where(lo, q2, pen).astype(BF16)
    m_ref[...] = jnp.full(m_ref.shape, NEG_INF, F32)
    acc_ref[...] = jnp.zeros(acc_ref.shape, F32)

    def body(c, carry):
        k0 = pl.multiple_of(c * SEL_CHUNK, SEL_CHUNK)
        s = _dot_nt(qst_ref[...], k_ref[0, pl.ds(k0, SEL_CHUNK), :])
        tiles = []
        for u in range(nsub):
            off = qi - nsub * c - u
            idx = jnp.where(off < 0, nbt, jnp.minimum(off, nbt - 1))
            tiles.append(s[:, u * QBLK:(u + 1) * QBLK] + bias_ref[idx].reshape(hpc * QBLK, QBLK))
        m_old = m_ref[...]
        m_new = jnp.maximum(m_old, jnp.max(functools.reduce(jnp.maximum, tiles), axis=-1, keepdims=True))
        p = jnp.concatenate([jnp.exp(tl - m_new).astype(BF16) for tl in tiles], axis=1)
        pv = jnp.dot(p, v_ref[0, pl.ds(k0, SEL_CHUNK), :], preferred_element_type=F32)
        acc_ref[...] = jnp.exp(m_old - m_new) * acc_ref[...] + pv
        m_ref[...] = m_new
        return carry

    lax.fori_loop(0, qi // nsub + 1, body, 0)
    acc = acc_ref[...]
    rolled = pltpu.roll(acc, HEAD_DIM, 1)
    for t2 in range(hpc // 2):
        ev = slice(2 * t2 * QBLK, (2 * t2 + 1) * QBLK)
        od = slice((2 * t2 + 1) * QBLK, (2 * t2 + 2) * QBLK)
        even = acc[ev] * (1.0 / rolled[ev])
        odd = rolled[od] * (1.0 / acc[od])
        o_ref[0, :, t2 * LANES:(t2 + 1) * LANES] = jnp.where(lo, even, odd)


def sel_attention(q, kx, vx, pen, bias_tiles):
    b, s, _ = q.shape
    hpc = N_HEADS // C_KV_HEADS
    gw = hpc * HEAD_DIM
    nbt = bias_tiles.shape[0] - 1
    assert s % SEL_CHUNK == 0
    return pl.pallas_call(
        functools.partial(_sel_attn_kernel, nbt=nbt),
        grid=(b, C_KV_HEADS, s // QBLK),
        in_specs=[pl.BlockSpec((1, QBLK, gw), lambda bb, g, i: (bb, i, g)),
                  pl.BlockSpec((1, QBLK, LANES), lambda bb, g, i: (bb, i, g)),
                  pl.BlockSpec((1, s, LANES), lambda bb, g, i: (bb, 0, g)),
                  pl.BlockSpec((1, s, LANES), lambda bb, g, i: (bb, 0, g)),
                  pl.BlockSpec((nbt + 1, hpc, QBLK, QBLK), lambda bb, g, i: (0, g, 0, 0))],
        out_specs=pl.BlockSpec((1, QBLK, gw), lambda bb, g, i: (bb, i, g)),
        out_shape=jax.ShapeDtypeStruct((b, s, HQ), F32),
        scratch_shapes=[pltpu.VMEM((hpc * QBLK, LANES), BF16),
                        pltpu.VMEM((hpc * QBLK, LANES), F32),
                        pltpu.VMEM((hpc * QBLK, LANES), F32)],
        compiler_params=_cparams("parallel", "parallel", "arbitrary"),
        name="nsa_sel_attention",
    )(q, pen, kx, vx, bias_tiles)


def _sel_bias_idx(s):
    nqt = s // QBLK
    far = -(-(int(np.argmax(_t5_bucket_np(np.arange(4 * MAX_DISTANCE)) == NUM_BUCKETS - 1)) + QBLK) // QBLK)
    nbt = min(nqt, far + 1)
    d0 = np.arange(nbt)[:, None, None] * QBLK
    dist = d0 + np.arange(QBLK)[None, :, None] - np.arange(QBLK)[None, None, :]
    idx = np.where(dist >= 0, _t5_bucket_np(dist), -1).astype(np.int32)
    return np.concatenate([idx, np.full((1, QBLK, QBLK), -1, np.int32)])


def _ffn_kernel(h_ref, g2_ref, wu_ref, cw_ref, cb_ref, wd_ref, g3_ref,
                o_ref, xn_ref, acc_ref, ua_ref, ub_ref, aa_ref, ab_ref, carry_ref, *, tiles_per_seq, nj):
    i = pl.program_id(0)
    tm, tn = xn_ref.shape[0], aa_ref.shape[1]
    cols = lambda c: slice(c * tn, (c + 1) * tn)
    xn_ref[...] = _rms(h_ref[...], g2_ref[...]).astype(xn_ref.dtype)
    acc_ref[...] = jnp.zeros(acc_ref.shape, F32)

    @pl.when(i % tiles_per_seq == 0)
    def _():
        carry_ref[...] = jnp.zeros(carry_ref.shape, F32)

    top = 16
    rowt = lax.broadcasted_iota(jnp.int32, (top, tn), 0)

    def up(jj, u_ref):
        xn = xn_ref[...]
        u_ref[0] = jnp.dot(xn, wu_ref[:, cols(jj)], preferred_element_type=F32)
        u_ref[1] = jnp.dot(xn, wu_ref[:, cols(nj + jj)], preferred_element_type=F32)

    def conv(u, s1, s2, c):
        cw = cw_ref[:, cols(c)]
        return ((cb_ref[:, cols(c)] + u * cw[2:3]) + s2 * cw[0:1]) + s1 * cw[1:2]

    def conv_body(u, c):
        return conv(u, pltpu.roll(u, 1, 0), pltpu.roll(u, 2, 0), c)

    def conv_top(u_ref, c):
        prev = carry_ref[c]
        p1, p2 = prev[7:8], prev[6:7]
        u = u_ref[0:top]
        s1 = jnp.where(rowt == 0, p1, pltpu.roll(u, 1, 0))
        s2 = jnp.where(rowt == 0, p2, jnp.where(rowt == 1, p1, pltpu.roll(u, 2, 0)))
        carry_ref[c] = u_ref[tm - 8:tm]
        return conv(u, s1, s2, c)

    def gated(cg, cv):
        return (jax.nn.gelu(cg, approximate=True) * cv).astype(BF16)

    def act(j, u_ref, a_ref):
        a_ref[...] = gated(conv_body(u_ref[0], j), conv_body(u_ref[1], nj + j))
        a_ref[0:top] = gated(conv_top(u_ref.at[0], j), conv_top(u_ref.at[1], nj + j))

    def down(j, a_ref):
        acc_ref[...] += jnp.dot(a_ref[...], wd_ref[j * tn:(j + 1) * tn, :], preferred_element_type=F32)

    up(0, ua_ref)
    up(1, ub_ref)
    act(0, ua_ref, aa_ref)

    for jj in range((nj - 3) // 2):
        j = 2 * jj + 1
        up(j + 1, ua_ref)
        act(j, ub_ref, ab_ref)
        down(j - 1, aa_ref)
        up(j + 2, ub_ref)
        act(j + 1, ua_ref, aa_ref)
        down(j, ab_ref)
    up(nj - 1, ua_ref)
    act(nj - 2, ub_ref, ab_ref)
    down(nj - 3, aa_ref)
    act(nj - 1, ua_ref, aa_ref)
    down(nj - 2, ab_ref)
    down(nj - 1, aa_ref)
    o_ref[...] = h_ref[...] + _rms(acc_ref[...], g3_ref[...])


def conv_ffn(h, g2, w_up, conv_w, conv_b, w_down, g3, seq, tm, tn):
    t, d = h.shape
    dff = w_down.shape[0]
    nj = dff // tn
    assert nj * tn == dff and nj % 2 == 1 and nj >= 3
    wu, cw, cb, wd = w_up, conv_w, conv_b.reshape(1, -1), w_down
    resident = lambda a: pl.BlockSpec(a.shape, (lambda nd: (lambda i: (0,) * nd))(a.ndim))
    return pl.pallas_call(
        functools.partial(_ffn_kernel, tiles_per_seq=seq // tm, nj=nj),
        grid=(t // tm,),
        in_specs=[pl.BlockSpec((tm, d), lambda i: (i, 0)),
                  pl.BlockSpec((1, d), lambda i: (0, 0)),
                  resident(wu), resident(cw), resident(cb), resident(wd),
                  pl.BlockSpec((1, d), lambda i: (0, 0))],
        out_specs=pl.BlockSpec((tm, d), lambda i: (i, 0)),
        out_shape=jax.ShapeDtypeStruct((t, d), F32),
        scratch_shapes=[pltpu.VMEM((tm, d), BF16), pltpu.VMEM((tm, d), F32),
                        pltpu.VMEM((2, tm, tn), F32), pltpu.VMEM((2, tm, tn), F32),
                        pltpu.VMEM((tm, tn), BF16), pltpu.VMEM((tm, tn), BF16),
                        pltpu.VMEM((2 * nj, 8, tn), F32)],
        compiler_params=_cparams("arbitrary"),
        name="conv_ffn",
    )(h, g2.reshape(1, d), wu, cw, cb, wd, g3.reshape(1, d))


def _row_tile(t):
    for tm in (1024, 512, 256, 128):
        if t % tm == 0:
            return tm
    raise ValueError(f"token count {t} is not a multiple of 128")


def mixer_a(h, gains, w_in, w_o, rel_table, bsz, seq):
    t, d = h.shape
    tm = _row_tile(t)
    n_dil = len(DIL_CONFIGS)
    a_in = w_in.shape[1]
    col_scale = np.ones((a_in,), np.float32).reshape(n_dil, 3, HQ)
    col_scale[:, 0] = ATTN_SCALE
    w = (w_in * col_scale.reshape(1, a_in)).astype(BF16)
    idx = np.concatenate([_banded_idx(1, window // dil, dil) for window, dil in DIL_CONFIGS])
    bias = bias_expand(rel_table, idx)
    outs, lses = [], []
    dils = [dil for _, dil in DIL_CONFIGS]
    blocks_per_seq = seq // QBLK
    xns = norm_permute(h, gains[0], dils, max([tm] + [QBLK * dil for dil in dils]))
    for gi, (window, dil) in enumerate(DIL_CONFIGS):
        assert window // dil <= QBLK and seq % (QBLK * dil) == 0
        qkv = matmul_resident(xns[gi], w[:, gi * 3 * HQ:(gi + 1) * 3 * HQ], BF16, tm)
        qkv = qkv.reshape(1, t, 3 * HQ)

        def rmap(part, dil=dil):
            return lambda n, i: (0, (n // dil) * blocks_per_seq + i * dil + n % dil, part)

        o, lse = banded_attention(
            qkv, qkv, qkv, bias[2 * gi:2 * gi + 2], n_rows=bsz * dil, n_tiles=seq // (QBLK * dil), tq=QBLK, n_prev=1, kw=HQ,
            q_map=rmap(0), k_map=rmap(1), v_map=rmap(2), o_map=rmap(0),
            out_shape=(1, t, HQ), lse_shape=(1, t, LANES), name=f"dilated_attention_{dil}")
        outs.append(o.reshape(t, HQ))
        lses.append(lse.reshape(t, LANES))
    e = jnp.asarray(_head_expand_matrix(), BF16)
    return combine_proj_a(outs, lses, dils, e, w_o.astype(BF16), gains[1], h, min(tm, 512))


def mixer_b(h, gains, w_in, sinks, w_o, rel_table, bsz, seq):
    t, d = h.shape
    tm = _row_tile(t)
    hk = B_KV_HEADS * HEAD_DIM
    n_in = w_in.shape[1]
    col_scale = np.ones((n_in,), np.float32)
    col_scale[:HQ] = ATTN_SCALE
    w = (w_in * col_scale[None]).astype(BF16)
    qkv = norm_matmul(h, gains[0], w, BF16, tm, n_in // 2 if (n_in // 2) % LANES == 0 else n_in)
    q = qkv[:, :HQ].reshape(bsz, seq, HQ)
    kd = _dup_groups(qkv[:, HQ:HQ + hk].reshape(bsz, seq, hk))
    vd = _dup_groups(qkv[:, HQ + hk:].reshape(bsz, seq, hk))
    bias = bias_expand(rel_table, _banded_idx(1, B_WINDOW - 1, 1))
    sink_rows = sinks.astype(F32)
    ident = lambda n, i: (n, i, 0)
    (o,) = banded_attention(q, kd, vd, bias, n_rows=bsz, n_tiles=seq // QBLK, tq=QBLK, n_prev=1,
                            kw=2 * LANES, q_map=ident, k_map=ident, v_map=ident, o_map=ident,
                            out_shape=(bsz, seq, HQ), sinks=sink_rows, stack=8, name="sink_window_attention")
    return _proj_call(_proj_kernel, "sink_proj", tm, [o.reshape(t, HQ)], [], w_o.astype(BF16), gains[1], h)


def mixer_c(h, gains, w_in, cmp_pos, cmp_w1, cmp_w2, w_o, rel_table, bsz, seq):
    t, d = h.shape
    tm = _row_tile(t)
    g = C_KV_HEADS
    hk = g * HEAD_DIM
    rest = w_in.shape[1] - HQ
    rest_pad = -(-rest // LANES) * LANES
    wq = (w_in[:, :HQ] * ATTN_SCALE).astype(BF16)
    wr = jnp.pad(w_in[:, HQ:], ((0, 0), (0, rest_pad - rest))).astype(BF16)
    q = norm_matmul(h, gains[0], wq, BF16, tm, HQ).reshape(bsz, seq, HQ)
    r = norm_matmul(h, gains[0], wr, F32, tm, rest_pad)
    kv = [r[:, i * hk:(i + 1) * hk].reshape(bsz, seq, hk) for i in range(6)]
    gates_raw = r[:, 6 * hk:6 * hk + LANES]

    ncr = seq // CMP_STRIDE
    half = CMP_BLOCK // 2
    assert half == CMP_STRIDE
    chunks = jnp.stack([kv[0], kv[1]]).reshape(2, bsz, ncr, half, g, HEAD_DIM)
    chunks = chunks.transpose(0, 1, 4, 2, 3, 5).reshape(2, bsz, g, ncr, half * HEAD_DIM)
    pos = cmp_pos.reshape(2, 2, half * HEAD_DIM)
    w1 = cmp_w1.reshape(2, 2, half * HEAD_DIM, -1).astype(BF16)
    cmp = compress(chunks, pos, w1, cmp_w2.astype(BF16))
    cmp = cmp.transpose(0, 1, 3, 2, 4).reshape(2, bsz, ncr, hk).astype(BF16)
    kcd, vcd = _dup_groups(cmp[0]), _dup_groups(cmp[1])

    ns = seq // SEL_BLOCK
    assert ns <= HEAD_DIM
    k_sel = min(N_SELECT, ns)
    starts = np.arange(ncr) * CMP_STRIDE
    blk = np.arange(HEAD_DIM)
    ov = ((starts[None, :] < (blk[:, None] + 1) * SEL_BLOCK)
          & (starts[None, :] + CMP_BLOCK > blk[:, None] * SEL_BLOCK)
          & (blk[:, None] < ns) & (starts[None, :] + CMP_BLOCK <= seq))
    o_c, pen = cmp_attention(q, kcd, vcd, jnp.asarray(ov.astype(np.float32), BF16), ns, k_sel)

    sel_bias = bias_expand(rel_table, _sel_bias_idx(seq))
    key_blk = np.arange(seq)[:, None] // SEL_BLOCK == np.arange(HEAD_DIM)[None, :]
    onehot = jnp.broadcast_to(jnp.asarray(np.where(key_blk, -NEG_INF, 0.0), BF16), (bsz, seq, HEAD_DIM))
    ones = jnp.ones((bsz, seq, HEAD_DIM), BF16)
    ks, vs = kv[2].astype(BF16), kv[3].astype(BF16)
    kx = jnp.concatenate([ks[..., :HEAD_DIM], onehot, ks[..., HEAD_DIM:], onehot], axis=-1)
    vx = jnp.concatenate([vs[..., :HEAD_DIM], ones, vs[..., HEAD_DIM:], ones], axis=-1)
    o_s = sel_attention(q, kx, vx, pen, sel_bias)

    n_prev = -(-(C_WINDOW - 1) // QBLK)
    tqw = n_prev * QBLK
    wbias = bias_expand(rel_table, _banded_idx(n_prev, C_WINDOW - 1, 1))
    ident = lambda n, i: (n, i, 0)
    (o_w,) = banded_attention(q, _dup_groups(kv[4].astype(BF16)), _dup_groups(kv[5].astype(BF16)), wbias,
                              n_rows=bsz, n_tiles=seq // tqw, tq=tqw, n_prev=n_prev, kw=2 * LANES,
                              q_map=ident, k_map=ident, v_map=ident, o_map=ident,
                              out_shape=(bsz, seq, HQ), stack=8, name="nsa_window_attention")

    e = jnp.asarray(_head_expand_matrix(), BF16)
    return _proj_call(_proj_c_kernel, "nsa_gate_proj", min(tm, 512),
                      [o_c.reshape(t, HQ), o_s.reshape(t, HQ), o_w.reshape(t, HQ), gates_raw], [e],
                      w_o.astype(BF16), gains[1], h)


def kernel(x, rel_table, norm_gains, a_w_in, a_w_o, b_w_in, b_sinks, b_w_o, c_w_in, c_cmp_pos, c_cmp_w1,
           c_cmp_w2, c_w_o, ffn_w_up, ffn_conv_w, ffn_conv_b, ffn_w_down):
    bsz, seq, d = x.shape
    depth = norm_gains.shape[0]
    h = x.reshape(bsz * seq, d)
    tm = _row_tile(seq)
    for i in range(depth):
        kind, j = i % 3, i // 3
        g = norm_gains[i]
        if kind == 0:
            h = mixer_a(h, g, a_w_in[j], a_w_o[j], rel_table, bsz, seq)
        elif kind == 1:
            h = mixer_b(h, g, b_w_in[j], b_sinks[j], b_w_o[j], rel_table, bsz, seq)
        else:
            h = mixer_c(h, g, c_w_in[j], c_cmp_pos[j], c_cmp_w1[j], c_cmp_w2[j], c_w_o[j], rel_table, bsz, seq)
        h = conv_ffn(h, g[2], ffn_w_up[i].astype(BF16), ffn_conv_w[i], ffn_conv_b[i],
                     ffn_w_down[i].astype(BF16), g[3], seq, min(tm, 256), 256)
    return h.reshape(bsz, seq, d)
```

```python
import functools
import math

import numpy as np
import jax
import jax.numpy as jnp
from jax import lax
from jax.experimental import pallas as pl
from jax.experimental.pallas import tpu as pltpu

F32 = jnp.float32
BF16 = jnp.bfloat16

N_HEADS = 16
HEAD_DIM = 64
HQ = N_HEADS * HEAD_DIM
LANES = 128
ATTN_SCALE = HEAD_DIM ** -0.5
NUM_BUCKETS = 32
MAX_DISTANCE = 2048
RMS_EPS = 1e-6
NEG_INF = -1e30
FORCE_SCORE = 1e9
DIL_CONFIGS = ((128, 1), (512, 4), (2048, 16))
B_KV_HEADS = 2
B_WINDOW = 128
C_KV_HEADS = 2
CMP_BLOCK = 32
CMP_STRIDE = 16
SEL_BLOCK = 64
N_SELECT = 16
C_WINDOW = 512
CONV_WIDTH = 3
QBLK = 128
SEL_CHUNK = 512
VMEM_LIMIT = 56 * 1024 * 1024


def _cparams(*sem):
    return pltpu.CompilerParams(dimension_semantics=sem, vmem_limit_bytes=VMEM_LIMIT)


def _t5_bucket_np(dist):
    max_exact = NUM_BUCKETS // 2
    d = np.maximum(dist, 0)
    df = np.maximum(d, 1).astype(np.float64)
    large = max_exact + np.floor(np.log(df / max_exact) / math.log(MAX_DISTANCE / max_exact)
                                 * (NUM_BUCKETS - max_exact) + 1e-9).astype(np.int64)
    large = np.minimum(large, NUM_BUCKETS - 1)
    return np.where(d < max_exact, d, large).astype(np.int32)


def _rms(x, g):
    ms = jnp.mean(x * x, axis=-1, keepdims=True)
    return (x * lax.rsqrt(ms + RMS_EPS)) * g


def _split3(w):
    hi = w.astype(BF16)
    r1 = w - hi.astype(F32)
    mid = r1.astype(BF16)
    lo = (r1 - mid.astype(F32)).astype(BF16)
    return hi, mid, lo


def _expand_heads(w, e3):
    lane = lax.broadcasted_iota(jnp.int32, w.shape, 1)
    r1 = w - w.astype(BF16).astype(F32)
    r2 = r1 - r1.astype(BF16).astype(F32)
    packed = jnp.where(lane < N_HEADS, w,
                       jnp.where(lane < 2 * N_HEADS, pltpu.roll(r1, N_HEADS, 1), pltpu.roll(r2, 2 * N_HEADS, 1)))
    return jnp.dot(packed.astype(BF16), e3, preferred_element_type=F32)


def _dot_nt(a, b):
    return lax.dot_general(a, b, (((1,), (1,)), ((), ())), preferred_element_type=F32)


def _stack_heads(q_tile, heads):
    lane = lax.broadcasted_iota(jnp.int32, (QBLK, LANES), 1)
    lo = lane < HEAD_DIM
    pieces = []
    for h in heads:
        q2 = q_tile(h // 2)
        keep = lo if h % 2 == 0 else jnp.logical_not(lo)
        pieces.append(jnp.where(keep, q2, jnp.zeros_like(q2)))
    return jnp.concatenate(pieces, axis=0)


def _merge_pairs(o, n_heads):
    lane = lax.broadcasted_iota(jnp.int32, (QBLK, LANES), 1)
    lo = lane < HEAD_DIM
    out = []
    for t in range(0, n_heads, 2):
        out.append(jnp.where(lo, o[t * QBLK:(t + 1) * QBLK], o[(t + 1) * QBLK:(t + 2) * QBLK]))
    return out


def _bias_kernel(tab_ref, rng_ref, idx_ref, o_ref):
    n = pl.program_id(0)
    idx = idx_ref[0]
    o_ref[...] = jnp.full(o_ref.shape, NEG_INF, F32)

    def body(b, carry):
        hit = idx == b
        for h in range(N_HEADS):
            o_ref[0, h] = jnp.where(hit, tab_ref[b, h], o_ref[0, h])
        return carry

    lax.fori_loop(rng_ref[n, 0], rng_ref[n, 1] + 1, body, 0)


def bias_expand(table, idx):
    n, r, c = idx.shape
    flat = idx.reshape(n, -1)
    lo = np.where(flat >= 0, flat, NUM_BUCKETS).min(axis=1)
    rng = np.stack([lo, flat.max(axis=1)], axis=1).astype(np.int32)
    return pl.pallas_call(
        _bias_kernel,
        grid=(n,),
        in_specs=[pl.BlockSpec(memory_space=pltpu.SMEM),
                  pl.BlockSpec(memory_space=pltpu.SMEM),
                  pl.BlockSpec((1, r, c), lambda i: (i, 0, 0))],
        out_specs=pl.BlockSpec((1, N_HEADS, r, c), lambda i: (i, 0, 0, 0)),
        out_shape=jax.ShapeDtypeStruct((n, N_HEADS, r, c), F32),
        compiler_params=_cparams("parallel"),
        name="bias_expand",
    )(table, jnp.asarray(rng), jnp.asarray(idx))


def _banded_idx(n_prev, max_dist, stride):
    kb = (n_prev + 1) * QBLK
    dist = np.arange(QBLK)[:, None] + n_prev * QBLK - np.arange(kb)[None, :]
    valid = (dist >= 0) & (dist <= max_dist)
    idx = np.where(valid, _t5_bucket_np(dist * stride), -1).astype(np.int32)
    if n_prev > 1:
        return idx[None]
    first = np.where(np.arange(kb)[None, :] < n_prev * QBLK, -1, idx).astype(np.int32)
    return np.stack([idx, first])


def _norm_mm_kernel(x_ref, g_ref, w_ref, o_ref, xn_ref):
    @pl.when(pl.program_id(1) == 0)
    def _():
        xn_ref[...] = _rms(x_ref[...], g_ref[...]).astype(xn_ref.dtype)

    o_ref[...] = jnp.dot(xn_ref[...], w_ref[...], preferred_element_type=F32).astype(o_ref.dtype)


def norm_matmul(x, gain, w, out_dtype, tm, tn):
    t, d = x.shape
    n = w.shape[1]
    scratch = [pltpu.VMEM((tm, d), BF16)]
    return pl.pallas_call(
        _norm_mm_kernel,
        grid=(t // tm, n // tn),
        in_specs=[pl.BlockSpec((tm, d), lambda i, j: (i, 0)),
                  pl.BlockSpec((1, d), lambda i, j: (0, 0)),
                  pl.BlockSpec((d, tn), lambda i, j: (0, j))],
        out_specs=pl.BlockSpec((tm, tn), lambda i, j: (i, j)),
        out_shape=jax.ShapeDtypeStruct((t, n), out_dtype),
        scratch_shapes=scratch,
        compiler_params=_cparams("parallel", "arbitrary"),
        name="norm_matmul",
    )(x, gain.reshape(1, d), w)


def _norm_perm_kernel(x_ref, g_ref, *refs, dils):
    o_refs, xs_ref = refs[:len(dils)], refs[len(dils)]
    xn = _rms(x_ref[...], g_ref[...])
    nc = xs_ref.shape[0]
    for c in range(nc):
        xs_ref[c] = xn[:, c * LANES:(c + 1) * LANES]
    for o_ref, dil in zip(o_refs, dils):
        if dil == 1:
            o_ref[...] = xn.astype(o_ref.dtype)
            continue
        span = QBLK * dil
        for u in range(x_ref.shape[0] // span):
            for r in range(dil):
                rows = jnp.concatenate(
                    [xs_ref[c, pl.ds(u * span + r, QBLK, stride=dil), :] for c in range(nc)], axis=1)
                dst = u * span + r * QBLK
                o_ref[dst:dst + QBLK, :] = rows.astype(o_ref.dtype)


def norm_permute(x, gain, dils, tm):
    t, d = x.shape
    assert all(tm % (QBLK * dil) == 0 for dil in dils)
    return pl.pallas_call(
        functools.partial(_norm_perm_kernel, dils=tuple(dils)),
        grid=(t // tm,),
        in_specs=[pl.BlockSpec((tm, d), lambda i: (i, 0)),
                  pl.BlockSpec((1, d), lambda i: (0, 0))],
        out_specs=[pl.BlockSpec((tm, d), lambda i: (i, 0)) for _ in dils],
        out_shape=[jax.ShapeDtypeStruct((t, d), BF16) for _ in dils],
        scratch_shapes=[pltpu.VMEM((d // LANES, tm, LANES), F32)],
        compiler_params=_cparams("parallel"),
        name="norm_permute",
    )(x, gain.reshape(1, d))


def _mm_kernel(x_ref, w_ref, o_ref):
    o_ref[...] = jnp.dot(x_ref[...], w_ref[...], preferred_element_type=F32).astype(o_ref.dtype)


def matmul_resident(x, w, out_dtype, tm):
    t, k = x.shape
    n = w.shape[1]
    return pl.pallas_call(
        _mm_kernel,
        grid=(t // tm,),
        in_specs=[pl.BlockSpec((tm, k), lambda i: (i, 0)),
                  pl.BlockSpec((k, n), lambda i: (0, 0))],
        out_specs=pl.BlockSpec((tm, n), lambda i: (i, 0)),
        out_shape=jax.ShapeDtypeStruct((t, n), out_dtype),
        compiler_params=_cparams("parallel"),
        name="matmul_resident",
    )(x, w)


def _finish_proj(y, w_ref, g_ref, h_ref, o_ref):
    z = jnp.dot(y.astype(BF16), w_ref[...], preferred_element_type=F32)
    o_ref[...] = h_ref[...] + _rms(z, g_ref[...])


def _proj_kernel(y_ref, w_ref, g_ref, h_ref, o_ref):
    _finish_proj(y_ref[...], w_ref, g_ref, h_ref, o_ref)


def _unpermute(src_ref, dst_ref, dil, tm, i):
    span = QBLK * dil
    nc = dst_ref.shape[0]
    if span <= tm:
        for u in range(tm // span):
            for r in range(dil):
                lo = u * span + r * QBLK
                for c in range(nc):
                    dst_ref[c, pl.ds(u * span + r, QBLK, stride=dil), :] = src_ref[lo:lo + QBLK,
                                                                                   c * LANES:(c + 1) * LANES]
    else:
        per = tm // dil
        off = (i % (span // tm)) * per
        for r in range(dil):
            lo = pl.multiple_of(r * QBLK + off, 8)
            for c in range(nc):
                dst_ref[c, pl.ds(r, per, stride=dil), :] = src_ref[pl.ds(lo, per), c * LANES:(c + 1) * LANES]
    return jnp.concatenate([dst_ref[c] for c in range(nc)], axis=1) if nc > 1 else dst_ref[0]


def _proj_a_kernel(*refs, dils, tm):
    n = len(dils)
    o_refs, l_refs = refs[:n], refs[n:2 * n]
    e_ref, w_ref, g_ref, h_ref, out_ref = refs[2 * n:2 * n + 5]
    scratch = refs[2 * n + 5:]
    i = pl.program_id(0)
    outs, lses = [], []
    si = 0
    for gi, dil in enumerate(dils):
        if dil == 1:
            outs.append(o_refs[gi][...])
            lses.append(l_refs[gi][...])
        else:
            outs.append(_unpermute(o_refs[gi], scratch[si], dil, tm, i))
            lses.append(_unpermute(l_refs[gi], scratch[si + 1], dil, tm, i))
            si += 2
    mx = functools.reduce(jnp.maximum, lses)
    es = [jnp.exp(l - mx) for l in lses]
    inv = 1.0 / functools.reduce(lambda a, b: a + b, es)
    e = e_ref[...]
    y = outs[0] * _expand_heads(es[0] * inv, e)
    for gi in range(1, n):
        y = y + outs[gi] * _expand_heads(es[gi] * inv, e)
    _finish_proj(y, w_ref, g_ref, h_ref, out_ref)


def combine_proj_a(outs, lses, dils, e, w, gain, h, tm):
    t, d = h.shape

    def row_spec(width, dil):
        rows = max(tm, QBLK * dil)
        return pl.BlockSpec((rows, width), lambda i, q=rows // tm: (i // q, 0))

    in_specs = [row_spec(HQ, dil) for dil in dils] + [row_spec(LANES, dil) for dil in dils] + [
        pl.BlockSpec(e.shape, lambda i: (0, 0)),
        pl.BlockSpec(w.shape, lambda i: (0, 0)),
        pl.BlockSpec((1, d), lambda i: (0, 0)),
        pl.BlockSpec((tm, d), lambda i: (i, 0))]
    scratch = []
    for dil in dils:
        if dil > 1:
            scratch += [pltpu.VMEM((HQ // LANES, tm, LANES), F32), pltpu.VMEM((1, tm, LANES), F32)]
    return pl.pallas_call(
        functools.partial(_proj_a_kernel, dils=tuple(dils), tm=tm),
        grid=(t // tm,),
        in_specs=in_specs,
        out_specs=pl.BlockSpec((tm, d), lambda i: (i, 0)),
        out_shape=jax.ShapeDtypeStruct((t, d), F32),
        scratch_shapes=scratch,
        compiler_params=_cparams("arbitrary"),
        name="dilated_combine_proj",
    )(*outs, *lses, e, w, gain.reshape(1, d), h)


def _proj_c_kernel(oc_ref, os_ref, ow_ref, gr_ref, e_ref, w_ref, g_ref, h_ref, o_ref):
    sig = jax.nn.sigmoid(gr_ref[...])
    e = e_ref[...]
    gate = lambda i: _expand_heads(sig if i == 0 else pltpu.roll(sig, LANES - i * N_HEADS, 1), e)
    y = gate(0) * oc_ref[...]
    y = y + gate(1) * os_ref[...]
    y = y + gate(2) * ow_ref[...]
    _finish_proj(y, w_ref, g_ref, h_ref, o_ref)


def _proj_call(kernel, name, tm, row_inputs, const_inputs, w, gain, h):
    t, d = h.shape
    k = w.shape[0]
    row_specs = [pl.BlockSpec((tm, a.shape[1]), lambda i: (i, 0)) for a in row_inputs]
    const_specs = [pl.BlockSpec(a.shape, (lambda nd: (lambda i: (0,) * nd))(a.ndim)) for a in const_inputs]
    return pl.pallas_call(
        kernel,
        grid=(t // tm,),
        in_specs=row_specs + const_specs + [
            pl.BlockSpec((k, d), lambda i: (0, 0)),
            pl.BlockSpec((1, d), lambda i: (0, 0)),
            pl.BlockSpec((tm, d), lambda i: (i, 0))],
        out_specs=pl.BlockSpec((tm, d), lambda i: (i, 0)),
        out_shape=jax.ShapeDtypeStruct((t, d), F32),
        compiler_params=_cparams("parallel"),
        name=name,
    )(*row_inputs, *const_inputs, w, gain.reshape(1, d), h)


def _head_expand_matrix():
    e = np.zeros((LANES, HQ), np.float32)
    for term in range(3):
        for h in range(N_HEADS):
            e[term * N_HEADS + h, h * HEAD_DIM:(h + 1) * HEAD_DIM] = 1.0
    return e


def _banded_kernel(*refs, tq, n_prev, n_chunks, stack, with_sinks, with_lse):
    q_ref, kp_ref, kc_ref, vp_ref, vc_ref, bias_ref = refs[:6]
    pos = 6
    sink_ref = None
    if with_sinks:
        sink_ref = refs[pos]
        pos += 1
    o_ref = refs[pos]
    pos += 1
    lse_ref = None
    if with_lse:
        lse_ref = refs[pos]
        pos += 1
    kbuf, vbuf = refs[pos], refs[pos + 1]

    i = pl.program_id(1)
    hpc = N_HEADS // n_chunks
    kb = (n_prev + 1) * QBLK
    kbuf[0:tq] = kp_ref[0]
    kbuf[tq:2 * tq] = kc_ref[0]
    vbuf[0:tq] = vp_ref[0]
    vbuf[tq:2 * tq] = vc_ref[0]
    lane = lax.broadcasted_iota(jnp.int32, (QBLK, LANES), 1)
    head_row = lax.broadcasted_iota(jnp.int32, (stack * QBLK, 1), 0)

    for s in range(tq // QBLK):
        r0 = s * QBLK
        k0 = tq + r0 - n_prev * QBLK
        variants = bias_ref.shape[0] > 1
        if variants:
            first = jnp.where(i == 0, 1, 0)
        else:
            col = lax.broadcasted_iota(jnp.int32, (1, kb), 1) + k0
            kmask = jnp.where(jnp.logical_and(i == 0, col < tq), NEG_INF, 0.0).astype(F32)
        lse_acc = jnp.zeros((QBLK, LANES), F32)
        for h0 in range(0, N_HEADS, stack):
            heads = list(range(h0, h0 + stack))
            c = h0 // hpc
            qst = _stack_heads(lambda p: q_ref[0, r0:r0 + QBLK, p * LANES:(p + 1) * LANES], heads)
            kx = kbuf[k0:k0 + kb, c * LANES:(c + 1) * LANES]
            vx = vbuf[k0:k0 + kb, c * LANES:(c + 1) * LANES]
            sc = _dot_nt(qst, kx)
            if variants:
                sc = sc + bias_ref[first, h0:h0 + stack].reshape(stack * QBLK, kb)
            else:
                sc = sc + bias_ref[0, h0:h0 + stack].reshape(stack * QBLK, kb) + kmask
            m = jnp.max(sc, axis=-1, keepdims=True)
            if with_sinks:
                sk = sink_ref[h0 + stack - 1]
                for t in range(stack - 2, -1, -1):
                    sk = jnp.where(head_row < (t + 1) * QBLK, sink_ref[h0 + t], sk)
                m = jnp.maximum(m, sk)
            p = jnp.exp(sc - m)
            den = jnp.sum(p, axis=-1, keepdims=True)
            norm = den + jnp.exp(sk - m) if with_sinks else den
            o = jnp.dot(p.astype(BF16), vx, preferred_element_type=F32) * (1.0 / norm)
            for t2, blk in enumerate(_merge_pairs(o, stack)):
                pidx = h0 // 2 + t2
                o_ref[0, r0:r0 + QBLK, pidx * LANES:(pidx + 1) * LANES] = blk.astype(o_ref.dtype)
            if with_lse:
                lse = m + jnp.log(den)
                for t, h in enumerate(heads):
                    lse_acc = jnp.where(lane == h, lse[t * QBLK:(t + 1) * QBLK], lse_acc)
        if with_lse:
            lse_ref[0, r0:r0 + QBLK, :] = lse_acc


def banded_attention(q_arr, k_arr, v_arr, bias, *, n_rows, n_tiles, tq, n_prev, kw,
                     q_map, k_map, v_map, o_map, out_shape, lse_shape=None, sinks=None, stack=2, name):
    n_chunks = kw // LANES
    assert stack % 2 == 0 and (N_HEADS // n_chunks) % stack == 0
    kb = (n_prev + 1) * QBLK
    with_sinks = sinks is not None
    with_lse = lse_shape is not None

    def prev(fn):
        return lambda n, i: fn(n, jnp.maximum(i - 1, 0))

    in_specs = [pl.BlockSpec((1, tq, HQ), q_map),
                pl.BlockSpec((1, tq, kw), prev(k_map)),
                pl.BlockSpec((1, tq, kw), k_map),
                pl.BlockSpec((1, tq, kw), prev(v_map)),
                pl.BlockSpec((1, tq, kw), v_map),
                pl.BlockSpec(bias.shape, lambda n, i: (0, 0, 0, 0))]
    assert bias.shape[1:] == (N_HEADS, QBLK, kb) and (bias.shape[0] == 1 or tq == QBLK)
    args = [q_arr, k_arr, k_arr, v_arr, v_arr, bias]
    if with_sinks:
        in_specs.append(pl.BlockSpec(memory_space=pltpu.SMEM))
        args.append(sinks)
    out_specs = [pl.BlockSpec((1, tq, HQ), o_map)]
    out_shapes = [jax.ShapeDtypeStruct(out_shape, F32)]
    if with_lse:
        out_specs.append(pl.BlockSpec((1, tq, LANES), o_map))
        out_shapes.append(jax.ShapeDtypeStruct(lse_shape, F32))
    res = pl.pallas_call(
        functools.partial(_banded_kernel, tq=tq, n_prev=n_prev, n_chunks=n_chunks, stack=stack,
                          with_sinks=with_sinks, with_lse=with_lse),
        grid=(n_rows, n_tiles),
        in_specs=in_specs,
        out_specs=out_specs,
        out_shape=out_shapes,
        scratch_shapes=[pltpu.VMEM((2 * tq, kw), BF16), pltpu.VMEM((2 * tq, kw), BF16)],
        compiler_params=_cparams("parallel", "arbitrary"),
        name=name,
    )(*args)
    return res


def _dup_groups(x):
    g0, g1 = x[..., :HEAD_DIM], x[..., HEAD_DIM:]
    return jnp.concatenate([g0, g0, g1, g1], axis=-1)


def _compress_kernel(ch_ref, pos_ref, w1_ref, w2_ref, o_ref):
    ch = ch_ref[0, 0, 0]
    rows = ch.shape[0]
    posv = pos_ref[0]
    a = jnp.dot((ch + posv[0:1]).astype(BF16), w1_ref[0, 0], preferred_element_type=F32)
    b = jnp.dot((ch + posv[1:2]).astype(BF16), w1_ref[0, 1], preferred_element_type=F32)
    hid = a + pltpu.roll(b, rows - 1, 0)
    act = jax.nn.gelu(hid, approximate=True)
    o_ref[0, 0, 0] = jnp.dot(act.astype(BF16), w2_ref[0], preferred_element_type=F32)


def compress(chunks, pos, w1, w2):
    _, b, g, rows, width = chunks.shape
    hid = w1.shape[-1]
    return pl.pallas_call(
        _compress_kernel,
        grid=(2, b, g),
        in_specs=[pl.BlockSpec((1, 1, 1, rows, width), lambda i, bb, gg: (i, bb, gg, 0, 0)),
                  pl.BlockSpec((1, 2, width), lambda i, bb, gg: (i, 0, 0)),
                  pl.BlockSpec((1, 2, width, hid), lambda i, bb, gg: (i, 0, 0, 0)),
                  pl.BlockSpec((1, hid, HEAD_DIM), lambda i, bb, gg: (i, 0, 0))],
        out_specs=pl.BlockSpec((1, 1, 1, rows, HEAD_DIM), lambda i, bb, gg: (i, bb, gg, 0, 0)),
        out_shape=jax.ShapeDtypeStruct((2, b, g, rows, HEAD_DIM), F32),
        compiler_params=_cparams("parallel", "parallel", "parallel"),
        name="nsa_compress",
    )(chunks, pos, w1, w2)


def _cmp_attn_kernel(q_ref, kc_ref, vc_ref, ov_ref, place_ref, o_ref, sel_ref, *, n_sel_blocks, k_sel):
    qi = pl.program_id(1)
    ncr = kc_ref.shape[1]
    hpc = N_HEADS // C_KV_HEADS
    qpos = qi * QBLK + lax.broadcasted_iota(jnp.int32, (QBLK, 1), 0)
    cidx = lax.broadcasted_iota(jnp.int32, (1, ncr), 1)
    valid = (cidx * CMP_STRIDE + (CMP_BLOCK - 1)) <= qpos
    maskc = jnp.where(valid, 0.0, NEG_INF).astype(F32)
    anyv = (qpos >= CMP_BLOCK - 1).astype(F32)
    blk_id = lax.broadcasted_iota(jnp.int32, (HEAD_DIM, QBLK), 0)
    cur = (qi * QBLK + lax.broadcasted_iota(jnp.int32, (HEAD_DIM, QBLK), 1)) // SEL_BLOCK
    forced = jnp.logical_or(jnp.logical_or(blk_id == 0, blk_id == cur), blk_id == cur - 1)
    allowed = blk_id <= cur
    for g in range(C_KV_HEADS):
        heads = list(range(g * hpc, (g + 1) * hpc))
        qst = _stack_heads(lambda p: q_ref[0, :, p * LANES:(p + 1) * LANES], heads)
        sc = _dot_nt(qst, kc_ref[0, :, g * LANES:(g + 1) * LANES])
        sc3 = sc.reshape(hpc, QBLK, ncr) + maskc[None]
        m = jnp.max(sc3, axis=-1, keepdims=True)
        e = jnp.exp(sc3 - m)
        ssum = jnp.sum(e, axis=-1, keepdims=True)
        p = (e * (1.0 / ssum)) * anyv[None]
        o = jnp.dot(p.reshape(hpc * QBLK, ncr).astype(BF16), vc_ref[0, :, g * LANES:(g + 1) * LANES],
                    preferred_element_type=F32)
        for t2, blk in enumerate(_merge_pairs(o, hpc)):
            pidx = (g * hpc) // 2 + t2
            o_ref[0, :, pidx * LANES:(pidx + 1) * LANES] = blk
        hi, mid, lo = _split3(jnp.sum(p, axis=0))
        ovt = ov_ref[...]
        imp = (_dot_nt(ovt, hi) + _dot_nt(ovt, mid)) + _dot_nt(ovt, lo)
        score = jnp.where(forced, FORCE_SCORE, jnp.where(allowed, imp, NEG_INF))
        bits = pltpu.bitcast(score, jnp.int32)
        key = jnp.where(bits < 0, bits ^ jnp.int32(0x7FFFFFFF), bits)
        key_m1 = key - 1
        rank = jnp.zeros((HEAD_DIM, QBLK), jnp.int32)
        for i in range(n_sel_blocks):
            thr = jnp.where(blk_id > i, key_m1, key)
            rank = rank + jnp.where(key[i:i + 1, :] > thr, 1, 0)
        keep = jnp.logical_and(rank < k_sel, blk_id < n_sel_blocks)
        pen_t = jnp.where(keep, 0.0, -1.0).astype(BF16)
        pen = lax.dot_general(pen_t, place_ref[...], (((0,), (0,)), ((), ())), preferred_element_type=F32)
        sel_ref[0, :, g * LANES:(g + 1) * LANES] = pen.astype(sel_ref.dtype)


def cmp_attention(q, kcd, vcd, ov, n_sel_blocks, k_sel):
    b, s, _ = q.shape
    ncr = kcd.shape[1]
    place = np.zeros((HEAD_DIM, LANES), np.float32)
    place[np.arange(HEAD_DIM), HEAD_DIM + np.arange(HEAD_DIM)] = 1.0
    return pl.pallas_call(
        functools.partial(_cmp_attn_kernel, n_sel_blocks=n_sel_blocks, k_sel=k_sel),
        grid=(b, s // QBLK),
        in_specs=[pl.BlockSpec((1, QBLK, HQ), lambda bb, i: (bb, i, 0)),
                  pl.BlockSpec((1, ncr, 2 * LANES), lambda bb, i: (bb, 0, 0)),
                  pl.BlockSpec((1, ncr, 2 * LANES), lambda bb, i: (bb, 0, 0)),
                  pl.BlockSpec((HEAD_DIM, ncr), lambda bb, i: (0, 0)),
                  pl.BlockSpec((HEAD_DIM, LANES), lambda bb, i: (0, 0))],
        out_specs=[pl.BlockSpec((1, QBLK, HQ), lambda bb, i: (bb, i, 0)),
                   pl.BlockSpec((1, QBLK, 2 * LANES), lambda bb, i: (bb, i, 0))],
        out_shape=[jax.ShapeDtypeStruct((b, s, HQ), F32),
                   jax.ShapeDtypeStruct((b, s, 2 * LANES), BF16)],
        compiler_params=_cparams("parallel", "parallel"),
        name="nsa_cmp_attention",
    )(q, kcd, vcd, ov, jnp.asarray(place, BF16))


def _sel_attn_kernel(q_ref, pen_ref, k_ref, v_ref, bias_ref, o_ref, qst_ref, m_ref, acc_ref,
                     sa_ref, sb_ref, *, nbt):
    qi = pl.program_id(2)
    hpc = N_HEADS // C_KV_HEADS
    nsub = SEL_CHUNK // QBLK
    lane = lax.broadcasted_iota(jnp.int32, (QBLK, LANES), 1)
    lo = lane < HEAD_DIM
    pen = pen_ref[0].astype(F32)
    for t in range(hpc):
        q2 = q_ref[0, :, (t // 2) * LANES:(t // 2 + 1) * LANES].astype(F32)
        if t % 2:
            q2 = pltpu.roll(q2, HEAD_DIM, 1)
        qst_ref[t * QBLK:(t + 1) * QBLK, :] = jnp.where(lo, q2, pen).astype(BF16)
    m_ref[...] = jnp.full(m_ref.shape, NEG_INF, F32)
    acc_ref[...] = jnp.zeros(acc_ref.shape, F32)

    n_chunks = qi // nsub + 1

    def scores(c, s_ref):
        k0 = pl.multiple_of(c * SEL_CHUNK, SEL_CHUNK)
        s_ref[...] = _dot_nt(qst_ref[...], k_ref[0, pl.ds(k0, SEL_CHUNK), :])

    def accumulate(c, s_ref):
        k0 = pl.multiple_of(c * SEL_CHUNK, SEL_CHUNK)
        tiles = []
        for u in range(nsub):
            off = qi - nsub * c - u
            idx = jnp.where(off < 0, nbt, jnp.minimum(off, nbt - 1))
            tiles.append(s_ref[:, u * QBLK:(u + 1) * QBLK] + bias_ref[idx].reshape(hpc * QBLK, QBLK))
        m_old = m_ref[...]
        m_new = jnp.maximum(m_old, jnp.max(functools.reduce(jnp.maximum, tiles), axis=-1, keepdims=True))
        p = jnp.concatenate([jnp.exp(tl - m_new).astype(BF16) for tl in tiles], axis=1)
        pv = jnp.dot(p, v_ref[0, pl.ds(k0, SEL_CHUNK), :], preferred_element_type=F32)
        acc_ref[...] = jnp.exp(m_old - m_new) * acc_ref[...] + pv
        m_ref[...] = m_new

    scores(0, sa_ref)

    def body(cc, carry):
        c = 2 * cc
        scores(c + 1, sb_ref)
        accumulate(c, sa_ref)
        scores(jnp.minimum(c + 2, n_chunks - 1), sa_ref)
        accumulate(c + 1, sb_ref)
        return carry

    lax.fori_loop(0, n_chunks // 2, body, 0)

    @pl.when(n_chunks % 2 == 1)
    def _():
        accumulate(n_chunks - 1, sa_ref)

    acc = acc_ref[...]
    rolled = pltpu.roll(acc, HEAD_DIM, 1)
    for t2 in range(hpc // 2):
        ev = slice(2 * t2 * QBLK, (2 * t2 + 1) * QBLK)
        od = slice((2 * t2 + 1) * QBLK, (2 * t2 + 2) * QBLK)
        even = acc[ev] * (1.0 / rolled[ev])
        odd = rolled[od] * (1.0 / acc[od])
        o_ref[0, :, t2 * LANES:(t2 + 1) * LANES] = jnp.where(lo, even, odd)


def sel_attention(q, kx, vx, pen, bias_tiles):
    b, s, _ = q.shape
    hpc = N_HEADS // C_KV_HEADS
    gw = hpc * HEAD_DIM
    nbt = bias_tiles.shape[0] - 1
    assert s % SEL_CHUNK == 0
    return pl.pallas_call(
        functools.partial(_sel_attn_kernel, nbt=nbt),
        grid=(b, C_KV_HEADS, s // QBLK),
        in_specs=[pl.BlockSpec((1, QBLK, gw), lambda bb, g, i: (bb, i, g)),
                  pl.BlockSpec((1, QBLK, LANES), lambda bb, g, i: (bb, i, g)),
                  pl.BlockSpec((1, s, LANES), lambda bb, g, i: (bb, 0, g)),
                  pl.BlockSpec((1, s, LANES), lambda bb, g, i: (bb, 0, g)),
                  pl.BlockSpec((nbt + 1, hpc, QBLK, QBLK), lambda bb, g, i: (0, g, 0, 0))],
        out_specs=pl.BlockSpec((1, QBLK, gw), lambda bb, g, i: (bb, i, g)),
        out_shape=jax.ShapeDtypeStruct((b, s, HQ), F32),
        scratch_shapes=[pltpu.VMEM((hpc * QBLK, LANES), BF16),
                        pltpu.VMEM((hpc * QBLK, LANES), F32),
                        pltpu.VMEM((hpc * QBLK, LANES), F32),
                        pltpu.VMEM((hpc * QBLK, SEL_CHUNK), F32),
                        pltpu.VMEM((hpc * QBLK, SEL_CHUNK), F32)],
        compiler_params=_cparams("parallel", "parallel", "arbitrary"),
        name="nsa_sel_attention",
    )(q, pen, kx, vx, bias_tiles)


def _sel_bias_idx(s):
    nqt = s // QBLK
    far = -(-(int(np.argmax(_t5_bucket_np(np.arange(4 * MAX_DISTANCE)) == NUM_BUCKETS - 1)) + QBLK) // QBLK)
    nbt = min(nqt, far + 1)
    d0 = np.arange(nbt)[:, None, None] * QBLK
    dist = d0 + np.arange(QBLK)[None, :, None] - np.arange(QBLK)[None, None, :]
    idx = np.where(dist >= 0, _t5_bucket_np(dist), -1).astype(np.int32)
    return np.concatenate([idx, np.full((1, QBLK, QBLK), -1, np.int32)])


def _ffn_kernel(h_ref, g2_ref, wu_ref, cw_ref, cb_ref, wd_ref, g3_ref,
                o_ref, xn_ref, acc_ref, ua_ref, ub_ref, aa_ref, ab_ref, carry_ref, *, tiles_per_seq, nj):
    i = pl.program_id(0)
    tm, tn = xn_ref.shape[0], aa_ref.shape[1]
    cols = lambda c: slice(c * tn, (c + 1) * tn)
    xn_ref[...] = _rms(h_ref[...], g2_ref[...]).astype(xn_ref.dtype)
    acc_ref[...] = jnp.zeros(acc_ref.shape, F32)

    @pl.when(i % tiles_per_seq == 0)
    def _():
        carry_ref[...] = jnp.zeros(carry_ref.shape, F32)

    top = 16
    rowt = lax.broadcasted_iota(jnp.int32, (top, tn), 0)

    def up(jj, u_ref):
        xn = xn_ref[...]
        u_ref[0] = jnp.dot(xn, wu_ref[:, cols(jj)], preferred_element_type=F32)
        u_ref[1] = jnp.dot(xn, wu_ref[:, cols(nj + jj)], preferred_element_type=F32)

    def conv(u, s1, s2, c):
        cw = cw_ref[:, cols(c)]
        return ((cb_ref[:, cols(c)] + u * cw[2:3]) + s2 * cw[0:1]) + s1 * cw[1:2]

    def conv_body(u, c):
        return conv(u, pltpu.roll(u, 1, 0), pltpu.roll(u, 2, 0), c)

    def conv_top(u_ref, c):
        prev = carry_ref[c]
        p1, p2 = prev[7:8], prev[6:7]
        u = u_ref[0:top]
        s1 = jnp.where(rowt == 0, p1, pltpu.roll(u, 1, 0))
        s2 = jnp.where(rowt == 0, p2, jnp.where(rowt == 1, p1, pltpu.roll(u, 2, 0)))
        carry_ref[c] = u_ref[tm - 8:tm]
        return conv(u, s1, s2, c)

    def gated(cg, cv):
        return (jax.nn.gelu(cg, approximate=True) * cv).astype(BF16)

    def act(j, u_ref, a_ref):
        a_ref[...] = gated(conv_body(u_ref[0], j), conv_body(u_ref[1], nj + j))
        a_ref[0:top] = gated(conv_top(u_ref.at[0], j), conv_top(u_ref.at[1], nj + j))

    def down(j, a_ref):
        acc_ref[...] += jnp.dot(a_ref[...], wd_ref[j * tn:(j + 1) * tn, :], preferred_element_type=F32)

    up(0, ua_ref)
    up(1, ub_ref)
    act(0, ua_ref, aa_ref)

    for jj in range((nj - 3) // 2):
        j = 2 * jj + 1
        up(j + 1, ua_ref)
        act(j, ub_ref, ab_ref)
        down(j - 1, aa_ref)
        up(j + 2, ub_ref)
        act(j + 1, ua_ref, aa_ref)
        down(j, ab_ref)
    up(nj - 1, ua_ref)
    act(nj - 2, ub_ref, ab_ref)
    down(nj - 3, aa_ref)
    act(nj - 1, ua_ref, aa_ref)
    down(nj - 2, ab_ref)
    down(nj - 1, aa_ref)
    o_ref[...] = h_ref[...] + _rms(acc_ref[...], g3_ref[...])


def conv_ffn(h, g2, w_up, conv_w, conv_b, w_down, g3, seq, tm, tn):
    t, d = h.shape
    dff = w_down.shape[0]
    nj = dff // tn
    assert nj * tn == dff and nj % 2 == 1 and nj >= 3
    wu, cw, cb, wd = w_up, conv_w, conv_b.reshape(1, -1), w_down
    resident = lambda a: pl.BlockSpec(a.shape, (lambda nd: (lambda i: (0,) * nd))(a.ndim))
    return pl.pallas_call(
        functools.partial(_ffn_kernel, tiles_per_seq=seq // tm, nj=nj),
        grid=(t // tm,),
        in_specs=[pl.BlockSpec((tm, d), lambda i: (i, 0)),
                  pl.BlockSpec((1, d), lambda i: (0, 0)),
                  resident(wu), resident(cw), resident(cb), resident(wd),
                  pl.BlockSpec((1, d), lambda i: (0, 0))],
        out_specs=pl.BlockSpec((tm, d), lambda i: (i, 0)),
        out_shape=jax.ShapeDtypeStruct((t, d), F32),
        scratch_shapes=[pltpu.VMEM((tm, d), BF16), pltpu.VMEM((tm, d), F32),
                        pltpu.VMEM((2, tm, tn), F32), pltpu.VMEM((2, tm, tn), F32),
                        pltpu.VMEM((tm, tn), BF16), pltpu.VMEM((tm, tn), BF16),
                        pltpu.VMEM((2 * nj, 8, tn), F32)],
        compiler_params=_cparams("arbitrary"),
        name="conv_ffn",
    )(h, g2.reshape(1, d), wu, cw, cb, wd, g3.reshape(1, d))


def _row_tile(t):
    for tm in (1024, 512, 256, 128):
        if t % tm == 0:
            return tm
    raise ValueError(f"token count {t} is not a multiple of 128")


def mixer_a(h, gains, w_in, w_o, rel_table, bsz, seq):
    t, d = h.shape
    tm = _row_tile(t)
    n_dil = len(DIL_CONFIGS)
    a_in = w_in.shape[1]
    col_scale = np.ones((a_in,), np.float32).reshape(n_dil, 3, HQ)
    col_scale[:, 0] = ATTN_SCALE
    w = (w_in * col_scale.reshape(1, a_in)).astype(BF16)
    idx = np.concatenate([_banded_idx(1, window // dil, dil) for window, dil in DIL_CONFIGS])
    bias = bias_expand(rel_table, idx)
    outs, lses = [], []
    dils = [dil for _, dil in DIL_CONFIGS]
    blocks_per_seq = seq // QBLK
    xns = norm_permute(h, gains[0], dils, max([tm] + [QBLK * dil for dil in dils]))
    for gi, (window, dil) in enumerate(DIL_CONFIGS):
        assert window // dil <= QBLK and seq % (QBLK * dil) == 0
        qkv = matmul_resident(xns[gi], w[:, gi * 3 * HQ:(gi + 1) * 3 * HQ], BF16, tm)
        qkv = qkv.reshape(1, t, 3 * HQ)

        def rmap(part, dil=dil):
            return lambda n, i: (0, (n // dil) * blocks_per_seq + i * dil + n % dil, part)

        o, lse = banded_attention(
            qkv, qkv, qkv, bias[2 * gi:2 * gi + 2], n_rows=bsz * dil, n_tiles=seq // (QBLK * dil), tq=QBLK, n_prev=1, kw=HQ,
            q_map=rmap(0), k_map=rmap(1), v_map=rmap(2), o_map=rmap(0),
            out_shape=(1, t, HQ), lse_shape=(1, t, LANES), name=f"dilated_attention_{dil}")
        outs.append(o.reshape(t, HQ))
        lses.append(lse.reshape(t, LANES))
    e = jnp.asarray(_head_expand_matrix(), BF16)
    return combine_proj_a(outs, lses, dils, e, w_o.astype(BF16), gains[1], h, min(tm, 512))


def mixer_b(h, gains, w_in, sinks, w_o, rel_table, bsz, seq):
    t, d = h.shape
    tm = _row_tile(t)
    hk = B_KV_HEADS * HEAD_DIM
    n_in = w_in.shape[1]
    col_scale = np.ones((n_in,), np.float32)
    col_scale[:HQ] = ATTN_SCALE
    w = (w_in * col_scale[None]).astype(BF16)
    qkv = norm_matmul(h, gains[0], w, BF16, tm, n_in // 2 if (n_in // 2) % LANES == 0 else n_in)
    q = qkv[:, :HQ].reshape(bsz, seq, HQ)
    kd = _dup_groups(qkv[:, HQ:HQ + hk].reshape(bsz, seq, hk))
    vd = _dup_groups(qkv[:, HQ + hk:].reshape(bsz, seq, hk))
    bias = bias_expand(rel_table, _banded_idx(1, B_WINDOW - 1, 1))
    sink_rows = sinks.astype(F32)
    ident = lambda n, i: (n, i, 0)
    (o,) = banded_attention(q, kd, vd, bias, n_rows=bsz, n_tiles=seq // QBLK, tq=QBLK, n_prev=1,
                            kw=2 * LANES, q_map=ident, k_map=ident, v_map=ident, o_map=ident,
                            out_shape=(bsz, seq, HQ), sinks=sink_rows, stack=8, name="sink_window_attention")
    return _proj_call(_proj_kernel, "sink_proj", tm, [o.reshape(t, HQ)], [], w_o.astype(BF16), gains[1], h)


def mixer_c(h, gains, w_in, cmp_pos, cmp_w1, cmp_w2, w_o, rel_table, bsz, seq):
    t, d = h.shape
    tm = _row_tile(t)
    g = C_KV_HEADS
    hk = g * HEAD_DIM
    rest = w_in.shape[1] - HQ
    rest_pad = -(-rest // LANES) * LANES
    wq = (w_in[:, :HQ] * ATTN_SCALE).astype(BF16)
    wr = jnp.pad(w_in[:, HQ:], ((0, 0), (0, rest_pad - rest))).astype(BF16)
    q = norm_matmul(h, gains[0], wq, BF16, tm, HQ).reshape(bsz, seq, HQ)
    r = norm_matmul(h, gains[0], wr, F32, tm, rest_pad)
    kv = [r[:, i * hk:(i + 1) * hk].reshape(bsz, seq, hk) for i in range(6)]
    gates_raw = r[:, 6 * hk:6 * hk + LANES]

    ncr = seq // CMP_STRIDE
    half = CMP_BLOCK // 2
    assert half == CMP_STRIDE
    chunks = jnp.stack([kv[0], kv[1]]).reshape(2, bsz, ncr, half, g, HEAD_DIM)
    chunks = chunks.transpose(0, 1, 4, 2, 3, 5).reshape(2, bsz, g, ncr, half * HEAD_DIM)
    pos = cmp_pos.reshape(2, 2, half * HEAD_DIM)
    w1 = cmp_w1.reshape(2, 2, half * HEAD_DIM, -1).astype(BF16)
    cmp = compress(chunks, pos, w1, cmp_w2.astype(BF16))
    cmp = cmp.transpose(0, 1, 3, 2, 4).reshape(2, bsz, ncr, hk).astype(BF16)
    kcd, vcd = _dup_groups(cmp[0]), _dup_groups(cmp[1])

    ns = seq // SEL_BLOCK
    assert ns <= HEAD_DIM
    k_sel = min(N_SELECT, ns)
    starts = np.arange(ncr) * CMP_STRIDE
    blk = np.arange(HEAD_DIM)
    ov = ((starts[None, :] < (blk[:, None] + 1) * SEL_BLOCK)
          & (starts[None, :] + CMP_BLOCK > blk[:, None] * SEL_BLOCK)
          & (blk[:, None] < ns) & (starts[None, :] + CMP_BLOCK <= seq))
    o_c, pen = cmp_attention(q, kcd, vcd, jnp.asarray(ov.astype(np.float32), BF16), ns, k_sel)

    sel_bias = bias_expand(rel_table, _sel_bias_idx(seq))
    key_blk = np.arange(seq)[:, None] // SEL_BLOCK == np.arange(HEAD_DIM)[None, :]
    onehot = jnp.broadcast_to(jnp.asarray(np.where(key_blk, -NEG_INF, 0.0), BF16), (bsz, seq, HEAD_DIM))
    ones = jnp.ones((bsz, seq, HEAD_DIM), BF16)
    ks, vs = kv[2].astype(BF16), kv[3].astype(BF16)
    kx = jnp.concatenate([ks[..., :HEAD_DIM], onehot, ks[..., HEAD_DIM:], onehot], axis=-1)
    vx = jnp.concatenate([vs[..., :HEAD_DIM], ones, vs[..., HEAD_DIM:], ones], axis=-1)
    o_s = sel_attention(q, kx, vx, pen, sel_bias)

    n_prev = -(-(C_WINDOW - 1) // QBLK)
    tqw = n_prev * QBLK
    wbias = bias_expand(rel_table, _banded_idx(n_prev, C_WINDOW - 1, 1))
    ident = lambda n, i: (n, i, 0)
    (o_w,) = banded_attention(q, _dup_groups(kv[4].astype(BF16)), _dup_groups(kv[5].astype(BF16)), wbias,
                              n_rows=bsz, n_tiles=seq // tqw, tq=tqw, n_prev=n_prev, kw=2 * LANES,
                              q_map=ident, k_map=ident, v_map=ident, o_map=ident,
                              out_shape=(bsz, seq, HQ), stack=8, name="nsa_window_attention")

    e = jnp.asarray(_head_expand_matrix(), BF16)
    return _proj_call(_proj_c_kernel, "nsa_gate_proj", min(tm, 512),
                      [o_c.reshape(t, HQ), o_s.reshape(t, HQ), o_w.reshape(t, HQ), gates_raw], [e],
                      w_o.astype(BF16), gains[1], h)


def kernel(x, rel_table, norm_gains, a_w_in, a_w_o, b_w_in, b_sinks, b_w_o, c_w_in, c_cmp_pos, c_cmp_w1,
           c_cmp_w2, c_w_o, ffn_w_up, ffn_conv_w, ffn_conv_b, ffn_w_down):
    bsz, seq, d = x.shape
    depth = norm_gains.shape[0]
    h = x.reshape(bsz * seq, d)
    tm = _row_tile(seq)
    for i in range(depth):
        kind, j = i % 3, i // 3
        g = norm_gains[i]
        if kind == 0:
            h = mixer_a(h, g, a_w_in[j], a_w_o[j], rel_table, bsz, seq)
        elif kind == 1:
            h = mixer_b(h, g, b_w_in[j], b_sinks[j], b_w_o[j], rel_table, bsz, seq)
        else:
            h = mixer_c(h, g, c_w_in[j], c_cmp_pos[j], c_cmp_w1[j], c_cmp_w2[j], c_w_o[j], rel_table, bsz, seq)
        h = conv_ffn(h, g[2], ffn_w_up[i].astype(BF16), ffn_conv_w[i], ffn_conv_b[i],
                     ffn_w_down[i].astype(BF16), g[3], seq, min(tm, 256), 256)
    return h.reshape(bsz, seq, d)
```

```python
import functools
import math

import numpy as np
import jax
import jax.numpy as jnp
from jax import lax
from jax.experimental import pallas as pl
from jax.experimental.pallas import tpu as pltpu

F32 = jnp.float32
BF16 = jnp.bfloat16

N_HEADS = 16
HEAD_DIM = 64
HQ = N_HEADS * HEAD_DIM
LANES = 128
ATTN_SCALE = HEAD_DIM ** -0.5
NUM_BUCKETS = 32
MAX_DISTANCE = 2048
RMS_EPS = 1e-6
NEG_INF = -1e30
FORCE_SCORE = 1e9
DIL_CONFIGS = ((128, 1), (512, 4), (2048, 16))
B_KV_HEADS = 2
B_WINDOW = 128
C_KV_HEADS = 2
CMP_BLOCK = 32
CMP_STRIDE = 16
SEL_BLOCK = 64
N_SELECT = 16
C_WINDOW = 512
CONV_WIDTH = 3
QBLK = 128
SEL_CHUNK = 512
SEL_QTILES = 2
FFN_ROWS = 512
FFN_SUBTILES = 2
FFN_LAG = 0
VMEM_LIMIT = 56 * 1024 * 1024


def _cparams(*sem):
    return pltpu.CompilerParams(dimension_semantics=sem, vmem_limit_bytes=VMEM_LIMIT)


def _t5_bucket_np(dist):
    max_exact = NUM_BUCKETS // 2
    d = np.maximum(dist, 0)
    df = np.maximum(d, 1).astype(np.float64)
    large = max_exact + np.floor(np.log(df / max_exact) / math.log(MAX_DISTANCE / max_exact)
                                 * (NUM_BUCKETS - max_exact) + 1e-9).astype(np.int64)
    large = np.minimum(large, NUM_BUCKETS - 1)
    return np.where(d < max_exact, d, large).astype(np.int32)


def _rms(x, g):
    ms = jnp.mean(x * x, axis=-1, keepdims=True)
    return (x * lax.rsqrt(ms + RMS_EPS)) * g


def _split3(w):
    hi = w.astype(BF16)
    r1 = w - hi.astype(F32)
    mid = r1.astype(BF16)
    lo = (r1 - mid.astype(F32)).astype(BF16)
    return hi, mid, lo


def _expand_heads(w, e3):
    lane = lax.broadcasted_iota(jnp.int32, w.shape, 1)
    r1 = w - w.astype(BF16).astype(F32)
    r2 = r1 - r1.astype(BF16).astype(F32)
    packed = jnp.where(lane < N_HEADS, w,
                       jnp.where(lane < 2 * N_HEADS, pltpu.roll(r1, N_HEADS, 1), pltpu.roll(r2, 2 * N_HEADS, 1)))
    return jnp.dot(packed.astype(BF16), e3, preferred_element_type=F32)


def _dot_nt(a, b):
    return lax.dot_general(a, b, (((1,), (1,)), ((), ())), preferred_element_type=F32)


def _stack_heads(q_tile, heads):
    lane = lax.broadcasted_iota(jnp.int32, (QBLK, LANES), 1)
    lo = lane < HEAD_DIM
    pieces = []
    for h in heads:
        q2 = q_tile(h // 2)
        keep = lo if h % 2 == 0 else jnp.logical_not(lo)
        pieces.append(jnp.where(keep, q2, jnp.zeros_like(q2)))
    return jnp.concatenate(pieces, axis=0)


def _merge_pairs(o, n_heads):
    lane = lax.broadcasted_iota(jnp.int32, (QBLK, LANES), 1)
    lo = lane < HEAD_DIM
    out = []
    for t in range(0, n_heads, 2):
        out.append(jnp.where(lo, o[t * QBLK:(t + 1) * QBLK], o[(t + 1) * QBLK:(t + 2) * QBLK]))
    return out


def _bias_kernel(tab_ref, rng_ref, idx_ref, o_ref):
    n = pl.program_id(0)
    idx = idx_ref[0]
    o_ref[...] = jnp.full(o_ref.shape, NEG_INF, F32)

    def body(b, carry):
        hit = idx == b
        for h in range(N_HEADS):
            o_ref[0, h] = jnp.where(hit, tab_ref[b, h], o_ref[0, h])
        return carry

    lax.fori_loop(rng_ref[n, 0], rng_ref[n, 1] + 1, body, 0)


def bias_expand(table, idx):
    n, r, c = idx.shape
    flat = idx.reshape(n, -1)
    lo = np.where(flat >= 0, flat, NUM_BUCKETS).min(axis=1)
    rng = np.stack([lo, flat.max(axis=1)], axis=1).astype(np.int32)
    return pl.pallas_call(
        _bias_kernel,
        grid=(n,),
        in_specs=[pl.BlockSpec(memory_space=pltpu.SMEM),
                  pl.BlockSpec(memory_space=pltpu.SMEM),
                  pl.BlockSpec((1, r, c), lambda i: (i, 0, 0))],
        out_specs=pl.BlockSpec((1, N_HEADS, r, c), lambda i: (i, 0, 0, 0)),
        out_shape=jax.ShapeDtypeStruct((n, N_HEADS, r, c), F32),
        compiler_params=_cparams("parallel"),
        name="bias_expand",
    )(table, jnp.asarray(rng), jnp.asarray(idx))


def _banded_idx(n_prev, max_dist, stride):
    kb = (n_prev + 1) * QBLK
    dist = np.arange(QBLK)[:, None] + n_prev * QBLK - np.arange(kb)[None, :]
    valid = (dist >= 0) & (dist <= max_dist)
    idx = np.where(valid, _t5_bucket_np(dist * stride), -1).astype(np.int32)
    if n_prev > 1:
        return idx[None]
    first = np.where(np.arange(kb)[None, :] < n_prev * QBLK, -1, idx).astype(np.int32)
    return np.stack([idx, first])


def _norm_mm_kernel(x_ref, g_ref, w_ref, o_ref, xn_ref):
    @pl.when(pl.program_id(1) == 0)
    def _():
        xn_ref[...] = _rms(x_ref[...], g_ref[...]).astype(xn_ref.dtype)

    o_ref[...] = jnp.dot(xn_ref[...], w_ref[...], preferred_element_type=F32).astype(o_ref.dtype)


def norm_matmul(x, gain, w, out_dtype, tm, tn):
    t, d = x.shape
    n = w.shape[1]
    scratch = [pltpu.VMEM((tm, d), BF16)]
    return pl.pallas_call(
        _norm_mm_kernel,
        grid=(t // tm, n // tn),
        in_specs=[pl.BlockSpec((tm, d), lambda i, j: (i, 0)),
                  pl.BlockSpec((1, d), lambda i, j: (0, 0)),
                  pl.BlockSpec((d, tn), lambda i, j: (0, j))],
        out_specs=pl.BlockSpec((tm, tn), lambda i, j: (i, j)),
        out_shape=jax.ShapeDtypeStruct((t, n), out_dtype),
        scratch_shapes=scratch,
        compiler_params=_cparams("parallel", "arbitrary"),
        name="norm_matmul",
    )(x, gain.reshape(1, d), w)


def _norm_perm_kernel(x_ref, g_ref, *refs, dils):
    o_refs, xs_ref = refs[:len(dils)], refs[len(dils)]
    xn = _rms(x_ref[...], g_ref[...])
    nc = xs_ref.shape[0]
    for c in range(nc):
        xs_ref[c] = xn[:, c * LANES:(c + 1) * LANES]
    for o_ref, dil in zip(o_refs, dils):
        if dil == 1:
            o_ref[...] = xn.astype(o_ref.dtype)
            continue
        span = QBLK * dil
        for u in range(x_ref.shape[0] // span):
            for r in range(dil):
                rows = jnp.concatenate(
                    [xs_ref[c, pl.ds(u * span + r, QBLK, stride=dil), :] for c in range(nc)], axis=1)
                dst = u * span + r * QBLK
                o_ref[dst:dst + QBLK, :] = rows.astype(o_ref.dtype)


def norm_permute(x, gain, dils, tm):
    t, d = x.shape
    assert all(tm % (QBLK * dil) == 0 for dil in dils)
    return pl.pallas_call(
        functools.partial(_norm_perm_kernel, dils=tuple(dils)),
        grid=(t // tm,),
        in_specs=[pl.BlockSpec((tm, d), lambda i: (i, 0)),
                  pl.BlockSpec((1, d), lambda i: (0, 0))],
        out_specs=[pl.BlockSpec((tm, d), lambda i: (i, 0)) for _ in dils],
        out_shape=[jax.ShapeDtypeStruct((t, d), BF16) for _ in dils],
        scratch_shapes=[pltpu.VMEM((d // LANES, tm, LANES), F32)],
        compiler_params=_cparams("parallel"),
        name="norm_permute",
    )(x, gain.reshape(1, d))


def _mm_kernel(x_ref, w_ref, o_ref):
    o_ref[...] = jnp.dot(x_ref[...], w_ref[...], preferred_element_type=F32).astype(o_ref.dtype)


def matmul_resident(x, w, out_dtype, tm):
    t, k = x.shape
    n = w.shape[1]
    return pl.pallas_call(
        _mm_kernel,
        grid=(t // tm,),
        in_specs=[pl.BlockSpec((tm, k), lambda i: (i, 0)),
                  pl.BlockSpec((k, n), lambda i: (0, 0))],
        out_specs=pl.BlockSpec((tm, n), lambda i: (i, 0)),
        out_shape=jax.ShapeDtypeStruct((t, n), out_dtype),
        compiler_params=_cparams("parallel"),
        name="matmul_resident",
    )(x, w)


def _finish_proj(y, w_ref, g_ref, h_ref, o_ref):
    z = jnp.dot(y.astype(BF16), w_ref[...], preferred_element_type=F32)
    o_ref[...] = h_ref[...] + _rms(z, g_ref[...])


def _proj_kernel(y_ref, w_ref, g_ref, h_ref, o_ref):
    _finish_proj(y_ref[...], w_ref, g_ref, h_ref, o_ref)


def _unpermute(src_ref, dst_ref, dil, tm, i):
    span = QBLK * dil
    nc = dst_ref.shape[0]
    if span <= tm:
        for u in range(tm // span):
            for r in range(dil):
                lo = u * span + r * QBLK
                for c in range(nc):
                    dst_ref[c, pl.ds(u * span + r, QBLK, stride=dil), :] = src_ref[lo:lo + QBLK,
                                                                                   c * LANES:(c + 1) * LANES]
    else:
        per = tm // dil
        off = (i % (span // tm)) * per
        for r in range(dil):
            lo = pl.multiple_of(r * QBLK + off, 8)
            for c in range(nc):
                dst_ref[c, pl.ds(r, per, stride=dil), :] = src_ref[pl.ds(lo, per), c * LANES:(c + 1) * LANES]
    return jnp.concatenate([dst_ref[c] for c in range(nc)], axis=1) if nc > 1 else dst_ref[0]


def _proj_a_kernel(*refs, dils, tm):
    n = len(dils)
    o_refs, l_refs = refs[:n], refs[n:2 * n]
    e_ref, w_ref, g_ref, h_ref, out_ref = refs[2 * n:2 * n + 5]
    scratch = refs[2 * n + 5:]
    i = pl.program_id(0)
    outs, lses = [], []
    si = 0
    for gi, dil in enumerate(dils):
        if dil == 1:
            outs.append(o_refs[gi][...])
            lses.append(l_refs[gi][...])
        else:
            outs.append(_unpermute(o_refs[gi], scratch[si], dil, tm, i))
            lses.append(_unpermute(l_refs[gi], scratch[si + 1], dil, tm, i))
            si += 2
    mx = functools.reduce(jnp.maximum, lses)
    es = [jnp.exp(l - mx) for l in lses]
    inv = 1.0 / functools.reduce(lambda a, b: a + b, es)
    e = e_ref[...]
    y = outs[0] * _expand_heads(es[0] * inv, e)
    for gi in range(1, n):
        y = y + outs[gi] * _expand_heads(es[gi] * inv, e)
    _finish_proj(y, w_ref, g_ref, h_ref, out_ref)


def combine_proj_a(outs, lses, dils, e, w, gain, h, tm):
    t, d = h.shape

    def row_spec(width, dil):
        rows = max(tm, QBLK * dil)
        return pl.BlockSpec((rows, width), lambda i, q=rows // tm: (i // q, 0))

    in_specs = [row_spec(HQ, dil) for dil in dils] + [row_spec(LANES, dil) for dil in dils] + [
        pl.BlockSpec(e.shape, lambda i: (0, 0)),
        pl.BlockSpec(w.shape, lambda i: (0, 0)),
        pl.BlockSpec((1, d), lambda i: (0, 0)),
        pl.BlockSpec((tm, d), lambda i: (i, 0))]
    scratch = []
    for dil in dils:
        if dil > 1:
            scratch += [pltpu.VMEM((HQ // LANES, tm, LANES), F32), pltpu.VMEM((1, tm, LANES), F32)]
    return pl.pallas_call(
        functools.partial(_proj_a_kernel, dils=tuple(dils), tm=tm),
        grid=(t // tm,),
        in_specs=in_specs,
        out_specs=pl.BlockSpec((tm, d), lambda i: (i, 0)),
        out_shape=jax.ShapeDtypeStruct((t, d), F32),
        scratch_shapes=scratch,
        compiler_params=_cparams("arbitrary"),
        name="dilated_combine_proj",
    )(*outs, *lses, e, w, gain.reshape(1, d), h)


def _proj_c_kernel(oc_ref, os_ref, ow_ref, gr_ref, e_ref, w_ref, g_ref, h_ref, o_ref):
    sig = jax.nn.sigmoid(gr_ref[...])
    e = e_ref[...]
    gate = lambda i: _expand_heads(sig if i == 0 else pltpu.roll(sig, LANES - i * N_HEADS, 1), e)
    y = gate(0) * oc_ref[...]
    y = y + gate(1) * os_ref[...]
    y = y + gate(2) * ow_ref[...]
    _finish_proj(y, w_ref, g_ref, h_ref, o_ref)


def _proj_call(kernel, name, tm, row_inputs, const_inputs, w, gain, h):
    t, d = h.shape
    k = w.shape[0]
    row_specs = [pl.BlockSpec((tm, a.shape[1]), lambda i: (i, 0)) for a in row_inputs]
    const_specs = [pl.BlockSpec(a.shape, (lambda nd: (lambda i: (0,) * nd))(a.ndim)) for a in const_inputs]
    return pl.pallas_call(
        kernel,
        grid=(t // tm,),
        in_specs=row_specs + const_specs + [
            pl.BlockSpec((k, d), lambda i: (0, 0)),
            pl.BlockSpec((1, d), lambda i: (0, 0)),
            pl.BlockSpec((tm, d), lambda i: (i, 0))],
        out_specs=pl.BlockSpec((tm, d), lambda i: (i, 0)),
        out_shape=jax.ShapeDtypeStruct((t, d), F32),
        compiler_params=_cparams("parallel"),
        name=name,
    )(*row_inputs, *const_inputs, w, gain.reshape(1, d), h)


def _head_expand_matrix():
    e = np.zeros((LANES, HQ), np.float32)
    for term in range(3):
        for h in range(N_HEADS):
            e[term * N_HEADS + h, h * HEAD_DIM:(h + 1) * HEAD_DIM] = 1.0
    return e


def _banded_kernel(*refs, tq, n_prev, n_chunks, stack, with_sinks, with_lse):
    q_ref, kp_ref, kc_ref, vp_ref, vc_ref, bias_ref = refs[:6]
    pos = 6
    sink_ref = None
    if with_sinks:
        sink_ref = refs[pos]
        pos += 1
    o_ref = refs[pos]
    pos += 1
    lse_ref = None
    if with_lse:
        lse_ref = refs[pos]
        pos += 1
    kbuf, vbuf = refs[pos], refs[pos + 1]

    i = pl.program_id(1)
    hpc = N_HEADS // n_chunks
    kb = (n_prev + 1) * QBLK
    kbuf[0:tq] = kp_ref[0]
    kbuf[tq:2 * tq] = kc_ref[0]
    vbuf[0:tq] = vp_ref[0]
    vbuf[tq:2 * tq] = vc_ref[0]
    lane = lax.broadcasted_iota(jnp.int32, (QBLK, LANES), 1)
    head_row = lax.broadcasted_iota(jnp.int32, (stack * QBLK, 1), 0)

    for s in range(tq // QBLK):
        r0 = s * QBLK
        k0 = tq + r0 - n_prev * QBLK
        variants = bias_ref.shape[0] > 1
        if variants:
            first = jnp.where(i == 0, 1, 0)
        else:
            col = lax.broadcasted_iota(jnp.int32, (1, kb), 1) + k0
            kmask = jnp.where(jnp.logical_and(i == 0, col < tq), NEG_INF, 0.0).astype(F32)
        lse_acc = jnp.zeros((QBLK, LANES), F32)
        for h0 in range(0, N_HEADS, stack):
            heads = list(range(h0, h0 + stack))
            c = h0 // hpc
            qst = _stack_heads(lambda p: q_ref[0, r0:r0 + QBLK, p * LANES:(p + 1) * LANES], heads)
            kx = kbuf[k0:k0 + kb, c * LANES:(c + 1) * LANES]
            vx = vbuf[k0:k0 + kb, c * LANES:(c + 1) * LANES]
            sc = _dot_nt(qst, kx)
            if variants:
                sc = sc + bias_ref[first, h0:h0 + stack].reshape(stack * QBLK, kb)
            else:
                sc = sc + bias_ref[0, h0:h0 + stack].reshape(stack * QBLK, kb) + kmask
            m = jnp.max(sc, axis=-1, keepdims=True)
            if with_sinks:
                sk = sink_ref[h0 + stack - 1]
                for t in range(stack - 2, -1, -1):
                    sk = jnp.where(head_row < (t + 1) * QBLK, sink_ref[h0 + t], sk)
                m = jnp.maximum(m, sk)
            p = jnp.exp(sc - m)
            den = jnp.sum(p, axis=-1, keepdims=True)
            norm = den + jnp.exp(sk - m) if with_sinks else den
            o = jnp.dot(p.astype(BF16), vx, preferred_element_type=F32) * (1.0 / norm)
            for t2, blk in enumerate(_merge_pairs(o, stack)):
                pidx = h0 // 2 + t2
                o_ref[0, r0:r0 + QBLK, pidx * LANES:(pidx + 1) * LANES] = blk.astype(o_ref.dtype)
            if with_lse:
                lse = m + jnp.log(den)
                for t, h in enumerate(heads):
                    lse_acc = jnp.where(lane == h, lse[t * QBLK:(t + 1) * QBLK], lse_acc)
        if with_lse:
            lse_ref[0, r0:r0 + QBLK, :] = lse_acc


def banded_attention(q_arr, k_arr, v_arr, bias, *, n_rows, n_tiles, tq, n_prev, kw,
                     q_map, k_map, v_map, o_map, out_shape, lse_shape=None, sinks=None, stack=2, name):
    n_chunks = kw // LANES
    assert stack % 2 == 0 and (N_HEADS // n_chunks) % stack == 0
    kb = (n_prev + 1) * QBLK
    with_sinks = sinks is not None
    with_lse = lse_shape is not None

    def prev(fn):
        return lambda n, i: fn(n, jnp.maximum(i - 1, 0))

    in_specs = [pl.BlockSpec((1, tq, HQ), q_map),
                pl.BlockSpec((1, tq, kw), prev(k_map)),
                pl.BlockSpec((1, tq, kw), k_map),
                pl.BlockSpec((1, tq, kw), prev(v_map)),
                pl.BlockSpec((1, tq, kw), v_map),
                pl.BlockSpec(bias.shape, lambda n, i: (0, 0, 0, 0))]
    assert bias.shape[1:] == (N_HEADS, QBLK, kb) and (bias.shape[0] == 1 or tq == QBLK)
    args = [q_arr, k_arr, k_arr, v_arr, v_arr, bias]
    if with_sinks:
        in_specs.append(pl.BlockSpec(memory_space=pltpu.SMEM))
        args.append(sinks)
    out_specs = [pl.BlockSpec((1, tq, HQ), o_map)]
    out_shapes = [jax.ShapeDtypeStruct(out_shape, F32)]
    if with_lse:
        out_specs.append(pl.BlockSpec((1, tq, LANES), o_map))
        out_shapes.append(jax.ShapeDtypeStruct(lse_shape, F32))
    res = pl.pallas_call(
        functools.partial(_banded_kernel, tq=tq, n_prev=n_prev, n_chunks=n_chunks, stack=stack,
                          with_sinks=with_sinks, with_lse=with_lse),
        grid=(n_rows, n_tiles),
        in_specs=in_specs,
        out_specs=out_specs,
        out_shape=out_shapes,
        scratch_shapes=[pltpu.VMEM((2 * tq, kw), BF16), pltpu.VMEM((2 * tq, kw), BF16)],
        compiler_params=_cparams("parallel", "arbitrary"),
        name=name,
    )(*args)
    return res


def _dup_groups(x):
    g0, g1 = x[..., :HEAD_DIM], x[..., HEAD_DIM:]
    return jnp.concatenate([g0, g0, g1, g1], axis=-1)


def _compress_kernel(ch_ref, pos_ref, w1_ref, w2_ref, o_ref):
    ch = ch_ref[0, 0, 0]
    rows = ch.shape[0]
    posv = pos_ref[0]
    a = jnp.dot((ch + posv[0:1]).astype(BF16), w1_ref[0, 0], preferred_element_type=F32)
    b = jnp.dot((ch + posv[1:2]).astype(BF16), w1_ref[0, 1], preferred_element_type=F32)
    hid = a + pltpu.roll(b, rows - 1, 0)
    act = jax.nn.gelu(hid, approximate=True)
    o_ref[0, 0, 0] = jnp.dot(act.astype(BF16), w2_ref[0], preferred_element_type=F32)


def compress(chunks, pos, w1, w2):
    _, b, g, rows, width = chunks.shape
    hid = w1.shape[-1]
    return pl.pallas_call(
        _compress_kernel,
        grid=(2, b, g),
        in_specs=[pl.BlockSpec((1, 1, 1, rows, width), lambda i, bb, gg: (i, bb, gg, 0, 0)),
                  pl.BlockSpec((1, 2, width), lambda i, bb, gg: (i, 0, 0)),
                  pl.BlockSpec((1, 2, width, hid), lambda i, bb, gg: (i, 0, 0, 0)),
                  pl.BlockSpec((1, hid, HEAD_DIM), lambda i, bb, gg: (i, 0, 0))],
        out_specs=pl.BlockSpec((1, 1, 1, rows, HEAD_DIM), lambda i, bb, gg: (i, bb, gg, 0, 0)),
        out_shape=jax.ShapeDtypeStruct((2, b, g, rows, HEAD_DIM), F32),
        compiler_params=_cparams("parallel", "parallel", "parallel"),
        name="nsa_compress",
    )(chunks, pos, w1, w2)


def _cmp_attn_kernel(q_ref, kc_ref, vc_ref, ov_ref, place_ref, o_ref, sel_ref, *, n_sel_blocks, k_sel):
    qi = pl.program_id(1)
    ncr = kc_ref.shape[1]
    hpc = N_HEADS // C_KV_HEADS
    qpos = qi * QBLK + lax.broadcasted_iota(jnp.int32, (QBLK, 1), 0)
    cidx = lax.broadcasted_iota(jnp.int32, (1, ncr), 1)
    valid = (cidx * CMP_STRIDE + (CMP_BLOCK - 1)) <= qpos
    maskc = jnp.where(valid, 0.0, NEG_INF).astype(F32)
    anyv = (qpos >= CMP_BLOCK - 1).astype(F32)
    blk_id = lax.broadcasted_iota(jnp.int32, (HEAD_DIM, QBLK), 0)
    cur = (qi * QBLK + lax.broadcasted_iota(jnp.int32, (HEAD_DIM, QBLK), 1)) // SEL_BLOCK
    forced = jnp.logical_or(jnp.logical_or(blk_id == 0, blk_id == cur), blk_id == cur - 1)
    allowed = blk_id <= cur
    for g in range(C_KV_HEADS):
        heads = list(range(g * hpc, (g + 1) * hpc))
        qst = _stack_heads(lambda p: q_ref[0, :, p * LANES:(p + 1) * LANES], heads)
        sc = _dot_nt(qst, kc_ref[0, :, g * LANES:(g + 1) * LANES])
        sc3 = sc.reshape(hpc, QBLK, ncr) + maskc[None]
        m = jnp.max(sc3, axis=-1, keepdims=True)
        e = jnp.exp(sc3 - m)
        ssum = jnp.sum(e, axis=-1, keepdims=True)
        p = (e * (1.0 / ssum)) * anyv[None]
        o = jnp.dot(p.reshape(hpc * QBLK, ncr).astype(BF16), vc_ref[0, :, g * LANES:(g + 1) * LANES],
                    preferred_element_type=F32)
        for t2, blk in enumerate(_merge_pairs(o, hpc)):
            pidx = (g * hpc) // 2 + t2
            o_ref[0, :, pidx * LANES:(pidx + 1) * LANES] = blk
        hi, mid, lo = _split3(jnp.sum(p, axis=0))
        ovt = ov_ref[...]
        imp = (_dot_nt(ovt, hi) + _dot_nt(ovt, mid)) + _dot_nt(ovt, lo)
        score = jnp.where(forced, FORCE_SCORE, jnp.where(allowed, imp, NEG_INF))
        bits = pltpu.bitcast(score, jnp.int32)
        key = jnp.where(bits < 0, bits ^ jnp.int32(0x7FFFFFFF), bits)
        key_m1 = key - 1
        rank = jnp.zeros((HEAD_DIM, QBLK), jnp.int32)
        for i in range(n_sel_blocks):
            thr = jnp.where(blk_id > i, key_m1, key)
            rank = rank + jnp.where(key[i:i + 1, :] > thr, 1, 0)
        keep = jnp.logical_and(rank < k_sel, blk_id < n_sel_blocks)
        pen_t = jnp.where(keep, 0.0, -1.0).astype(BF16)
        pen = lax.dot_general(pen_t, place_ref[...], (((0,), (0,)), ((), ())), preferred_element_type=F32)
        sel_ref[0, :, g * LANES:(g + 1) * LANES] = pen.astype(sel_ref.dtype)


def cmp_attention(q, kcd, vcd, ov, n_sel_blocks, k_sel):
    b, s, _ = q.shape
    ncr = kcd.shape[1]
    place = np.zeros((HEAD_DIM, LANES), np.float32)
    place[np.arange(HEAD_DIM), HEAD_DIM + np.arange(HEAD_DIM)] = 1.0
    return pl.pallas_call(
        functools.partial(_cmp_attn_kernel, n_sel_blocks=n_sel_blocks, k_sel=k_sel),
        grid=(b, s // QBLK),
        in_specs=[pl.BlockSpec((1, QBLK, HQ), lambda bb, i: (bb, i, 0)),
                  pl.BlockSpec((1, ncr, 2 * LANES), lambda bb, i: (bb, 0, 0)),
                  pl.BlockSpec((1, ncr, 2 * LANES), lambda bb, i: (bb, 0, 0)),
                  pl.BlockSpec((HEAD_DIM, ncr), lambda bb, i: (0, 0)),
                  pl.BlockSpec((HEAD_DIM, LANES), lambda bb, i: (0, 0))],
        out_specs=[pl.BlockSpec((1, QBLK, HQ), lambda bb, i: (bb, i, 0)),
                   pl.BlockSpec((1, QBLK, 2 * LANES), lambda bb, i: (bb, i, 0))],
        out_shape=[jax.ShapeDtypeStruct((b, s, HQ), F32),
                   jax.ShapeDtypeStruct((b, s, 2 * LANES), BF16)],
        compiler_params=_cparams("parallel", "parallel"),
        name="nsa_cmp_attention",
    )(q, kcd, vcd, ov, jnp.asarray(place, BF16))


def _sel_attn_kernel(q_ref, pen_ref, k_ref, v_ref, bias_ref, o_ref, qst_ref, m_ref, acc_ref,
                     sa_ref, sb_ref, *, nbt):
    qi0 = pl.program_id(2) * SEL_QTILES
    hpc = N_HEADS // C_KV_HEADS
    nsub = SEL_CHUNK // QBLK
    rpt = hpc * QBLK
    lane = lax.broadcasted_iota(jnp.int32, (QBLK, LANES), 1)
    lo = lane < HEAD_DIM
    for w in range(SEL_QTILES):
        pen = pen_ref[0, w * QBLK:(w + 1) * QBLK, :].astype(F32)
        for t in range(hpc):
            q2 = q_ref[0, w * QBLK:(w + 1) * QBLK, (t // 2) * LANES:(t // 2 + 1) * LANES].astype(F32)
            if t % 2:
                q2 = pltpu.roll(q2, HEAD_DIM, 1)
            qst_ref[w * rpt + t * QBLK:w * rpt + (t + 1) * QBLK, :] = jnp.where(lo, q2, pen).astype(BF16)
    m_ref[...] = jnp.full(m_ref.shape, NEG_INF, F32)
    acc_ref[...] = jnp.zeros(acc_ref.shape, F32)

    n_chunks = (qi0 + SEL_QTILES - 1) // nsub + 1

    def scores(c, s_ref):
        k0 = pl.multiple_of(c * SEL_CHUNK, SEL_CHUNK)
        s_ref[...] = _dot_nt(qst_ref[...], k_ref[0, pl.ds(k0, SEL_CHUNK), :])

    def accumulate(c, s_ref):
        k0 = pl.multiple_of(c * SEL_CHUNK, SEL_CHUNK)
        ps, alphas = [], []
        for w in range(SEL_QTILES):
            rows = slice(w * rpt, (w + 1) * rpt)
            tiles = []
            for u in range(nsub):
                off = qi0 + w - nsub * c - u
                idx = jnp.where(off < 0, nbt, jnp.minimum(off, nbt - 1))
                tiles.append(s_ref[rows, u * QBLK:(u + 1) * QBLK] + bias_ref[idx].reshape(rpt, QBLK))
            m_old = m_ref[rows, :]
            m_new = jnp.maximum(m_old, jnp.max(functools.reduce(jnp.maximum, tiles), axis=-1, keepdims=True))
            ps.append(jnp.concatenate([jnp.exp(tl - m_new).astype(BF16) for tl in tiles], axis=1))
            alphas.append(jnp.exp(m_old - m_new))
            m_ref[rows, :] = m_new
        pv = jnp.dot(jnp.concatenate(ps, axis=0), v_ref[0, pl.ds(k0, SEL_CHUNK), :], preferred_element_type=F32)
        acc_ref[...] = jnp.concatenate(alphas, axis=0) * acc_ref[...] + pv

    scores(0, sa_ref)

    def body(cc, carry):
        c = 2 * cc
        scores(c + 1, sb_ref)
        accumulate(c, sa_ref)
        scores(jnp.minimum(c + 2, n_chunks - 1), sa_ref)
        accumulate(c + 1, sb_ref)
        return carry

    lax.fori_loop(0, n_chunks // 2, body, 0)

    @pl.when(n_chunks % 2 == 1)
    def _():
        accumulate(n_chunks - 1, sa_ref)

    acc = acc_ref[...]
    rolled = pltpu.roll(acc, HEAD_DIM, 1)
    for w in range(SEL_QTILES):
        for t2 in range(hpc // 2):
            ev = slice(w * rpt + 2 * t2 * QBLK, w * rpt + (2 * t2 + 1) * QBLK)
            od = slice(w * rpt + (2 * t2 + 1) * QBLK, w * rpt + (2 * t2 + 2) * QBLK)
            even = acc[ev] * (1.0 / rolled[ev])
            odd = rolled[od] * (1.0 / acc[od])
            o_ref[0, w * QBLK:(w + 1) * QBLK, t2 * LANES:(t2 + 1) * LANES] = jnp.where(lo, even, odd)


def sel_attention(q, kx, vx, pen, bias_tiles):
    b, s, _ = q.shape
    hpc = N_HEADS // C_KV_HEADS
    gw = hpc * HEAD_DIM
    nbt = bias_tiles.shape[0] - 1
    tq = SEL_QTILES * QBLK
    rows = SEL_QTILES * hpc * QBLK
    assert s % SEL_CHUNK == 0 and s % tq == 0
    return pl.pallas_call(
        functools.partial(_sel_attn_kernel, nbt=nbt),
        grid=(b, C_KV_HEADS, s // tq),
        in_specs=[pl.BlockSpec((1, tq, gw), lambda bb, g, i: (bb, i, g)),
                  pl.BlockSpec((1, tq, LANES), lambda bb, g, i: (bb, i, g)),
                  pl.BlockSpec((1, s, LANES), lambda bb, g, i: (bb, 0, g)),
                  pl.BlockSpec((1, s, LANES), lambda bb, g, i: (bb, 0, g)),
                  pl.BlockSpec((nbt + 1, hpc, QBLK, QBLK), lambda bb, g, i: (0, g, 0, 0))],
        out_specs=pl.BlockSpec((1, tq, gw), lambda bb, g, i: (bb, i, g)),
        out_shape=jax.ShapeDtypeStruct((b, s, HQ), F32),
        scratch_shapes=[pltpu.VMEM((rows, LANES), BF16),
                        pltpu.VMEM((rows, LANES), F32),
                        pltpu.VMEM((rows, LANES), F32),
                        pltpu.VMEM((rows, SEL_CHUNK), F32),
                        pltpu.VMEM((rows, SEL_CHUNK), F32)],
        compiler_params=_cparams("parallel", "parallel", "arbitrary"),
        name="nsa_sel_attention",
    )(q, pen, kx, vx, bias_tiles)


def _sel_bias_idx(s):
    nqt = s // QBLK
    far = -(-(int(np.argmax(_t5_bucket_np(np.arange(4 * MAX_DISTANCE)) == NUM_BUCKETS - 1)) + QBLK) // QBLK)
    nbt = min(nqt, far + 1)
    d0 = np.arange(nbt)[:, None, None] * QBLK
    dist = d0 + np.arange(QBLK)[None, :, None] - np.arange(QBLK)[None, None, :]
    idx = np.where(dist >= 0, _t5_bucket_np(dist), -1).astype(np.int32)
    return np.concatenate([idx, np.full((1, QBLK, QBLK), -1, np.int32)])


def _ffn_kernel(h_ref, g2_ref, wu_ref, cw_ref, cb_ref, wd_ref, g3_ref,
                o_ref, xn_ref, acc_ref, u_ref, a_ref, carry_ref, *, tiles_per_seq, nj, lag):
    i = pl.program_id(0)
    ns, ts, tn = xn_ref.shape[0], xn_ref.shape[1], a_ref.shape[3]
    cols = lambda c: slice(c * tn, (c + 1) * tn)

    @pl.when(i % tiles_per_seq == 0)
    def _():
        carry_ref[...] = jnp.zeros(carry_ref.shape, F32)

    top = 16
    rowt = lax.broadcasted_iota(jnp.int32, (top, tn), 0)

    def up(s, j):
        xn = xn_ref[s]
        u_ref[s, j % 2, 0] = jnp.dot(xn, wu_ref[:, cols(j)], preferred_element_type=F32)
        u_ref[s, j % 2, 1] = jnp.dot(xn, wu_ref[:, cols(nj + j)], preferred_element_type=F32)

    def conv(u, s1, s2, c):
        cw = cw_ref[:, cols(c)]
        return ((cb_ref[:, cols(c)] + u * cw[2:3]) + s2 * cw[0:1]) + s1 * cw[1:2]

    def conv_body(u, c):
        return conv(u, pltpu.roll(u, 1, 0), pltpu.roll(u, 2, 0), c)

    def conv_top(uc_ref, c):
        prev = carry_ref[c]
        p1, p2 = prev[7:8], prev[6:7]
        u = uc_ref[0:top]
        s1 = jnp.where(rowt == 0, p1, pltpu.roll(u, 1, 0))
        s2 = jnp.where(rowt == 0, p2, jnp.where(rowt == 1, p1, pltpu.roll(u, 2, 0)))
        carry_ref[c] = uc_ref[ts - 8:ts]
        return conv(u, s1, s2, c)

    def gated(cg, cv):
        return (jax.nn.gelu(cg, approximate=True) * cv).astype(BF16)

    def act(s, j):
        ur, ar = u_ref.at[s, j % 2], a_ref.at[s, j % 2]
        ar[...] = gated(conv_body(ur[0], j), conv_body(ur[1], nj + j))
        ar[0:top] = gated(conv_top(ur.at[0], j), conv_top(ur.at[1], nj + j))

    def down(s, j):
        acc_ref[s] += jnp.dot(a_ref[s, j % 2], wd_ref[j * tn:(j + 1) * tn, :], preferred_element_type=F32)

    def iteration(s, k):
        rows = slice(s * ts, (s + 1) * ts)
        if k == -1:
            xn_ref[s] = _rms(h_ref[rows, :], g2_ref[...]).astype(xn_ref.dtype)
            acc_ref[s] = jnp.zeros(acc_ref.shape[1:], F32)
        if 0 <= k + 1 < nj:
            up(s, k + 1)
        if 0 <= k < nj:
            act(s, k)
        if 0 <= k - 1 < nj:
            down(s, k - 1)
        if k == nj:
            o_ref[rows, :] = h_ref[rows, :] + _rms(acc_ref[s], g3_ref[...])

    for slot in range(-1, nj + 1 + (ns - 1) * lag):
        for s in range(ns):
            k = slot - s * lag
            if -1 <= k <= nj:
                iteration(s, k)


def conv_ffn(h, g2, w_up, conv_w, conv_b, w_down, g3, seq, tm, tn, ns=FFN_SUBTILES, lag=FFN_LAG):
    t, d = h.shape
    dff = w_down.shape[0]
    nj = dff // tn
    ts = tm // ns
    assert nj * tn == dff and ts * ns == tm and ts % 16 == 0
    wu, cw, cb, wd = w_up, conv_w, conv_b.reshape(1, -1), w_down
    resident = lambda a: pl.BlockSpec(a.shape, (lambda nd: (lambda i: (0,) * nd))(a.ndim))
    return pl.pallas_call(
        functools.partial(_ffn_kernel, tiles_per_seq=seq // tm, nj=nj, lag=lag),
        grid=(t // tm,),
        in_specs=[pl.BlockSpec((tm, d), lambda i: (i, 0)),
                  pl.BlockSpec((1, d), lambda i: (0, 0)),
                  resident(wu), resident(cw), resident(cb), resident(wd),
                  pl.BlockSpec((1, d), lambda i: (0, 0))],
        out_specs=pl.BlockSpec((tm, d), lambda i: (i, 0)),
        out_shape=jax.ShapeDtypeStruct((t, d), F32),
        scratch_shapes=[pltpu.VMEM((ns, ts, d), BF16), pltpu.VMEM((ns, ts, d), F32),
                        pltpu.VMEM((ns, 2, 2, ts, tn), F32), pltpu.VMEM((ns, 2, ts, tn), BF16),
                        pltpu.VMEM((2 * nj, 8, tn), F32)],
        compiler_params=_cparams("arbitrary"),
        name="conv_ffn",
    )(h, g2.reshape(1, d), wu, cw, cb, wd, g3.reshape(1, d))


def _row_tile(t):
    for tm in (1024, 512, 256, 128):
        if t % tm == 0:
            return tm
    raise ValueError(f"token count {t} is not a multiple of 128")


def mixer_a(h, gains, w_in, w_o, rel_table, bsz, seq):
    t, d = h.shape
    tm = _row_tile(t)
    n_dil = len(DIL_CONFIGS)
    a_in = w_in.shape[1]
    col_scale = np.ones((a_in,), np.float32).reshape(n_dil, 3, HQ)
    col_scale[:, 0] = ATTN_SCALE
    w = (w_in * col_scale.reshape(1, a_in)).astype(BF16)
    idx = np.concatenate([_banded_idx(1, window // dil, dil) for window, dil in DIL_CONFIGS])
    bias = bias_expand(rel_table, idx)
    outs, lses = [], []
    dils = [dil for _, dil in DIL_CONFIGS]
    blocks_per_seq = seq // QBLK
    xns = norm_permute(h, gains[0], dils, max([tm] + [QBLK * dil for dil in dils]))
    for gi, (window, dil) in enumerate(DIL_CONFIGS):
        assert window // dil <= QBLK and seq % (QBLK * dil) == 0
        qkv = matmul_resident(xns[gi], w[:, gi * 3 * HQ:(gi + 1) * 3 * HQ], BF16, tm)
        qkv = qkv.reshape(1, t, 3 * HQ)

        def rmap(part, dil=dil):
            return lambda n, i: (0, (n // dil) * blocks_per_seq + i * dil + n % dil, part)

        o, lse = banded_attention(
            qkv, qkv, qkv, bias[2 * gi:2 * gi + 2], n_rows=bsz * dil, n_tiles=seq // (QBLK * dil), tq=QBLK, n_prev=1, kw=HQ,
            q_map=rmap(0), k_map=rmap(1), v_map=rmap(2), o_map=rmap(0),
            out_shape=(1, t, HQ), lse_shape=(1, t, LANES), name=f"dilated_attention_{dil}")
        outs.append(o.reshape(t, HQ))
        lses.append(lse.reshape(t, LANES))
    e = jnp.asarray(_head_expand_matrix(), BF16)
    return combine_proj_a(outs, lses, dils, e, w_o.astype(BF16), gains[1], h, min(tm, 512))


def mixer_b(h, gains, w_in, sinks, w_o, rel_table, bsz, seq):
    t, d = h.shape
    tm = _row_tile(t)
    hk = B_KV_HEADS * HEAD_DIM
    n_in = w_in.shape[1]
    col_scale = np.ones((n_in,), np.float32)
    col_scale[:HQ] = ATTN_SCALE
    w = (w_in * col_scale[None]).astype(BF16)
    qkv = norm_matmul(h, gains[0], w, BF16, tm, n_in // 2 if (n_in // 2) % LANES == 0 else n_in)
    q = qkv[:, :HQ].reshape(bsz, seq, HQ)
    kd = _dup_groups(qkv[:, HQ:HQ + hk].reshape(bsz, seq, hk))
    vd = _dup_groups(qkv[:, HQ + hk:].reshape(bsz, seq, hk))
    bias = bias_expand(rel_table, _banded_idx(1, B_WINDOW - 1, 1))
    sink_rows = sinks.astype(F32)
    ident = lambda n, i: (n, i, 0)
    (o,) = banded_attention(q, kd, vd, bias, n_rows=bsz, n_tiles=seq // QBLK, tq=QBLK, n_prev=1,
                            kw=2 * LANES, q_map=ident, k_map=ident, v_map=ident, o_map=ident,
                            out_shape=(bsz, seq, HQ), sinks=sink_rows, stack=8, name="sink_window_attention")
    return _proj_call(_proj_kernel, "sink_proj", tm, [o.reshape(t, HQ)], [], w_o.astype(BF16), gains[1], h)


def mixer_c(h, gains, w_in, cmp_pos, cmp_w1, cmp_w2, w_o, rel_table, bsz, seq):
    t, d = h.shape
    tm = _row_tile(t)
    g = C_KV_HEADS
    hk = g * HEAD_DIM
    rest = w_in.shape[1] - HQ
    rest_pad = -(-rest // LANES) * LANES
    wq = (w_in[:, :HQ] * ATTN_SCALE).astype(BF16)
    wr = jnp.pad(w_in[:, HQ:], ((0, 0), (0, rest_pad - rest))).astype(BF16)
    q = norm_matmul(h, gains[0], wq, BF16, tm, HQ).reshape(bsz, seq, HQ)
    r = norm_matmul(h, gains[0], wr, F32, tm, rest_pad)
    kv = [r[:, i * hk:(i + 1) * hk].reshape(bsz, seq, hk) for i in range(6)]
    gates_raw = r[:, 6 * hk:6 * hk + LANES]

    ncr = seq // CMP_STRIDE
    half = CMP_BLOCK // 2
    assert half == CMP_STRIDE
    chunks = jnp.stack([kv[0], kv[1]]).reshape(2, bsz, ncr, half, g, HEAD_DIM)
    chunks = chunks.transpose(0, 1, 4, 2, 3, 5).reshape(2, bsz, g, ncr, half * HEAD_DIM)
    pos = cmp_pos.reshape(2, 2, half * HEAD_DIM)
    w1 = cmp_w1.reshape(2, 2, half * HEAD_DIM, -1).astype(BF16)
    cmp = compress(chunks, pos, w1, cmp_w2.astype(BF16))
    cmp = cmp.transpose(0, 1, 3, 2, 4).reshape(2, bsz, ncr, hk).astype(BF16)
    kcd, vcd = _dup_groups(cmp[0]), _dup_groups(cmp[1])

    ns = seq // SEL_BLOCK
    assert ns <= HEAD_DIM
    k_sel = min(N_SELECT, ns)
    starts = np.arange(ncr) * CMP_STRIDE
    blk = np.arange(HEAD_DIM)
    ov = ((starts[None, :] < (blk[:, None] + 1) * SEL_BLOCK)
          & (starts[None, :] + CMP_BLOCK > blk[:, None] * SEL_BLOCK)
          & (blk[:, None] < ns) & (starts[None, :] + CMP_BLOCK <= seq))
    o_c, pen = cmp_attention(q, kcd, vcd, jnp.asarray(ov.astype(np.float32), BF16), ns, k_sel)

    sel_bias = bias_expand(rel_table, _sel_bias_idx(seq))
    key_blk = np.arange(seq)[:, None] // SEL_BLOCK == np.arange(HEAD_DIM)[None, :]
    onehot = jnp.broadcast_to(jnp.asarray(np.where(key_blk, -NEG_INF, 0.0), BF16), (bsz, seq, HEAD_DIM))
    ones = jnp.ones((bsz, seq, HEAD_DIM), BF16)
    ks, vs = kv[2].astype(BF16), kv[3].astype(BF16)
    kx = jnp.concatenate([ks[..., :HEAD_DIM], onehot, ks[..., HEAD_DIM:], onehot], axis=-1)
    vx = jnp.concatenate([vs[..., :HEAD_DIM], ones, vs[..., HEAD_DIM:], ones], axis=-1)
    o_s = sel_attention(q, kx, vx, pen, sel_bias)

    n_prev = -(-(C_WINDOW - 1) // QBLK)
    tqw = n_prev * QBLK
    wbias = bias_expand(rel_table, _banded_idx(n_prev, C_WINDOW - 1, 1))
    ident = lambda n, i: (n, i, 0)
    (o_w,) = banded_attention(q, _dup_groups(kv[4].astype(BF16)), _dup_groups(kv[5].astype(BF16)), wbias,
                              n_rows=bsz, n_tiles=seq // tqw, tq=tqw, n_prev=n_prev, kw=2 * LANES,
                              q_map=ident, k_map=ident, v_map=ident, o_map=ident,
                              out_shape=(bsz, seq, HQ), stack=8, name="nsa_window_attention")

    e = jnp.asarray(_head_expand_matrix(), BF16)
    return _proj_call(_proj_c_kernel, "nsa_gate_proj", min(tm, 512),
                      [o_c.reshape(t, HQ), o_s.reshape(t, HQ), o_w.reshape(t, HQ), gates_raw], [e],
                      w_o.astype(BF16), gains[1], h)


def kernel(x, rel_table, norm_gains, a_w_in, a_w_o, b_w_in, b_sinks, b_w_o, c_w_in, c_cmp_pos, c_cmp_w1,
           c_cmp_w2, c_w_o, ffn_w_up, ffn_conv_w, ffn_conv_b, ffn_w_down):
    bsz, seq, d = x.shape
    depth = norm_gains.shape[0]
    h = x.reshape(bsz * seq, d)
    tm = _row_tile(seq)
    for i in range(depth):
        kind, j = i % 3, i // 3
        g = norm_gains[i]
        if kind == 0:
            h = mixer_a(h, g, a_w_in[j], a_w_o[j], rel_table, bsz, seq)
        elif kind == 1:
            h = mixer_b(h, g, b_w_in[j], b_sinks[j], b_w_o[j], rel_table, bsz, seq)
        else:
            h = mixer_c(h, g, c_w_in[j], c_cmp_pos[j], c_cmp_w1[j], c_cmp_w2[j], c_w_o[j], rel_table, bsz, seq)
        h = conv_ffn(h, g[2], ffn_w_up[i].astype(BF16), ffn_conv_w[i], ffn_conv_b[i],
                     ffn_w_down[i].astype(BF16), g[3], seq, min(tm, FFN_ROWS), 256)
    return h.reshape(bsz, seq, d)
```

```python
import functools
import math

import numpy as np
import jax
import jax.numpy as jnp
from jax import lax
from jax.experimental import pallas as pl
from jax.experimental.pallas import tpu as pltpu

F32 = jnp.float32
BF16 = jnp.bfloat16

N_HEADS = 16
HEAD_DIM = 64
HQ = N_HEADS * HEAD_DIM
LANES = 128
ATTN_SCALE = HEAD_DIM ** -0.5
NUM_BUCKETS = 32
MAX_DISTANCE = 2048
RMS_EPS = 1e-6
NEG_INF = -1e30
FORCE_SCORE = 1e9
DIL_CONFIGS = ((128, 1), (512, 4), (2048, 16))
B_KV_HEADS = 2
B_WINDOW = 128
C_KV_HEADS = 2
CMP_BLOCK = 32
CMP_STRIDE = 16
SEL_BLOCK = 64
N_SELECT = 16
C_WINDOW = 512
CONV_WIDTH = 3
QBLK = 128
SEL_CHUNK = 512
SEL_QTILES = 2
FFN_COLS = 256
FFN_ROWS = 512
FFN_SUBTILES = 2
FFN_LAG = 0
VMEM_LIMIT = 56 * 1024 * 1024


def _cparams(*sem):
    return pltpu.CompilerParams(dimension_semantics=sem, vmem_limit_bytes=VMEM_LIMIT)


def _t5_bucket_np(dist):
    max_exact = NUM_BUCKETS // 2
    d = np.maximum(dist, 0)
    df = np.maximum(d, 1).astype(np.float64)
    large = max_exact + np.floor(np.log(df / max_exact) / math.log(MAX_DISTANCE / max_exact)
                                 * (NUM_BUCKETS - max_exact) + 1e-9).astype(np.int64)
    large = np.minimum(large, NUM_BUCKETS - 1)
    return np.where(d < max_exact, d, large).astype(np.int32)


def _rms(x, g):
    ms = jnp.mean(x * x, axis=-1, keepdims=True)
    return (x * lax.rsqrt(ms + RMS_EPS)) * g


def _split3(w):
    hi = w.astype(BF16)
    r1 = w - hi.astype(F32)
    mid = r1.astype(BF16)
    lo = (r1 - mid.astype(F32)).astype(BF16)
    return hi, mid, lo


def _expand_heads(w, e3):
    lane = lax.broadcasted_iota(jnp.int32, w.shape, 1)
    r1 = w - w.astype(BF16).astype(F32)
    r2 = r1 - r1.astype(BF16).astype(F32)
    packed = jnp.where(lane < N_HEADS, w,
                       jnp.where(lane < 2 * N_HEADS, pltpu.roll(r1, N_HEADS, 1), pltpu.roll(r2, 2 * N_HEADS, 1)))
    return jnp.dot(packed.astype(BF16), e3, preferred_element_type=F32)


def _dot_nt(a, b):
    return lax.dot_general(a, b, (((1,), (1,)), ((), ())), preferred_element_type=F32)


def _stack_heads(q_tile, heads):
    lane = lax.broadcasted_iota(jnp.int32, (QBLK, LANES), 1)
    lo = lane < HEAD_DIM
    pieces = []
    for h in heads:
        q2 = q_tile(h // 2)
        keep = lo if h % 2 == 0 else jnp.logical_not(lo)
        pieces.append(jnp.where(keep, q2, jnp.zeros_like(q2)))
    return jnp.concatenate(pieces, axis=0)


def _merge_pairs(o, n_heads):
    lane = lax.broadcasted_iota(jnp.int32, (QBLK, LANES), 1)
    lo = lane < HEAD_DIM
    out = []
    for t in range(0, n_heads, 2):
        out.append(jnp.where(lo, o[t * QBLK:(t + 1) * QBLK], o[(t + 1) * QBLK:(t + 2) * QBLK]))
    return out


def _bias_kernel(tab_ref, rng_ref, idx_ref, o_ref):
    n = pl.program_id(0)
    idx = idx_ref[0]
    o_ref[...] = jnp.full(o_ref.shape, NEG_INF, F32)

    def body(b, carry):
        hit = idx == b
        for h in range(N_HEADS):
            o_ref[0, h] = jnp.where(hit, tab_ref[b, h], o_ref[0, h])
        return carry

    lax.fori_loop(rng_ref[n, 0], rng_ref[n, 1] + 1, body, 0)


def bias_expand(table, idx):
    n, r, c = idx.shape
    flat = idx.reshape(n, -1)
    lo = np.where(flat >= 0, flat, NUM_BUCKETS).min(axis=1)
    rng = np.stack([lo, flat.max(axis=1)], axis=1).astype(np.int32)
    return pl.pallas_call(
        _bias_kernel,
        grid=(n,),
        in_specs=[pl.BlockSpec(memory_space=pltpu.SMEM),
                  pl.BlockSpec(memory_space=pltpu.SMEM),
                  pl.BlockSpec((1, r, c), lambda i: (i, 0, 0))],
        out_specs=pl.BlockSpec((1, N_HEADS, r, c), lambda i: (i, 0, 0, 0)),
        out_shape=jax.ShapeDtypeStruct((n, N_HEADS, r, c), F32),
        compiler_params=_cparams("parallel"),
        name="bias_expand",
    )(table, jnp.asarray(rng), jnp.asarray(idx))


def _banded_idx(n_prev, max_dist, stride):
    kb = (n_prev + 1) * QBLK
    dist = np.arange(QBLK)[:, None] + n_prev * QBLK - np.arange(kb)[None, :]
    valid = (dist >= 0) & (dist <= max_dist)
    idx = np.where(valid, _t5_bucket_np(dist * stride), -1).astype(np.int32)
    if n_prev > 1:
        return idx[None]
    first = np.where(np.arange(kb)[None, :] < n_prev * QBLK, -1, idx).astype(np.int32)
    return np.stack([idx, first])


def _norm_mm_kernel(x_ref, g_ref, w_ref, o_ref, xn_ref):
    @pl.when(pl.program_id(1) == 0)
    def _():
        xn_ref[...] = _rms(x_ref[...], g_ref[...]).astype(xn_ref.dtype)

    o_ref[...] = jnp.dot(xn_ref[...], w_ref[...], preferred_element_type=F32).astype(o_ref.dtype)


def norm_matmul(x, gain, w, out_dtype, tm, tn):
    t, d = x.shape
    n = w.shape[1]
    scratch = [pltpu.VMEM((tm, d), BF16)]
    return pl.pallas_call(
        _norm_mm_kernel,
        grid=(t // tm, n // tn),
        in_specs=[pl.BlockSpec((tm, d), lambda i, j: (i, 0)),
                  pl.BlockSpec((1, d), lambda i, j: (0, 0)),
                  pl.BlockSpec((d, tn), lambda i, j: (0, j))],
        out_specs=pl.BlockSpec((tm, tn), lambda i, j: (i, j)),
        out_shape=jax.ShapeDtypeStruct((t, n), out_dtype),
        scratch_shapes=scratch,
        compiler_params=_cparams("parallel", "arbitrary"),
        name="norm_matmul",
    )(x, gain.reshape(1, d), w)


def _norm_perm_kernel(x_ref, g_ref, *refs, dils):
    o_refs, xs_ref = refs[:len(dils)], refs[len(dils)]
    xn = _rms(x_ref[...], g_ref[...])
    nc = xs_ref.shape[0]
    for c in range(nc):
        xs_ref[c] = xn[:, c * LANES:(c + 1) * LANES]
    for o_ref, dil in zip(o_refs, dils):
        if dil == 1:
            o_ref[...] = xn.astype(o_ref.dtype)
            continue
        span = QBLK * dil
        for u in range(x_ref.shape[0] // span):
            for r in range(dil):
                rows = jnp.concatenate(
                    [xs_ref[c, pl.ds(u * span + r, QBLK, stride=dil), :] for c in range(nc)], axis=1)
                dst = u * span + r * QBLK
                o_ref[dst:dst + QBLK, :] = rows.astype(o_ref.dtype)


def norm_permute(x, gain, dils, tm):
    t, d = x.shape
    assert all(tm % (QBLK * dil) == 0 for dil in dils)
    return pl.pallas_call(
        functools.partial(_norm_perm_kernel, dils=tuple(dils)),
        grid=(t // tm,),
        in_specs=[pl.BlockSpec((tm, d), lambda i: (i, 0)),
                  pl.BlockSpec((1, d), lambda i: (0, 0))],
        out_specs=[pl.BlockSpec((tm, d), lambda i: (i, 0)) for _ in dils],
        out_shape=[jax.ShapeDtypeStruct((t, d), BF16) for _ in dils],
        scratch_shapes=[pltpu.VMEM((d // LANES, tm, LANES), F32)],
        compiler_params=_cparams("parallel"),
        name="norm_permute",
    )(x, gain.reshape(1, d))


def _mm_kernel(x_ref, w_ref, o_ref):
    o_ref[...] = jnp.dot(x_ref[...], w_ref[...], preferred_element_type=F32).astype(o_ref.dtype)


def matmul_resident(x, w, out_dtype, tm):
    t, k = x.shape
    n = w.shape[1]
    return pl.pallas_call(
        _mm_kernel,
        grid=(t // tm,),
        in_specs=[pl.BlockSpec((tm, k), lambda i: (i, 0)),
                  pl.BlockSpec((k, n), lambda i: (0, 0))],
        out_specs=pl.BlockSpec((tm, n), lambda i: (i, 0)),
        out_shape=jax.ShapeDtypeStruct((t, n), out_dtype),
        compiler_params=_cparams("parallel"),
        name="matmul_resident",
    )(x, w)


def _finish_proj(y, w_ref, g_ref, h_ref, o_ref):
    z = jnp.dot(y.astype(BF16), w_ref[...], preferred_element_type=F32)
    o_ref[...] = h_ref[...] + _rms(z, g_ref[...])


def _proj_kernel(y_ref, w_ref, g_ref, h_ref, o_ref):
    _finish_proj(y_ref[...], w_ref, g_ref, h_ref, o_ref)


def _unpermute(src_ref, dst_ref, dil, tm, i):
    span = QBLK * dil
    nc = dst_ref.shape[0]
    if span <= tm:
        for u in range(tm // span):
            for r in range(dil):
                lo = u * span + r * QBLK
                for c in range(nc):
                    dst_ref[c, pl.ds(u * span + r, QBLK, stride=dil), :] = src_ref[lo:lo + QBLK,
                                                                                   c * LANES:(c + 1) * LANES]
    else:
        per = tm // dil
        off = (i % (span // tm)) * per
        for r in range(dil):
            lo = pl.multiple_of(r * QBLK + off, 8)
            for c in range(nc):
                dst_ref[c, pl.ds(r, per, stride=dil), :] = src_ref[pl.ds(lo, per), c * LANES:(c + 1) * LANES]
    return jnp.concatenate([dst_ref[c] for c in range(nc)], axis=1) if nc > 1 else dst_ref[0]


def _proj_a_kernel(*refs, dils, tm):
    n = len(dils)
    o_refs, l_refs = refs[:n], refs[n:2 * n]
    e_ref, w_ref, g_ref, h_ref, out_ref = refs[2 * n:2 * n + 5]
    scratch = refs[2 * n + 5:]
    i = pl.program_id(0)
    outs, lses = [], []
    si = 0
    for gi, dil in enumerate(dils):
        if dil == 1:
            outs.append(o_refs[gi][...])
            lses.append(l_refs[gi][...])
        else:
            outs.append(_unpermute(o_refs[gi], scratch[si], dil, tm, i))
            lses.append(_unpermute(l_refs[gi], scratch[si + 1], dil, tm, i))
            si += 2
    mx = functools.reduce(jnp.maximum, lses)
    es = [jnp.exp(l - mx) for l in lses]
    inv = 1.0 / functools.reduce(lambda a, b: a + b, es)
    e = e_ref[...]
    y = outs[0] * _expand_heads(es[0] * inv, e)
    for gi in range(1, n):
        y = y + outs[gi] * _expand_heads(es[gi] * inv, e)
    _finish_proj(y, w_ref, g_ref, h_ref, out_ref)


def combine_proj_a(outs, lses, dils, e, w, gain, h, tm):
    t, d = h.shape

    def row_spec(width, dil):
        rows = max(tm, QBLK * dil)
        return pl.BlockSpec((rows, width), lambda i, q=rows // tm: (i // q, 0))

    in_specs = [row_spec(HQ, dil) for dil in dils] + [row_spec(LANES, dil) for dil in dils] + [
        pl.BlockSpec(e.shape, lambda i: (0, 0)),
        pl.BlockSpec(w.shape, lambda i: (0, 0)),
        pl.BlockSpec((1, d), lambda i: (0, 0)),
        pl.BlockSpec((tm, d), lambda i: (i, 0))]
    scratch = []
    for dil in dils:
        if dil > 1:
            scratch += [pltpu.VMEM((HQ // LANES, tm, LANES), F32), pltpu.VMEM((1, tm, LANES), F32)]
    return pl.pallas_call(
        functools.partial(_proj_a_kernel, dils=tuple(dils), tm=tm),
        grid=(t // tm,),
        in_specs=in_specs,
        out_specs=pl.BlockSpec((tm, d), lambda i: (i, 0)),
        out_shape=jax.ShapeDtypeStruct((t, d), F32),
        scratch_shapes=scratch,
        compiler_params=_cparams("arbitrary"),
        name="dilated_combine_proj",
    )(*outs, *lses, e, w, gain.reshape(1, d), h)


def _proj_c_kernel(oc_ref, os_ref, ow_ref, gr_ref, e_ref, w_ref, g_ref, h_ref, o_ref):
    sig = jax.nn.sigmoid(gr_ref[...])
    e = e_ref[...]
    gate = lambda i: _expand_heads(sig if i == 0 else pltpu.roll(sig, LANES - i * N_HEADS, 1), e)
    y = gate(0) * oc_ref[...]
    y = y + gate(1) * os_ref[...]
    y = y + gate(2) * ow_ref[...]
    _finish_proj(y, w_ref, g_ref, h_ref, o_ref)


def _proj_call(kernel, name, tm, row_inputs, const_inputs, w, gain, h):
    t, d = h.shape
    k = w.shape[0]
    row_specs = [pl.BlockSpec((tm, a.shape[1]), lambda i: (i, 0)) for a in row_inputs]
    const_specs = [pl.BlockSpec(a.shape, (lambda nd: (lambda i: (0,) * nd))(a.ndim)) for a in const_inputs]
    return pl.pallas_call(
        kernel,
        grid=(t // tm,),
        in_specs=row_specs + const_specs + [
            pl.BlockSpec((k, d), lambda i: (0, 0)),
            pl.BlockSpec((1, d), lambda i: (0, 0)),
            pl.BlockSpec((tm, d), lambda i: (i, 0))],
        out_specs=pl.BlockSpec((tm, d), lambda i: (i, 0)),
        out_shape=jax.ShapeDtypeStruct((t, d), F32),
        compiler_params=_cparams("parallel"),
        name=name,
    )(*row_inputs, *const_inputs, w, gain.reshape(1, d), h)


def _head_expand_matrix():
    e = np.zeros((LANES, HQ), np.float32)
    for term in range(3):
        for h in range(N_HEADS):
            e[term * N_HEADS + h, h * HEAD_DIM:(h + 1) * HEAD_DIM] = 1.0
    return e


def _banded_kernel(*refs, tq, n_prev, n_chunks, stack, with_sinks, with_lse):
    q_ref, kp_ref, kc_ref, vp_ref, vc_ref, bias_ref = refs[:6]
    pos = 6
    sink_ref = None
    if with_sinks:
        sink_ref = refs[pos]
        pos += 1
    o_ref = refs[pos]
    pos += 1
    lse_ref = None
    if with_lse:
        lse_ref = refs[pos]
        pos += 1
    kbuf, vbuf = refs[pos], refs[pos + 1]

    i = pl.program_id(1)
    hpc = N_HEADS // n_chunks
    kb = (n_prev + 1) * QBLK
    kbuf[0:tq] = kp_ref[0]
    kbuf[tq:2 * tq] = kc_ref[0]
    vbuf[0:tq] = vp_ref[0]
    vbuf[tq:2 * tq] = vc_ref[0]
    lane = lax.broadcasted_iota(jnp.int32, (QBLK, LANES), 1)
    head_row = lax.broadcasted_iota(jnp.int32, (stack * QBLK, 1), 0)

    for s in range(tq // QBLK):
        r0 = s * QBLK
        k0 = tq + r0 - n_prev * QBLK
        variants = bias_ref.shape[0] > 1
        if variants:
            first = jnp.where(i == 0, 1, 0)
        else:
            col = lax.broadcasted_iota(jnp.int32, (1, kb), 1) + k0
            kmask = jnp.where(jnp.logical_and(i == 0, col < tq), NEG_INF, 0.0).astype(F32)
        lse_acc = jnp.zeros((QBLK, LANES), F32)
        for h0 in range(0, N_HEADS, stack):
            heads = list(range(h0, h0 + stack))
            c = h0 // hpc
            qst = _stack_heads(lambda p: q_ref[0, r0:r0 + QBLK, p * LANES:(p + 1) * LANES], heads)
            kx = kbuf[k0:k0 + kb, c * LANES:(c + 1) * LANES]
            vx = vbuf[k0:k0 + kb, c * LANES:(c + 1) * LANES]
            sc = _dot_nt(qst, kx)
            if variants:
                sc = sc + bias_ref[first, h0:h0 + stack].reshape(stack * QBLK, kb)
            else:
                sc = sc + bias_ref[0, h0:h0 + stack].reshape(stack * QBLK, kb) + kmask
            m = jnp.max(sc, axis=-1, keepdims=True)
            if with_sinks:
                sk = sink_ref[h0 + stack - 1]
                for t in range(stack - 2, -1, -1):
                    sk = jnp.where(head_row < (t + 1) * QBLK, sink_ref[h0 + t], sk)
                m = jnp.maximum(m, sk)
            p = jnp.exp(sc - m)
            den = jnp.sum(p, axis=-1, keepdims=True)
            norm = den + jnp.exp(sk - m) if with_sinks else den
            o = jnp.dot(p.astype(BF16), vx, preferred_element_type=F32) * (1.0 / norm)
            for t2, blk in enumerate(_merge_pairs(o, stack)):
                pidx = h0 // 2 + t2
                o_ref[0, r0:r0 + QBLK, pidx * LANES:(pidx + 1) * LANES] = blk.astype(o_ref.dtype)
            if with_lse:
                lse = m + jnp.log(den)
                for t, h in enumerate(heads):
                    lse_acc = jnp.where(lane == h, lse[t * QBLK:(t + 1) * QBLK], lse_acc)
        if with_lse:
            lse_ref[0, r0:r0 + QBLK, :] = lse_acc


def banded_attention(q_arr, k_arr, v_arr, bias, *, n_rows, n_tiles, tq, n_prev, kw,
                     q_map, k_map, v_map, o_map, out_shape, lse_shape=None, sinks=None, stack=2, name):
    n_chunks = kw // LANES
    assert stack % 2 == 0 and (N_HEADS // n_chunks) % stack == 0
    kb = (n_prev + 1) * QBLK
    with_sinks = sinks is not None
    with_lse = lse_shape is not None

    def prev(fn):
        return lambda n, i: fn(n, jnp.maximum(i - 1, 0))

    in_specs = [pl.BlockSpec((1, tq, HQ), q_map),
                pl.BlockSpec((1, tq, kw), prev(k_map)),
                pl.BlockSpec((1, tq, kw), k_map),
                pl.BlockSpec((1, tq, kw), prev(v_map)),
                pl.BlockSpec((1, tq, kw), v_map),
                pl.BlockSpec(bias.shape, lambda n, i: (0, 0, 0, 0))]
    assert bias.shape[1:] == (N_HEADS, QBLK, kb) and (bias.shape[0] == 1 or tq == QBLK)
    args = [q_arr, k_arr, k_arr, v_arr, v_arr, bias]
    if with_sinks:
        in_specs.append(pl.BlockSpec(memory_space=pltpu.SMEM))
        args.append(sinks)
    out_specs = [pl.BlockSpec((1, tq, HQ), o_map)]
    out_shapes = [jax.ShapeDtypeStruct(out_shape, F32)]
    if with_lse:
        out_specs.append(pl.BlockSpec((1, tq, LANES), o_map))
        out_shapes.append(jax.ShapeDtypeStruct(lse_shape, F32))
    res = pl.pallas_call(
        functools.partial(_banded_kernel, tq=tq, n_prev=n_prev, n_chunks=n_chunks, stack=stack,
                          with_sinks=with_sinks, with_lse=with_lse),
        grid=(n_rows, n_tiles),
        in_specs=in_specs,
        out_specs=out_specs,
        out_shape=out_shapes,
        scratch_shapes=[pltpu.VMEM((2 * tq, kw), BF16), pltpu.VMEM((2 * tq, kw), BF16)],
        compiler_params=_cparams("parallel", "arbitrary"),
        name=name,
    )(*args)
    return res


def _dup_groups(x):
    g0, g1 = x[..., :HEAD_DIM], x[..., HEAD_DIM:]
    return jnp.concatenate([g0, g0, g1, g1], axis=-1)


def _compress_kernel(ch_ref, pos_ref, w1_ref, w2_ref, o_ref):
    ch = ch_ref[0, 0, 0]
    rows = ch.shape[0]
    posv = pos_ref[0]
    a = jnp.dot((ch + posv[0:1]).astype(BF16), w1_ref[0, 0], preferred_element_type=F32)
    b = jnp.dot((ch + posv[1:2]).astype(BF16), w1_ref[0, 1], preferred_element_type=F32)
    hid = a + pltpu.roll(b, rows - 1, 0)
    act = jax.nn.gelu(hid, approximate=True)
    o_ref[0, 0, 0] = jnp.dot(act.astype(BF16), w2_ref[0], preferred_element_type=F32)


def compress(chunks, pos, w1, w2):
    _, b, g, rows, width = chunks.shape
    hid = w1.shape[-1]
    return pl.pallas_call(
        _compress_kernel,
        grid=(2, b, g),
        in_specs=[pl.BlockSpec((1, 1, 1, rows, width), lambda i, bb, gg: (i, bb, gg, 0, 0)),
                  pl.BlockSpec((1, 2, width), lambda i, bb, gg: (i, 0, 0)),
                  pl.BlockSpec((1, 2, width, hid), lambda i, bb, gg: (i, 0, 0, 0)),
                  pl.BlockSpec((1, hid, HEAD_DIM), lambda i, bb, gg: (i, 0, 0))],
        out_specs=pl.BlockSpec((1, 1, 1, rows, HEAD_DIM), lambda i, bb, gg: (i, bb, gg, 0, 0)),
        out_shape=jax.ShapeDtypeStruct((2, b, g, rows, HEAD_DIM), F32),
        compiler_params=_cparams("parallel", "parallel", "parallel"),
        name="nsa_compress",
    )(chunks, pos, w1, w2)


def _cmp_attn_kernel(q_ref, kc_ref, vc_ref, ov_ref, place_ref, o_ref, sel_ref, *, n_sel_blocks, k_sel):
    qi = pl.program_id(1)
    ncr = kc_ref.shape[1]
    hpc = N_HEADS // C_KV_HEADS
    qpos = qi * QBLK + lax.broadcasted_iota(jnp.int32, (QBLK, 1), 0)
    cidx = lax.broadcasted_iota(jnp.int32, (1, ncr), 1)
    valid = (cidx * CMP_STRIDE + (CMP_BLOCK - 1)) <= qpos
    maskc = jnp.where(valid, 0.0, NEG_INF).astype(F32)
    anyv = (qpos >= CMP_BLOCK - 1).astype(F32)
    blk_id = lax.broadcasted_iota(jnp.int32, (HEAD_DIM, QBLK), 0)
    cur = (qi * QBLK + lax.broadcasted_iota(jnp.int32, (HEAD_DIM, QBLK), 1)) // SEL_BLOCK
    forced = jnp.logical_or(jnp.logical_or(blk_id == 0, blk_id == cur), blk_id == cur - 1)
    allowed = blk_id <= cur
    for g in range(C_KV_HEADS):
        heads = list(range(g * hpc, (g + 1) * hpc))
        qst = _stack_heads(lambda p: q_ref[0, :, p * LANES:(p + 1) * LANES], heads)
        sc = _dot_nt(qst, kc_ref[0, :, g * LANES:(g + 1) * LANES])
        sc3 = sc.reshape(hpc, QBLK, ncr) + maskc[None]
        m = jnp.max(sc3, axis=-1, keepdims=True)
        e = jnp.exp(sc3 - m)
        ssum = jnp.sum(e, axis=-1, keepdims=True)
        p = (e * (1.0 / ssum)) * anyv[None]
        o = jnp.dot(p.reshape(hpc * QBLK, ncr).astype(BF16), vc_ref[0, :, g * LANES:(g + 1) * LANES],
                    preferred_element_type=F32)
        for t2, blk in enumerate(_merge_pairs(o, hpc)):
            pidx = (g * hpc) // 2 + t2
            o_ref[0, :, pidx * LANES:(pidx + 1) * LANES] = blk
        hi, mid, lo = _split3(jnp.sum(p, axis=0))
        ovt = ov_ref[...]
        imp = (_dot_nt(ovt, hi) + _dot_nt(ovt, mid)) + _dot_nt(ovt, lo)
        score = jnp.where(forced, FORCE_SCORE, jnp.where(allowed, imp, NEG_INF))
        bits = pltpu.bitcast(score, jnp.int32)
        key = jnp.where(bits < 0, bits ^ jnp.int32(0x7FFFFFFF), bits)
        key_m1 = key - 1
        rank = jnp.zeros((HEAD_DIM, QBLK), jnp.int32)
        for i in range(n_sel_blocks):
            thr = jnp.where(blk_id > i, key_m1, key)
            rank = rank + jnp.where(key[i:i + 1, :] > thr, 1, 0)
        keep = jnp.logical_and(rank < k_sel, blk_id < n_sel_blocks)
        pen_t = jnp.where(keep, 0.0, -1.0).astype(BF16)
        pen = lax.dot_general(pen_t, place_ref[...], (((0,), (0,)), ((), ())), preferred_element_type=F32)
        sel_ref[0, :, g * LANES:(g + 1) * LANES] = pen.astype(sel_ref.dtype)


def cmp_attention(q, kcd, vcd, ov, n_sel_blocks, k_sel):
    b, s, _ = q.shape
    ncr = kcd.shape[1]
    place = np.zeros((HEAD_DIM, LANES), np.float32)
    place[np.arange(HEAD_DIM), HEAD_DIM + np.arange(HEAD_DIM)] = 1.0
    return pl.pallas_call(
        functools.partial(_cmp_attn_kernel, n_sel_blocks=n_sel_blocks, k_sel=k_sel),
        grid=(b, s // QBLK),
        in_specs=[pl.BlockSpec((1, QBLK, HQ), lambda bb, i: (bb, i, 0)),
                  pl.BlockSpec((1, ncr, 2 * LANES), lambda bb, i: (bb, 0, 0)),
                  pl.BlockSpec((1, ncr, 2 * LANES), lambda bb, i: (bb, 0, 0)),
                  pl.BlockSpec((HEAD_DIM, ncr), lambda bb, i: (0, 0)),
                  pl.BlockSpec((HEAD_DIM, LANES), lambda bb, i: (0, 0))],
        out_specs=[pl.BlockSpec((1, QBLK, HQ), lambda bb, i: (bb, i, 0)),
                   pl.BlockSpec((1, QBLK, 2 * LANES), lambda bb, i: (bb, i, 0))],
        out_shape=[jax.ShapeDtypeStruct((b, s, HQ), F32),
                   jax.ShapeDtypeStruct((b, s, 2 * LANES), BF16)],
        compiler_params=_cparams("parallel", "parallel"),
        name="nsa_cmp_attention",
    )(q, kcd, vcd, ov, jnp.asarray(place, BF16))


def _sel_attn_kernel(q_ref, pen_ref, k_ref, v_ref, bias_ref, o_ref, qst_ref, m_ref, acc_ref,
                     sa_ref, sb_ref, *, nbt):
    qi0 = pl.program_id(2) * SEL_QTILES
    hpc = N_HEADS // C_KV_HEADS
    nsub = SEL_CHUNK // QBLK
    rpt = hpc * QBLK
    lane = lax.broadcasted_iota(jnp.int32, (QBLK, LANES), 1)
    lo = lane < HEAD_DIM
    for w in range(SEL_QTILES):
        pen = pen_ref[0, w * QBLK:(w + 1) * QBLK, :].astype(F32)
        for t in range(hpc):
            q2 = q_ref[0, w * QBLK:(w + 1) * QBLK, (t // 2) * LANES:(t // 2 + 1) * LANES].astype(F32)
            if t % 2:
                q2 = pltpu.roll(q2, HEAD_DIM, 1)
            qst_ref[w * rpt + t * QBLK:w * rpt + (t + 1) * QBLK, :] = jnp.where(lo, q2, pen).astype(BF16)
    m_ref[...] = jnp.full(m_ref.shape, NEG_INF, F32)
    acc_ref[...] = jnp.zeros(acc_ref.shape, F32)

    n_chunks = (qi0 + SEL_QTILES - 1) // nsub + 1

    def scores(c, s_ref):
        k0 = pl.multiple_of(c * SEL_CHUNK, SEL_CHUNK)
        s_ref[...] = _dot_nt(qst_ref[...], k_ref[0, pl.ds(k0, SEL_CHUNK), :])

    def accumulate(c, s_ref):
        k0 = pl.multiple_of(c * SEL_CHUNK, SEL_CHUNK)
        ps, alphas = [], []
        for w in range(SEL_QTILES):
            rows = slice(w * rpt, (w + 1) * rpt)
            tiles = []
            for u in range(nsub):
                off = qi0 + w - nsub * c - u
                idx = jnp.where(off < 0, nbt, jnp.minimum(off, nbt - 1))
                tiles.append(s_ref[rows, u * QBLK:(u + 1) * QBLK] + bias_ref[idx].reshape(rpt, QBLK))
            m_old = m_ref[rows, :]
            m_new = jnp.maximum(m_old, jnp.max(functools.reduce(jnp.maximum, tiles), axis=-1, keepdims=True))
            ps.append(jnp.concatenate([jnp.exp(tl - m_new).astype(BF16) for tl in tiles], axis=1))
            alphas.append(jnp.exp(m_old - m_new))
            m_ref[rows, :] = m_new
        pv = jnp.dot(jnp.concatenate(ps, axis=0), v_ref[0, pl.ds(k0, SEL_CHUNK), :], preferred_element_type=F32)
        acc_ref[...] = jnp.concatenate(alphas, axis=0) * acc_ref[...] + pv

    scores(0, sa_ref)

    def body(cc, carry):
        c = 2 * cc
        scores(c + 1, sb_ref)
        accumulate(c, sa_ref)
        scores(jnp.minimum(c + 2, n_chunks - 1), sa_ref)
        accumulate(c + 1, sb_ref)
        return carry

    lax.fori_loop(0, n_chunks // 2, body, 0)

    @pl.when(n_chunks % 2 == 1)
    def _():
        accumulate(n_chunks - 1, sa_ref)

    acc = acc_ref[...]
    rolled = pltpu.roll(acc, HEAD_DIM, 1)
    for w in range(SEL_QTILES):
        for t2 in range(hpc // 2):
            ev = slice(w * rpt + 2 * t2 * QBLK, w * rpt + (2 * t2 + 1) * QBLK)
            od = slice(w * rpt + (2 * t2 + 1) * QBLK, w * rpt + (2 * t2 + 2) * QBLK)
            even = acc[ev] * (1.0 / rolled[ev])
            odd = rolled[od] * (1.0 / acc[od])
            o_ref[0, w * QBLK:(w + 1) * QBLK, t2 * LANES:(t2 + 1) * LANES] = jnp.where(lo, even, odd)


def sel_attention(q, kx, vx, pen, bias_tiles):
    b, s, _ = q.shape
    hpc = N_HEADS // C_KV_HEADS
    gw = hpc * HEAD_DIM
    nbt = bias_tiles.shape[0] - 1
    tq = SEL_QTILES * QBLK
    rows = SEL_QTILES * hpc * QBLK
    assert s % SEL_CHUNK == 0 and s % tq == 0
    return pl.pallas_call(
        functools.partial(_sel_attn_kernel, nbt=nbt),
        grid=(b, C_KV_HEADS, s // tq),
        in_specs=[pl.BlockSpec((1, tq, gw), lambda bb, g, i: (bb, i, g)),
                  pl.BlockSpec((1, tq, LANES), lambda bb, g, i: (bb, i, g)),
                  pl.BlockSpec((1, s, LANES), lambda bb, g, i: (bb, 0, g)),
                  pl.BlockSpec((1, s, LANES), lambda bb, g, i: (bb, 0, g)),
                  pl.BlockSpec((nbt + 1, hpc, QBLK, QBLK), lambda bb, g, i: (0, g, 0, 0))],
        out_specs=pl.BlockSpec((1, tq, gw), lambda bb, g, i: (bb, i, g)),
        out_shape=jax.ShapeDtypeStruct((b, s, HQ), F32),
        scratch_shapes=[pltpu.VMEM((rows, LANES), BF16),
                        pltpu.VMEM((rows, LANES), F32),
                        pltpu.VMEM((rows, LANES), F32),
                        pltpu.VMEM((rows, SEL_CHUNK), F32),
                        pltpu.VMEM((rows, SEL_CHUNK), F32)],
        compiler_params=_cparams("parallel", "parallel", "arbitrary"),
        name="nsa_sel_attention",
    )(q, pen, kx, vx, bias_tiles)


def _sel_bias_idx(s):
    nqt = s // QBLK
    far = -(-(int(np.argmax(_t5_bucket_np(np.arange(4 * MAX_DISTANCE)) == NUM_BUCKETS - 1)) + QBLK) // QBLK)
    nbt = min(nqt, far + 1)
    d0 = np.arange(nbt)[:, None, None] * QBLK
    dist = d0 + np.arange(QBLK)[None, :, None] - np.arange(QBLK)[None, None, :]
    idx = np.where(dist >= 0, _t5_bucket_np(dist), -1).astype(np.int32)
    return np.concatenate([idx, np.full((1, QBLK, QBLK), -1, np.int32)])


def _ffn_kernel(h_ref, g2_ref, wu_ref, cw_ref, cb_ref, wd_ref, g3_ref,
                o_ref, xn_ref, acc_ref, u_ref, a_ref, carry_ref, *, tiles_per_seq, nj, lag):
    i = pl.program_id(0)
    ns, ts, tn = xn_ref.shape[0], xn_ref.shape[1], a_ref.shape[3]
    cols = lambda c: slice(c * tn, (c + 1) * tn)

    @pl.when(i % tiles_per_seq == 0)
    def _():
        carry_ref[...] = jnp.zeros(carry_ref.shape, F32)

    top = 16
    rowt = lax.broadcasted_iota(jnp.int32, (top, tn), 0)

    def up(s, j):
        xn = xn_ref[s]
        u_ref[s, j % 2, 0] = jnp.dot(xn, wu_ref[:, cols(j)], preferred_element_type=F32)
        u_ref[s, j % 2, 1] = jnp.dot(xn, wu_ref[:, cols(nj + j)], preferred_element_type=F32)

    def conv(u, s1, s2, c):
        cw = cw_ref[:, cols(c)]
        return ((cb_ref[:, cols(c)] + u * cw[2:3]) + s2 * cw[0:1]) + s1 * cw[1:2]

    def conv_body(u, c):
        return conv(u, pltpu.roll(u, 1, 0), pltpu.roll(u, 2, 0), c)

    def conv_top(uc_ref, c):
        prev = carry_ref[c]
        p1, p2 = prev[7:8], prev[6:7]
        u = uc_ref[0:top]
        s1 = jnp.where(rowt == 0, p1, pltpu.roll(u, 1, 0))
        s2 = jnp.where(rowt == 0, p2, jnp.where(rowt == 1, p1, pltpu.roll(u, 2, 0)))
        carry_ref[c] = uc_ref[ts - 8:ts]
        return conv(u, s1, s2, c)

    def gated(cg, cv):
        return (jax.nn.gelu(cg, approximate=True) * cv).astype(BF16)

    def act(s, j):
        ur, ar = u_ref.at[s, j % 2], a_ref.at[s, j % 2]
        ar[...] = gated(conv_body(ur[0], j), conv_body(ur[1], nj + j))
        ar[0:top] = gated(conv_top(ur.at[0], j), conv_top(ur.at[1], nj + j))

    def down(s, j):
        acc_ref[s] += jnp.dot(a_ref[s, j % 2], wd_ref[j * tn:(j + 1) * tn, :], preferred_element_type=F32)

    def iteration(s, k):
        rows = slice(s * ts, (s + 1) * ts)
        if k == -1:
            xn_ref[s] = _rms(h_ref[rows, :], g2_ref[...]).astype(xn_ref.dtype)
            acc_ref[s] = jnp.zeros(acc_ref.shape[1:], F32)
        if 0 <= k + 1 < nj:
            up(s, k + 1)
        if 0 <= k < nj:
            act(s, k)
        if 0 <= k - 1 < nj:
            down(s, k - 1)
        if k == nj:
            o_ref[rows, :] = h_ref[rows, :] + _rms(acc_ref[s], g3_ref[...])

    for slot in range(-1, nj + 1 + (ns - 1) * lag):
        for s in range(ns):
            k = slot - s * lag
            if -1 <= k <= nj:
                iteration(s, k)


def conv_ffn(h, g2, w_up, conv_w, conv_b, w_down, g3, seq, tm, tn, ns=FFN_SUBTILES, lag=FFN_LAG):
    t, d = h.shape
    dff = w_down.shape[0]
    nj = dff // tn
    ts = tm // ns
    assert nj * tn == dff and ts * ns == tm and ts % 16 == 0
    wu, cw, cb, wd = w_up, conv_w, conv_b.reshape(1, -1), w_down
    resident = lambda a: pl.BlockSpec(a.shape, (lambda nd: (lambda i: (0,) * nd))(a.ndim))
    return pl.pallas_call(
        functools.partial(_ffn_kernel, tiles_per_seq=seq // tm, nj=nj, lag=lag),
        grid=(t // tm,),
        in_specs=[pl.BlockSpec((tm, d), lambda i: (i, 0)),
                  pl.BlockSpec((1, d), lambda i: (0, 0)),
                  resident(wu), resident(cw), resident(cb), resident(wd),
                  pl.BlockSpec((1, d), lambda i: (0, 0))],
        out_specs=pl.BlockSpec((tm, d), lambda i: (i, 0)),
        out_shape=jax.ShapeDtypeStruct((t, d), F32),
        scratch_shapes=[pltpu.VMEM((ns, ts, d), BF16), pltpu.VMEM((ns, ts, d), F32),
                        pltpu.VMEM((ns, 2, 2, ts, tn), F32), pltpu.VMEM((ns, 2, ts, tn), BF16),
                        pltpu.VMEM((2 * nj, 8, tn), F32)],
        compiler_params=_cparams("arbitrary"),
        name="conv_ffn",
    )(h, g2.reshape(1, d), wu, cw, cb, wd, g3.reshape(1, d))


def _row_tile(t):
    for tm in (1024, 512, 256, 128):
        if t % tm == 0:
            return tm
    raise ValueError(f"token count {t} is not a multiple of 128")


def mixer_a(h, gains, w_in, w_o, rel_table, bsz, seq):
    t, d = h.shape
    tm = _row_tile(t)
    n_dil = len(DIL_CONFIGS)
    a_in = w_in.shape[1]
    col_scale = np.ones((a_in,), np.float32).reshape(n_dil, 3, HQ)
    col_scale[:, 0] = ATTN_SCALE
    w = (w_in * col_scale.reshape(1, a_in)).astype(BF16)
    idx = np.concatenate([_banded_idx(1, window // dil, dil) for window, dil in DIL_CONFIGS])
    bias = bias_expand(rel_table, idx)
    outs, lses = [], []
    dils = [dil for _, dil in DIL_CONFIGS]
    blocks_per_seq = seq // QBLK
    xns = norm_permute(h, gains[0], dils, max([tm] + [QBLK * dil for dil in dils]))
    for gi, (window, dil) in enumerate(DIL_CONFIGS):
        assert window // dil <= QBLK and seq % (QBLK * dil) == 0
        qkv = matmul_resident(xns[gi], w[:, gi * 3 * HQ:(gi + 1) * 3 * HQ], BF16, tm)
        qkv = qkv.reshape(1, t, 3 * HQ)

        def rmap(part, dil=dil):
            return lambda n, i: (0, (n // dil) * blocks_per_seq + i * dil + n % dil, part)

        o, lse = banded_attention(
            qkv, qkv, qkv, bias[2 * gi:2 * gi + 2], n_rows=bsz * dil, n_tiles=seq // (QBLK * dil), tq=QBLK, n_prev=1, kw=HQ,
            q_map=rmap(0), k_map=rmap(1), v_map=rmap(2), o_map=rmap(0),
            out_shape=(1, t, HQ), lse_shape=(1, t, LANES), name=f"dilated_attention_{dil}")
        outs.append(o.reshape(t, HQ))
        lses.append(lse.reshape(t, LANES))
    e = jnp.asarray(_head_expand_matrix(), BF16)
    return combine_proj_a(outs, lses, dils, e, w_o.astype(BF16), gains[1], h, min(tm, 512))


def mixer_b(h, gains, w_in, sinks, w_o, rel_table, bsz, seq):
    t, d = h.shape
    tm = _row_tile(t)
    hk = B_KV_HEADS * HEAD_DIM
    n_in = w_in.shape[1]
    assert n_in == HQ + 2 * hk
    w = jnp.concatenate([w_in[:, :HQ] * ATTN_SCALE, _dup_groups(w_in[:, HQ:HQ + hk]),
                         _dup_groups(w_in[:, HQ + hk:])], axis=1).astype(BF16)
    kw = 2 * hk
    qkv = norm_matmul(h, gains[0], w, BF16, tm, kw).reshape(bsz, seq, HQ + 2 * kw)
    bias = bias_expand(rel_table, _banded_idx(1, B_WINDOW - 1, 1))
    sink_rows = sinks.astype(F32)
    ident = lambda n, i: (n, i, 0)
    col = lambda c: (lambda n, i: (n, i, c))
    (o,) = banded_attention(qkv, qkv, qkv, bias, n_rows=bsz, n_tiles=seq // QBLK, tq=QBLK, n_prev=1,
                            kw=kw, q_map=ident, k_map=col(HQ // kw), v_map=col(HQ // kw + 1), o_map=ident,
                            out_shape=(bsz, seq, HQ), sinks=sink_rows, stack=8, name="sink_window_attention")
    return _proj_call(_proj_kernel, "sink_proj", tm, [o.reshape(t, HQ)], [], w_o.astype(BF16), gains[1], h)


def mixer_c(h, gains, w_in, cmp_pos, cmp_w1, cmp_w2, w_o, rel_table, bsz, seq):
    t, d = h.shape
    tm = _row_tile(t)
    g = C_KV_HEADS
    hk = g * HEAD_DIM
    wkv = [w_in[:, HQ + i * hk:HQ + (i + 1) * hk] for i in range(6)]
    n_gate = w_in.shape[1] - HQ - 6 * hk
    assert n_gate == 3 * N_HEADS and hk == LANES
    wq = (w_in[:, :HQ] * ATTN_SCALE).astype(BF16)
    spread = lambda w: jnp.concatenate([w[:, :HEAD_DIM], jnp.zeros_like(w[:, :HEAD_DIM]),
                                        w[:, HEAD_DIM:], jnp.zeros_like(w[:, :HEAD_DIM])], axis=1)
    wb = jnp.concatenate([spread(wkv[2]), spread(wkv[3]), _dup_groups(wkv[4]), _dup_groups(wkv[5])],
                         axis=1).astype(BF16)
    wf = jnp.concatenate([wkv[0], wkv[1], jnp.pad(w_in[:, HQ + 6 * hk:], ((0, 0), (0, LANES - n_gate)))],
                         axis=1).astype(BF16)
    kw = 2 * LANES
    q = norm_matmul(h, gains[0], wq, BF16, tm, HQ).reshape(bsz, seq, HQ)
    kvb = norm_matmul(h, gains[0], wb, BF16, tm, kw).reshape(bsz, seq, 4 * kw)
    r = norm_matmul(h, gains[0], wf, F32, tm, 3 * LANES)
    kv = [r[:, i * hk:(i + 1) * hk].reshape(bsz, seq, hk) for i in range(2)]
    gates_raw = r[:, 2 * hk:2 * hk + LANES]

    ncr = seq // CMP_STRIDE
    half = CMP_BLOCK // 2
    assert half == CMP_STRIDE
    chunks = jnp.stack([kv[0], kv[1]]).reshape(2, bsz, ncr, half, g, HEAD_DIM)
    chunks = chunks.transpose(0, 1, 4, 2, 3, 5).reshape(2, bsz, g, ncr, half * HEAD_DIM)
    pos = cmp_pos.reshape(2, 2, half * HEAD_DIM)
    w1 = cmp_w1.reshape(2, 2, half * HEAD_DIM, -1).astype(BF16)
    cmp = compress(chunks, pos, w1, cmp_w2.astype(BF16))
    cmp = cmp.transpose(0, 1, 3, 2, 4).reshape(2, bsz, ncr, hk).astype(BF16)
    kcd, vcd = _dup_groups(cmp[0]), _dup_groups(cmp[1])

    ns = seq // SEL_BLOCK
    assert ns <= HEAD_DIM
    k_sel = min(N_SELECT, ns)
    starts = np.arange(ncr) * CMP_STRIDE
    blk = np.arange(HEAD_DIM)
    ov = ((starts[None, :] < (blk[:, None] + 1) * SEL_BLOCK)
          & (starts[None, :] + CMP_BLOCK > blk[:, None] * SEL_BLOCK)
          & (blk[:, None] < ns) & (starts[None, :] + CMP_BLOCK <= seq))
    o_c, pen = cmp_attention(q, kcd, vcd, jnp.asarray(ov.astype(np.float32), BF16), ns, k_sel)

    sel_bias = bias_expand(rel_table, _sel_bias_idx(seq))
    lane = np.arange(kw)[None, :]
    key_blk = np.arange(seq)[:, None] // SEL_BLOCK
    onehot = np.where((lane % LANES >= HEAD_DIM) & (lane % HEAD_DIM == key_blk), -NEG_INF, 0.0)
    ones = np.broadcast_to(np.where(lane % LANES >= HEAD_DIM, 1.0, 0.0), (seq, kw))
    kx = kvb[..., :kw] + jnp.asarray(onehot, BF16)[None]
    vx = kvb[..., kw:2 * kw] + jnp.asarray(ones, BF16)[None]
    o_s = sel_attention(q, kx, vx, pen, sel_bias)

    n_prev = -(-(C_WINDOW - 1) // QBLK)
    tqw = n_prev * QBLK
    wbias = bias_expand(rel_table, _banded_idx(n_prev, C_WINDOW - 1, 1))
    ident = lambda n, i: (n, i, 0)
    col = lambda c: (lambda n, i: (n, i, c))
    (o_w,) = banded_attention(q, kvb, kvb, wbias,
                              n_rows=bsz, n_tiles=seq // tqw, tq=tqw, n_prev=n_prev, kw=kw,
                              q_map=ident, k_map=col(2), v_map=col(3), o_map=ident,
                              out_shape=(bsz, seq, HQ), stack=8, name="nsa_window_attention")

    e = jnp.asarray(_head_expand_matrix(), BF16)
    return _proj_call(_proj_c_kernel, "nsa_gate_proj", min(tm, 512),
                      [o_c.reshape(t, HQ), o_s.reshape(t, HQ), o_w.reshape(t, HQ), gates_raw], [e],
                      w_o.astype(BF16), gains[1], h)


def kernel(x, rel_table, norm_gains, a_w_in, a_w_o, b_w_in, b_sinks, b_w_o, c_w_in, c_cmp_pos, c_cmp_w1,
           c_cmp_w2, c_w_o, ffn_w_up, ffn_conv_w, ffn_conv_b, ffn_w_down):
    bsz, seq, d = x.shape
    depth = norm_gains.shape[0]
    h = x.reshape(bsz * seq, d)
    tm = _row_tile(seq)
    for i in range(depth):
        kind, j = i % 3, i // 3
        g = norm_gains[i]
        if kind == 0:
            h = mixer_a(h, g, a_w_in[j], a_w_o[j], rel_table, bsz, seq)
        elif kind == 1:
            h = mixer_b(h, g, b_w_in[j], b_sinks[j], b_w_o[j], rel_table, bsz, seq)
        else:
            h = mixer_c(h, g, c_w_in[j], c_cmp_pos[j], c_cmp_w1[j], c_cmp_w2[j], c_w_o[j], rel_table, bsz, seq)
        h = conv_ffn(h, g[2], ffn_w_up[i].astype(BF16), ffn_conv_w[i], ffn_conv_b[i],
                     ffn_w_down[i].astype(BF16), g[3], seq, min(tm, FFN_ROWS), FFN_COLS)
    return h.reshape(bsz, seq, d)
```

```python
import functools
import math

import numpy as np
import jax
import jax.numpy as jnp
from jax import lax
from jax.experimental import pallas as pl
from jax.experimental.pallas import tpu as pltpu

F32 = jnp.float32
BF16 = jnp.bfloat16

N_HEADS = 16
HEAD_DIM = 64
HQ = N_HEADS * HEAD_DIM
LANES = 128
ATTN_SCALE = HEAD_DIM ** -0.5
NUM_BUCKETS = 32
MAX_DISTANCE = 2048
RMS_EPS = 1e-6
NEG_INF = -1e30
FORCE_SCORE = 1e9
DIL_CONFIGS = ((128, 1), (512, 4), (2048, 16))
B_KV_HEADS = 2
B_WINDOW = 128
C_KV_HEADS = 2
CMP_BLOCK = 32
CMP_STRIDE = 16
SEL_BLOCK = 64
N_SELECT = 16
C_WINDOW = 512
CONV_WIDTH = 3
QBLK = 128
SEL_CHUNK = 512
SEL_QTILES = 2
FFN_COLS = 256
FFN_ROWS = 512
FFN_SUBTILES = 2
FFN_LAG = 0
VMEM_LIMIT = 56 * 1024 * 1024


def _cparams(*sem):
    return pltpu.CompilerParams(dimension_semantics=sem, vmem_limit_bytes=VMEM_LIMIT)


def _t5_bucket_np(dist):
    max_exact = NUM_BUCKETS // 2
    d = np.maximum(dist, 0)
    df = np.maximum(d, 1).astype(np.float64)
    large = max_exact + np.floor(np.log(df / max_exact) / math.log(MAX_DISTANCE / max_exact)
                                 * (NUM_BUCKETS - max_exact) + 1e-9).astype(np.int64)
    large = np.minimum(large, NUM_BUCKETS - 1)
    return np.where(d < max_exact, d, large).astype(np.int32)


def _rms(x, g):
    ms = jnp.mean(x * x, axis=-1, keepdims=True)
    return (x * lax.rsqrt(ms + RMS_EPS)) * g


def _split3(w):
    hi = w.astype(BF16)
    r1 = w - hi.astype(F32)
    mid = r1.astype(BF16)
    lo = (r1 - mid.astype(F32)).astype(BF16)
    return hi, mid, lo


def _expand_heads(w, e3):
    lane = lax.broadcasted_iota(jnp.int32, w.shape, 1)
    r1 = w - w.astype(BF16).astype(F32)
    r2 = r1 - r1.astype(BF16).astype(F32)
    packed = jnp.where(lane < N_HEADS, w,
                       jnp.where(lane < 2 * N_HEADS, pltpu.roll(r1, N_HEADS, 1), pltpu.roll(r2, 2 * N_HEADS, 1)))
    return jnp.dot(packed.astype(BF16), e3, preferred_element_type=F32)


def _dot_nt(a, b):
    return lax.dot_general(a, b, (((1,), (1,)), ((), ())), preferred_element_type=F32)


def _stack_heads(q_tile, heads):
    lane = lax.broadcasted_iota(jnp.int32, (QBLK, LANES), 1)
    lo = lane < HEAD_DIM
    pieces = []
    for h in heads:
        q2 = q_tile(h // 2)
        keep = lo if h % 2 == 0 else jnp.logical_not(lo)
        pieces.append(jnp.where(keep, q2, jnp.zeros_like(q2)))
    return jnp.concatenate(pieces, axis=0)


def _merge_pairs(o, n_heads):
    lane = lax.broadcasted_iota(jnp.int32, (QBLK, LANES), 1)
    lo = lane < HEAD_DIM
    out = []
    for t in range(0, n_heads, 2):
        out.append(jnp.where(lo, o[t * QBLK:(t + 1) * QBLK], o[(t + 1) * QBLK:(t + 2) * QBLK]))
    return out


def _bias_kernel(tab_ref, rng_ref, idx_ref, o_ref):
    n = pl.program_id(0)
    idx = idx_ref[0]
    o_ref[...] = jnp.full(o_ref.shape, NEG_INF, F32)

    def body(b, carry):
        hit = idx == b
        for h in range(N_HEADS):
            o_ref[0, h] = jnp.where(hit, tab_ref[b, h], o_ref[0, h])
        return carry

    lax.fori_loop(rng_ref[n, 0], rng_ref[n, 1] + 1, body, 0)


def bias_expand(table, idx):
    n, r, c = idx.shape
    flat = idx.reshape(n, -1)
    lo = np.where(flat >= 0, flat, NUM_BUCKETS).min(axis=1)
    rng = np.stack([lo, flat.max(axis=1)], axis=1).astype(np.int32)
    return pl.pallas_call(
        _bias_kernel,
        grid=(n,),
        in_specs=[pl.BlockSpec(memory_space=pltpu.SMEM),
                  pl.BlockSpec(memory_space=pltpu.SMEM),
                  pl.BlockSpec((1, r, c), lambda i: (i, 0, 0))],
        out_specs=pl.BlockSpec((1, N_HEADS, r, c), lambda i: (i, 0, 0, 0)),
        out_shape=jax.ShapeDtypeStruct((n, N_HEADS, r, c), F32),
        compiler_params=_cparams("parallel"),
        name="bias_expand",
    )(table, jnp.asarray(rng), jnp.asarray(idx))


def _banded_idx(n_prev, max_dist, stride):
    kb = (n_prev + 1) * QBLK
    dist = np.arange(QBLK)[:, None] + n_prev * QBLK - np.arange(kb)[None, :]
    valid = (dist >= 0) & (dist <= max_dist)
    idx = np.where(valid, _t5_bucket_np(dist * stride), -1).astype(np.int32)
    if n_prev > 1:
        return idx[None]
    first = np.where(np.arange(kb)[None, :] < n_prev * QBLK, -1, idx).astype(np.int32)
    return np.stack([idx, first])


def _norm_mm_kernel(x_ref, g_ref, w_ref, o_ref, xn_ref):
    @pl.when(pl.program_id(1) == 0)
    def _():
        xn_ref[...] = _rms(x_ref[...], g_ref[...]).astype(xn_ref.dtype)

    o_ref[...] = jnp.dot(xn_ref[...], w_ref[...], preferred_element_type=F32).astype(o_ref.dtype)


def norm_matmul(x, gain, w, out_dtype, tm, tn):
    t, d = x.shape
    n = w.shape[1]
    scratch = [pltpu.VMEM((tm, d), BF16)]
    return pl.pallas_call(
        _norm_mm_kernel,
        grid=(t // tm, n // tn),
        in_specs=[pl.BlockSpec((tm, d), lambda i, j: (i, 0)),
                  pl.BlockSpec((1, d), lambda i, j: (0, 0)),
                  pl.BlockSpec((d, tn), lambda i, j: (0, j))],
        out_specs=pl.BlockSpec((tm, tn), lambda i, j: (i, j)),
        out_shape=jax.ShapeDtypeStruct((t, n), out_dtype),
        scratch_shapes=scratch,
        compiler_params=_cparams("parallel", "arbitrary"),
        name="norm_matmul",
    )(x, gain.reshape(1, d), w)


def _norm_perm_kernel(x_ref, g_ref, *refs, dils):
    o_refs, xs_ref = refs[:len(dils)], refs[len(dils)]
    xn = _rms(x_ref[...], g_ref[...])
    nc = xs_ref.shape[0]
    for c in range(nc):
        xs_ref[c] = xn[:, c * LANES:(c + 1) * LANES]
    for o_ref, dil in zip(o_refs, dils):
        if dil == 1:
            o_ref[...] = xn.astype(o_ref.dtype)
            continue
        span = QBLK * dil
        for u in range(x_ref.shape[0] // span):
            for r in range(dil):
                rows = jnp.concatenate(
                    [xs_ref[c, pl.ds(u * span + r, QBLK, stride=dil), :] for c in range(nc)], axis=1)
                dst = u * span + r * QBLK
                o_ref[dst:dst + QBLK, :] = rows.astype(o_ref.dtype)


def norm_permute(x, gain, dils, tm):
    t, d = x.shape
    assert all(tm % (QBLK * dil) == 0 for dil in dils)
    return pl.pallas_call(
        functools.partial(_norm_perm_kernel, dils=tuple(dils)),
        grid=(t // tm,),
        in_specs=[pl.BlockSpec((tm, d), lambda i: (i, 0)),
                  pl.BlockSpec((1, d), lambda i: (0, 0))],
        out_specs=[pl.BlockSpec((tm, d), lambda i: (i, 0)) for _ in dils],
        out_shape=[jax.ShapeDtypeStruct((t, d), BF16) for _ in dils],
        scratch_shapes=[pltpu.VMEM((d // LANES, tm, LANES), F32)],
        compiler_params=_cparams("parallel"),
        name="norm_permute",
    )(x, gain.reshape(1, d))


def _mm_kernel(x_ref, w_ref, o_ref):
    o_ref[...] = jnp.dot(x_ref[...], w_ref[...], preferred_element_type=F32).astype(o_ref.dtype)


def matmul_resident(x, w, out_dtype, tm):
    t, k = x.shape
    n = w.shape[1]
    return pl.pallas_call(
        _mm_kernel,
        grid=(t // tm,),
        in_specs=[pl.BlockSpec((tm, k), lambda i: (i, 0)),
                  pl.BlockSpec((k, n), lambda i: (0, 0))],
        out_specs=pl.BlockSpec((tm, n), lambda i: (i, 0)),
        out_shape=jax.ShapeDtypeStruct((t, n), out_dtype),
        compiler_params=_cparams("parallel"),
        name="matmul_resident",
    )(x, w)


def _finish_proj(y, w_ref, g_ref, h_ref, o_ref):
    z = jnp.dot(y.astype(BF16), w_ref[...], preferred_element_type=F32)
    o_ref[...] = h_ref[...] + _rms(z, g_ref[...])


def _proj_kernel(y_ref, w_ref, g_ref, h_ref, o_ref):
    _finish_proj(y_ref[...], w_ref, g_ref, h_ref, o_ref)


def _unpermute(src_ref, dst_ref, dil, tm, i):
    span = QBLK * dil
    nc = dst_ref.shape[0]
    if span <= tm:
        for u in range(tm // span):
            for r in range(dil):
                lo = u * span + r * QBLK
                for c in range(nc):
                    dst_ref[c, pl.ds(u * span + r, QBLK, stride=dil), :] = src_ref[lo:lo + QBLK,
                                                                                   c * LANES:(c + 1) * LANES]
    else:
        per = tm // dil
        off = (i % (span // tm)) * per
        for r in range(dil):
            lo = pl.multiple_of(r * QBLK + off, 8)
            for c in range(nc):
                dst_ref[c, pl.ds(r, per, stride=dil), :] = src_ref[pl.ds(lo, per), c * LANES:(c + 1) * LANES]
    return jnp.concatenate([dst_ref[c] for c in range(nc)], axis=1) if nc > 1 else dst_ref[0]


def _proj_a_kernel(*refs, dils, tm):
    n = len(dils)
    o_refs, l_refs = refs[:n], refs[n:2 * n]
    e_ref, w_ref, g_ref, h_ref, out_ref = refs[2 * n:2 * n + 5]
    scratch = refs[2 * n + 5:]
    i = pl.program_id(0)
    outs, lses = [], []
    si = 0
    for gi, dil in enumerate(dils):
        if dil == 1:
            outs.append(o_refs[gi][...])
            lses.append(l_refs[gi][...])
        else:
            outs.append(_unpermute(o_refs[gi], scratch[si], dil, tm, i))
            lses.append(_unpermute(l_refs[gi], scratch[si + 1], dil, tm, i))
            si += 2
    mx = functools.reduce(jnp.maximum, lses)
    es = [jnp.exp(l - mx) for l in lses]
    inv = 1.0 / functools.reduce(lambda a, b: a + b, es)
    e = e_ref[...]
    y = outs[0] * _expand_heads(es[0] * inv, e)
    for gi in range(1, n):
        y = y + outs[gi] * _expand_heads(es[gi] * inv, e)
    _finish_proj(y, w_ref, g_ref, h_ref, out_ref)


def combine_proj_a(outs, lses, dils, e, w, gain, h, tm):
    t, d = h.shape

    def row_spec(width, dil):
        rows = max(tm, QBLK * dil)
        return pl.BlockSpec((rows, width), lambda i, q=rows // tm: (i // q, 0))

    in_specs = [row_spec(HQ, dil) for dil in dils] + [row_spec(LANES, dil) for dil in dils] + [
        pl.BlockSpec(e.shape, lambda i: (0, 0)),
        pl.BlockSpec(w.shape, lambda i: (0, 0)),
        pl.BlockSpec((1, d), lambda i: (0, 0)),
        pl.BlockSpec((tm, d), lambda i: (i, 0))]
    scratch = []
    for dil in dils:
        if dil > 1:
            scratch += [pltpu.VMEM((HQ // LANES, tm, LANES), F32), pltpu.VMEM((1, tm, LANES), F32)]
    return pl.pallas_call(
        functools.partial(_proj_a_kernel, dils=tuple(dils), tm=tm),
        grid=(t // tm,),
        in_specs=in_specs,
        out_specs=pl.BlockSpec((tm, d), lambda i: (i, 0)),
        out_shape=jax.ShapeDtypeStruct((t, d), F32),
        scratch_shapes=scratch,
        compiler_params=_cparams("arbitrary"),
        name="dilated_combine_proj",
    )(*outs, *lses, e, w, gain.reshape(1, d), h)


def _proj_c_kernel(oc_ref, os_ref, ow_ref, gr_ref, e_ref, w_ref, g_ref, h_ref, o_ref):
    sig = jax.nn.sigmoid(gr_ref[...])
    e = e_ref[...]
    gate = lambda i: _expand_heads(sig if i == 0 else pltpu.roll(sig, LANES - i * N_HEADS, 1), e)
    y = gate(0) * oc_ref[...]
    y = y + gate(1) * os_ref[...]
    y = y + gate(2) * ow_ref[...]
    _finish_proj(y, w_ref, g_ref, h_ref, o_ref)


def _proj_call(kernel, name, tm, row_inputs, const_inputs, w, gain, h):
    t, d = h.shape
    k = w.shape[0]
    row_specs = [pl.BlockSpec((tm, a.shape[1]), lambda i: (i, 0)) for a in row_inputs]
    const_specs = [pl.BlockSpec(a.shape, (lambda nd: (lambda i: (0,) * nd))(a.ndim)) for a in const_inputs]
    return pl.pallas_call(
        kernel,
        grid=(t // tm,),
        in_specs=row_specs + const_specs + [
            pl.BlockSpec((k, d), lambda i: (0, 0)),
            pl.BlockSpec((1, d), lambda i: (0, 0)),
            pl.BlockSpec((tm, d), lambda i: (i, 0))],
        out_specs=pl.BlockSpec((tm, d), lambda i: (i, 0)),
        out_shape=jax.ShapeDtypeStruct((t, d), F32),
        compiler_params=_cparams("parallel"),
        name=name,
    )(*row_inputs, *const_inputs, w, gain.reshape(1, d), h)


def _head_expand_matrix():
    e = np.zeros((LANES, HQ), np.float32)
    for term in range(3):
        for h in range(N_HEADS):
            e[term * N_HEADS + h, h * HEAD_DIM:(h + 1) * HEAD_DIM] = 1.0
    return e


def _banded_kernel(*refs, tq, n_prev, n_chunks, stack, with_sinks, with_lse):
    q_ref, kp_ref, kc_ref, vp_ref, vc_ref, bias_ref = refs[:6]
    pos = 6
    sink_ref = None
    if with_sinks:
        sink_ref = refs[pos]
        pos += 1
    o_ref = refs[pos]
    pos += 1
    lse_ref = None
    if with_lse:
        lse_ref = refs[pos]
        pos += 1
    kbuf, vbuf = refs[pos], refs[pos + 1]

    i = pl.program_id(1)
    hpc = N_HEADS // n_chunks
    kb = (n_prev + 1) * QBLK
    kbuf[0:tq] = kp_ref[0]
    kbuf[tq:2 * tq] = kc_ref[0]
    vbuf[0:tq] = vp_ref[0]
    vbuf[tq:2 * tq] = vc_ref[0]
    lane = lax.broadcasted_iota(jnp.int32, (QBLK, LANES), 1)
    head_row = lax.broadcasted_iota(jnp.int32, (stack * QBLK, 1), 0)

    for s in range(tq // QBLK):
        r0 = s * QBLK
        k0 = tq + r0 - n_prev * QBLK
        variants = bias_ref.shape[0] > 1
        if variants:
            first = jnp.where(i == 0, 1, 0)
        else:
            col = lax.broadcasted_iota(jnp.int32, (1, kb), 1) + k0
            kmask = jnp.where(jnp.logical_and(i == 0, col < tq), NEG_INF, 0.0).astype(F32)
        lse_acc = jnp.zeros((QBLK, LANES), F32)
        for h0 in range(0, N_HEADS, stack):
            heads = list(range(h0, h0 + stack))
            c = h0 // hpc
            qst = _stack_heads(lambda p: q_ref[0, r0:r0 + QBLK, p * LANES:(p + 1) * LANES], heads)
            kx = kbuf[k0:k0 + kb, c * LANES:(c + 1) * LANES]
            vx = vbuf[k0:k0 + kb, c * LANES:(c + 1) * LANES]
            sc = _dot_nt(qst, kx)
            if variants:
                sc = sc + bias_ref[first, h0:h0 + stack].reshape(stack * QBLK, kb)
            else:
                sc = sc + bias_ref[0, h0:h0 + stack].reshape(stack * QBLK, kb) + kmask
            m = jnp.max(sc, axis=-1, keepdims=True)
            if with_sinks:
                sk = sink_ref[h0 + stack - 1]
                for t in range(stack - 2, -1, -1):
                    sk = jnp.where(head_row < (t + 1) * QBLK, sink_ref[h0 + t], sk)
                m = jnp.maximum(m, sk)
            p = jnp.exp(sc - m)
            den = jnp.sum(p, axis=-1, keepdims=True)
            norm = den + jnp.exp(sk - m) if with_sinks else den
            o = jnp.dot(p.astype(BF16), vx, preferred_element_type=F32) * (1.0 / norm)
            for t2, blk in enumerate(_merge_pairs(o, stack)):
                pidx = h0 // 2 + t2
                o_ref[0, r0:r0 + QBLK, pidx * LANES:(pidx + 1) * LANES] = blk.astype(o_ref.dtype)
            if with_lse:
                lse = m + jnp.log(den)
                for t, h in enumerate(heads):
                    lse_acc = jnp.where(lane == h, lse[t * QBLK:(t + 1) * QBLK], lse_acc)
        if with_lse:
            lse_ref[0, r0:r0 + QBLK, :] = lse_acc


def banded_attention(q_arr, k_arr, v_arr, bias, *, n_rows, n_tiles, tq, n_prev, kw,
                     q_map, k_map, v_map, o_map, out_shape, lse_shape=None, sinks=None, stack=2, name):
    n_chunks = kw // LANES
    assert stack % 2 == 0 and (N_HEADS // n_chunks) % stack == 0
    kb = (n_prev + 1) * QBLK
    with_sinks = sinks is not None
    with_lse = lse_shape is not None

    def prev(fn):
        return lambda n, i: fn(n, jnp.maximum(i - 1, 0))

    in_specs = [pl.BlockSpec((1, tq, HQ), q_map),
                pl.BlockSpec((1, tq, kw), prev(k_map)),
                pl.BlockSpec((1, tq, kw), k_map),
                pl.BlockSpec((1, tq, kw), prev(v_map)),
                pl.BlockSpec((1, tq, kw), v_map),
                pl.BlockSpec(bias.shape, lambda n, i: (0, 0, 0, 0))]
    assert bias.shape[1:] == (N_HEADS, QBLK, kb) and (bias.shape[0] == 1 or tq == QBLK)
    args = [q_arr, k_arr, k_arr, v_arr, v_arr, bias]
    if with_sinks:
        in_specs.append(pl.BlockSpec(memory_space=pltpu.SMEM))
        args.append(sinks)
    out_specs = [pl.BlockSpec((1, tq, HQ), o_map)]
    out_shapes = [jax.ShapeDtypeStruct(out_shape, F32)]
    if with_lse:
        out_specs.append(pl.BlockSpec((1, tq, LANES), o_map))
        out_shapes.append(jax.ShapeDtypeStruct(lse_shape, F32))
    res = pl.pallas_call(
        functools.partial(_banded_kernel, tq=tq, n_prev=n_prev, n_chunks=n_chunks, stack=stack,
                          with_sinks=with_sinks, with_lse=with_lse),
        grid=(n_rows, n_tiles),
        in_specs=in_specs,
        out_specs=out_specs,
        out_shape=out_shapes,
        scratch_shapes=[pltpu.VMEM((2 * tq, kw), BF16), pltpu.VMEM((2 * tq, kw), BF16)],
        compiler_params=_cparams("parallel", "arbitrary"),
        name=name,
    )(*args)
    return res


def _dup_groups(x):
    g0, g1 = x[..., :HEAD_DIM], x[..., HEAD_DIM:]
    return jnp.concatenate([g0, g0, g1, g1], axis=-1)


def _compress_kernel(ch_ref, pos_ref, w1_ref, w2_ref, o_ref):
    ch = ch_ref[0, 0, 0]
    rows = ch.shape[0]
    posv = pos_ref[0]
    a = jnp.dot((ch + posv[0:1]).astype(BF16), w1_ref[0, 0], preferred_element_type=F32)
    b = jnp.dot((ch + posv[1:2]).astype(BF16), w1_ref[0, 1], preferred_element_type=F32)
    hid = a + pltpu.roll(b, rows - 1, 0)
    act = jax.nn.gelu(hid, approximate=True)
    o_ref[0, 0, 0] = jnp.dot(act.astype(BF16), w2_ref[0], preferred_element_type=F32)


def compress(chunks, pos, w1, w2):
    _, b, g, rows, width = chunks.shape
    hid = w1.shape[-1]
    return pl.pallas_call(
        _compress_kernel,
        grid=(2, b, g),
        in_specs=[pl.BlockSpec((1, 1, 1, rows, width), lambda i, bb, gg: (i, bb, gg, 0, 0)),
                  pl.BlockSpec((1, 2, width), lambda i, bb, gg: (i, 0, 0)),
                  pl.BlockSpec((1, 2, width, hid), lambda i, bb, gg: (i, 0, 0, 0)),
                  pl.BlockSpec((1, hid, HEAD_DIM), lambda i, bb, gg: (i, 0, 0))],
        out_specs=pl.BlockSpec((1, 1, 1, rows, HEAD_DIM), lambda i, bb, gg: (i, bb, gg, 0, 0)),
        out_shape=jax.ShapeDtypeStruct((2, b, g, rows, HEAD_DIM), F32),
        compiler_params=_cparams("parallel", "parallel", "parallel"),
        name="nsa_compress",
    )(chunks, pos, w1, w2)


def _cmp_attn_kernel(q_ref, kc_ref, vc_ref, ov_ref, place_ref, o_ref, sel_ref, *, n_sel_blocks, k_sel):
    qi = pl.program_id(1)
    ncr = kc_ref.shape[1]
    hpc = N_HEADS // C_KV_HEADS
    qpos = qi * QBLK + lax.broadcasted_iota(jnp.int32, (QBLK, 1), 0)
    cidx = lax.broadcasted_iota(jnp.int32, (1, ncr), 1)
    valid = (cidx * CMP_STRIDE + (CMP_BLOCK - 1)) <= qpos
    maskc = jnp.where(valid, 0.0, NEG_INF).astype(F32)
    anyv = (qpos >= CMP_BLOCK - 1).astype(F32)
    blk_id = lax.broadcasted_iota(jnp.int32, (HEAD_DIM, QBLK), 0)
    cur = (qi * QBLK + lax.broadcasted_iota(jnp.int32, (HEAD_DIM, QBLK), 1)) // SEL_BLOCK
    forced = jnp.logical_or(jnp.logical_or(blk_id == 0, blk_id == cur), blk_id == cur - 1)
    allowed = blk_id <= cur
    for g in range(C_KV_HEADS):
        heads = list(range(g * hpc, (g + 1) * hpc))
        qst = _stack_heads(lambda p: q_ref[0, :, p * LANES:(p + 1) * LANES], heads)
        sc = _dot_nt(qst, kc_ref[0, :, g * LANES:(g + 1) * LANES])
        sc3 = sc.reshape(hpc, QBLK, ncr) + maskc[None]
        m = jnp.max(sc3, axis=-1, keepdims=True)
        e = jnp.exp(sc3 - m)
        ssum = jnp.sum(e, axis=-1, keepdims=True)
        p = (e * (1.0 / ssum)) * anyv[None]
        o = jnp.dot(p.reshape(hpc * QBLK, ncr).astype(BF16), vc_ref[0, :, g * LANES:(g + 1) * LANES],
                    preferred_element_type=F32)
        for t2, blk in enumerate(_merge_pairs(o, hpc)):
            pidx = (g * hpc) // 2 + t2
            o_ref[0, :, pidx * LANES:(pidx + 1) * LANES] = blk
        hi, mid, lo = _split3(jnp.sum(p, axis=0))
        ovt = ov_ref[...]
        imp = (_dot_nt(ovt, hi) + _dot_nt(ovt, mid)) + _dot_nt(ovt, lo)
        score = jnp.where(forced, FORCE_SCORE, jnp.where(allowed, imp, NEG_INF))
        bits = pltpu.bitcast(score, jnp.int32)
        key = jnp.where(bits < 0, bits ^ jnp.int32(0x7FFFFFFF), bits)
        key_m1 = key - 1
        rank = jnp.zeros((HEAD_DIM, QBLK), jnp.int32)
        for i in range(n_sel_blocks):
            thr = jnp.where(blk_id > i, key_m1, key)
            rank = rank + jnp.where(key[i:i + 1, :] > thr, 1, 0)
        keep = jnp.logical_and(rank < k_sel, blk_id < n_sel_blocks)
        pen_t = jnp.where(keep, 0.0, -1.0).astype(BF16)
        pen = lax.dot_general(pen_t, place_ref[...], (((0,), (0,)), ((), ())), preferred_element_type=F32)
        sel_ref[0, :, g * LANES:(g + 1) * LANES] = pen.astype(sel_ref.dtype)


def cmp_attention(q, kcd, vcd, ov, n_sel_blocks, k_sel):
    b, s, _ = q.shape
    ncr = kcd.shape[1]
    place = np.zeros((HEAD_DIM, LANES), np.float32)
    place[np.arange(HEAD_DIM), HEAD_DIM + np.arange(HEAD_DIM)] = 1.0
    return pl.pallas_call(
        functools.partial(_cmp_attn_kernel, n_sel_blocks=n_sel_blocks, k_sel=k_sel),
        grid=(b, s // QBLK),
        in_specs=[pl.BlockSpec((1, QBLK, HQ), lambda bb, i: (bb, i, 0)),
                  pl.BlockSpec((1, ncr, 2 * LANES), lambda bb, i: (bb, 0, 0)),
                  pl.BlockSpec((1, ncr, 2 * LANES), lambda bb, i: (bb, 0, 0)),
                  pl.BlockSpec((HEAD_DIM, ncr), lambda bb, i: (0, 0)),
                  pl.BlockSpec((HEAD_DIM, LANES), lambda bb, i: (0, 0))],
        out_specs=[pl.BlockSpec((1, QBLK, HQ), lambda bb, i: (bb, i, 0)),
                   pl.BlockSpec((1, QBLK, 2 * LANES), lambda bb, i: (bb, i, 0))],
        out_shape=[jax.ShapeDtypeStruct((b, s, HQ), F32),
                   jax.ShapeDtypeStruct((b, s, 2 * LANES), BF16)],
        compiler_params=_cparams("parallel", "parallel"),
        name="nsa_cmp_attention",
    )(q, kcd, vcd, ov, jnp.asarray(place, BF16))


def _sel_attn_kernel(q_ref, pen_ref, k_ref, v_ref, bias_ref, o_ref, qst_ref, m_ref, acc_ref,
                     sa_ref, sb_ref, *, nbt):
    qi0 = pl.program_id(2) * SEL_QTILES
    hpc = N_HEADS // C_KV_HEADS
    nsub = SEL_CHUNK // QBLK
    rpt = hpc * QBLK
    lane = lax.broadcasted_iota(jnp.int32, (QBLK, LANES), 1)
    lo = lane < HEAD_DIM
    for w in range(SEL_QTILES):
        pen = pen_ref[0, w * QBLK:(w + 1) * QBLK, :].astype(F32)
        for t in range(hpc):
            q2 = q_ref[0, w * QBLK:(w + 1) * QBLK, (t // 2) * LANES:(t // 2 + 1) * LANES].astype(F32)
            if t % 2:
                q2 = pltpu.roll(q2, HEAD_DIM, 1)
            qst_ref[w * rpt + t * QBLK:w * rpt + (t + 1) * QBLK, :] = jnp.where(lo, q2, pen).astype(BF16)
    m_ref[...] = jnp.full(m_ref.shape, NEG_INF, F32)
    acc_ref[...] = jnp.zeros(acc_ref.shape, F32)

    n_chunks = (qi0 + SEL_QTILES - 1) // nsub + 1

    def scores(c, s_ref):
        k0 = pl.multiple_of(c * SEL_CHUNK, SEL_CHUNK)
        s_ref[...] = _dot_nt(qst_ref[...], k_ref[0, pl.ds(k0, SEL_CHUNK), :])

    def accumulate(c, s_ref):
        k0 = pl.multiple_of(c * SEL_CHUNK, SEL_CHUNK)
        ps, alphas = [], []
        for w in range(SEL_QTILES):
            rows = slice(w * rpt, (w + 1) * rpt)
            tiles = []
            for u in range(nsub):
                off = qi0 + w - nsub * c - u
                idx = jnp.where(off < 0, nbt, jnp.minimum(off, nbt - 1))
                tiles.append(s_ref[rows, u * QBLK:(u + 1) * QBLK] + bias_ref[idx].reshape(rpt, QBLK))
            m_old = m_ref[rows, :]
            m_new = jnp.maximum(m_old, jnp.max(functools.reduce(jnp.maximum, tiles), axis=-1, keepdims=True))
            ps.append(jnp.concatenate([jnp.exp(tl - m_new).astype(BF16) for tl in tiles], axis=1))
            alphas.append(jnp.exp(m_old - m_new))
            m_ref[rows, :] = m_new
        pv = jnp.dot(jnp.concatenate(ps, axis=0), v_ref[0, pl.ds(k0, SEL_CHUNK), :], preferred_element_type=F32)
        acc_ref[...] = jnp.concatenate(alphas, axis=0) * acc_ref[...] + pv

    scores(0, sa_ref)

    def body(cc, carry):
        c = 2 * cc
        scores(c + 1, sb_ref)
        accumulate(c, sa_ref)
        scores(jnp.minimum(c + 2, n_chunks - 1), sa_ref)
        accumulate(c + 1, sb_ref)
        return carry

    lax.fori_loop(0, n_chunks // 2, body, 0)

    @pl.when(n_chunks % 2 == 1)
    def _():
        accumulate(n_chunks - 1, sa_ref)

    acc = acc_ref[...]
    rolled = pltpu.roll(acc, HEAD_DIM, 1)
    for w in range(SEL_QTILES):
        for t2 in range(hpc // 2):
            ev = slice(w * rpt + 2 * t2 * QBLK, w * rpt + (2 * t2 + 1) * QBLK)
            od = slice(w * rpt + (2 * t2 + 1) * QBLK, w * rpt + (2 * t2 + 2) * QBLK)
            even = acc[ev] * (1.0 / rolled[ev])
            odd = rolled[od] * (1.0 / acc[od])
            o_ref[0, w * QBLK:(w + 1) * QBLK, t2 * LANES:(t2 + 1) * LANES] = jnp.where(lo, even, odd)


def sel_attention(q, kx, vx, pen, bias_tiles):
    b, s, _ = q.shape
    hpc = N_HEADS // C_KV_HEADS
    gw = hpc * HEAD_DIM
    nbt = bias_tiles.shape[0] - 1
    tq = SEL_QTILES * QBLK
    rows = SEL_QTILES * hpc * QBLK
    assert s % SEL_CHUNK == 0 and s % tq == 0
    return pl.pallas_call(
        functools.partial(_sel_attn_kernel, nbt=nbt),
        grid=(b, C_KV_HEADS, s // tq),
        in_specs=[pl.BlockSpec((1, tq, gw), lambda bb, g, i: (bb, i, g)),
                  pl.BlockSpec((1, tq, LANES), lambda bb, g, i: (bb, i, g)),
                  pl.BlockSpec((1, s, LANES), lambda bb, g, i: (bb, 0, g)),
                  pl.BlockSpec((1, s, LANES), lambda bb, g, i: (bb, 0, g)),
                  pl.BlockSpec((nbt + 1, hpc, QBLK, QBLK), lambda bb, g, i: (0, g, 0, 0))],
        out_specs=pl.BlockSpec((1, tq, gw), lambda bb, g, i: (bb, i, g)),
        out_shape=jax.ShapeDtypeStruct((b, s, HQ), F32),
        scratch_shapes=[pltpu.VMEM((rows, LANES), BF16),
                        pltpu.VMEM((rows, LANES), F32),
                        pltpu.VMEM((rows, LANES), F32),
                        pltpu.VMEM((rows, SEL_CHUNK), F32),
                        pltpu.VMEM((rows, SEL_CHUNK), F32)],
        compiler_params=_cparams("parallel", "parallel", "arbitrary"),
        name="nsa_sel_attention",
    )(q, pen, kx, vx, bias_tiles)


def _sel_bias_idx(s):
    nqt = s // QBLK
    far = -(-(int(np.argmax(_t5_bucket_np(np.arange(4 * MAX_DISTANCE)) == NUM_BUCKETS - 1)) + QBLK) // QBLK)
    nbt = min(nqt, far + 1)
    d0 = np.arange(nbt)[:, None, None] * QBLK
    dist = d0 + np.arange(QBLK)[None, :, None] - np.arange(QBLK)[None, None, :]
    idx = np.where(dist >= 0, _t5_bucket_np(dist), -1).astype(np.int32)
    return np.concatenate([idx, np.full((1, QBLK, QBLK), -1, np.int32)])


def _ffn_kernel(h_ref, g2_ref, wu_ref, cw_ref, cb_ref, wd_ref, g3_ref,
                o_ref, xn_ref, acc_ref, u_ref, a_ref, carry_ref, *, tiles_per_seq, nj, lag):
    i = pl.program_id(0)
    ns, ts, tn = xn_ref.shape[0], xn_ref.shape[1], a_ref.shape[3]
    cols = lambda c: slice(c * tn, (c + 1) * tn)

    @pl.when(i % tiles_per_seq == 0)
    def _():
        carry_ref[...] = jnp.zeros(carry_ref.shape, F32)

    top = 16
    rowt = lax.broadcasted_iota(jnp.int32, (top, tn), 0)

    def up(s, j):
        xn = xn_ref[s]
        u_ref[s, j % 2, 0] = jnp.dot(xn, wu_ref[:, cols(j)], preferred_element_type=F32)
        u_ref[s, j % 2, 1] = jnp.dot(xn, wu_ref[:, cols(nj + j)], preferred_element_type=F32)

    def conv(u, s1, s2, c):
        cw = cw_ref[:, cols(c)]
        return ((cb_ref[:, cols(c)] + u * cw[2:3]) + s2 * cw[0:1]) + s1 * cw[1:2]

    def conv_body(u, c):
        return conv(u, pltpu.roll(u, 1, 0), pltpu.roll(u, 2, 0), c)

    def conv_top(uc_ref, c):
        prev = carry_ref[c]
        p1, p2 = prev[7:8], prev[6:7]
        u = uc_ref[0:top]
        s1 = jnp.where(rowt == 0, p1, pltpu.roll(u, 1, 0))
        s2 = jnp.where(rowt == 0, p2, jnp.where(rowt == 1, p1, pltpu.roll(u, 2, 0)))
        carry_ref[c] = uc_ref[ts - 8:ts]
        return conv(u, s1, s2, c)

    def gated(cg, cv):
        return (jax.nn.gelu(cg, approximate=True) * cv).astype(BF16)

    def act(s, j):
        ur, ar = u_ref.at[s, j % 2], a_ref.at[s, j % 2]
        ar[...] = gated(conv_body(ur[0], j), conv_body(ur[1], nj + j))
        ar[0:top] = gated(conv_top(ur.at[0], j), conv_top(ur.at[1], nj + j))

    def down(s, j):
        acc_ref[s] += jnp.dot(a_ref[s, j % 2], wd_ref[j * tn:(j + 1) * tn, :], preferred_element_type=F32)

    def iteration(s, k):
        rows = slice(s * ts, (s + 1) * ts)
        if k == -1:
            xn_ref[s] = _rms(h_ref[rows, :], g2_ref[...]).astype(xn_ref.dtype)
            acc_ref[s] = jnp.zeros(acc_ref.shape[1:], F32)
        if 0 <= k + 1 < nj:
            up(s, k + 1)
        if 0 <= k < nj:
            act(s, k)
        if 0 <= k - 1 < nj:
            down(s, k - 1)
        if k == nj:
            o_ref[rows, :] = h_ref[rows, :] + _rms(acc_ref[s], g3_ref[...])

    for slot in range(-1, nj + 1 + (ns - 1) * lag):
        for s in range(ns):
            k = slot - s * lag
            if -1 <= k <= nj:
                iteration(s, k)


def conv_ffn(h, g2, w_up, conv_w, conv_b, w_down, g3, seq, tm, tn, ns=FFN_SUBTILES, lag=FFN_LAG):
    t, d = h.shape
    dff = w_down.shape[0]
    nj = dff // tn
    ts = tm // ns
    assert nj * tn == dff and ts * ns == tm and ts % 16 == 0
    wu, cw, cb, wd = w_up, conv_w, conv_b.reshape(1, -1), w_down
    resident = lambda a: pl.BlockSpec(a.shape, (lambda nd: (lambda i: (0,) * nd))(a.ndim))
    return pl.pallas_call(
        functools.partial(_ffn_kernel, tiles_per_seq=seq // tm, nj=nj, lag=lag),
        grid=(t // tm,),
        in_specs=[pl.BlockSpec((tm, d), lambda i: (i, 0)),
                  pl.BlockSpec((1, d), lambda i: (0, 0)),
                  resident(wu), resident(cw), resident(cb), resident(wd),
                  pl.BlockSpec((1, d), lambda i: (0, 0))],
        out_specs=pl.BlockSpec((tm, d), lambda i: (i, 0)),
        out_shape=jax.ShapeDtypeStruct((t, d), F32),
        scratch_shapes=[pltpu.VMEM((ns, ts, d), BF16), pltpu.VMEM((ns, ts, d), F32),
                        pltpu.VMEM((ns, 2, 2, ts, tn), F32), pltpu.VMEM((ns, 2, ts, tn), BF16),
                        pltpu.VMEM((2 * nj, 8, tn), F32)],
        compiler_params=_cparams("arbitrary"),
        name="conv_ffn",
    )(h, g2.reshape(1, d), wu, cw, cb, wd, g3.reshape(1, d))


def _row_tile(t):
    for tm in (1024, 512, 256, 128):
        if t % tm == 0:
            return tm
    raise ValueError(f"token count {t} is not a multiple of 128")


def mixer_a(h, gains, w_in, w_o, rel_table, bsz, seq):
    t, d = h.shape
    tm = _row_tile(t)
    n_dil = len(DIL_CONFIGS)
    a_in = w_in.shape[1]
    col_scale = np.ones((a_in,), np.float32).reshape(n_dil, 3, HQ)
    col_scale[:, 0] = ATTN_SCALE
    w = (w_in * col_scale.reshape(1, a_in)).astype(BF16)
    idx = np.concatenate([_banded_idx(1, window // dil, dil) for window, dil in DIL_CONFIGS])
    bias = bias_expand(rel_table, idx)
    outs, lses = [], []
    dils = [dil for _, dil in DIL_CONFIGS]
    blocks_per_seq = seq // QBLK
    xns = norm_permute(h, gains[0], dils, max([tm] + [QBLK * dil for dil in dils]))
    for gi, (window, dil) in enumerate(DIL_CONFIGS):
        assert window // dil <= QBLK and seq % (QBLK * dil) == 0
        qkv = matmul_resident(xns[gi], w[:, gi * 3 * HQ:(gi + 1) * 3 * HQ], BF16, tm)
        qkv = qkv.reshape(1, t, 3 * HQ)

        def rmap(part, dil=dil):
            return lambda n, i: (0, (n // dil) * blocks_per_seq + i * dil + n % dil, part)

        o, lse = banded_attention(
            qkv, qkv, qkv, bias[2 * gi:2 * gi + 2], n_rows=bsz * dil, n_tiles=seq // (QBLK * dil), tq=QBLK, n_prev=1, kw=HQ,
            q_map=rmap(0), k_map=rmap(1), v_map=rmap(2), o_map=rmap(0),
            out_shape=(1, t, HQ), lse_shape=(1, t, LANES), name=f"dilated_attention_{dil}")
        outs.append(o.reshape(t, HQ))
        lses.append(lse.reshape(t, LANES))
    e = jnp.asarray(_head_expand_matrix(), BF16)
    return combine_proj_a(outs, lses, dils, e, w_o.astype(BF16), gains[1], h, min(tm, 512))


def mixer_b(h, gains, w_in, sinks, w_o, rel_table, bsz, seq):
    t, d = h.shape
    tm = _row_tile(t)
    hk = B_KV_HEADS * HEAD_DIM
    n_in = w_in.shape[1]
    assert n_in == HQ + 2 * hk
    w = jnp.concatenate([w_in[:, :HQ] * ATTN_SCALE, _dup_groups(w_in[:, HQ:HQ + hk]),
                         _dup_groups(w_in[:, HQ + hk:])], axis=1).astype(BF16)
    kw = 2 * hk
    qkv = norm_matmul(h, gains[0], w, BF16, tm, HQ + 2 * kw).reshape(bsz, seq, HQ + 2 * kw)
    bias = bias_expand(rel_table, _banded_idx(1, B_WINDOW - 1, 1))
    sink_rows = sinks.astype(F32)
    ident = lambda n, i: (n, i, 0)
    col = lambda c: (lambda n, i: (n, i, c))
    (o,) = banded_attention(qkv, qkv, qkv, bias, n_rows=bsz, n_tiles=seq // QBLK, tq=QBLK, n_prev=1,
                            kw=kw, q_map=ident, k_map=col(HQ // kw), v_map=col(HQ // kw + 1), o_map=ident,
                            out_shape=(bsz, seq, HQ), sinks=sink_rows, stack=8, name="sink_window_attention")
    return _proj_call(_proj_kernel, "sink_proj", tm, [o.reshape(t, HQ)], [], w_o.astype(BF16), gains[1], h)


def mixer_c(h, gains, w_in, cmp_pos, cmp_w1, cmp_w2, w_o, rel_table, bsz, seq):
    t, d = h.shape
    tm = _row_tile(t)
    g = C_KV_HEADS
    hk = g * HEAD_DIM
    wkv = [w_in[:, HQ + i * hk:HQ + (i + 1) * hk] for i in range(6)]
    n_gate = w_in.shape[1] - HQ - 6 * hk
    assert n_gate == 3 * N_HEADS and hk == LANES
    wq = (w_in[:, :HQ] * ATTN_SCALE).astype(BF16)
    spread = lambda w: jnp.concatenate([w[:, :HEAD_DIM], jnp.zeros_like(w[:, :HEAD_DIM]),
                                        w[:, HEAD_DIM:], jnp.zeros_like(w[:, :HEAD_DIM])], axis=1)
    wb = jnp.concatenate([spread(wkv[2]), spread(wkv[3]), _dup_groups(wkv[4]), _dup_groups(wkv[5])],
                         axis=1).astype(BF16)
    wf = jnp.concatenate([wkv[0], wkv[1], jnp.pad(w_in[:, HQ + 6 * hk:], ((0, 0), (0, LANES - n_gate)))],
                         axis=1).astype(BF16)
    kw = 2 * LANES
    q = norm_matmul(h, gains[0], jnp.concatenate([wq, wb], axis=1), BF16, tm, HQ + 4 * kw)
    q = q.reshape(bsz, seq, HQ + 4 * kw)
    kvb = q[..., HQ:HQ + 2 * kw]
    r = norm_matmul(h, gains[0], wf, F32, tm, 3 * LANES)
    kv = [r[:, i * hk:(i + 1) * hk].reshape(bsz, seq, hk) for i in range(2)]
    gates_raw = r[:, 2 * hk:2 * hk + LANES]

    ncr = seq // CMP_STRIDE
    half = CMP_BLOCK // 2
    assert half == CMP_STRIDE
    chunks = jnp.stack([kv[0], kv[1]]).reshape(2, bsz, ncr, half, g, HEAD_DIM)
    chunks = chunks.transpose(0, 1, 4, 2, 3, 5).reshape(2, bsz, g, ncr, half * HEAD_DIM)
    pos = cmp_pos.reshape(2, 2, half * HEAD_DIM)
    w1 = cmp_w1.reshape(2, 2, half * HEAD_DIM, -1).astype(BF16)
    cmp = compress(chunks, pos, w1, cmp_w2.astype(BF16))
    cmp = cmp.transpose(0, 1, 3, 2, 4).reshape(2, bsz, ncr, hk).astype(BF16)
    kcd, vcd = _dup_groups(cmp[0]), _dup_groups(cmp[1])

    ns = seq // SEL_BLOCK
    assert ns <= HEAD_DIM
    k_sel = min(N_SELECT, ns)
    starts = np.arange(ncr) * CMP_STRIDE
    blk = np.arange(HEAD_DIM)
    ov = ((starts[None, :] < (blk[:, None] + 1) * SEL_BLOCK)
          & (starts[None, :] + CMP_BLOCK > blk[:, None] * SEL_BLOCK)
          & (blk[:, None] < ns) & (starts[None, :] + CMP_BLOCK <= seq))
    o_c, pen = cmp_attention(q, kcd, vcd, jnp.asarray(ov.astype(np.float32), BF16), ns, k_sel)

    sel_bias = bias_expand(rel_table, _sel_bias_idx(seq))
    lane = np.arange(kw)[None, :]
    key_blk = np.arange(seq)[:, None] // SEL_BLOCK
    onehot = np.where((lane % LANES >= HEAD_DIM) & (lane % HEAD_DIM == key_blk), -NEG_INF, 0.0)
    ones = np.broadcast_to(np.where(lane % LANES >= HEAD_DIM, 1.0, 0.0), (seq, kw))
    kx = kvb[..., :kw] + jnp.asarray(onehot, BF16)[None]
    vx = kvb[..., kw:2 * kw] + jnp.asarray(ones, BF16)[None]
    o_s = sel_attention(q, kx, vx, pen, sel_bias)

    n_prev = -(-(C_WINDOW - 1) // QBLK)
    tqw = n_prev * QBLK
    wbias = bias_expand(rel_table, _banded_idx(n_prev, C_WINDOW - 1, 1))
    ident = lambda n, i: (n, i, 0)
    col = lambda c: (lambda n, i: (n, i, c))
    (o_w,) = banded_attention(q, q, q, wbias,
                              n_rows=bsz, n_tiles=seq // tqw, tq=tqw, n_prev=n_prev, kw=kw,
                              q_map=ident, k_map=col(HQ // kw + 2), v_map=col(HQ // kw + 3), o_map=ident,
                              out_shape=(bsz, seq, HQ), stack=8, name="nsa_window_attention")

    e = jnp.asarray(_head_expand_matrix(), BF16)
    return _proj_call(_proj_c_kernel, "nsa_gate_proj", min(tm, 512),
                      [o_c.reshape(t, HQ), o_s.reshape(t, HQ), o_w.reshape(t, HQ), gates_raw], [e],
                      w_o.astype(BF16), gains[1], h)


def kernel(x, rel_table, norm_gains, a_w_in, a_w_o, b_w_in, b_sinks, b_w_o, c_w_in, c_cmp_pos, c_cmp_w1,
           c_cmp_w2, c_w_o, ffn_w_up, ffn_conv_w, ffn_conv_b, ffn_w_down):
    bsz, seq, d = x.shape
    depth = norm_gains.shape[0]
    h = x.reshape(bsz * seq, d)
    tm = _row_tile(seq)
    for i in range(depth):
        kind, j = i % 3, i // 3
        g = norm_gains[i]
        if kind == 0:
            h = mixer_a(h, g, a_w_in[j], a_w_o[j], rel_table, bsz, seq)
        elif kind == 1:
            h = mixer_b(h, g, b_w_in[j], b_sinks[j], b_w_o[j], rel_table, bsz, seq)
        else:
            h = mixer_c(h, g, c_w_in[j], c_cmp_pos[j], c_cmp_w1[j], c_cmp_w2[j], c_w_o[j], rel_table, bsz, seq)
        h = conv_ffn(h, g[2], ffn_w_up[i].astype(BF16), ffn_conv_w[i], ffn_conv_b[i],
                     ffn_w_down[i].astype(BF16), g[3], seq, min(tm, FFN_ROWS), FFN_COLS)
    return h.reshape(bsz, seq, d)
```

```python
import functools
import math

import numpy as np
import jax
import jax.numpy as jnp
from jax import lax
from jax.experimental import pallas as pl
from jax.experimental.pallas import tpu as pltpu

F32 = jnp.float32
BF16 = jnp.bfloat16

N_HEADS = 16
HEAD_DIM = 64
HQ = N_HEADS * HEAD_DIM
LANES = 128
ATTN_SCALE = HEAD_DIM ** -0.5
LOG2E = math.log2(math.e)
LN2 = math.log(2.0)
Q_SCALE = ATTN_SCALE * LOG2E
NUM_BUCKETS = 32
MAX_DISTANCE = 2048
RMS_EPS = 1e-6
NEG_INF = -1e30
FORCE_SCORE = 1e9
DIL_CONFIGS = ((128, 1), (512, 4), (2048, 16))
B_KV_HEADS = 2
B_WINDOW = 128
C_KV_HEADS = 2
CMP_BLOCK = 32
CMP_STRIDE = 16
SEL_BLOCK = 64
N_SELECT = 16
C_WINDOW = 512
CONV_WIDTH = 3
QBLK = 128
SEL_CHUNK = 512
SEL_QTILES = 2
FFN_COLS = 256
FFN_ROWS = 512
FFN_SUBTILES = 2
FFN_LAG = 0
VMEM_LIMIT = 56 * 1024 * 1024


def _cparams(*sem):
    return pltpu.CompilerParams(dimension_semantics=sem, vmem_limit_bytes=VMEM_LIMIT)


def _t5_bucket_np(dist):
    max_exact = NUM_BUCKETS // 2
    d = np.maximum(dist, 0)
    df = np.maximum(d, 1).astype(np.float64)
    large = max_exact + np.floor(np.log(df / max_exact) / math.log(MAX_DISTANCE / max_exact)
                                 * (NUM_BUCKETS - max_exact) + 1e-9).astype(np.int64)
    large = np.minimum(large, NUM_BUCKETS - 1)
    return np.where(d < max_exact, d, large).astype(np.int32)


def _rms(x, g):
    ms = jnp.mean(x * x, axis=-1, keepdims=True)
    return (x * lax.rsqrt(ms + RMS_EPS)) * g


def _split3(w):
    hi = w.astype(BF16)
    r1 = w - hi.astype(F32)
    mid = r1.astype(BF16)
    lo = (r1 - mid.astype(F32)).astype(BF16)
    return hi, mid, lo


def _expand_heads(w, e3):
    lane = lax.broadcasted_iota(jnp.int32, w.shape, 1)
    r1 = w - w.astype(BF16).astype(F32)
    r2 = r1 - r1.astype(BF16).astype(F32)
    packed = jnp.where(lane < N_HEADS, w,
                       jnp.where(lane < 2 * N_HEADS, pltpu.roll(r1, N_HEADS, 1), pltpu.roll(r2, 2 * N_HEADS, 1)))
    return jnp.dot(packed.astype(BF16), e3, preferred_element_type=F32)


def _dot_nt(a, b):
    return lax.dot_general(a, b, (((1,), (1,)), ((), ())), preferred_element_type=F32)


def _stack_heads(q_tile, heads):
    lane = lax.broadcasted_iota(jnp.int32, (QBLK, LANES), 1)
    lo = lane < HEAD_DIM
    pieces = []
    for h in heads:
        q2 = q_tile(h // 2)
        keep = lo if h % 2 == 0 else jnp.logical_not(lo)
        pieces.append(jnp.where(keep, q2, jnp.zeros_like(q2)))
    return jnp.concatenate(pieces, axis=0)


def _merge_pairs(o, n_heads):
    lane = lax.broadcasted_iota(jnp.int32, (QBLK, LANES), 1)
    lo = lane < HEAD_DIM
    out = []
    for t in range(0, n_heads, 2):
        out.append(jnp.where(lo, o[t * QBLK:(t + 1) * QBLK], o[(t + 1) * QBLK:(t + 2) * QBLK]))
    return out


def _bias_kernel(tab_ref, rng_ref, idx_ref, o_ref):
    n = pl.program_id(0)
    idx = idx_ref[0]
    o_ref[...] = jnp.full(o_ref.shape, NEG_INF, F32)

    def body(b, carry):
        hit = idx == b
        for h in range(N_HEADS):
            o_ref[0, h] = jnp.where(hit, tab_ref[b, h] * LOG2E, o_ref[0, h])
        return carry

    lax.fori_loop(rng_ref[n, 0], rng_ref[n, 1] + 1, body, 0)


def bias_expand(table, idx):
    n, r, c = idx.shape
    flat = idx.reshape(n, -1)
    lo = np.where(flat >= 0, flat, NUM_BUCKETS).min(axis=1)
    rng = np.stack([lo, flat.max(axis=1)], axis=1).astype(np.int32)
    return pl.pallas_call(
        _bias_kernel,
        grid=(n,),
        in_specs=[pl.BlockSpec(memory_space=pltpu.SMEM),
                  pl.BlockSpec(memory_space=pltpu.SMEM),
                  pl.BlockSpec((1, r, c), lambda i: (i, 0, 0))],
        out_specs=pl.BlockSpec((1, N_HEADS, r, c), lambda i: (i, 0, 0, 0)),
        out_shape=jax.ShapeDtypeStruct((n, N_HEADS, r, c), F32),
        compiler_params=_cparams("parallel"),
        name="bias_expand",
    )(table, jnp.asarray(rng), jnp.asarray(idx))


def _banded_idx(n_prev, max_dist, stride):
    kb = (n_prev + 1) * QBLK
    dist = np.arange(QBLK)[:, None] + n_prev * QBLK - np.arange(kb)[None, :]
    valid = (dist >= 0) & (dist <= max_dist)
    idx = np.where(valid, _t5_bucket_np(dist * stride), -1).astype(np.int32)
    if n_prev > 1:
        return idx[None]
    first = np.where(np.arange(kb)[None, :] < n_prev * QBLK, -1, idx).astype(np.int32)
    return np.stack([idx, first])


def _norm_mm_kernel(x_ref, g_ref, w_ref, o_ref, xn_ref):
    @pl.when(pl.program_id(1) == 0)
    def _():
        xn_ref[...] = _rms(x_ref[...], g_ref[...]).astype(xn_ref.dtype)

    o_ref[...] = jnp.dot(xn_ref[...], w_ref[...], preferred_element_type=F32).astype(o_ref.dtype)


def norm_matmul(x, gain, w, out_dtype, tm, tn):
    t, d = x.shape
    n = w.shape[1]
    scratch = [pltpu.VMEM((tm, d), BF16)]
    return pl.pallas_call(
        _norm_mm_kernel,
        grid=(t // tm, n // tn),
        in_specs=[pl.BlockSpec((tm, d), lambda i, j: (i, 0)),
                  pl.BlockSpec((1, d), lambda i, j: (0, 0)),
                  pl.BlockSpec((d, tn), lambda i, j: (0, j))],
        out_specs=pl.BlockSpec((tm, tn), lambda i, j: (i, j)),
        out_shape=jax.ShapeDtypeStruct((t, n), out_dtype),
        scratch_shapes=scratch,
        compiler_params=_cparams("parallel", "arbitrary"),
        name="norm_matmul",
    )(x, gain.reshape(1, d), w)


def _norm_perm_kernel(x_ref, g_ref, *refs, dils):
    o_refs, xs_ref = refs[:len(dils)], refs[len(dils)]
    xn = _rms(x_ref[...], g_ref[...])
    nc = xs_ref.shape[0]
    for c in range(nc):
        xs_ref[c] = xn[:, c * LANES:(c + 1) * LANES]
    for o_ref, dil in zip(o_refs, dils):
        if dil == 1:
            o_ref[...] = xn.astype(o_ref.dtype)
            continue
        span = QBLK * dil
        for u in range(x_ref.shape[0] // span):
            for r in range(dil):
                rows = jnp.concatenate(
                    [xs_ref[c, pl.ds(u * span + r, QBLK, stride=dil), :] for c in range(nc)], axis=1)
                dst = u * span + r * QBLK
                o_ref[dst:dst + QBLK, :] = rows.astype(o_ref.dtype)


def norm_permute(x, gain, dils, tm):
    t, d = x.shape
    assert all(tm % (QBLK * dil) == 0 for dil in dils)
    return pl.pallas_call(
        functools.partial(_norm_perm_kernel, dils=tuple(dils)),
        grid=(t // tm,),
        in_specs=[pl.BlockSpec((tm, d), lambda i: (i, 0)),
                  pl.BlockSpec((1, d), lambda i: (0, 0))],
        out_specs=[pl.BlockSpec((tm, d), lambda i: (i, 0)) for _ in dils],
        out_shape=[jax.ShapeDtypeStruct((t, d), BF16) for _ in dils],
        scratch_shapes=[pltpu.VMEM((d // LANES, tm, LANES), F32)],
        compiler_params=_cparams("parallel"),
        name="norm_permute",
    )(x, gain.reshape(1, d))


def _mm_kernel(x_ref, w_ref, o_ref):
    o_ref[...] = jnp.dot(x_ref[...], w_ref[...], preferred_element_type=F32).astype(o_ref.dtype)


def matmul_resident(x, w, out_dtype, tm):
    t, k = x.shape
    n = w.shape[1]
    return pl.pallas_call(
        _mm_kernel,
        grid=(t // tm,),
        in_specs=[pl.BlockSpec((tm, k), lambda i: (i, 0)),
                  pl.BlockSpec((k, n), lambda i: (0, 0))],
        out_specs=pl.BlockSpec((tm, n), lambda i: (i, 0)),
        out_shape=jax.ShapeDtypeStruct((t, n), out_dtype),
        compiler_params=_cparams("parallel"),
        name="matmul_resident",
    )(x, w)


def _finish_proj(y, w_ref, g_ref, h_ref, o_ref):
    z = jnp.dot(y.astype(BF16), w_ref[...], preferred_element_type=F32)
    o_ref[...] = h_ref[...] + _rms(z, g_ref[...])


def _proj_kernel(y_ref, w_ref, g_ref, h_ref, o_ref):
    _finish_proj(y_ref[...], w_ref, g_ref, h_ref, o_ref)


def _unpermute(src_ref, dst_ref, dil, tm, i):
    span = QBLK * dil
    nc = dst_ref.shape[0]
    if span <= tm:
        for u in range(tm // span):
            for r in range(dil):
                lo = u * span + r * QBLK
                for c in range(nc):
                    dst_ref[c, pl.ds(u * span + r, QBLK, stride=dil), :] = src_ref[lo:lo + QBLK,
                                                                                   c * LANES:(c + 1) * LANES]
    else:
        per = tm // dil
        off = (i % (span // tm)) * per
        for r in range(dil):
            lo = pl.multiple_of(r * QBLK + off, 8)
            for c in range(nc):
                dst_ref[c, pl.ds(r, per, stride=dil), :] = src_ref[pl.ds(lo, per), c * LANES:(c + 1) * LANES]
    return jnp.concatenate([dst_ref[c] for c in range(nc)], axis=1) if nc > 1 else dst_ref[0]


def _proj_a_kernel(*refs, dils, tm):
    n = len(dils)
    o_refs, l_refs = refs[:n], refs[n:2 * n]
    e_ref, w_ref, g_ref, h_ref, out_ref = refs[2 * n:2 * n + 5]
    scratch = refs[2 * n + 5:]
    i = pl.program_id(0)
    outs, lses = [], []
    si = 0
    for gi, dil in enumerate(dils):
        if dil == 1:
            outs.append(o_refs[gi][...])
            lses.append(l_refs[gi][...])
        else:
            outs.append(_unpermute(o_refs[gi], scratch[si], dil, tm, i))
            lses.append(_unpermute(l_refs[gi], scratch[si + 1], dil, tm, i))
            si += 2
    mx = functools.reduce(jnp.maximum, lses)
    es = [jnp.exp(l - mx) for l in lses]
    inv = 1.0 / functools.reduce(lambda a, b: a + b, es)
    e = e_ref[...]
    y = outs[0] * _expand_heads(es[0] * inv, e)
    for gi in range(1, n):
        y = y + outs[gi] * _expand_heads(es[gi] * inv, e)
    _finish_proj(y, w_ref, g_ref, h_ref, out_ref)


def combine_proj_a(outs, lses, dils, e, w, gain, h, tm):
    t, d = h.shape

    def row_spec(width, dil):
        rows = max(tm, QBLK * dil)
        return pl.BlockSpec((rows, width), lambda i, q=rows // tm: (i // q, 0))

    in_specs = [row_spec(HQ, dil) for dil in dils] + [row_spec(LANES, dil) for dil in dils] + [
        pl.BlockSpec(e.shape, lambda i: (0, 0)),
        pl.BlockSpec(w.shape, lambda i: (0, 0)),
        pl.BlockSpec((1, d), lambda i: (0, 0)),
        pl.BlockSpec((tm, d), lambda i: (i, 0))]
    scratch = []
    for dil in dils:
        if dil > 1:
            scratch += [pltpu.VMEM((HQ // LANES, tm, LANES), F32), pltpu.VMEM((1, tm, LANES), F32)]
    return pl.pallas_call(
        functools.partial(_proj_a_kernel, dils=tuple(dils), tm=tm),
        grid=(t // tm,),
        in_specs=in_specs,
        out_specs=pl.BlockSpec((tm, d), lambda i: (i, 0)),
        out_shape=jax.ShapeDtypeStruct((t, d), F32),
        scratch_shapes=scratch,
        compiler_params=_cparams("arbitrary"),
        name="dilated_combine_proj",
    )(*outs, *lses, e, w, gain.reshape(1, d), h)


def _proj_c_kernel(oc_ref, os_ref, ow_ref, gr_ref, e_ref, w_ref, g_ref, h_ref, o_ref):
    sig = jax.nn.sigmoid(gr_ref[...])
    e = e_ref[...]
    gate = lambda i: _expand_heads(sig if i == 0 else pltpu.roll(sig, LANES - i * N_HEADS, 1), e)
    y = gate(0) * oc_ref[...]
    y = y + gate(1) * os_ref[...]
    y = y + gate(2) * ow_ref[...]
    _finish_proj(y, w_ref, g_ref, h_ref, o_ref)


def _proj_call(kernel, name, tm, row_inputs, const_inputs, w, gain, h):
    t, d = h.shape
    k = w.shape[0]
    row_specs = [pl.BlockSpec((tm, a.shape[1]), lambda i: (i, 0)) for a in row_inputs]
    const_specs = [pl.BlockSpec(a.shape, (lambda nd: (lambda i: (0,) * nd))(a.ndim)) for a in const_inputs]
    return pl.pallas_call(
        kernel,
        grid=(t // tm,),
        in_specs=row_specs + const_specs + [
            pl.BlockSpec((k, d), lambda i: (0, 0)),
            pl.BlockSpec((1, d), lambda i: (0, 0)),
            pl.BlockSpec((tm, d), lambda i: (i, 0))],
        out_specs=pl.BlockSpec((tm, d), lambda i: (i, 0)),
        out_shape=jax.ShapeDtypeStruct((t, d), F32),
        compiler_params=_cparams("parallel"),
        name=name,
    )(*row_inputs, *const_inputs, w, gain.reshape(1, d), h)


def _head_expand_matrix():
    e = np.zeros((LANES, HQ), np.float32)
    for term in range(3):
        for h in range(N_HEADS):
            e[term * N_HEADS + h, h * HEAD_DIM:(h + 1) * HEAD_DIM] = 1.0
    return e


def _banded_kernel(*refs, tq, n_prev, n_chunks, stack, with_sinks, with_lse):
    q_ref, kp_ref, kc_ref, vp_ref, vc_ref, bias_ref = refs[:6]
    pos = 6
    sink_ref = None
    if with_sinks:
        sink_ref = refs[pos]
        pos += 1
    o_ref = refs[pos]
    pos += 1
    lse_ref = None
    if with_lse:
        lse_ref = refs[pos]
        pos += 1
    kbuf, vbuf = refs[pos], refs[pos + 1]

    i = pl.program_id(1)
    hpc = N_HEADS // n_chunks
    kb = (n_prev + 1) * QBLK
    kbuf[0:tq] = kp_ref[0]
    kbuf[tq:2 * tq] = kc_ref[0]
    vbuf[0:tq] = vp_ref[0]
    vbuf[tq:2 * tq] = vc_ref[0]
    lane = lax.broadcasted_iota(jnp.int32, (QBLK, LANES), 1)
    head_row = lax.broadcasted_iota(jnp.int32, (stack * QBLK, 1), 0)

    for s in range(tq // QBLK):
        r0 = s * QBLK
        k0 = tq + r0 - n_prev * QBLK
        variants = bias_ref.shape[0] > 1
        if variants:
            first = jnp.where(i == 0, 1, 0)
        else:
            col = lax.broadcasted_iota(jnp.int32, (1, kb), 1) + k0
            kmask = jnp.where(jnp.logical_and(i == 0, col < tq), NEG_INF, 0.0).astype(F32)
        lse_acc = jnp.zeros((QBLK, LANES), F32)
        for h0 in range(0, N_HEADS, stack):
            heads = list(range(h0, h0 + stack))
            c = h0 // hpc
            qst = _stack_heads(lambda p: q_ref[0, r0:r0 + QBLK, p * LANES:(p + 1) * LANES], heads)
            kx = kbuf[k0:k0 + kb, c * LANES:(c + 1) * LANES]
            vx = vbuf[k0:k0 + kb, c * LANES:(c + 1) * LANES]
            sc = _dot_nt(qst, kx)
            if variants:
                sc = sc + bias_ref[first, h0:h0 + stack].reshape(stack * QBLK, kb)
            else:
                sc = sc + bias_ref[0, h0:h0 + stack].reshape(stack * QBLK, kb) + kmask
            m = jnp.max(sc, axis=-1, keepdims=True)
            if with_sinks:
                sk = sink_ref[h0 + stack - 1]
                for t in range(stack - 2, -1, -1):
                    sk = jnp.where(head_row < (t + 1) * QBLK, sink_ref[h0 + t], sk)
                m = jnp.maximum(m, sk)
            p = jnp.exp2(sc - m)
            den = jnp.sum(p, axis=-1, keepdims=True)
            norm = den + jnp.exp2(sk - m) if with_sinks else den
            o = jnp.dot(p.astype(BF16), vx, preferred_element_type=F32) * (1.0 / norm)
            for t2, blk in enumerate(_merge_pairs(o, stack)):
                pidx = h0 // 2 + t2
                o_ref[0, r0:r0 + QBLK, pidx * LANES:(pidx + 1) * LANES] = blk.astype(o_ref.dtype)
            if with_lse:
                lse = (m + jnp.log2(den)) * LN2
                for t, h in enumerate(heads):
                    lse_acc = jnp.where(lane == h, lse[t * QBLK:(t + 1) * QBLK], lse_acc)
        if with_lse:
            lse_ref[0, r0:r0 + QBLK, :] = lse_acc


def banded_attention(q_arr, k_arr, v_arr, bias, *, n_rows, n_tiles, tq, n_prev, kw,
                     q_map, k_map, v_map, o_map, out_shape, lse_shape=None, sinks=None, stack=2, name):
    n_chunks = kw // LANES
    assert stack % 2 == 0 and (N_HEADS // n_chunks) % stack == 0
    kb = (n_prev + 1) * QBLK
    with_sinks = sinks is not None
    with_lse = lse_shape is not None

    def prev(fn):
        return lambda n, i: fn(n, jnp.maximum(i - 1, 0))

    in_specs = [pl.BlockSpec((1, tq, HQ), q_map),
                pl.BlockSpec((1, tq, kw), prev(k_map)),
                pl.BlockSpec((1, tq, kw), k_map),
                pl.BlockSpec((1, tq, kw), prev(v_map)),
                pl.BlockSpec((1, tq, kw), v_map),
                pl.BlockSpec(bias.shape, lambda n, i: (0, 0, 0, 0))]
    assert bias.shape[1:] == (N_HEADS, QBLK, kb) and (bias.shape[0] == 1 or tq == QBLK)
    args = [q_arr, k_arr, k_arr, v_arr, v_arr, bias]
    if with_sinks:
        in_specs.append(pl.BlockSpec(memory_space=pltpu.SMEM))
        args.append(sinks)
    out_specs = [pl.BlockSpec((1, tq, HQ), o_map)]
    out_shapes = [jax.ShapeDtypeStruct(out_shape, F32)]
    if with_lse:
        out_specs.append(pl.BlockSpec((1, tq, LANES), o_map))
        out_shapes.append(jax.ShapeDtypeStruct(lse_shape, F32))
    res = pl.pallas_call(
        functools.partial(_banded_kernel, tq=tq, n_prev=n_prev, n_chunks=n_chunks, stack=stack,
                          with_sinks=with_sinks, with_lse=with_lse),
        grid=(n_rows, n_tiles),
        in_specs=in_specs,
        out_specs=out_specs,
        out_shape=out_shapes,
        scratch_shapes=[pltpu.VMEM((2 * tq, kw), BF16), pltpu.VMEM((2 * tq, kw), BF16)],
        compiler_params=_cparams("parallel", "arbitrary"),
        name=name,
    )(*args)
    return res


def _dup_groups(x):
    g0, g1 = x[..., :HEAD_DIM], x[..., HEAD_DIM:]
    return jnp.concatenate([g0, g0, g1, g1], axis=-1)


def _compress_kernel(ch_ref, pos_ref, w1_ref, w2_ref, o_ref):
    ch = ch_ref[0, 0, 0]
    rows = ch.shape[0]
    posv = pos_ref[0]
    a = jnp.dot((ch + posv[0:1]).astype(BF16), w1_ref[0, 0], preferred_element_type=F32)
    b = jnp.dot((ch + posv[1:2]).astype(BF16), w1_ref[0, 1], preferred_element_type=F32)
    hid = a + pltpu.roll(b, rows - 1, 0)
    act = jax.nn.gelu(hid, approximate=True)
    o_ref[0, 0, 0] = jnp.dot(act.astype(BF16), w2_ref[0], preferred_element_type=F32)


def compress(chunks, pos, w1, w2):
    _, b, g, rows, width = chunks.shape
    hid = w1.shape[-1]
    return pl.pallas_call(
        _compress_kernel,
        grid=(2, b, g),
        in_specs=[pl.BlockSpec((1, 1, 1, rows, width), lambda i, bb, gg: (i, bb, gg, 0, 0)),
                  pl.BlockSpec((1, 2, width), lambda i, bb, gg: (i, 0, 0)),
                  pl.BlockSpec((1, 2, width, hid), lambda i, bb, gg: (i, 0, 0, 0)),
                  pl.BlockSpec((1, hid, HEAD_DIM), lambda i, bb, gg: (i, 0, 0))],
        out_specs=pl.BlockSpec((1, 1, 1, rows, HEAD_DIM), lambda i, bb, gg: (i, bb, gg, 0, 0)),
        out_shape=jax.ShapeDtypeStruct((2, b, g, rows, HEAD_DIM), F32),
        compiler_params=_cparams("parallel", "parallel", "parallel"),
        name="nsa_compress",
    )(chunks, pos, w1, w2)


def _cmp_attn_kernel(q_ref, kc_ref, vc_ref, ov_ref, place_ref, o_ref, sel_ref, *, n_sel_blocks, k_sel):
    qi = pl.program_id(1)
    ncr = kc_ref.shape[1]
    hpc = N_HEADS // C_KV_HEADS
    qpos = qi * QBLK + lax.broadcasted_iota(jnp.int32, (QBLK, 1), 0)
    cidx = lax.broadcasted_iota(jnp.int32, (1, ncr), 1)
    valid = (cidx * CMP_STRIDE + (CMP_BLOCK - 1)) <= qpos
    maskc = jnp.where(valid, 0.0, NEG_INF).astype(F32)
    anyv = (qpos >= CMP_BLOCK - 1).astype(F32)
    blk_id = lax.broadcasted_iota(jnp.int32, (HEAD_DIM, QBLK), 0)
    cur = (qi * QBLK + lax.broadcasted_iota(jnp.int32, (HEAD_DIM, QBLK), 1)) // SEL_BLOCK
    forced = jnp.logical_or(jnp.logical_or(blk_id == 0, blk_id == cur), blk_id == cur - 1)
    allowed = blk_id <= cur
    for g in range(C_KV_HEADS):
        heads = list(range(g * hpc, (g + 1) * hpc))
        qst = _stack_heads(lambda p: q_ref[0, :, p * LANES:(p + 1) * LANES], heads)
        sc = _dot_nt(qst, kc_ref[0, :, g * LANES:(g + 1) * LANES])
        sc3 = sc.reshape(hpc, QBLK, ncr) + maskc[None]
        m = jnp.max(sc3, axis=-1, keepdims=True)
        e = jnp.exp2(sc3 - m)
        ssum = jnp.sum(e, axis=-1, keepdims=True)
        p = (e * (1.0 / ssum)) * anyv[None]
        o = jnp.dot(p.reshape(hpc * QBLK, ncr).astype(BF16), vc_ref[0, :, g * LANES:(g + 1) * LANES],
                    preferred_element_type=F32)
        for t2, blk in enumerate(_merge_pairs(o, hpc)):
            pidx = (g * hpc) // 2 + t2
            o_ref[0, :, pidx * LANES:(pidx + 1) * LANES] = blk
        hi, mid, lo = _split3(jnp.sum(p, axis=0))
        ovt = ov_ref[...]
        imp = (_dot_nt(ovt, hi) + _dot_nt(ovt, mid)) + _dot_nt(ovt, lo)
        score = jnp.where(forced, FORCE_SCORE, jnp.where(allowed, imp, NEG_INF))
        bits = pltpu.bitcast(score, jnp.int32)
        key = jnp.where(bits < 0, bits ^ jnp.int32(0x7FFFFFFF), bits)
        key_m1 = key - 1
        rank = jnp.zeros((HEAD_DIM, QBLK), jnp.int32)
        for i in range(n_sel_blocks):
            thr = jnp.where(blk_id > i, key_m1, key)
            rank = rank + jnp.where(key[i:i + 1, :] > thr, 1, 0)
        keep = jnp.logical_and(rank < k_sel, blk_id < n_sel_blocks)
        pen_t = jnp.where(keep, 0.0, -1.0).astype(BF16)
        pen = lax.dot_general(pen_t, place_ref[...], (((0,), (0,)), ((), ())), preferred_element_type=F32)
        sel_ref[0, :, g * LANES:(g + 1) * LANES] = pen.astype(sel_ref.dtype)


def cmp_attention(q, kcd, vcd, ov, n_sel_blocks, k_sel):
    b, s, _ = q.shape
    ncr = kcd.shape[1]
    place = np.zeros((HEAD_DIM, LANES), np.float32)
    place[np.arange(HEAD_DIM), HEAD_DIM + np.arange(HEAD_DIM)] = 1.0
    return pl.pallas_call(
        functools.partial(_cmp_attn_kernel, n_sel_blocks=n_sel_blocks, k_sel=k_sel),
        grid=(b, s // QBLK),
        in_specs=[pl.BlockSpec((1, QBLK, HQ), lambda bb, i: (bb, i, 0)),
                  pl.BlockSpec((1, ncr, 2 * LANES), lambda bb, i: (bb, 0, 0)),
                  pl.BlockSpec((1, ncr, 2 * LANES), lambda bb, i: (bb, 0, 0)),
                  pl.BlockSpec((HEAD_DIM, ncr), lambda bb, i: (0, 0)),
                  pl.BlockSpec((HEAD_DIM, LANES), lambda bb, i: (0, 0))],
        out_specs=[pl.BlockSpec((1, QBLK, HQ), lambda bb, i: (bb, i, 0)),
                   pl.BlockSpec((1, QBLK, 2 * LANES), lambda bb, i: (bb, i, 0))],
        out_shape=[jax.ShapeDtypeStruct((b, s, HQ), F32),
                   jax.ShapeDtypeStruct((b, s, 2 * LANES), BF16)],
        compiler_params=_cparams("parallel", "parallel"),
        name="nsa_cmp_attention",
    )(q, kcd, vcd, ov, jnp.asarray(place, BF16))


def _sel_attn_kernel(q_ref, pen_ref, k_ref, v_ref, bias_ref, o_ref, qst_ref, m_ref, acc_ref,
                     sa_ref, sb_ref, *, nbt):
    qi0 = pl.program_id(2) * SEL_QTILES
    hpc = N_HEADS // C_KV_HEADS
    nsub = SEL_CHUNK // QBLK
    rpt = hpc * QBLK
    lane = lax.broadcasted_iota(jnp.int32, (QBLK, LANES), 1)
    lo = lane < HEAD_DIM
    for w in range(SEL_QTILES):
        pen = pen_ref[0, w * QBLK:(w + 1) * QBLK, :].astype(F32)
        for t in range(hpc):
            q2 = q_ref[0, w * QBLK:(w + 1) * QBLK, (t // 2) * LANES:(t // 2 + 1) * LANES].astype(F32)
            if t % 2:
                q2 = pltpu.roll(q2, HEAD_DIM, 1)
            qst_ref[w * rpt + t * QBLK:w * rpt + (t + 1) * QBLK, :] = jnp.where(lo, q2, pen).astype(BF16)
    m_ref[...] = jnp.full(m_ref.shape, NEG_INF, F32)
    acc_ref[...] = jnp.zeros(acc_ref.shape, F32)

    n_chunks = (qi0 + SEL_QTILES - 1) // nsub + 1

    def scores(c, s_ref):
        k0 = pl.multiple_of(c * SEL_CHUNK, SEL_CHUNK)
        s_ref[...] = _dot_nt(qst_ref[...], k_ref[0, pl.ds(k0, SEL_CHUNK), :])

    def accumulate(c, s_ref):
        k0 = pl.multiple_of(c * SEL_CHUNK, SEL_CHUNK)
        ps, alphas = [], []
        for w in range(SEL_QTILES):
            rows = slice(w * rpt, (w + 1) * rpt)
            tiles = []
            for u in range(nsub):
                off = qi0 + w - nsub * c - u
                idx = jnp.where(off < 0, nbt, jnp.minimum(off, nbt - 1))
                tiles.append(s_ref[rows, u * QBLK:(u + 1) * QBLK] + bias_ref[idx].reshape(rpt, QBLK))
            m_old = m_ref[rows, :]
            m_new = jnp.maximum(m_old, jnp.max(functools.reduce(jnp.maximum, tiles), axis=-1, keepdims=True))
            ps.append(jnp.concatenate([jnp.exp2(tl - m_new).astype(BF16) for tl in tiles], axis=1))
            alphas.append(jnp.exp2(m_old - m_new))
            m_ref[rows, :] = m_new
        pv = jnp.dot(jnp.concatenate(ps, axis=0), v_ref[0, pl.ds(k0, SEL_CHUNK), :], preferred_element_type=F32)
        acc_ref[...] = jnp.concatenate(alphas, axis=0) * acc_ref[...] + pv

    scores(0, sa_ref)

    def body(cc, carry):
        c = 2 * cc
        scores(c + 1, sb_ref)
        accumulate(c, sa_ref)
        scores(jnp.minimum(c + 2, n_chunks - 1), sa_ref)
        accumulate(c + 1, sb_ref)
        return carry

    lax.fori_loop(0, n_chunks // 2, body, 0)

    @pl.when(n_chunks % 2 == 1)
    def _():
        accumulate(n_chunks - 1, sa_ref)

    acc = acc_ref[...]
    rolled = pltpu.roll(acc, HEAD_DIM, 1)
    for w in range(SEL_QTILES):
        for t2 in range(hpc // 2):
            ev = slice(w * rpt + 2 * t2 * QBLK, w * rpt + (2 * t2 + 1) * QBLK)
            od = slice(w * rpt + (2 * t2 + 1) * QBLK, w * rpt + (2 * t2 + 2) * QBLK)
            even = acc[ev] * (1.0 / rolled[ev])
            odd = rolled[od] * (1.0 / acc[od])
            o_ref[0, w * QBLK:(w + 1) * QBLK, t2 * LANES:(t2 + 1) * LANES] = jnp.where(lo, even, odd)


def sel_attention(q, kx, vx, pen, bias_tiles):
    b, s, _ = q.shape
    hpc = N_HEADS // C_KV_HEADS
    gw = hpc * HEAD_DIM
    nbt = bias_tiles.shape[0] - 1
    tq = SEL_QTILES * QBLK
    rows = SEL_QTILES * hpc * QBLK
    assert s % SEL_CHUNK == 0 and s % tq == 0
    return pl.pallas_call(
        functools.partial(_sel_attn_kernel, nbt=nbt),
        grid=(b, C_KV_HEADS, s // tq),
        in_specs=[pl.BlockSpec((1, tq, gw), lambda bb, g, i: (bb, i, g)),
                  pl.BlockSpec((1, tq, LANES), lambda bb, g, i: (bb, i, g)),
                  pl.BlockSpec((1, s, LANES), lambda bb, g, i: (bb, 0, g)),
                  pl.BlockSpec((1, s, LANES), lambda bb, g, i: (bb, 0, g)),
                  pl.BlockSpec((nbt + 1, hpc, QBLK, QBLK), lambda bb, g, i: (0, g, 0, 0))],
        out_specs=pl.BlockSpec((1, tq, gw), lambda bb, g, i: (bb, i, g)),
        out_shape=jax.ShapeDtypeStruct((b, s, HQ), F32),
        scratch_shapes=[pltpu.VMEM((rows, LANES), BF16),
                        pltpu.VMEM((rows, LANES), F32),
                        pltpu.VMEM((rows, LANES), F32),
                        pltpu.VMEM((rows, SEL_CHUNK), F32),
                        pltpu.VMEM((rows, SEL_CHUNK), F32)],
        compiler_params=_cparams("parallel", "parallel", "arbitrary"),
        name="nsa_sel_attention",
    )(q, pen, kx, vx, bias_tiles)


def _sel_bias_idx(s):
    nqt = s // QBLK
    far = -(-(int(np.argmax(_t5_bucket_np(np.arange(4 * MAX_DISTANCE)) == NUM_BUCKETS - 1)) + QBLK) // QBLK)
    nbt = min(nqt, far + 1)
    d0 = np.arange(nbt)[:, None, None] * QBLK
    dist = d0 + np.arange(QBLK)[None, :, None] - np.arange(QBLK)[None, None, :]
    idx = np.where(dist >= 0, _t5_bucket_np(dist), -1).astype(np.int32)
    return np.concatenate([idx, np.full((1, QBLK, QBLK), -1, np.int32)])


def _ffn_kernel(h_ref, g2_ref, wu_ref, cw_ref, cb_ref, wd_ref, g3_ref,
                o_ref, xn_ref, acc_ref, u_ref, a_ref, carry_ref, *, tiles_per_seq, nj, lag):
    i = pl.program_id(0)
    ns, ts, tn = xn_ref.shape[0], xn_ref.shape[1], a_ref.shape[3]
    cols = lambda c: slice(c * tn, (c + 1) * tn)

    @pl.when(i % tiles_per_seq == 0)
    def _():
        carry_ref[...] = jnp.zeros(carry_ref.shape, F32)

    top = 16
    rowt = lax.broadcasted_iota(jnp.int32, (top, tn), 0)

    def up(s, j):
        xn = xn_ref[s]
        u_ref[s, j % 2, 0] = jnp.dot(xn, wu_ref[:, cols(j)], preferred_element_type=F32)
        u_ref[s, j % 2, 1] = jnp.dot(xn, wu_ref[:, cols(nj + j)], preferred_element_type=F32)

    def conv(u, s1, s2, c):
        cw = cw_ref[:, cols(c)]
        return ((cb_ref[:, cols(c)] + u * cw[2:3]) + s2 * cw[0:1]) + s1 * cw[1:2]

    def conv_body(u, c):
        return conv(u, pltpu.roll(u, 1, 0), pltpu.roll(u, 2, 0), c)

    def conv_top(uc_ref, c):
        prev = carry_ref[c]
        p1, p2 = prev[7:8], prev[6:7]
        u = uc_ref[0:top]
        s1 = jnp.where(rowt == 0, p1, pltpu.roll(u, 1, 0))
        s2 = jnp.where(rowt == 0, p2, jnp.where(rowt == 1, p1, pltpu.roll(u, 2, 0)))
        carry_ref[c] = uc_ref[ts - 8:ts]
        return conv(u, s1, s2, c)

    def gated(cg, cv):
        return (jax.nn.gelu(cg, approximate=True) * cv).astype(BF16)

    def act(s, j):
        ur, ar = u_ref.at[s, j % 2], a_ref.at[s, j % 2]
        ar[...] = gated(conv_body(ur[0], j), conv_body(ur[1], nj + j))
        ar[0:top] = gated(conv_top(ur.at[0], j), conv_top(ur.at[1], nj + j))

    def down(s, j):
        acc_ref[s] += jnp.dot(a_ref[s, j % 2], wd_ref[j * tn:(j + 1) * tn, :], preferred_element_type=F32)

    def iteration(s, k):
        rows = slice(s * ts, (s + 1) * ts)
        if k == -1:
            xn_ref[s] = _rms(h_ref[rows, :], g2_ref[...]).astype(xn_ref.dtype)
            acc_ref[s] = jnp.zeros(acc_ref.shape[1:], F32)
        if 0 <= k + 1 < nj:
            up(s, k + 1)
        if 0 <= k < nj:
            act(s, k)
        if 0 <= k - 1 < nj:
            down(s, k - 1)
        if k == nj:
            o_ref[rows, :] = h_ref[rows, :] + _rms(acc_ref[s], g3_ref[...])

    for slot in range(-1, nj + 1 + (ns - 1) * lag):
        for s in range(ns):
            k = slot - s * lag
            if -1 <= k <= nj:
                iteration(s, k)


def conv_ffn(h, g2, w_up, conv_w, conv_b, w_down, g3, seq, tm, tn, ns=FFN_SUBTILES, lag=FFN_LAG):
    t, d = h.shape
    dff = w_down.shape[0]
    nj = dff // tn
    ts = tm // ns
    assert nj * tn == dff and ts * ns == tm and ts % 16 == 0
    wu, cw, cb, wd = w_up, conv_w, conv_b.reshape(1, -1), w_down
    resident = lambda a: pl.BlockSpec(a.shape, (lambda nd: (lambda i: (0,) * nd))(a.ndim))
    return pl.pallas_call(
        functools.partial(_ffn_kernel, tiles_per_seq=seq // tm, nj=nj, lag=lag),
        grid=(t // tm,),
        in_specs=[pl.BlockSpec((tm, d), lambda i: (i, 0)),
                  pl.BlockSpec((1, d), lambda i: (0, 0)),
                  resident(wu), resident(cw), resident(cb), resident(wd),
                  pl.BlockSpec((1, d), lambda i: (0, 0))],
        out_specs=pl.BlockSpec((tm, d), lambda i: (i, 0)),
        out_shape=jax.ShapeDtypeStruct((t, d), F32),
        scratch_shapes=[pltpu.VMEM((ns, ts, d), BF16), pltpu.VMEM((ns, ts, d), F32),
                        pltpu.VMEM((ns, 2, 2, ts, tn), F32), pltpu.VMEM((ns, 2, ts, tn), BF16),
                        pltpu.VMEM((2 * nj, 8, tn), F32)],
        compiler_params=_cparams("arbitrary"),
        name="conv_ffn",
    )(h, g2.reshape(1, d), wu, cw, cb, wd, g3.reshape(1, d))


def _row_tile(t):
    for tm in (1024, 512, 256, 128):
        if t % tm == 0:
            return tm
    raise ValueError(f"token count {t} is not a multiple of 128")


def mixer_a(h, gains, w_in, w_o, rel_table, bsz, seq):
    t, d = h.shape
    tm = _row_tile(t)
    n_dil = len(DIL_CONFIGS)
    a_in = w_in.shape[1]
    col_scale = np.ones((a_in,), np.float32).reshape(n_dil, 3, HQ)
    col_scale[:, 0] = Q_SCALE
    w = (w_in * col_scale.reshape(1, a_in)).astype(BF16)
    idx = np.concatenate([_banded_idx(1, window // dil, dil) for window, dil in DIL_CONFIGS])
    bias = bias_expand(rel_table, idx)
    outs, lses = [], []
    dils = [dil for _, dil in DIL_CONFIGS]
    blocks_per_seq = seq // QBLK
    xns = norm_permute(h, gains[0], dils, max([tm] + [QBLK * dil for dil in dils]))
    for gi, (window, dil) in enumerate(DIL_CONFIGS):
        assert window // dil <= QBLK and seq % (QBLK * dil) == 0
        qkv = matmul_resident(xns[gi], w[:, gi * 3 * HQ:(gi + 1) * 3 * HQ], BF16, tm)
        qkv = qkv.reshape(1, t, 3 * HQ)

        def rmap(part, dil=dil):
            return lambda n, i: (0, (n // dil) * blocks_per_seq + i * dil + n % dil, part)

        o, lse = banded_attention(
            qkv, qkv, qkv, bias[2 * gi:2 * gi + 2], n_rows=bsz * dil, n_tiles=seq // (QBLK * dil), tq=QBLK, n_prev=1, kw=HQ,
            q_map=rmap(0), k_map=rmap(1), v_map=rmap(2), o_map=rmap(0),
            out_shape=(1, t, HQ), lse_shape=(1, t, LANES), name=f"dilated_attention_{dil}")
        outs.append(o.reshape(t, HQ))
        lses.append(lse.reshape(t, LANES))
    e = jnp.asarray(_head_expand_matrix(), BF16)
    return combine_proj_a(outs, lses, dils, e, w_o.astype(BF16), gains[1], h, min(tm, 512))


def mixer_b(h, gains, w_in, sinks, w_o, rel_table, bsz, seq):
    t, d = h.shape
    tm = _row_tile(t)
    hk = B_KV_HEADS * HEAD_DIM
    n_in = w_in.shape[1]
    assert n_in == HQ + 2 * hk
    w = jnp.concatenate([w_in[:, :HQ] * Q_SCALE, _dup_groups(w_in[:, HQ:HQ + hk]),
                         _dup_groups(w_in[:, HQ + hk:])], axis=1).astype(BF16)
    kw = 2 * hk
    qkv = norm_matmul(h, gains[0], w, BF16, tm, HQ + 2 * kw).reshape(bsz, seq, HQ + 2 * kw)
    bias = bias_expand(rel_table, _banded_idx(1, B_WINDOW - 1, 1))
    sink_rows = sinks.astype(F32) * LOG2E
    ident = lambda n, i: (n, i, 0)
    col = lambda c: (lambda n, i: (n, i, c))
    (o,) = banded_attention(qkv, qkv, qkv, bias, n_rows=bsz, n_tiles=seq // QBLK, tq=QBLK, n_prev=1,
                            kw=kw, q_map=ident, k_map=col(HQ // kw), v_map=col(HQ // kw + 1), o_map=ident,
                            out_shape=(bsz, seq, HQ), sinks=sink_rows, stack=8, name="sink_window_attention")
    return _proj_call(_proj_kernel, "sink_proj", tm, [o.reshape(t, HQ)], [], w_o.astype(BF16), gains[1], h)


def mixer_c(h, gains, w_in, cmp_pos, cmp_w1, cmp_w2, w_o, rel_table, bsz, seq):
    t, d = h.shape
    tm = _row_tile(t)
    g = C_KV_HEADS
    hk = g * HEAD_DIM
    wkv = [w_in[:, HQ + i * hk:HQ + (i + 1) * hk] for i in range(6)]
    n_gate = w_in.shape[1] - HQ - 6 * hk
    assert n_gate == 3 * N_HEADS and hk == LANES
    wq = (w_in[:, :HQ] * Q_SCALE).astype(BF16)
    spread = lambda w: jnp.concatenate([w[:, :HEAD_DIM], jnp.zeros_like(w[:, :HEAD_DIM]),
                                        w[:, HEAD_DIM:], jnp.zeros_like(w[:, :HEAD_DIM])], axis=1)
    wb = jnp.concatenate([spread(wkv[2]), spread(wkv[3]), _dup_groups(wkv[4]), _dup_groups(wkv[5])],
                         axis=1).astype(BF16)
    wf = jnp.concatenate([wkv[0], wkv[1], jnp.pad(w_in[:, HQ + 6 * hk:], ((0, 0), (0, LANES - n_gate)))],
                         axis=1).astype(BF16)
    kw = 2 * LANES
    q = norm_matmul(h, gains[0], jnp.concatenate([wq, wb], axis=1), BF16, tm, HQ + 4 * kw)
    q = q.reshape(bsz, seq, HQ + 4 * kw)
    kvb = q[..., HQ:HQ + 2 * kw]
    r = norm_matmul(h, gains[0], wf, F32, tm, 3 * LANES)
    kv = [r[:, i * hk:(i + 1) * hk].reshape(bsz, seq, hk) for i in range(2)]
    gates_raw = r[:, 2 * hk:2 * hk + LANES]

    ncr = seq // CMP_STRIDE
    half = CMP_BLOCK // 2
    assert half == CMP_STRIDE
    chunks = jnp.stack([kv[0], kv[1]]).reshape(2, bsz, ncr, half, g, HEAD_DIM)
    chunks = chunks.transpose(0, 1, 4, 2, 3, 5).reshape(2, bsz, g, ncr, half * HEAD_DIM)
    pos = cmp_pos.reshape(2, 2, half * HEAD_DIM)
    w1 = cmp_w1.reshape(2, 2, half * HEAD_DIM, -1).astype(BF16)
    cmp = compress(chunks, pos, w1, cmp_w2.astype(BF16))
    cmp = cmp.transpose(0, 1, 3, 2, 4).reshape(2, bsz, ncr, hk).astype(BF16)
    kcd, vcd = _dup_groups(cmp[0]), _dup_groups(cmp[1])

    ns = seq // SEL_BLOCK
    assert ns <= HEAD_DIM
    k_sel = min(N_SELECT, ns)
    starts = np.arange(ncr) * CMP_STRIDE
    blk = np.arange(HEAD_DIM)
    ov = ((starts[None, :] < (blk[:, None] + 1) * SEL_BLOCK)
          & (starts[None, :] + CMP_BLOCK > blk[:, None] * SEL_BLOCK)
          & (blk[:, None] < ns) & (starts[None, :] + CMP_BLOCK <= seq))
    o_c, pen = cmp_attention(q, kcd, vcd, jnp.asarray(ov.astype(np.float32), BF16), ns, k_sel)

    sel_bias = bias_expand(rel_table, _sel_bias_idx(seq))
    lane = np.arange(kw)[None, :]
    key_blk = np.arange(seq)[:, None] // SEL_BLOCK
    onehot = np.where((lane % LANES >= HEAD_DIM) & (lane % HEAD_DIM == key_blk), -NEG_INF, 0.0)
    ones = np.broadcast_to(np.where(lane % LANES >= HEAD_DIM, 1.0, 0.0), (seq, kw))
    kx = kvb[..., :kw] + jnp.asarray(onehot, BF16)[None]
    vx = kvb[..., kw:2 * kw] + jnp.asarray(ones, BF16)[None]
    o_s = sel_attention(q, kx, vx, pen, sel_bias)

    n_prev = -(-(C_WINDOW - 1) // QBLK)
    tqw = n_prev * QBLK
    wbias = bias_expand(rel_table, _banded_idx(n_prev, C_WINDOW - 1, 1))
    ident = lambda n, i: (n, i, 0)
    col = lambda c: (lambda n, i: (n, i, c))
    (o_w,) = banded_attention(q, q, q, wbias,
                              n_rows=bsz, n_tiles=seq // tqw, tq=tqw, n_prev=n_prev, kw=kw,
                              q_map=ident, k_map=col(HQ // kw + 2), v_map=col(HQ // kw + 3), o_map=ident,
                              out_shape=(bsz, seq, HQ), stack=8, name="nsa_window_attention")

    e = jnp.asarray(_head_expand_matrix(), BF16)
    return _proj_call(_proj_c_kernel, "nsa_gate_proj", min(tm, 512),
                      [o_c.reshape(t, HQ), o_s.reshape(t, HQ), o_w.reshape(t, HQ), gates_raw], [e],
                      w_o.astype(BF16), gains[1], h)


def kernel(x, rel_table, norm_gains, a_w_in, a_w_o, b_w_in, b_sinks, b_w_o, c_w_in, c_cmp_pos, c_cmp_w1,
           c_cmp_w2, c_w_o, ffn_w_up, ffn_conv_w, ffn_conv_b, ffn_w_down):
    bsz, seq, d = x.shape
    depth = norm_gains.shape[0]
    h = x.reshape(bsz * seq, d)
    tm = _row_tile(seq)
    for i in range(depth):
        kind, j = i % 3, i // 3
        g = norm_gains[i]
        if kind == 0:
            h = mixer_a(h, g, a_w_in[j], a_w_o[j], rel_table, bsz, seq)
        elif kind == 1:
            h = mixer_b(h, g, b_w_in[j], b_sinks[j], b_w_o[j], rel_table, bsz, seq)
        else:
            h = mixer_c(h, g, c_w_in[j], c_cmp_pos[j], c_cmp_w1[j], c_cmp_w2[j], c_w_o[j], rel_table, bsz, seq)
        h = conv_ffn(h, g[2], ffn_w_up[i].astype(BF16), ffn_conv_w[i], ffn_conv_b[i],
                     ffn_w_down[i].astype(BF16), g[3], seq, min(tm, FFN_ROWS), FFN_COLS)
    return h.reshape(bsz, seq, d)
```

```python
import functools
import math

import numpy as np
import jax
import jax.numpy as jnp
from jax import lax
from jax.experimental import pallas as pl
from jax.experimental.pallas import tpu as pltpu

F32 = jnp.float32
BF16 = jnp.bfloat16

N_HEADS = 16
HEAD_DIM = 64
HQ = N_HEADS * HEAD_DIM
LANES = 128
ATTN_SCALE = HEAD_DIM ** -0.5
NUM_BUCKETS = 32
MAX_DISTANCE = 2048
RMS_EPS = 1e-6
NEG_INF = -1e30
FORCE_SCORE = 1e9
DIL_CONFIGS = ((128, 1), (512, 4), (2048, 16))
B_KV_HEADS = 2
B_WINDOW = 128
C_KV_HEADS = 2
CMP_BLOCK = 32
CMP_STRIDE = 16
SEL_BLOCK = 64
N_SELECT = 16
C_WINDOW = 512
CONV_WIDTH = 3
QBLK = 128
BAND_ROWS = 256
PROJ_ROWS = 512
SEL_CHUNK = 512
SEL_QTILES = 2
FFN_COLS = 256
FFN_ROWS = 512
FFN_SUBTILES = 2
FFN_LAG = 0
VMEM_LIMIT = 56 * 1024 * 1024


def _cparams(*sem):
    return pltpu.CompilerParams(dimension_semantics=sem, vmem_limit_bytes=VMEM_LIMIT)


def _t5_bucket_np(dist):
    max_exact = NUM_BUCKETS // 2
    d = np.maximum(dist, 0)
    df = np.maximum(d, 1).astype(np.float64)
    large = max_exact + np.floor(np.log(df / max_exact) / math.log(MAX_DISTANCE / max_exact)
                                 * (NUM_BUCKETS - max_exact) + 1e-9).astype(np.int64)
    large = np.minimum(large, NUM_BUCKETS - 1)
    return np.where(d < max_exact, d, large).astype(np.int32)


def _rms(x, g):
    ms = jnp.mean(x * x, axis=-1, keepdims=True)
    return (x * lax.rsqrt(ms + RMS_EPS)) * g


def _split3(w):
    hi = w.astype(BF16)
    r1 = w - hi.astype(F32)
    mid = r1.astype(BF16)
    lo = (r1 - mid.astype(F32)).astype(BF16)
    return hi, mid, lo


def _expand_heads(w, e3):
    lane = lax.broadcasted_iota(jnp.int32, w.shape, 1)
    r1 = w - w.astype(BF16).astype(F32)
    r2 = r1 - r1.astype(BF16).astype(F32)
    packed = jnp.where(lane < N_HEADS, w,
                       jnp.where(lane < 2 * N_HEADS, pltpu.roll(r1, N_HEADS, 1), pltpu.roll(r2, 2 * N_HEADS, 1)))
    return jnp.dot(packed.astype(BF16), e3, preferred_element_type=F32)


def _dot_nt(a, b):
    return lax.dot_general(a, b, (((1,), (1,)), ((), ())), preferred_element_type=F32)


def _stack_heads(q_tile, heads):
    lane = lax.broadcasted_iota(jnp.int32, (QBLK, LANES), 1)
    lo = lane < HEAD_DIM
    pieces = []
    for h in heads:
        q2 = q_tile(h // 2)
        keep = lo if h % 2 == 0 else jnp.logical_not(lo)
        pieces.append(jnp.where(keep, q2, jnp.zeros_like(q2)))
    return jnp.concatenate(pieces, axis=0)


def _merge_pairs(o, n_heads):
    lane = lax.broadcasted_iota(jnp.int32, (QBLK, LANES), 1)
    lo = lane < HEAD_DIM
    out = []
    for t in range(0, n_heads, 2):
        out.append(jnp.where(lo, o[t * QBLK:(t + 1) * QBLK], o[(t + 1) * QBLK:(t + 2) * QBLK]))
    return out


def _bias_kernel(tab_ref, rng_ref, idx_ref, o_ref):
    n = pl.program_id(0)
    idx = idx_ref[0]
    o_ref[...] = jnp.full(o_ref.shape, NEG_INF, F32)

    def body(b, carry):
        hit = idx == b
        for h in range(N_HEADS):
            o_ref[0, h] = jnp.where(hit, tab_ref[b, h], o_ref[0, h])
        return carry

    lax.fori_loop(rng_ref[n, 0], rng_ref[n, 1] + 1, body, 0)


def bias_expand(table, idx):
    n, r, c = idx.shape
    flat = idx.reshape(n, -1)
    lo = np.where(flat >= 0, flat, NUM_BUCKETS).min(axis=1)
    rng = np.stack([lo, flat.max(axis=1)], axis=1).astype(np.int32)
    return pl.pallas_call(
        _bias_kernel,
        grid=(n,),
        in_specs=[pl.BlockSpec(memory_space=pltpu.SMEM),
                  pl.BlockSpec(memory_space=pltpu.SMEM),
                  pl.BlockSpec((1, r, c), lambda i: (i, 0, 0))],
        out_specs=pl.BlockSpec((1, N_HEADS, r, c), lambda i: (i, 0, 0, 0)),
        out_shape=jax.ShapeDtypeStruct((n, N_HEADS, r, c), F32),
        compiler_params=_cparams("parallel"),
        name="bias_expand",
    )(table, jnp.asarray(rng), jnp.asarray(idx))


def _banded_idx(n_prev, max_dist, stride):
    kb = (n_prev + 1) * QBLK
    dist = np.arange(QBLK)[:, None] + n_prev * QBLK - np.arange(kb)[None, :]
    valid = (dist >= 0) & (dist <= max_dist)
    idx = np.where(valid, _t5_bucket_np(dist * stride), -1).astype(np.int32)
    if n_prev > 1:
        return idx[None]
    first = np.where(np.arange(kb)[None, :] < n_prev * QBLK, -1, idx).astype(np.int32)
    return np.stack([idx, first])


def _norm_mm_kernel(x_ref, g_ref, w_ref, o_ref, xn_ref):
    @pl.when(pl.program_id(1) == 0)
    def _():
        xn_ref[...] = _rms(x_ref[...], g_ref[...]).astype(xn_ref.dtype)

    o_ref[...] = jnp.dot(xn_ref[...], w_ref[...], preferred_element_type=F32).astype(o_ref.dtype)


def norm_matmul(x, gain, w, out_dtype, tm, tn):
    t, d = x.shape
    n = w.shape[1]
    scratch = [pltpu.VMEM((tm, d), BF16)]
    return pl.pallas_call(
        _norm_mm_kernel,
        grid=(t // tm, n // tn),
        in_specs=[pl.BlockSpec((tm, d), lambda i, j: (i, 0)),
                  pl.BlockSpec((1, d), lambda i, j: (0, 0)),
                  pl.BlockSpec((d, tn), lambda i, j: (0, j))],
        out_specs=pl.BlockSpec((tm, tn), lambda i, j: (i, j)),
        out_shape=jax.ShapeDtypeStruct((t, n), out_dtype),
        scratch_shapes=scratch,
        compiler_params=_cparams("parallel", "arbitrary"),
        name="norm_matmul",
    )(x, gain.reshape(1, d), w)


def _norm_perm_kernel(x_ref, g_ref, *refs, dils):
    o_refs, xs_ref = refs[:len(dils)], refs[len(dils)]
    xn = _rms(x_ref[...], g_ref[...])
    nc = xs_ref.shape[0]
    for c in range(nc):
        xs_ref[c] = xn[:, c * LANES:(c + 1) * LANES]
    for o_ref, dil in zip(o_refs, dils):
        if dil == 1:
            o_ref[...] = xn.astype(o_ref.dtype)
            continue
        span = QBLK * dil
        for u in range(x_ref.shape[0] // span):
            for r in range(dil):
                rows = jnp.concatenate(
                    [xs_ref[c, pl.ds(u * span + r, QBLK, stride=dil), :] for c in range(nc)], axis=1)
                dst = u * span + r * QBLK
                o_ref[dst:dst + QBLK, :] = rows.astype(o_ref.dtype)


def norm_permute(x, gain, dils, tm):
    t, d = x.shape
    assert all(tm % (QBLK * dil) == 0 for dil in dils)
    return pl.pallas_call(
        functools.partial(_norm_perm_kernel, dils=tuple(dils)),
        grid=(t // tm,),
        in_specs=[pl.BlockSpec((tm, d), lambda i: (i, 0)),
                  pl.BlockSpec((1, d), lambda i: (0, 0))],
        out_specs=[pl.BlockSpec((tm, d), lambda i: (i, 0)) for _ in dils],
        out_shape=[jax.ShapeDtypeStruct((t, d), BF16) for _ in dils],
        scratch_shapes=[pltpu.VMEM((d // LANES, tm, LANES), F32)],
        compiler_params=_cparams("parallel"),
        name="norm_permute",
    )(x, gain.reshape(1, d))


def _mm_kernel(x_ref, w_ref, o_ref):
    o_ref[...] = jnp.dot(x_ref[...], w_ref[...], preferred_element_type=F32).astype(o_ref.dtype)


def matmul_resident(x, w, out_dtype, tm):
    t, k = x.shape
    n = w.shape[1]
    return pl.pallas_call(
        _mm_kernel,
        grid=(t // tm,),
        in_specs=[pl.BlockSpec((tm, k), lambda i: (i, 0)),
                  pl.BlockSpec((k, n), lambda i: (0, 0))],
        out_specs=pl.BlockSpec((tm, n), lambda i: (i, 0)),
        out_shape=jax.ShapeDtypeStruct((t, n), out_dtype),
        compiler_params=_cparams("parallel"),
        name="matmul_resident",
    )(x, w)


def _finish_proj(y, w_ref, g_ref, h_ref, o_ref):
    z = jnp.dot(y.astype(BF16), w_ref[...], preferred_element_type=F32)
    o_ref[...] = h_ref[...] + _rms(z, g_ref[...])


def _proj_kernel(y_ref, w_ref, g_ref, h_ref, o_ref):
    _finish_proj(y_ref[...], w_ref, g_ref, h_ref, o_ref)


def _unpermute(src_ref, dst_ref, dil, tm, i):
    span = QBLK * dil
    nc = dst_ref.shape[0]
    if span <= tm:
        for u in range(tm // span):
            for r in range(dil):
                lo = u * span + r * QBLK
                for c in range(nc):
                    dst_ref[c, pl.ds(u * span + r, QBLK, stride=dil), :] = src_ref[lo:lo + QBLK,
                                                                                   c * LANES:(c + 1) * LANES]
    else:
        per = tm // dil
        off = (i % (span // tm)) * per
        for r in range(dil):
            lo = pl.multiple_of(r * QBLK + off, 8)
            for c in range(nc):
                dst_ref[c, pl.ds(r, per, stride=dil), :] = src_ref[pl.ds(lo, per), c * LANES:(c + 1) * LANES]
    return jnp.concatenate([dst_ref[c] for c in range(nc)], axis=1) if nc > 1 else dst_ref[0]


def _proj_a_kernel(*refs, dils, tm):
    n = len(dils)
    o_refs, l_refs = refs[:n], refs[n:2 * n]
    e_ref, w_ref, g_ref, h_ref, out_ref = refs[2 * n:2 * n + 5]
    scratch = refs[2 * n + 5:]
    i = pl.program_id(0)
    outs, lses = [], []
    si = 0
    for gi, dil in enumerate(dils):
        if dil == 1:
            outs.append(o_refs[gi][...])
            lses.append(l_refs[gi][...])
        else:
            outs.append(_unpermute(o_refs[gi], scratch[si], dil, tm, i))
            lses.append(_unpermute(l_refs[gi], scratch[si + 1], dil, tm, i))
            si += 2
    mx = functools.reduce(jnp.maximum, lses)
    es = [jnp.exp(l - mx) for l in lses]
    inv = 1.0 / functools.reduce(lambda a, b: a + b, es)
    e = e_ref[...]
    y = outs[0] * _expand_heads(es[0] * inv, e)
    for gi in range(1, n):
        y = y + outs[gi] * _expand_heads(es[gi] * inv, e)
    _finish_proj(y, w_ref, g_ref, h_ref, out_ref)


def combine_proj_a(outs, lses, dils, e, w, gain, h, tm):
    t, d = h.shape

    def row_spec(width, dil):
        rows = max(tm, QBLK * dil)
        return pl.BlockSpec((rows, width), lambda i, q=rows // tm: (i // q, 0))

    in_specs = [row_spec(HQ, dil) for dil in dils] + [row_spec(LANES, dil) for dil in dils] + [
        pl.BlockSpec(e.shape, lambda i: (0, 0)),
        pl.BlockSpec(w.shape, lambda i: (0, 0)),
        pl.BlockSpec((1, d), lambda i: (0, 0)),
        pl.BlockSpec((tm, d), lambda i: (i, 0))]
    scratch = []
    for dil in dils:
        if dil > 1:
            scratch += [pltpu.VMEM((HQ // LANES, tm, LANES), F32), pltpu.VMEM((1, tm, LANES), F32)]
    return pl.pallas_call(
        functools.partial(_proj_a_kernel, dils=tuple(dils), tm=tm),
        grid=(t // tm,),
        in_specs=in_specs,
        out_specs=pl.BlockSpec((tm, d), lambda i: (i, 0)),
        out_shape=jax.ShapeDtypeStruct((t, d), F32),
        scratch_shapes=scratch,
        compiler_params=_cparams("arbitrary"),
        name="dilated_combine_proj",
    )(*outs, *lses, e, w, gain.reshape(1, d), h)


def _proj_c_kernel(oc_ref, os_ref, ow_ref, gr_ref, e_ref, w_ref, g_ref, h_ref, o_ref):
    sig = jax.nn.sigmoid(gr_ref[...])
    e = e_ref[...]
    gate = lambda i: _expand_heads(sig if i == 0 else pltpu.roll(sig, LANES - i * N_HEADS, 1), e)
    y = gate(0) * oc_ref[...]
    y = y + gate(1) * os_ref[...]
    y = y + gate(2) * ow_ref[...]
    _finish_proj(y, w_ref, g_ref, h_ref, o_ref)


def _proj_call(kernel, name, tm, row_inputs, const_inputs, w, gain, h):
    t, d = h.shape
    k = w.shape[0]
    row_specs = [pl.BlockSpec((tm, a.shape[1]), lambda i: (i, 0)) for a in row_inputs]
    const_specs = [pl.BlockSpec(a.shape, (lambda nd: (lambda i: (0,) * nd))(a.ndim)) for a in const_inputs]
    return pl.pallas_call(
        kernel,
        grid=(t // tm,),
        in_specs=row_specs + const_specs + [
            pl.BlockSpec((k, d), lambda i: (0, 0)),
            pl.BlockSpec((1, d), lambda i: (0, 0)),
            pl.BlockSpec((tm, d), lambda i: (i, 0))],
        out_specs=pl.BlockSpec((tm, d), lambda i: (i, 0)),
        out_shape=jax.ShapeDtypeStruct((t, d), F32),
        compiler_params=_cparams("parallel"),
        name=name,
    )(*row_inputs, *const_inputs, w, gain.reshape(1, d), h)


def _head_expand_matrix():
    e = np.zeros((LANES, HQ), np.float32)
    for term in range(3):
        for h in range(N_HEADS):
            e[term * N_HEADS + h, h * HEAD_DIM:(h + 1) * HEAD_DIM] = 1.0
    return e


def _banded_kernel(*refs, tq, n_prev, n_chunks, stack, with_sinks, with_lse):
    q_ref, kp_ref, kc_ref, vp_ref, vc_ref, bias_ref = refs[:6]
    pos = 6
    sink_ref = None
    if with_sinks:
        sink_ref = refs[pos]
        pos += 1
    o_ref = refs[pos]
    pos += 1
    lse_ref = None
    if with_lse:
        lse_ref = refs[pos]
        pos += 1
    kbuf, vbuf = refs[pos], refs[pos + 1]

    i = pl.program_id(1)
    hpc = N_HEADS // n_chunks
    kb = (n_prev + 1) * QBLK
    kbuf[0:tq] = kp_ref[0]
    kbuf[tq:2 * tq] = kc_ref[0]
    vbuf[0:tq] = vp_ref[0]
    vbuf[tq:2 * tq] = vc_ref[0]
    lane = lax.broadcasted_iota(jnp.int32, (QBLK, LANES), 1)
    head_row = lax.broadcasted_iota(jnp.int32, (stack * QBLK, 1), 0)

    for s in range(tq // QBLK):
        r0 = s * QBLK
        k0 = tq + r0 - n_prev * QBLK
        variants = bias_ref.shape[0] > 1
        if variants:
            first = jnp.where(i == 0, 1, 0) if s == 0 else 0
        else:
            col = lax.broadcasted_iota(jnp.int32, (1, kb), 1) + k0
            kmask = jnp.where(jnp.logical_and(i == 0, col < tq), NEG_INF, 0.0).astype(F32)
        lse_acc = jnp.zeros((QBLK, LANES), F32)
        for h0 in range(0, N_HEADS, stack):
            heads = list(range(h0, h0 + stack))
            c = h0 // hpc
            qst = _stack_heads(lambda p: q_ref[0, r0:r0 + QBLK, p * LANES:(p + 1) * LANES], heads)
            kx = kbuf[k0:k0 + kb, c * LANES:(c + 1) * LANES]
            vx = vbuf[k0:k0 + kb, c * LANES:(c + 1) * LANES]
            sc = _dot_nt(qst, kx)
            if variants:
                sc = sc + bias_ref[first, h0:h0 + stack].reshape(stack * QBLK, kb)
            else:
                sc = sc + bias_ref[0, h0:h0 + stack].reshape(stack * QBLK, kb) + kmask
            m = jnp.max(sc, axis=-1, keepdims=True)
            if with_sinks:
                sk = sink_ref[h0 + stack - 1]
                for t in range(stack - 2, -1, -1):
                    sk = jnp.where(head_row < (t + 1) * QBLK, sink_ref[h0 + t], sk)
                m = jnp.maximum(m, sk)
            p = jnp.exp(sc - m)
            den = jnp.sum(p, axis=-1, keepdims=True)
            norm = den + jnp.exp(sk - m) if with_sinks else den
            o = jnp.dot(p.astype(BF16), vx, preferred_element_type=F32) * (1.0 / norm)
            for t2, blk in enumerate(_merge_pairs(o, stack)):
                pidx = h0 // 2 + t2
                o_ref[0, r0:r0 + QBLK, pidx * LANES:(pidx + 1) * LANES] = blk.astype(o_ref.dtype)
            if with_lse:
                lse = m + jnp.log(den)
                for t, h in enumerate(heads):
                    lse_acc = jnp.where(lane == h, lse[t * QBLK:(t + 1) * QBLK], lse_acc)
        if with_lse:
            lse_ref[0, r0:r0 + QBLK, :] = lse_acc


def banded_attention(q_arr, k_arr, v_arr, bias, *, n_rows, n_tiles, tq, n_prev, kw,
                     q_map, k_map, v_map, o_map, out_shape, lse_shape=None, sinks=None, stack=2, name):
    n_chunks = kw // LANES
    assert stack % 2 == 0 and (N_HEADS // n_chunks) % stack == 0
    kb = (n_prev + 1) * QBLK
    with_sinks = sinks is not None
    with_lse = lse_shape is not None

    def prev(fn):
        return lambda n, i: fn(n, jnp.maximum(i - 1, 0))

    in_specs = [pl.BlockSpec((1, tq, HQ), q_map),
                pl.BlockSpec((1, tq, kw), prev(k_map)),
                pl.BlockSpec((1, tq, kw), k_map),
                pl.BlockSpec((1, tq, kw), prev(v_map)),
                pl.BlockSpec((1, tq, kw), v_map),
                pl.BlockSpec(bias.shape, lambda n, i: (0, 0, 0, 0))]
    assert bias.shape[1:] == (N_HEADS, QBLK, kb) and (bias.shape[0] == 1 or n_prev == 1)
    args = [q_arr, k_arr, k_arr, v_arr, v_arr, bias]
    if with_sinks:
        in_specs.append(pl.BlockSpec(memory_space=pltpu.SMEM))
        args.append(sinks)
    out_specs = [pl.BlockSpec((1, tq, HQ), o_map)]
    out_shapes = [jax.ShapeDtypeStruct(out_shape, F32)]
    if with_lse:
        out_specs.append(pl.BlockSpec((1, tq, LANES), o_map))
        out_shapes.append(jax.ShapeDtypeStruct(lse_shape, F32))
    res = pl.pallas_call(
        functools.partial(_banded_kernel, tq=tq, n_prev=n_prev, n_chunks=n_chunks, stack=stack,
                          with_sinks=with_sinks, with_lse=with_lse),
        grid=(n_rows, n_tiles),
        in_specs=in_specs,
        out_specs=out_specs,
        out_shape=out_shapes,
        scratch_shapes=[pltpu.VMEM((2 * tq, kw), BF16), pltpu.VMEM((2 * tq, kw), BF16)],
        compiler_params=_cparams("parallel", "arbitrary"),
        name=name,
    )(*args)
    return res


def _dup_groups(x):
    g0, g1 = x[..., :HEAD_DIM], x[..., HEAD_DIM:]
    return jnp.concatenate([g0, g0, g1, g1], axis=-1)


def _compress_kernel(ch_ref, pos_ref, w1_ref, w2_ref, o_ref):
    ch = ch_ref[0, 0, 0]
    rows = ch.shape[0]
    posv = pos_ref[0]
    a = jnp.dot((ch + posv[0:1]).astype(BF16), w1_ref[0, 0], preferred_element_type=F32)
    b = jnp.dot((ch + posv[1:2]).astype(BF16), w1_ref[0, 1], preferred_element_type=F32)
    hid = a + pltpu.roll(b, rows - 1, 0)
    act = jax.nn.gelu(hid, approximate=True)
    o_ref[0, 0, 0] = jnp.dot(act.astype(BF16), w2_ref[0], preferred_element_type=F32)


def compress(chunks, pos, w1, w2):
    _, b, g, rows, width = chunks.shape
    hid = w1.shape[-1]
    return pl.pallas_call(
        _compress_kernel,
        grid=(2, b, g),
        in_specs=[pl.BlockSpec((1, 1, 1, rows, width), lambda i, bb, gg: (i, bb, gg, 0, 0)),
                  pl.BlockSpec((1, 2, width), lambda i, bb, gg: (i, 0, 0)),
                  pl.BlockSpec((1, 2, width, hid), lambda i, bb, gg: (i, 0, 0, 0)),
                  pl.BlockSpec((1, hid, HEAD_DIM), lambda i, bb, gg: (i, 0, 0))],
        out_specs=pl.BlockSpec((1, 1, 1, rows, HEAD_DIM), lambda i, bb, gg: (i, bb, gg, 0, 0)),
        out_shape=jax.ShapeDtypeStruct((2, b, g, rows, HEAD_DIM), F32),
        compiler_params=_cparams("parallel", "parallel", "parallel"),
        name="nsa_compress",
    )(chunks, pos, w1, w2)


def _cmp_attn_kernel(q_ref, kc_ref, vc_ref, ov_ref, place_ref, o_ref, sel_ref, *, n_sel_blocks, k_sel):
    qi = pl.program_id(1)
    ncr = kc_ref.shape[1]
    hpc = N_HEADS // C_KV_HEADS
    qpos = qi * QBLK + lax.broadcasted_iota(jnp.int32, (QBLK, 1), 0)
    cidx = lax.broadcasted_iota(jnp.int32, (1, ncr), 1)
    valid = (cidx * CMP_STRIDE + (CMP_BLOCK - 1)) <= qpos
    maskc = jnp.where(valid, 0.0, NEG_INF).astype(F32)
    anyv = (qpos >= CMP_BLOCK - 1).astype(F32)
    blk_id = lax.broadcasted_iota(jnp.int32, (HEAD_DIM, QBLK), 0)
    cur = (qi * QBLK + lax.broadcasted_iota(jnp.int32, (HEAD_DIM, QBLK), 1)) // SEL_BLOCK
    forced = jnp.logical_or(jnp.logical_or(blk_id == 0, blk_id == cur), blk_id == cur - 1)
    allowed = blk_id <= cur
    for g in range(C_KV_HEADS):
        heads = list(range(g * hpc, (g + 1) * hpc))
        qst = _stack_heads(lambda p: q_ref[0, :, p * LANES:(p + 1) * LANES], heads)
        sc = _dot_nt(qst, kc_ref[0, :, g * LANES:(g + 1) * LANES])
        sc3 = sc.reshape(hpc, QBLK, ncr) + maskc[None]
        m = jnp.max(sc3, axis=-1, keepdims=True)
        e = jnp.exp(sc3 - m)
        ssum = jnp.sum(e, axis=-1, keepdims=True)
        p = (e * (1.0 / ssum)) * anyv[None]
        o = jnp.dot(p.reshape(hpc * QBLK, ncr).astype(BF16), vc_ref[0, :, g * LANES:(g + 1) * LANES],
                    preferred_element_type=F32)
        for t2, blk in enumerate(_merge_pairs(o, hpc)):
            pidx = (g * hpc) // 2 + t2
            o_ref[0, :, pidx * LANES:(pidx + 1) * LANES] = blk
        hi, mid, lo = _split3(jnp.sum(p, axis=0))
        ovt = ov_ref[...]
        imp = (_dot_nt(ovt, hi) + _dot_nt(ovt, mid)) + _dot_nt(ovt, lo)
        score = jnp.where(forced, FORCE_SCORE, jnp.where(allowed, imp, NEG_INF))
        bits = pltpu.bitcast(score, jnp.int32)
        key = jnp.where(bits < 0, bits ^ jnp.int32(0x7FFFFFFF), bits)
        key_m1 = key - 1
        rank = jnp.zeros((HEAD_DIM, QBLK), jnp.int32)
        for i in range(n_sel_blocks):
            thr = jnp.where(blk_id > i, key_m1, key)
            rank = rank + jnp.where(key[i:i + 1, :] > thr, 1, 0)
        keep = jnp.logical_and(rank < k_sel, blk_id < n_sel_blocks)
        pen_t = jnp.where(keep, 0.0, -1.0).astype(BF16)
        pen = lax.dot_general(pen_t, place_ref[...], (((0,), (0,)), ((), ())), preferred_element_type=F32)
        sel_ref[0, :, g * LANES:(g + 1) * LANES] = pen.astype(sel_ref.dtype)


def cmp_attention(q, kcd, vcd, ov, n_sel_blocks, k_sel):
    b, s, _ = q.shape
    ncr = kcd.shape[1]
    place = np.zeros((HEAD_DIM, LANES), np.float32)
    place[np.arange(HEAD_DIM), HEAD_DIM + np.arange(HEAD_DIM)] = 1.0
    return pl.pallas_call(
        functools.partial(_cmp_attn_kernel, n_sel_blocks=n_sel_blocks, k_sel=k_sel),
        grid=(b, s // QBLK),
        in_specs=[pl.BlockSpec((1, QBLK, HQ), lambda bb, i: (bb, i, 0)),
                  pl.BlockSpec((1, ncr, 2 * LANES), lambda bb, i: (bb, 0, 0)),
                  pl.BlockSpec((1, ncr, 2 * LANES), lambda bb, i: (bb, 0, 0)),
                  pl.BlockSpec((HEAD_DIM, ncr), lambda bb, i: (0, 0)),
                  pl.BlockSpec((HEAD_DIM, LANES), lambda bb, i: (0, 0))],
        out_specs=[pl.BlockSpec((1, QBLK, HQ), lambda bb, i: (bb, i, 0)),
                   pl.BlockSpec((1, QBLK, 2 * LANES), lambda bb, i: (bb, i, 0))],
        out_shape=[jax.ShapeDtypeStruct((b, s, HQ), F32),
                   jax.ShapeDtypeStruct((b, s, 2 * LANES), BF16)],
        compiler_params=_cparams("parallel", "parallel"),
        name="nsa_cmp_attention",
    )(q, kcd, vcd, ov, jnp.asarray(place, BF16))


def _sel_attn_kernel(q_ref, pen_ref, k_ref, v_ref, bias_ref, o_ref, qst_ref, m_ref, acc_ref,
                     sa_ref, sb_ref, *, nbt):
    qi0 = pl.program_id(2) * SEL_QTILES
    hpc = N_HEADS // C_KV_HEADS
    nsub = SEL_CHUNK // QBLK
    rpt = hpc * QBLK
    lane = lax.broadcasted_iota(jnp.int32, (QBLK, LANES), 1)
    lo = lane < HEAD_DIM
    for w in range(SEL_QTILES):
        pen = pen_ref[0, w * QBLK:(w + 1) * QBLK, :].astype(F32)
        for t in range(hpc):
            q2 = q_ref[0, w * QBLK:(w + 1) * QBLK, (t // 2) * LANES:(t // 2 + 1) * LANES].astype(F32)
            if t % 2:
                q2 = pltpu.roll(q2, HEAD_DIM, 1)
            qst_ref[w * rpt + t * QBLK:w * rpt + (t + 1) * QBLK, :] = jnp.where(lo, q2, pen).astype(BF16)
    m_ref[...] = jnp.full(m_ref.shape, NEG_INF, F32)
    acc_ref[...] = jnp.zeros(acc_ref.shape, F32)

    n_chunks = (qi0 + SEL_QTILES - 1) // nsub + 1

    def scores(c, s_ref):
        k0 = pl.multiple_of(c * SEL_CHUNK, SEL_CHUNK)
        s_ref[...] = _dot_nt(qst_ref[...], k_ref[0, pl.ds(k0, SEL_CHUNK), :])

    def accumulate(c, s_ref):
        k0 = pl.multiple_of(c * SEL_CHUNK, SEL_CHUNK)
        ps, alphas = [], []
        for w in range(SEL_QTILES):
            rows = slice(w * rpt, (w + 1) * rpt)
            tiles = []
            for u in range(nsub):
                off = qi0 + w - nsub * c - u
                idx = jnp.where(off < 0, nbt, jnp.minimum(off, nbt - 1))
                tiles.append(s_ref[rows, u * QBLK:(u + 1) * QBLK] + bias_ref[idx].reshape(rpt, QBLK))
            m_old = m_ref[rows, :]
            m_new = jnp.maximum(m_old, jnp.max(functools.reduce(jnp.maximum, tiles), axis=-1, keepdims=True))
            ps.append(jnp.concatenate([jnp.exp(tl - m_new).astype(BF16) for tl in tiles], axis=1))
            alphas.append(jnp.exp(m_old - m_new))
            m_ref[rows, :] = m_new
        pv = jnp.dot(jnp.concatenate(ps, axis=0), v_ref[0, pl.ds(k0, SEL_CHUNK), :], preferred_element_type=F32)
        acc_ref[...] = jnp.concatenate(alphas, axis=0) * acc_ref[...] + pv

    scores(0, sa_ref)

    def body(cc, carry):
        c = 2 * cc
        scores(c + 1, sb_ref)
        accumulate(c, sa_ref)
        scores(jnp.minimum(c + 2, n_chunks - 1), sa_ref)
        accumulate(c + 1, sb_ref)
        return carry

    lax.fori_loop(0, n_chunks // 2, body, 0)

    @pl.when(n_chunks % 2 == 1)
    def _():
        accumulate(n_chunks - 1, sa_ref)

    acc = acc_ref[...]
    rolled = pltpu.roll(acc, HEAD_DIM, 1)
    for w in range(SEL_QTILES):
        for t2 in range(hpc // 2):
            ev = slice(w * rpt + 2 * t2 * QBLK, w * rpt + (2 * t2 + 1) * QBLK)
            od = slice(w * rpt + (2 * t2 + 1) * QBLK, w * rpt + (2 * t2 + 2) * QBLK)
            even = acc[ev] * (1.0 / rolled[ev])
            odd = rolled[od] * (1.0 / acc[od])
            o_ref[0, w * QBLK:(w + 1) * QBLK, t2 * LANES:(t2 + 1) * LANES] = jnp.where(lo, even, odd)


def sel_attention(q, kx, vx, pen, bias_tiles):
    b, s, _ = q.shape
    hpc = N_HEADS // C_KV_HEADS
    gw = hpc * HEAD_DIM
    nbt = bias_tiles.shape[0] - 1
    tq = SEL_QTILES * QBLK
    rows = SEL_QTILES * hpc * QBLK
    assert s % SEL_CHUNK == 0 and s % tq == 0
    return pl.pallas_call(
        functools.partial(_sel_attn_kernel, nbt=nbt),
        grid=(b, C_KV_HEADS, s // tq),
        in_specs=[pl.BlockSpec((1, tq, gw), lambda bb, g, i: (bb, i, g)),
                  pl.BlockSpec((1, tq, LANES), lambda bb, g, i: (bb, i, g)),
                  pl.BlockSpec((1, s, LANES), lambda bb, g, i: (bb, 0, g)),
                  pl.BlockSpec((1, s, LANES), lambda bb, g, i: (bb, 0, g)),
                  pl.BlockSpec((nbt + 1, hpc, QBLK, QBLK), lambda bb, g, i: (0, g, 0, 0))],
        out_specs=pl.BlockSpec((1, tq, gw), lambda bb, g, i: (bb, i, g)),
        out_shape=jax.ShapeDtypeStruct((b, s, HQ), F32),
        scratch_shapes=[pltpu.VMEM((rows, LANES), BF16),
                        pltpu.VMEM((rows, LANES), F32),
                        pltpu.VMEM((rows, LANES), F32),
                        pltpu.VMEM((rows, SEL_CHUNK), F32),
                        pltpu.VMEM((rows, SEL_CHUNK), F32)],
        compiler_params=_cparams("parallel", "parallel", "arbitrary"),
        name="nsa_sel_attention",
    )(q, pen, kx, vx, bias_tiles)


def _sel_bias_idx(s):
    nqt = s // QBLK
    far = -(-(int(np.argmax(_t5_bucket_np(np.arange(4 * MAX_DISTANCE)) == NUM_BUCKETS - 1)) + QBLK) // QBLK)
    nbt = min(nqt, far + 1)
    d0 = np.arange(nbt)[:, None, None] * QBLK
    dist = d0 + np.arange(QBLK)[None, :, None] - np.arange(QBLK)[None, None, :]
    idx = np.where(dist >= 0, _t5_bucket_np(dist), -1).astype(np.int32)
    return np.concatenate([idx, np.full((1, QBLK, QBLK), -1, np.int32)])


def _ffn_kernel(h_ref, g2_ref, wu_ref, cw_ref, cb_ref, wd_ref, g3_ref,
                o_ref, xn_ref, acc_ref, u_ref, a_ref, carry_ref, *, tiles_per_seq, nj, lag):
    i = pl.program_id(0)
    ns, ts, tn = xn_ref.shape[0], xn_ref.shape[1], a_ref.shape[3]
    cols = lambda c: slice(c * tn, (c + 1) * tn)

    @pl.when(i % tiles_per_seq == 0)
    def _():
        carry_ref[...] = jnp.zeros(carry_ref.shape, F32)

    top = 16
    rowt = lax.broadcasted_iota(jnp.int32, (top, tn), 0)

    def up(s, j):
        xn = xn_ref[s]
        u_ref[s, j % 2, 0] = jnp.dot(xn, wu_ref[:, cols(j)], preferred_element_type=F32)
        u_ref[s, j % 2, 1] = jnp.dot(xn, wu_ref[:, cols(nj + j)], preferred_element_type=F32)

    def conv(u, s1, s2, c):
        cw = cw_ref[:, cols(c)]
        return ((cb_ref[:, cols(c)] + u * cw[2:3]) + s2 * cw[0:1]) + s1 * cw[1:2]

    def conv_body(u, c):
        return conv(u, pltpu.roll(u, 1, 0), pltpu.roll(u, 2, 0), c)

    def conv_top(uc_ref, c):
        prev = carry_ref[c]
        p1, p2 = prev[7:8], prev[6:7]
        u = uc_ref[0:top]
        s1 = jnp.where(rowt == 0, p1, pltpu.roll(u, 1, 0))
        s2 = jnp.where(rowt == 0, p2, jnp.where(rowt == 1, p1, pltpu.roll(u, 2, 0)))
        carry_ref[c] = uc_ref[ts - 8:ts]
        return conv(u, s1, s2, c)

    def gated(cg, cv):
        return (jax.nn.gelu(cg, approximate=True) * cv).astype(BF16)

    def act(s, j):
        ur, ar = u_ref.at[s, j % 2], a_ref.at[s, j % 2]
        ar[...] = gated(conv_body(ur[0], j), conv_body(ur[1], nj + j))
        ar[0:top] = gated(conv_top(ur.at[0], j), conv_top(ur.at[1], nj + j))

    def down(s, j):
        acc_ref[s] += jnp.dot(a_ref[s, j % 2], wd_ref[j * tn:(j + 1) * tn, :], preferred_element_type=F32)

    def iteration(s, k):
        rows = slice(s * ts, (s + 1) * ts)
        if k == -1:
            xn_ref[s] = _rms(h_ref[rows, :], g2_ref[...]).astype(xn_ref.dtype)
            acc_ref[s] = jnp.zeros(acc_ref.shape[1:], F32)
        if 0 <= k + 1 < nj:
            up(s, k + 1)
        if 0 <= k < nj:
            act(s, k)
        if 0 <= k - 1 < nj:
            down(s, k - 1)
        if k == nj:
            o_ref[rows, :] = h_ref[rows, :] + _rms(acc_ref[s], g3_ref[...])

    for slot in range(-1, nj + 1 + (ns - 1) * lag):
        for s in range(ns):
            k = slot - s * lag
            if -1 <= k <= nj:
                iteration(s, k)


def conv_ffn(h, g2, w_up, conv_w, conv_b, w_down, g3, seq, tm, tn, ns=FFN_SUBTILES, lag=FFN_LAG):
    t, d = h.shape
    dff = w_down.shape[0]
    nj = dff // tn
    ts = tm // ns
    assert nj * tn == dff and ts * ns == tm and ts % 16 == 0
    wu, cw, cb, wd = w_up, conv_w, conv_b.reshape(1, -1), w_down
    resident = lambda a: pl.BlockSpec(a.shape, (lambda nd: (lambda i: (0,) * nd))(a.ndim))
    return pl.pallas_call(
        functools.partial(_ffn_kernel, tiles_per_seq=seq // tm, nj=nj, lag=lag),
        grid=(t // tm,),
        in_specs=[pl.BlockSpec((tm, d), lambda i: (i, 0)),
                  pl.BlockSpec((1, d), lambda i: (0, 0)),
                  resident(wu), resident(cw), resident(cb), resident(wd),
                  pl.BlockSpec((1, d), lambda i: (0, 0))],
        out_specs=pl.BlockSpec((tm, d), lambda i: (i, 0)),
        out_shape=jax.ShapeDtypeStruct((t, d), F32),
        scratch_shapes=[pltpu.VMEM((ns, ts, d), BF16), pltpu.VMEM((ns, ts, d), F32),
                        pltpu.VMEM((ns, 2, 2, ts, tn), F32), pltpu.VMEM((ns, 2, ts, tn), BF16),
                        pltpu.VMEM((2 * nj, 8, tn), F32)],
        compiler_params=_cparams("arbitrary"),
        name="conv_ffn",
    )(h, g2.reshape(1, d), wu, cw, cb, wd, g3.reshape(1, d))


def _row_tile(t):
    for tm in (1024, 512, 256, 128):
        if t % tm == 0:
            return tm
    raise ValueError(f"token count {t} is not a multiple of 128")


def mixer_a(h, gains, w_in, w_o, rel_table, bsz, seq):
    t, d = h.shape
    tm = _row_tile(t)
    n_dil = len(DIL_CONFIGS)
    a_in = w_in.shape[1]
    col_scale = np.ones((a_in,), np.float32).reshape(n_dil, 3, HQ)
    col_scale[:, 0] = ATTN_SCALE
    w = (w_in * col_scale.reshape(1, a_in)).astype(BF16)
    idx = np.concatenate([_banded_idx(1, window // dil, dil) for window, dil in DIL_CONFIGS])
    bias = bias_expand(rel_table, idx)
    outs, lses = [], []
    dils = [dil for _, dil in DIL_CONFIGS]
    xns = norm_permute(h, gains[0], dils, max([tm] + [QBLK * dil for dil in dils]))
    for gi, (window, dil) in enumerate(DIL_CONFIGS):
        assert window // dil <= QBLK and seq % (QBLK * dil) == 0
        qkv = matmul_resident(xns[gi], w[:, gi * 3 * HQ:(gi + 1) * 3 * HQ], BF16, tm)
        qkv = qkv.reshape(1, t, 3 * HQ)

        tq = BAND_ROWS if dil == 1 else QBLK

        def rmap(part, dil=dil, per_seq=seq // tq):
            return lambda n, i: (0, (n // dil) * per_seq + i * dil + n % dil, part)

        o, lse = banded_attention(
            qkv, qkv, qkv, bias[2 * gi:2 * gi + 2], n_rows=bsz * dil, n_tiles=seq // (tq * dil), tq=tq, n_prev=1, kw=HQ,
            q_map=rmap(0), k_map=rmap(1), v_map=rmap(2), o_map=rmap(0),
            out_shape=(1, t, HQ), lse_shape=(1, t, LANES), name=f"dilated_attention_{dil}")
        outs.append(o.reshape(t, HQ))
        lses.append(lse.reshape(t, LANES))
    e = jnp.asarray(_head_expand_matrix(), BF16)
    return combine_proj_a(outs, lses, dils, e, w_o.astype(BF16), gains[1], h, min(tm, PROJ_ROWS))


def mixer_b(h, gains, w_in, sinks, w_o, rel_table, bsz, seq):
    t, d = h.shape
    tm = _row_tile(t)
    hk = B_KV_HEADS * HEAD_DIM
    n_in = w_in.shape[1]
    assert n_in == HQ + 2 * hk
    w = jnp.concatenate([w_in[:, :HQ] * ATTN_SCALE, _dup_groups(w_in[:, HQ:HQ + hk]),
                         _dup_groups(w_in[:, HQ + hk:])], axis=1).astype(BF16)
    kw = 2 * hk
    qkv = norm_matmul(h, gains[0], w, BF16, tm, HQ + 2 * kw).reshape(bsz, seq, HQ + 2 * kw)
    bias = bias_expand(rel_table, _banded_idx(1, B_WINDOW - 1, 1))
    sink_rows = sinks.astype(F32)
    ident = lambda n, i: (n, i, 0)
    col = lambda c: (lambda n, i: (n, i, c))
    (o,) = banded_attention(qkv, qkv, qkv, bias, n_rows=bsz, n_tiles=seq // BAND_ROWS, tq=BAND_ROWS, n_prev=1,
                            kw=kw, q_map=ident, k_map=col(HQ // kw), v_map=col(HQ // kw + 1), o_map=ident,
                            out_shape=(bsz, seq, HQ), sinks=sink_rows, stack=8, name="sink_window_attention")
    return _proj_call(_proj_kernel, "sink_proj", tm, [o.reshape(t, HQ)], [], w_o.astype(BF16), gains[1], h)


def mixer_c(h, gains, w_in, cmp_pos, cmp_w1, cmp_w2, w_o, rel_table, bsz, seq):
    t, d = h.shape
    tm = _row_tile(t)
    g = C_KV_HEADS
    hk = g * HEAD_DIM
    wkv = [w_in[:, HQ + i * hk:HQ + (i + 1) * hk] for i in range(6)]
    n_gate = w_in.shape[1] - HQ - 6 * hk
    assert n_gate == 3 * N_HEADS and hk == LANES
    wq = (w_in[:, :HQ] * ATTN_SCALE).astype(BF16)
    spread = lambda w: jnp.concatenate([w[:, :HEAD_DIM], jnp.zeros_like(w[:, :HEAD_DIM]),
                                        w[:, HEAD_DIM:], jnp.zeros_like(w[:, :HEAD_DIM])], axis=1)
    wb = jnp.concatenate([spread(wkv[2]), spread(wkv[3]), _dup_groups(wkv[4]), _dup_groups(wkv[5])],
                         axis=1).astype(BF16)
    wf = jnp.concatenate([wkv[0], wkv[1], jnp.pad(w_in[:, HQ + 6 * hk:], ((0, 0), (0, LANES - n_gate)))],
                         axis=1).astype(BF16)
    kw = 2 * LANES
    q = norm_matmul(h, gains[0], jnp.concatenate([wq, wb], axis=1), BF16, tm, HQ + 4 * kw)
    q = q.reshape(bsz, seq, HQ + 4 * kw)
    kvb = q[..., HQ:HQ + 2 * kw]
    r = norm_matmul(h, gains[0], wf, F32, tm, 3 * LANES)
    kv = [r[:, i * hk:(i + 1) * hk].reshape(bsz, seq, hk) for i in range(2)]
    gates_raw = r[:, 2 * hk:2 * hk + LANES]

    ncr = seq // CMP_STRIDE
    half = CMP_BLOCK // 2
    assert half == CMP_STRIDE
    chunks = jnp.stack([kv[0], kv[1]]).reshape(2, bsz, ncr, half, g, HEAD_DIM)
    chunks = chunks.transpose(0, 1, 4, 2, 3, 5).reshape(2, bsz, g, ncr, half * HEAD_DIM)
    pos = cmp_pos.reshape(2, 2, half * HEAD_DIM)
    w1 = cmp_w1.reshape(2, 2, half * HEAD_DIM, -1).astype(BF16)
    cmp = compress(chunks, pos, w1, cmp_w2.astype(BF16))
    cmp = cmp.transpose(0, 1, 3, 2, 4).reshape(2, bsz, ncr, hk).astype(BF16)
    kcd, vcd = _dup_groups(cmp[0]), _dup_groups(cmp[1])

    ns = seq // SEL_BLOCK
    assert ns <= HEAD_DIM
    k_sel = min(N_SELECT, ns)
    starts = np.arange(ncr) * CMP_STRIDE
    blk = np.arange(HEAD_DIM)
    ov = ((starts[None, :] < (blk[:, None] + 1) * SEL_BLOCK)
          & (starts[None, :] + CMP_BLOCK > blk[:, None] * SEL_BLOCK)
          & (blk[:, None] < ns) & (starts[None, :] + CMP_BLOCK <= seq))
    o_c, pen = cmp_attention(q, kcd, vcd, jnp.asarray(ov.astype(np.float32), BF16), ns, k_sel)

    sel_bias = bias_expand(rel_table, _sel_bias_idx(seq))
    lane = np.arange(kw)[None, :]
    key_blk = np.arange(seq)[:, None] // SEL_BLOCK
    onehot = np.where((lane % LANES >= HEAD_DIM) & (lane % HEAD_DIM == key_blk), -NEG_INF, 0.0)
    ones = np.broadcast_to(np.where(lane % LANES >= HEAD_DIM, 1.0, 0.0), (seq, kw))
    kx = kvb[..., :kw] + jnp.asarray(onehot, BF16)[None]
    vx = kvb[..., kw:2 * kw] + jnp.asarray(ones, BF16)[None]
    o_s = sel_attention(q, kx, vx, pen, sel_bias)

    n_prev = -(-(C_WINDOW - 1) // QBLK)
    tqw = n_prev * QBLK
    wbias = bias_expand(rel_table, _banded_idx(n_prev, C_WINDOW - 1, 1))
    ident = lambda n, i: (n, i, 0)
    col = lambda c: (lambda n, i: (n, i, c))
    (o_w,) = banded_attention(q, q, q, wbias,
                              n_rows=bsz, n_tiles=seq // tqw, tq=tqw, n_prev=n_prev, kw=kw,
                              q_map=ident, k_map=col(HQ // kw + 2), v_map=col(HQ // kw + 3), o_map=ident,
                              out_shape=(bsz, seq, HQ), stack=8, name="nsa_window_attention")

    e = jnp.asarray(_head_expand_matrix(), BF16)
    return _proj_call(_proj_c_kernel, "nsa_gate_proj", min(tm, PROJ_ROWS),
                      [o_c.reshape(t, HQ), o_s.reshape(t, HQ), o_w.reshape(t, HQ), gates_raw], [e],
                      w_o.astype(BF16), gains[1], h)


def kernel(x, rel_table, norm_gains, a_w_in, a_w_o, b_w_in, b_sinks, b_w_o, c_w_in, c_cmp_pos, c_cmp_w1,
           c_cmp_w2, c_w_o, ffn_w_up, ffn_conv_w, ffn_conv_b, ffn_w_down):
    bsz, seq, d = x.shape
    depth = norm_gains.shape[0]
    h = x.reshape(bsz * seq, d)
    tm = _row_tile(seq)
    for i in range(depth):
        kind, j = i % 3, i // 3
        g = norm_gains[i]
        if kind == 0:
            h = mixer_a(h, g, a_w_in[j], a_w_o[j], rel_table, bsz, seq)
        elif kind == 1:
            h = mixer_b(h, g, b_w_in[j], b_sinks[j], b_w_o[j], rel_table, bsz, seq)
        else:
            h = mixer_c(h, g, c_w_in[j], c_cmp_pos[j], c_cmp_w1[j], c_cmp_w2[j], c_w_o[j], rel_table, bsz, seq)
        h = conv_ffn(h, g[2], ffn_w_up[i].astype(BF16), ffn_conv_w[i], ffn_conv_b[i],
                     ffn_w_down[i].astype(BF16), g[3], seq, min(tm, FFN_ROWS), FFN_COLS)
    return h.reshape(bsz, seq, d)
```

```python
import functools
import math

import numpy as np
import jax
import jax.numpy as jnp
from jax import lax
from jax.experimental import pallas as pl
from jax.experimental.pallas import tpu as pltpu

F32 = jnp.float32
BF16 = jnp.bfloat16

N_HEADS = 16
HEAD_DIM = 64
HQ = N_HEADS * HEAD_DIM
LANES = 128
ATTN_SCALE = HEAD_DIM ** -0.5
NUM_BUCKETS = 32
MAX_DISTANCE = 2048
RMS_EPS = 1e-6
NEG_INF = -1e30
FORCE_SCORE = 1e9
DIL_CONFIGS = ((128, 1), (512, 4), (2048, 16))
B_KV_HEADS = 2
B_WINDOW = 128
C_KV_HEADS = 2
CMP_BLOCK = 32
CMP_STRIDE = 16
SEL_BLOCK = 64
N_SELECT = 16
C_WINDOW = 512
CONV_WIDTH = 3
QBLK = 128
BAND_ROWS = 256
PROJ_ROWS = 512
PERM_ROWS = 2048
SEL_CHUNK = 512
SEL_QTILES = 2
FFN_COLS = 256
FFN_ROWS = 512
FFN_SUBTILES = 2
FFN_LAG = 0
VMEM_LIMIT = 56 * 1024 * 1024


def _cparams(*sem):
    return pltpu.CompilerParams(dimension_semantics=sem, vmem_limit_bytes=VMEM_LIMIT)


def _t5_bucket_np(dist):
    max_exact = NUM_BUCKETS // 2
    d = np.maximum(dist, 0)
    df = np.maximum(d, 1).astype(np.float64)
    large = max_exact + np.floor(np.log(df / max_exact) / math.log(MAX_DISTANCE / max_exact)
                                 * (NUM_BUCKETS - max_exact) + 1e-9).astype(np.int64)
    large = np.minimum(large, NUM_BUCKETS - 1)
    return np.where(d < max_exact, d, large).astype(np.int32)


def _rms(x, g):
    ms = jnp.mean(x * x, axis=-1, keepdims=True)
    return (x * lax.rsqrt(ms + RMS_EPS)) * g


def _split3(w):
    hi = w.astype(BF16)
    r1 = w - hi.astype(F32)
    mid = r1.astype(BF16)
    lo = (r1 - mid.astype(F32)).astype(BF16)
    return hi, mid, lo


def _expand_heads(w, e3):
    lane = lax.broadcasted_iota(jnp.int32, w.shape, 1)
    r1 = w - w.astype(BF16).astype(F32)
    r2 = r1 - r1.astype(BF16).astype(F32)
    packed = jnp.where(lane < N_HEADS, w,
                       jnp.where(lane < 2 * N_HEADS, pltpu.roll(r1, N_HEADS, 1), pltpu.roll(r2, 2 * N_HEADS, 1)))
    return jnp.dot(packed.astype(BF16), e3, preferred_element_type=F32)


def _dot_nt(a, b):
    return lax.dot_general(a, b, (((1,), (1,)), ((), ())), preferred_element_type=F32)


def _stack_heads(q_tile, heads):
    lane = lax.broadcasted_iota(jnp.int32, (QBLK, LANES), 1)
    lo = lane < HEAD_DIM
    pieces = []
    for h in heads:
        q2 = q_tile(h // 2)
        keep = lo if h % 2 == 0 else jnp.logical_not(lo)
        pieces.append(jnp.where(keep, q2, jnp.zeros_like(q2)))
    return jnp.concatenate(pieces, axis=0)


def _merge_pairs(o, n_heads):
    lane = lax.broadcasted_iota(jnp.int32, (QBLK, LANES), 1)
    lo = lane < HEAD_DIM
    out = []
    for t in range(0, n_heads, 2):
        out.append(jnp.where(lo, o[t * QBLK:(t + 1) * QBLK], o[(t + 1) * QBLK:(t + 2) * QBLK]))
    return out


def _bias_kernel(tab_ref, rng_ref, idx_ref, o_ref):
    n = pl.program_id(0)
    idx = idx_ref[0]
    o_ref[...] = jnp.full(o_ref.shape, NEG_INF, F32)

    def body(b, carry):
        hit = idx == b
        for h in range(N_HEADS):
            o_ref[0, h] = jnp.where(hit, tab_ref[b, h], o_ref[0, h])
        return carry

    lax.fori_loop(rng_ref[n, 0], rng_ref[n, 1] + 1, body, 0)


def bias_expand(table, idx):
    n, r, c = idx.shape
    flat = idx.reshape(n, -1)
    lo = np.where(flat >= 0, flat, NUM_BUCKETS).min(axis=1)
    rng = np.stack([lo, flat.max(axis=1)], axis=1).astype(np.int32)
    return pl.pallas_call(
        _bias_kernel,
        grid=(n,),
        in_specs=[pl.BlockSpec(memory_space=pltpu.SMEM),
                  pl.BlockSpec(memory_space=pltpu.SMEM),
                  pl.BlockSpec((1, r, c), lambda i: (i, 0, 0))],
        out_specs=pl.BlockSpec((1, N_HEADS, r, c), lambda i: (i, 0, 0, 0)),
        out_shape=jax.ShapeDtypeStruct((n, N_HEADS, r, c), F32),
        compiler_params=_cparams("parallel"),
        name="bias_expand",
    )(table, jnp.asarray(rng), jnp.asarray(idx))


def _banded_idx(n_prev, max_dist, stride):
    kb = (n_prev + 1) * QBLK
    dist = np.arange(QBLK)[:, None] + n_prev * QBLK - np.arange(kb)[None, :]
    valid = (dist >= 0) & (dist <= max_dist)
    idx = np.where(valid, _t5_bucket_np(dist * stride), -1).astype(np.int32)
    if n_prev > 1:
        return idx[None]
    first = np.where(np.arange(kb)[None, :] < n_prev * QBLK, -1, idx).astype(np.int32)
    return np.stack([idx, first])


def _norm_mm_kernel(x_ref, g_ref, w_ref, o_ref, xn_ref):
    @pl.when(pl.program_id(1) == 0)
    def _():
        xn_ref[...] = _rms(x_ref[...], g_ref[...]).astype(xn_ref.dtype)

    o_ref[...] = jnp.dot(xn_ref[...], w_ref[...], preferred_element_type=F32).astype(o_ref.dtype)


def norm_matmul(x, gain, w, out_dtype, tm, tn):
    t, d = x.shape
    n = w.shape[1]
    scratch = [pltpu.VMEM((tm, d), BF16)]
    return pl.pallas_call(
        _norm_mm_kernel,
        grid=(t // tm, n // tn),
        in_specs=[pl.BlockSpec((tm, d), lambda i, j: (i, 0)),
                  pl.BlockSpec((1, d), lambda i, j: (0, 0)),
                  pl.BlockSpec((d, tn), lambda i, j: (0, j))],
        out_specs=pl.BlockSpec((tm, tn), lambda i, j: (i, j)),
        out_shape=jax.ShapeDtypeStruct((t, n), out_dtype),
        scratch_shapes=scratch,
        compiler_params=_cparams("parallel", "arbitrary"),
        name="norm_matmul",
    )(x, gain.reshape(1, d), w)


def _band_chunk(dil, seq):
    if seq % (BAND_ROWS * dil) == 0 and BAND_ROWS * dil <= PERM_ROWS:
        return BAND_ROWS
    return QBLK


def _norm_perm_kernel(x_ref, g_ref, *refs, dils, chunks):
    o_refs, xs_ref = refs[:len(dils)], refs[len(dils)]
    xn = _rms(x_ref[...], g_ref[...])
    nc = xs_ref.shape[0]
    for c in range(nc):
        xs_ref[c] = xn[:, c * LANES:(c + 1) * LANES]
    for o_ref, dil, cl in zip(o_refs, dils, chunks):
        if dil == 1:
            o_ref[...] = xn.astype(o_ref.dtype)
            continue
        span = cl * dil
        for u in range(x_ref.shape[0] // span):
            for r in range(dil):
                rows = jnp.concatenate(
                    [xs_ref[c, pl.ds(u * span + r, cl, stride=dil), :] for c in range(nc)], axis=1)
                dst = u * span + r * cl
                o_ref[dst:dst + cl, :] = rows.astype(o_ref.dtype)


def norm_permute(x, gain, dils, chunks, tm):
    t, d = x.shape
    assert all(tm % (cl * dil) == 0 for dil, cl in zip(dils, chunks))
    return pl.pallas_call(
        functools.partial(_norm_perm_kernel, dils=tuple(dils), chunks=tuple(chunks)),
        grid=(t // tm,),
        in_specs=[pl.BlockSpec((tm, d), lambda i: (i, 0)),
                  pl.BlockSpec((1, d), lambda i: (0, 0))],
        out_specs=[pl.BlockSpec((tm, d), lambda i: (i, 0)) for _ in dils],
        out_shape=[jax.ShapeDtypeStruct((t, d), BF16) for _ in dils],
        scratch_shapes=[pltpu.VMEM((d // LANES, tm, LANES), F32)],
        compiler_params=_cparams("parallel"),
        name="norm_permute",
    )(x, gain.reshape(1, d))


def _mm_kernel(x_ref, w_ref, o_ref):
    o_ref[...] = jnp.dot(x_ref[...], w_ref[...], preferred_element_type=F32).astype(o_ref.dtype)


def matmul_resident(x, w, out_dtype, tm):
    t, k = x.shape
    n = w.shape[1]
    return pl.pallas_call(
        _mm_kernel,
        grid=(t // tm,),
        in_specs=[pl.BlockSpec((tm, k), lambda i: (i, 0)),
                  pl.BlockSpec((k, n), lambda i: (0, 0))],
        out_specs=pl.BlockSpec((tm, n), lambda i: (i, 0)),
        out_shape=jax.ShapeDtypeStruct((t, n), out_dtype),
        compiler_params=_cparams("parallel"),
        name="matmul_resident",
    )(x, w)


def _finish_proj(y, w_ref, g_ref, h_ref, o_ref):
    z = jnp.dot(y.astype(BF16), w_ref[...], preferred_element_type=F32)
    o_ref[...] = h_ref[...] + _rms(z, g_ref[...])


def _proj_kernel(y_ref, w_ref, g_ref, h_ref, o_ref):
    _finish_proj(y_ref[...], w_ref, g_ref, h_ref, o_ref)


def _unpermute(src_ref, dst_ref, dil, cl, tm, i):
    span = cl * dil
    nc = dst_ref.shape[0]
    if span <= tm:
        for u in range(tm // span):
            for r in range(dil):
                lo = u * span + r * cl
                for c in range(nc):
                    dst_ref[c, pl.ds(u * span + r, cl, stride=dil), :] = src_ref[lo:lo + cl,
                                                                                 c * LANES:(c + 1) * LANES]
    else:
        per = tm // dil
        off = (i % (span // tm)) * per
        for r in range(dil):
            lo = pl.multiple_of(r * cl + off, 8)
            for c in range(nc):
                dst_ref[c, pl.ds(r, per, stride=dil), :] = src_ref[pl.ds(lo, per), c * LANES:(c + 1) * LANES]
    return jnp.concatenate([dst_ref[c] for c in range(nc)], axis=1) if nc > 1 else dst_ref[0]


def _proj_a_kernel(*refs, dils, chunks, tm):
    n = len(dils)
    o_refs, l_refs = refs[:n], refs[n:2 * n]
    e_ref, w_ref, g_ref, h_ref, out_ref = refs[2 * n:2 * n + 5]
    scratch = refs[2 * n + 5:]
    i = pl.program_id(0)
    outs, lses = [], []
    si = 0
    for gi, dil in enumerate(dils):
        if dil == 1:
            outs.append(o_refs[gi][...])
            lses.append(l_refs[gi][...])
        else:
            outs.append(_unpermute(o_refs[gi], scratch[si], dil, chunks[gi], tm, i))
            lses.append(_unpermute(l_refs[gi], scratch[si + 1], dil, chunks[gi], tm, i))
            si += 2
    mx = functools.reduce(jnp.maximum, lses)
    es = [jnp.exp(l - mx) for l in lses]
    inv = 1.0 / functools.reduce(lambda a, b: a + b, es)
    e = e_ref[...]
    y = outs[0] * _expand_heads(es[0] * inv, e)
    for gi in range(1, n):
        y = y + outs[gi] * _expand_heads(es[gi] * inv, e)
    _finish_proj(y, w_ref, g_ref, h_ref, out_ref)


def combine_proj_a(outs, lses, dils, chunks, e, w, gain, h, tm):
    t, d = h.shape

    def row_spec(width, dil, cl):
        rows = max(tm, cl * dil)
        return pl.BlockSpec((rows, width), lambda i, q=rows // tm: (i // q, 0))

    in_specs = [row_spec(HQ, dil, cl) for dil, cl in zip(dils, chunks)] + [
        row_spec(LANES, dil, cl) for dil, cl in zip(dils, chunks)] + [
        pl.BlockSpec(e.shape, lambda i: (0, 0)),
        pl.BlockSpec(w.shape, lambda i: (0, 0)),
        pl.BlockSpec((1, d), lambda i: (0, 0)),
        pl.BlockSpec((tm, d), lambda i: (i, 0))]
    scratch = []
    for dil in dils:
        if dil > 1:
            scratch += [pltpu.VMEM((HQ // LANES, tm, LANES), F32), pltpu.VMEM((1, tm, LANES), F32)]
    return pl.pallas_call(
        functools.partial(_proj_a_kernel, dils=tuple(dils), chunks=tuple(chunks), tm=tm),
        grid=(t // tm,),
        in_specs=in_specs,
        out_specs=pl.BlockSpec((tm, d), lambda i: (i, 0)),
        out_shape=jax.ShapeDtypeStruct((t, d), F32),
        scratch_shapes=scratch,
        compiler_params=_cparams("arbitrary"),
        name="dilated_combine_proj",
    )(*outs, *lses, e, w, gain.reshape(1, d), h)


def _proj_c_kernel(oc_ref, os_ref, ow_ref, gr_ref, e_ref, w_ref, g_ref, h_ref, o_ref):
    sig = jax.nn.sigmoid(gr_ref[...])
    e = e_ref[...]
    gate = lambda i: _expand_heads(sig if i == 0 else pltpu.roll(sig, LANES - i * N_HEADS, 1), e)
    y = gate(0) * oc_ref[...]
    y = y + gate(1) * os_ref[...]
    y = y + gate(2) * ow_ref[...]
    _finish_proj(y, w_ref, g_ref, h_ref, o_ref)


def _proj_call(kernel, name, tm, row_inputs, const_inputs, w, gain, h):
    t, d = h.shape
    k = w.shape[0]
    row_specs = [pl.BlockSpec((tm, a.shape[1]), lambda i: (i, 0)) for a in row_inputs]
    const_specs = [pl.BlockSpec(a.shape, (lambda nd: (lambda i: (0,) * nd))(a.ndim)) for a in const_inputs]
    return pl.pallas_call(
        kernel,
        grid=(t // tm,),
        in_specs=row_specs + const_specs + [
            pl.BlockSpec((k, d), lambda i: (0, 0)),
            pl.BlockSpec((1, d), lambda i: (0, 0)),
            pl.BlockSpec((tm, d), lambda i: (i, 0))],
        out_specs=pl.BlockSpec((tm, d), lambda i: (i, 0)),
        out_shape=jax.ShapeDtypeStruct((t, d), F32),
        compiler_params=_cparams("parallel"),
        name=name,
    )(*row_inputs, *const_inputs, w, gain.reshape(1, d), h)


def _head_expand_matrix():
    e = np.zeros((LANES, HQ), np.float32)
    for term in range(3):
        for h in range(N_HEADS):
            e[term * N_HEADS + h, h * HEAD_DIM:(h + 1) * HEAD_DIM] = 1.0
    return e


def _banded_kernel(*refs, tq, n_prev, n_chunks, stack, with_sinks, with_lse):
    q_ref, kp_ref, kc_ref, vp_ref, vc_ref, bias_ref = refs[:6]
    pos = 6
    sink_ref = None
    if with_sinks:
        sink_ref = refs[pos]
        pos += 1
    o_ref = refs[pos]
    pos += 1
    lse_ref = None
    if with_lse:
        lse_ref = refs[pos]
        pos += 1
    kbuf, vbuf = refs[pos], refs[pos + 1]

    i = pl.program_id(1)
    hpc = N_HEADS // n_chunks
    kb = (n_prev + 1) * QBLK
    kbuf[0:tq] = kp_ref[0]
    kbuf[tq:2 * tq] = kc_ref[0]
    vbuf[0:tq] = vp_ref[0]
    vbuf[tq:2 * tq] = vc_ref[0]
    lane = lax.broadcasted_iota(jnp.int32, (QBLK, LANES), 1)
    head_row = lax.broadcasted_iota(jnp.int32, (stack * QBLK, 1), 0)

    for s in range(tq // QBLK):
        r0 = s * QBLK
        k0 = tq + r0 - n_prev * QBLK
        variants = bias_ref.shape[0] > 1
        if variants:
            first = jnp.where(i == 0, 1, 0) if s == 0 else 0
        else:
            col = lax.broadcasted_iota(jnp.int32, (1, kb), 1) + k0
            kmask = jnp.where(jnp.logical_and(i == 0, col < tq), NEG_INF, 0.0).astype(F32)
        lse_acc = jnp.zeros((QBLK, LANES), F32)
        for h0 in range(0, N_HEADS, stack):
            heads = list(range(h0, h0 + stack))
            c = h0 // hpc
            qst = _stack_heads(lambda p: q_ref[0, r0:r0 + QBLK, p * LANES:(p + 1) * LANES], heads)
            kx = kbuf[k0:k0 + kb, c * LANES:(c + 1) * LANES]
            vx = vbuf[k0:k0 + kb, c * LANES:(c + 1) * LANES]
            sc = _dot_nt(qst, kx)
            if variants:
                sc = sc + bias_ref[first, h0:h0 + stack].reshape(stack * QBLK, kb)
            else:
                sc = sc + bias_ref[0, h0:h0 + stack].reshape(stack * QBLK, kb) + kmask
            m = jnp.max(sc, axis=-1, keepdims=True)
            if with_sinks:
                sk = sink_ref[h0 + stack - 1]
                for t in range(stack - 2, -1, -1):
                    sk = jnp.where(head_row < (t + 1) * QBLK, sink_ref[h0 + t], sk)
                m = jnp.maximum(m, sk)
            p = jnp.exp(sc - m)
            den = jnp.sum(p, axis=-1, keepdims=True)
            norm = den + jnp.exp(sk - m) if with_sinks else den
            o = jnp.dot(p.astype(BF16), vx, preferred_element_type=F32) * (1.0 / norm)
            for t2, blk in enumerate(_merge_pairs(o, stack)):
                pidx = h0 // 2 + t2
                o_ref[0, r0:r0 + QBLK, pidx * LANES:(pidx + 1) * LANES] = blk.astype(o_ref.dtype)
            if with_lse:
                lse = m + jnp.log(den)
                for t, h in enumerate(heads):
                    lse_acc = jnp.where(lane == h, lse[t * QBLK:(t + 1) * QBLK], lse_acc)
        if with_lse:
            lse_ref[0, r0:r0 + QBLK, :] = lse_acc


def banded_attention(q_arr, k_arr, v_arr, bias, *, n_rows, n_tiles, tq, n_prev, kw,
                     q_map, k_map, v_map, o_map, out_shape, lse_shape=None, sinks=None, stack=2, name):
    n_chunks = kw // LANES
    assert stack % 2 == 0 and (N_HEADS // n_chunks) % stack == 0
    kb = (n_prev + 1) * QBLK
    with_sinks = sinks is not None
    with_lse = lse_shape is not None

    def prev(fn):
        return lambda n, i: fn(n, jnp.maximum(i - 1, 0))

    in_specs = [pl.BlockSpec((1, tq, HQ), q_map),
                pl.BlockSpec((1, tq, kw), prev(k_map)),
                pl.BlockSpec((1, tq, kw), k_map),
                pl.BlockSpec((1, tq, kw), prev(v_map)),
                pl.BlockSpec((1, tq, kw), v_map),
                pl.BlockSpec(bias.shape, lambda n, i: (0, 0, 0, 0))]
    assert bias.shape[1:] == (N_HEADS, QBLK, kb) and (bias.shape[0] == 1 or n_prev == 1)
    args = [q_arr, k_arr, k_arr, v_arr, v_arr, bias]
    if with_sinks:
        in_specs.append(pl.BlockSpec(memory_space=pltpu.SMEM))
        args.append(sinks)
    out_specs = [pl.BlockSpec((1, tq, HQ), o_map)]
    out_shapes = [jax.ShapeDtypeStruct(out_shape, F32)]
    if with_lse:
        out_specs.append(pl.BlockSpec((1, tq, LANES), o_map))
        out_shapes.append(jax.ShapeDtypeStruct(lse_shape, F32))
    res = pl.pallas_call(
        functools.partial(_banded_kernel, tq=tq, n_prev=n_prev, n_chunks=n_chunks, stack=stack,
                          with_sinks=with_sinks, with_lse=with_lse),
        grid=(n_rows, n_tiles),
        in_specs=in_specs,
        out_specs=out_specs,
        out_shape=out_shapes,
        scratch_shapes=[pltpu.VMEM((2 * tq, kw), BF16), pltpu.VMEM((2 * tq, kw), BF16)],
        compiler_params=_cparams("parallel", "arbitrary"),
        name=name,
    )(*args)
    return res


def _dup_groups(x):
    g0, g1 = x[..., :HEAD_DIM], x[..., HEAD_DIM:]
    return jnp.concatenate([g0, g0, g1, g1], axis=-1)


def _compress_kernel(ch_ref, pos_ref, w1_ref, w2_ref, o_ref):
    ch = ch_ref[0, 0, 0]
    rows = ch.shape[0]
    posv = pos_ref[0]
    a = jnp.dot((ch + posv[0:1]).astype(BF16), w1_ref[0, 0], preferred_element_type=F32)
    b = jnp.dot((ch + posv[1:2]).astype(BF16), w1_ref[0, 1], preferred_element_type=F32)
    hid = a + pltpu.roll(b, rows - 1, 0)
    act = jax.nn.gelu(hid, approximate=True)
    o_ref[0, 0, 0] = jnp.dot(act.astype(BF16), w2_ref[0], preferred_element_type=F32)


def compress(chunks, pos, w1, w2):
    _, b, g, rows, width = chunks.shape
    hid = w1.shape[-1]
    return pl.pallas_call(
        _compress_kernel,
        grid=(2, b, g),
        in_specs=[pl.BlockSpec((1, 1, 1, rows, width), lambda i, bb, gg: (i, bb, gg, 0, 0)),
                  pl.BlockSpec((1, 2, width), lambda i, bb, gg: (i, 0, 0)),
                  pl.BlockSpec((1, 2, width, hid), lambda i, bb, gg: (i, 0, 0, 0)),
                  pl.BlockSpec((1, hid, HEAD_DIM), lambda i, bb, gg: (i, 0, 0))],
        out_specs=pl.BlockSpec((1, 1, 1, rows, HEAD_DIM), lambda i, bb, gg: (i, bb, gg, 0, 0)),
        out_shape=jax.ShapeDtypeStruct((2, b, g, rows, HEAD_DIM), F32),
        compiler_params=_cparams("parallel", "parallel", "parallel"),
        name="nsa_compress",
    )(chunks, pos, w1, w2)


def _cmp_attn_kernel(q_ref, kc_ref, vc_ref, ov_ref, place_ref, o_ref, sel_ref, *, n_sel_blocks, k_sel):
    qi = pl.program_id(1)
    ncr = kc_ref.shape[1]
    hpc = N_HEADS // C_KV_HEADS
    qpos = qi * QBLK + lax.broadcasted_iota(jnp.int32, (QBLK, 1), 0)
    cidx = lax.broadcasted_iota(jnp.int32, (1, ncr), 1)
    valid = (cidx * CMP_STRIDE + (CMP_BLOCK - 1)) <= qpos
    maskc = jnp.where(valid, 0.0, NEG_INF).astype(F32)
    anyv = (qpos >= CMP_BLOCK - 1).astype(F32)
    blk_id = lax.broadcasted_iota(jnp.int32, (HEAD_DIM, QBLK), 0)
    cur = (qi * QBLK + lax.broadcasted_iota(jnp.int32, (HEAD_DIM, QBLK), 1)) // SEL_BLOCK
    forced = jnp.logical_or(jnp.logical_or(blk_id == 0, blk_id == cur), blk_id == cur - 1)
    allowed = blk_id <= cur
    for g in range(C_KV_HEADS):
        heads = list(range(g * hpc, (g + 1) * hpc))
        qst = _stack_heads(lambda p: q_ref[0, :, p * LANES:(p + 1) * LANES], heads)
        sc = _dot_nt(qst, kc_ref[0, :, g * LANES:(g + 1) * LANES])
        sc3 = sc.reshape(hpc, QBLK, ncr) + maskc[None]
        m = jnp.max(sc3, axis=-1, keepdims=True)
        e = jnp.exp(sc3 - m)
        ssum = jnp.sum(e, axis=-1, keepdims=True)
        p = (e * (1.0 / ssum)) * anyv[None]
        o = jnp.dot(p.reshape(hpc * QBLK, ncr).astype(BF16), vc_ref[0, :, g * LANES:(g + 1) * LANES],
                    preferred_element_type=F32)
        for t2, blk in enumerate(_merge_pairs(o, hpc)):
            pidx = (g * hpc) // 2 + t2
            o_ref[0, :, pidx * LANES:(pidx + 1) * LANES] = blk
        hi, mid, lo = _split3(jnp.sum(p, axis=0))
        ovt = ov_ref[...]
        imp = (_dot_nt(ovt, hi) + _dot_nt(ovt, mid)) + _dot_nt(ovt, lo)
        score = jnp.where(forced, FORCE_SCORE, jnp.where(allowed, imp, NEG_INF))
        bits = pltpu.bitcast(score, jnp.int32)
        key = jnp.where(bits < 0, bits ^ jnp.int32(0x7FFFFFFF), bits)
        key_m1 = key - 1
        rank = jnp.zeros((HEAD_DIM, QBLK), jnp.int32)
        for i in range(n_sel_blocks):
            thr = jnp.where(blk_id > i, key_m1, key)
            rank = rank + jnp.where(key[i:i + 1, :] > thr, 1, 0)
        keep = jnp.logical_and(rank < k_sel, blk_id < n_sel_blocks)
        pen_t = jnp.where(keep, 0.0, -1.0).astype(BF16)
        pen = lax.dot_general(pen_t, place_ref[...], (((0,), (0,)), ((), ())), preferred_element_type=F32)
        sel_ref[0, :, g * LANES:(g + 1) * LANES] = pen.astype(sel_ref.dtype)


def cmp_attention(q, kcd, vcd, ov, n_sel_blocks, k_sel):
    b, s, _ = q.shape
    ncr = kcd.shape[1]
    place = np.zeros((HEAD_DIM, LANES), np.float32)
    place[np.arange(HEAD_DIM), HEAD_DIM + np.arange(HEAD_DIM)] = 1.0
    return pl.pallas_call(
        functools.partial(_cmp_attn_kernel, n_sel_blocks=n_sel_blocks, k_sel=k_sel),
        grid=(b, s // QBLK),
        in_specs=[pl.BlockSpec((1, QBLK, HQ), lambda bb, i: (bb, i, 0)),
                  pl.BlockSpec((1, ncr, 2 * LANES), lambda bb, i: (bb, 0, 0)),
                  pl.BlockSpec((1, ncr, 2 * LANES), lambda bb, i: (bb, 0, 0)),
                  pl.BlockSpec((HEAD_DIM, ncr), lambda bb, i: (0, 0)),
                  pl.BlockSpec((HEAD_DIM, LANES), lambda bb, i: (0, 0))],
        out_specs=[pl.BlockSpec((1, QBLK, HQ), lambda bb, i: (bb, i, 0)),
                   pl.BlockSpec((1, QBLK, 2 * LANES), lambda bb, i: (bb, i, 0))],
        out_shape=[jax.ShapeDtypeStruct((b, s, HQ), F32),
                   jax.ShapeDtypeStruct((b, s, 2 * LANES), BF16)],
        compiler_params=_cparams("parallel", "parallel"),
        name="nsa_cmp_attention",
    )(q, kcd, vcd, ov, jnp.asarray(place, BF16))


def _sel_attn_kernel(q_ref, pen_ref, k_ref, v_ref, bias_ref, o_ref, qst_ref, m_ref, acc_ref,
                     sa_ref, sb_ref, *, nbt):
    qi0 = pl.program_id(2) * SEL_QTILES
    hpc = N_HEADS // C_KV_HEADS
    nsub = SEL_CHUNK // QBLK
    rpt = hpc * QBLK
    lane = lax.broadcasted_iota(jnp.int32, (QBLK, LANES), 1)
    lo = lane < HEAD_DIM
    for w in range(SEL_QTILES):
        pen = pen_ref[0, w * QBLK:(w + 1) * QBLK, :].astype(F32)
        for t in range(hpc):
            q2 = q_ref[0, w * QBLK:(w + 1) * QBLK, (t // 2) * LANES:(t // 2 + 1) * LANES].astype(F32)
            if t % 2:
                q2 = pltpu.roll(q2, HEAD_DIM, 1)
            qst_ref[w * rpt + t * QBLK:w * rpt + (t + 1) * QBLK, :] = jnp.where(lo, q2, pen).astype(BF16)
    m_ref[...] = jnp.full(m_ref.shape, NEG_INF, F32)
    acc_ref[...] = jnp.zeros(acc_ref.shape, F32)

    n_chunks = (qi0 + SEL_QTILES - 1) // nsub + 1

    def scores(c, s_ref):
        k0 = pl.multiple_of(c * SEL_CHUNK, SEL_CHUNK)
        s_ref[...] = _dot_nt(qst_ref[...], k_ref[0, pl.ds(k0, SEL_CHUNK), :])

    def accumulate(c, s_ref):
        k0 = pl.multiple_of(c * SEL_CHUNK, SEL_CHUNK)
        ps, alphas = [], []
        for w in range(SEL_QTILES):
            rows = slice(w * rpt, (w + 1) * rpt)
            tiles = []
            for u in range(nsub):
                off = qi0 + w - nsub * c - u
                idx = jnp.where(off < 0, nbt, jnp.minimum(off, nbt - 1))
                tiles.append(s_ref[rows, u * QBLK:(u + 1) * QBLK] + bias_ref[idx].reshape(rpt, QBLK))
            m_old = m_ref[rows, :]
            m_new = jnp.maximum(m_old, jnp.max(functools.reduce(jnp.maximum, tiles), axis=-1, keepdims=True))
            ps.append(jnp.concatenate([jnp.exp(tl - m_new).astype(BF16) for tl in tiles], axis=1))
            alphas.append(jnp.exp(m_old - m_new))
            m_ref[rows, :] = m_new
        pv = jnp.dot(jnp.concatenate(ps, axis=0), v_ref[0, pl.ds(k0, SEL_CHUNK), :], preferred_element_type=F32)
        acc_ref[...] = jnp.concatenate(alphas, axis=0) * acc_ref[...] + pv

    scores(0, sa_ref)

    def body(cc, carry):
        c = 2 * cc
        scores(c + 1, sb_ref)
        accumulate(c, sa_ref)
        scores(jnp.minimum(c + 2, n_chunks - 1), sa_ref)
        accumulate(c + 1, sb_ref)
        return carry

    lax.fori_loop(0, n_chunks // 2, body, 0)

    @pl.when(n_chunks % 2 == 1)
    def _():
        accumulate(n_chunks - 1, sa_ref)

    acc = acc_ref[...]
    rolled = pltpu.roll(acc, HEAD_DIM, 1)
    for w in range(SEL_QTILES):
        for t2 in range(hpc // 2):
            ev = slice(w * rpt + 2 * t2 * QBLK, w * rpt + (2 * t2 + 1) * QBLK)
            od = slice(w * rpt + (2 * t2 + 1) * QBLK, w * rpt + (2 * t2 + 2) * QBLK)
            even = acc[ev] * (1.0 / rolled[ev])
            odd = rolled[od] * (1.0 / acc[od])
            o_ref[0, w * QBLK:(w + 1) * QBLK, t2 * LANES:(t2 + 1) * LANES] = jnp.where(lo, even, odd)


def sel_attention(q, kx, vx, pen, bias_tiles):
    b, s, _ = q.shape
    hpc = N_HEADS // C_KV_HEADS
    gw = hpc * HEAD_DIM
    nbt = bias_tiles.shape[0] - 1
    tq = SEL_QTILES * QBLK
    rows = SEL_QTILES * hpc * QBLK
    assert s % SEL_CHUNK == 0 and s % tq == 0
    return pl.pallas_call(
        functools.partial(_sel_attn_kernel, nbt=nbt),
        grid=(b, C_KV_HEADS, s // tq),
        in_specs=[pl.BlockSpec((1, tq, gw), lambda bb, g, i: (bb, i, g)),
                  pl.BlockSpec((1, tq, LANES), lambda bb, g, i: (bb, i, g)),
                  pl.BlockSpec((1, s, LANES), lambda bb, g, i: (bb, 0, g)),
                  pl.BlockSpec((1, s, LANES), lambda bb, g, i: (bb, 0, g)),
                  pl.BlockSpec((nbt + 1, hpc, QBLK, QBLK), lambda bb, g, i: (0, g, 0, 0))],
        out_specs=pl.BlockSpec((1, tq, gw), lambda bb, g, i: (bb, i, g)),
        out_shape=jax.ShapeDtypeStruct((b, s, HQ), F32),
        scratch_shapes=[pltpu.VMEM((rows, LANES), BF16),
                        pltpu.VMEM((rows, LANES), F32),
                        pltpu.VMEM((rows, LANES), F32),
                        pltpu.VMEM((rows, SEL_CHUNK), F32),
                        pltpu.VMEM((rows, SEL_CHUNK), F32)],
        compiler_params=_cparams("parallel", "parallel", "arbitrary"),
        name="nsa_sel_attention",
    )(q, pen, kx, vx, bias_tiles)


def _sel_bias_idx(s):
    nqt = s // QBLK
    far = -(-(int(np.argmax(_t5_bucket_np(np.arange(4 * MAX_DISTANCE)) == NUM_BUCKETS - 1)) + QBLK) // QBLK)
    nbt = min(nqt, far + 1)
    d0 = np.arange(nbt)[:, None, None] * QBLK
    dist = d0 + np.arange(QBLK)[None, :, None] - np.arange(QBLK)[None, None, :]
    idx = np.where(dist >= 0, _t5_bucket_np(dist), -1).astype(np.int32)
    return np.concatenate([idx, np.full((1, QBLK, QBLK), -1, np.int32)])


def _ffn_kernel(h_ref, g2_ref, wu_ref, cw_ref, cb_ref, wd_ref, g3_ref,
                o_ref, xn_ref, acc_ref, u_ref, a_ref, carry_ref, *, tiles_per_seq, nj, lag):
    i = pl.program_id(0)
    ns, ts, tn = xn_ref.shape[0], xn_ref.shape[1], a_ref.shape[3]
    cols = lambda c: slice(c * tn, (c + 1) * tn)

    @pl.when(i % tiles_per_seq == 0)
    def _():
        carry_ref[...] = jnp.zeros(carry_ref.shape, F32)

    top = 16
    rowt = lax.broadcasted_iota(jnp.int32, (top, tn), 0)

    def up(s, j):
        xn = xn_ref[s]
        u_ref[s, j % 2, 0] = jnp.dot(xn, wu_ref[:, cols(j)], preferred_element_type=F32)
        u_ref[s, j % 2, 1] = jnp.dot(xn, wu_ref[:, cols(nj + j)], preferred_element_type=F32)

    def conv(u, s1, s2, c):
        cw = cw_ref[:, cols(c)]
        return ((cb_ref[:, cols(c)] + u * cw[2:3]) + s2 * cw[0:1]) + s1 * cw[1:2]

    def conv_body(u, c):
        return conv(u, pltpu.roll(u, 1, 0), pltpu.roll(u, 2, 0), c)

    def conv_top(uc_ref, c):
        prev = carry_ref[c]
        p1, p2 = prev[7:8], prev[6:7]
        u = uc_ref[0:top]
        s1 = jnp.where(rowt == 0, p1, pltpu.roll(u, 1, 0))
        s2 = jnp.where(rowt == 0, p2, jnp.where(rowt == 1, p1, pltpu.roll(u, 2, 0)))
        carry_ref[c] = uc_ref[ts - 8:ts]
        return conv(u, s1, s2, c)

    def gated(cg, cv):
        return (jax.nn.gelu(cg, approximate=True) * cv).astype(BF16)

    def act(s, j):
        ur, ar = u_ref.at[s, j % 2], a_ref.at[s, j % 2]
        ar[...] = gated(conv_body(ur[0], j), conv_body(ur[1], nj + j))
        ar[0:top] = gated(conv_top(ur.at[0], j), conv_top(ur.at[1], nj + j))

    def down(s, j):
        acc_ref[s] += jnp.dot(a_ref[s, j % 2], wd_ref[j * tn:(j + 1) * tn, :], preferred_element_type=F32)

    def iteration(s, k):
        rows = slice(s * ts, (s + 1) * ts)
        if k == -1:
            xn_ref[s] = _rms(h_ref[rows, :], g2_ref[...]).astype(xn_ref.dtype)
            acc_ref[s] = jnp.zeros(acc_ref.shape[1:], F32)
        if 0 <= k + 1 < nj:
            up(s, k + 1)
        if 0 <= k < nj:
            act(s, k)
        if 0 <= k - 1 < nj:
            down(s, k - 1)
        if k == nj:
            o_ref[rows, :] = h_ref[rows, :] + _rms(acc_ref[s], g3_ref[...])

    for slot in range(-1, nj + 1 + (ns - 1) * lag):
        for s in range(ns):
            k = slot - s * lag
            if -1 <= k <= nj:
                iteration(s, k)


def conv_ffn(h, g2, w_up, conv_w, conv_b, w_down, g3, seq, tm, tn, ns=FFN_SUBTILES, lag=FFN_LAG):
    t, d = h.shape
    dff = w_down.shape[0]
    nj = dff // tn
    ts = tm // ns
    assert nj * tn == dff and ts * ns == tm and ts % 16 == 0
    wu, cw, cb, wd = w_up, conv_w, conv_b.reshape(1, -1), w_down
    resident = lambda a: pl.BlockSpec(a.shape, (lambda nd: (lambda i: (0,) * nd))(a.ndim))
    return pl.pallas_call(
        functools.partial(_ffn_kernel, tiles_per_seq=seq // tm, nj=nj, lag=lag),
        grid=(t // tm,),
        in_specs=[pl.BlockSpec((tm, d), lambda i: (i, 0)),
                  pl.BlockSpec((1, d), lambda i: (0, 0)),
                  resident(wu), resident(cw), resident(cb), resident(wd),
                  pl.BlockSpec((1, d), lambda i: (0, 0))],
        out_specs=pl.BlockSpec((tm, d), lambda i: (i, 0)),
        out_shape=jax.ShapeDtypeStruct((t, d), F32),
        scratch_shapes=[pltpu.VMEM((ns, ts, d), BF16), pltpu.VMEM((ns, ts, d), F32),
                        pltpu.VMEM((ns, 2, 2, ts, tn), F32), pltpu.VMEM((ns, 2, ts, tn), BF16),
                        pltpu.VMEM((2 * nj, 8, tn), F32)],
        compiler_params=_cparams("arbitrary"),
        name="conv_ffn",
    )(h, g2.reshape(1, d), wu, cw, cb, wd, g3.reshape(1, d))


def _row_tile(t):
    for tm in (1024, 512, 256, 128):
        if t % tm == 0:
            return tm
    raise ValueError(f"token count {t} is not a multiple of 128")


def mixer_a(h, gains, w_in, w_o, rel_table, bsz, seq):
    t, d = h.shape
    tm = _row_tile(t)
    n_dil = len(DIL_CONFIGS)
    a_in = w_in.shape[1]
    col_scale = np.ones((a_in,), np.float32).reshape(n_dil, 3, HQ)
    col_scale[:, 0] = ATTN_SCALE
    w = (w_in * col_scale.reshape(1, a_in)).astype(BF16)
    idx = np.concatenate([_banded_idx(1, window // dil, dil) for window, dil in DIL_CONFIGS])
    bias = bias_expand(rel_table, idx)
    outs, lses = [], []
    dils = [dil for _, dil in DIL_CONFIGS]
    chunks = [_band_chunk(dil, seq) for dil in dils]
    xns = norm_permute(h, gains[0], dils, chunks, max([tm] + [cl * dil for dil, cl in zip(dils, chunks)]))
    for gi, (window, dil) in enumerate(DIL_CONFIGS):
        tq = chunks[gi]
        assert window // dil <= QBLK and seq % (tq * dil) == 0
        qkv = matmul_resident(xns[gi], w[:, gi * 3 * HQ:(gi + 1) * 3 * HQ], BF16, tm)
        qkv = qkv.reshape(1, t, 3 * HQ)

        def rmap(part, dil=dil, per_seq=seq // tq):
            return lambda n, i: (0, (n // dil) * per_seq + i * dil + n % dil, part)

        o, lse = banded_attention(
            qkv, qkv, qkv, bias[2 * gi:2 * gi + 2], n_rows=bsz * dil, n_tiles=seq // (tq * dil), tq=tq, n_prev=1, kw=HQ,
            q_map=rmap(0), k_map=rmap(1), v_map=rmap(2), o_map=rmap(0),
            out_shape=(1, t, HQ), lse_shape=(1, t, LANES), name=f"dilated_attention_{dil}")
        outs.append(o.reshape(t, HQ))
        lses.append(lse.reshape(t, LANES))
    e = jnp.asarray(_head_expand_matrix(), BF16)
    return combine_proj_a(outs, lses, dils, chunks, e, w_o.astype(BF16), gains[1], h, min(tm, PROJ_ROWS))


def mixer_b(h, gains, w_in, sinks, w_o, rel_table, bsz, seq):
    t, d = h.shape
    tm = _row_tile(t)
    hk = B_KV_HEADS * HEAD_DIM
    n_in = w_in.shape[1]
    assert n_in == HQ + 2 * hk
    w = jnp.concatenate([w_in[:, :HQ] * ATTN_SCALE, _dup_groups(w_in[:, HQ:HQ + hk]),
                         _dup_groups(w_in[:, HQ + hk:])], axis=1).astype(BF16)
    kw = 2 * hk
    qkv = norm_matmul(h, gains[0], w, BF16, tm, HQ + 2 * kw).reshape(bsz, seq, HQ + 2 * kw)
    bias = bias_expand(rel_table, _banded_idx(1, B_WINDOW - 1, 1))
    sink_rows = sinks.astype(F32)
    ident = lambda n, i: (n, i, 0)
    col = lambda c: (lambda n, i: (n, i, c))
    (o,) = banded_attention(qkv, qkv, qkv, bias, n_rows=bsz, n_tiles=seq // BAND_ROWS, tq=BAND_ROWS, n_prev=1,
                            kw=kw, q_map=ident, k_map=col(HQ // kw), v_map=col(HQ // kw + 1), o_map=ident,
                            out_shape=(bsz, seq, HQ), sinks=sink_rows, stack=8, name="sink_window_attention")
    return _proj_call(_proj_kernel, "sink_proj", tm, [o.reshape(t, HQ)], [], w_o.astype(BF16), gains[1], h)


def mixer_c(h, gains, w_in, cmp_pos, cmp_w1, cmp_w2, w_o, rel_table, bsz, seq):
    t, d = h.shape
    tm = _row_tile(t)
    g = C_KV_HEADS
    hk = g * HEAD_DIM
    wkv = [w_in[:, HQ + i * hk:HQ + (i + 1) * hk] for i in range(6)]
    n_gate = w_in.shape[1] - HQ - 6 * hk
    assert n_gate == 3 * N_HEADS and hk == LANES
    wq = (w_in[:, :HQ] * ATTN_SCALE).astype(BF16)
    spread = lambda w: jnp.concatenate([w[:, :HEAD_DIM], jnp.zeros_like(w[:, :HEAD_DIM]),
                                        w[:, HEAD_DIM:], jnp.zeros_like(w[:, :HEAD_DIM])], axis=1)
    wb = jnp.concatenate([spread(wkv[2]), spread(wkv[3]), _dup_groups(wkv[4]), _dup_groups(wkv[5])],
                         axis=1).astype(BF16)
    wf = jnp.concatenate([wkv[0], wkv[1], jnp.pad(w_in[:, HQ + 6 * hk:], ((0, 0), (0, LANES - n_gate)))],
                         axis=1).astype(BF16)
    kw = 2 * LANES
    q = norm_matmul(h, gains[0], jnp.concatenate([wq, wb], axis=1), BF16, tm, HQ + 4 * kw)
    q = q.reshape(bsz, seq, HQ + 4 * kw)
    kvb = q[..., HQ:HQ + 2 * kw]
    r = norm_matmul(h, gains[0], wf, F32, tm, 3 * LANES)
    kv = [r[:, i * hk:(i + 1) * hk].reshape(bsz, seq, hk) for i in range(2)]
    gates_raw = r[:, 2 * hk:2 * hk + LANES]

    ncr = seq // CMP_STRIDE
    half = CMP_BLOCK // 2
    assert half == CMP_STRIDE
    chunks = jnp.stack([kv[0], kv[1]]).reshape(2, bsz, ncr, half, g, HEAD_DIM)
    chunks = chunks.transpose(0, 1, 4, 2, 3, 5).reshape(2, bsz, g, ncr, half * HEAD_DIM)
    pos = cmp_pos.reshape(2, 2, half * HEAD_DIM)
    w1 = cmp_w1.reshape(2, 2, half * HEAD_DIM, -1).astype(BF16)
    cmp = compress(chunks, pos, w1, cmp_w2.astype(BF16))
    cmp = cmp.transpose(0, 1, 3, 2, 4).reshape(2, bsz, ncr, hk).astype(BF16)
    kcd, vcd = _dup_groups(cmp[0]), _dup_groups(cmp[1])

    ns = seq // SEL_BLOCK
    assert ns <= HEAD_DIM
    k_sel = min(N_SELECT, ns)
    starts = np.arange(ncr) * CMP_STRIDE
    blk = np.arange(HEAD_DIM)
    ov = ((starts[None, :] < (blk[:, None] + 1) * SEL_BLOCK)
          & (starts[None, :] + CMP_BLOCK > blk[:, None] * SEL_BLOCK)
          & (blk[:, None] < ns) & (starts[None, :] + CMP_BLOCK <= seq))
    o_c, pen = cmp_attention(q, kcd, vcd, jnp.asarray(ov.astype(np.float32), BF16), ns, k_sel)

    sel_bias = bias_expand(rel_table, _sel_bias_idx(seq))
    lane = np.arange(kw)[None, :]
    key_blk = np.arange(seq)[:, None] // SEL_BLOCK
    onehot = np.where((lane % LANES >= HEAD_DIM) & (lane % HEAD_DIM == key_blk), -NEG_INF, 0.0)
    ones = np.broadcast_to(np.where(lane % LANES >= HEAD_DIM, 1.0, 0.0), (seq, kw))
    kx = kvb[..., :kw] + jnp.asarray(onehot, BF16)[None]
    vx = kvb[..., kw:2 * kw] + jnp.asarray(ones, BF16)[None]
    o_s = sel_attention(q, kx, vx, pen, sel_bias)

    n_prev = -(-(C_WINDOW - 1) // QBLK)
    tqw = n_prev * QBLK
    wbias = bias_expand(rel_table, _banded_idx(n_prev, C_WINDOW - 1, 1))
    ident = lambda n, i: (n, i, 0)
    col = lambda c: (lambda n, i: (n, i, c))
    (o_w,) = banded_attention(q, q, q, wbias,
                              n_rows=bsz, n_tiles=seq // tqw, tq=tqw, n_prev=n_prev, kw=kw,
                              q_map=ident, k_map=col(HQ // kw + 2), v_map=col(HQ // kw + 3), o_map=ident,
                              out_shape=(bsz, seq, HQ), stack=8, name="nsa_window_attention")

    e = jnp.asarray(_head_expand_matrix(), BF16)
    return _proj_call(_proj_c_kernel, "nsa_gate_proj", min(tm, PROJ_ROWS),
                      [o_c.reshape(t, HQ), o_s.reshape(t, HQ), o_w.reshape(t, HQ), gates_raw], [e],
                      w_o.astype(BF16), gains[1], h)


def kernel(x, rel_table, norm_gains, a_w_in, a_w_o, b_w_in, b_sinks, b_w_o, c_w_in, c_cmp_pos, c_cmp_w1,
           c_cmp_w2, c_w_o, ffn_w_up, ffn_conv_w, ffn_conv_b, ffn_w_down):
    bsz, seq, d = x.shape
    depth = norm_gains.shape[0]
    h = x.reshape(bsz * seq, d)
    tm = _row_tile(seq)
    for i in range(depth):
        kind, j = i % 3, i // 3
        g = norm_gains[i]
        if kind == 0:
            h = mixer_a(h, g, a_w_in[j], a_w_o[j], rel_table, bsz, seq)
        elif kind == 1:
            h = mixer_b(h, g, b_w_in[j], b_sinks[j], b_w_o[j], rel_table, bsz, seq)
        else:
            h = mixer_c(h, g, c_w_in[j], c_cmp_pos[j], c_cmp_w1[j], c_cmp_w2[j], c_w_o[j], rel_table, bsz, seq)
        h = conv_ffn(h, g[2], ffn_w_up[i].astype(BF16), ffn_conv_w[i], ffn_conv_b[i],
                     ffn_w_down[i].astype(BF16), g[3], seq, min(tm, FFN_ROWS), FFN_COLS)
    return h.reshape(bsz, seq, d)
```

```python
import functools
import math

import numpy as np
import jax
import jax.numpy as jnp
from jax import lax
from jax.experimental import pallas as pl
from jax.experimental.pallas import tpu as pltpu

F32 = jnp.float32
BF16 = jnp.bfloat16

N_HEADS = 16
HEAD_DIM = 64
HQ = N_HEADS * HEAD_DIM
LANES = 128
ATTN_SCALE = HEAD_DIM ** -0.5
NUM_BUCKETS = 32
MAX_DISTANCE = 2048
RMS_EPS = 1e-6
NEG_INF = -1e30
FORCE_SCORE = 1e9
DIL_CONFIGS = ((128, 1), (512, 4), (2048, 16))
B_KV_HEADS = 2
B_WINDOW = 128
C_KV_HEADS = 2
CMP_BLOCK = 32
CMP_STRIDE = 16
SEL_BLOCK = 64
N_SELECT = 16
C_WINDOW = 512
CONV_WIDTH = 3
QBLK = 128
BAND_ROWS = 256
PROJ_ROWS = 512
PERM_ROWS = 2048
SEL_CHUNK = 512
SEL_QTILES = 2
FFN_COLS = 256
FFN_ROWS = 512
FFN_SUBTILES = 2
VMEM_LIMIT = 56 * 1024 * 1024


def _cparams(*sem):
    return pltpu.CompilerParams(dimension_semantics=sem, vmem_limit_bytes=VMEM_LIMIT)


def _t5_bucket_np(dist):
    max_exact = NUM_BUCKETS // 2
    d = np.maximum(dist, 0)
    df = np.maximum(d, 1).astype(np.float64)
    large = max_exact + np.floor(np.log(df / max_exact) / math.log(MAX_DISTANCE / max_exact)
                                 * (NUM_BUCKETS - max_exact) + 1e-9).astype(np.int64)
    large = np.minimum(large, NUM_BUCKETS - 1)
    return np.where(d < max_exact, d, large).astype(np.int32)


def _rms(x, g):
    ms = jnp.mean(x * x, axis=-1, keepdims=True)
    return (x * lax.rsqrt(ms + RMS_EPS)) * g


def _split3(w):
    hi = w.astype(BF16)
    r1 = w - hi.astype(F32)
    mid = r1.astype(BF16)
    lo = (r1 - mid.astype(F32)).astype(BF16)
    return hi, mid, lo


def _expand_heads(w, e3):
    lane = lax.broadcasted_iota(jnp.int32, w.shape, 1)
    r1 = w - w.astype(BF16).astype(F32)
    r2 = r1 - r1.astype(BF16).astype(F32)
    packed = jnp.where(lane < N_HEADS, w,
                       jnp.where(lane < 2 * N_HEADS, pltpu.roll(r1, N_HEADS, 1), pltpu.roll(r2, 2 * N_HEADS, 1)))
    return jnp.dot(packed.astype(BF16), e3, preferred_element_type=F32)


def _dot_nt(a, b):
    return lax.dot_general(a, b, (((1,), (1,)), ((), ())), preferred_element_type=F32)


def _stack_heads(q_tile, heads):
    lane = lax.broadcasted_iota(jnp.int32, (QBLK, LANES), 1)
    lo = lane < HEAD_DIM
    pieces = []
    for h in heads:
        q2 = q_tile(h // 2)
        keep = lo if h % 2 == 0 else jnp.logical_not(lo)
        pieces.append(jnp.where(keep, q2, jnp.zeros_like(q2)))
    return jnp.concatenate(pieces, axis=0)


def _merge_pairs(o, n_heads):
    lane = lax.broadcasted_iota(jnp.int32, (QBLK, LANES), 1)
    lo = lane < HEAD_DIM
    out = []
    for t in range(0, n_heads, 2):
        out.append(jnp.where(lo, o[t * QBLK:(t + 1) * QBLK], o[(t + 1) * QBLK:(t + 2) * QBLK]))
    return out


def _bias_kernel(tab_ref, rng_ref, idx_ref, o_ref):
    n = pl.program_id(0)
    idx = idx_ref[0]
    o_ref[...] = jnp.full(o_ref.shape, NEG_INF, F32)

    def body(b, carry):
        hit = idx == b
        for h in range(N_HEADS):
            o_ref[0, h] = jnp.where(hit, tab_ref[b, h], o_ref[0, h])
        return carry

    lax.fori_loop(rng_ref[n, 0], rng_ref[n, 1] + 1, body, 0)


def bias_expand(table, idx):
    n, r, c = idx.shape
    flat = idx.reshape(n, -1)
    lo = np.where(flat >= 0, flat, NUM_BUCKETS).min(axis=1)
    rng = np.stack([lo, flat.max(axis=1)], axis=1).astype(np.int32)
    return pl.pallas_call(
        _bias_kernel,
        grid=(n,),
        in_specs=[pl.BlockSpec(memory_space=pltpu.SMEM),
                  pl.BlockSpec(memory_space=pltpu.SMEM),
                  pl.BlockSpec((1, r, c), lambda i: (i, 0, 0))],
        out_specs=pl.BlockSpec((1, N_HEADS, r, c), lambda i: (i, 0, 0, 0)),
        out_shape=jax.ShapeDtypeStruct((n, N_HEADS, r, c), F32),
        compiler_params=_cparams("parallel"),
        name="bias_expand",
    )(table, jnp.asarray(rng), jnp.asarray(idx))


def _banded_idx(n_prev, max_dist, stride):
    kb = (n_prev + 1) * QBLK
    dist = np.arange(QBLK)[:, None] + n_prev * QBLK - np.arange(kb)[None, :]
    valid = (dist >= 0) & (dist <= max_dist)
    idx = np.where(valid, _t5_bucket_np(dist * stride), -1).astype(np.int32)
    if n_prev > 1:
        return idx[None]
    first = np.where(np.arange(kb)[None, :] < n_prev * QBLK, -1, idx).astype(np.int32)
    return np.stack([idx, first])


def _norm_mm_kernel(x_ref, g_ref, w_ref, o_ref, xn_ref):
    @pl.when(pl.program_id(1) == 0)
    def _():
        xn_ref[...] = _rms(x_ref[...], g_ref[...]).astype(xn_ref.dtype)

    o_ref[...] = jnp.dot(xn_ref[...], w_ref[...], preferred_element_type=F32).astype(o_ref.dtype)


def norm_matmul(x, gain, w, out_dtype, tm, tn):
    t, d = x.shape
    n = w.shape[1]
    scratch = [pltpu.VMEM((tm, d), BF16)]
    return pl.pallas_call(
        _norm_mm_kernel,
        grid=(t // tm, n // tn),
        in_specs=[pl.BlockSpec((tm, d), lambda i, j: (i, 0)),
                  pl.BlockSpec((1, d), lambda i, j: (0, 0)),
                  pl.BlockSpec((d, tn), lambda i, j: (0, j))],
        out_specs=pl.BlockSpec((tm, tn), lambda i, j: (i, j)),
        out_shape=jax.ShapeDtypeStruct((t, n), out_dtype),
        scratch_shapes=scratch,
        compiler_params=_cparams("parallel", "arbitrary"),
        name="norm_matmul",
    )(x, gain.reshape(1, d), w)


def _band_chunk(dil, seq):
    if seq % (BAND_ROWS * dil) == 0 and BAND_ROWS * dil <= PERM_ROWS:
        return BAND_ROWS
    return QBLK


def _norm_perm_kernel(x_ref, g_ref, *refs, dils, chunks):
    o_refs, xs_ref = refs[:len(dils)], refs[len(dils)]
    xn = _rms(x_ref[...], g_ref[...])
    nc = xs_ref.shape[0]
    for c in range(nc):
        xs_ref[c] = xn[:, c * LANES:(c + 1) * LANES]
    for o_ref, dil, cl in zip(o_refs, dils, chunks):
        if dil == 1:
            o_ref[...] = xn.astype(o_ref.dtype)
            continue
        span = cl * dil
        for u in range(x_ref.shape[0] // span):
            for r in range(dil):
                rows = jnp.concatenate(
                    [xs_ref[c, pl.ds(u * span + r, cl, stride=dil), :] for c in range(nc)], axis=1)
                dst = u * span + r * cl
                o_ref[dst:dst + cl, :] = rows.astype(o_ref.dtype)


def norm_permute(x, gain, dils, chunks, tm):
    t, d = x.shape
    assert all(tm % (cl * dil) == 0 for dil, cl in zip(dils, chunks))
    return pl.pallas_call(
        functools.partial(_norm_perm_kernel, dils=tuple(dils), chunks=tuple(chunks)),
        grid=(t // tm,),
        in_specs=[pl.BlockSpec((tm, d), lambda i: (i, 0)),
                  pl.BlockSpec((1, d), lambda i: (0, 0))],
        out_specs=[pl.BlockSpec((tm, d), lambda i: (i, 0)) for _ in dils],
        out_shape=[jax.ShapeDtypeStruct((t, d), BF16) for _ in dils],
        scratch_shapes=[pltpu.VMEM((d // LANES, tm, LANES), F32)],
        compiler_params=_cparams("parallel"),
        name="norm_permute",
    )(x, gain.reshape(1, d))


def _mm_kernel(x_ref, w_ref, o_ref):
    o_ref[...] = jnp.dot(x_ref[...], w_ref[...], preferred_element_type=F32).astype(o_ref.dtype)


def matmul_resident(x, w, out_dtype, tm):
    t, k = x.shape
    n = w.shape[1]
    return pl.pallas_call(
        _mm_kernel,
        grid=(t // tm,),
        in_specs=[pl.BlockSpec((tm, k), lambda i: (i, 0)),
                  pl.BlockSpec((k, n), lambda i: (0, 0))],
        out_specs=pl.BlockSpec((tm, n), lambda i: (i, 0)),
        out_shape=jax.ShapeDtypeStruct((t, n), out_dtype),
        compiler_params=_cparams("parallel"),
        name="matmul_resident",
    )(x, w)


def _finish_proj(y, w_ref, g_ref, h_ref, o_ref):
    z = jnp.dot(y.astype(BF16), w_ref[...], preferred_element_type=F32)
    o_ref[...] = h_ref[...] + _rms(z, g_ref[...])


def _proj_kernel(y_ref, w_ref, g_ref, h_ref, o_ref):
    _finish_proj(y_ref[...], w_ref, g_ref, h_ref, o_ref)


def _unpermute(src_ref, dst_ref, dil, cl, tm, i):
    span = cl * dil
    nc = dst_ref.shape[0]
    if span <= tm:
        for u in range(tm // span):
            for r in range(dil):
                lo = u * span + r * cl
                for c in range(nc):
                    dst_ref[c, pl.ds(u * span + r, cl, stride=dil), :] = src_ref[lo:lo + cl,
                                                                                 c * LANES:(c + 1) * LANES]
    else:
        per = tm // dil
        off = (i % (span // tm)) * per
        for r in range(dil):
            lo = pl.multiple_of(r * cl + off, 8)
            for c in range(nc):
                dst_ref[c, pl.ds(r, per, stride=dil), :] = src_ref[pl.ds(lo, per), c * LANES:(c + 1) * LANES]
    return jnp.concatenate([dst_ref[c] for c in range(nc)], axis=1) if nc > 1 else dst_ref[0]


def _proj_a_kernel(*refs, dils, chunks, tm):
    n = len(dils)
    o_refs, l_refs = refs[:n], refs[n:2 * n]
    e_ref, w_ref, g_ref, h_ref, out_ref = refs[2 * n:2 * n + 5]
    scratch = refs[2 * n + 5:]
    i = pl.program_id(0)
    outs, lses = [], []
    si = 0
    for gi, dil in enumerate(dils):
        if dil == 1:
            outs.append(o_refs[gi][...])
            lses.append(l_refs[gi][...])
        else:
            outs.append(_unpermute(o_refs[gi], scratch[si], dil, chunks[gi], tm, i))
            lses.append(_unpermute(l_refs[gi], scratch[si + 1], dil, chunks[gi], tm, i))
            si += 2
    mx = functools.reduce(jnp.maximum, lses)
    es = [jnp.exp(l - mx) for l in lses]
    inv = 1.0 / functools.reduce(lambda a, b: a + b, es)
    e = e_ref[...]
    y = outs[0] * _expand_heads(es[0] * inv, e)
    for gi in range(1, n):
        y = y + outs[gi] * _expand_heads(es[gi] * inv, e)
    _finish_proj(y, w_ref, g_ref, h_ref, out_ref)


def combine_proj_a(outs, lses, dils, chunks, e, w, gain, h, tm):
    t, d = h.shape

    def row_spec(width, dil, cl):
        rows = max(tm, cl * dil)
        return pl.BlockSpec((rows, width), lambda i, q=rows // tm: (i // q, 0))

    in_specs = [row_spec(HQ, dil, cl) for dil, cl in zip(dils, chunks)] + [
        row_spec(LANES, dil, cl) for dil, cl in zip(dils, chunks)] + [
        pl.BlockSpec(e.shape, lambda i: (0, 0)),
        pl.BlockSpec(w.shape, lambda i: (0, 0)),
        pl.BlockSpec((1, d), lambda i: (0, 0)),
        pl.BlockSpec((tm, d), lambda i: (i, 0))]
    scratch = []
    for dil in dils:
        if dil > 1:
            scratch += [pltpu.VMEM((HQ // LANES, tm, LANES), F32), pltpu.VMEM((1, tm, LANES), F32)]
    return pl.pallas_call(
        functools.partial(_proj_a_kernel, dils=tuple(dils), chunks=tuple(chunks), tm=tm),
        grid=(t // tm,),
        in_specs=in_specs,
        out_specs=pl.BlockSpec((tm, d), lambda i: (i, 0)),
        out_shape=jax.ShapeDtypeStruct((t, d), F32),
        scratch_shapes=scratch,
        compiler_params=_cparams("arbitrary"),
        name="dilated_combine_proj",
    )(*outs, *lses, e, w, gain.reshape(1, d), h)


def _proj_c_kernel(oc_ref, os_ref, ow_ref, gr_ref, e_ref, w_ref, g_ref, h_ref, o_ref):
    sig = jax.nn.sigmoid(gr_ref[...])
    e = e_ref[...]
    gate = lambda i: _expand_heads(sig if i == 0 else pltpu.roll(sig, LANES - i * N_HEADS, 1), e)
    y = gate(0) * oc_ref[...]
    y = y + gate(1) * os_ref[...]
    y = y + gate(2) * ow_ref[...]
    _finish_proj(y, w_ref, g_ref, h_ref, o_ref)


def _proj_call(kernel, name, tm, row_inputs, const_inputs, w, gain, h):
    t, d = h.shape
    k = w.shape[0]
    row_specs = [pl.BlockSpec((tm, a.shape[1]), lambda i: (i, 0)) for a in row_inputs]
    const_specs = [pl.BlockSpec(a.shape, (lambda nd: (lambda i: (0,) * nd))(a.ndim)) for a in const_inputs]
    return pl.pallas_call(
        kernel,
        grid=(t // tm,),
        in_specs=row_specs + const_specs + [
            pl.BlockSpec((k, d), lambda i: (0, 0)),
            pl.BlockSpec((1, d), lambda i: (0, 0)),
            pl.BlockSpec((tm, d), lambda i: (i, 0))],
        out_specs=pl.BlockSpec((tm, d), lambda i: (i, 0)),
        out_shape=jax.ShapeDtypeStruct((t, d), F32),
        compiler_params=_cparams("parallel"),
        name=name,
    )(*row_inputs, *const_inputs, w, gain.reshape(1, d), h)


def _head_expand_matrix():
    e = np.zeros((LANES, HQ), np.float32)
    for term in range(3):
        for h in range(N_HEADS):
            e[term * N_HEADS + h, h * HEAD_DIM:(h + 1) * HEAD_DIM] = 1.0
    return e


def _banded_kernel(*refs, tq, n_prev, n_chunks, stack, with_sinks, with_lse):
    q_ref, kp_ref, kc_ref, vp_ref, vc_ref, bias_ref = refs[:6]
    pos = 6
    sink_ref = None
    if with_sinks:
        sink_ref = refs[pos]
        pos += 1
    o_ref = refs[pos]
    pos += 1
    lse_ref = None
    if with_lse:
        lse_ref = refs[pos]
        pos += 1
    kbuf, vbuf = refs[pos], refs[pos + 1]

    i = pl.program_id(1)
    hpc = N_HEADS // n_chunks
    kb = (n_prev + 1) * QBLK
    kbuf[0:tq] = kp_ref[0]
    kbuf[tq:2 * tq] = kc_ref[0]
    vbuf[0:tq] = vp_ref[0]
    vbuf[tq:2 * tq] = vc_ref[0]
    lane = lax.broadcasted_iota(jnp.int32, (QBLK, LANES), 1)
    head_row = lax.broadcasted_iota(jnp.int32, (stack * QBLK, 1), 0)

    for s in range(tq // QBLK):
        r0 = s * QBLK
        k0 = tq + r0 - n_prev * QBLK
        variants = bias_ref.shape[0] > 1
        if variants:
            first = jnp.where(i == 0, 1, 0) if s == 0 else 0
        else:
            col = lax.broadcasted_iota(jnp.int32, (1, kb), 1) + k0
            kmask = jnp.where(jnp.logical_and(i == 0, col < tq), NEG_INF, 0.0).astype(F32)
        lse_acc = jnp.zeros((QBLK, LANES), F32)
        for h0 in range(0, N_HEADS, stack):
            heads = list(range(h0, h0 + stack))
            c = h0 // hpc
            qst = _stack_heads(lambda p: q_ref[0, r0:r0 + QBLK, p * LANES:(p + 1) * LANES], heads)
            kx = kbuf[k0:k0 + kb, c * LANES:(c + 1) * LANES]
            vx = vbuf[k0:k0 + kb, c * LANES:(c + 1) * LANES]
            sc = _dot_nt(qst, kx)
            if variants:
                sc = sc + bias_ref[first, h0:h0 + stack].reshape(stack * QBLK, kb)
            else:
                sc = sc + bias_ref[0, h0:h0 + stack].reshape(stack * QBLK, kb) + kmask
            m = jnp.max(sc, axis=-1, keepdims=True)
            if with_sinks:
                sk = sink_ref[h0 + stack - 1]
                for t in range(stack - 2, -1, -1):
                    sk = jnp.where(head_row < (t + 1) * QBLK, sink_ref[h0 + t], sk)
                m = jnp.maximum(m, sk)
            p = jnp.exp(sc - m)
            den = jnp.sum(p, axis=-1, keepdims=True)
            norm = den + jnp.exp(sk - m) if with_sinks else den
            o = jnp.dot(p.astype(BF16), vx, preferred_element_type=F32) * (1.0 / norm)
            for t2, blk in enumerate(_merge_pairs(o, stack)):
                pidx = h0 // 2 + t2
                o_ref[0, r0:r0 + QBLK, pidx * LANES:(pidx + 1) * LANES] = blk.astype(o_ref.dtype)
            if with_lse:
                lse = m + jnp.log(den)
                for t, h in enumerate(heads):
                    lse_acc = jnp.where(lane == h, lse[t * QBLK:(t + 1) * QBLK], lse_acc)
        if with_lse:
            lse_ref[0, r0:r0 + QBLK, :] = lse_acc


def banded_attention(q_arr, k_arr, v_arr, bias, *, n_rows, n_tiles, tq, n_prev, kw,
                     q_map, k_map, v_map, o_map, out_shape, lse_shape=None, sinks=None, stack=2, name):
    n_chunks = kw // LANES
    assert stack % 2 == 0 and (N_HEADS // n_chunks) % stack == 0
    kb = (n_prev + 1) * QBLK
    with_sinks = sinks is not None
    with_lse = lse_shape is not None

    def prev(fn):
        return lambda n, i: fn(n, jnp.maximum(i - 1, 0))

    in_specs = [pl.BlockSpec((1, tq, HQ), q_map),
                pl.BlockSpec((1, tq, kw), prev(k_map)),
                pl.BlockSpec((1, tq, kw), k_map),
                pl.BlockSpec((1, tq, kw), prev(v_map)),
                pl.BlockSpec((1, tq, kw), v_map),
                pl.BlockSpec(bias.shape, lambda n, i: (0, 0, 0, 0))]
    assert bias.shape[1:] == (N_HEADS, QBLK, kb) and (bias.shape[0] == 1 or n_prev == 1)
    args = [q_arr, k_arr, k_arr, v_arr, v_arr, bias]
    if with_sinks:
        in_specs.append(pl.BlockSpec(memory_space=pltpu.SMEM))
        args.append(sinks)
    out_specs = [pl.BlockSpec((1, tq, HQ), o_map)]
    out_shapes = [jax.ShapeDtypeStruct(out_shape, F32)]
    if with_lse:
        out_specs.append(pl.BlockSpec((1, tq, LANES), o_map))
        out_shapes.append(jax.ShapeDtypeStruct(lse_shape, F32))
    res = pl.pallas_call(
        functools.partial(_banded_kernel, tq=tq, n_prev=n_prev, n_chunks=n_chunks, stack=stack,
                          with_sinks=with_sinks, with_lse=with_lse),
        grid=(n_rows, n_tiles),
        in_specs=in_specs,
        out_specs=out_specs,
        out_shape=out_shapes,
        scratch_shapes=[pltpu.VMEM((2 * tq, kw), BF16), pltpu.VMEM((2 * tq, kw), BF16)],
        compiler_params=_cparams("parallel", "arbitrary"),
        name=name,
    )(*args)
    return res


def _dup_groups(x):
    g0, g1 = x[..., :HEAD_DIM], x[..., HEAD_DIM:]
    return jnp.concatenate([g0, g0, g1, g1], axis=-1)


def _compress_kernel(ch_ref, pos_ref, w1_ref, w2_ref, o_ref):
    ch = ch_ref[0, 0, 0]
    rows = ch.shape[0]
    posv = pos_ref[0]
    a = jnp.dot((ch + posv[0:1]).astype(BF16), w1_ref[0, 0], preferred_element_type=F32)
    b = jnp.dot((ch + posv[1:2]).astype(BF16), w1_ref[0, 1], preferred_element_type=F32)
    hid = a + pltpu.roll(b, rows - 1, 0)
    act = jax.nn.gelu(hid, approximate=True)
    o_ref[0, 0, 0] = jnp.dot(act.astype(BF16), w2_ref[0], preferred_element_type=F32)


def compress(chunks, pos, w1, w2):
    _, b, g, rows, width = chunks.shape
    hid = w1.shape[-1]
    return pl.pallas_call(
        _compress_kernel,
        grid=(2, b, g),
        in_specs=[pl.BlockSpec((1, 1, 1, rows, width), lambda i, bb, gg: (i, bb, gg, 0, 0)),
                  pl.BlockSpec((1, 2, width), lambda i, bb, gg: (i, 0, 0)),
                  pl.BlockSpec((1, 2, width, hid), lambda i, bb, gg: (i, 0, 0, 0)),
                  pl.BlockSpec((1, hid, HEAD_DIM), lambda i, bb, gg: (i, 0, 0))],
        out_specs=pl.BlockSpec((1, 1, 1, rows, HEAD_DIM), lambda i, bb, gg: (i, bb, gg, 0, 0)),
        out_shape=jax.ShapeDtypeStruct((2, b, g, rows, HEAD_DIM), F32),
        compiler_params=_cparams("parallel", "parallel", "parallel"),
        name="nsa_compress",
    )(chunks, pos, w1, w2)


def _cmp_attn_kernel(q_ref, kc_ref, vc_ref, ov_ref, place_ref, o_ref, sel_ref, *, n_sel_blocks, k_sel):
    qi = pl.program_id(1)
    ncr = kc_ref.shape[1]
    hpc = N_HEADS // C_KV_HEADS
    qpos = qi * QBLK + lax.broadcasted_iota(jnp.int32, (QBLK, 1), 0)
    cidx = lax.broadcasted_iota(jnp.int32, (1, ncr), 1)
    valid = (cidx * CMP_STRIDE + (CMP_BLOCK - 1)) <= qpos
    maskc = jnp.where(valid, 0.0, NEG_INF).astype(F32)
    anyv = (qpos >= CMP_BLOCK - 1).astype(F32)
    blk_id = lax.broadcasted_iota(jnp.int32, (HEAD_DIM, QBLK), 0)
    cur = (qi * QBLK + lax.broadcasted_iota(jnp.int32, (HEAD_DIM, QBLK), 1)) // SEL_BLOCK
    forced = jnp.logical_or(jnp.logical_or(blk_id == 0, blk_id == cur), blk_id == cur - 1)
    allowed = blk_id <= cur
    for g in range(C_KV_HEADS):
        heads = list(range(g * hpc, (g + 1) * hpc))
        qst = _stack_heads(lambda p: q_ref[0, :, p * LANES:(p + 1) * LANES], heads)
        sc = _dot_nt(qst, kc_ref[0, :, g * LANES:(g + 1) * LANES])
        sc3 = sc.reshape(hpc, QBLK, ncr) + maskc[None]
        m = jnp.max(sc3, axis=-1, keepdims=True)
        e = jnp.exp(sc3 - m)
        ssum = jnp.sum(e, axis=-1, keepdims=True)
        p = (e * (1.0 / ssum)) * anyv[None]
        o = jnp.dot(p.reshape(hpc * QBLK, ncr).astype(BF16), vc_ref[0, :, g * LANES:(g + 1) * LANES],
                    preferred_element_type=F32)
        for t2, blk in enumerate(_merge_pairs(o, hpc)):
            pidx = (g * hpc) // 2 + t2
            o_ref[0, :, pidx * LANES:(pidx + 1) * LANES] = blk
        hi, mid, lo = _split3(jnp.sum(p, axis=0))
        ovt = ov_ref[...]
        imp = (_dot_nt(ovt, hi) + _dot_nt(ovt, mid)) + _dot_nt(ovt, lo)
        score = jnp.where(forced, FORCE_SCORE, jnp.where(allowed, imp, NEG_INF))
        bits = pltpu.bitcast(score, jnp.int32)
        key = jnp.where(bits < 0, bits ^ jnp.int32(0x7FFFFFFF), bits)
        key_m1 = key - 1
        rank = jnp.zeros((HEAD_DIM, QBLK), jnp.int32)
        for i in range(n_sel_blocks):
            thr = jnp.where(blk_id > i, key_m1, key)
            rank = rank + jnp.where(key[i:i + 1, :] > thr, 1, 0)
        keep = jnp.logical_and(rank < k_sel, blk_id < n_sel_blocks)
        pen_t = jnp.where(keep, 0.0, -1.0).astype(BF16)
        pen = lax.dot_general(pen_t, place_ref[...], (((0,), (0,)), ((), ())), preferred_element_type=F32)
        sel_ref[0, :, g * LANES:(g + 1) * LANES] = pen.astype(sel_ref.dtype)


def cmp_attention(q, kcd, vcd, ov, n_sel_blocks, k_sel):
    b, s, _ = q.shape
    ncr = kcd.shape[1]
    place = np.zeros((HEAD_DIM, LANES), np.float32)
    place[np.arange(HEAD_DIM), HEAD_DIM + np.arange(HEAD_DIM)] = 1.0
    return pl.pallas_call(
        functools.partial(_cmp_attn_kernel, n_sel_blocks=n_sel_blocks, k_sel=k_sel),
        grid=(b, s // QBLK),
        in_specs=[pl.BlockSpec((1, QBLK, HQ), lambda bb, i: (bb, i, 0)),
                  pl.BlockSpec((1, ncr, 2 * LANES), lambda bb, i: (bb, 0, 0)),
                  pl.BlockSpec((1, ncr, 2 * LANES), lambda bb, i: (bb, 0, 0)),
                  pl.BlockSpec((HEAD_DIM, ncr), lambda bb, i: (0, 0)),
                  pl.BlockSpec((HEAD_DIM, LANES), lambda bb, i: (0, 0))],
        out_specs=[pl.BlockSpec((1, QBLK, HQ), lambda bb, i: (bb, i, 0)),
                   pl.BlockSpec((1, QBLK, 2 * LANES), lambda bb, i: (bb, i, 0))],
        out_shape=[jax.ShapeDtypeStruct((b, s, HQ), F32),
                   jax.ShapeDtypeStruct((b, s, 2 * LANES), BF16)],
        compiler_params=_cparams("parallel", "parallel"),
        name="nsa_cmp_attention",
    )(q, kcd, vcd, ov, jnp.asarray(place, BF16))


def _sel_attn_kernel(q_ref, pen_ref, k_ref, v_ref, bias_ref, o_ref, qst_ref, m_ref, acc_ref,
                     sa_ref, sb_ref, *, nbt):
    qi0 = pl.program_id(2) * SEL_QTILES
    hpc = N_HEADS // C_KV_HEADS
    nsub = SEL_CHUNK // QBLK
    rpt = hpc * QBLK
    lane = lax.broadcasted_iota(jnp.int32, (QBLK, LANES), 1)
    lo = lane < HEAD_DIM
    for w in range(SEL_QTILES):
        pen = pen_ref[0, w * QBLK:(w + 1) * QBLK, :].astype(F32)
        for t in range(hpc):
            q2 = q_ref[0, w * QBLK:(w + 1) * QBLK, (t // 2) * LANES:(t // 2 + 1) * LANES].astype(F32)
            if t % 2:
                q2 = pltpu.roll(q2, HEAD_DIM, 1)
            qst_ref[w * rpt + t * QBLK:w * rpt + (t + 1) * QBLK, :] = jnp.where(lo, q2, pen).astype(BF16)
    m_ref[...] = jnp.full(m_ref.shape, NEG_INF, F32)
    acc_ref[...] = jnp.zeros(acc_ref.shape, F32)

    n_chunks = (qi0 + SEL_QTILES - 1) // nsub + 1

    def scores(c, s_ref):
        k0 = pl.multiple_of(c * SEL_CHUNK, SEL_CHUNK)
        s_ref[...] = _dot_nt(qst_ref[...], k_ref[0, pl.ds(k0, SEL_CHUNK), :])

    def accumulate(c, s_ref):
        k0 = pl.multiple_of(c * SEL_CHUNK, SEL_CHUNK)
        ps, alphas = [], []
        for w in range(SEL_QTILES):
            rows = slice(w * rpt, (w + 1) * rpt)
            tiles = []
            for u in range(nsub):
                off = qi0 + w - nsub * c - u
                idx = jnp.where(off < 0, nbt, jnp.minimum(off, nbt - 1))
                tiles.append(s_ref[rows, u * QBLK:(u + 1) * QBLK] + bias_ref[idx].reshape(rpt, QBLK))
            m_old = m_ref[rows, :]
            m_new = jnp.maximum(m_old, jnp.max(functools.reduce(jnp.maximum, tiles), axis=-1, keepdims=True))
            ps.append(jnp.concatenate([jnp.exp(tl - m_new).astype(BF16) for tl in tiles], axis=1))
            alphas.append(jnp.exp(m_old - m_new))
            m_ref[rows, :] = m_new
        pv = jnp.dot(jnp.concatenate(ps, axis=0), v_ref[0, pl.ds(k0, SEL_CHUNK), :], preferred_element_type=F32)
        acc_ref[...] = jnp.concatenate(alphas, axis=0) * acc_ref[...] + pv

    scores(0, sa_ref)

    def body(cc, carry):
        c = 2 * cc
        scores(c + 1, sb_ref)
        accumulate(c, sa_ref)
        scores(jnp.minimum(c + 2, n_chunks - 1), sa_ref)
        accumulate(c + 1, sb_ref)
        return carry

    lax.fori_loop(0, n_chunks // 2, body, 0)

    @pl.when(n_chunks % 2 == 1)
    def _():
        accumulate(n_chunks - 1, sa_ref)

    acc = acc_ref[...]
    rolled = pltpu.roll(acc, HEAD_DIM, 1)
    for w in range(SEL_QTILES):
        for t2 in range(hpc // 2):
            ev = slice(w * rpt + 2 * t2 * QBLK, w * rpt + (2 * t2 + 1) * QBLK)
            od = slice(w * rpt + (2 * t2 + 1) * QBLK, w * rpt + (2 * t2 + 2) * QBLK)
            even = acc[ev] * (1.0 / rolled[ev])
            odd = rolled[od] * (1.0 / acc[od])
            o_ref[0, w * QBLK:(w + 1) * QBLK, t2 * LANES:(t2 + 1) * LANES] = jnp.where(lo, even, odd)


def sel_attention(q, kx, vx, pen, bias_tiles):
    b, s, _ = q.shape
    hpc = N_HEADS // C_KV_HEADS
    gw = hpc * HEAD_DIM
    nbt = bias_tiles.shape[0] - 1
    tq = SEL_QTILES * QBLK
    rows = SEL_QTILES * hpc * QBLK
    assert s % SEL_CHUNK == 0 and s % tq == 0
    return pl.pallas_call(
        functools.partial(_sel_attn_kernel, nbt=nbt),
        grid=(b, C_KV_HEADS, s // tq),
        in_specs=[pl.BlockSpec((1, tq, gw), lambda bb, g, i: (bb, i, g)),
                  pl.BlockSpec((1, tq, LANES), lambda bb, g, i: (bb, i, g)),
                  pl.BlockSpec((1, s, LANES), lambda bb, g, i: (bb, 0, g)),
                  pl.BlockSpec((1, s, LANES), lambda bb, g, i: (bb, 0, g)),
                  pl.BlockSpec((nbt + 1, hpc, QBLK, QBLK), lambda bb, g, i: (0, g, 0, 0))],
        out_specs=pl.BlockSpec((1, tq, gw), lambda bb, g, i: (bb, i, g)),
        out_shape=jax.ShapeDtypeStruct((b, s, HQ), F32),
        scratch_shapes=[pltpu.VMEM((rows, LANES), BF16),
                        pltpu.VMEM((rows, LANES), F32),
                        pltpu.VMEM((rows, LANES), F32),
                        pltpu.VMEM((rows, SEL_CHUNK), F32),
                        pltpu.VMEM((rows, SEL_CHUNK), F32)],
        compiler_params=_cparams("parallel", "parallel", "arbitrary"),
        name="nsa_sel_attention",
    )(q, pen, kx, vx, bias_tiles)


def _sel_bias_idx(s):
    nqt = s // QBLK
    far = -(-(int(np.argmax(_t5_bucket_np(np.arange(4 * MAX_DISTANCE)) == NUM_BUCKETS - 1)) + QBLK) // QBLK)
    nbt = min(nqt, far + 1)
    d0 = np.arange(nbt)[:, None, None] * QBLK
    dist = d0 + np.arange(QBLK)[None, :, None] - np.arange(QBLK)[None, None, :]
    idx = np.where(dist >= 0, _t5_bucket_np(dist), -1).astype(np.int32)
    return np.concatenate([idx, np.full((1, QBLK, QBLK), -1, np.int32)])


def _ffn_kernel(h_ref, g2_ref, wu_ref, cw_ref, cb_ref, wd_ref, g3_ref,
                o_ref, xn_ref, acc_ref, u_ref, a_ref, carry_ref, *, tiles_per_seq, nj):
    i = pl.program_id(0)
    ns, ts, tn = xn_ref.shape[0], xn_ref.shape[1], a_ref.shape[3]
    cols = lambda c: slice(c * tn, (c + 1) * tn)

    @pl.when(i % tiles_per_seq == 0)
    def _():
        carry_ref[...] = jnp.zeros(carry_ref.shape, F32)

    top = 16
    rowt = lax.broadcasted_iota(jnp.int32, (top, tn), 0)

    def up(s, j):
        xn = xn_ref[s]
        u_ref[s, j % 2, 0] = jnp.dot(xn, wu_ref[:, cols(j)], preferred_element_type=F32)
        u_ref[s, j % 2, 1] = jnp.dot(xn, wu_ref[:, cols(nj + j)], preferred_element_type=F32)

    def conv(u, s1, s2, c):
        cw = cw_ref[:, cols(c)]
        return ((cb_ref[:, cols(c)] + u * cw[2:3]) + s2 * cw[0:1]) + s1 * cw[1:2]

    def conv_body(u, c):
        return conv(u, pltpu.roll(u, 1, 0), pltpu.roll(u, 2, 0), c)

    def conv_top(uc_ref, c):
        prev = carry_ref[c]
        p1, p2 = prev[7:8], prev[6:7]
        u = uc_ref[0:top]
        s1 = jnp.where(rowt == 0, p1, pltpu.roll(u, 1, 0))
        s2 = jnp.where(rowt == 0, p2, jnp.where(rowt == 1, p1, pltpu.roll(u, 2, 0)))
        carry_ref[c] = uc_ref[ts - 8:ts]
        return conv(u, s1, s2, c)

    def gated(cg, cv):
        return (jax.nn.gelu(cg, approximate=True) * cv).astype(BF16)

    def act(s, j):
        ur, ar = u_ref.at[s, j % 2], a_ref.at[s, j % 2]
        ar[...] = gated(conv_body(ur[0], j), conv_body(ur[1], nj + j))
        ar[0:top] = gated(conv_top(ur.at[0], j), conv_top(ur.at[1], nj + j))

    def down(s, j):
        acc_ref[s] += jnp.dot(a_ref[s, j % 2], wd_ref[j * tn:(j + 1) * tn, :], preferred_element_type=F32)

    def iteration(s, k):
        rows = slice(s * ts, (s + 1) * ts)
        if k == -1:
            xn_ref[s] = _rms(h_ref[rows, :], g2_ref[...]).astype(xn_ref.dtype)
            acc_ref[s] = jnp.zeros(acc_ref.shape[1:], F32)
        if 0 <= k + 1 < nj:
            up(s, k + 1)
        if 0 <= k < nj:
            act(s, k)
        if 0 <= k - 1 < nj:
            down(s, k - 1)
        if k == nj:
            o_ref[rows, :] = h_ref[rows, :] + _rms(acc_ref[s], g3_ref[...])

    for k in range(-1, nj + 1):
        for s in range(ns):
            iteration(s, k)


def conv_ffn(h, g2, w_up, conv_w, conv_b, w_down, g3, seq, tm, tn, ns=FFN_SUBTILES):
    t, d = h.shape
    dff = w_down.shape[0]
    nj = dff // tn
    ts = tm // ns
    assert nj * tn == dff and ts * ns == tm and ts % 16 == 0
    wu, cw, cb, wd = w_up, conv_w, conv_b.reshape(1, -1), w_down
    resident = lambda a: pl.BlockSpec(a.shape, (lambda nd: (lambda i: (0,) * nd))(a.ndim))
    return pl.pallas_call(
        functools.partial(_ffn_kernel, tiles_per_seq=seq // tm, nj=nj),
        grid=(t // tm,),
        in_specs=[pl.BlockSpec((tm, d), lambda i: (i, 0)),
                  pl.BlockSpec((1, d), lambda i: (0, 0)),
                  resident(wu), resident(cw), resident(cb), resident(wd),
                  pl.BlockSpec((1, d), lambda i: (0, 0))],
        out_specs=pl.BlockSpec((tm, d), lambda i: (i, 0)),
        out_shape=jax.ShapeDtypeStruct((t, d), F32),
        scratch_shapes=[pltpu.VMEM((ns, ts, d), BF16), pltpu.VMEM((ns, ts, d), F32),
                        pltpu.VMEM((ns, 2, 2, ts, tn), F32), pltpu.VMEM((ns, 2, ts, tn), BF16),
                        pltpu.VMEM((2 * nj, 8, tn), F32)],
        compiler_params=_cparams("arbitrary"),
        name="conv_ffn",
    )(h, g2.reshape(1, d), wu, cw, cb, wd, g3.reshape(1, d))


def _row_tile(t):
    for tm in (1024, 512, 256, 128):
        if t % tm == 0:
            return tm
    raise ValueError(f"token count {t} is not a multiple of 128")


def mixer_a(h, gains, w_in, w_o, rel_table, bsz, seq):
    t, d = h.shape
    tm = _row_tile(t)
    n_dil = len(DIL_CONFIGS)
    a_in = w_in.shape[1]
    col_scale = np.ones((a_in,), np.float32).reshape(n_dil, 3, HQ)
    col_scale[:, 0] = ATTN_SCALE
    w = (w_in * col_scale.reshape(1, a_in)).astype(BF16)
    idx = np.concatenate([_banded_idx(1, window // dil, dil) for window, dil in DIL_CONFIGS])
    bias = bias_expand(rel_table, idx)
    outs, lses = [], []
    dils = [dil for _, dil in DIL_CONFIGS]
    chunks = [_band_chunk(dil, seq) for dil in dils]
    xns = norm_permute(h, gains[0], dils, chunks, max([tm] + [cl * dil for dil, cl in zip(dils, chunks)]))
    for gi, (window, dil) in enumerate(DIL_CONFIGS):
        tq = chunks[gi]
        assert window // dil <= QBLK and seq % (tq * dil) == 0
        qkv = matmul_resident(xns[gi], w[:, gi * 3 * HQ:(gi + 1) * 3 * HQ], BF16, tm)
        qkv = qkv.reshape(1, t, 3 * HQ)

        def rmap(part, dil=dil, per_seq=seq // tq):
            return lambda n, i: (0, (n // dil) * per_seq + i * dil + n % dil, part)

        o, lse = banded_attention(
            qkv, qkv, qkv, bias[2 * gi:2 * gi + 2], n_rows=bsz * dil, n_tiles=seq // (tq * dil), tq=tq, n_prev=1, kw=HQ,
            q_map=rmap(0), k_map=rmap(1), v_map=rmap(2), o_map=rmap(0),
            out_shape=(1, t, HQ), lse_shape=(1, t, LANES), name=f"dilated_attention_{dil}")
        outs.append(o.reshape(t, HQ))
        lses.append(lse.reshape(t, LANES))
    e = jnp.asarray(_head_expand_matrix(), BF16)
    return combine_proj_a(outs, lses, dils, chunks, e, w_o.astype(BF16), gains[1], h, min(tm, PROJ_ROWS))


def mixer_b(h, gains, w_in, sinks, w_o, rel_table, bsz, seq):
    t, d = h.shape
    tm = _row_tile(t)
    hk = B_KV_HEADS * HEAD_DIM
    n_in = w_in.shape[1]
    assert n_in == HQ + 2 * hk
    w = jnp.concatenate([w_in[:, :HQ] * ATTN_SCALE, _dup_groups(w_in[:, HQ:HQ + hk]),
                         _dup_groups(w_in[:, HQ + hk:])], axis=1).astype(BF16)
    kw = 2 * hk
    qkv = norm_matmul(h, gains[0], w, BF16, tm, HQ + 2 * kw).reshape(bsz, seq, HQ + 2 * kw)
    bias = bias_expand(rel_table, _banded_idx(1, B_WINDOW - 1, 1))
    sink_rows = sinks.astype(F32)
    ident = lambda n, i: (n, i, 0)
    col = lambda c: (lambda n, i: (n, i, c))
    (o,) = banded_attention(qkv, qkv, qkv, bias, n_rows=bsz, n_tiles=seq // BAND_ROWS, tq=BAND_ROWS, n_prev=1,
                            kw=kw, q_map=ident, k_map=col(HQ // kw), v_map=col(HQ // kw + 1), o_map=ident,
                            out_shape=(bsz, seq, HQ), sinks=sink_rows, stack=8, name="sink_window_attention")
    return _proj_call(_proj_kernel, "sink_proj", tm, [o.reshape(t, HQ)], [], w_o.astype(BF16), gains[1], h)


def mixer_c(h, gains, w_in, cmp_pos, cmp_w1, cmp_w2, w_o, rel_table, bsz, seq):
    t, d = h.shape
    tm = _row_tile(t)
    g = C_KV_HEADS
    hk = g * HEAD_DIM
    wkv = [w_in[:, HQ + i * hk:HQ + (i + 1) * hk] for i in range(6)]
    n_gate = w_in.shape[1] - HQ - 6 * hk
    assert n_gate == 3 * N_HEADS and hk == LANES
    wq = (w_in[:, :HQ] * ATTN_SCALE).astype(BF16)
    spread = lambda w: jnp.concatenate([w[:, :HEAD_DIM], jnp.zeros_like(w[:, :HEAD_DIM]),
                                        w[:, HEAD_DIM:], jnp.zeros_like(w[:, :HEAD_DIM])], axis=1)
    wb = jnp.concatenate([spread(wkv[2]), spread(wkv[3]), _dup_groups(wkv[4]), _dup_groups(wkv[5])],
                         axis=1).astype(BF16)
    wf = jnp.concatenate([wkv[0], wkv[1], jnp.pad(w_in[:, HQ + 6 * hk:], ((0, 0), (0, LANES - n_gate)))],
                         axis=1).astype(BF16)
    kw = 2 * LANES
    q = norm_matmul(h, gains[0], jnp.concatenate([wq, wb], axis=1), BF16, tm, HQ + 4 * kw)
    q = q.reshape(bsz, seq, HQ + 4 * kw)
    kvb = q[..., HQ:HQ + 2 * kw]
    r = norm_matmul(h, gains[0], wf, F32, tm, 3 * LANES)
    kv = [r[:, i * hk:(i + 1) * hk].reshape(bsz, seq, hk) for i in range(2)]
    gates_raw = r[:, 2 * hk:2 * hk + LANES]

    ncr = seq // CMP_STRIDE
    half = CMP_BLOCK // 2
    assert half == CMP_STRIDE
    chunks = jnp.stack([kv[0], kv[1]]).reshape(2, bsz, ncr, half, g, HEAD_DIM)
    chunks = chunks.transpose(0, 1, 4, 2, 3, 5).reshape(2, bsz, g, ncr, half * HEAD_DIM)
    pos = cmp_pos.reshape(2, 2, half * HEAD_DIM)
    w1 = cmp_w1.reshape(2, 2, half * HEAD_DIM, -1).astype(BF16)
    cmp = compress(chunks, pos, w1, cmp_w2.astype(BF16))
    cmp = cmp.transpose(0, 1, 3, 2, 4).reshape(2, bsz, ncr, hk).astype(BF16)
    kcd, vcd = _dup_groups(cmp[0]), _dup_groups(cmp[1])

    ns = seq // SEL_BLOCK
    assert ns <= HEAD_DIM
    k_sel = min(N_SELECT, ns)
    starts = np.arange(ncr) * CMP_STRIDE
    blk = np.arange(HEAD_DIM)
    ov = ((starts[None, :] < (blk[:, None] + 1) * SEL_BLOCK)
          & (starts[None, :] + CMP_BLOCK > blk[:, None] * SEL_BLOCK)
          & (blk[:, None] < ns) & (starts[None, :] + CMP_BLOCK <= seq))
    o_c, pen = cmp_attention(q, kcd, vcd, jnp.asarray(ov.astype(np.float32), BF16), ns, k_sel)

    sel_bias = bias_expand(rel_table, _sel_bias_idx(seq))
    lane = np.arange(kw)[None, :]
    key_blk = np.arange(seq)[:, None] // SEL_BLOCK
    onehot = np.where((lane % LANES >= HEAD_DIM) & (lane % HEAD_DIM == key_blk), -NEG_INF, 0.0)
    ones = np.broadcast_to(np.where(lane % LANES >= HEAD_DIM, 1.0, 0.0), (seq, kw))
    kx = kvb[..., :kw] + jnp.asarray(onehot, BF16)[None]
    vx = kvb[..., kw:2 * kw] + jnp.asarray(ones, BF16)[None]
    o_s = sel_attention(q, kx, vx, pen, sel_bias)

    n_prev = -(-(C_WINDOW - 1) // QBLK)
    tqw = n_prev * QBLK
    wbias = bias_expand(rel_table, _banded_idx(n_prev, C_WINDOW - 1, 1))
    ident = lambda n, i: (n, i, 0)
    col = lambda c: (lambda n, i: (n, i, c))
    (o_w,) = banded_attention(q, q, q, wbias,
                              n_rows=bsz, n_tiles=seq // tqw, tq=tqw, n_prev=n_prev, kw=kw,
                              q_map=ident, k_map=col(HQ // kw + 2), v_map=col(HQ // kw + 3), o_map=ident,
                              out_shape=(bsz, seq, HQ), stack=8, name="nsa_window_attention")

    e = jnp.asarray(_head_expand_matrix(), BF16)
    return _proj_call(_proj_c_kernel, "nsa_gate_proj", min(tm, PROJ_ROWS),
                      [o_c.reshape(t, HQ), o_s.reshape(t, HQ), o_w.reshape(t, HQ), gates_raw], [e],
                      w_o.astype(BF16), gains[1], h)


def kernel(x, rel_table, norm_gains, a_w_in, a_w_o, b_w_in, b_sinks, b_w_o, c_w_in, c_cmp_pos, c_cmp_w1,
           c_cmp_w2, c_w_o, ffn_w_up, ffn_conv_w, ffn_conv_b, ffn_w_down):
    bsz, seq, d = x.shape
    depth = norm_gains.shape[0]
    h = x.reshape(bsz * seq, d)
    tm = _row_tile(seq)
    for i in range(depth):
        kind, j = i % 3, i // 3
        g = norm_gains[i]
        if kind == 0:
            h = mixer_a(h, g, a_w_in[j], a_w_o[j], rel_table, bsz, seq)
        elif kind == 1:
            h = mixer_b(h, g, b_w_in[j], b_sinks[j], b_w_o[j], rel_table, bsz, seq)
        else:
            h = mixer_c(h, g, c_w_in[j], c_cmp_pos[j], c_cmp_w1[j], c_cmp_w2[j], c_w_o[j], rel_table, bsz, seq)
        h = conv_ffn(h, g[2], ffn_w_up[i].astype(BF16), ffn_conv_w[i], ffn_conv_b[i],
                     ffn_w_down[i].astype(BF16), g[3], seq, min(tm, FFN_ROWS), FFN_COLS)
    return h.reshape(bsz, seq, d)
```

```python
import functools
import math

import numpy as np
import jax
import jax.numpy as jnp
from jax import lax
from jax.experimental import pallas as pl
from jax.experimental.pallas import tpu as pltpu

F32 = jnp.float32
BF16 = jnp.bfloat16

N_HEADS = 16
HEAD_DIM = 64
HQ = N_HEADS * HEAD_DIM
LANES = 128
ATTN_SCALE = HEAD_DIM ** -0.5
NUM_BUCKETS = 32
MAX_DISTANCE = 2048
RMS_EPS = 1e-6
NEG_INF = -1e30
FORCE_SCORE = 1e9
DIL_CONFIGS = ((128, 1), (512, 4), (2048, 16))
B_KV_HEADS = 2
B_WINDOW = 128
C_KV_HEADS = 2
CMP_BLOCK = 32
CMP_STRIDE = 16
SEL_BLOCK = 64
N_SELECT = 16
C_WINDOW = 512
CONV_WIDTH = 3
QBLK = 128
BAND_ROWS = 256
PROJ_ROWS = 512
PERM_ROWS = 2048
SEL_CHUNK = 512
SEL_QTILES = 2
FFN_COLS = 256
FFN_ROWS = 512
FFN_SUBTILES = 2
VMEM_LIMIT = 56 * 1024 * 1024


def _cparams(*sem):
    return pltpu.CompilerParams(dimension_semantics=sem, vmem_limit_bytes=VMEM_LIMIT)


def _t5_bucket_np(dist):
    max_exact = NUM_BUCKETS // 2
    d = np.maximum(dist, 0)
    df = np.maximum(d, 1).astype(np.float64)
    large = max_exact + np.floor(np.log(df / max_exact) / math.log(MAX_DISTANCE / max_exact)
                                 * (NUM_BUCKETS - max_exact) + 1e-9).astype(np.int64)
    large = np.minimum(large, NUM_BUCKETS - 1)
    return np.where(d < max_exact, d, large).astype(np.int32)


def _rms(x, g):
    ms = jnp.mean(x * x, axis=-1, keepdims=True)
    return (x * lax.rsqrt(ms + RMS_EPS)) * g


def _split3(w):
    hi = w.astype(BF16)
    r1 = w - hi.astype(F32)
    mid = r1.astype(BF16)
    lo = (r1 - mid.astype(F32)).astype(BF16)
    return hi, mid, lo


def _expand_heads(w, e3):
    lane = lax.broadcasted_iota(jnp.int32, w.shape, 1)
    r1 = w - w.astype(BF16).astype(F32)
    r2 = r1 - r1.astype(BF16).astype(F32)
    packed = jnp.where(lane < N_HEADS, w,
                       jnp.where(lane < 2 * N_HEADS, pltpu.roll(r1, N_HEADS, 1), pltpu.roll(r2, 2 * N_HEADS, 1)))
    return jnp.dot(packed.astype(BF16), e3, preferred_element_type=F32)


def _dot_nt(a, b):
    return lax.dot_general(a, b, (((1,), (1,)), ((), ())), preferred_element_type=F32)


def _stack_heads(q_tile, heads):
    lane = lax.broadcasted_iota(jnp.int32, (QBLK, LANES), 1)
    lo = lane < HEAD_DIM
    pieces = []
    for h in heads:
        q2 = q_tile(h // 2)
        keep = lo if h % 2 == 0 else jnp.logical_not(lo)
        pieces.append(jnp.where(keep, q2, jnp.zeros_like(q2)))
    return jnp.concatenate(pieces, axis=0)


def _merge_pairs(o, n_heads):
    lane = lax.broadcasted_iota(jnp.int32, (QBLK, LANES), 1)
    lo = lane < HEAD_DIM
    out = []
    for t in range(0, n_heads, 2):
        out.append(jnp.where(lo, o[t * QBLK:(t + 1) * QBLK], o[(t + 1) * QBLK:(t + 2) * QBLK]))
    return out


def _bias_kernel(tab_ref, rng_ref, idx_ref, o_ref):
    n = pl.program_id(0)
    idx = idx_ref[0]
    o_ref[...] = jnp.full(o_ref.shape, NEG_INF, F32)

    def body(b, carry):
        hit = idx == b
        for h in range(N_HEADS):
            o_ref[0, h] = jnp.where(hit, tab_ref[b, h], o_ref[0, h])
        return carry

    lax.fori_loop(rng_ref[n, 0], rng_ref[n, 1] + 1, body, 0)


def bias_expand(table, idx):
    n, r, c = idx.shape
    flat = idx.reshape(n, -1)
    lo = np.where(flat >= 0, flat, NUM_BUCKETS).min(axis=1)
    rng = np.stack([lo, flat.max(axis=1)], axis=1).astype(np.int32)
    return pl.pallas_call(
        _bias_kernel,
        grid=(n,),
        in_specs=[pl.BlockSpec(memory_space=pltpu.SMEM),
                  pl.BlockSpec(memory_space=pltpu.SMEM),
                  pl.BlockSpec((1, r, c), lambda i: (i, 0, 0))],
        out_specs=pl.BlockSpec((1, N_HEADS, r, c), lambda i: (i, 0, 0, 0)),
        out_shape=jax.ShapeDtypeStruct((n, N_HEADS, r, c), F32),
        compiler_params=_cparams("parallel"),
        name="bias_expand",
    )(table, jnp.asarray(rng), jnp.asarray(idx))


def _banded_idx(n_prev, max_dist, stride):
    kb = (n_prev + 1) * QBLK
    dist = np.arange(QBLK)[:, None] + n_prev * QBLK - np.arange(kb)[None, :]
    valid = (dist >= 0) & (dist <= max_dist)
    idx = np.where(valid, _t5_bucket_np(dist * stride), -1).astype(np.int32)
    if n_prev > 1:
        return idx[None]
    first = np.where(np.arange(kb)[None, :] < n_prev * QBLK, -1, idx).astype(np.int32)
    return np.stack([idx, first])


def _norm_mm_kernel(x_ref, g_ref, w_ref, o_ref, xn_ref):
    @pl.when(pl.program_id(1) == 0)
    def _():
        xn_ref[...] = _rms(x_ref[...], g_ref[...]).astype(xn_ref.dtype)

    o_ref[...] = jnp.dot(xn_ref[...], w_ref[...], preferred_element_type=F32).astype(o_ref.dtype)


def norm_matmul(x, gain, w, out_dtype, tm, tn):
    t, d = x.shape
    n = w.shape[1]
    scratch = [pltpu.VMEM((tm, d), BF16)]
    return pl.pallas_call(
        _norm_mm_kernel,
        grid=(t // tm, n // tn),
        in_specs=[pl.BlockSpec((tm, d), lambda i, j: (i, 0)),
                  pl.BlockSpec((1, d), lambda i, j: (0, 0)),
                  pl.BlockSpec((d, tn), lambda i, j: (0, j))],
        out_specs=pl.BlockSpec((tm, tn), lambda i, j: (i, j)),
        out_shape=jax.ShapeDtypeStruct((t, n), out_dtype),
        scratch_shapes=scratch,
        compiler_params=_cparams("parallel", "arbitrary"),
        name="norm_matmul",
    )(x, gain.reshape(1, d), w)


def _band_chunk(dil, seq):
    if seq % (BAND_ROWS * dil) == 0 and BAND_ROWS * dil <= PERM_ROWS:
        return BAND_ROWS
    return QBLK


def _norm_perm_kernel(x_ref, g_ref, *refs, dils, chunks):
    o_refs, xs_ref = refs[:len(dils)], refs[len(dils)]
    xn = _rms(x_ref[...], g_ref[...])
    nc = xs_ref.shape[0]
    for c in range(nc):
        xs_ref[c] = xn[:, c * LANES:(c + 1) * LANES]
    for o_ref, dil, cl in zip(o_refs, dils, chunks):
        if dil == 1:
            o_ref[...] = xn.astype(o_ref.dtype)
            continue
        span = cl * dil
        for u in range(x_ref.shape[0] // span):
            for r in range(dil):
                rows = jnp.concatenate(
                    [xs_ref[c, pl.ds(u * span + r, cl, stride=dil), :] for c in range(nc)], axis=1)
                dst = u * span + r * cl
                o_ref[dst:dst + cl, :] = rows.astype(o_ref.dtype)


def norm_permute(x, gain, dils, chunks, tm):
    t, d = x.shape
    assert all(tm % (cl * dil) == 0 for dil, cl in zip(dils, chunks))
    return pl.pallas_call(
        functools.partial(_norm_perm_kernel, dils=tuple(dils), chunks=tuple(chunks)),
        grid=(t // tm,),
        in_specs=[pl.BlockSpec((tm, d), lambda i: (i, 0)),
                  pl.BlockSpec((1, d), lambda i: (0, 0))],
        out_specs=[pl.BlockSpec((tm, d), lambda i: (i, 0)) for _ in dils],
        out_shape=[jax.ShapeDtypeStruct((t, d), BF16) for _ in dils],
        scratch_shapes=[pltpu.VMEM((d // LANES, tm, LANES), F32)],
        compiler_params=_cparams("parallel"),
        name="norm_permute",
    )(x, gain.reshape(1, d))


def _mm_kernel(x_ref, w_ref, o_ref):
    o_ref[...] = jnp.dot(x_ref[...], w_ref[...], preferred_element_type=F32).astype(o_ref.dtype)


def matmul_resident(x, w, out_dtype, tm):
    t, k = x.shape
    n = w.shape[1]
    return pl.pallas_call(
        _mm_kernel,
        grid=(t // tm,),
        in_specs=[pl.BlockSpec((tm, k), lambda i: (i, 0)),
                  pl.BlockSpec((k, n), lambda i: (0, 0))],
        out_specs=pl.BlockSpec((tm, n), lambda i: (i, 0)),
        out_shape=jax.ShapeDtypeStruct((t, n), out_dtype),
        compiler_params=_cparams("parallel"),
        name="matmul_resident",
    )(x, w)


def _finish_proj(y, w_ref, g_ref, h_ref, o_ref):
    z = jnp.dot(y.astype(BF16), w_ref[...], preferred_element_type=F32)
    o_ref[...] = h_ref[...] + _rms(z, g_ref[...])


def _proj_kernel(y_ref, w_ref, g_ref, h_ref, o_ref):
    _finish_proj(y_ref[...], w_ref, g_ref, h_ref, o_ref)


def _unpermute(src_ref, dst_ref, dil, cl, tm, i):
    span = cl * dil
    nc = dst_ref.shape[0]
    if span <= tm:
        for u in range(tm // span):
            for r in range(dil):
                lo = u * span + r * cl
                for c in range(nc):
                    dst_ref[c, pl.ds(u * span + r, cl, stride=dil), :] = src_ref[lo:lo + cl,
                                                                                 c * LANES:(c + 1) * LANES]
    else:
        per = tm // dil
        off = (i % (span // tm)) * per
        for r in range(dil):
            lo = pl.multiple_of(r * cl + off, 8)
            for c in range(nc):
                dst_ref[c, pl.ds(r, per, stride=dil), :] = src_ref[pl.ds(lo, per), c * LANES:(c + 1) * LANES]
    return jnp.concatenate([dst_ref[c] for c in range(nc)], axis=1) if nc > 1 else dst_ref[0]


def _proj_a_kernel(*refs, dils, chunks, tm):
    n = len(dils)
    o_refs, l_refs = refs[:n], refs[n:2 * n]
    e_ref, w_ref, g_ref, h_ref, out_ref = refs[2 * n:2 * n + 5]
    scratch = refs[2 * n + 5:]
    i = pl.program_id(0)
    outs, lses = [], []
    si = 0
    for gi, dil in enumerate(dils):
        if dil == 1:
            outs.append(o_refs[gi][...])
            lses.append(l_refs[gi][...])
        else:
            outs.append(_unpermute(o_refs[gi], scratch[si], dil, chunks[gi], tm, i))
            lses.append(_unpermute(l_refs[gi], scratch[si + 1], dil, chunks[gi], tm, i))
            si += 2
    mx = functools.reduce(jnp.maximum, lses)
    es = [jnp.exp(l - mx) for l in lses]
    inv = 1.0 / functools.reduce(lambda a, b: a + b, es)
    e = e_ref[...]
    y = outs[0] * _expand_heads(es[0] * inv, e)
    for gi in range(1, n):
        y = y + outs[gi] * _expand_heads(es[gi] * inv, e)
    _finish_proj(y, w_ref, g_ref, h_ref, out_ref)


def combine_proj_a(outs, lses, dils, chunks, e, w, gain, h, tm):
    t, d = h.shape

    def row_spec(width, dil, cl):
        rows = max(tm, cl * dil)
        return pl.BlockSpec((rows, width), lambda i, q=rows // tm: (i // q, 0))

    in_specs = [row_spec(HQ, dil, cl) for dil, cl in zip(dils, chunks)] + [
        row_spec(LANES, dil, cl) for dil, cl in zip(dils, chunks)] + [
        pl.BlockSpec(e.shape, lambda i: (0, 0)),
        pl.BlockSpec(w.shape, lambda i: (0, 0)),
        pl.BlockSpec((1, d), lambda i: (0, 0)),
        pl.BlockSpec((tm, d), lambda i: (i, 0))]
    scratch = []
    for dil in dils:
        if dil > 1:
            scratch += [pltpu.VMEM((HQ // LANES, tm, LANES), F32), pltpu.VMEM((1, tm, LANES), F32)]
    return pl.pallas_call(
        functools.partial(_proj_a_kernel, dils=tuple(dils), chunks=tuple(chunks), tm=tm),
        grid=(t // tm,),
        in_specs=in_specs,
        out_specs=pl.BlockSpec((tm, d), lambda i: (i, 0)),
        out_shape=jax.ShapeDtypeStruct((t, d), F32),
        scratch_shapes=scratch,
        compiler_params=_cparams("arbitrary"),
        name="dilated_combine_proj",
    )(*outs, *lses, e, w, gain.reshape(1, d), h)


def _proj_c_kernel(oc_ref, os_ref, ow_ref, gr_ref, e_ref, w_ref, g_ref, h_ref, o_ref):
    sig = jax.nn.sigmoid(gr_ref[...])
    e = e_ref[...]
    gate = lambda i: _expand_heads(sig if i == 0 else pltpu.roll(sig, LANES - i * N_HEADS, 1), e)
    y = gate(0) * oc_ref[...]
    y = y + gate(1) * os_ref[...]
    y = y + gate(2) * ow_ref[...]
    _finish_proj(y, w_ref, g_ref, h_ref, o_ref)


def _proj_call(kernel, name, tm, row_inputs, const_inputs, w, gain, h):
    t, d = h.shape
    k = w.shape[0]
    row_specs = [pl.BlockSpec((tm, a.shape[1]), lambda i: (i, 0)) for a in row_inputs]
    const_specs = [pl.BlockSpec(a.shape, (lambda nd: (lambda i: (0,) * nd))(a.ndim)) for a in const_inputs]
    return pl.pallas_call(
        kernel,
        grid=(t // tm,),
        in_specs=row_specs + const_specs + [
            pl.BlockSpec((k, d), lambda i: (0, 0)),
            pl.BlockSpec((1, d), lambda i: (0, 0)),
            pl.BlockSpec((tm, d), lambda i: (i, 0))],
        out_specs=pl.BlockSpec((tm, d), lambda i: (i, 0)),
        out_shape=jax.ShapeDtypeStruct((t, d), F32),
        compiler_params=_cparams("parallel"),
        name=name,
    )(*row_inputs, *const_inputs, w, gain.reshape(1, d), h)


def _head_expand_matrix():
    e = np.zeros((LANES, HQ), np.float32)
    for term in range(3):
        for h in range(N_HEADS):
            e[term * N_HEADS + h, h * HEAD_DIM:(h + 1) * HEAD_DIM] = 1.0
    return e


def _banded_kernel(*refs, tq, n_prev, n_chunks, stack, paired, with_sinks, with_lse):
    q_ref, kp_ref, kc_ref, vp_ref, vc_ref, bias_ref = refs[:6]
    pos = 6
    sink_ref = None
    if with_sinks:
        sink_ref = refs[pos]
        pos += 1
    o_ref = refs[pos]
    pos += 1
    lse_ref = None
    if with_lse:
        lse_ref = refs[pos]
        pos += 1
    kbuf, vbuf = refs[pos], refs[pos + 1]

    i = pl.program_id(1)
    hpc = N_HEADS // n_chunks
    kb = (n_prev + 1) * QBLK
    if paired:
        for s in range(tq // QBLK):
            blk = slice(s * QBLK, (s + 1) * QBLK)
            for buf, p_ref, c_ref in ((kbuf, kp_ref, kc_ref), (vbuf, vp_ref, vc_ref)):
                buf[2 * s * QBLK:(2 * s + 1) * QBLK] = p_ref[0, blk]
                buf[(2 * s + 1) * QBLK:(2 * s + 2) * QBLK] = c_ref[0, blk]
    else:
        kbuf[0:tq] = kp_ref[0]
        kbuf[tq:2 * tq] = kc_ref[0]
        vbuf[0:tq] = vp_ref[0]
        vbuf[tq:2 * tq] = vc_ref[0]
    lane = lax.broadcasted_iota(jnp.int32, (QBLK, LANES), 1)
    head_row = lax.broadcasted_iota(jnp.int32, (stack * QBLK, 1), 0)

    for s in range(tq // QBLK):
        r0 = s * QBLK
        k0 = 2 * r0 if paired else tq + r0 - n_prev * QBLK
        variants = bias_ref.shape[0] > 1
        if variants:
            first = jnp.where(i == 0, 1, 0) if (s == 0 or paired) else 0
        else:
            col = lax.broadcasted_iota(jnp.int32, (1, kb), 1) + k0
            kmask = jnp.where(jnp.logical_and(i == 0, col < tq), NEG_INF, 0.0).astype(F32)
        lse_acc = jnp.zeros((QBLK, LANES), F32)
        for h0 in range(0, N_HEADS, stack):
            heads = list(range(h0, h0 + stack))
            c = h0 // hpc
            qst = _stack_heads(lambda p: q_ref[0, r0:r0 + QBLK, p * LANES:(p + 1) * LANES], heads)
            kx = kbuf[k0:k0 + kb, c * LANES:(c + 1) * LANES]
            vx = vbuf[k0:k0 + kb, c * LANES:(c + 1) * LANES]
            sc = _dot_nt(qst, kx)
            if variants:
                sc = sc + bias_ref[first, h0:h0 + stack].reshape(stack * QBLK, kb)
            else:
                sc = sc + bias_ref[0, h0:h0 + stack].reshape(stack * QBLK, kb) + kmask
            m = jnp.max(sc, axis=-1, keepdims=True)
            if with_sinks:
                sk = sink_ref[h0 + stack - 1]
                for t in range(stack - 2, -1, -1):
                    sk = jnp.where(head_row < (t + 1) * QBLK, sink_ref[h0 + t], sk)
                m = jnp.maximum(m, sk)
            p = jnp.exp(sc - m)
            den = jnp.sum(p, axis=-1, keepdims=True)
            norm = den + jnp.exp(sk - m) if with_sinks else den
            o = jnp.dot(p.astype(BF16), vx, preferred_element_type=F32) * (1.0 / norm)
            for t2, blk in enumerate(_merge_pairs(o, stack)):
                pidx = h0 // 2 + t2
                o_ref[0, r0:r0 + QBLK, pidx * LANES:(pidx + 1) * LANES] = blk.astype(o_ref.dtype)
            if with_lse:
                lse = m + jnp.log(den)
                for t, h in enumerate(heads):
                    lse_acc = jnp.where(lane == h, lse[t * QBLK:(t + 1) * QBLK], lse_acc)
        if with_lse:
            lse_ref[0, r0:r0 + QBLK, :] = lse_acc


def banded_attention(q_arr, k_arr, v_arr, bias, *, n_rows, n_tiles, tq, n_prev, kw,
                     q_map, k_map, v_map, o_map, out_shape, lse_shape=None, sinks=None, stack=2,
                     paired=False, name):
    n_chunks = kw // LANES
    assert stack % 2 == 0 and (N_HEADS // n_chunks) % stack == 0 and (not paired or n_prev == 1)
    kb = (n_prev + 1) * QBLK
    with_sinks = sinks is not None
    with_lse = lse_shape is not None

    def prev(fn):
        return lambda n, i: fn(n, jnp.maximum(i - 1, 0))

    in_specs = [pl.BlockSpec((1, tq, HQ), q_map),
                pl.BlockSpec((1, tq, kw), prev(k_map)),
                pl.BlockSpec((1, tq, kw), k_map),
                pl.BlockSpec((1, tq, kw), prev(v_map)),
                pl.BlockSpec((1, tq, kw), v_map),
                pl.BlockSpec(bias.shape, lambda n, i: (0, 0, 0, 0))]
    assert bias.shape[1:] == (N_HEADS, QBLK, kb) and (bias.shape[0] == 1 or n_prev == 1)
    args = [q_arr, k_arr, k_arr, v_arr, v_arr, bias]
    if with_sinks:
        in_specs.append(pl.BlockSpec(memory_space=pltpu.SMEM))
        args.append(sinks)
    out_specs = [pl.BlockSpec((1, tq, HQ), o_map)]
    out_shapes = [jax.ShapeDtypeStruct(out_shape, F32)]
    if with_lse:
        out_specs.append(pl.BlockSpec((1, tq, LANES), o_map))
        out_shapes.append(jax.ShapeDtypeStruct(lse_shape, F32))
    res = pl.pallas_call(
        functools.partial(_banded_kernel, tq=tq, n_prev=n_prev, n_chunks=n_chunks, stack=stack,
                          paired=paired, with_sinks=with_sinks, with_lse=with_lse),
        grid=(n_rows, n_tiles),
        in_specs=in_specs,
        out_specs=out_specs,
        out_shape=out_shapes,
        scratch_shapes=[pltpu.VMEM((2 * tq, kw), BF16), pltpu.VMEM((2 * tq, kw), BF16)],
        compiler_params=_cparams("parallel", "arbitrary"),
        name=name,
    )(*args)
    return res


def _dup_groups(x):
    g0, g1 = x[..., :HEAD_DIM], x[..., HEAD_DIM:]
    return jnp.concatenate([g0, g0, g1, g1], axis=-1)


def _compress_kernel(ch_ref, pos_ref, w1_ref, w2_ref, o_ref):
    ch = ch_ref[0, 0, 0]
    rows = ch.shape[0]
    posv = pos_ref[0]
    a = jnp.dot((ch + posv[0:1]).astype(BF16), w1_ref[0, 0], preferred_element_type=F32)
    b = jnp.dot((ch + posv[1:2]).astype(BF16), w1_ref[0, 1], preferred_element_type=F32)
    hid = a + pltpu.roll(b, rows - 1, 0)
    act = jax.nn.gelu(hid, approximate=True)
    o_ref[0, 0, 0] = jnp.dot(act.astype(BF16), w2_ref[0], preferred_element_type=F32)


def compress(chunks, pos, w1, w2):
    _, b, g, rows, width = chunks.shape
    hid = w1.shape[-1]
    return pl.pallas_call(
        _compress_kernel,
        grid=(2, b, g),
        in_specs=[pl.BlockSpec((1, 1, 1, rows, width), lambda i, bb, gg: (i, bb, gg, 0, 0)),
                  pl.BlockSpec((1, 2, width), lambda i, bb, gg: (i, 0, 0)),
                  pl.BlockSpec((1, 2, width, hid), lambda i, bb, gg: (i, 0, 0, 0)),
                  pl.BlockSpec((1, hid, HEAD_DIM), lambda i, bb, gg: (i, 0, 0))],
        out_specs=pl.BlockSpec((1, 1, 1, rows, HEAD_DIM), lambda i, bb, gg: (i, bb, gg, 0, 0)),
        out_shape=jax.ShapeDtypeStruct((2, b, g, rows, HEAD_DIM), F32),
        compiler_params=_cparams("parallel", "parallel", "parallel"),
        name="nsa_compress",
    )(chunks, pos, w1, w2)


def _cmp_attn_kernel(q_ref, kc_ref, vc_ref, ov_ref, place_ref, o_ref, sel_ref, *, n_sel_blocks, k_sel):
    qi = pl.program_id(1)
    ncr = kc_ref.shape[1]
    hpc = N_HEADS // C_KV_HEADS
    qpos = qi * QBLK + lax.broadcasted_iota(jnp.int32, (QBLK, 1), 0)
    cidx = lax.broadcasted_iota(jnp.int32, (1, ncr), 1)
    valid = (cidx * CMP_STRIDE + (CMP_BLOCK - 1)) <= qpos
    maskc = jnp.where(valid, 0.0, NEG_INF).astype(F32)
    anyv = (qpos >= CMP_BLOCK - 1).astype(F32)
    blk_id = lax.broadcasted_iota(jnp.int32, (HEAD_DIM, QBLK), 0)
    cur = (qi * QBLK + lax.broadcasted_iota(jnp.int32, (HEAD_DIM, QBLK), 1)) // SEL_BLOCK
    forced = jnp.logical_or(jnp.logical_or(blk_id == 0, blk_id == cur), blk_id == cur - 1)
    allowed = blk_id <= cur
    for g in range(C_KV_HEADS):
        heads = list(range(g * hpc, (g + 1) * hpc))
        qst = _stack_heads(lambda p: q_ref[0, :, p * LANES:(p + 1) * LANES], heads)
        sc = _dot_nt(qst, kc_ref[0, :, g * LANES:(g + 1) * LANES])
        sc3 = sc.reshape(hpc, QBLK, ncr) + maskc[None]
        m = jnp.max(sc3, axis=-1, keepdims=True)
        e = jnp.exp(sc3 - m)
        ssum = jnp.sum(e, axis=-1, keepdims=True)
        p = (e * (1.0 / ssum)) * anyv[None]
        o = jnp.dot(p.reshape(hpc * QBLK, ncr).astype(BF16), vc_ref[0, :, g * LANES:(g + 1) * LANES],
                    preferred_element_type=F32)
        for t2, blk in enumerate(_merge_pairs(o, hpc)):
            pidx = (g * hpc) // 2 + t2
            o_ref[0, :, pidx * LANES:(pidx + 1) * LANES] = blk
        hi, mid, lo = _split3(jnp.sum(p, axis=0))
        ovt = ov_ref[...]
        imp = (_dot_nt(ovt, hi) + _dot_nt(ovt, mid)) + _dot_nt(ovt, lo)
        score = jnp.where(forced, FORCE_SCORE, jnp.where(allowed, imp, NEG_INF))
        bits = pltpu.bitcast(score, jnp.int32)
        key = jnp.where(bits < 0, bits ^ jnp.int32(0x7FFFFFFF), bits)
        key_m1 = key - 1
        rank = jnp.zeros((HEAD_DIM, QBLK), jnp.int32)
        for i in range(n_sel_blocks):
            thr = jnp.where(blk_id > i, key_m1, key)
            rank = rank + jnp.where(key[i:i + 1, :] > thr, 1, 0)
        keep = jnp.logical_and(rank < k_sel, blk_id < n_sel_blocks)
        pen_t = jnp.where(keep, 0.0, -1.0).astype(BF16)
        pen = lax.dot_general(pen_t, place_ref[...], (((0,), (0,)), ((), ())), preferred_element_type=F32)
        sel_ref[0, :, g * LANES:(g + 1) * LANES] = pen.astype(sel_ref.dtype)


def cmp_attention(q, kcd, vcd, ov, n_sel_blocks, k_sel):
    b, s, _ = q.shape
    ncr = kcd.shape[1]
    place = np.zeros((HEAD_DIM, LANES), np.float32)
    place[np.arange(HEAD_DIM), HEAD_DIM + np.arange(HEAD_DIM)] = 1.0
    return pl.pallas_call(
        functools.partial(_cmp_attn_kernel, n_sel_blocks=n_sel_blocks, k_sel=k_sel),
        grid=(b, s // QBLK),
        in_specs=[pl.BlockSpec((1, QBLK, HQ), lambda bb, i: (bb, i, 0)),
                  pl.BlockSpec((1, ncr, 2 * LANES), lambda bb, i: (bb, 0, 0)),
                  pl.BlockSpec((1, ncr, 2 * LANES), lambda bb, i: (bb, 0, 0)),
                  pl.BlockSpec((HEAD_DIM, ncr), lambda bb, i: (0, 0)),
                  pl.BlockSpec((HEAD_DIM, LANES), lambda bb, i: (0, 0))],
        out_specs=[pl.BlockSpec((1, QBLK, HQ), lambda bb, i: (bb, i, 0)),
                   pl.BlockSpec((1, QBLK, 2 * LANES), lambda bb, i: (bb, i, 0))],
        out_shape=[jax.ShapeDtypeStruct((b, s, HQ), F32),
                   jax.ShapeDtypeStruct((b, s, 2 * LANES), BF16)],
        compiler_params=_cparams("parallel", "parallel"),
        name="nsa_cmp_attention",
    )(q, kcd, vcd, ov, jnp.asarray(place, BF16))


def _sel_attn_kernel(q_ref, pen_ref, k_ref, v_ref, bias_ref, o_ref, qst_ref, m_ref, acc_ref,
                     sa_ref, sb_ref, *, nbt):
    qi0 = pl.program_id(2) * SEL_QTILES
    hpc = N_HEADS // C_KV_HEADS
    nsub = SEL_CHUNK // QBLK
    rpt = hpc * QBLK
    lane = lax.broadcasted_iota(jnp.int32, (QBLK, LANES), 1)
    lo = lane < HEAD_DIM
    for w in range(SEL_QTILES):
        pen = pen_ref[0, w * QBLK:(w + 1) * QBLK, :].astype(F32)
        for t in range(hpc):
            q2 = q_ref[0, w * QBLK:(w + 1) * QBLK, (t // 2) * LANES:(t // 2 + 1) * LANES].astype(F32)
            if t % 2:
                q2 = pltpu.roll(q2, HEAD_DIM, 1)
            qst_ref[w * rpt + t * QBLK:w * rpt + (t + 1) * QBLK, :] = jnp.where(lo, q2, pen).astype(BF16)
    m_ref[...] = jnp.full(m_ref.shape, NEG_INF, F32)
    acc_ref[...] = jnp.zeros(acc_ref.shape, F32)

    n_chunks = (qi0 + SEL_QTILES - 1) // nsub + 1

    def scores(c, s_ref):
        k0 = pl.multiple_of(c * SEL_CHUNK, SEL_CHUNK)
        s_ref[...] = _dot_nt(qst_ref[...], k_ref[0, pl.ds(k0, SEL_CHUNK), :])

    def accumulate(c, s_ref):
        k0 = pl.multiple_of(c * SEL_CHUNK, SEL_CHUNK)
        ps, alphas = [], []
        for w in range(SEL_QTILES):
            rows = slice(w * rpt, (w + 1) * rpt)
            tiles = []
            for u in range(nsub):
                off = qi0 + w - nsub * c - u
                idx = jnp.where(off < 0, nbt, jnp.minimum(off, nbt - 1))
                tiles.append(s_ref[rows, u * QBLK:(u + 1) * QBLK] + bias_ref[idx].reshape(rpt, QBLK))
            m_old = m_ref[rows, :]
            m_new = jnp.maximum(m_old, jnp.max(functools.reduce(jnp.maximum, tiles), axis=-1, keepdims=True))
            ps.append(jnp.concatenate([jnp.exp(tl - m_new).astype(BF16) for tl in tiles], axis=1))
            alphas.append(jnp.exp(m_old - m_new))
            m_ref[rows, :] = m_new
        pv = jnp.dot(jnp.concatenate(ps, axis=0), v_ref[0, pl.ds(k0, SEL_CHUNK), :], preferred_element_type=F32)
        acc_ref[...] = jnp.concatenate(alphas, axis=0) * acc_ref[...] + pv

    scores(0, sa_ref)

    def body(cc, carry):
        c = 2 * cc
        scores(c + 1, sb_ref)
        accumulate(c, sa_ref)
        scores(jnp.minimum(c + 2, n_chunks - 1), sa_ref)
        accumulate(c + 1, sb_ref)
        return carry

    lax.fori_loop(0, n_chunks // 2, body, 0)

    @pl.when(n_chunks % 2 == 1)
    def _():
        accumulate(n_chunks - 1, sa_ref)

    acc = acc_ref[...]
    rolled = pltpu.roll(acc, HEAD_DIM, 1)
    for w in range(SEL_QTILES):
        for t2 in range(hpc // 2):
            ev = slice(w * rpt + 2 * t2 * QBLK, w * rpt + (2 * t2 + 1) * QBLK)
            od = slice(w * rpt + (2 * t2 + 1) * QBLK, w * rpt + (2 * t2 + 2) * QBLK)
            even = acc[ev] * (1.0 / rolled[ev])
            odd = rolled[od] * (1.0 / acc[od])
            o_ref[0, w * QBLK:(w + 1) * QBLK, t2 * LANES:(t2 + 1) * LANES] = jnp.where(lo, even, odd)


def sel_attention(q, kx, vx, pen, bias_tiles):
    b, s, _ = q.shape
    hpc = N_HEADS // C_KV_HEADS
    gw = hpc * HEAD_DIM
    nbt = bias_tiles.shape[0] - 1
    tq = SEL_QTILES * QBLK
    rows = SEL_QTILES * hpc * QBLK
    assert s % SEL_CHUNK == 0 and s % tq == 0
    return pl.pallas_call(
        functools.partial(_sel_attn_kernel, nbt=nbt),
        grid=(b, C_KV_HEADS, s // tq),
        in_specs=[pl.BlockSpec((1, tq, gw), lambda bb, g, i: (bb, i, g)),
                  pl.BlockSpec((1, tq, LANES), lambda bb, g, i: (bb, i, g)),
                  pl.BlockSpec((1, s, LANES), lambda bb, g, i: (bb, 0, g)),
                  pl.BlockSpec((1, s, LANES), lambda bb, g, i: (bb, 0, g)),
                  pl.BlockSpec((nbt + 1, hpc, QBLK, QBLK), lambda bb, g, i: (0, g, 0, 0))],
        out_specs=pl.BlockSpec((1, tq, gw), lambda bb, g, i: (bb, i, g)),
        out_shape=jax.ShapeDtypeStruct((b, s, HQ), F32),
        scratch_shapes=[pltpu.VMEM((rows, LANES), BF16),
                        pltpu.VMEM((rows, LANES), F32),
                        pltpu.VMEM((rows, LANES), F32),
                        pltpu.VMEM((rows, SEL_CHUNK), F32),
                        pltpu.VMEM((rows, SEL_CHUNK), F32)],
        compiler_params=_cparams("parallel", "parallel", "arbitrary"),
        name="nsa_sel_attention",
    )(q, pen, kx, vx, bias_tiles)


def _sel_bias_idx(s):
    nqt = s // QBLK
    far = -(-(int(np.argmax(_t5_bucket_np(np.arange(4 * MAX_DISTANCE)) == NUM_BUCKETS - 1)) + QBLK) // QBLK)
    nbt = min(nqt, far + 1)
    d0 = np.arange(nbt)[:, None, None] * QBLK
    dist = d0 + np.arange(QBLK)[None, :, None] - np.arange(QBLK)[None, None, :]
    idx = np.where(dist >= 0, _t5_bucket_np(dist), -1).astype(np.int32)
    return np.concatenate([idx, np.full((1, QBLK, QBLK), -1, np.int32)])


def _ffn_kernel(h_ref, g2_ref, wu_ref, cw_ref, cb_ref, wd_ref, g3_ref,
                o_ref, xn_ref, acc_ref, u_ref, a_ref, carry_ref, *, tiles_per_seq, nj):
    i = pl.program_id(0)
    ns, ts, tn = xn_ref.shape[0], xn_ref.shape[1], a_ref.shape[3]
    cols = lambda c: slice(c * tn, (c + 1) * tn)

    @pl.when(i % tiles_per_seq == 0)
    def _():
        carry_ref[...] = jnp.zeros(carry_ref.shape, F32)

    top = 16
    rowt = lax.broadcasted_iota(jnp.int32, (top, tn), 0)

    def up(s, j):
        xn = xn_ref[s]
        u_ref[s, j % 2, 0] = jnp.dot(xn, wu_ref[:, cols(j)], preferred_element_type=F32)
        u_ref[s, j % 2, 1] = jnp.dot(xn, wu_ref[:, cols(nj + j)], preferred_element_type=F32)

    def conv(u, s1, s2, c):
        cw = cw_ref[:, cols(c)]
        return ((cb_ref[:, cols(c)] + u * cw[2:3]) + s2 * cw[0:1]) + s1 * cw[1:2]

    def conv_body(u, c):
        return conv(u, pltpu.roll(u, 1, 0), pltpu.roll(u, 2, 0), c)

    def conv_top(uc_ref, c):
        prev = carry_ref[c]
        p1, p2 = prev[7:8], prev[6:7]
        u = uc_ref[0:top]
        s1 = jnp.where(rowt == 0, p1, pltpu.roll(u, 1, 0))
        s2 = jnp.where(rowt == 0, p2, jnp.where(rowt == 1, p1, pltpu.roll(u, 2, 0)))
        carry_ref[c] = uc_ref[ts - 8:ts]
        return conv(u, s1, s2, c)

    def gated(cg, cv):
        return (jax.nn.gelu(cg, approximate=True) * cv).astype(BF16)

    def act(s, j):
        ur, ar = u_ref.at[s, j % 2], a_ref.at[s, j % 2]
        ar[...] = gated(conv_body(ur[0], j), conv_body(ur[1], nj + j))
        ar[0:top] = gated(conv_top(ur.at[0], j), conv_top(ur.at[1], nj + j))

    def down(s, j):
        acc_ref[s] += jnp.dot(a_ref[s, j % 2], wd_ref[j * tn:(j + 1) * tn, :], preferred_element_type=F32)

    def iteration(s, k):
        rows = slice(s * ts, (s + 1) * ts)
        if k == -1:
            xn_ref[s] = _rms(h_ref[rows, :], g2_ref[...]).astype(xn_ref.dtype)
            acc_ref[s] = jnp.zeros(acc_ref.shape[1:], F32)
        if 0 <= k + 1 < nj:
            up(s, k + 1)
        if 0 <= k < nj:
            act(s, k)
        if 0 <= k - 1 < nj:
            down(s, k - 1)
        if k == nj:
            o_ref[rows, :] = h_ref[rows, :] + _rms(acc_ref[s], g3_ref[...])

    for k in range(-1, nj + 1):
        for s in range(ns):
            iteration(s, k)


def conv_ffn(h, g2, w_up, conv_w, conv_b, w_down, g3, seq, tm, tn, ns=FFN_SUBTILES):
    t, d = h.shape
    dff = w_down.shape[0]
    nj = dff // tn
    ts = tm // ns
    assert nj * tn == dff and ts * ns == tm and ts % 16 == 0
    wu, cw, cb, wd = w_up, conv_w, conv_b.reshape(1, -1), w_down
    resident = lambda a: pl.BlockSpec(a.shape, (lambda nd: (lambda i: (0,) * nd))(a.ndim))
    return pl.pallas_call(
        functools.partial(_ffn_kernel, tiles_per_seq=seq // tm, nj=nj),
        grid=(t // tm,),
        in_specs=[pl.BlockSpec((tm, d), lambda i: (i, 0)),
                  pl.BlockSpec((1, d), lambda i: (0, 0)),
                  resident(wu), resident(cw), resident(cb), resident(wd),
                  pl.BlockSpec((1, d), lambda i: (0, 0))],
        out_specs=pl.BlockSpec((tm, d), lambda i: (i, 0)),
        out_shape=jax.ShapeDtypeStruct((t, d), F32),
        scratch_shapes=[pltpu.VMEM((ns, ts, d), BF16), pltpu.VMEM((ns, ts, d), F32),
                        pltpu.VMEM((ns, 2, 2, ts, tn), F32), pltpu.VMEM((ns, 2, ts, tn), BF16),
                        pltpu.VMEM((2 * nj, 8, tn), F32)],
        compiler_params=_cparams("arbitrary"),
        name="conv_ffn",
    )(h, g2.reshape(1, d), wu, cw, cb, wd, g3.reshape(1, d))


def _row_tile(t):
    for tm in (1024, 512, 256, 128):
        if t % tm == 0:
            return tm
    raise ValueError(f"token count {t} is not a multiple of 128")


def mixer_a(h, gains, w_in, w_o, rel_table, bsz, seq):
    t, d = h.shape
    tm = _row_tile(t)
    n_dil = len(DIL_CONFIGS)
    a_in = w_in.shape[1]
    col_scale = np.ones((a_in,), np.float32).reshape(n_dil, 3, HQ)
    col_scale[:, 0] = ATTN_SCALE
    w = (w_in * col_scale.reshape(1, a_in)).astype(BF16)
    idx = np.concatenate([_banded_idx(1, window // dil, dil) for window, dil in DIL_CONFIGS])
    bias = bias_expand(rel_table, idx)
    outs, lses = [], []
    dils = [dil for _, dil in DIL_CONFIGS]
    chunks = [_band_chunk(dil, seq) for dil in dils]
    xns = norm_permute(h, gains[0], dils, chunks, max([tm] + [cl * dil for dil, cl in zip(dils, chunks)]))
    for gi, (window, dil) in enumerate(DIL_CONFIGS):
        cl = chunks[gi]
        pair = BAND_ROWS // cl if dil % (BAND_ROWS // cl) == 0 else 1
        tq, rows = cl * pair, dil // pair
        assert window // dil <= QBLK and seq % (cl * dil) == 0
        qkv = matmul_resident(xns[gi], w[:, gi * 3 * HQ:(gi + 1) * 3 * HQ], BF16, tm)
        qkv = qkv.reshape(1, t, 3 * HQ)

        def rmap(part, rows=rows, per_seq=seq // tq):
            return lambda n, i: (0, (n // rows) * per_seq + i * rows + n % rows, part)

        o, lse = banded_attention(
            qkv, qkv, qkv, bias[2 * gi:2 * gi + 2], n_rows=bsz * rows, n_tiles=seq // (cl * dil), tq=tq,
            n_prev=1, kw=HQ, q_map=rmap(0), k_map=rmap(1), v_map=rmap(2), o_map=rmap(0), paired=pair > 1,
            out_shape=(1, t, HQ), lse_shape=(1, t, LANES), name=f"dilated_attention_{dil}")
        outs.append(o.reshape(t, HQ))
        lses.append(lse.reshape(t, LANES))
    e = jnp.asarray(_head_expand_matrix(), BF16)
    return combine_proj_a(outs, lses, dils, chunks, e, w_o.astype(BF16), gains[1], h, min(tm, PROJ_ROWS))


def mixer_b(h, gains, w_in, sinks, w_o, rel_table, bsz, seq):
    t, d = h.shape
    tm = _row_tile(t)
    hk = B_KV_HEADS * HEAD_DIM
    n_in = w_in.shape[1]
    assert n_in == HQ + 2 * hk
    w = jnp.concatenate([w_in[:, :HQ] * ATTN_SCALE, _dup_groups(w_in[:, HQ:HQ + hk]),
                         _dup_groups(w_in[:, HQ + hk:])], axis=1).astype(BF16)
    kw = 2 * hk
    qkv = norm_matmul(h, gains[0], w, BF16, tm, HQ + 2 * kw).reshape(bsz, seq, HQ + 2 * kw)
    bias = bias_expand(rel_table, _banded_idx(1, B_WINDOW - 1, 1))
    sink_rows = sinks.astype(F32)
    ident = lambda n, i: (n, i, 0)
    col = lambda c: (lambda n, i: (n, i, c))
    (o,) = banded_attention(qkv, qkv, qkv, bias, n_rows=bsz, n_tiles=seq // BAND_ROWS, tq=BAND_ROWS, n_prev=1,
                            kw=kw, q_map=ident, k_map=col(HQ // kw), v_map=col(HQ // kw + 1), o_map=ident,
                            out_shape=(bsz, seq, HQ), sinks=sink_rows, stack=8, name="sink_window_attention")
    return _proj_call(_proj_kernel, "sink_proj", tm, [o.reshape(t, HQ)], [], w_o.astype(BF16), gains[1], h)


def mixer_c(h, gains, w_in, cmp_pos, cmp_w1, cmp_w2, w_o, rel_table, bsz, seq):
    t, d = h.shape
    tm = _row_tile(t)
    g = C_KV_HEADS
    hk = g * HEAD_DIM
    wkv = [w_in[:, HQ + i * hk:HQ + (i + 1) * hk] for i in range(6)]
    n_gate = w_in.shape[1] - HQ - 6 * hk
    assert n_gate == 3 * N_HEADS and hk == LANES
    wq = (w_in[:, :HQ] * ATTN_SCALE).astype(BF16)
    spread = lambda w: jnp.concatenate([w[:, :HEAD_DIM], jnp.zeros_like(w[:, :HEAD_DIM]),
                                        w[:, HEAD_DIM:], jnp.zeros_like(w[:, :HEAD_DIM])], axis=1)
    wb = jnp.concatenate([spread(wkv[2]), spread(wkv[3]), _dup_groups(wkv[4]), _dup_groups(wkv[5])],
                         axis=1).astype(BF16)
    wf = jnp.concatenate([wkv[0], wkv[1], jnp.pad(w_in[:, HQ + 6 * hk:], ((0, 0), (0, LANES - n_gate)))],
                         axis=1).astype(BF16)
    kw = 2 * LANES
    q = norm_matmul(h, gains[0], jnp.concatenate([wq, wb], axis=1), BF16, tm, HQ + 4 * kw)
    q = q.reshape(bsz, seq, HQ + 4 * kw)
    kvb = q[..., HQ:HQ + 2 * kw]
    r = norm_matmul(h, gains[0], wf, F32, tm, 3 * LANES)
    kv = [r[:, i * hk:(i + 1) * hk].reshape(bsz, seq, hk) for i in range(2)]
    gates_raw = r[:, 2 * hk:2 * hk + LANES]

    ncr = seq // CMP_STRIDE
    half = CMP_BLOCK // 2
    assert half == CMP_STRIDE
    chunks = jnp.stack([kv[0], kv[1]]).reshape(2, bsz, ncr, half, g, HEAD_DIM)
    chunks = chunks.transpose(0, 1, 4, 2, 3, 5).reshape(2, bsz, g, ncr, half * HEAD_DIM)
    pos = cmp_pos.reshape(2, 2, half * HEAD_DIM)
    w1 = cmp_w1.reshape(2, 2, half * HEAD_DIM, -1).astype(BF16)
    cmp = compress(chunks, pos, w1, cmp_w2.astype(BF16))
    cmp = cmp.transpose(0, 1, 3, 2, 4).reshape(2, bsz, ncr, hk).astype(BF16)
    kcd, vcd = _dup_groups(cmp[0]), _dup_groups(cmp[1])

    ns = seq // SEL_BLOCK
    assert ns <= HEAD_DIM
    k_sel = min(N_SELECT, ns)
    starts = np.arange(ncr) * CMP_STRIDE
    blk = np.arange(HEAD_DIM)
    ov = ((starts[None, :] < (blk[:, None] + 1) * SEL_BLOCK)
          & (starts[None, :] + CMP_BLOCK > blk[:, None] * SEL_BLOCK)
          & (blk[:, None] < ns) & (starts[None, :] + CMP_BLOCK <= seq))
    o_c, pen = cmp_attention(q, kcd, vcd, jnp.asarray(ov.astype(np.float32), BF16), ns, k_sel)

    sel_bias = bias_expand(rel_table, _sel_bias_idx(seq))
    lane = np.arange(kw)[None, :]
    key_blk = np.arange(seq)[:, None] // SEL_BLOCK
    onehot = np.where((lane % LANES >= HEAD_DIM) & (lane % HEAD_DIM == key_blk), -NEG_INF, 0.0)
    ones = np.broadcast_to(np.where(lane % LANES >= HEAD_DIM, 1.0, 0.0), (seq, kw))
    kx = kvb[..., :kw] + jnp.asarray(onehot, BF16)[None]
    vx = kvb[..., kw:2 * kw] + jnp.asarray(ones, BF16)[None]
    o_s = sel_attention(q, kx, vx, pen, sel_bias)

    n_prev = -(-(C_WINDOW - 1) // QBLK)
    tqw = n_prev * QBLK
    wbias = bias_expand(rel_table, _banded_idx(n_prev, C_WINDOW - 1, 1))
    ident = lambda n, i: (n, i, 0)
    col = lambda c: (lambda n, i: (n, i, c))
    (o_w,) = banded_attention(q, q, q, wbias,
                              n_rows=bsz, n_tiles=seq // tqw, tq=tqw, n_prev=n_prev, kw=kw,
                              q_map=ident, k_map=col(HQ // kw + 2), v_map=col(HQ // kw + 3), o_map=ident,
                              out_shape=(bsz, seq, HQ), stack=8, name="nsa_window_attention")

    e = jnp.asarray(_head_expand_matrix(), BF16)
    return _proj_call(_proj_c_kernel, "nsa_gate_proj", min(tm, PROJ_ROWS),
                      [o_c.reshape(t, HQ), o_s.reshape(t, HQ), o_w.reshape(t, HQ), gates_raw], [e],
                      w_o.astype(BF16), gains[1], h)


def kernel(x, rel_table, norm_gains, a_w_in, a_w_o, b_w_in, b_sinks, b_w_o, c_w_in, c_cmp_pos, c_cmp_w1,
           c_cmp_w2, c_w_o, ffn_w_up, ffn_conv_w, ffn_conv_b, ffn_w_down):
    bsz, seq, d = x.shape
    depth = norm_gains.shape[0]
    h = x.reshape(bsz * seq, d)
    tm = _row_tile(seq)
    for i in range(depth):
        kind, j = i % 3, i // 3
        g = norm_gains[i]
        if kind == 0:
            h = mixer_a(h, g, a_w_in[j], a_w_o[j], rel_table, bsz, seq)
        elif kind == 1:
            h = mixer_b(h, g, b_w_in[j], b_sinks[j], b_w_o[j], rel_table, bsz, seq)
        else:
            h = mixer_c(h, g, c_w_in[j], c_cmp_pos[j], c_cmp_w1[j], c_cmp_w2[j], c_w_o[j], rel_table, bsz, seq)
        h = conv_ffn(h, g[2], ffn_w_up[i].astype(BF16), ffn_conv_w[i], ffn_conv_b[i],
                     ffn_w_down[i].astype(BF16), g[3], seq, min(tm, FFN_ROWS), FFN_COLS)
    return h.reshape(bsz, seq, d)
```

```python
import functools
import math

import numpy as np
import jax
import jax.numpy as jnp
from jax import lax
from jax.experimental import pallas as pl
from jax.experimental.pallas import tpu as pltpu

F32 = jnp.float32
BF16 = jnp.bfloat16

N_HEADS = 16
HEAD_DIM = 64
HQ = N_HEADS * HEAD_DIM
LANES = 128
ATTN_SCALE = HEAD_DIM ** -0.5
NUM_BUCKETS = 32
MAX_DISTANCE = 2048
RMS_EPS = 1e-6
NEG_INF = -1e30
FORCE_SCORE = 1e9
DIL_CONFIGS = ((128, 1), (512, 4), (2048, 16))
B_KV_HEADS = 2
B_WINDOW = 128
C_KV_HEADS = 2
CMP_BLOCK = 32
CMP_STRIDE = 16
SEL_BLOCK = 64
N_SELECT = 16
C_WINDOW = 512
CONV_WIDTH = 3
QBLK = 128
BAND_ROWS = 256
PROJ_ROWS = 512
PERM_ROWS = 2048
SEL_CHUNK = 512
SEL_QTILES = 2
FFN_COLS = 256
FFN_ROWS = 512
FFN_SUBTILES = 2
VMEM_LIMIT = 56 * 1024 * 1024


def _cparams(*sem):
    return pltpu.CompilerParams(dimension_semantics=sem, vmem_limit_bytes=VMEM_LIMIT)


def _t5_bucket_np(dist):
    max_exact = NUM_BUCKETS // 2
    d = np.maximum(dist, 0)
    df = np.maximum(d, 1).astype(np.float64)
    large = max_exact + np.floor(np.log(df / max_exact) / math.log(MAX_DISTANCE / max_exact)
                                 * (NUM_BUCKETS - max_exact) + 1e-9).astype(np.int64)
    large = np.minimum(large, NUM_BUCKETS - 1)
    return np.where(d < max_exact, d, large).astype(np.int32)


def _rms(x, g):
    ms = jnp.mean(x * x, axis=-1, keepdims=True)
    return (x * lax.rsqrt(ms + RMS_EPS)) * g


def _split3(w):
    hi = w.astype(BF16)
    r1 = w - hi.astype(F32)
    mid = r1.astype(BF16)
    lo = (r1 - mid.astype(F32)).astype(BF16)
    return hi, mid, lo


def _expand_heads(w, e3):
    lane = lax.broadcasted_iota(jnp.int32, w.shape, 1)
    r1 = w - w.astype(BF16).astype(F32)
    r2 = r1 - r1.astype(BF16).astype(F32)
    packed = jnp.where(lane < N_HEADS, w,
                       jnp.where(lane < 2 * N_HEADS, pltpu.roll(r1, N_HEADS, 1), pltpu.roll(r2, 2 * N_HEADS, 1)))
    return jnp.dot(packed.astype(BF16), e3, preferred_element_type=F32)


def _dot_nt(a, b):
    return lax.dot_general(a, b, (((1,), (1,)), ((), ())), preferred_element_type=F32)


def _stack_heads(q_tile, heads):
    lane = lax.broadcasted_iota(jnp.int32, (QBLK, LANES), 1)
    lo = lane < HEAD_DIM
    pieces = []
    for h in heads:
        q2 = q_tile(h // 2)
        keep = lo if h % 2 == 0 else jnp.logical_not(lo)
        pieces.append(jnp.where(keep, q2, jnp.zeros_like(q2)))
    return jnp.concatenate(pieces, axis=0)


def _merge_pairs(o, n_heads):
    lane = lax.broadcasted_iota(jnp.int32, (QBLK, LANES), 1)
    lo = lane < HEAD_DIM
    out = []
    for t in range(0, n_heads, 2):
        out.append(jnp.where(lo, o[t * QBLK:(t + 1) * QBLK], o[(t + 1) * QBLK:(t + 2) * QBLK]))
    return out


def _bias_kernel(tab_ref, rng_ref, idx_ref, o_ref):
    n = pl.program_id(0)
    idx = idx_ref[0]
    o_ref[...] = jnp.full(o_ref.shape, NEG_INF, F32)

    def body(b, carry):
        hit = idx == b
        for h in range(N_HEADS):
            o_ref[0, h] = jnp.where(hit, tab_ref[b, h], o_ref[0, h])
        return carry

    lax.fori_loop(rng_ref[n, 0], rng_ref[n, 1] + 1, body, 0)


def bias_expand(table, idx):
    n, r, c = idx.shape
    flat = idx.reshape(n, -1)
    lo = np.where(flat >= 0, flat, NUM_BUCKETS).min(axis=1)
    rng = np.stack([lo, flat.max(axis=1)], axis=1).astype(np.int32)
    return pl.pallas_call(
        _bias_kernel,
        grid=(n,),
        in_specs=[pl.BlockSpec(memory_space=pltpu.SMEM),
                  pl.BlockSpec(memory_space=pltpu.SMEM),
                  pl.BlockSpec((1, r, c), lambda i: (i, 0, 0))],
        out_specs=pl.BlockSpec((1, N_HEADS, r, c), lambda i: (i, 0, 0, 0)),
        out_shape=jax.ShapeDtypeStruct((n, N_HEADS, r, c), F32),
        compiler_params=_cparams("parallel"),
        name="bias_expand",
    )(table, jnp.asarray(rng), jnp.asarray(idx))


def _banded_idx(n_prev, max_dist, stride):
    kb = (n_prev + 1) * QBLK
    dist = np.arange(QBLK)[:, None] + n_prev * QBLK - np.arange(kb)[None, :]
    valid = (dist >= 0) & (dist <= max_dist)
    idx = np.where(valid, _t5_bucket_np(dist * stride), -1).astype(np.int32)
    if n_prev > 1:
        return idx[None]
    first = np.where(np.arange(kb)[None, :] < n_prev * QBLK, -1, idx).astype(np.int32)
    return np.stack([idx, first])


def _norm_mm_kernel(x_ref, g_ref, w_ref, o_ref, xn_ref):
    @pl.when(pl.program_id(1) == 0)
    def _():
        xn_ref[...] = _rms(x_ref[...], g_ref[...]).astype(xn_ref.dtype)

    o_ref[...] = jnp.dot(xn_ref[...], w_ref[...], preferred_element_type=F32).astype(o_ref.dtype)


def norm_matmul(x, gain, w, out_dtype, tm, tn):
    t, d = x.shape
    n = w.shape[1]
    scratch = [pltpu.VMEM((tm, d), BF16)]
    return pl.pallas_call(
        _norm_mm_kernel,
        grid=(t // tm, n // tn),
        in_specs=[pl.BlockSpec((tm, d), lambda i, j: (i, 0)),
                  pl.BlockSpec((1, d), lambda i, j: (0, 0)),
                  pl.BlockSpec((d, tn), lambda i, j: (0, j))],
        out_specs=pl.BlockSpec((tm, tn), lambda i, j: (i, j)),
        out_shape=jax.ShapeDtypeStruct((t, n), out_dtype),
        scratch_shapes=scratch,
        compiler_params=_cparams("parallel", "arbitrary"),
        name="norm_matmul",
    )(x, gain.reshape(1, d), w)


def _band_chunk(dil, seq):
    if seq % (BAND_ROWS * dil) == 0 and BAND_ROWS * dil <= PERM_ROWS:
        return BAND_ROWS
    return QBLK


def _norm_perm_kernel(x_ref, g_ref, *refs, dils, chunks):
    o_refs, xs_ref = refs[:len(dils)], refs[len(dils)]
    xn = _rms(x_ref[...], g_ref[...])
    nc = xs_ref.shape[0]
    for c in range(nc):
        xs_ref[c] = xn[:, c * LANES:(c + 1) * LANES]
    for o_ref, dil, cl in zip(o_refs, dils, chunks):
        if dil == 1:
            o_ref[...] = xn.astype(o_ref.dtype)
            continue
        span = cl * dil
        for u in range(x_ref.shape[0] // span):
            for r in range(dil):
                rows = jnp.concatenate(
                    [xs_ref[c, pl.ds(u * span + r, cl, stride=dil), :] for c in range(nc)], axis=1)
                dst = u * span + r * cl
                o_ref[dst:dst + cl, :] = rows.astype(o_ref.dtype)


def norm_permute(x, gain, dils, chunks, tm):
    t, d = x.shape
    assert all(tm % (cl * dil) == 0 for dil, cl in zip(dils, chunks))
    return pl.pallas_call(
        functools.partial(_norm_perm_kernel, dils=tuple(dils), chunks=tuple(chunks)),
        grid=(t // tm,),
        in_specs=[pl.BlockSpec((tm, d), lambda i: (i, 0)),
                  pl.BlockSpec((1, d), lambda i: (0, 0))],
        out_specs=[pl.BlockSpec((tm, d), lambda i: (i, 0)) for _ in dils],
        out_shape=[jax.ShapeDtypeStruct((t, d), BF16) for _ in dils],
        scratch_shapes=[pltpu.VMEM((d // LANES, tm, LANES), F32)],
        compiler_params=_cparams("parallel"),
        name="norm_permute",
    )(x, gain.reshape(1, d))


def _mm_kernel(x_ref, w_ref, o_ref):
    o_ref[...] = jnp.dot(x_ref[...], w_ref[...], preferred_element_type=F32).astype(o_ref.dtype)


def matmul_resident(x, w, out_dtype, tm):
    t, k = x.shape
    n = w.shape[1]
    return pl.pallas_call(
        _mm_kernel,
        grid=(t // tm,),
        in_specs=[pl.BlockSpec((tm, k), lambda i: (i, 0)),
                  pl.BlockSpec((k, n), lambda i: (0, 0))],
        out_specs=pl.BlockSpec((tm, n), lambda i: (i, 0)),
        out_shape=jax.ShapeDtypeStruct((t, n), out_dtype),
        compiler_params=_cparams("parallel"),
        name="matmul_resident",
    )(x, w)


def _finish_proj(y, w_ref, g_ref, h_ref, o_ref):
    z = jnp.dot(y.astype(BF16), w_ref[...], preferred_element_type=F32)
    o_ref[...] = h_ref[...] + _rms(z, g_ref[...])


def _proj_kernel(y_ref, w_ref, g_ref, h_ref, o_ref):
    _finish_proj(y_ref[...], w_ref, g_ref, h_ref, o_ref)


def _unpermute(src_ref, dst_ref, dil, cl, tm, i):
    span = cl * dil
    nc = dst_ref.shape[0]
    if span <= tm:
        for u in range(tm // span):
            for r in range(dil):
                lo = u * span + r * cl
                for c in range(nc):
                    dst_ref[c, pl.ds(u * span + r, cl, stride=dil), :] = src_ref[lo:lo + cl,
                                                                                 c * LANES:(c + 1) * LANES]
    else:
        per = tm // dil
        off = (i % (span // tm)) * per
        for r in range(dil):
            lo = pl.multiple_of(r * cl + off, 8)
            for c in range(nc):
                dst_ref[c, pl.ds(r, per, stride=dil), :] = src_ref[pl.ds(lo, per), c * LANES:(c + 1) * LANES]
    return jnp.concatenate([dst_ref[c] for c in range(nc)], axis=1) if nc > 1 else dst_ref[0]


def _proj_a_kernel(*refs, dils, chunks, tm):
    n = len(dils)
    o_refs, l_refs = refs[:n], refs[n:2 * n]
    e_ref, w_ref, g_ref, h_ref, out_ref = refs[2 * n:2 * n + 5]
    scratch = refs[2 * n + 5:]
    i = pl.program_id(0)
    outs, lses = [], []
    si = 0
    for gi, dil in enumerate(dils):
        if dil == 1:
            outs.append(o_refs[gi][...])
            lses.append(l_refs[gi][...])
        else:
            outs.append(_unpermute(o_refs[gi], scratch[si], dil, chunks[gi], tm, i))
            lses.append(_unpermute(l_refs[gi], scratch[si + 1], dil, chunks[gi], tm, i))
            si += 2
    mx = functools.reduce(jnp.maximum, lses)
    es = [jnp.exp(l - mx) for l in lses]
    inv = 1.0 / functools.reduce(lambda a, b: a + b, es)
    e = e_ref[...]
    y = outs[0] * _expand_heads(es[0] * inv, e)
    for gi in range(1, n):
        y = y + outs[gi] * _expand_heads(es[gi] * inv, e)
    _finish_proj(y, w_ref, g_ref, h_ref, out_ref)


def combine_proj_a(outs, lses, dils, chunks, e, w, gain, h, tm):
    t, d = h.shape

    def row_spec(width, dil, cl):
        rows = max(tm, cl * dil)
        return pl.BlockSpec((rows, width), lambda i, q=rows // tm: (i // q, 0))

    in_specs = [row_spec(HQ, dil, cl) for dil, cl in zip(dils, chunks)] + [
        row_spec(LANES, dil, cl) for dil, cl in zip(dils, chunks)] + [
        pl.BlockSpec(e.shape, lambda i: (0, 0)),
        pl.BlockSpec(w.shape, lambda i: (0, 0)),
        pl.BlockSpec((1, d), lambda i: (0, 0)),
        pl.BlockSpec((tm, d), lambda i: (i, 0))]
    scratch = []
    for dil in dils:
        if dil > 1:
            scratch += [pltpu.VMEM((HQ // LANES, tm, LANES), F32), pltpu.VMEM((1, tm, LANES), F32)]
    return pl.pallas_call(
        functools.partial(_proj_a_kernel, dils=tuple(dils), chunks=tuple(chunks), tm=tm),
        grid=(t // tm,),
        in_specs=in_specs,
        out_specs=pl.BlockSpec((tm, d), lambda i: (i, 0)),
        out_shape=jax.ShapeDtypeStruct((t, d), F32),
        scratch_shapes=scratch,
        compiler_params=_cparams("arbitrary"),
        name="dilated_combine_proj",
    )(*outs, *lses, e, w, gain.reshape(1, d), h)


def _proj_c_kernel(oc_ref, os_ref, ow_ref, gr_ref, e_ref, w_ref, g_ref, h_ref, o_ref):
    sig = jax.nn.sigmoid(gr_ref[...])
    e = e_ref[...]
    gate = lambda i: _expand_heads(sig if i == 0 else pltpu.roll(sig, LANES - i * N_HEADS, 1), e)
    y = gate(0) * oc_ref[...]
    y = y + gate(1) * os_ref[...]
    y = y + gate(2) * ow_ref[...]
    _finish_proj(y, w_ref, g_ref, h_ref, o_ref)


def _proj_call(kernel, name, tm, row_inputs, const_inputs, w, gain, h):
    t, d = h.shape
    k = w.shape[0]
    row_specs = [pl.BlockSpec((tm, a.shape[1]), lambda i: (i, 0)) for a in row_inputs]
    const_specs = [pl.BlockSpec(a.shape, (lambda nd: (lambda i: (0,) * nd))(a.ndim)) for a in const_inputs]
    return pl.pallas_call(
        kernel,
        grid=(t // tm,),
        in_specs=row_specs + const_specs + [
            pl.BlockSpec((k, d), lambda i: (0, 0)),
            pl.BlockSpec((1, d), lambda i: (0, 0)),
            pl.BlockSpec((tm, d), lambda i: (i, 0))],
        out_specs=pl.BlockSpec((tm, d), lambda i: (i, 0)),
        out_shape=jax.ShapeDtypeStruct((t, d), F32),
        compiler_params=_cparams("parallel"),
        name=name,
    )(*row_inputs, *const_inputs, w, gain.reshape(1, d), h)


def _head_expand_matrix():
    e = np.zeros((LANES, HQ), np.float32)
    for term in range(3):
        for h in range(N_HEADS):
            e[term * N_HEADS + h, h * HEAD_DIM:(h + 1) * HEAD_DIM] = 1.0
    return e


def _banded_kernel(*refs, tq, n_prev, n_chunks, stack, paired, with_sinks, with_lse):
    q_ref, kp_ref, kc_ref, vp_ref, vc_ref, bias_ref = refs[:6]
    pos = 6
    sink_ref = None
    if with_sinks:
        sink_ref = refs[pos]
        pos += 1
    o_ref = refs[pos]
    pos += 1
    lse_ref = None
    if with_lse:
        lse_ref = refs[pos]
        pos += 1
    kbuf, vbuf = refs[pos], refs[pos + 1]

    i = pl.program_id(1)
    hpc = N_HEADS // n_chunks
    kb = (n_prev + 1) * QBLK
    if paired:
        for s in range(tq // QBLK):
            blk = slice(s * QBLK, (s + 1) * QBLK)
            for buf, p_ref, c_ref in ((kbuf, kp_ref, kc_ref), (vbuf, vp_ref, vc_ref)):
                buf[2 * s * QBLK:(2 * s + 1) * QBLK] = p_ref[0, blk]
                buf[(2 * s + 1) * QBLK:(2 * s + 2) * QBLK] = c_ref[0, blk]
    else:
        kbuf[0:tq] = kp_ref[0]
        kbuf[tq:2 * tq] = kc_ref[0]
        vbuf[0:tq] = vp_ref[0]
        vbuf[tq:2 * tq] = vc_ref[0]
    lane = lax.broadcasted_iota(jnp.int32, (QBLK, LANES), 1)
    head_row = lax.broadcasted_iota(jnp.int32, (stack * QBLK, 1), 0)

    for s in range(tq // QBLK):
        r0 = s * QBLK
        k0 = 2 * r0 if paired else tq + r0 - n_prev * QBLK
        variants = bias_ref.shape[0] > 1
        if variants:
            first = jnp.where(i == 0, 1, 0) if (s == 0 or paired) else 0
        else:
            col = lax.broadcasted_iota(jnp.int32, (1, kb), 1) + k0
            kmask = jnp.where(jnp.logical_and(i == 0, col < tq), NEG_INF, 0.0).astype(F32)
        lse_acc = jnp.zeros((QBLK, LANES), F32)
        for h0 in range(0, N_HEADS, stack):
            heads = list(range(h0, h0 + stack))
            c = h0 // hpc
            qst = _stack_heads(lambda p: q_ref[0, r0:r0 + QBLK, p * LANES:(p + 1) * LANES], heads)
            kx = kbuf[k0:k0 + kb, c * LANES:(c + 1) * LANES]
            vx = vbuf[k0:k0 + kb, c * LANES:(c + 1) * LANES]
            sc = _dot_nt(qst, kx)
            if variants:
                sc = sc + bias_ref[first, h0:h0 + stack].reshape(stack * QBLK, kb)
            else:
                sc = sc + bias_ref[0, h0:h0 + stack].reshape(stack * QBLK, kb) + kmask
            m = jnp.max(sc, axis=-1, keepdims=True)
            if with_sinks:
                sk = sink_ref[h0 + stack - 1]
                for t in range(stack - 2, -1, -1):
                    sk = jnp.where(head_row < (t + 1) * QBLK, sink_ref[h0 + t], sk)
                m = jnp.maximum(m, sk)
            p = jnp.exp(sc - m)
            den = jnp.sum(p, axis=-1, keepdims=True)
            norm = den + jnp.exp(sk - m) if with_sinks else den
            o = jnp.dot(p.astype(BF16), vx, preferred_element_type=F32) * (1.0 / norm)
            for t2, blk in enumerate(_merge_pairs(o, stack)):
                pidx = h0 // 2 + t2
                o_ref[0, r0:r0 + QBLK, pidx * LANES:(pidx + 1) * LANES] = blk.astype(o_ref.dtype)
            if with_lse:
                lse = m + jnp.log(den)
                for t, h in enumerate(heads):
                    lse_acc = jnp.where(lane == h, lse[t * QBLK:(t + 1) * QBLK], lse_acc)
        if with_lse:
            lse_ref[0, r0:r0 + QBLK, :] = lse_acc


def banded_attention(q_arr, k_arr, v_arr, bias, *, n_rows, n_tiles, tq, n_prev, kw,
                     q_map, k_map, v_map, o_map, out_shape, lse_shape=None, sinks=None, stack=2,
                     paired=False, name):
    n_chunks = kw // LANES
    assert stack % 2 == 0 and (N_HEADS // n_chunks) % stack == 0 and (not paired or n_prev == 1)
    kb = (n_prev + 1) * QBLK
    with_sinks = sinks is not None
    with_lse = lse_shape is not None

    def prev(fn):
        return lambda n, i: fn(n, jnp.maximum(i - 1, 0))

    in_specs = [pl.BlockSpec((1, tq, HQ), q_map),
                pl.BlockSpec((1, tq, kw), prev(k_map)),
                pl.BlockSpec((1, tq, kw), k_map),
                pl.BlockSpec((1, tq, kw), prev(v_map)),
                pl.BlockSpec((1, tq, kw), v_map),
                pl.BlockSpec(bias.shape, lambda n, i: (0, 0, 0, 0))]
    assert bias.shape[1:] == (N_HEADS, QBLK, kb) and (bias.shape[0] == 1 or n_prev == 1)
    args = [q_arr, k_arr, k_arr, v_arr, v_arr, bias]
    if with_sinks:
        in_specs.append(pl.BlockSpec(memory_space=pltpu.SMEM))
        args.append(sinks)
    out_specs = [pl.BlockSpec((1, tq, HQ), o_map)]
    out_shapes = [jax.ShapeDtypeStruct(out_shape, F32)]
    if with_lse:
        out_specs.append(pl.BlockSpec((1, tq, LANES), o_map))
        out_shapes.append(jax.ShapeDtypeStruct(lse_shape, F32))
    res = pl.pallas_call(
        functools.partial(_banded_kernel, tq=tq, n_prev=n_prev, n_chunks=n_chunks, stack=stack,
                          paired=paired, with_sinks=with_sinks, with_lse=with_lse),
        grid=(n_rows, n_tiles),
        in_specs=in_specs,
        out_specs=out_specs,
        out_shape=out_shapes,
        scratch_shapes=[pltpu.VMEM((2 * tq, kw), BF16), pltpu.VMEM((2 * tq, kw), BF16)],
        compiler_params=_cparams("parallel", "arbitrary"),
        name=name,
    )(*args)
    return res


def _dup_groups(x):
    g0, g1 = x[..., :HEAD_DIM], x[..., HEAD_DIM:]
    return jnp.concatenate([g0, g0, g1, g1], axis=-1)


def _compress_kernel(ch_ref, pos_ref, w1_ref, w2_ref, o_ref):
    ch = ch_ref[0, 0, 0]
    rows = ch.shape[0]
    posv = pos_ref[0]
    a = jnp.dot((ch + posv[0:1]).astype(BF16), w1_ref[0, 0], preferred_element_type=F32)
    b = jnp.dot((ch + posv[1:2]).astype(BF16), w1_ref[0, 1], preferred_element_type=F32)
    hid = a + pltpu.roll(b, rows - 1, 0)
    act = jax.nn.gelu(hid, approximate=True)
    o_ref[0, 0, 0] = jnp.dot(act.astype(BF16), w2_ref[0], preferred_element_type=F32)


def compress(chunks, pos, w1, w2):
    _, b, g, rows, width = chunks.shape
    hid = w1.shape[-1]
    return pl.pallas_call(
        _compress_kernel,
        grid=(2, b, g),
        in_specs=[pl.BlockSpec((1, 1, 1, rows, width), lambda i, bb, gg: (i, bb, gg, 0, 0)),
                  pl.BlockSpec((1, 2, width), lambda i, bb, gg: (i, 0, 0)),
                  pl.BlockSpec((1, 2, width, hid), lambda i, bb, gg: (i, 0, 0, 0)),
                  pl.BlockSpec((1, hid, HEAD_DIM), lambda i, bb, gg: (i, 0, 0))],
        out_specs=pl.BlockSpec((1, 1, 1, rows, HEAD_DIM), lambda i, bb, gg: (i, bb, gg, 0, 0)),
        out_shape=jax.ShapeDtypeStruct((2, b, g, rows, HEAD_DIM), F32),
        compiler_params=_cparams("parallel", "parallel", "parallel"),
        name="nsa_compress",
    )(chunks, pos, w1, w2)


def _cmp_attn_kernel(q_ref, kc_ref, vc_ref, ov_ref, place_ref, o_ref, sel_ref, *, n_sel_blocks, k_sel):
    for w in range(q_ref.shape[1] // QBLK):
        rows = slice(w * QBLK, (w + 1) * QBLK)
        _cmp_attn_tile(pl.program_id(1) * (q_ref.shape[1] // QBLK) + w, q_ref.at[0, rows], kc_ref, vc_ref, ov_ref,
                       place_ref, o_ref.at[0, rows], sel_ref.at[0, rows], n_sel_blocks, k_sel)


def _cmp_attn_tile(qi, q_ref, kc_ref, vc_ref, ov_ref, place_ref, o_ref, sel_ref, n_sel_blocks, k_sel):
    ncr = kc_ref.shape[1]
    hpc = N_HEADS // C_KV_HEADS
    qpos = qi * QBLK + lax.broadcasted_iota(jnp.int32, (QBLK, 1), 0)
    cidx = lax.broadcasted_iota(jnp.int32, (1, ncr), 1)
    valid = (cidx * CMP_STRIDE + (CMP_BLOCK - 1)) <= qpos
    maskc = jnp.where(valid, 0.0, NEG_INF).astype(F32)
    anyv = (qpos >= CMP_BLOCK - 1).astype(F32)
    blk_id = lax.broadcasted_iota(jnp.int32, (HEAD_DIM, QBLK), 0)
    cur = (qi * QBLK + lax.broadcasted_iota(jnp.int32, (HEAD_DIM, QBLK), 1)) // SEL_BLOCK
    forced = jnp.logical_or(jnp.logical_or(blk_id == 0, blk_id == cur), blk_id == cur - 1)
    allowed = blk_id <= cur
    for g in range(C_KV_HEADS):
        heads = list(range(g * hpc, (g + 1) * hpc))
        qst = _stack_heads(lambda p: q_ref[:, p * LANES:(p + 1) * LANES], heads)
        sc = _dot_nt(qst, kc_ref[0, :, g * LANES:(g + 1) * LANES])
        sc3 = sc.reshape(hpc, QBLK, ncr) + maskc[None]
        m = jnp.max(sc3, axis=-1, keepdims=True)
        e = jnp.exp(sc3 - m)
        ssum = jnp.sum(e, axis=-1, keepdims=True)
        p = (e * (1.0 / ssum)) * anyv[None]
        o = jnp.dot(p.reshape(hpc * QBLK, ncr).astype(BF16), vc_ref[0, :, g * LANES:(g + 1) * LANES],
                    preferred_element_type=F32)
        for t2, blk in enumerate(_merge_pairs(o, hpc)):
            pidx = (g * hpc) // 2 + t2
            o_ref[:, pidx * LANES:(pidx + 1) * LANES] = blk
        hi, mid, lo = _split3(jnp.sum(p, axis=0))
        ovt = ov_ref[...]
        imp = (_dot_nt(ovt, hi) + _dot_nt(ovt, mid)) + _dot_nt(ovt, lo)
        score = jnp.where(forced, FORCE_SCORE, jnp.where(allowed, imp, NEG_INF))
        bits = pltpu.bitcast(score, jnp.int32)
        key = jnp.where(bits < 0, bits ^ jnp.int32(0x7FFFFFFF), bits)
        key_m1 = key - 1
        rank = jnp.zeros((HEAD_DIM, QBLK), jnp.int32)
        for i in range(n_sel_blocks):
            thr = jnp.where(blk_id > i, key_m1, key)
            rank = rank + jnp.where(key[i:i + 1, :] > thr, 1, 0)
        keep = jnp.logical_and(rank < k_sel, blk_id < n_sel_blocks)
        pen_t = jnp.where(keep, 0.0, -1.0).astype(BF16)
        pen = lax.dot_general(pen_t, place_ref[...], (((0,), (0,)), ((), ())), preferred_element_type=F32)
        sel_ref[:, g * LANES:(g + 1) * LANES] = pen.astype(sel_ref.dtype)


def cmp_attention(q, kcd, vcd, ov, n_sel_blocks, k_sel):
    b, s, _ = q.shape
    ncr = kcd.shape[1]
    place = np.zeros((HEAD_DIM, LANES), np.float32)
    place[np.arange(HEAD_DIM), HEAD_DIM + np.arange(HEAD_DIM)] = 1.0
    return pl.pallas_call(
        functools.partial(_cmp_attn_kernel, n_sel_blocks=n_sel_blocks, k_sel=k_sel),
        grid=(b, s // BAND_ROWS),
        in_specs=[pl.BlockSpec((1, BAND_ROWS, HQ), lambda bb, i: (bb, i, 0)),
                  pl.BlockSpec((1, ncr, 2 * LANES), lambda bb, i: (bb, 0, 0)),
                  pl.BlockSpec((1, ncr, 2 * LANES), lambda bb, i: (bb, 0, 0)),
                  pl.BlockSpec((HEAD_DIM, ncr), lambda bb, i: (0, 0)),
                  pl.BlockSpec((HEAD_DIM, LANES), lambda bb, i: (0, 0))],
        out_specs=[pl.BlockSpec((1, BAND_ROWS, HQ), lambda bb, i: (bb, i, 0)),
                   pl.BlockSpec((1, BAND_ROWS, 2 * LANES), lambda bb, i: (bb, i, 0))],
        out_shape=[jax.ShapeDtypeStruct((b, s, HQ), F32),
                   jax.ShapeDtypeStruct((b, s, 2 * LANES), BF16)],
        compiler_params=_cparams("parallel", "parallel"),
        name="nsa_cmp_attention",
    )(q, kcd, vcd, ov, jnp.asarray(place, BF16))


def _sel_attn_kernel(q_ref, pen_ref, k_ref, v_ref, bias_ref, o_ref, qst_ref, m_ref, acc_ref,
                     sa_ref, sb_ref, *, nbt):
    qi0 = pl.program_id(2) * SEL_QTILES
    hpc = N_HEADS // C_KV_HEADS
    nsub = SEL_CHUNK // QBLK
    rpt = hpc * QBLK
    lane = lax.broadcasted_iota(jnp.int32, (QBLK, LANES), 1)
    lo = lane < HEAD_DIM
    for w in range(SEL_QTILES):
        pen = pen_ref[0, w * QBLK:(w + 1) * QBLK, :].astype(F32)
        for t in range(hpc):
            q2 = q_ref[0, w * QBLK:(w + 1) * QBLK, (t // 2) * LANES:(t // 2 + 1) * LANES].astype(F32)
            if t % 2:
                q2 = pltpu.roll(q2, HEAD_DIM, 1)
            qst_ref[w * rpt + t * QBLK:w * rpt + (t + 1) * QBLK, :] = jnp.where(lo, q2, pen).astype(BF16)
    m_ref[...] = jnp.full(m_ref.shape, NEG_INF, F32)
    acc_ref[...] = jnp.zeros(acc_ref.shape, F32)

    n_chunks = (qi0 + SEL_QTILES - 1) // nsub + 1

    def scores(c, s_ref):
        k0 = pl.multiple_of(c * SEL_CHUNK, SEL_CHUNK)
        s_ref[...] = _dot_nt(qst_ref[...], k_ref[0, pl.ds(k0, SEL_CHUNK), :])

    def accumulate(c, s_ref):
        k0 = pl.multiple_of(c * SEL_CHUNK, SEL_CHUNK)
        ps, alphas = [], []
        for w in range(SEL_QTILES):
            rows = slice(w * rpt, (w + 1) * rpt)
            tiles = []
            for u in range(nsub):
                off = qi0 + w - nsub * c - u
                idx = jnp.where(off < 0, nbt, jnp.minimum(off, nbt - 1))
                tiles.append(s_ref[rows, u * QBLK:(u + 1) * QBLK] + bias_ref[idx].reshape(rpt, QBLK))
            m_old = m_ref[rows, :]
            m_new = jnp.maximum(m_old, jnp.max(functools.reduce(jnp.maximum, tiles), axis=-1, keepdims=True))
            ps.append(jnp.concatenate([jnp.exp(tl - m_new).astype(BF16) for tl in tiles], axis=1))
            alphas.append(jnp.exp(m_old - m_new))
            m_ref[rows, :] = m_new
        pv = jnp.dot(jnp.concatenate(ps, axis=0), v_ref[0, pl.ds(k0, SEL_CHUNK), :], preferred_element_type=F32)
        acc_ref[...] = jnp.concatenate(alphas, axis=0) * acc_ref[...] + pv

    scores(0, sa_ref)

    def body(cc, carry):
        c = 2 * cc
        scores(c + 1, sb_ref)
        accumulate(c, sa_ref)
        scores(jnp.minimum(c + 2, n_chunks - 1), sa_ref)
        accumulate(c + 1, sb_ref)
        return carry

    lax.fori_loop(0, n_chunks // 2, body, 0)

    @pl.when(n_chunks % 2 == 1)
    def _():
        accumulate(n_chunks - 1, sa_ref)

    acc = acc_ref[...]
    rolled = pltpu.roll(acc, HEAD_DIM, 1)
    for w in range(SEL_QTILES):
        for t2 in range(hpc // 2):
            ev = slice(w * rpt + 2 * t2 * QBLK, w * rpt + (2 * t2 + 1) * QBLK)
            od = slice(w * rpt + (2 * t2 + 1) * QBLK, w * rpt + (2 * t2 + 2) * QBLK)
            even = acc[ev] * (1.0 / rolled[ev])
            odd = rolled[od] * (1.0 / acc[od])
            o_ref[0, w * QBLK:(w + 1) * QBLK, t2 * LANES:(t2 + 1) * LANES] = jnp.where(lo, even, odd)


def sel_attention(q, kx, vx, pen, bias_tiles):
    b, s, _ = q.shape
    hpc = N_HEADS // C_KV_HEADS
    gw = hpc * HEAD_DIM
    nbt = bias_tiles.shape[0] - 1
    tq = SEL_QTILES * QBLK
    rows = SEL_QTILES * hpc * QBLK
    assert s % SEL_CHUNK == 0 and s % tq == 0
    return pl.pallas_call(
        functools.partial(_sel_attn_kernel, nbt=nbt),
        grid=(b, C_KV_HEADS, s // tq),
        in_specs=[pl.BlockSpec((1, tq, gw), lambda bb, g, i: (bb, i, g)),
                  pl.BlockSpec((1, tq, LANES), lambda bb, g, i: (bb, i, g)),
                  pl.BlockSpec((1, s, LANES), lambda bb, g, i: (bb, 0, g)),
                  pl.BlockSpec((1, s, LANES), lambda bb, g, i: (bb, 0, g)),
                  pl.BlockSpec((nbt + 1, hpc, QBLK, QBLK), lambda bb, g, i: (0, g, 0, 0))],
        out_specs=pl.BlockSpec((1, tq, gw), lambda bb, g, i: (bb, i, g)),
        out_shape=jax.ShapeDtypeStruct((b, s, HQ), F32),
        scratch_shapes=[pltpu.VMEM((rows, LANES), BF16),
                        pltpu.VMEM((rows, LANES), F32),
                        pltpu.VMEM((rows, LANES), F32),
                        pltpu.VMEM((rows, SEL_CHUNK), F32),
                        pltpu.VMEM((rows, SEL_CHUNK), F32)],
        compiler_params=_cparams("parallel", "parallel", "arbitrary"),
        name="nsa_sel_attention",
    )(q, pen, kx, vx, bias_tiles)


def _sel_bias_idx(s):
    nqt = s // QBLK
    far = -(-(int(np.argmax(_t5_bucket_np(np.arange(4 * MAX_DISTANCE)) == NUM_BUCKETS - 1)) + QBLK) // QBLK)
    nbt = min(nqt, far + 1)
    d0 = np.arange(nbt)[:, None, None] * QBLK
    dist = d0 + np.arange(QBLK)[None, :, None] - np.arange(QBLK)[None, None, :]
    idx = np.where(dist >= 0, _t5_bucket_np(dist), -1).astype(np.int32)
    return np.concatenate([idx, np.full((1, QBLK, QBLK), -1, np.int32)])


def _ffn_kernel(h_ref, g2_ref, wu_ref, cw_ref, cb_ref, wd_ref, g3_ref,
                o_ref, xn_ref, acc_ref, u_ref, a_ref, carry_ref, *, tiles_per_seq, nj):
    i = pl.program_id(0)
    ns, ts, tn = xn_ref.shape[0], xn_ref.shape[1], a_ref.shape[3]
    cols = lambda c: slice(c * tn, (c + 1) * tn)

    @pl.when(i % tiles_per_seq == 0)
    def _():
        carry_ref[...] = jnp.zeros(carry_ref.shape, F32)

    top = 16
    rowt = lax.broadcasted_iota(jnp.int32, (top, tn), 0)

    def up(s, j):
        xn = xn_ref[s]
        u_ref[s, j % 2, 0] = jnp.dot(xn, wu_ref[:, cols(j)], preferred_element_type=F32)
        u_ref[s, j % 2, 1] = jnp.dot(xn, wu_ref[:, cols(nj + j)], preferred_element_type=F32)

    def conv(u, s1, s2, c):
        cw = cw_ref[:, cols(c)]
        return ((cb_ref[:, cols(c)] + u * cw[2:3]) + s2 * cw[0:1]) + s1 * cw[1:2]

    def conv_body(u, c):
        return conv(u, pltpu.roll(u, 1, 0), pltpu.roll(u, 2, 0), c)

    def conv_top(uc_ref, c):
        prev = carry_ref[c]
        p1, p2 = prev[7:8], prev[6:7]
        u = uc_ref[0:top]
        s1 = jnp.where(rowt == 0, p1, pltpu.roll(u, 1, 0))
        s2 = jnp.where(rowt == 0, p2, jnp.where(rowt == 1, p1, pltpu.roll(u, 2, 0)))
        carry_ref[c] = uc_ref[ts - 8:ts]
        return conv(u, s1, s2, c)

    def gated(cg, cv):
        return (jax.nn.gelu(cg, approximate=True) * cv).astype(BF16)

    def act(s, j):
        ur, ar = u_ref.at[s, j % 2], a_ref.at[s, j % 2]
        ar[...] = gated(conv_body(ur[0], j), conv_body(ur[1], nj + j))
        ar[0:top] = gated(conv_top(ur.at[0], j), conv_top(ur.at[1], nj + j))

    def down(s, j):
        acc_ref[s] += jnp.dot(a_ref[s, j % 2], wd_ref[j * tn:(j + 1) * tn, :], preferred_element_type=F32)

    def iteration(s, k):
        rows = slice(s * ts, (s + 1) * ts)
        if k == -1:
            xn_ref[s] = _rms(h_ref[rows, :], g2_ref[...]).astype(xn_ref.dtype)
            acc_ref[s] = jnp.zeros(acc_ref.shape[1:], F32)
        if 0 <= k + 1 < nj:
            up(s, k + 1)
        if 0 <= k < nj:
            act(s, k)
        if 0 <= k - 1 < nj:
            down(s, k - 1)
        if k == nj:
            o_ref[rows, :] = h_ref[rows, :] + _rms(acc_ref[s], g3_ref[...])

    for k in range(-1, nj + 1):
        for s in range(ns):
            iteration(s, k)


def conv_ffn(h, g2, w_up, conv_w, conv_b, w_down, g3, seq, tm, tn, ns=FFN_SUBTILES):
    t, d = h.shape
    dff = w_down.shape[0]
    nj = dff // tn
    ts = tm // ns
    assert nj * tn == dff and ts * ns == tm and ts % 16 == 0
    wu, cw, cb, wd = w_up, conv_w, conv_b.reshape(1, -1), w_down
    resident = lambda a: pl.BlockSpec(a.shape, (lambda nd: (lambda i: (0,) * nd))(a.ndim))
    return pl.pallas_call(
        functools.partial(_ffn_kernel, tiles_per_seq=seq // tm, nj=nj),
        grid=(t // tm,),
        in_specs=[pl.BlockSpec((tm, d), lambda i: (i, 0)),
                  pl.BlockSpec((1, d), lambda i: (0, 0)),
                  resident(wu), resident(cw), resident(cb), resident(wd),
                  pl.BlockSpec((1, d), lambda i: (0, 0))],
        out_specs=pl.BlockSpec((tm, d), lambda i: (i, 0)),
        out_shape=jax.ShapeDtypeStruct((t, d), F32),
        scratch_shapes=[pltpu.VMEM((ns, ts, d), BF16), pltpu.VMEM((ns, ts, d), F32),
                        pltpu.VMEM((ns, 2, 2, ts, tn), F32), pltpu.VMEM((ns, 2, ts, tn), BF16),
                        pltpu.VMEM((2 * nj, 8, tn), F32)],
        compiler_params=_cparams("arbitrary"),
        name="conv_ffn",
    )(h, g2.reshape(1, d), wu, cw, cb, wd, g3.reshape(1, d))


def _row_tile(t):
    for tm in (1024, 512, 256, 128):
        if t % tm == 0:
            return tm
    raise ValueError(f"token count {t} is not a multiple of 128")


def mixer_a(h, gains, w_in, w_o, rel_table, bsz, seq):
    t, d = h.shape
    tm = _row_tile(t)
    n_dil = len(DIL_CONFIGS)
    a_in = w_in.shape[1]
    col_scale = np.ones((a_in,), np.float32).reshape(n_dil, 3, HQ)
    col_scale[:, 0] = ATTN_SCALE
    w = (w_in * col_scale.reshape(1, a_in)).astype(BF16)
    idx = np.concatenate([_banded_idx(1, window // dil, dil) for window, dil in DIL_CONFIGS])
    bias = bias_expand(rel_table, idx)
    outs, lses = [], []
    dils = [dil for _, dil in DIL_CONFIGS]
    chunks = [_band_chunk(dil, seq) for dil in dils]
    xns = norm_permute(h, gains[0], dils, chunks, max([tm] + [cl * dil for dil, cl in zip(dils, chunks)]))
    for gi, (window, dil) in enumerate(DIL_CONFIGS):
        cl = chunks[gi]
        pair = BAND_ROWS // cl if dil % (BAND_ROWS // cl) == 0 else 1
        tq, rows = cl * pair, dil // pair
        assert window // dil <= QBLK and seq % (cl * dil) == 0
        qkv = matmul_resident(xns[gi], w[:, gi * 3 * HQ:(gi + 1) * 3 * HQ], BF16, tm)
        qkv = qkv.reshape(1, t, 3 * HQ)

        def rmap(part, rows=rows, per_seq=seq // tq):
            return lambda n, i: (0, (n // rows) * per_seq + i * rows + n % rows, part)

        o, lse = banded_attention(
            qkv, qkv, qkv, bias[2 * gi:2 * gi + 2], n_rows=bsz * rows, n_tiles=seq // (cl * dil), tq=tq,
            n_prev=1, kw=HQ, q_map=rmap(0), k_map=rmap(1), v_map=rmap(2), o_map=rmap(0), paired=pair > 1,
            out_shape=(1, t, HQ), lse_shape=(1, t, LANES), name=f"dilated_attention_{dil}")
        outs.append(o.reshape(t, HQ))
        lses.append(lse.reshape(t, LANES))
    e = jnp.asarray(_head_expand_matrix(), BF16)
    return combine_proj_a(outs, lses, dils, chunks, e, w_o.astype(BF16), gains[1], h, min(tm, PROJ_ROWS))


def mixer_b(h, gains, w_in, sinks, w_o, rel_table, bsz, seq):
    t, d = h.shape
    tm = _row_tile(t)
    hk = B_KV_HEADS * HEAD_DIM
    n_in = w_in.shape[1]
    assert n_in == HQ + 2 * hk
    w = jnp.concatenate([w_in[:, :HQ] * ATTN_SCALE, _dup_groups(w_in[:, HQ:HQ + hk]),
                         _dup_groups(w_in[:, HQ + hk:])], axis=1).astype(BF16)
    kw = 2 * hk
    qkv = norm_matmul(h, gains[0], w, BF16, tm, HQ + 2 * kw).reshape(bsz, seq, HQ + 2 * kw)
    bias = bias_expand(rel_table, _banded_idx(1, B_WINDOW - 1, 1))
    sink_rows = sinks.astype(F32)
    ident = lambda n, i: (n, i, 0)
    col = lambda c: (lambda n, i: (n, i, c))
    (o,) = banded_attention(qkv, qkv, qkv, bias, n_rows=bsz, n_tiles=seq // BAND_ROWS, tq=BAND_ROWS, n_prev=1,
                            kw=kw, q_map=ident, k_map=col(HQ // kw), v_map=col(HQ // kw + 1), o_map=ident,
                            out_shape=(bsz, seq, HQ), sinks=sink_rows, stack=8, name="sink_window_attention")
    return _proj_call(_proj_kernel, "sink_proj", tm, [o.reshape(t, HQ)], [], w_o.astype(BF16), gains[1], h)


def mixer_c(h, gains, w_in, cmp_pos, cmp_w1, cmp_w2, w_o, rel_table, bsz, seq):
    t, d = h.shape
    tm = _row_tile(t)
    g = C_KV_HEADS
    hk = g * HEAD_DIM
    wkv = [w_in[:, HQ + i * hk:HQ + (i + 1) * hk] for i in range(6)]
    n_gate = w_in.shape[1] - HQ - 6 * hk
    assert n_gate == 3 * N_HEADS and hk == LANES
    wq = (w_in[:, :HQ] * ATTN_SCALE).astype(BF16)
    spread = lambda w: jnp.concatenate([w[:, :HEAD_DIM], jnp.zeros_like(w[:, :HEAD_DIM]),
                                        w[:, HEAD_DIM:], jnp.zeros_like(w[:, :HEAD_DIM])], axis=1)
    wb = jnp.concatenate([spread(wkv[2]), spread(wkv[3]), _dup_groups(wkv[4]), _dup_groups(wkv[5])],
                         axis=1).astype(BF16)
    wf = jnp.concatenate([wkv[0], wkv[1], jnp.pad(w_in[:, HQ + 6 * hk:], ((0, 0), (0, LANES - n_gate)))],
                         axis=1).astype(BF16)
    kw = 2 * LANES
    q = norm_matmul(h, gains[0], jnp.concatenate([wq, wb], axis=1), BF16, tm, HQ + 4 * kw)
    q = q.reshape(bsz, seq, HQ + 4 * kw)
    kvb = q[..., HQ:HQ + 2 * kw]
    r = norm_matmul(h, gains[0], wf, F32, tm, 3 * LANES)
    kv = [r[:, i * hk:(i + 1) * hk].reshape(bsz, seq, hk) for i in range(2)]
    gates_raw = r[:, 2 * hk:2 * hk + LANES]

    ncr = seq // CMP_STRIDE
    half = CMP_BLOCK // 2
    assert half == CMP_STRIDE
    chunks = jnp.stack([kv[0], kv[1]]).reshape(2, bsz, ncr, half, g, HEAD_DIM)
    chunks = chunks.transpose(0, 1, 4, 2, 3, 5).reshape(2, bsz, g, ncr, half * HEAD_DIM)
    pos = cmp_pos.reshape(2, 2, half * HEAD_DIM)
    w1 = cmp_w1.reshape(2, 2, half * HEAD_DIM, -1).astype(BF16)
    cmp = compress(chunks, pos, w1, cmp_w2.astype(BF16))
    cmp = cmp.transpose(0, 1, 3, 2, 4).reshape(2, bsz, ncr, hk).astype(BF16)
    kcd, vcd = _dup_groups(cmp[0]), _dup_groups(cmp[1])

    ns = seq // SEL_BLOCK
    assert ns <= HEAD_DIM
    k_sel = min(N_SELECT, ns)
    starts = np.arange(ncr) * CMP_STRIDE
    blk = np.arange(HEAD_DIM)
    ov = ((starts[None, :] < (blk[:, None] + 1) * SEL_BLOCK)
          & (starts[None, :] + CMP_BLOCK > blk[:, None] * SEL_BLOCK)
          & (blk[:, None] < ns) & (starts[None, :] + CMP_BLOCK <= seq))
    o_c, pen = cmp_attention(q, kcd, vcd, jnp.asarray(ov.astype(np.float32), BF16), ns, k_sel)

    sel_bias = bias_expand(rel_table, _sel_bias_idx(seq))
    lane = np.arange(kw)[None, :]
    key_blk = np.arange(seq)[:, None] // SEL_BLOCK
    onehot = np.where((lane % LANES >= HEAD_DIM) & (lane % HEAD_DIM == key_blk), -NEG_INF, 0.0)
    ones = np.broadcast_to(np.where(lane % LANES >= HEAD_DIM, 1.0, 0.0), (seq, kw))
    kx = kvb[..., :kw] + jnp.asarray(onehot, BF16)[None]
    vx = kvb[..., kw:2 * kw] + jnp.asarray(ones, BF16)[None]
    o_s = sel_attention(q, kx, vx, pen, sel_bias)

    n_prev = -(-(C_WINDOW - 1) // QBLK)
    tqw = n_prev * QBLK
    wbias = bias_expand(rel_table, _banded_idx(n_prev, C_WINDOW - 1, 1))
    ident = lambda n, i: (n, i, 0)
    col = lambda c: (lambda n, i: (n, i, c))
    (o_w,) = banded_attention(q, q, q, wbias,
                              n_rows=bsz, n_tiles=seq // tqw, tq=tqw, n_prev=n_prev, kw=kw,
                              q_map=ident, k_map=col(HQ // kw + 2), v_map=col(HQ // kw + 3), o_map=ident,
                              out_shape=(bsz, seq, HQ), stack=8, name="nsa_window_attention")

    e = jnp.asarray(_head_expand_matrix(), BF16)
    return _proj_call(_proj_c_kernel, "nsa_gate_proj", min(tm, PROJ_ROWS),
                      [o_c.reshape(t, HQ), o_s.reshape(t, HQ), o_w.reshape(t, HQ), gates_raw], [e],
                      w_o.astype(BF16), gains[1], h)


def kernel(x, rel_table, norm_gains, a_w_in, a_w_o, b_w_in, b_sinks, b_w_o, c_w_in, c_cmp_pos, c_cmp_w1,
           c_cmp_w2, c_w_o, ffn_w_up, ffn_conv_w, ffn_conv_b, ffn_w_down):
    bsz, seq, d = x.shape
    depth = norm_gains.shape[0]
    h = x.reshape(bsz * seq, d)
    tm = _row_tile(seq)
    for i in range(depth):
        kind, j = i % 3, i // 3
        g = norm_gains[i]
        if kind == 0:
            h = mixer_a(h, g, a_w_in[j], a_w_o[j], rel_table, bsz, seq)
        elif kind == 1:
            h = mixer_b(h, g, b_w_in[j], b_sinks[j], b_w_o[j], rel_table, bsz, seq)
        else:
            h = mixer_c(h, g, c_w_in[j], c_cmp_pos[j], c_cmp_w1[j], c_cmp_w2[j], c_w_o[j], rel_table, bsz, seq)
        h = conv_ffn(h, g[2], ffn_w_up[i].astype(BF16), ffn_conv_w[i], ffn_conv_b[i],
                     ffn_w_down[i].astype(BF16), g[3], seq, min(tm, FFN_ROWS), FFN_COLS)
    return h.reshape(bsz, seq, d)
```

```python
import functools
import math

import numpy as np
import jax
import jax.numpy as jnp
from jax import lax
from jax.experimental import pallas as pl
from jax.experimental.pallas import tpu as pltpu

F32 = jnp.float32
BF16 = jnp.bfloat16

N_HEADS = 16
HEAD_DIM = 64
HQ = N_HEADS * HEAD_DIM
LANES = 128
ATTN_SCALE = HEAD_DIM ** -0.5
NUM_BUCKETS = 32
MAX_DISTANCE = 2048
RMS_EPS = 1e-6
NEG_INF = -1e30
FORCE_SCORE = 1e9
DIL_CONFIGS = ((128, 1), (512, 4), (2048, 16))
B_KV_HEADS = 2
B_WINDOW = 128
C_KV_HEADS = 2
CMP_BLOCK = 32
CMP_STRIDE = 16
SEL_BLOCK = 64
N_SELECT = 16
C_WINDOW = 512
CONV_WIDTH = 3
QBLK = 128
BAND_ROWS = 256
PROJ_ROWS = 512
PERM_ROWS = 2048
SEL_CHUNK = 512
SEL_QTILES = 2
FFN_COLS = 256
FFN_ROWS = 512
FFN_SUBTILES = 2
VMEM_LIMIT = 56 * 1024 * 1024


def _cparams(*sem):
    return pltpu.CompilerParams(dimension_semantics=sem, vmem_limit_bytes=VMEM_LIMIT)


def _t5_bucket_np(dist):
    max_exact = NUM_BUCKETS // 2
    d = np.maximum(dist, 0)
    df = np.maximum(d, 1).astype(np.float64)
    large = max_exact + np.floor(np.log(df / max_exact) / math.log(MAX_DISTANCE / max_exact)
                                 * (NUM_BUCKETS - max_exact) + 1e-9).astype(np.int64)
    large = np.minimum(large, NUM_BUCKETS - 1)
    return np.where(d < max_exact, d, large).astype(np.int32)


def _rms(x, g):
    ms = jnp.mean(x * x, axis=-1, keepdims=True)
    return (x * lax.rsqrt(ms + RMS_EPS)) * g


def _split3(w):
    hi = w.astype(BF16)
    r1 = w - hi.astype(F32)
    mid = r1.astype(BF16)
    lo = (r1 - mid.astype(F32)).astype(BF16)
    return hi, mid, lo


def _expand_heads(w, e3):
    lane = lax.broadcasted_iota(jnp.int32, w.shape, 1)
    r1 = w - w.astype(BF16).astype(F32)
    r2 = r1 - r1.astype(BF16).astype(F32)
    packed = jnp.where(lane < N_HEADS, w,
                       jnp.where(lane < 2 * N_HEADS, pltpu.roll(r1, N_HEADS, 1), pltpu.roll(r2, 2 * N_HEADS, 1)))
    return jnp.dot(packed.astype(BF16), e3, preferred_element_type=F32)


def _dot_nt(a, b):
    return lax.dot_general(a, b, (((1,), (1,)), ((), ())), preferred_element_type=F32)


def _stack_heads(q_tile, heads):
    lane = lax.broadcasted_iota(jnp.int32, (QBLK, LANES), 1)
    lo = lane < HEAD_DIM
    pieces = []
    for h in heads:
        q2 = q_tile(h // 2)
        keep = lo if h % 2 == 0 else jnp.logical_not(lo)
        pieces.append(jnp.where(keep, q2, jnp.zeros_like(q2)))
    return jnp.concatenate(pieces, axis=0)


def _merge_pairs(o, n_heads):
    lane = lax.broadcasted_iota(jnp.int32, (QBLK, LANES), 1)
    lo = lane < HEAD_DIM
    out = []
    for t in range(0, n_heads, 2):
        out.append(jnp.where(lo, o[t * QBLK:(t + 1) * QBLK], o[(t + 1) * QBLK:(t + 2) * QBLK]))
    return out


def _bias_kernel(tab_ref, rng_ref, idx_ref, o_ref):
    n = pl.program_id(0)
    idx = idx_ref[0]
    o_ref[...] = jnp.full(o_ref.shape, NEG_INF, F32)

    def body(b, carry):
        hit = idx == b
        for h in range(N_HEADS):
            o_ref[0, h] = jnp.where(hit, tab_ref[b, h], o_ref[0, h])
        return carry

    lax.fori_loop(rng_ref[n, 0], rng_ref[n, 1] + 1, body, 0)


def bias_expand(table, idx):
    n, r, c = idx.shape
    flat = idx.reshape(n, -1)
    lo = np.where(flat >= 0, flat, NUM_BUCKETS).min(axis=1)
    rng = np.stack([lo, flat.max(axis=1)], axis=1).astype(np.int32)
    return pl.pallas_call(
        _bias_kernel,
        grid=(n,),
        in_specs=[pl.BlockSpec(memory_space=pltpu.SMEM),
                  pl.BlockSpec(memory_space=pltpu.SMEM),
                  pl.BlockSpec((1, r, c), lambda i: (i, 0, 0))],
        out_specs=pl.BlockSpec((1, N_HEADS, r, c), lambda i: (i, 0, 0, 0)),
        out_shape=jax.ShapeDtypeStruct((n, N_HEADS, r, c), F32),
        compiler_params=_cparams("parallel"),
        name="bias_expand",
    )(table, jnp.asarray(rng), jnp.asarray(idx))


def _banded_idx(n_prev, max_dist, stride):
    kb = (n_prev + 1) * QBLK
    dist = np.arange(QBLK)[:, None] + n_prev * QBLK - np.arange(kb)[None, :]
    valid = (dist >= 0) & (dist <= max_dist)
    idx = np.where(valid, _t5_bucket_np(dist * stride), -1).astype(np.int32)
    if n_prev > 1:
        return idx[None]
    first = np.where(np.arange(kb)[None, :] < n_prev * QBLK, -1, idx).astype(np.int32)
    return np.stack([idx, first])


def _norm_mm_kernel(x_ref, g_ref, w_ref, o_ref, xn_ref):
    @pl.when(pl.program_id(1) == 0)
    def _():
        xn_ref[...] = _rms(x_ref[...], g_ref[...]).astype(xn_ref.dtype)

    o_ref[...] = jnp.dot(xn_ref[...], w_ref[...], preferred_element_type=F32).astype(o_ref.dtype)


def norm_matmul(x, gain, w, out_dtype, tm, tn):
    t, d = x.shape
    n = w.shape[1]
    scratch = [pltpu.VMEM((tm, d), BF16)]
    return pl.pallas_call(
        _norm_mm_kernel,
        grid=(t // tm, n // tn),
        in_specs=[pl.BlockSpec((tm, d), lambda i, j: (i, 0)),
                  pl.BlockSpec((1, d), lambda i, j: (0, 0)),
                  pl.BlockSpec((d, tn), lambda i, j: (0, j))],
        out_specs=pl.BlockSpec((tm, tn), lambda i, j: (i, j)),
        out_shape=jax.ShapeDtypeStruct((t, n), out_dtype),
        scratch_shapes=scratch,
        compiler_params=_cparams("parallel", "arbitrary"),
        name="norm_matmul",
    )(x, gain.reshape(1, d), w)


def _band_chunk(dil, seq):
    if seq % (BAND_ROWS * dil) == 0 and BAND_ROWS * dil <= PERM_ROWS:
        return BAND_ROWS
    return QBLK


def _norm_perm_kernel(x_ref, g_ref, *refs, dils, chunks):
    o_refs, xs_ref = refs[:len(dils)], refs[len(dils)]
    xn = _rms(x_ref[...], g_ref[...])
    nc = xs_ref.shape[0]
    for c in range(nc):
        xs_ref[c] = xn[:, c * LANES:(c + 1) * LANES]
    for o_ref, dil, cl in zip(o_refs, dils, chunks):
        if dil == 1:
            o_ref[...] = xn.astype(o_ref.dtype)
            continue
        span = cl * dil
        for u in range(x_ref.shape[0] // span):
            for r in range(dil):
                rows = jnp.concatenate(
                    [xs_ref[c, pl.ds(u * span + r, cl, stride=dil), :] for c in range(nc)], axis=1)
                dst = u * span + r * cl
                o_ref[dst:dst + cl, :] = rows.astype(o_ref.dtype)


def norm_permute(x, gain, dils, chunks, tm):
    t, d = x.shape
    assert all(tm % (cl * dil) == 0 for dil, cl in zip(dils, chunks))
    return pl.pallas_call(
        functools.partial(_norm_perm_kernel, dils=tuple(dils), chunks=tuple(chunks)),
        grid=(t // tm,),
        in_specs=[pl.BlockSpec((tm, d), lambda i: (i, 0)),
                  pl.BlockSpec((1, d), lambda i: (0, 0))],
        out_specs=[pl.BlockSpec((tm, d), lambda i: (i, 0)) for _ in dils],
        out_shape=[jax.ShapeDtypeStruct((t, d), BF16) for _ in dils],
        scratch_shapes=[pltpu.VMEM((d // LANES, tm, LANES), F32)],
        compiler_params=_cparams("parallel"),
        name="norm_permute",
    )(x, gain.reshape(1, d))


def _mm_kernel(x_ref, w_ref, o_ref):
    o_ref[...] = jnp.dot(x_ref[...], w_ref[...], preferred_element_type=F32).astype(o_ref.dtype)


def matmul_resident(x, w, out_dtype, tm):
    t, k = x.shape
    n = w.shape[1]
    return pl.pallas_call(
        _mm_kernel,
        grid=(t // tm,),
        in_specs=[pl.BlockSpec((tm, k), lambda i: (i, 0)),
                  pl.BlockSpec((k, n), lambda i: (0, 0))],
        out_specs=pl.BlockSpec((tm, n), lambda i: (i, 0)),
        out_shape=jax.ShapeDtypeStruct((t, n), out_dtype),
        compiler_params=_cparams("parallel"),
        name="matmul_resident",
    )(x, w)


def _finish_proj(y, w_ref, g_ref, h_ref, o_ref):
    z = jnp.dot(y.astype(BF16), w_ref[...], preferred_element_type=F32)
    o_ref[...] = h_ref[...] + _rms(z, g_ref[...])


def _proj_kernel(y_ref, w_ref, g_ref, h_ref, o_ref):
    _finish_proj(y_ref[...], w_ref, g_ref, h_ref, o_ref)


def _unpermute(src_ref, dst_ref, dil, cl, tm, i):
    span = cl * dil
    nc = dst_ref.shape[0]
    if span <= tm:
        for u in range(tm // span):
            for r in range(dil):
                lo = u * span + r * cl
                for c in range(nc):
                    dst_ref[c, pl.ds(u * span + r, cl, stride=dil), :] = src_ref[lo:lo + cl,
                                                                                 c * LANES:(c + 1) * LANES]
    else:
        per = tm // dil
        off = (i % (span // tm)) * per
        for r in range(dil):
            lo = pl.multiple_of(r * cl + off, 8)
            for c in range(nc):
                dst_ref[c, pl.ds(r, per, stride=dil), :] = src_ref[pl.ds(lo, per), c * LANES:(c + 1) * LANES]
    return jnp.concatenate([dst_ref[c] for c in range(nc)], axis=1) if nc > 1 else dst_ref[0]


def _proj_a_kernel(*refs, dils, chunks, tm):
    n = len(dils)
    o_refs, l_refs = refs[:n], refs[n:2 * n]
    e_ref, w_ref, g_ref, h_ref, out_ref = refs[2 * n:2 * n + 5]
    scratch = refs[2 * n + 5:]
    i = pl.program_id(0)
    outs, lses = [], []
    si = 0
    for gi, dil in enumerate(dils):
        if dil == 1:
            outs.append(o_refs[gi][...])
            lses.append(l_refs[gi][...])
        else:
            outs.append(_unpermute(o_refs[gi], scratch[si], dil, chunks[gi], tm, i))
            lses.append(_unpermute(l_refs[gi], scratch[si + 1], dil, chunks[gi], tm, i))
            si += 2
    mx = functools.reduce(jnp.maximum, lses)
    es = [jnp.exp(l - mx) for l in lses]
    inv = 1.0 / functools.reduce(lambda a, b: a + b, es)
    e = e_ref[...]
    y = outs[0] * _expand_heads(es[0] * inv, e)
    for gi in range(1, n):
        y = y + outs[gi] * _expand_heads(es[gi] * inv, e)
    _finish_proj(y, w_ref, g_ref, h_ref, out_ref)


def combine_proj_a(outs, lses, dils, chunks, e, w, gain, h, tm):
    t, d = h.shape

    def row_spec(width, dil, cl):
        rows = max(tm, cl * dil)
        return pl.BlockSpec((rows, width), lambda i, q=rows // tm: (i // q, 0))

    in_specs = [row_spec(HQ, dil, cl) for dil, cl in zip(dils, chunks)] + [
        row_spec(LANES, dil, cl) for dil, cl in zip(dils, chunks)] + [
        pl.BlockSpec(e.shape, lambda i: (0, 0)),
        pl.BlockSpec(w.shape, lambda i: (0, 0)),
        pl.BlockSpec((1, d), lambda i: (0, 0)),
        pl.BlockSpec((tm, d), lambda i: (i, 0))]
    scratch = []
    for dil in dils:
        if dil > 1:
            scratch += [pltpu.VMEM((HQ // LANES, tm, LANES), F32), pltpu.VMEM((1, tm, LANES), F32)]
    return pl.pallas_call(
        functools.partial(_proj_a_kernel, dils=tuple(dils), chunks=tuple(chunks), tm=tm),
        grid=(t // tm,),
        in_specs=in_specs,
        out_specs=pl.BlockSpec((tm, d), lambda i: (i, 0)),
        out_shape=jax.ShapeDtypeStruct((t, d), F32),
        scratch_shapes=scratch,
        compiler_params=_cparams("arbitrary"),
        name="dilated_combine_proj",
    )(*outs, *lses, e, w, gain.reshape(1, d), h)


def _proj_c_kernel(oc_ref, os_ref, ow_ref, gr_ref, e_ref, w_ref, g_ref, h_ref, o_ref):
    sig = jax.nn.sigmoid(gr_ref[...])
    e = e_ref[...]
    gate = lambda i: _expand_heads(sig if i == 0 else pltpu.roll(sig, LANES - i * N_HEADS, 1), e)
    y = gate(0) * oc_ref[...]
    y = y + gate(1) * os_ref[...]
    y = y + gate(2) * ow_ref[...]
    _finish_proj(y, w_ref, g_ref, h_ref, o_ref)


def _proj_call(kernel, name, tm, row_inputs, const_inputs, w, gain, h):
    t, d = h.shape
    k = w.shape[0]
    row_specs = [pl.BlockSpec((tm, a.shape[1]), lambda i: (i, 0)) for a in row_inputs]
    const_specs = [pl.BlockSpec(a.shape, (lambda nd: (lambda i: (0,) * nd))(a.ndim)) for a in const_inputs]
    return pl.pallas_call(
        kernel,
        grid=(t // tm,),
        in_specs=row_specs + const_specs + [
            pl.BlockSpec((k, d), lambda i: (0, 0)),
            pl.BlockSpec((1, d), lambda i: (0, 0)),
            pl.BlockSpec((tm, d), lambda i: (i, 0))],
        out_specs=pl.BlockSpec((tm, d), lambda i: (i, 0)),
        out_shape=jax.ShapeDtypeStruct((t, d), F32),
        compiler_params=_cparams("parallel"),
        name=name,
    )(*row_inputs, *const_inputs, w, gain.reshape(1, d), h)


def _head_expand_matrix():
    e = np.zeros((LANES, HQ), np.float32)
    for term in range(3):
        for h in range(N_HEADS):
            e[term * N_HEADS + h, h * HEAD_DIM:(h + 1) * HEAD_DIM] = 1.0
    return e


def _banded_kernel(*refs, tq, n_prev, n_chunks, stack, paired, with_sinks, with_lse, with_proj):
    q_ref, kp_ref, kc_ref, vp_ref, vc_ref, bias_ref = refs[:6]
    pos = 6
    sink_ref = None
    if with_sinks:
        sink_ref = refs[pos]
        pos += 1
    if with_proj:
        w_ref, g_ref, h_ref = refs[pos:pos + 3]
        pos += 3
    o_ref = refs[pos]
    pos += 1
    lse_ref = None
    if with_lse:
        lse_ref = refs[pos]
        pos += 1
    kbuf, vbuf = refs[pos], refs[pos + 1]

    i = pl.program_id(1)
    hpc = N_HEADS // n_chunks
    kb = (n_prev + 1) * QBLK
    if paired:
        for s in range(tq // QBLK):
            blk = slice(s * QBLK, (s + 1) * QBLK)
            for buf, p_ref, c_ref in ((kbuf, kp_ref, kc_ref), (vbuf, vp_ref, vc_ref)):
                buf[2 * s * QBLK:(2 * s + 1) * QBLK] = p_ref[0, blk]
                buf[(2 * s + 1) * QBLK:(2 * s + 2) * QBLK] = c_ref[0, blk]
    else:
        kbuf[0:tq] = kp_ref[0]
        kbuf[tq:2 * tq] = kc_ref[0]
        vbuf[0:tq] = vp_ref[0]
        vbuf[tq:2 * tq] = vc_ref[0]
    lane = lax.broadcasted_iota(jnp.int32, (QBLK, LANES), 1)
    head_row = lax.broadcasted_iota(jnp.int32, (stack * QBLK, 1), 0)

    for s in range(tq // QBLK):
        r0 = s * QBLK
        k0 = 2 * r0 if paired else tq + r0 - n_prev * QBLK
        variants = bias_ref.shape[0] > 1
        if variants:
            first = jnp.where(i == 0, 1, 0) if (s == 0 or paired) else 0
        else:
            col = lax.broadcasted_iota(jnp.int32, (1, kb), 1) + k0
            kmask = jnp.where(jnp.logical_and(i == 0, col < tq), NEG_INF, 0.0).astype(F32)
        lse_acc = jnp.zeros((QBLK, LANES), F32)
        for h0 in range(0, N_HEADS, stack):
            heads = list(range(h0, h0 + stack))
            c = h0 // hpc
            qst = _stack_heads(lambda p: q_ref[0, r0:r0 + QBLK, p * LANES:(p + 1) * LANES], heads)
            kx = kbuf[k0:k0 + kb, c * LANES:(c + 1) * LANES]
            vx = vbuf[k0:k0 + kb, c * LANES:(c + 1) * LANES]
            sc = _dot_nt(qst, kx)
            if variants:
                sc = sc + bias_ref[first, h0:h0 + stack].reshape(stack * QBLK, kb)
            else:
                sc = sc + bias_ref[0, h0:h0 + stack].reshape(stack * QBLK, kb) + kmask
            m = jnp.max(sc, axis=-1, keepdims=True)
            if with_sinks:
                sk = sink_ref[h0 + stack - 1]
                for t in range(stack - 2, -1, -1):
                    sk = jnp.where(head_row < (t + 1) * QBLK, sink_ref[h0 + t], sk)
                m = jnp.maximum(m, sk)
            p = jnp.exp(sc - m)
            den = jnp.sum(p, axis=-1, keepdims=True)
            norm = den + jnp.exp(sk - m) if with_sinks else den
            o = jnp.dot(p.astype(BF16), vx, preferred_element_type=F32) * (1.0 / norm)
            for t2, blk in enumerate(_merge_pairs(o, stack)):
                pidx = h0 // 2 + t2
                if with_proj:
                    refs[-1][r0:r0 + QBLK, pidx * LANES:(pidx + 1) * LANES] = blk
                else:
                    o_ref[0, r0:r0 + QBLK, pidx * LANES:(pidx + 1) * LANES] = blk.astype(o_ref.dtype)
            if with_lse:
                lse = m + jnp.log(den)
                for t, h in enumerate(heads):
                    lse_acc = jnp.where(lane == h, lse[t * QBLK:(t + 1) * QBLK], lse_acc)
        if with_lse:
            lse_ref[0, r0:r0 + QBLK, :] = lse_acc
    if with_proj:
        _finish_proj(refs[-1][...], w_ref, g_ref, h_ref.at[0], o_ref.at[0])


def banded_attention(q_arr, k_arr, v_arr, bias, *, n_rows, n_tiles, tq, n_prev, kw,
                     q_map, k_map, v_map, o_map, out_shape, lse_shape=None, sinks=None, stack=2,
                     paired=False, proj=None, name):
    n_chunks = kw // LANES
    assert stack % 2 == 0 and (N_HEADS // n_chunks) % stack == 0 and (not paired or n_prev == 1)
    kb = (n_prev + 1) * QBLK
    with_sinks = sinks is not None
    with_lse = lse_shape is not None

    def prev(fn):
        return lambda n, i: fn(n, jnp.maximum(i - 1, 0))

    in_specs = [pl.BlockSpec((1, tq, HQ), q_map),
                pl.BlockSpec((1, tq, kw), prev(k_map)),
                pl.BlockSpec((1, tq, kw), k_map),
                pl.BlockSpec((1, tq, kw), prev(v_map)),
                pl.BlockSpec((1, tq, kw), v_map),
                pl.BlockSpec(bias.shape, lambda n, i: (0, 0, 0, 0))]
    assert bias.shape[1:] == (N_HEADS, QBLK, kb) and (bias.shape[0] == 1 or n_prev == 1)
    args = [q_arr, k_arr, k_arr, v_arr, v_arr, bias]
    if with_sinks:
        in_specs.append(pl.BlockSpec(memory_space=pltpu.SMEM))
        args.append(sinks)
    scratch = [pltpu.VMEM((2 * tq, kw), BF16), pltpu.VMEM((2 * tq, kw), BF16)]
    if proj is not None:
        w_o, gain, h3 = proj
        assert w_o.shape == (HQ, HQ) and h3.shape == out_shape and not with_lse
        in_specs += [pl.BlockSpec(w_o.shape, lambda n, i: (0, 0)),
                     pl.BlockSpec((1, HQ), lambda n, i: (0, 0)),
                     pl.BlockSpec((1, tq, HQ), o_map)]
        args += [w_o, gain.reshape(1, HQ), h3]
        scratch.append(pltpu.VMEM((tq, HQ), F32))
    out_specs = [pl.BlockSpec((1, tq, HQ), o_map)]
    out_shapes = [jax.ShapeDtypeStruct(out_shape, F32)]
    if with_lse:
        out_specs.append(pl.BlockSpec((1, tq, LANES), o_map))
        out_shapes.append(jax.ShapeDtypeStruct(lse_shape, F32))
    res = pl.pallas_call(
        functools.partial(_banded_kernel, tq=tq, n_prev=n_prev, n_chunks=n_chunks, stack=stack,
                          paired=paired, with_sinks=with_sinks, with_lse=with_lse, with_proj=proj is not None),
        grid=(n_rows, n_tiles),
        in_specs=in_specs,
        out_specs=out_specs,
        out_shape=out_shapes,
        scratch_shapes=scratch,
        compiler_params=_cparams("parallel", "arbitrary"),
        name=name,
    )(*args)
    return res


def _dup_groups(x):
    g0, g1 = x[..., :HEAD_DIM], x[..., HEAD_DIM:]
    return jnp.concatenate([g0, g0, g1, g1], axis=-1)


def _compress_kernel(ch_ref, pos_ref, w1_ref, w2_ref, o_ref):
    ch = ch_ref[0, 0, 0]
    rows = ch.shape[0]
    posv = pos_ref[0]
    a = jnp.dot((ch + posv[0:1]).astype(BF16), w1_ref[0, 0], preferred_element_type=F32)
    b = jnp.dot((ch + posv[1:2]).astype(BF16), w1_ref[0, 1], preferred_element_type=F32)
    hid = a + pltpu.roll(b, rows - 1, 0)
    act = jax.nn.gelu(hid, approximate=True)
    o_ref[0, 0, 0] = jnp.dot(act.astype(BF16), w2_ref[0], preferred_element_type=F32)


def compress(chunks, pos, w1, w2):
    _, b, g, rows, width = chunks.shape
    hid = w1.shape[-1]
    return pl.pallas_call(
        _compress_kernel,
        grid=(2, b, g),
        in_specs=[pl.BlockSpec((1, 1, 1, rows, width), lambda i, bb, gg: (i, bb, gg, 0, 0)),
                  pl.BlockSpec((1, 2, width), lambda i, bb, gg: (i, 0, 0)),
                  pl.BlockSpec((1, 2, width, hid), lambda i, bb, gg: (i, 0, 0, 0)),
                  pl.BlockSpec((1, hid, HEAD_DIM), lambda i, bb, gg: (i, 0, 0))],
        out_specs=pl.BlockSpec((1, 1, 1, rows, HEAD_DIM), lambda i, bb, gg: (i, bb, gg, 0, 0)),
        out_shape=jax.ShapeDtypeStruct((2, b, g, rows, HEAD_DIM), F32),
        compiler_params=_cparams("parallel", "parallel", "parallel"),
        name="nsa_compress",
    )(chunks, pos, w1, w2)


def _cmp_attn_kernel(q_ref, kc_ref, vc_ref, ov_ref, place_ref, o_ref, sel_ref, *, n_sel_blocks, k_sel):
    qi = pl.program_id(1)
    ncr = kc_ref.shape[1]
    hpc = N_HEADS // C_KV_HEADS
    qpos = qi * QBLK + lax.broadcasted_iota(jnp.int32, (QBLK, 1), 0)
    cidx = lax.broadcasted_iota(jnp.int32, (1, ncr), 1)
    valid = (cidx * CMP_STRIDE + (CMP_BLOCK - 1)) <= qpos
    maskc = jnp.where(valid, 0.0, NEG_INF).astype(F32)
    anyv = (qpos >= CMP_BLOCK - 1).astype(F32)
    blk_id = lax.broadcasted_iota(jnp.int32, (HEAD_DIM, QBLK), 0)
    cur = (qi * QBLK + lax.broadcasted_iota(jnp.int32, (HEAD_DIM, QBLK), 1)) // SEL_BLOCK
    forced = jnp.logical_or(jnp.logical_or(blk_id == 0, blk_id == cur), blk_id == cur - 1)
    allowed = blk_id <= cur
    for g in range(C_KV_HEADS):
        heads = list(range(g * hpc, (g + 1) * hpc))
        qst = _stack_heads(lambda p: q_ref[0, :, p * LANES:(p + 1) * LANES], heads)
        sc = _dot_nt(qst, kc_ref[0, :, g * LANES:(g + 1) * LANES])
        sc3 = sc.reshape(hpc, QBLK, ncr) + maskc[None]
        m = jnp.max(sc3, axis=-1, keepdims=True)
        e = jnp.exp(sc3 - m)
        ssum = jnp.sum(e, axis=-1, keepdims=True)
        p = (e * (1.0 / ssum)) * anyv[None]
        o = jnp.dot(p.reshape(hpc * QBLK, ncr).astype(BF16), vc_ref[0, :, g * LANES:(g + 1) * LANES],
                    preferred_element_type=F32)
        for t2, blk in enumerate(_merge_pairs(o, hpc)):
            pidx = (g * hpc) // 2 + t2
            o_ref[0, :, pidx * LANES:(pidx + 1) * LANES] = blk
        hi, mid, lo = _split3(jnp.sum(p, axis=0))
        ovt = ov_ref[...]
        imp = (_dot_nt(ovt, hi) + _dot_nt(ovt, mid)) + _dot_nt(ovt, lo)
        score = jnp.where(forced, FORCE_SCORE, jnp.where(allowed, imp, NEG_INF))
        bits = pltpu.bitcast(score, jnp.int32)
        key = jnp.where(bits < 0, bits ^ jnp.int32(0x7FFFFFFF), bits)
        key_m1 = key - 1
        rank = jnp.zeros((HEAD_DIM, QBLK), jnp.int32)
        for i in range(n_sel_blocks):
            thr = jnp.where(blk_id > i, key_m1, key)
            rank = rank + jnp.where(key[i:i + 1, :] > thr, 1, 0)
        keep = jnp.logical_and(rank < k_sel, blk_id < n_sel_blocks)
        pen_t = jnp.where(keep, 0.0, -1.0).astype(BF16)
        pen = lax.dot_general(pen_t, place_ref[...], (((0,), (0,)), ((), ())), preferred_element_type=F32)
        sel_ref[0, :, g * LANES:(g + 1) * LANES] = pen.astype(sel_ref.dtype)


def cmp_attention(q, kcd, vcd, ov, n_sel_blocks, k_sel):
    b, s, _ = q.shape
    ncr = kcd.shape[1]
    place = np.zeros((HEAD_DIM, LANES), np.float32)
    place[np.arange(HEAD_DIM), HEAD_DIM + np.arange(HEAD_DIM)] = 1.0
    return pl.pallas_call(
        functools.partial(_cmp_attn_kernel, n_sel_blocks=n_sel_blocks, k_sel=k_sel),
        grid=(b, s // QBLK),
        in_specs=[pl.BlockSpec((1, QBLK, HQ), lambda bb, i: (bb, i, 0)),
                  pl.BlockSpec((1, ncr, 2 * LANES), lambda bb, i: (bb, 0, 0)),
                  pl.BlockSpec((1, ncr, 2 * LANES), lambda bb, i: (bb, 0, 0)),
                  pl.BlockSpec((HEAD_DIM, ncr), lambda bb, i: (0, 0)),
                  pl.BlockSpec((HEAD_DIM, LANES), lambda bb, i: (0, 0))],
        out_specs=[pl.BlockSpec((1, QBLK, HQ), lambda bb, i: (bb, i, 0)),
                   pl.BlockSpec((1, QBLK, 2 * LANES), lambda bb, i: (bb, i, 0))],
        out_shape=[jax.ShapeDtypeStruct((b, s, HQ), F32),
                   jax.ShapeDtypeStruct((b, s, 2 * LANES), BF16)],
        compiler_params=_cparams("parallel", "parallel"),
        name="nsa_cmp_attention",
    )(q, kcd, vcd, ov, jnp.asarray(place, BF16))


def _sel_attn_kernel(q_ref, pen_ref, k_ref, v_ref, bias_ref, o_ref, qst_ref, m_ref, acc_ref,
                     sa_ref, sb_ref, *, nbt):
    qi0 = pl.program_id(2) * SEL_QTILES
    hpc = N_HEADS // C_KV_HEADS
    nsub = SEL_CHUNK // QBLK
    rpt = hpc * QBLK
    lane = lax.broadcasted_iota(jnp.int32, (QBLK, LANES), 1)
    lo = lane < HEAD_DIM
    for w in range(SEL_QTILES):
        pen = pen_ref[0, w * QBLK:(w + 1) * QBLK, :].astype(F32)
        for t in range(hpc):
            q2 = q_ref[0, w * QBLK:(w + 1) * QBLK, (t // 2) * LANES:(t // 2 + 1) * LANES].astype(F32)
            if t % 2:
                q2 = pltpu.roll(q2, HEAD_DIM, 1)
            qst_ref[w * rpt + t * QBLK:w * rpt + (t + 1) * QBLK, :] = jnp.where(lo, q2, pen).astype(BF16)
    m_ref[...] = jnp.full(m_ref.shape, NEG_INF, F32)
    acc_ref[...] = jnp.zeros(acc_ref.shape, F32)

    n_chunks = (qi0 + SEL_QTILES - 1) // nsub + 1

    def scores(c, s_ref):
        k0 = pl.multiple_of(c * SEL_CHUNK, SEL_CHUNK)
        s_ref[...] = _dot_nt(qst_ref[...], k_ref[0, pl.ds(k0, SEL_CHUNK), :])

    def accumulate(c, s_ref):
        k0 = pl.multiple_of(c * SEL_CHUNK, SEL_CHUNK)
        ps, alphas = [], []
        for w in range(SEL_QTILES):
            rows = slice(w * rpt, (w + 1) * rpt)
            tiles = []
            for u in range(nsub):
                off = qi0 + w - nsub * c - u
                idx = jnp.where(off < 0, nbt, jnp.minimum(off, nbt - 1))
                tiles.append(s_ref[rows, u * QBLK:(u + 1) * QBLK] + bias_ref[idx].reshape(rpt, QBLK))
            m_old = m_ref[rows, :]
            m_new = jnp.maximum(m_old, jnp.max(functools.reduce(jnp.maximum, tiles), axis=-1, keepdims=True))
            ps.append(jnp.concatenate([jnp.exp(tl - m_new).astype(BF16) for tl in tiles], axis=1))
            alphas.append(jnp.exp(m_old - m_new))
            m_ref[rows, :] = m_new
        pv = jnp.dot(jnp.concatenate(ps, axis=0), v_ref[0, pl.ds(k0, SEL_CHUNK), :], preferred_element_type=F32)
        acc_ref[...] = jnp.concatenate(alphas, axis=0) * acc_ref[...] + pv

    scores(0, sa_ref)

    def body(cc, carry):
        c = 2 * cc
        scores(c + 1, sb_ref)
        accumulate(c, sa_ref)
        scores(jnp.minimum(c + 2, n_chunks - 1), sa_ref)
        accumulate(c + 1, sb_ref)
        return carry

    lax.fori_loop(0, n_chunks // 2, body, 0)

    @pl.when(n_chunks % 2 == 1)
    def _():
        accumulate(n_chunks - 1, sa_ref)

    acc = acc_ref[...]
    rolled = pltpu.roll(acc, HEAD_DIM, 1)
    for w in range(SEL_QTILES):
        for t2 in range(hpc // 2):
            ev = slice(w * rpt + 2 * t2 * QBLK, w * rpt + (2 * t2 + 1) * QBLK)
            od = slice(w * rpt + (2 * t2 + 1) * QBLK, w * rpt + (2 * t2 + 2) * QBLK)
            even = acc[ev] * (1.0 / rolled[ev])
            odd = rolled[od] * (1.0 / acc[od])
            o_ref[0, w * QBLK:(w + 1) * QBLK, t2 * LANES:(t2 + 1) * LANES] = jnp.where(lo, even, odd)


def sel_attention(q, kx, vx, pen, bias_tiles):
    b, s, _ = q.shape
    hpc = N_HEADS // C_KV_HEADS
    gw = hpc * HEAD_DIM
    nbt = bias_tiles.shape[0] - 1
    tq = SEL_QTILES * QBLK
    rows = SEL_QTILES * hpc * QBLK
    assert s % SEL_CHUNK == 0 and s % tq == 0
    return pl.pallas_call(
        functools.partial(_sel_attn_kernel, nbt=nbt),
        grid=(b, C_KV_HEADS, s // tq),
        in_specs=[pl.BlockSpec((1, tq, gw), lambda bb, g, i: (bb, i, g)),
                  pl.BlockSpec((1, tq, LANES), lambda bb, g, i: (bb, i, g)),
                  pl.BlockSpec((1, s, LANES), lambda bb, g, i: (bb, 0, g)),
                  pl.BlockSpec((1, s, LANES), lambda bb, g, i: (bb, 0, g)),
                  pl.BlockSpec((nbt + 1, hpc, QBLK, QBLK), lambda bb, g, i: (0, g, 0, 0))],
        out_specs=pl.BlockSpec((1, tq, gw), lambda bb, g, i: (bb, i, g)),
        out_shape=jax.ShapeDtypeStruct((b, s, HQ), F32),
        scratch_shapes=[pltpu.VMEM((rows, LANES), BF16),
                        pltpu.VMEM((rows, LANES), F32),
                        pltpu.VMEM((rows, LANES), F32),
                        pltpu.VMEM((rows, SEL_CHUNK), F32),
                        pltpu.VMEM((rows, SEL_CHUNK), F32)],
        compiler_params=_cparams("parallel", "parallel", "arbitrary"),
        name="nsa_sel_attention",
    )(q, pen, kx, vx, bias_tiles)


def _sel_bias_idx(s):
    nqt = s // QBLK
    far = -(-(int(np.argmax(_t5_bucket_np(np.arange(4 * MAX_DISTANCE)) == NUM_BUCKETS - 1)) + QBLK) // QBLK)
    nbt = min(nqt, far + 1)
    d0 = np.arange(nbt)[:, None, None] * QBLK
    dist = d0 + np.arange(QBLK)[None, :, None] - np.arange(QBLK)[None, None, :]
    idx = np.where(dist >= 0, _t5_bucket_np(dist), -1).astype(np.int32)
    return np.concatenate([idx, np.full((1, QBLK, QBLK), -1, np.int32)])


def _ffn_kernel(h_ref, g2_ref, wu_ref, cw_ref, cb_ref, wd_ref, g3_ref,
                o_ref, xn_ref, acc_ref, u_ref, a_ref, carry_ref, *, tiles_per_seq, nj):
    i = pl.program_id(0)
    ns, ts, tn = xn_ref.shape[0], xn_ref.shape[1], a_ref.shape[3]
    cols = lambda c: slice(c * tn, (c + 1) * tn)

    @pl.when(i % tiles_per_seq == 0)
    def _():
        carry_ref[...] = jnp.zeros(carry_ref.shape, F32)

    top = 16
    rowt = lax.broadcasted_iota(jnp.int32, (top, tn), 0)

    def up(s, j):
        xn = xn_ref[s]
        u_ref[s, j % 2, 0] = jnp.dot(xn, wu_ref[:, cols(j)], preferred_element_type=F32)
        u_ref[s, j % 2, 1] = jnp.dot(xn, wu_ref[:, cols(nj + j)], preferred_element_type=F32)

    def conv(u, s1, s2, c):
        cw = cw_ref[:, cols(c)]
        return ((cb_ref[:, cols(c)] + u * cw[2:3]) + s2 * cw[0:1]) + s1 * cw[1:2]

    def conv_body(u, c):
        return conv(u, pltpu.roll(u, 1, 0), pltpu.roll(u, 2, 0), c)

    def conv_top(uc_ref, c):
        prev = carry_ref[c]
        p1, p2 = prev[7:8], prev[6:7]
        u = uc_ref[0:top]
        s1 = jnp.where(rowt == 0, p1, pltpu.roll(u, 1, 0))
        s2 = jnp.where(rowt == 0, p2, jnp.where(rowt == 1, p1, pltpu.roll(u, 2, 0)))
        carry_ref[c] = uc_ref[ts - 8:ts]
        return conv(u, s1, s2, c)

    def gated(cg, cv):
        return (jax.nn.gelu(cg, approximate=True) * cv).astype(BF16)

    def act(s, j):
        ur, ar = u_ref.at[s, j % 2], a_ref.at[s, j % 2]
        ar[...] = gated(conv_body(ur[0], j), conv_body(ur[1], nj + j))
        ar[0:top] = gated(conv_top(ur.at[0], j), conv_top(ur.at[1], nj + j))

    def down(s, j):
        acc_ref[s] += jnp.dot(a_ref[s, j % 2], wd_ref[j * tn:(j + 1) * tn, :], preferred_element_type=F32)

    def iteration(s, k):
        rows = slice(s * ts, (s + 1) * ts)
        if k == -1:
            xn_ref[s] = _rms(h_ref[rows, :], g2_ref[...]).astype(xn_ref.dtype)
            acc_ref[s] = jnp.zeros(acc_ref.shape[1:], F32)
        if 0 <= k + 1 < nj:
            up(s, k + 1)
        if 0 <= k < nj:
            act(s, k)
        if 0 <= k - 1 < nj:
            down(s, k - 1)
        if k == nj:
            o_ref[rows, :] = h_ref[rows, :] + _rms(acc_ref[s], g3_ref[...])

    for k in range(-1, nj + 1):
        for s in range(ns):
            iteration(s, k)


def conv_ffn(h, g2, w_up, conv_w, conv_b, w_down, g3, seq, tm, tn, ns=FFN_SUBTILES):
    t, d = h.shape
    dff = w_down.shape[0]
    nj = dff // tn
    ts = tm // ns
    assert nj * tn == dff and ts * ns == tm and ts % 16 == 0
    wu, cw, cb, wd = w_up, conv_w, conv_b.reshape(1, -1), w_down
    resident = lambda a: pl.BlockSpec(a.shape, (lambda nd: (lambda i: (0,) * nd))(a.ndim))
    return pl.pallas_call(
        functools.partial(_ffn_kernel, tiles_per_seq=seq // tm, nj=nj),
        grid=(t // tm,),
        in_specs=[pl.BlockSpec((tm, d), lambda i: (i, 0)),
                  pl.BlockSpec((1, d), lambda i: (0, 0)),
                  resident(wu), resident(cw), resident(cb), resident(wd),
                  pl.BlockSpec((1, d), lambda i: (0, 0))],
        out_specs=pl.BlockSpec((tm, d), lambda i: (i, 0)),
        out_shape=jax.ShapeDtypeStruct((t, d), F32),
        scratch_shapes=[pltpu.VMEM((ns, ts, d), BF16), pltpu.VMEM((ns, ts, d), F32),
                        pltpu.VMEM((ns, 2, 2, ts, tn), F32), pltpu.VMEM((ns, 2, ts, tn), BF16),
                        pltpu.VMEM((2 * nj, 8, tn), F32)],
        compiler_params=_cparams("arbitrary"),
        name="conv_ffn",
    )(h, g2.reshape(1, d), wu, cw, cb, wd, g3.reshape(1, d))


def _row_tile(t):
    for tm in (1024, 512, 256, 128):
        if t % tm == 0:
            return tm
    raise ValueError(f"token count {t} is not a multiple of 128")


def mixer_a(h, gains, w_in, w_o, rel_table, bsz, seq):
    t, d = h.shape
    tm = _row_tile(t)
    n_dil = len(DIL_CONFIGS)
    a_in = w_in.shape[1]
    col_scale = np.ones((a_in,), np.float32).reshape(n_dil, 3, HQ)
    col_scale[:, 0] = ATTN_SCALE
    w = (w_in * col_scale.reshape(1, a_in)).astype(BF16)
    idx = np.concatenate([_banded_idx(1, window // dil, dil) for window, dil in DIL_CONFIGS])
    bias = bias_expand(rel_table, idx)
    outs, lses = [], []
    dils = [dil for _, dil in DIL_CONFIGS]
    chunks = [_band_chunk(dil, seq) for dil in dils]
    xns = norm_permute(h, gains[0], dils, chunks, max([tm] + [cl * dil for dil, cl in zip(dils, chunks)]))
    for gi, (window, dil) in enumerate(DIL_CONFIGS):
        cl = chunks[gi]
        pair = BAND_ROWS // cl if dil % (BAND_ROWS // cl) == 0 else 1
        tq, rows = cl * pair, dil // pair
        assert window // dil <= QBLK and seq % (cl * dil) == 0
        qkv = matmul_resident(xns[gi], w[:, gi * 3 * HQ:(gi + 1) * 3 * HQ], BF16, tm)
        qkv = qkv.reshape(1, t, 3 * HQ)

        def rmap(part, rows=rows, per_seq=seq // tq):
            return lambda n, i: (0, (n // rows) * per_seq + i * rows + n % rows, part)

        o, lse = banded_attention(
            qkv, qkv, qkv, bias[2 * gi:2 * gi + 2], n_rows=bsz * rows, n_tiles=seq // (cl * dil), tq=tq,
            n_prev=1, kw=HQ, q_map=rmap(0), k_map=rmap(1), v_map=rmap(2), o_map=rmap(0), paired=pair > 1,
            out_shape=(1, t, HQ), lse_shape=(1, t, LANES), name=f"dilated_attention_{dil}")
        outs.append(o.reshape(t, HQ))
        lses.append(lse.reshape(t, LANES))
    e = jnp.asarray(_head_expand_matrix(), BF16)
    return combine_proj_a(outs, lses, dils, chunks, e, w_o.astype(BF16), gains[1], h, min(tm, PROJ_ROWS))


def mixer_b(h, gains, w_in, sinks, w_o, rel_table, bsz, seq):
    t, d = h.shape
    tm = _row_tile(t)
    hk = B_KV_HEADS * HEAD_DIM
    n_in = w_in.shape[1]
    assert n_in == HQ + 2 * hk
    w = jnp.concatenate([w_in[:, :HQ] * ATTN_SCALE, _dup_groups(w_in[:, HQ:HQ + hk]),
                         _dup_groups(w_in[:, HQ + hk:])], axis=1).astype(BF16)
    kw = 2 * hk
    qkv = norm_matmul(h, gains[0], w, BF16, tm, HQ + 2 * kw).reshape(bsz, seq, HQ + 2 * kw)
    bias = bias_expand(rel_table, _banded_idx(1, B_WINDOW - 1, 1))
    sink_rows = sinks.astype(F32)
    ident = lambda n, i: (n, i, 0)
    col = lambda c: (lambda n, i: (n, i, c))
    (o,) = banded_attention(qkv, qkv, qkv, bias, n_rows=bsz, n_tiles=seq // BAND_ROWS, tq=BAND_ROWS, n_prev=1,
                            kw=kw, q_map=ident, k_map=col(HQ // kw), v_map=col(HQ // kw + 1), o_map=ident,
                            out_shape=(bsz, seq, HQ), sinks=sink_rows, stack=8,
                            proj=(w_o.astype(BF16), gains[1], h.reshape(bsz, seq, d)), name="sink_window_attention")
    return o.reshape(t, d)


def mixer_c(h, gains, w_in, cmp_pos, cmp_w1, cmp_w2, w_o, rel_table, bsz, seq):
    t, d = h.shape
    tm = _row_tile(t)
    g = C_KV_HEADS
    hk = g * HEAD_DIM
    wkv = [w_in[:, HQ + i * hk:HQ + (i + 1) * hk] for i in range(6)]
    n_gate = w_in.shape[1] - HQ - 6 * hk
    assert n_gate == 3 * N_HEADS and hk == LANES
    wq = (w_in[:, :HQ] * ATTN_SCALE).astype(BF16)
    spread = lambda w: jnp.concatenate([w[:, :HEAD_DIM], jnp.zeros_like(w[:, :HEAD_DIM]),
                                        w[:, HEAD_DIM:], jnp.zeros_like(w[:, :HEAD_DIM])], axis=1)
    wb = jnp.concatenate([spread(wkv[2]), spread(wkv[3]), _dup_groups(wkv[4]), _dup_groups(wkv[5])],
                         axis=1).astype(BF16)
    wf = jnp.concatenate([wkv[0], wkv[1], jnp.pad(w_in[:, HQ + 6 * hk:], ((0, 0), (0, LANES - n_gate)))],
                         axis=1).astype(BF16)
    kw = 2 * LANES
    q = norm_matmul(h, gains[0], jnp.concatenate([wq, wb], axis=1), BF16, tm, HQ + 4 * kw)
    q = q.reshape(bsz, seq, HQ + 4 * kw)
    kvb = q[..., HQ:HQ + 2 * kw]
    r = norm_matmul(h, gains[0], wf, F32, tm, 3 * LANES)
    kv = [r[:, i * hk:(i + 1) * hk].reshape(bsz, seq, hk) for i in range(2)]
    gates_raw = r[:, 2 * hk:2 * hk + LANES]

    ncr = seq // CMP_STRIDE
    half = CMP_BLOCK // 2
    assert half == CMP_STRIDE
    chunks = jnp.stack([kv[0], kv[1]]).reshape(2, bsz, ncr, half, g, HEAD_DIM)
    chunks = chunks.transpose(0, 1, 4, 2, 3, 5).reshape(2, bsz, g, ncr, half * HEAD_DIM)
    pos = cmp_pos.reshape(2, 2, half * HEAD_DIM)
    w1 = cmp_w1.reshape(2, 2, half * HEAD_DIM, -1).astype(BF16)
    cmp = compress(chunks, pos, w1, cmp_w2.astype(BF16))
    cmp = cmp.transpose(0, 1, 3, 2, 4).reshape(2, bsz, ncr, hk).astype(BF16)
    kcd, vcd = _dup_groups(cmp[0]), _dup_groups(cmp[1])

    ns = seq // SEL_BLOCK
    assert ns <= HEAD_DIM
    k_sel = min(N_SELECT, ns)
    starts = np.arange(ncr) * CMP_STRIDE
    blk = np.arange(HEAD_DIM)
    ov = ((starts[None, :] < (blk[:, None] + 1) * SEL_BLOCK)
          & (starts[None, :] + CMP_BLOCK > blk[:, None] * SEL_BLOCK)
          & (blk[:, None] < ns) & (starts[None, :] + CMP_BLOCK <= seq))
    o_c, pen = cmp_attention(q, kcd, vcd, jnp.asarray(ov.astype(np.float32), BF16), ns, k_sel)

    sel_bias = bias_expand(rel_table, _sel_bias_idx(seq))
    lane = np.arange(kw)[None, :]
    key_blk = np.arange(seq)[:, None] // SEL_BLOCK
    onehot = np.where((lane % LANES >= HEAD_DIM) & (lane % HEAD_DIM == key_blk), -NEG_INF, 0.0)
    ones = np.broadcast_to(np.where(lane % LANES >= HEAD_DIM, 1.0, 0.0), (seq, kw))
    kx = kvb[..., :kw] + jnp.asarray(onehot, BF16)[None]
    vx = kvb[..., kw:2 * kw] + jnp.asarray(ones, BF16)[None]
    o_s = sel_attention(q, kx, vx, pen, sel_bias)

    n_prev = -(-(C_WINDOW - 1) // QBLK)
    tqw = n_prev * QBLK
    wbias = bias_expand(rel_table, _banded_idx(n_prev, C_WINDOW - 1, 1))
    ident = lambda n, i: (n, i, 0)
    col = lambda c: (lambda n, i: (n, i, c))
    (o_w,) = banded_attention(q, q, q, wbias,
                              n_rows=bsz, n_tiles=seq // tqw, tq=tqw, n_prev=n_prev, kw=kw,
                              q_map=ident, k_map=col(HQ // kw + 2), v_map=col(HQ // kw + 3), o_map=ident,
                              out_shape=(bsz, seq, HQ), stack=8, name="nsa_window_attention")

    e = jnp.asarray(_head_expand_matrix(), BF16)
    return _proj_call(_proj_c_kernel, "nsa_gate_proj", min(tm, PROJ_ROWS),
                      [o_c.reshape(t, HQ), o_s.reshape(t, HQ), o_w.reshape(t, HQ), gates_raw], [e],
                      w_o.astype(BF16), gains[1], h)


def kernel(x, rel_table, norm_gains, a_w_in, a_w_o, b_w_in, b_sinks, b_w_o, c_w_in, c_cmp_pos, c_cmp_w1,
           c_cmp_w2, c_w_o, ffn_w_up, ffn_conv_w, ffn_conv_b, ffn_w_down):
    bsz, seq, d = x.shape
    depth = norm_gains.shape[0]
    h = x.reshape(bsz * seq, d)
    tm = _row_tile(seq)
    for i in range(depth):
        kind, j = i % 3, i // 3
        g = norm_gains[i]
        if kind == 0:
            h = mixer_a(h, g, a_w_in[j], a_w_o[j], rel_table, bsz, seq)
        elif kind == 1:
            h = mixer_b(h, g, b_w_in[j], b_sinks[j], b_w_o[j], rel_table, bsz, seq)
        else:
            h = mixer_c(h, g, c_w_in[j], c_cmp_pos[j], c_cmp_w1[j], c_cmp_w2[j], c_w_o[j], rel_table, bsz, seq)
        h = conv_ffn(h, g[2], ffn_w_up[i].astype(BF16), ffn_conv_w[i], ffn_conv_b[i],
                     ffn_w_down[i].astype(BF16), g[3], seq, min(tm, FFN_ROWS), FFN_COLS)
    return h.reshape(bsz, seq, d)
```
